```python
import math
import jax, jax.numpy as jnp
from jax import lax
import numpy as np

D_MODEL = 1024
BATCH = 8
SEQ = 4096
DEPTH = 4

ROPE_THETA = 10000.0
QBLK = 128
MLA_HEADS = 8
MLA_NOPE = 64
MLA_ROPE = 32
MLA_V = 64
MLA_Q_RANK = 256
MLA_KV_RANK = 128
FOX_HEADS = 8
FOX_DIM = 64
SWA_HEADS = 16
SWA_KV_HEADS = 2
SWA_DIM = 64
WINDOW = 128

RMS_EPS = 1e-6
LN_EPS = 1e-5
ALPHA = (2 * DEPTH) ** 0.25
BETA = (8 * DEPTH) ** -0.25

EVEN_WIDTH = MLA_HEADS * MLA_V + FOX_HEADS * FOX_DIM
ODD_WIDTH = SWA_HEADS * SWA_DIM
EVEN_SIZES = (MLA_Q_RANK, MLA_KV_RANK, MLA_ROPE, FOX_HEADS * FOX_DIM,
              FOX_HEADS * FOX_DIM, FOX_HEADS * FOX_DIM, FOX_HEADS, EVEN_WIDTH)
EVEN_IN = sum(EVEN_SIZES)
EVEN_V_START = MLA_Q_RANK + MLA_KV_RANK + MLA_ROPE + 2 * FOX_HEADS * FOX_DIM
ODD_SIZES = (SWA_HEADS * SWA_DIM, SWA_KV_HEADS * SWA_DIM, SWA_KV_HEADS * SWA_DIM, ODD_WIDTH)
ODD_IN = sum(ODD_SIZES)
ODD_V_START = SWA_HEADS * SWA_DIM + SWA_KV_HEADS * SWA_DIM
N_EVEN = (DEPTH + 1) // 2
N_ODD = DEPTH // 2

kernel_name = "hybrid_mla_fox_swa_deepnorm"


def _split(h, sizes):
    cuts = [int(c) for c in np.cumsum(sizes)[:-1]]
    return jnp.split(h, cuts, axis=-1)


def _heads(t, n_heads):
    b, s, _ = t.shape
    return t.reshape(b, s, n_heads, -1).transpose(0, 2, 1, 3)


def _merge(t):
    b, h, s, d = t.shape
    return t.transpose(0, 2, 1, 3).reshape(b, s, h * d)


def rms_norm(t, g):
    tf = t.astype(jnp.float32)
    tf = tf * lax.rsqrt(jnp.mean(tf * tf, axis=-1, keepdims=True) + RMS_EPS)
    return (tf * g.astype(jnp.float32)).astype(t.dtype)


def layer_norm(t, g, b):
    tf = t.astype(jnp.float32)
    mu = jnp.mean(tf, axis=-1, keepdims=True)
    var = jnp.mean(jnp.square(tf - mu), axis=-1, keepdims=True)
    y = (tf - mu) * lax.rsqrt(var + LN_EPS) * g.astype(jnp.float32) + b.astype(jnp.float32)
    return y.astype(t.dtype)


def rope(t, pos):
    d = t.shape[-1]
    inv = ROPE_THETA ** (-jnp.arange(0, d, 2, dtype=jnp.float32) / d)
    ang = pos.astype(jnp.float32)[:, None] * inv[None, :]
    cos, sin = jnp.cos(ang), jnp.sin(ang)
    t1, t2 = jnp.split(t.astype(jnp.float32), 2, axis=-1)
    return jnp.concatenate([t1 * cos - t2 * sin, t2 * cos + t1 * sin], axis=-1).astype(t.dtype)


def causal_block_attention(q, k, v, scale, cum_logf=None):
    b, h, s, dk = q.shape
    dv = v.shape[-1]
    nb = s // QBLK
    qb = q.reshape(b, h, nb, QBLK, dk).transpose(2, 0, 1, 3, 4)
    kpos = jnp.arange(s)
    idx = jnp.arange(nb)

    def block(args):
        if cum_logf is None:
            q_i, i = args
        else:
            q_i, c_i, i = args
        sc = jnp.einsum('bhqd,bhkd->bhqk', q_i, k,
                        preferred_element_type=jnp.float32) * scale
        if cum_logf is not None:
            sc = sc + c_i[..., :, None] - cum_logf[..., None, :]
        qpos = i * QBLK + jnp.arange(QBLK)
        mask = kpos[None, :] <= qpos[:, None]
        sc = jnp.where(mask, sc, -jnp.inf)
        p = jax.nn.softmax(sc, axis=-1)
        return jnp.einsum('bhqk,bhkd->bhqd', p.astype(v.dtype), v)

    if cum_logf is None:
        xs = (qb, idx)
    else:
        cb = cum_logf.reshape(b, h, nb, QBLK).transpose(2, 0, 1, 3)
        xs = (qb, cb, idx)
    out = lax.map(block, xs)
    return out.transpose(1, 2, 0, 3, 4).reshape(b, h, s, dv)


def sliding_window_sink_attention(q, k, v, sinks):
    b, h, s, d = q.shape
    hkv = k.shape[1]
    g = h // hkv
    nb = s // QBLK
    scale = d ** -0.5
    qb = q.reshape(b, hkv, g, nb, QBLK, d).transpose(3, 0, 1, 2, 4, 5)
    pad = ((0, 0), (0, 0), (QBLK, 0), (0, 0))
    kp = jnp.pad(k, pad)
    vp = jnp.pad(v, pad)
    sink = sinks.astype(jnp.float32).reshape(1, hkv, g, 1, 1)

    def block(args):
        q_i, i = args
        start = i * QBLK
        k_i = lax.dynamic_slice_in_dim(kp, start, 2 * QBLK, axis=2)
        v_i = lax.dynamic_slice_in_dim(vp, start, 2 * QBLK, axis=2)
        sc = jnp.einsum('bkgqd,bkjd->bkgqj', q_i, k_i,
                        preferred_element_type=jnp.float32) * scale
        qpos = start + jnp.arange(QBLK)
        kpos = start - QBLK + jnp.arange(2 * QBLK)
        diff = qpos[:, None] - kpos[None, :]
        mask = (diff >= 0) & (diff < WINDOW) & (kpos[None, :] >= 0)
        sc = jnp.where(mask, sc, -jnp.inf)
        logits = jnp.concatenate(
            [sc, jnp.broadcast_to(sink, sc.shape[:-1] + (1,))], axis=-1)
        p = jax.nn.softmax(logits, axis=-1)[..., :-1]
        return jnp.einsum('bkgqj,bkjd->bkgqd', p.astype(v.dtype), v_i)

    out = lax.map(block, (qb, jnp.arange(nb)))
    return out.transpose(1, 2, 3, 0, 4, 5).reshape(b, h, s, d)


def even_mixer(x, w_in, q_norm, w_uq, kv_norm, w_ukv, b_f, w_out, pos):
    b, s, _ = x.shape
    h = x @ w_in
    cq, ckv, k_pe, fq, fk, fv, f_logit, gate = _split(h, EVEN_SIZES)
    q = (rms_norm(cq, q_norm) @ w_uq).reshape(b, s, MLA_HEADS, MLA_NOPE + MLA_ROPE)
    q = q.transpose(0, 2, 1, 3)
    q_nope, q_pe = q[..., :MLA_NOPE], rope(q[..., MLA_NOPE:], pos)
    kv = (rms_norm(ckv, kv_norm) @ w_ukv).reshape(b, s, MLA_HEADS, MLA_NOPE + MLA_V)
    kv = kv.transpose(0, 2, 1, 3)
    k_nope, v_mla = kv[..., :MLA_NOPE], kv[..., MLA_NOPE:]
    k_pe = rope(k_pe[:, None], pos)
    q_mla = jnp.concatenate([q_nope, q_pe], axis=-1)
    k_mla = jnp.concatenate(
        [k_nope, jnp.broadcast_to(k_pe, (b, MLA_HEADS, s, MLA_ROPE))], axis=-1)
    o_mla = causal_block_attention(q_mla, k_mla, v_mla, (MLA_NOPE + MLA_ROPE) ** -0.5)
    log_f = jax.nn.log_sigmoid((f_logit + b_f).astype(jnp.float32))
    cum = lax.cumsum(log_f, axis=1).transpose(0, 2, 1)
    o_fox = causal_block_attention(_heads(fq, FOX_HEADS), _heads(fk, FOX_HEADS),
                                   _heads(fv, FOX_HEADS), FOX_DIM ** -0.5, cum)
    o = jnp.concatenate([_merge(o_mla), _merge(o_fox)], axis=-1)
    return (o * jax.nn.silu(gate)) @ w_out


def odd_mixer(x, w_in, sinks, w_out, pos):
    h = x @ w_in
    q, k, v, gate = _split(h, ODD_SIZES)
    q = rope(_heads(q, SWA_HEADS), pos)
    k = rope(_heads(k, SWA_KV_HEADS), pos)
    v = _heads(v, SWA_KV_HEADS)
    o = _merge(sliding_window_sink_attention(q, k, v, sinks))
    return (o * jax.nn.silu(gate)) @ w_out


def _fwd_setup_inputs(seed: int = 0) -> dict:
    key = jax.random.key(seed)
    ks = jax.random.split(key, 16)
    nrm = jax.random.normal
    even_in_scale = jnp.ones((EVEN_IN,), jnp.float32).at[
        EVEN_V_START:EVEN_V_START + FOX_HEADS * FOX_DIM].set(BETA)
    ukv_scale = jnp.tile(jnp.concatenate([jnp.ones((MLA_NOPE,), jnp.float32),
                                          jnp.full((MLA_V,), BETA, jnp.float32)]), MLA_HEADS)
    odd_in_scale = jnp.ones((ODD_IN,), jnp.float32).at[
        ODD_V_START:ODD_V_START + SWA_KV_HEADS * SWA_DIM].set(BETA)
    return {
        "x": nrm(ks[0], (BATCH, SEQ, D_MODEL), jnp.float32),
        "even_w_in": nrm(ks[1], (N_EVEN, D_MODEL, EVEN_IN), jnp.float32) * D_MODEL ** -0.5 * even_in_scale,
        "even_q_norm": 1.0 + 0.02 * nrm(ks[2], (N_EVEN, MLA_Q_RANK), jnp.float32),
        "even_w_uq": nrm(ks[3], (N_EVEN, MLA_Q_RANK, MLA_HEADS * (MLA_NOPE + MLA_ROPE)), jnp.float32) * MLA_Q_RANK ** -0.5,
        "even_kv_norm": 1.0 + 0.02 * nrm(ks[4], (N_EVEN, MLA_KV_RANK), jnp.float32),
        "even_w_ukv": nrm(ks[5], (N_EVEN, MLA_KV_RANK, MLA_HEADS * (MLA_NOPE + MLA_V)), jnp.float32) * MLA_KV_RANK ** -0.5 * ukv_scale,
        "even_b_f": jax.random.uniform(ks[6], (N_EVEN, FOX_HEADS), jnp.float32, 1.0, 6.0),
        "even_w_out": nrm(ks[7], (N_EVEN, EVEN_WIDTH, D_MODEL), jnp.float32) * EVEN_WIDTH ** -0.5 * BETA,
        "even_ln_g": 1.0 + 0.02 * nrm(ks[8], (N_EVEN, D_MODEL), jnp.float32),
        "even_ln_b": 0.02 * nrm(ks[9], (N_EVEN, D_MODEL), jnp.float32),
        "odd_w_in": nrm(ks[10], (N_ODD, D_MODEL, ODD_IN), jnp.float32) * D_MODEL ** -0.5 * odd_in_scale,
        "odd_sinks": 0.5 * nrm(ks[11], (N_ODD, SWA_HEADS), jnp.float32),
        "odd_w_out": nrm(ks[12], (N_ODD, ODD_WIDTH, D_MODEL), jnp.float32) * ODD_WIDTH ** -0.5 * BETA,
        "odd_ln_g": 1.0 + 0.02 * nrm(ks[13], (N_ODD, D_MODEL), jnp.float32),
        "odd_ln_b": 0.02 * nrm(ks[14], (N_ODD, D_MODEL), jnp.float32),
    }


def _fwd_reference(x, even_w_in, even_q_norm, even_w_uq, even_kv_norm, even_w_ukv, even_b_f,
              even_w_out, even_ln_g, even_ln_b, odd_w_in, odd_sinks, odd_w_out,
              odd_ln_g, odd_ln_b):
    pos = jnp.arange(x.shape[1])
    for layer in range(DEPTH):
        j = layer // 2
        if layer % 2 == 0:
            y = even_mixer(x, even_w_in[j], even_q_norm[j], even_w_uq[j], even_kv_norm[j],
                           even_w_ukv[j], even_b_f[j], even_w_out[j], pos)
            x = layer_norm(ALPHA * x + y, even_ln_g[j], even_ln_b[j])
        else:
            y = odd_mixer(x, odd_w_in[j], odd_sinks[j], odd_w_out[j], pos)
            x = layer_norm(ALPHA * x + y, odd_ln_g[j], odd_ln_b[j])
    return x


import jax as _jax
import jax.numpy as _jnp

TWIN_FORMAT = 'train_step'
FWD_PARAMS = ['x', 'even_w_in', 'even_q_norm', 'even_w_uq', 'even_kv_norm', 'even_w_ukv', 'even_b_f', 'even_w_out', 'even_ln_g', 'even_ln_b', 'odd_w_in', 'odd_sinks', 'odd_w_out', 'odd_ln_g', 'odd_ln_b']
TWIN_WEIGHTS = ['even_w_in', 'even_q_norm', 'even_w_uq', 'even_kv_norm', 'even_w_ukv', 'even_b_f', 'even_w_out', 'even_ln_g', 'even_ln_b', 'odd_w_in', 'odd_sinks', 'odd_w_out', 'odd_ln_g', 'odd_ln_b']
TWIN_DIFF_INPUT = 'x'
TWIN_INPUTS = ['x', 'even_w_in', 'even_q_norm', 'even_w_uq', 'even_kv_norm', 'even_w_ukv', 'even_b_f', 'even_w_out', 'even_ln_g', 'even_ln_b', 'odd_w_in', 'odd_sinks', 'odd_w_out', 'odd_ln_g', 'odd_ln_b', 'loss_target', 'm_even_w_in', 'm_even_q_norm', 'm_even_w_uq', 'm_even_kv_norm', 'm_even_w_ukv', 'm_even_b_f', 'm_even_w_out', 'm_even_ln_g', 'm_even_ln_b', 'm_odd_w_in', 'm_odd_sinks', 'm_odd_w_out', 'm_odd_ln_g', 'm_odd_ln_b', 'v_even_w_in', 'v_even_q_norm', 'v_even_w_uq', 'v_even_kv_norm', 'v_even_w_ukv', 'v_even_b_f', 'v_even_w_out', 'v_even_ln_g', 'v_even_ln_b', 'v_odd_w_in', 'v_odd_sinks', 'v_odd_w_out', 'v_odd_ln_g', 'v_odd_ln_b']
TWIN_OUTPUTS = ['loss', 'grad_x', 'grad_even_w_in', 'grad_even_q_norm', 'grad_even_w_uq', 'grad_even_kv_norm', 'grad_even_w_ukv', 'grad_even_b_f', 'grad_even_w_out', 'grad_even_ln_g', 'grad_even_ln_b', 'grad_odd_w_in', 'grad_odd_sinks', 'grad_odd_w_out', 'grad_odd_ln_g', 'grad_odd_ln_b', 'delta_even_w_in', 'delta_even_q_norm', 'delta_even_w_uq', 'delta_even_kv_norm', 'delta_even_w_ukv', 'delta_even_b_f', 'delta_even_w_out', 'delta_even_ln_g', 'delta_even_ln_b', 'delta_odd_w_in', 'delta_odd_sinks', 'delta_odd_w_out', 'delta_odd_ln_g', 'delta_odd_ln_b', 'new_m_even_w_in', 'new_m_even_q_norm', 'new_m_even_w_uq', 'new_m_even_kv_norm', 'new_m_even_w_ukv', 'new_m_even_b_f', 'new_m_even_w_out', 'new_m_even_ln_g', 'new_m_even_ln_b', 'new_m_odd_w_in', 'new_m_odd_sinks', 'new_m_odd_w_out', 'new_m_odd_ln_g', 'new_m_odd_ln_b', 'new_v_even_w_in', 'new_v_even_q_norm', 'new_v_even_w_uq', 'new_v_even_kv_norm', 'new_v_even_w_ukv', 'new_v_even_b_f', 'new_v_even_w_out', 'new_v_even_ln_g', 'new_v_even_ln_b', 'new_v_odd_w_in', 'new_v_odd_sinks', 'new_v_odd_w_out', 'new_v_odd_ln_g', 'new_v_odd_ln_b']
TWIN_LEAF_KINDS = {'loss': 'loss', 'grad_x': 'grad_x', 'grad_even_w_in': 'grad_w', 'grad_even_q_norm': 'grad_w', 'grad_even_w_uq': 'grad_w', 'grad_even_kv_norm': 'grad_w', 'grad_even_w_ukv': 'grad_w', 'grad_even_b_f': 'grad_w', 'grad_even_w_out': 'grad_w', 'grad_even_ln_g': 'grad_w', 'grad_even_ln_b': 'grad_w', 'grad_odd_w_in': 'grad_w', 'grad_odd_sinks': 'grad_w', 'grad_odd_w_out': 'grad_w', 'grad_odd_ln_g': 'grad_w', 'grad_odd_ln_b': 'grad_w', 'delta_even_w_in': 'delta_w', 'delta_even_q_norm': 'delta_w', 'delta_even_w_uq': 'delta_w', 'delta_even_kv_norm': 'delta_w', 'delta_even_w_ukv': 'delta_w', 'delta_even_b_f': 'delta_w', 'delta_even_w_out': 'delta_w', 'delta_even_ln_g': 'delta_w', 'delta_even_ln_b': 'delta_w', 'delta_odd_w_in': 'delta_w', 'delta_odd_sinks': 'delta_w', 'delta_odd_w_out': 'delta_w', 'delta_odd_ln_g': 'delta_w', 'delta_odd_ln_b': 'delta_w', 'new_m_even_w_in': 'new_m', 'new_m_even_q_norm': 'new_m', 'new_m_even_w_uq': 'new_m', 'new_m_even_kv_norm': 'new_m', 'new_m_even_w_ukv': 'new_m', 'new_m_even_b_f': 'new_m', 'new_m_even_w_out': 'new_m', 'new_m_even_ln_g': 'new_m', 'new_m_even_ln_b': 'new_m', 'new_m_odd_w_in': 'new_m', 'new_m_odd_sinks': 'new_m', 'new_m_odd_w_out': 'new_m', 'new_m_odd_ln_g': 'new_m', 'new_m_odd_ln_b': 'new_m', 'new_v_even_w_in': 'new_v', 'new_v_even_q_norm': 'new_v', 'new_v_even_w_uq': 'new_v', 'new_v_even_kv_norm': 'new_v', 'new_v_even_w_ukv': 'new_v', 'new_v_even_b_f': 'new_v', 'new_v_even_w_out': 'new_v', 'new_v_even_ln_g': 'new_v', 'new_v_even_ln_b': 'new_v', 'new_v_odd_w_in': 'new_v', 'new_v_odd_sinks': 'new_v', 'new_v_odd_w_out': 'new_v', 'new_v_odd_ln_g': 'new_v', 'new_v_odd_ln_b': 'new_v'}


def _forward(args):
    return _fwd_reference(*[args[k] for k in FWD_PARAMS])


def _output_shape():
    def fwd():
        inp = _fwd_setup_inputs(0)
        return _fwd_reference(*[inp[k] for k in FWD_PARAMS])
    out = _jax.eval_shape(fwd)
    return out.shape, out.dtype

N_MICROBATCH = 1
ADAM_LR = 0.001
ADAM_B1 = 0.9
ADAM_B2 = 0.999
ADAM_EPS = 1e-08
ADAM_WD = 0.01
ADAM_STEP = 10
PER_EXAMPLE_BATCH_AXIS = {'x': 0, 'loss_target': 0}
SHARED_INPUTS = []
_WEIGHT_DTYPES = {'even_w_in': _jnp.float32, 'even_q_norm': _jnp.float32, 'even_w_uq': _jnp.float32, 'even_kv_norm': _jnp.float32, 'even_w_ukv': _jnp.float32, 'even_b_f': _jnp.float32, 'even_w_out': _jnp.float32, 'even_ln_g': _jnp.float32, 'even_ln_b': _jnp.float32, 'odd_w_in': _jnp.float32, 'odd_sinks': _jnp.float32, 'odd_w_out': _jnp.float32, 'odd_ln_g': _jnp.float32, 'odd_ln_b': _jnp.float32}
MOMENT_SCALE = {'even_w_in': 4.376536e-03, 'even_q_norm': 2.877972e-03, 'even_w_uq': 1.701980e-03, 'even_kv_norm': 6.605345e-03, 'even_w_ukv': 4.365892e-03, 'even_b_f': 3.718853e-02, 'even_w_out': 7.060874e-03, 'even_ln_g': 1.217493e+00, 'even_ln_b': 6.005375e-01, 'odd_w_in': 4.442344e-03, 'odd_sinks': 1.876294e-03, 'odd_w_out': 5.474852e-03, 'odd_ln_g': 2.270102e+01, 'odd_ln_b': 9.214027e-01}


def _to_microbatches(a, axis):
    t = _jnp.moveaxis(a, axis, 0)
    t = t.reshape((N_MICROBATCH, t.shape[0] // N_MICROBATCH) + t.shape[1:])
    return _jnp.moveaxis(t, 1, axis + 1)


def setup_inputs(seed: int = 0) -> dict:
    inp = _fwd_setup_inputs(seed)
    key = _jax.random.fold_in(_jax.random.key(seed), 7919)
    shape, _ = _output_shape()
    out = dict(inp)
    out["loss_target"] = _jax.random.normal(_jax.random.fold_in(key, 0), shape, _jnp.float32)
    for i, name in enumerate(TWIN_WEIGHTS):
        w = inp[name].astype(_jnp.float32)
        if MOMENT_SCALE is None:
            s = _jnp.sqrt(_jnp.mean(_jnp.square(w)) + 1e-30)
        else:
            s = MOMENT_SCALE[name]
        km, kv = _jax.random.split(_jax.random.fold_in(key, i + 1))
        out[name] = w
        out["m_" + name] = s * _jax.random.normal(km, w.shape, _jnp.float32)
        out["v_" + name] = (s * s) * _jax.random.uniform(kv, w.shape, _jnp.float32, 0.5, 1.5)
    if N_MICROBATCH > 1:
        for name, axis in PER_EXAMPLE_BATCH_AXIS.items():
            out[name] = _to_microbatches(out[name], axis)
    return {'x': out['x'], 'even_w_in': out['even_w_in'], 'even_q_norm': out['even_q_norm'], 'even_w_uq': out['even_w_uq'], 'even_kv_norm': out['even_kv_norm'], 'even_w_ukv': out['even_w_ukv'], 'even_b_f': out['even_b_f'], 'even_w_out': out['even_w_out'], 'even_ln_g': out['even_ln_g'], 'even_ln_b': out['even_ln_b'], 'odd_w_in': out['odd_w_in'], 'odd_sinks': out['odd_sinks'], 'odd_w_out': out['odd_w_out'], 'odd_ln_g': out['odd_ln_g'], 'odd_ln_b': out['odd_ln_b'], 'loss_target': out['loss_target'], 'm_even_w_in': out['m_even_w_in'], 'm_even_q_norm': out['m_even_q_norm'], 'm_even_w_uq': out['m_even_w_uq'], 'm_even_kv_norm': out['m_even_kv_norm'], 'm_even_w_ukv': out['m_even_w_ukv'], 'm_even_b_f': out['m_even_b_f'], 'm_even_w_out': out['m_even_w_out'], 'm_even_ln_g': out['m_even_ln_g'], 'm_even_ln_b': out['m_even_ln_b'], 'm_odd_w_in': out['m_odd_w_in'], 'm_odd_sinks': out['m_odd_sinks'], 'm_odd_w_out': out['m_odd_w_out'], 'm_odd_ln_g': out['m_odd_ln_g'], 'm_odd_ln_b': out['m_odd_ln_b'], 'v_even_w_in': out['v_even_w_in'], 'v_even_q_norm': out['v_even_q_norm'], 'v_even_w_uq': out['v_even_w_uq'], 'v_even_kv_norm': out['v_even_kv_norm'], 'v_even_w_ukv': out['v_even_w_ukv'], 'v_even_b_f': out['v_even_b_f'], 'v_even_w_out': out['v_even_w_out'], 'v_even_ln_g': out['v_even_ln_g'], 'v_even_ln_b': out['v_even_ln_b'], 'v_odd_w_in': out['v_odd_w_in'], 'v_odd_sinks': out['v_odd_sinks'], 'v_odd_w_out': out['v_odd_w_out'], 'v_odd_ln_g': out['v_odd_ln_g'], 'v_odd_ln_b': out['v_odd_ln_b']}


def _loss(weights, diff, rest, loss_target):
    with _jax.named_scope("forward"):
        args = {**rest, TWIN_DIFF_INPUT: diff, **{k: w.astype(_WEIGHT_DTYPES[k]) for k, w in weights.items()}}
        y = _forward(args)
    with _jax.named_scope("loss_head"):
        err = _jnp.square(y.astype(_jnp.float32) - loss_target)
        return 0.5 * _jnp.sum(_jnp.mean(err, axis=-1)) if err.ndim else 0.5 * err


def _adamw(w, g, m, v):
    m = ADAM_B1 * m + (1.0 - ADAM_B1) * g
    v = ADAM_B2 * v + (1.0 - ADAM_B2) * _jnp.square(g)
    m_hat = m / (1.0 - ADAM_B1 ** ADAM_STEP)
    v_hat = v / (1.0 - ADAM_B2 ** ADAM_STEP)
    delta = -ADAM_LR * (m_hat / (_jnp.sqrt(v_hat) + ADAM_EPS) + ADAM_WD * w)
    return delta, m, v


def reference(x, even_w_in, even_q_norm, even_w_uq, even_kv_norm, even_w_ukv, even_b_f, even_w_out, even_ln_g, even_ln_b, odd_w_in, odd_sinks, odd_w_out, odd_ln_g, odd_ln_b, loss_target, m_even_w_in, m_even_q_norm, m_even_w_uq, m_even_kv_norm, m_even_w_ukv, m_even_b_f, m_even_w_out, m_even_ln_g, m_even_ln_b, m_odd_w_in, m_odd_sinks, m_odd_w_out, m_odd_ln_g, m_odd_ln_b, v_even_w_in, v_even_q_norm, v_even_w_uq, v_even_kv_norm, v_even_w_ukv, v_even_b_f, v_even_w_out, v_even_ln_g, v_even_ln_b, v_odd_w_in, v_odd_sinks, v_odd_w_out, v_odd_ln_g, v_odd_ln_b):
    given = dict(x=x, even_w_in=even_w_in, even_q_norm=even_q_norm, even_w_uq=even_w_uq, even_kv_norm=even_kv_norm, even_w_ukv=even_w_ukv, even_b_f=even_b_f, even_w_out=even_w_out, even_ln_g=even_ln_g, even_ln_b=even_ln_b, odd_w_in=odd_w_in, odd_sinks=odd_sinks, odd_w_out=odd_w_out, odd_ln_g=odd_ln_g, odd_ln_b=odd_ln_b, loss_target=loss_target, m_even_w_in=m_even_w_in, m_even_q_norm=m_even_q_norm, m_even_w_uq=m_even_w_uq, m_even_kv_norm=m_even_kv_norm, m_even_w_ukv=m_even_w_ukv, m_even_b_f=m_even_b_f, m_even_w_out=m_even_w_out, m_even_ln_g=m_even_ln_g, m_even_ln_b=m_even_ln_b, m_odd_w_in=m_odd_w_in, m_odd_sinks=m_odd_sinks, m_odd_w_out=m_odd_w_out, m_odd_ln_g=m_odd_ln_g, m_odd_ln_b=m_odd_ln_b, v_even_w_in=v_even_w_in, v_even_q_norm=v_even_q_norm, v_even_w_uq=v_even_w_uq, v_even_kv_norm=v_even_kv_norm, v_even_w_ukv=v_even_w_ukv, v_even_b_f=v_even_b_f, v_even_w_out=v_even_w_out, v_even_ln_g=v_even_ln_g, v_even_ln_b=v_even_ln_b, v_odd_w_in=v_odd_w_in, v_odd_sinks=v_odd_sinks, v_odd_w_out=v_odd_w_out, v_odd_ln_g=v_odd_ln_g, v_odd_ln_b=v_odd_ln_b)
    weights = {n: given[n] for n in TWIN_WEIGHTS}
    shared = {n: given[n] for n in SHARED_INPUTS}
    per_example = {n: given[n] for n in ['x']}
    grad_fn = _jax.value_and_grad(_loss, argnums=(0, 1))

    def one_microbatch(ex, loss_target):
        ex = dict(ex)
        diff = ex.pop(TWIN_DIFF_INPUT)
        return grad_fn(weights, diff, {**shared, **ex}, loss_target)

    if N_MICROBATCH == 1:
        loss, (grad_w, grad_x) = one_microbatch(per_example, given["loss_target"])
    else:
        def body(carry, xs):
            loss_sum, grad_sum = carry
            l_k, (gw_k, gx_k) = one_microbatch(xs[0], xs[1])
            with _jax.named_scope("update"):
                return (loss_sum + l_k, _jax.tree.map(_jnp.add, grad_sum, gw_k)), gx_k

        init = (_jnp.zeros((), _jnp.float32), _jax.tree.map(_jnp.zeros_like, weights))
        (loss, grad_w), grad_x = _jax.lax.scan(body, init, (per_example, given["loss_target"]))
    with _jax.named_scope("update"):
        delta_w, new_m, new_v = {}, {}, {}
        for n in TWIN_WEIGHTS:
            delta_w[n], new_m[n], new_v[n] = _adamw(weights[n], grad_w[n], given["m_" + n], given["v_" + n])
    return (loss, grad_x, *[grad_w[n] for n in TWIN_WEIGHTS], *[delta_w[n] for n in TWIN_WEIGHTS],
            *[new_m[n] for n in TWIN_WEIGHTS], *[new_v[n] for n in TWIN_WEIGHTS])
```

```python
import functools
import math

import jax
import jax.numpy as jnp
from jax import lax
from jax.experimental import pallas as pl
from jax.experimental.pallas import tpu as pltpu

F32 = jnp.float32
BF16 = jnp.bfloat16
LANES = 128
HALF = 64
N_DEV = 8
AXES = ("x", "y", "c")
VMEM_LIMIT = 48 * 1024 * 1024

D_MODEL = 1024
DEPTH = 4
ROPE_THETA = 10000.0
MLA_HEADS, MLA_NOPE, MLA_ROPE, MLA_V, MLA_Q_RANK, MLA_KV_RANK = 8, 64, 32, 64, 256, 128
FOX_HEADS, FOX_DIM = 8, 64
SWA_HEADS, SWA_KV_HEADS, SWA_DIM, WINDOW = 16, 2, 64, 128
RMS_EPS, LN_EPS = 1e-6, 1e-5
ALPHA = (2 * DEPTH) ** 0.25
ADAM_LR, ADAM_B1, ADAM_B2, ADAM_EPS, ADAM_WD, ADAM_STEP = 0.001, 0.9, 0.999, 1e-08, 0.01, 10
NEG = -1e30

WEIGHTS = ["even_w_in", "even_q_norm", "even_w_uq", "even_kv_norm", "even_w_ukv", "even_b_f", "even_w_out",
           "even_ln_g", "even_ln_b", "odd_w_in", "odd_sinks", "odd_w_out", "odd_ln_g", "odd_ln_b"]
SHARD_AXIS = {"even_w_in": 2, "even_w_uq": 2, "even_w_ukv": 2, "even_w_out": 1, "odd_w_in": 2, "odd_w_out": 1,
              "odd_ln_g": 1, "odd_ln_b": 1, "even_q_norm": None, "even_kv_norm": None, "even_b_f": None,
              "even_ln_g": None, "even_ln_b": None, "odd_sinks": None}
BIG = ["even_w_in", "even_w_uq", "even_w_ukv", "even_w_out", "odd_w_in", "odd_w_out"]
SMALL_SHARDED = ["odd_ln_g", "odd_ln_b"]
REPL = [n for n in WEIGHTS if SHARD_AXIS[n] is None]


def _pick(n, cands):
    for c in cands:
        if n % c == 0:
            return c
    return n


def _params(**kw):
    return pltpu.CompilerParams(vmem_limit_bytes=VMEM_LIMIT, **kw)


def _me():
    return lax.axis_index("x"), lax.axis_index("y"), lax.axis_index("c")


def _peer(k):
    x, y, c = _me()
    px = 1 - x if (k >> 2) & 1 else x
    py = 1 - y if (k >> 1) & 1 else y
    pc = 1 - c if k & 1 else c
    return px, py, pc


def _lin(p):
    return 4 * p[0] + 2 * p[1] + p[2]


def _all_gather(xs, name):
    rows, lanes = xs.shape

    def body(x_ref, out_ref, send_sems, recv_sems, local_sem):
        me = _lin(_me())
        mine = pltpu.make_async_copy(x_ref, out_ref.at[me], local_sem)
        mine.start()
        sends = []
        for k in range(1, N_DEV):
            cp = pltpu.make_async_remote_copy(
                src_ref=x_ref, dst_ref=out_ref.at[me], send_sem=send_sems.at[k - 1], recv_sem=recv_sems.at[k - 1],
                device_id=_peer(k), device_id_type=pl.DeviceIdType.MESH)
            cp.start()
            sends.append(cp)
        for k in range(1, N_DEV):
            pltpu.make_async_remote_copy(
                src_ref=x_ref, dst_ref=out_ref.at[_lin(_peer(k))], send_sem=send_sems.at[k - 1],
                recv_sem=recv_sems.at[k - 1], device_id=_peer(k), device_id_type=pl.DeviceIdType.MESH).wait_recv()
        for cp in sends:
            cp.wait_send()
        mine.wait()

    return pl.pallas_call(
        body, name=name,
        out_shape=jax.ShapeDtypeStruct((N_DEV, rows, lanes), xs.dtype),
        in_specs=[pl.BlockSpec(memory_space=pl.ANY)],
        out_specs=pl.BlockSpec(memory_space=pl.ANY),
        scratch_shapes=[pltpu.SemaphoreType.DMA((N_DEV - 1,)), pltpu.SemaphoreType.DMA((N_DEV - 1,)),
                        pltpu.SemaphoreType.DMA],
    )(xs)


def _exchange(parts, name):
    _, rows, lanes = parts.shape

    def body(p_ref, out_ref, send_sems, recv_sems, local_sem):
        me = _lin(_me())
        mine = pltpu.make_async_copy(p_ref.at[me], out_ref.at[me], local_sem)
        mine.start()
        sends = []
        for k in range(1, N_DEV):
            peer = _peer(k)
            cp = pltpu.make_async_remote_copy(
                src_ref=p_ref.at[_lin(peer)], dst_ref=out_ref.at[me], send_sem=send_sems.at[k - 1],
                recv_sem=recv_sems.at[k - 1], device_id=peer, device_id_type=pl.DeviceIdType.MESH)
            cp.start()
            sends.append(cp)
        for k in range(1, N_DEV):
            peer = _peer(k)
            pltpu.make_async_remote_copy(
                src_ref=p_ref.at[_lin(peer)], dst_ref=out_ref.at[_lin(peer)], send_sem=send_sems.at[k - 1],
                recv_sem=recv_sems.at[k - 1], device_id=peer, device_id_type=pl.DeviceIdType.MESH).wait_recv()
        for cp in sends:
            cp.wait_send()
        mine.wait()

    return pl.pallas_call(
        body, name=name,
        out_shape=jax.ShapeDtypeStruct((N_DEV, rows, lanes), parts.dtype),
        in_specs=[pl.BlockSpec(memory_space=pl.ANY)],
        out_specs=pl.BlockSpec(memory_space=pl.ANY),
        scratch_shapes=[pltpu.SemaphoreType.DMA((N_DEV - 1,)), pltpu.SemaphoreType.DMA((N_DEV - 1,)),
                        pltpu.SemaphoreType.DMA],
    )(parts)


def _sum_adamw(recv, w, m, v):
    _, rows, lanes = recv.shape
    tr = _pick(rows, (512, 256, 128, 64, 32, 16, 8))
    c1 = 1.0 - ADAM_B1 ** ADAM_STEP
    c2 = 1.0 - ADAM_B2 ** ADAM_STEP

    def body(r_ref, w_ref, m_ref, v_ref, g_out, d_out, m_out, v_out):
        g = r_ref[0]
        for s in range(1, N_DEV):
            g = g + r_ref[s]
        mn = ADAM_B1 * m_ref[...] + (1.0 - ADAM_B1) * g
        vn = ADAM_B2 * v_ref[...] + (1.0 - ADAM_B2) * (g * g)
        m_hat = mn / c1
        v_hat = vn / c2
        g_out[...] = g
        d_out[...] = -ADAM_LR * (m_hat / (jnp.sqrt(v_hat) + ADAM_EPS) + ADAM_WD * w_ref[...])
        m_out[...] = mn
        v_out[...] = vn

    blk = pl.BlockSpec((tr, lanes), lambda i: (i, 0))
    shp = jax.ShapeDtypeStruct((rows, lanes), F32)
    return pl.pallas_call(
        body, name="sum_adamw", grid=(rows // tr,),
        in_specs=[pl.BlockSpec((N_DEV, tr, lanes), lambda i: (0, i, 0)), blk, blk, blk],
        out_specs=[blk, blk, blk, blk], out_shape=[shp, shp, shp, shp],
        compiler_params=_params(dimension_semantics=("parallel",)),
    )(recv, w, m, v)


def _mm_nn(a, b):
    m, k = a.shape
    _, n = b.shape
    tm = _pick(m, (512, 256, 128))
    tn = _pick(n, (640, 512, 256, 128))
    tk = _pick(k, (1024, 640, 512, 256, 128))
    nk = k // tk

    def body(a_ref, b_ref, o_ref, acc_ref):
        kk = pl.program_id(2)

        @pl.when(kk == 0)
        def _():
            acc_ref[...] = jnp.zeros_like(acc_ref)

        acc_ref[...] += jnp.dot(a_ref[...].astype(BF16), b_ref[...].astype(BF16), preferred_element_type=F32)

        @pl.when(kk == nk - 1)
        def _():
            o_ref[...] = acc_ref[...]

    return pl.pallas_call(
        body, name=f"mm_nn_{m}x{k}x{n}", grid=(m // tm, n // tn, nk),
        in_specs=[pl.BlockSpec((tm, tk), lambda i, j, kk: (i, kk)), pl.BlockSpec((tk, tn), lambda i, j, kk: (kk, j))],
        out_specs=pl.BlockSpec((tm, tn), lambda i, j, kk: (i, j)),
        out_shape=jax.ShapeDtypeStruct((m, n), F32),
        scratch_shapes=[pltpu.VMEM((tm, tn), F32)],
        compiler_params=_params(dimension_semantics=("parallel", "parallel", "arbitrary")),
    )(a, b)


def _mm_tn(a, g):
    s, k = a.shape
    _, n = g.shape
    tm = _pick(k, (512, 256, 128))
    tn = _pick(n, (640, 512, 256, 128))
    ts = _pick(s, (512, 256, 128))
    ns = s // ts

    def body(a_ref, g_ref, o_ref, acc_ref):
        ss = pl.program_id(2)

        @pl.when(ss == 0)
        def _():
            acc_ref[...] = jnp.zeros_like(acc_ref)

        acc_ref[...] += lax.dot_general(a_ref[...].astype(BF16), g_ref[...].astype(BF16),
                                        (((0,), (0,)), ((), ())), preferred_element_type=F32)

        @pl.when(ss == ns - 1)
        def _():
            o_ref[...] = acc_ref[...]

    return pl.pallas_call(
        body, name=f"mm_tn_{s}x{k}x{n}", grid=(k // tm, n // tn, ns),
        in_specs=[pl.BlockSpec((ts, tm), lambda i, j, ss: (ss, i)), pl.BlockSpec((ts, tn), lambda i, j, ss: (ss, j))],
        out_specs=pl.BlockSpec((tm, tn), lambda i, j, ss: (i, j)),
        out_shape=jax.ShapeDtypeStruct((k, n), F32),
        scratch_shapes=[pltpu.VMEM((tm, tn), F32)],
        compiler_params=_params(dimension_semantics=("parallel", "parallel", "arbitrary")),
    )(a, g)


@jax.custom_vjp
def mm(a, w):
    return _mm_nn(a, w.astype(BF16))


def _mm_fwd(a, w):
    wb = w.astype(BF16)
    return _mm_nn(a, wb), (a, wb)


def _mm_bwd(res, g):
    a, wb = res
    return _mm_nn(g, wb.T), _mm_tn(a, g)


mm.defvjp(_mm_fwd, _mm_bwd)


def _row_block(s):
    return _pick(s, (512, 256, 128, 64, 32, 16, 8))


def _rms_fwd_call(x, g):
    s, k = x.shape
    tr = _row_block(s)

    def body(x_ref, g_ref, o_ref):
        xv = x_ref[...]
        r = lax.rsqrt(jnp.mean(xv * xv, axis=-1, keepdims=True) + RMS_EPS)
        o_ref[...] = xv * r * g_ref[...]

    return pl.pallas_call(
        body, name=f"rms_fwd_{k}", grid=(s // tr,),
        in_specs=[pl.BlockSpec((tr, k), lambda i: (i, 0)), pl.BlockSpec((1, k), lambda i: (0, 0))],
        out_specs=pl.BlockSpec((tr, k), lambda i: (i, 0)), out_shape=jax.ShapeDtypeStruct((s, k), F32),
        compiler_params=_params(dimension_semantics=("parallel",)),
    )(x, g.reshape(1, k))


def _rms_bwd_call(x, g, dy):
    s, k = x.shape
    tr = _row_block(s)

    def body(x_ref, g_ref, dy_ref, dx_ref, dg_ref):
        @pl.when(pl.program_id(0) == 0)
        def _():
            dg_ref[...] = jnp.zeros_like(dg_ref)

        xv = x_ref[...]
        r = lax.rsqrt(jnp.mean(xv * xv, axis=-1, keepdims=True) + RMS_EPS)
        xh = xv * r
        dyv = dy_ref[...]
        dg_ref[...] += jnp.sum(dyv * xh, axis=0, keepdims=True)
        dxh = dyv * g_ref[...]
        dx_ref[...] = r * (dxh - xh * jnp.mean(dxh * xh, axis=-1, keepdims=True))

    dx, dg = pl.pallas_call(
        body, name=f"rms_bwd_{k}", grid=(s // tr,),
        in_specs=[pl.BlockSpec((tr, k), lambda i: (i, 0)), pl.BlockSpec((1, k), lambda i: (0, 0)),
                  pl.BlockSpec((tr, k), lambda i: (i, 0))],
        out_specs=[pl.BlockSpec((tr, k), lambda i: (i, 0)), pl.BlockSpec((1, k), lambda i: (0, 0))],
        out_shape=[jax.ShapeDtypeStruct((s, k), F32), jax.ShapeDtypeStruct((1, k), F32)],
        compiler_params=_params(dimension_semantics=("arbitrary",)),
    )(x, g.reshape(1, k), dy)
    return dx, dg.reshape(k)


@jax.custom_vjp
def rms_norm(x, g):
    return _rms_fwd_call(x, g)


rms_norm.defvjp(lambda x, g: (_rms_fwd_call(x, g), (x, g)), lambda res, dy: _rms_bwd_call(res[0], res[1], dy))


def _ln_fwd_call(x, y, g, b):
    s, k = x.shape
    tr = _row_block(s)

    def body(x_ref, y_ref, g_ref, b_ref, o_ref):
        u = ALPHA * x_ref[...] + y_ref[...]
        mu = jnp.mean(u, axis=-1, keepdims=True)
        d = u - mu
        var = jnp.mean(d * d, axis=-1, keepdims=True)
        o_ref[...] = d * lax.rsqrt(var + LN_EPS) * g_ref[...] + b_ref[...]

    row = pl.BlockSpec((tr, k), lambda i: (i, 0))
    vec = pl.BlockSpec((1, k), lambda i: (0, 0))
    return pl.pallas_call(
        body, name="ln_fwd", grid=(s // tr,), in_specs=[row, row, vec, vec], out_specs=row,
        out_shape=jax.ShapeDtypeStruct((s, k), F32), compiler_params=_params(dimension_semantics=("parallel",)),
    )(x, y, g.reshape(1, k), b.reshape(1, k))


def _ln_bwd_call(x, y, g, do):
    s, k = x.shape
    tr = _row_block(s)

    def body(x_ref, y_ref, g_ref, do_ref, dx_ref, dy_ref, dg_ref, db_ref):
        @pl.when(pl.program_id(0) == 0)
        def _():
            dg_ref[...] = jnp.zeros_like(dg_ref)
            db_ref[...] = jnp.zeros_like(db_ref)

        u = ALPHA * x_ref[...] + y_ref[...]
        mu = jnp.mean(u, axis=-1, keepdims=True)
        d = u - mu
        r = lax.rsqrt(jnp.mean(d * d, axis=-1, keepdims=True) + LN_EPS)
        xh = d * r
        dov = do_ref[...]
        dg_ref[...] += jnp.sum(dov * xh, axis=0, keepdims=True)
        db_ref[...] += jnp.sum(dov, axis=0, keepdims=True)
        dxh = dov * g_ref[...]
        du = r * (dxh - jnp.mean(dxh, axis=-1, keepdims=True) - xh * jnp.mean(dxh * xh, axis=-1, keepdims=True))
        dy_ref[...] = du
        dx_ref[...] = ALPHA * du

    row = pl.BlockSpec((tr, k), lambda i: (i, 0))
    vec = pl.BlockSpec((1, k), lambda i: (0, 0))
    dx, dy, dg, db = pl.pallas_call(
        body, name="ln_bwd", grid=(s // tr,), in_specs=[row, row, vec, row], out_specs=[row, row, vec, vec],
        out_shape=[jax.ShapeDtypeStruct((s, k), F32), jax.ShapeDtypeStruct((s, k), F32),
                   jax.ShapeDtypeStruct((1, k), F32), jax.ShapeDtypeStruct((1, k), F32)],
        compiler_params=_params(dimension_semantics=("arbitrary",)),
    )(x, y, g.reshape(1, k), do)
    return dx, dy, dg.reshape(k), db.reshape(k)


@jax.custom_vjp
def ln_res(x, y, g, b):
    return _ln_fwd_call(x, y, g, b)


ln_res.defvjp(lambda x, y, g, b: (_ln_fwd_call(x, y, g, b), (x, y, g)),
              lambda res, do: _ln_bwd_call(res[0], res[1], res[2], do))


def _rope_call(x, c, s1, s2, hs):
    s, w = x.shape
    tr = _row_block(s)
    nb = w // LANES

    def body(x_ref, c_ref, s1_ref, s2_ref, o_ref):
        cv, s1v, s2v = c_ref[...], s1_ref[...], s2_ref[...]
        for cb in range(nb):
            xb = x_ref[:, cb * LANES:(cb + 1) * LANES]
            o_ref[:, cb * LANES:(cb + 1) * LANES] = (
                xb * cv + pltpu.roll(xb, LANES - hs, 1) * s1v + pltpu.roll(xb, hs, 1) * s2v)

    row = pl.BlockSpec((tr, w), lambda i: (i, 0))
    tab = pl.BlockSpec((tr, LANES), lambda i: (i, 0))
    return pl.pallas_call(
        body, name=f"rope_{w}_{hs}", grid=(s // tr,), in_specs=[row, tab, tab, tab], out_specs=row,
        out_shape=jax.ShapeDtypeStruct((s, w), F32), compiler_params=_params(dimension_semantics=("parallel",)),
    )(x, c, s1, s2)


@functools.partial(jax.custom_vjp, nondiff_argnums=(2,))
def rope(x, tabs, hs):
    return _rope_call(x, tabs[0], tabs[1], tabs[2], hs)


def _rope_fwd(x, tabs, hs):
    return _rope_call(x, tabs[0], tabs[1], tabs[2], hs), tabs


def _rope_bwd(hs, tabs, dy):
    return _rope_call(dy, tabs[0], -tabs[1], -tabs[2], hs), jax.tree.map(jnp.zeros_like, tabs)


rope.defvjp(_rope_fwd, _rope_bwd)


def _rope_tables(s, layout):
    pos = jnp.arange(s, dtype=F32)[:, None]
    lane = jnp.arange(LANES)
    if layout == "mla":
        dim, hs = MLA_ROPE, MLA_ROPE // 2
        r = lane - MLA_NOPE
        active = (r >= 0) & (r < MLA_ROPE)
    else:
        dim, hs = SWA_DIM, SWA_DIM // 2
        r = lane % SWA_DIM
        active = jnp.ones_like(lane, dtype=bool)
    f = jnp.where(active, r % hs, 0)
    inv = ROPE_THETA ** (-(2.0 * f.astype(F32)) / dim)
    ang = pos * inv[None, :]
    cos, sin = jnp.cos(ang), jnp.sin(ang)
    first = (active & (r < hs))[None, :]
    second = (active & (r >= hs))[None, :]
    c = jnp.where(active[None, :], cos, 1.0)
    s1 = jnp.where(first, -sin, 0.0)
    s2 = jnp.where(second, sin, 0.0)
    return (c, s1, s2), hs


def _gate_fwd_call(o_parts, gate):
    s, w = gate.shape
    tr = _row_block(s)
    widths = [o.shape[1] for o in o_parts]

    def body(*refs):
        o_refs, g_ref, z_ref = refs[:len(widths)], refs[len(widths)], refs[len(widths) + 1]
        off = 0
        for o_ref, wd in zip(o_refs, widths):
            gv = g_ref[:, off:off + wd]
            z_ref[:, off:off + wd] = o_ref[...] * (gv * jax.nn.sigmoid(gv))
            off += wd

    specs = [pl.BlockSpec((tr, wd), lambda i: (i, 0)) for wd in widths]
    row = pl.BlockSpec((tr, w), lambda i: (i, 0))
    return pl.pallas_call(
        body, name=f"gate_fwd_{len(widths)}", grid=(s // tr,), in_specs=specs + [row], out_specs=row,
        out_shape=jax.ShapeDtypeStruct((s, w), F32), compiler_params=_params(dimension_semantics=("parallel",)),
    )(*o_parts, gate)


def _gate_bwd_call(o_parts, gate, dz):
    s, w = gate.shape
    tr = _row_block(s)
    widths = [o.shape[1] for o in o_parts]
    n = len(widths)

    def body(*refs):
        o_refs, g_ref, dz_ref = refs[:n], refs[n], refs[n + 1]
        do_refs, dg_ref = refs[n + 2:2 * n + 2], refs[2 * n + 2]
        off = 0
        for o_ref, do_ref, wd in zip(o_refs, do_refs, widths):
            gv = g_ref[:, off:off + wd]
            sg = jax.nn.sigmoid(gv)
            dzv = dz_ref[:, off:off + wd]
            do_ref[...] = dzv * (gv * sg)
            dg_ref[:, off:off + wd] = dzv * o_ref[...] * (sg * (1.0 + gv * (1.0 - sg)))
            off += wd

    specs = [pl.BlockSpec((tr, wd), lambda i: (i, 0)) for wd in widths]
    row = pl.BlockSpec((tr, w), lambda i: (i, 0))
    outs = pl.pallas_call(
        body, name=f"gate_bwd_{n}", grid=(s // tr,), in_specs=specs + [row, row], out_specs=specs + [row],
        out_shape=[jax.ShapeDtypeStruct((s, wd), F32) for wd in widths] + [jax.ShapeDtypeStruct((s, w), F32)],
        compiler_params=_params(dimension_semantics=("parallel",)),
    )(*o_parts, gate, dz)
    return tuple(outs[:n]), outs[n]


@jax.custom_vjp
def gate_mul(o_parts, gate):
    return _gate_fwd_call(o_parts, gate)


gate_mul.defvjp(lambda o_parts, gate: (_gate_fwd_call(o_parts, gate), (o_parts, gate)),
                lambda res, dz: _gate_bwd_call(res[0], res[1], dz))


def _loss_call(y, t):
    s, k = y.shape
    tr = _row_block(s)
    nsteps = s // tr

    def body(y_ref, t_ref, l_ref, dy_ref, acc_ref):
        i = pl.program_id(0)

        @pl.when(i == 0)
        def _():
            acc_ref[...] = jnp.zeros_like(acc_ref)

        d = y_ref[...] - t_ref[...]
        dy_ref[...] = d / k
        acc_ref[...] += jnp.sum(d * d, axis=0, keepdims=True)

        @pl.when(i == nsteps - 1)
        def _():
            tot = jnp.sum(acc_ref[...], axis=1, keepdims=True) * (0.5 / k)
            l_ref[...] = jnp.broadcast_to(tot, l_ref.shape)

    row = pl.BlockSpec((tr, k), lambda i: (i, 0))
    return pl.pallas_call(
        body, name="loss", grid=(nsteps,), in_specs=[row, row],
        out_specs=[pl.BlockSpec((1, LANES), lambda i: (0, 0)), row],
        out_shape=[jax.ShapeDtypeStruct((1, LANES), F32), jax.ShapeDtypeStruct((s, k), F32)],
        scratch_shapes=[pltpu.VMEM((1, k), F32)], compiler_params=_params(dimension_semantics=("arbitrary",)),
    )(y, t)


@jax.custom_vjp
def mse_loss(y, t):
    return _loss_call(y, t)[0][0, 0]


def _mse_fwd(y, t):
    l, dy = _loss_call(y, t)
    return l[0, 0], (dy, t)


mse_loss.defvjp(_mse_fwd, lambda res, g: (g * res[0], jnp.zeros_like(res[1])))


def _scan_call(x, b, mode):
    s, w = x.shape
    nt = s // 8

    def tile_scan(t):
        row = lax.broadcasted_iota(jnp.int32, (8, w), 0)
        for sh in (1, 2, 4):
            t = t + jnp.where(row >= sh, pltpu.roll(t, sh, 0), 0.0)
        return t

    def body(x_ref, b_ref, o_ref):
        def step(i, carry):
            rows = pl.ds(pl.multiple_of(i * 8, 8), 8)
            t = x_ref[rows, :]
            if mode == "fwd":
                t = jax.nn.log_sigmoid(t + b_ref[...])
            t = tile_scan(t) + carry
            o_ref[rows, :] = t
            return t[7:8, :]

        total = lax.fori_loop(0, nt, step, jnp.zeros((1, w), F32))
        if mode == "rev":
            def fix(i, c):
                rows = pl.ds(pl.multiple_of(i * 8, 8), 8)
                o_ref[rows, :] = total - o_ref[rows, :] + x_ref[rows, :]
                return c
            lax.fori_loop(0, nt, fix, 0)

    full = pl.BlockSpec((s, w), lambda: (0, 0))
    return pl.pallas_call(
        body, name=f"scan_{mode}", in_specs=[full, pl.BlockSpec((1, w), lambda: (0, 0))], out_specs=full,
        out_shape=jax.ShapeDtypeStruct((s, w), F32), compiler_params=_params(),
    )(x, b)


def _fox_dlogit_call(x, b, dlogf):
    s, w = x.shape
    tr = _row_block(s)

    def body(x_ref, b_ref, d_ref, dx_ref, db_ref):
        @pl.when(pl.program_id(0) == 0)
        def _():
            db_ref[...] = jnp.zeros_like(db_ref)

        dx = d_ref[...] * jax.nn.sigmoid(-(x_ref[...] + b_ref[...]))
        dx_ref[...] = dx
        db_ref[...] += jnp.sum(dx, axis=0, keepdims=True)

    row = pl.BlockSpec((tr, w), lambda i: (i, 0))
    vec = pl.BlockSpec((1, w), lambda i: (0, 0))
    return pl.pallas_call(
        body, name="fox_dlogit", grid=(s // tr,), in_specs=[row, vec, row], out_specs=[row, vec],
        out_shape=[jax.ShapeDtypeStruct((s, w), F32), jax.ShapeDtypeStruct((1, w), F32)],
        compiler_params=_params(dimension_semantics=("arbitrary",)),
    )(x, b, dlogf)


@jax.custom_vjp
def fox_cum(fl, b):
    return _scan_call(fl, b, "fwd")


def _fox_cum_bwd(res, dcum):
    fl, b = res
    dlogf = _scan_call(dcum, b, "rev")
    return _fox_dlogit_call(fl, b, dlogf)


fox_cum.defvjp(lambda fl, b: (_scan_call(fl, b, "fwd"), (fl, b)), _fox_cum_bwd)


def _lane_col(x, lane_idx):
    lane = lax.broadcasted_iota(jnp.int32, (1, x.shape[1]), 1)
    return jnp.sum(jnp.where(lane == lane_idx, x, 0.0), axis=1, keepdims=True)


def _row_of(x, row_idx):
    row = lax.broadcasted_iota(jnp.int32, (x.shape[0], 1), 0)
    return jnp.sum(jnp.where(row == row_idx, x, 0.0), axis=0, keepdims=True)


def _attn_cfg(mode, s):
    if mode == "swa":
        blk = 256 if s >= 2048 else 128
        return dict(blk=blk, n_outer=SWA_KV_HEADS, pps=4, wide=False, scale=SWA_DIM ** -0.5)
    blk = 512 if s >= 2048 else 128
    if mode == "mla":
        return dict(blk=blk, n_outer=4, pps=1, wide=True, scale=(MLA_NOPE + MLA_ROPE) ** -0.5)
    return dict(blk=blk, n_outer=4, pps=1, wide=False, scale=FOX_DIM ** -0.5)


def _valid_mask(mode, i, jb, blk):
    qpos = i * blk + lax.broadcasted_iota(jnp.int32, (blk, blk), 0)
    kpos = jb * blk + lax.broadcasted_iota(jnp.int32, (blk, blk), 1)
    ok = kpos <= qpos
    if mode == "swa":
        ok = ok & (qpos - kpos < WINDOW)
    return ok


def _attn_fwd_call(mode, q, k, v, extra):
    s = q.shape[0]
    cfg = _attn_cfg(mode, s)
    blk, n_outer, pps, wide, scale = cfg["blk"], cfg["n_outer"], cfg["pps"], cfg["wide"], cfg["scale"]
    nq = s // blk
    swa, fox = mode == "swa", mode == "fox"
    qw = (2 * LANES if wide else LANES) * pps
    kw = 2 * LANES if wide else LANES
    ow = LANES * pps
    reps = blk // LANES

    def body(*refs):
        q_ref, k_ref, v_ref = refs[:3]
        n_in = 3
        if fox:
            cum_ref, cumt_ref = refs[3:5]
            n_in = 5
        if swa:
            sink_ref = refs[3]
            n_in = 4
        o_ref, lse_ref, m_ref, l_ref, acc_ref = refs[n_in:]
        p_id, i, j = pl.program_id(0), pl.program_id(1), pl.program_id(2)
        if swa:
            jb, run, first, last = i - 1 + j, (i - 1 + j) >= 0, j == 0, j == 1
        else:
            jb, run, first, last = j, j <= i, j == 0, j == i
        lane = lax.broadcasted_iota(jnp.int32, (1, LANES), 1)
        msk = [lane < HALF, lane >= HALF]

        @pl.when(first)
        def _():
            for hh in range(2 * pps):
                if swa:
                    m_ref[hh] = jnp.broadcast_to(sink_ref[hh:hh + 1, :], (blk, LANES))
                    l_ref[hh] = jnp.ones((blk, LANES), F32)
                else:
                    m_ref[hh] = jnp.full((blk, LANES), NEG, F32)
                    l_ref[hh] = jnp.zeros((blk, LANES), F32)
            acc_ref[...] = jnp.zeros_like(acc_ref)

        @pl.when(run)
        def _():
            ok = _valid_mask(mode, i, jb, blk)
            for pp in range(pps):
                vb = v_ref[...]
                pvs, alphas = [], []
                for h in range(2):
                    hh = 2 * pp + h
                    if wide:
                        qh = q_ref[:, h * LANES:(h + 1) * LANES]
                        kh = k_ref[:, h * LANES:(h + 1) * LANES]
                    else:
                        qh = jnp.where(msk[h], q_ref[:, pp * LANES:(pp + 1) * LANES], 0.0)
                        kh = k_ref[...]
                    sc = lax.dot_general(qh.astype(BF16), kh.astype(BF16), (((1,), (1,)), ((), ())),
                                         preferred_element_type=F32) * scale
                    if fox:
                        head = 2 * p_id + h
                        sc = sc + _lane_col(cum_ref[...], head) - _row_of(cumt_ref[...], head)
                    sc = jnp.where(ok, sc, NEG)
                    m_prev, l_prev = m_ref[hh], l_ref[hh]
                    m_next = jnp.maximum(m_prev, jnp.max(sc, axis=1, keepdims=True))
                    p = jnp.exp(sc - jnp.tile(m_next, (1, reps)))
                    alpha = jnp.exp(m_prev - m_next)
                    l_ref[hh] = alpha * l_prev + jnp.sum(p, axis=1, keepdims=True)
                    m_ref[hh] = m_next
                    vh = jnp.where(msk[h], vb, 0.0).astype(BF16)
                    pvs.append(jnp.dot(p.astype(BF16), vh, preferred_element_type=F32))
                    alphas.append(alpha)
                acc_ref[pp] = acc_ref[pp] * jnp.where(msk[0], alphas[0], alphas[1]) + pvs[0] + pvs[1]

        @pl.when(last)
        def _():
            for pp in range(pps):
                l0, l1 = l_ref[2 * pp], l_ref[2 * pp + 1]
                o_ref[:, pp * LANES:(pp + 1) * LANES] = acc_ref[pp] / jnp.where(msk[0], l0, l1)
                lse_ref[:, pp * LANES:(pp + 1) * LANES] = jnp.where(
                    msk[0], m_ref[2 * pp] + jnp.log(l0), m_ref[2 * pp + 1] + jnp.log(l1))

    if swa:
        kv_map = lambda g, i, j: (jnp.maximum(i - 1 + j, 0), g)
        grid = (n_outer, nq, 2)
    else:
        kv_map = lambda p, i, j: (jnp.minimum(j, i), p)
        grid = (n_outer, nq, nq)
    q_map = lambda p, i, j: (i, p)
    in_specs = [pl.BlockSpec((blk, qw), q_map), pl.BlockSpec((blk, kw), kv_map), pl.BlockSpec((blk, LANES), kv_map)]
    args = [q, k, v]
    if fox:
        cum, cumt = extra
        in_specs += [pl.BlockSpec((blk, LANES), lambda p, i, j: (i, 0)),
                     pl.BlockSpec((8, blk), lambda p, i, j: (0, jnp.minimum(j, i)))]
        args += [cum, cumt]
    if swa:
        in_specs += [pl.BlockSpec((8, LANES), lambda g, i, j: (g, 0))]
        args += [extra]
    n_pairs = n_outer * pps
    return pl.pallas_call(
        body, name=f"attn_fwd_{mode}", grid=grid, in_specs=in_specs,
        out_specs=[pl.BlockSpec((blk, ow), q_map), pl.BlockSpec((blk, ow), q_map)],
        out_shape=[jax.ShapeDtypeStruct((s, n_pairs * LANES), F32), jax.ShapeDtypeStruct((s, n_pairs * LANES), F32)],
        scratch_shapes=[pltpu.VMEM((2 * pps, blk, LANES), F32), pltpu.VMEM((2 * pps, blk, LANES), F32),
                        pltpu.VMEM((pps, blk, LANES), F32)],
        compiler_params=_params(dimension_semantics=("parallel", "parallel", "arbitrary")),
    )(*args)


def _rowdot_call(do, o):
    s, w = o.shape
    tr = _row_block(s)

    def body(do_ref, o_ref, d_ref):
        lane = lax.broadcasted_iota(jnp.int32, (1, LANES), 1)
        low = lane < HALF
        for cb in range(w // LANES):
            sl = slice(cb * LANES, (cb + 1) * LANES)
            prod = do_ref[:, sl] * o_ref[:, sl]
            d0 = jnp.sum(jnp.where(low, prod, 0.0), axis=1, keepdims=True)
            d1 = jnp.sum(jnp.where(low, 0.0, prod), axis=1, keepdims=True)
            d_ref[:, sl] = jnp.where(low, d0, d1)

    row = pl.BlockSpec((tr, w), lambda i: (i, 0))
    return pl.pallas_call(
        body, name=f"rowdot_{w}", grid=(s // tr,), in_specs=[row, row], out_specs=row,
        out_shape=jax.ShapeDtypeStruct((s, w), F32), compiler_params=_params(dimension_semantics=("parallel",)),
    )(do, o)


def _attn_bwd_call(mode, q, k, v, extra, lse, dd, do):
    s = q.shape[0]
    cfg = _attn_cfg(mode, s)
    blk, n_outer, pps, wide, scale = cfg["blk"], cfg["n_outer"], cfg["pps"], cfg["wide"], cfg["scale"]
    nq = s // blk
    swa, fox = mode == "swa", mode == "fox"
    qw = (2 * LANES if wide else LANES) * pps
    kw = 2 * LANES if wide else LANES
    ow = LANES * pps
    reps = blk // LANES

    def body(*refs):
        q_ref, k_ref, v_ref, lse_ref, dd_ref, do_ref = refs[:6]
        n_in = 6
        if fox:
            cum_ref, cumt_ref = refs[6:8]
            n_in = 8
        if swa:
            sink_ref = refs[6]
            n_in = 7
        dq_ref, dk_ref, dv_ref = refs[n_in:n_in + 3]
        n_out = n_in + 3
        if fox:
            dck_ref, dcq_ref = refs[n_out:n_out + 2]
            n_out += 2
        if swa:
            dsink_ref = refs[n_out]
            n_out += 1
        dk_acc, dv_acc = refs[n_out:n_out + 2]
        if fox:
            dck_acc = refs[n_out + 2]
        p_id, j, ii = pl.program_id(0), pl.program_id(1), pl.program_id(2)
        if swa:
            i, run, first_i, last_i = j + ii, (j + ii) < nq, ii == 0, ii == 1
        else:
            i, run, first_i, last_i = ii, ii >= j, ii == j, ii == nq - 1
        lane = lax.broadcasted_iota(jnp.int32, (1, LANES), 1)
        msk = [lane < HALF, lane >= HALF]

        @pl.when((j == 0) & (ii == 0))
        def _():
            dq_ref[...] = jnp.zeros_like(dq_ref)
            if swa:
                dsink_ref[...] = jnp.zeros_like(dsink_ref)
            if fox:
                dcq_ref[...] = jnp.zeros_like(dcq_ref)

        @pl.when(first_i)
        def _():
            dk_acc[...] = jnp.zeros_like(dk_acc)
            dv_acc[...] = jnp.zeros_like(dv_acc)
            if fox:
                dck_acc[...] = jnp.zeros_like(dck_acc)

        @pl.when(run)
        def _():
            ok = _valid_mask(mode, i, j, blk)
            rows = pl.ds(pl.multiple_of(i * blk, blk), blk)
            vb = v_ref[...].astype(BF16)
            for pp in range(pps):
                psl = slice(pp * LANES, (pp + 1) * LANES)
                lse_blk, dd_blk, do_blk = lse_ref[:, psl], dd_ref[:, psl], do_ref[:, psl]
                for h in range(2):
                    hh = 2 * pp + h
                    if wide:
                        hsl = slice(h * LANES, (h + 1) * LANES)
                        qh = q_ref[:, hsl].astype(BF16)
                        kh = k_ref[:, hsl].astype(BF16)
                    else:
                        qh = jnp.where(msk[h], q_ref[:, psl], 0.0).astype(BF16)
                        kh = k_ref[...].astype(BF16)
                    sc = lax.dot_general(qh, kh, (((1,), (1,)), ((), ())), preferred_element_type=F32) * scale
                    if fox:
                        head = 2 * p_id + h
                        sc = sc + _lane_col(cum_ref[...], head) - _row_of(cumt_ref[...], head)
                    lse_h = _lane_col(lse_blk, HALF * h)
                    d_h = _lane_col(dd_blk, HALF * h)
                    p = jnp.where(ok, jnp.exp(sc - lse_h), 0.0)
                    do_h = jnp.where(msk[h], do_blk, 0.0).astype(BF16)
                    dv_acc[...] += lax.dot_general(p.astype(BF16), do_h, (((0,), (0,)), ((), ())),
                                                   preferred_element_type=F32)
                    dp = lax.dot_general(do_h, vb, (((1,), (1,)), ((), ())), preferred_element_type=F32)
                    ds = p * (dp - d_h)
                    if fox:
                        dck_acc[h:h + 1, :] += -jnp.sum(ds, axis=0, keepdims=True)
                        dcq_ref[rows, :] += jnp.where(msk[h], jnp.sum(ds, axis=1, keepdims=True), 0.0)
                    dsb = (ds * scale).astype(BF16)
                    dq_h = jnp.dot(dsb, kh, preferred_element_type=F32)
                    dk_h = lax.dot_general(dsb, qh, (((0,), (0,)), ((), ())), preferred_element_type=F32)
                    if wide:
                        dq_ref[rows, hsl] += dq_h
                        dk_acc[:, hsl] += dk_h
                    else:
                        dq_ref[rows, psl] += jnp.where(msk[h], dq_h, 0.0)
                        dk_acc[...] += dk_h
                    if swa:
                        @pl.when(first_i)
                        def _():
                            e = jnp.exp(sink_ref[hh:hh + 1, :] - lse_h) * d_h
                            dsink_ref[hh:hh + 1, :] += -jnp.sum(e, axis=0, keepdims=True)

        @pl.when(last_i)
        def _():
            dk_ref[...] = dk_acc[...]
            dv_ref[...] = dv_acc[...]
            if fox:
                dck_ref[0] = dck_acc[...]

    if swa:
        q_map = lambda g, j, ii: (jnp.minimum(j + ii, nq - 1), g)
        grid = (n_outer, nq, 2)
    else:
        q_map = lambda p, j, ii: (jnp.maximum(ii, j), p)
        grid = (n_outer, nq, nq)
    kv_map = lambda p, j, ii: (j, p)
    in_specs = [pl.BlockSpec((blk, qw), q_map), pl.BlockSpec((blk, kw), kv_map), pl.BlockSpec((blk, LANES), kv_map),
                pl.BlockSpec((blk, ow), q_map), pl.BlockSpec((blk, ow), q_map), pl.BlockSpec((blk, ow), q_map)]
    args = [q, k, v, lse, dd, do]
    n_pairs = n_outer * pps
    out_specs = [pl.BlockSpec((s, qw), lambda p, j, ii: (0, p)), pl.BlockSpec((blk, kw), kv_map),
                 pl.BlockSpec((blk, LANES), kv_map)]
    out_shape = [jax.ShapeDtypeStruct((s, q.shape[1]), F32), jax.ShapeDtypeStruct((s, k.shape[1]), F32),
                 jax.ShapeDtypeStruct((s, v.shape[1]), F32)]
    scratch = [pltpu.VMEM((blk, kw), F32), pltpu.VMEM((blk, LANES), F32)]
    if fox:
        cum, cumt = extra
        if swa:
            raise AssertionError
        in_specs += [pl.BlockSpec((blk, LANES), lambda p, j, ii: (jnp.maximum(ii, j), 0)),
                     pl.BlockSpec((8, blk), lambda p, j, ii: (0, j))]
        args += [cum, cumt]
        out_specs += [pl.BlockSpec((1, 8, blk), lambda p, j, ii: (p, 0, j)),
                      pl.BlockSpec((s, LANES), lambda p, j, ii: (0, p))]
        out_shape += [jax.ShapeDtypeStruct((n_pairs, 8, s), F32), jax.ShapeDtypeStruct((s, n_pairs * LANES), F32)]
        scratch += [pltpu.VMEM((8, blk), F32)]
    if swa:
        in_specs += [pl.BlockSpec((8, LANES), lambda g, j, ii: (g, 0))]
        args += [extra]
        out_specs += [pl.BlockSpec((8, LANES), lambda g, j, ii: (g, 0))]
        out_shape += [jax.ShapeDtypeStruct((SWA_HEADS, LANES), F32)]
    return pl.pallas_call(
        body, name=f"attn_bwd_{mode}", grid=grid, in_specs=in_specs, out_specs=out_specs, out_shape=out_shape,
        scratch_shapes=scratch,
        compiler_params=_params(dimension_semantics=("parallel", "arbitrary", "arbitrary")),
    )(*args)


def _make_attn(mode):
    @jax.custom_vjp
    def attn(q, k, v, extra):
        return _attn_fwd_call(mode, q, k, v, extra)[0]

    def fwd(q, k, v, extra):
        o, lse = _attn_fwd_call(mode, q, k, v, extra)
        return o, (q, k, v, extra, o, lse)

    def bwd(res, do):
        q, k, v, extra, o, lse = res
        dd = _rowdot_call(do, o)
        outs = _attn_bwd_call(mode, q, k, v, extra, lse, dd, do)
        dq, dk, dv = outs[:3]
        if mode == "fox":
            cum, cumt = extra
            dck = outs[3]
            dcumt = dck[:, :2, :].reshape(FOX_HEADS, -1)
            dcq = outs[4].reshape(-1, FOX_HEADS, HALF)[:, :, 0]
            dextra = (jnp.pad(dcq, ((0, 0), (0, LANES - FOX_HEADS))), dcumt)
        elif mode == "swa":
            dextra = jnp.where(jnp.arange(LANES)[None, :] == 0, outs[3], 0.0)
        else:
            dextra = None
        return dq, dk, dv, dextra

    attn.defvjp(fwd, bwd)
    return attn


attn_mla = _make_attn("mla")
attn_fox = _make_attn("fox")
attn_swa = _make_attn("swa")


def _even_w_in_layout(w):
    d = w.shape[0]
    z = lambda n: jnp.zeros((d, n), w.dtype)
    c = 0
    cq = w[:, c:c + MLA_Q_RANK]; c += MLA_Q_RANK
    ckv = w[:, c:c + MLA_KV_RANK]; c += MLA_KV_RANK
    kpe = w[:, c:c + MLA_ROPE]; c += MLA_ROPE
    fqkv = w[:, c:c + 3 * FOX_HEADS * FOX_DIM]; c += 3 * FOX_HEADS * FOX_DIM
    fl = w[:, c:c + FOX_HEADS]; c += FOX_HEADS
    gate = w[:, c:]
    return jnp.concatenate([cq, ckv, z(MLA_NOPE), kpe, z(LANES - MLA_NOPE - MLA_ROPE), fqkv, fl, z(LANES - FOX_HEADS),
                            gate], axis=1)


def _uq_layout(w):
    r = w.shape[0]
    w3 = w.reshape(r, MLA_HEADS, MLA_NOPE + MLA_ROPE)
    return jnp.pad(w3, ((0, 0), (0, 0), (0, LANES - MLA_NOPE - MLA_ROPE))).reshape(r, MLA_HEADS * LANES)


def _ukv_layout(w):
    r = w.shape[0]
    w3 = w.reshape(r, MLA_HEADS, MLA_NOPE + MLA_V)
    wk = jnp.pad(w3[:, :, :MLA_NOPE], ((0, 0), (0, 0), (0, LANES - MLA_NOPE))).reshape(r, MLA_HEADS * LANES)
    wv = w3[:, :, MLA_NOPE:].reshape(r, MLA_HEADS * MLA_V)
    return wk, wv


def _odd_w_in_layout(w):
    nq, nkv = SWA_HEADS * SWA_DIM, SWA_KV_HEADS * SWA_DIM
    q, k, v, gate = w[:, :nq], w[:, nq:nq + nkv], w[:, nq + nkv:nq + 2 * nkv], w[:, nq + 2 * nkv:]
    dup = lambda t: jnp.concatenate([t[:, g * SWA_DIM:(g + 1) * SWA_DIM] for g in range(SWA_KV_HEADS) for _ in range(2)], 1)
    return jnp.concatenate([q, dup(k), dup(v), gate], axis=1)


def _even_layer(x, w_in, q_norm, w_uq, kv_norm, w_ukv, b_f, w_out, ln_g, ln_b, tabs_mla):
    h = mm(x, _even_w_in_layout(w_in))
    c = 0
    cq = h[:, c:c + 256]; c += 256
    ckv = h[:, c:c + 128]; c += 128
    kpe = h[:, c:c + 128]; c += 128
    fq = h[:, c:c + 512]; c += 512
    fk = h[:, c:c + 512]; c += 512
    fv = h[:, c:c + 512]; c += 512
    fl = h[:, c:c + 128]; c += 128
    gate = h[:, c:]
    tabs, hs = tabs_mla
    q = rope(mm(rms_norm(cq, q_norm), _uq_layout(w_uq)), tabs, hs)
    ckvn = rms_norm(ckv, kv_norm)
    wk, wv = _ukv_layout(w_ukv)
    kk = mm(ckvn, wk) + jnp.tile(rope(kpe, tabs, hs), (1, MLA_HEADS))
    o_mla = attn_mla(q, kk, mm(ckvn, wv), None)
    cum = fox_cum(fl, jnp.pad(b_f, (0, LANES - FOX_HEADS)).reshape(1, LANES))
    o_fox = attn_fox(fq, fk, fv, (cum, cum[:, :8].T))
    y = mm(gate_mul((o_mla, o_fox), gate), w_out)
    return ln_res(x, y, ln_g, ln_b)


def _odd_layer(x, w_in, sinks, w_out, ln_g, ln_b, tabs_swa):
    h = mm(x, _odd_w_in_layout(w_in))
    tabs, hs = tabs_swa
    q = rope(h[:, :1024], tabs, hs)
    kd = rope(h[:, 1024:1280], tabs, hs)
    vd = h[:, 1280:1536]
    gate = h[:, 1536:]
    o = attn_swa(q, kd, vd, jnp.broadcast_to(sinks[:, None], (SWA_HEADS, LANES)))
    y = mm(gate_mul((o,), gate), w_out)
    return ln_res(x, y, ln_g, ln_b)


def _local_loss(p, x, target):
    s = x.shape[0]
    tabs_mla = _rope_tables(s, "mla")
    tabs_swa = _rope_tables(s, "swa")
    for layer in range(DEPTH):
        j = layer // 2
        if layer % 2 == 0:
            x = _even_layer(x, p["even_w_in"][j], p["even_q_norm"][j], p["even_w_uq"][j], p["even_kv_norm"][j],
                            p["even_w_ukv"][j], p["even_b_f"][j], p["even_w_out"][j], p["even_ln_g"][j],
                            p["even_ln_b"][j], tabs_mla)
        else:
            x = _odd_layer(x, p["odd_w_in"][j], p["odd_sinks"][j], p["odd_w_out"][j], p["odd_ln_g"][j],
                           p["odd_ln_b"][j], tabs_swa)
    return mse_loss(x, target)


def _pad_rows(flat, mult):
    n = flat.shape[-1]
    per = mult * LANES
    padded = -(-n // per) * per
    if padded != n:
        flat = jnp.pad(flat, [(0, 0)] * (flat.ndim - 1) + [(0, padded - n)])
    return flat.reshape(flat.shape[:-1] + (padded // LANES, LANES))


def _to_shards(full, axis):
    shp = full.shape
    t = full.reshape(shp[:axis] + (N_DEV, shp[axis] // N_DEV) + shp[axis + 1:])
    return jnp.moveaxis(t, axis, 0).reshape(N_DEV, -1)


def _from_shards(rows, shard_shape, axis):
    t = jnp.moveaxis(rows.reshape((N_DEV,) + tuple(shard_shape)), 0, axis)
    shp = list(shard_shape)
    shp[axis] *= N_DEV
    return t.reshape(shp)


def kernel(x, even_w_in, even_q_norm, even_w_uq, even_kv_norm, even_w_ukv, even_b_f, even_w_out, even_ln_g, even_ln_b, odd_w_in, odd_sinks, odd_w_out, odd_ln_g, odd_ln_b, loss_target, m_even_w_in, m_even_q_norm, m_even_w_uq, m_even_kv_norm, m_even_w_ukv, m_even_b_f, m_even_w_out, m_even_ln_g, m_even_ln_b, m_odd_w_in, m_odd_sinks, m_odd_w_out, m_odd_ln_g, m_odd_ln_b, v_even_w_in, v_even_q_norm, v_even_w_uq, v_even_kv_norm, v_even_w_ukv, v_even_b_f, v_even_w_out, v_even_ln_g, v_even_ln_b, v_odd_w_in, v_odd_sinks, v_odd_w_out, v_odd_ln_g, v_odd_ln_b):
    w = dict(even_w_in=even_w_in, even_q_norm=even_q_norm, even_w_uq=even_w_uq, even_kv_norm=even_kv_norm,
             even_w_ukv=even_w_ukv, even_b_f=even_b_f, even_w_out=even_w_out, even_ln_g=even_ln_g, even_ln_b=even_ln_b,
             odd_w_in=odd_w_in, odd_sinks=odd_sinks, odd_w_out=odd_w_out, odd_ln_g=odd_ln_g, odd_ln_b=odd_ln_b)
    mom = dict(even_w_in=m_even_w_in, even_q_norm=m_even_q_norm, even_w_uq=m_even_w_uq, even_kv_norm=m_even_kv_norm,
               even_w_ukv=m_even_w_ukv, even_b_f=m_even_b_f, even_w_out=m_even_w_out, even_ln_g=m_even_ln_g,
               even_ln_b=m_even_ln_b, odd_w_in=m_odd_w_in, odd_sinks=m_odd_sinks, odd_w_out=m_odd_w_out,
               odd_ln_g=m_odd_ln_g, odd_ln_b=m_odd_ln_b)
    vel = dict(even_w_in=v_even_w_in, even_q_norm=v_even_q_norm, even_w_uq=v_even_w_uq, even_kv_norm=v_even_kv_norm,
               even_w_ukv=v_even_w_ukv, even_b_f=v_even_b_f, even_w_out=v_even_w_out, even_ln_g=v_even_ln_g,
               even_ln_b=v_even_ln_b, odd_w_in=v_odd_w_in, odd_sinks=v_odd_sinks, odd_w_out=v_odd_w_out,
               odd_ln_g=v_odd_ln_g, odd_ln_b=v_odd_ln_b)
    sharded = BIG + SMALL_SHARDED

    big_mine = _pad_rows(jnp.concatenate([w[n].astype(BF16).reshape(-1) for n in BIG]), 16)
    small_mine = _pad_rows(jnp.concatenate([w[n].reshape(-1) for n in SMALL_SHARDED]), 8)
    big_all = _all_gather(big_mine, "all_gather_big").reshape(N_DEV, -1)
    small_all = _all_gather(small_mine, "all_gather_small").reshape(N_DEV, -1)
    full = {}
    off = 0
    for n in BIG:
        size = math.prod(w[n].shape)
        full[n] = _from_shards(big_all[:, off:off + size], w[n].shape, SHARD_AXIS[n]).astype(F32)
        off += size
    off = 0
    for n in SMALL_SHARDED:
        size = math.prod(w[n].shape)
        full[n] = _from_shards(small_all[:, off:off + size], w[n].shape, SHARD_AXIS[n])
        off += size
    for n in REPL:
        full[n] = w[n]

    loss_local, (grads, grad_x) = jax.value_and_grad(_local_loss, argnums=(0, 1))(full, x[0], loss_target[0])
    loss = lax.psum(loss_local, AXES)

    repl_flat = jnp.concatenate([grads[n].reshape(-1) for n in REPL])
    parts = jnp.concatenate([_to_shards(grads[n], SHARD_AXIS[n]) for n in sharded]
                            + [jnp.broadcast_to(repl_flat[None, :], (N_DEV, repl_flat.shape[0]))], axis=1)
    recv = _exchange(_pad_rows(parts, 8), "grad_exchange")

    pack = lambda d: _pad_rows(jnp.concatenate([d[n].reshape(-1) for n in sharded + REPL]), 8)
    outs = _sum_adamw(recv, pack(w), pack(mom), pack(vel))
    unpacked = []
    for buf in outs:
        flat = buf.reshape(-1)
        d, off = {}, 0
        for n in sharded + REPL:
            size = math.prod(w[n].shape)
            d[n] = flat[off:off + size].reshape(w[n].shape)
            off += size
        unpacked.append(d)
    g_out, d_out, m_out, v_out = unpacked
    return (loss, grad_x[None], *[g_out[n] for n in WEIGHTS], *[d_out[n] for n in WEIGHTS],
            *[m_out[n] for n in WEIGHTS], *[v_out[n] for n in WEIGHTS])
```

```python
import functools
import math

import jax
import jax.numpy as jnp
from jax import lax
from jax.experimental import pallas as pl
from jax.experimental.pallas import tpu as pltpu

F32 = jnp.float32
BF16 = jnp.bfloat16
LANES = 128
HALF = 64
N_DEV = 8
AXES = ("x", "y", "c")
VMEM_LIMIT = 48 * 1024 * 1024

D_MODEL = 1024
DEPTH = 4
ROPE_THETA = 10000.0
MLA_HEADS, MLA_NOPE, MLA_ROPE, MLA_V, MLA_Q_RANK, MLA_KV_RANK = 8, 64, 32, 64, 256, 128
FOX_HEADS, FOX_DIM = 8, 64
SWA_HEADS, SWA_KV_HEADS, SWA_DIM, WINDOW = 16, 2, 64, 128
RMS_EPS, LN_EPS = 1e-6, 1e-5
ALPHA = (2 * DEPTH) ** 0.25
ADAM_LR, ADAM_B1, ADAM_B2, ADAM_EPS, ADAM_WD, ADAM_STEP = 0.001, 0.9, 0.999, 1e-08, 0.01, 10
NEG = -1e30

WEIGHTS = ["even_w_in", "even_q_norm", "even_w_uq", "even_kv_norm", "even_w_ukv", "even_b_f", "even_w_out",
           "even_ln_g", "even_ln_b", "odd_w_in", "odd_sinks", "odd_w_out", "odd_ln_g", "odd_ln_b"]
SHARD_AXIS = {"even_w_in": 2, "even_w_uq": 2, "even_w_ukv": 2, "even_w_out": 1, "odd_w_in": 2, "odd_w_out": 1,
              "odd_ln_g": 1, "odd_ln_b": 1, "even_q_norm": None, "even_kv_norm": None, "even_b_f": None,
              "even_ln_g": None, "even_ln_b": None, "odd_sinks": None}
BIG = ["even_w_in", "even_w_uq", "even_w_ukv", "even_w_out", "odd_w_in", "odd_w_out"]
SMALL_SHARDED = ["odd_ln_g", "odd_ln_b"]
REPL = [n for n in WEIGHTS if SHARD_AXIS[n] is None]


def _pick(n, cands):
    for c in cands:
        if n % c == 0:
            return c
    return n


def _params(**kw):
    return pltpu.CompilerParams(vmem_limit_bytes=VMEM_LIMIT, **kw)


def _me():
    return lax.axis_index("x"), lax.axis_index("y"), lax.axis_index("c")


def _peer(k):
    x, y, c = _me()
    px = 1 - x if (k >> 2) & 1 else x
    py = 1 - y if (k >> 1) & 1 else y
    pc = 1 - c if k & 1 else c
    return px, py, pc


def _lin(p):
    return 4 * p[0] + 2 * p[1] + p[2]


def _comm_call(body, n, out_shape, args, name):
    any_spec = pl.BlockSpec(memory_space=pl.ANY)
    return pl.pallas_call(
        body, name=name, out_shape=out_shape, in_specs=[any_spec] * n, out_specs=[any_spec] * n,
        scratch_shapes=[pltpu.SemaphoreType.DMA((n, N_DEV - 1)), pltpu.SemaphoreType.DMA((n, N_DEV - 1)),
                        pltpu.SemaphoreType.DMA((n,))],
    )(*args)


def _all_gather(xs, name):
    n = len(xs)

    def body(*refs):
        x_refs, out_refs = refs[:n], refs[n:2 * n]
        send_sems, recv_sems, local_sems = refs[2 * n:]
        me = _lin(_me())
        local = [pltpu.make_async_copy(x_refs[a], out_refs[a].at[me], local_sems.at[a]) for a in range(n)]
        for cp in local:
            cp.start()
        sends = []
        for k in range(1, N_DEV):
            for a in range(n):
                cp = pltpu.make_async_remote_copy(
                    src_ref=x_refs[a], dst_ref=out_refs[a].at[me], send_sem=send_sems.at[a, k - 1],
                    recv_sem=recv_sems.at[a, k - 1], device_id=_peer(k), device_id_type=pl.DeviceIdType.MESH)
                cp.start()
                sends.append(cp)
        for k in range(1, N_DEV):
            for a in range(n):
                pltpu.make_async_remote_copy(
                    src_ref=x_refs[a], dst_ref=out_refs[a].at[_lin(_peer(k))], send_sem=send_sems.at[a, k - 1],
                    recv_sem=recv_sems.at[a, k - 1], device_id=_peer(k),
                    device_id_type=pl.DeviceIdType.MESH).wait_recv()
        for cp in sends:
            cp.wait_send()
        for cp in local:
            cp.wait()

    out_shape = [jax.ShapeDtypeStruct((N_DEV,) + x.shape, x.dtype) for x in xs]
    return _comm_call(body, n, out_shape, xs, name)


def _exchange(parts, name):
    n = len(parts)

    def body(*refs):
        p_refs, out_refs = refs[:n], refs[n:2 * n]
        send_sems, recv_sems, local_sems = refs[2 * n:]
        me = _lin(_me())
        local = [pltpu.make_async_copy(p_refs[a].at[me], out_refs[a].at[me], local_sems.at[a]) for a in range(n)]
        for cp in local:
            cp.start()
        sends = []
        for k in range(1, N_DEV):
            peer = _peer(k)
            for a in range(n):
                cp = pltpu.make_async_remote_copy(
                    src_ref=p_refs[a].at[_lin(peer)], dst_ref=out_refs[a].at[me], send_sem=send_sems.at[a, k - 1],
                    recv_sem=recv_sems.at[a, k - 1], device_id=peer, device_id_type=pl.DeviceIdType.MESH)
                cp.start()
                sends.append(cp)
        for k in range(1, N_DEV):
            peer = _peer(k)
            for a in range(n):
                pltpu.make_async_remote_copy(
                    src_ref=p_refs[a].at[_lin(peer)], dst_ref=out_refs[a].at[_lin(peer)],
                    send_sem=send_sems.at[a, k - 1], recv_sem=recv_sems.at[a, k - 1], device_id=peer,
                    device_id_type=pl.DeviceIdType.MESH).wait_recv()
        for cp in sends:
            cp.wait_send()
        for cp in local:
            cp.wait()

    out_shape = [jax.ShapeDtypeStruct(p.shape, p.dtype) for p in parts]
    return _comm_call(body, n, out_shape, parts, name)


def _sum_adamw(recv, w, m, v):
    _, rows, lanes = recv.shape
    tr = _pick(rows, (256, 128, 64, 32, 16, 8))
    c1 = 1.0 - ADAM_B1 ** ADAM_STEP
    c2 = 1.0 - ADAM_B2 ** ADAM_STEP

    def body(r_ref, w_ref, m_ref, v_ref, g_out, d_out, m_out, v_out):
        g = r_ref[0].astype(F32)
        for s in range(1, N_DEV):
            g = g + r_ref[s].astype(F32)
        mn = ADAM_B1 * m_ref[...] + (1.0 - ADAM_B1) * g
        vn = ADAM_B2 * v_ref[...] + (1.0 - ADAM_B2) * (g * g)
        m_hat = mn / c1
        v_hat = vn / c2
        g_out[...] = g
        d_out[...] = -ADAM_LR * (m_hat / (jnp.sqrt(v_hat) + ADAM_EPS) + ADAM_WD * w_ref[...])
        m_out[...] = mn
        v_out[...] = vn

    blk = pl.BlockSpec((tr, lanes), lambda i: (i, 0))
    shp = jax.ShapeDtypeStruct((rows, lanes), F32)
    return pl.pallas_call(
        body, name=f"sum_adamw_{rows}x{lanes}", grid=(rows // tr,),
        in_specs=[pl.BlockSpec((N_DEV, tr, lanes), lambda i: (0, i, 0)), blk, blk, blk],
        out_specs=[blk, blk, blk, blk], out_shape=[shp, shp, shp, shp],
        compiler_params=_params(dimension_semantics=("parallel",)),
    )(recv, w, m, v)


def _mm_nn(a, b):
    m, k = a.shape
    _, n = b.shape
    tm = _pick(m, (512, 256, 128))
    tn = _pick(n, (640, 512, 256, 128))
    tk = _pick(k, (1024, 640, 512, 256, 128))
    nk = k // tk

    def body(a_ref, b_ref, o_ref, acc_ref):
        kk = pl.program_id(2)

        @pl.when(kk == 0)
        def _():
            acc_ref[...] = jnp.zeros_like(acc_ref)

        acc_ref[...] += jnp.dot(a_ref[...].astype(BF16), b_ref[...].astype(BF16), preferred_element_type=F32)

        @pl.when(kk == nk - 1)
        def _():
            o_ref[...] = acc_ref[...]

    return pl.pallas_call(
        body, name=f"mm_nn_{m}x{k}x{n}", grid=(m // tm, n // tn, nk),
        in_specs=[pl.BlockSpec((tm, tk), lambda i, j, kk: (i, kk)), pl.BlockSpec((tk, tn), lambda i, j, kk: (kk, j))],
        out_specs=pl.BlockSpec((tm, tn), lambda i, j, kk: (i, j)),
        out_shape=jax.ShapeDtypeStruct((m, n), F32),
        scratch_shapes=[pltpu.VMEM((tm, tn), F32)],
        compiler_params=_params(dimension_semantics=("parallel", "parallel", "arbitrary")),
    )(a, b)


def _mm_tn(a, g):
    s, k = a.shape
    _, n = g.shape
    tm = _pick(k, (512, 256, 128))
    tn = _pick(n, (640, 512, 256, 128))
    ts = _pick(s, (512, 256, 128))
    ns = s // ts

    def body(a_ref, g_ref, o_ref, acc_ref):
        ss = pl.program_id(2)

        @pl.when(ss == 0)
        def _():
            acc_ref[...] = jnp.zeros_like(acc_ref)

        acc_ref[...] += lax.dot_general(a_ref[...].astype(BF16), g_ref[...].astype(BF16),
                                        (((0,), (0,)), ((), ())), preferred_element_type=F32)

        @pl.when(ss == ns - 1)
        def _():
            o_ref[...] = acc_ref[...]

    return pl.pallas_call(
        body, name=f"mm_tn_{s}x{k}x{n}", grid=(k // tm, n // tn, ns),
        in_specs=[pl.BlockSpec((ts, tm), lambda i, j, ss: (ss, i)), pl.BlockSpec((ts, tn), lambda i, j, ss: (ss, j))],
        out_specs=pl.BlockSpec((tm, tn), lambda i, j, ss: (i, j)),
        out_shape=jax.ShapeDtypeStruct((k, n), F32),
        scratch_shapes=[pltpu.VMEM((tm, tn), F32)],
        compiler_params=_params(dimension_semantics=("parallel", "parallel", "arbitrary")),
    )(a, g)


@jax.custom_vjp
def mm(a, w):
    return _mm_nn(a, w.astype(BF16))


def _mm_fwd(a, w):
    wb = w.astype(BF16)
    return _mm_nn(a, wb), (a, wb)


def _mm_bwd(res, g):
    a, wb = res
    return _mm_nn(g, wb.T), _mm_tn(a, g)


mm.defvjp(_mm_fwd, _mm_bwd)


def _make_in_proj(widths):
    cuts = [sum(widths[:i]) for i in range(len(widths) + 1)]

    @jax.custom_vjp
    def in_proj(x, w):
        return fwd(x, w)[0]

    def fwd(x, w):
        xb, wb = x.astype(BF16), w.astype(BF16)
        return tuple(_mm_nn(xb, wb[:, a:b]) for a, b in zip(cuts[:-1], cuts[1:])), (xb, wb)

    def bwd(res, gs):
        xb, wb = res
        g = jnp.concatenate(gs, axis=1)
        return _mm_nn(g, wb.T), _mm_tn(xb, g)

    in_proj.defvjp(fwd, bwd)
    return in_proj


EVEN_GROUPS = (256, 128, 128, 512, 512, 512, 128, 1024)
ODD_GROUPS = (1024, 256, 256, 1024)
even_in_proj = _make_in_proj(EVEN_GROUPS)
odd_in_proj = _make_in_proj(ODD_GROUPS)

SHARD_PAD = 384


def _source_columns(kind):
    if kind == "even":
        src = [list(range(0, 384)), [-1] * 64, list(range(384, 416)), [-1] * 32, list(range(416, 1952)),
               list(range(1952, 1960)), [-1] * 120, list(range(1960, 2984))]
        return sum(src, []), 373
    q0, k0, v0, g0 = 0, 1024, 1152, 1280
    dup = lambda base: [base + 64 * g + c for g in range(SWA_KV_HEADS) for _ in range(2) for c in range(64)]
    return list(range(q0, k0)) + dup(k0) + dup(v0) + list(range(g0, 2304)), 288


def _selection(kind):
    src, shard = _source_columns(kind)
    cat = jnp.asarray([s + (SHARD_PAD - shard) * (s // shard) if s >= 0 else -1 for s in src], jnp.int32)
    rows = lax.broadcasted_iota(jnp.int32, (N_DEV * SHARD_PAD, len(src)), 0)
    return (rows == cat[None, :]).astype(BF16)


@functools.partial(jax.custom_vjp, nondiff_argnums=(1,))
def relayout(wcat, kind):
    return _mm_nn(wcat, _selection(kind))


def _relayout_bwd(kind, _, g):
    return (_mm_nn(g, _selection(kind).T),)


relayout.defvjp(lambda wcat, kind: (_mm_nn(wcat, _selection(kind)), None), _relayout_bwd)


def _row_block(s):
    return _pick(s, (512, 256, 128, 64, 32, 16, 8))


def _rms_fwd_call(x, g):
    s, k = x.shape
    tr = _row_block(s)

    def body(x_ref, g_ref, o_ref):
        xv = x_ref[...]
        r = lax.rsqrt(jnp.mean(xv * xv, axis=-1, keepdims=True) + RMS_EPS)
        o_ref[...] = xv * r * g_ref[...]

    return pl.pallas_call(
        body, name=f"rms_fwd_{k}", grid=(s // tr,),
        in_specs=[pl.BlockSpec((tr, k), lambda i: (i, 0)), pl.BlockSpec((1, k), lambda i: (0, 0))],
        out_specs=pl.BlockSpec((tr, k), lambda i: (i, 0)), out_shape=jax.ShapeDtypeStruct((s, k), F32),
        compiler_params=_params(dimension_semantics=("parallel",)),
    )(x, g.reshape(1, k))


def _rms_bwd_call(x, g, dy):
    s, k = x.shape
    tr = _row_block(s)

    def body(x_ref, g_ref, dy_ref, dx_ref, dg_ref):
        @pl.when(pl.program_id(0) == 0)
        def _():
            dg_ref[...] = jnp.zeros_like(dg_ref)

        xv = x_ref[...]
        r = lax.rsqrt(jnp.mean(xv * xv, axis=-1, keepdims=True) + RMS_EPS)
        xh = xv * r
        dyv = dy_ref[...]
        dg_ref[...] += jnp.sum(dyv * xh, axis=0, keepdims=True)
        dxh = dyv * g_ref[...]
        dx_ref[...] = r * (dxh - xh * jnp.mean(dxh * xh, axis=-1, keepdims=True))

    dx, dg = pl.pallas_call(
        body, name=f"rms_bwd_{k}", grid=(s // tr,),
        in_specs=[pl.BlockSpec((tr, k), lambda i: (i, 0)), pl.BlockSpec((1, k), lambda i: (0, 0)),
                  pl.BlockSpec((tr, k), lambda i: (i, 0))],
        out_specs=[pl.BlockSpec((tr, k), lambda i: (i, 0)), pl.BlockSpec((1, k), lambda i: (0, 0))],
        out_shape=[jax.ShapeDtypeStruct((s, k), F32), jax.ShapeDtypeStruct((1, k), F32)],
        compiler_params=_params(dimension_semantics=("arbitrary",)),
    )(x, g.reshape(1, k), dy)
    return dx, dg.reshape(k)


@jax.custom_vjp
def rms_norm(x, g):
    return _rms_fwd_call(x, g)


rms_norm.defvjp(lambda x, g: (_rms_fwd_call(x, g), (x, g)), lambda res, dy: _rms_bwd_call(res[0], res[1], dy))


def _ln_fwd_call(x, y, g, b):
    s, k = x.shape
    tr = _row_block(s)

    def body(x_ref, y_ref, g_ref, b_ref, o_ref):
        u = ALPHA * x_ref[...] + y_ref[...]
        mu = jnp.mean(u, axis=-1, keepdims=True)
        d = u - mu
        var = jnp.mean(d * d, axis=-1, keepdims=True)
        o_ref[...] = d * lax.rsqrt(var + LN_EPS) * g_ref[...] + b_ref[...]

    row = pl.BlockSpec((tr, k), lambda i: (i, 0))
    vec = pl.BlockSpec((1, k), lambda i: (0, 0))
    return pl.pallas_call(
        body, name="ln_fwd", grid=(s // tr,), in_specs=[row, row, vec, vec], out_specs=row,
        out_shape=jax.ShapeDtypeStruct((s, k), F32), compiler_params=_params(dimension_semantics=("parallel",)),
    )(x, y, g.reshape(1, k), b.reshape(1, k))


def _ln_bwd_call(x, y, g, do):
    s, k = x.shape
    tr = _row_block(s)

    def body(x_ref, y_ref, g_ref, do_ref, dx_ref, dy_ref, dg_ref, db_ref):
        @pl.when(pl.program_id(0) == 0)
        def _():
            dg_ref[...] = jnp.zeros_like(dg_ref)
            db_ref[...] = jnp.zeros_like(db_ref)

        u = ALPHA * x_ref[...] + y_ref[...]
        mu = jnp.mean(u, axis=-1, keepdims=True)
        d = u - mu
        r = lax.rsqrt(jnp.mean(d * d, axis=-1, keepdims=True) + LN_EPS)
        xh = d * r
        dov = do_ref[...]
        dg_ref[...] += jnp.sum(dov * xh, axis=0, keepdims=True)
        db_ref[...] += jnp.sum(dov, axis=0, keepdims=True)
        dxh = dov * g_ref[...]
        du = r * (dxh - jnp.mean(dxh, axis=-1, keepdims=True) - xh * jnp.mean(dxh * xh, axis=-1, keepdims=True))
        dy_ref[...] = du
        dx_ref[...] = ALPHA * du

    row = pl.BlockSpec((tr, k), lambda i: (i, 0))
    vec = pl.BlockSpec((1, k), lambda i: (0, 0))
    dx, dy, dg, db = pl.pallas_call(
        body, name="ln_bwd", grid=(s // tr,), in_specs=[row, row, vec, row], out_specs=[row, row, vec, vec],
        out_shape=[jax.ShapeDtypeStruct((s, k), F32), jax.ShapeDtypeStruct((s, k), F32),
                   jax.ShapeDtypeStruct((1, k), F32), jax.ShapeDtypeStruct((1, k), F32)],
        compiler_params=_params(dimension_semantics=("arbitrary",)),
    )(x, y, g.reshape(1, k), do)
    return dx, dy, dg.reshape(k), db.reshape(k)


@jax.custom_vjp
def ln_res(x, y, g, b):
    return _ln_fwd_call(x, y, g, b)


ln_res.defvjp(lambda x, y, g, b: (_ln_fwd_call(x, y, g, b), (x, y, g)),
              lambda res, do: _ln_bwd_call(res[0], res[1], res[2], do))


def _rope_call(x, c, s1, s2, hs):
    s, w = x.shape
    tr = _row_block(s)
    nb = w // LANES

    def body(x_ref, c_ref, s1_ref, s2_ref, o_ref):
        cv, s1v, s2v = c_ref[...], s1_ref[...], s2_ref[...]
        for cb in range(nb):
            xb = x_ref[:, cb * LANES:(cb + 1) * LANES]
            o_ref[:, cb * LANES:(cb + 1) * LANES] = (
                xb * cv + pltpu.roll(xb, LANES - hs, 1) * s1v + pltpu.roll(xb, hs, 1) * s2v)

    row = pl.BlockSpec((tr, w), lambda i: (i, 0))
    tab = pl.BlockSpec((tr, LANES), lambda i: (i, 0))
    return pl.pallas_call(
        body, name=f"rope_{w}_{hs}", grid=(s // tr,), in_specs=[row, tab, tab, tab], out_specs=row,
        out_shape=jax.ShapeDtypeStruct((s, w), F32), compiler_params=_params(dimension_semantics=("parallel",)),
    )(x, c, s1, s2)


@functools.partial(jax.custom_vjp, nondiff_argnums=(2,))
def rope(x, tabs, hs):
    return _rope_call(x, tabs[0], tabs[1], tabs[2], hs)


def _rope_fwd(x, tabs, hs):
    return _rope_call(x, tabs[0], tabs[1], tabs[2], hs), tabs


def _rope_bwd(hs, tabs, dy):
    return _rope_call(dy, tabs[0], -tabs[1], -tabs[2], hs), jax.tree.map(jnp.zeros_like, tabs)


rope.defvjp(_rope_fwd, _rope_bwd)


def _rope_tables(s, layout):
    pos = jnp.arange(s, dtype=F32)[:, None]
    lane = jnp.arange(LANES)
    if layout == "mla":
        dim, hs = MLA_ROPE, MLA_ROPE // 2
        r = lane - MLA_NOPE
        active = (r >= 0) & (r < MLA_ROPE)
    else:
        dim, hs = SWA_DIM, SWA_DIM // 2
        r = lane % SWA_DIM
        active = jnp.ones_like(lane, dtype=bool)
    f = jnp.where(active, r % hs, 0)
    inv = ROPE_THETA ** (-(2.0 * f.astype(F32)) / dim)
    ang = pos * inv[None, :]
    cos, sin = jnp.cos(ang), jnp.sin(ang)
    first = (active & (r < hs))[None, :]
    second = (active & (r >= hs))[None, :]
    c = jnp.where(active[None, :], cos, 1.0)
    s1 = jnp.where(first, -sin, 0.0)
    s2 = jnp.where(second, sin, 0.0)
    return (c, s1, s2), hs


def _gate_fwd_call(o_parts, gate):
    s, w = gate.shape
    tr = _row_block(s)
    widths = [o.shape[1] for o in o_parts]

    def body(*refs):
        o_refs, g_ref, z_ref = refs[:len(widths)], refs[len(widths)], refs[len(widths) + 1]
        off = 0
        for o_ref, wd in zip(o_refs, widths):
            gv = g_ref[:, off:off + wd]
            z_ref[:, off:off + wd] = o_ref[...] * (gv * jax.nn.sigmoid(gv))
            off += wd

    specs = [pl.BlockSpec((tr, wd), lambda i: (i, 0)) for wd in widths]
    row = pl.BlockSpec((tr, w), lambda i: (i, 0))
    return pl.pallas_call(
        body, name=f"gate_fwd_{len(widths)}", grid=(s // tr,), in_specs=specs + [row], out_specs=row,
        out_shape=jax.ShapeDtypeStruct((s, w), F32), compiler_params=_params(dimension_semantics=("parallel",)),
    )(*o_parts, gate)


def _gate_bwd_call(o_parts, gate, dz):
    s, w = gate.shape
    tr = _row_block(s)
    widths = [o.shape[1] for o in o_parts]
    n = len(widths)

    def body(*refs):
        o_refs, g_ref, dz_ref = refs[:n], refs[n], refs[n + 1]
        do_refs, dg_ref = refs[n + 2:2 * n + 2], refs[2 * n + 2]
        off = 0
        for o_ref, do_ref, wd in zip(o_refs, do_refs, widths):
            gv = g_ref[:, off:off + wd]
            sg = jax.nn.sigmoid(gv)
            dzv = dz_ref[:, off:off + wd]
            do_ref[...] = dzv * (gv * sg)
            dg_ref[:, off:off + wd] = dzv * o_ref[...] * (sg * (1.0 + gv * (1.0 - sg)))
            off += wd

    specs = [pl.BlockSpec((tr, wd), lambda i: (i, 0)) for wd in widths]
    row = pl.BlockSpec((tr, w), lambda i: (i, 0))
    outs = pl.pallas_call(
        body, name=f"gate_bwd_{n}", grid=(s // tr,), in_specs=specs + [row, row], out_specs=specs + [row],
        out_shape=[jax.ShapeDtypeStruct((s, wd), F32) for wd in widths] + [jax.ShapeDtypeStruct((s, w), F32)],
        compiler_params=_params(dimension_semantics=("parallel",)),
    )(*o_parts, gate, dz)
    return tuple(outs[:n]), outs[n]


@jax.custom_vjp
def gate_mul(o_parts, gate):
    return _gate_fwd_call(o_parts, gate)


gate_mul.defvjp(lambda o_parts, gate: (_gate_fwd_call(o_parts, gate), (o_parts, gate)),
                lambda res, dz: _gate_bwd_call(res[0], res[1], dz))


def _loss_call(y, t):
    s, k = y.shape
    tr = _row_block(s)
    nsteps = s // tr

    def body(y_ref, t_ref, l_ref, dy_ref, acc_ref):
        i = pl.program_id(0)

        @pl.when(i == 0)
        def _():
            acc_ref[...] = jnp.zeros_like(acc_ref)

        d = y_ref[...] - t_ref[...]
        dy_ref[...] = d / k
        acc_ref[...] += jnp.sum(d * d, axis=0, keepdims=True)

        @pl.when(i == nsteps - 1)
        def _():
            tot = jnp.sum(acc_ref[...], axis=1, keepdims=True) * (0.5 / k)
            l_ref[...] = jnp.broadcast_to(tot, l_ref.shape)

    row = pl.BlockSpec((tr, k), lambda i: (i, 0))
    return pl.pallas_call(
        body, name="loss", grid=(nsteps,), in_specs=[row, row],
        out_specs=[pl.BlockSpec((1, LANES), lambda i: (0, 0)), row],
        out_shape=[jax.ShapeDtypeStruct((1, LANES), F32), jax.ShapeDtypeStruct((s, k), F32)],
        scratch_shapes=[pltpu.VMEM((1, k), F32)], compiler_params=_params(dimension_semantics=("arbitrary",)),
    )(y, t)


@jax.custom_vjp
def mse_loss(y, t):
    return _loss_call(y, t)[0][0, 0]


def _mse_fwd(y, t):
    l, dy = _loss_call(y, t)
    return l[0, 0], (dy, t)


mse_loss.defvjp(_mse_fwd, lambda res, g: (g * res[0], jnp.zeros_like(res[1])))


def _scan_call(x, b, mode):
    s, w = x.shape
    nt = s // 8

    def tile_scan(t):
        row = lax.broadcasted_iota(jnp.int32, (8, w), 0)
        for sh in (1, 2, 4):
            t = t + jnp.where(row >= sh, pltpu.roll(t, sh, 0), 0.0)
        return t

    def body(x_ref, b_ref, o_ref):
        def step(i, carry):
            rows = pl.ds(pl.multiple_of(i * 8, 8), 8)
            t = x_ref[rows, :]
            if mode == "fwd":
                t = jax.nn.log_sigmoid(t + b_ref[...])
            t = tile_scan(t) + carry
            o_ref[rows, :] = t
            return t[7:8, :]

        total = lax.fori_loop(0, nt, step, jnp.zeros((1, w), F32))
        if mode == "rev":
            def fix(i, c):
                rows = pl.ds(pl.multiple_of(i * 8, 8), 8)
                o_ref[rows, :] = total - o_ref[rows, :] + x_ref[rows, :]
                return c
            lax.fori_loop(0, nt, fix, 0)

    full = pl.BlockSpec((s, w), lambda: (0, 0))
    return pl.pallas_call(
        body, name=f"scan_{mode}", in_specs=[full, pl.BlockSpec((1, w), lambda: (0, 0))], out_specs=full,
        out_shape=jax.ShapeDtypeStruct((s, w), F32), compiler_params=_params(),
    )(x, b)


def _fox_dlogit_call(x, b, dlogf):
    s, w = x.shape
    tr = _row_block(s)

    def body(x_ref, b_ref, d_ref, dx_ref, db_ref):
        @pl.when(pl.program_id(0) == 0)
        def _():
            db_ref[...] = jnp.zeros_like(db_ref)

        dx = d_ref[...] * jax.nn.sigmoid(-(x_ref[...] + b_ref[...]))
        dx_ref[...] = dx
        db_ref[...] += jnp.sum(dx, axis=0, keepdims=True)

    row = pl.BlockSpec((tr, w), lambda i: (i, 0))
    vec = pl.BlockSpec((1, w), lambda i: (0, 0))
    return pl.pallas_call(
        body, name="fox_dlogit", grid=(s // tr,), in_specs=[row, vec, row], out_specs=[row, vec],
        out_shape=[jax.ShapeDtypeStruct((s, w), F32), jax.ShapeDtypeStruct((1, w), F32)],
        compiler_params=_params(dimension_semantics=("arbitrary",)),
    )(x, b, dlogf)


@jax.custom_vjp
def fox_cum(fl, b):
    return _scan_call(fl, b, "fwd")


def _fox_cum_bwd(res, dcum):
    fl, b = res
    dlogf = _scan_call(dcum, b, "rev")
    return _fox_dlogit_call(fl, b, dlogf)


fox_cum.defvjp(lambda fl, b: (_scan_call(fl, b, "fwd"), (fl, b)), _fox_cum_bwd)


def _lane_col(x, lane_idx):
    lane = lax.broadcasted_iota(jnp.int32, (1, x.shape[1]), 1)
    return jnp.sum(jnp.where(lane == lane_idx, x, 0.0), axis=1, keepdims=True)


def _row_of(x, row_idx):
    row = lax.broadcasted_iota(jnp.int32, (x.shape[0], 1), 0)
    return jnp.sum(jnp.where(row == row_idx, x, 0.0), axis=0, keepdims=True)


def _attn_cfg(mode, s):
    if mode == "swa":
        blk = 256 if s >= 2048 else 128
        return dict(blk=blk, n_outer=SWA_KV_HEADS, pps=4, wide=False, scale=SWA_DIM ** -0.5)
    blk = 512 if s >= 2048 else 128
    if mode == "mla":
        return dict(blk=blk, n_outer=4, pps=1, wide=True, scale=(MLA_NOPE + MLA_ROPE) ** -0.5)
    return dict(blk=blk, n_outer=4, pps=1, wide=False, scale=FOX_DIM ** -0.5)


def _valid_mask(mode, i, jb, blk):
    qpos = i * blk + lax.broadcasted_iota(jnp.int32, (blk, blk), 0)
    kpos = jb * blk + lax.broadcasted_iota(jnp.int32, (blk, blk), 1)
    ok = kpos <= qpos
    if mode == "swa":
        ok = ok & (qpos - kpos < WINDOW)
    return ok


def _attn_fwd_call(mode, q, k, v, extra):
    s = q.shape[0]
    cfg = _attn_cfg(mode, s)
    blk, n_outer, pps, wide, scale = cfg["blk"], cfg["n_outer"], cfg["pps"], cfg["wide"], cfg["scale"]
    nq = s // blk
    swa, fox = mode == "swa", mode == "fox"
    qw = (2 * LANES if wide else LANES) * pps
    kw = 2 * LANES if wide else LANES
    ow = LANES * pps
    reps = blk // LANES

    def body(*refs):
        q_ref, k_ref, v_ref = refs[:3]
        n_in = 3
        if fox:
            cum_ref, cumt_ref = refs[3:5]
            n_in = 5
        if swa:
            sink_ref = refs[3]
            n_in = 4
        o_ref, lse_ref, m_ref, l_ref, acc_ref = refs[n_in:]
        p_id, i, j = pl.program_id(0), pl.program_id(1), pl.program_id(2)
        if swa:
            jb, run, first, last = i - 1 + j, (i - 1 + j) >= 0, j == 0, j == 1
        else:
            jb, run, first, last = j, j <= i, j == 0, j == i
        lane = lax.broadcasted_iota(jnp.int32, (1, LANES), 1)
        msk = [lane < HALF, lane >= HALF]

        @pl.when(first)
        def _():
            for hh in range(2 * pps):
                if swa:
                    m_ref[hh] = jnp.broadcast_to(sink_ref[hh:hh + 1, :], (blk, LANES))
                    l_ref[hh] = jnp.ones((blk, LANES), F32)
                else:
                    m_ref[hh] = jnp.full((blk, LANES), NEG, F32)
                    l_ref[hh] = jnp.zeros((blk, LANES), F32)
            acc_ref[...] = jnp.zeros_like(acc_ref)

        @pl.when(run)
        def _():
            ok = _valid_mask(mode, i, jb, blk)
            for pp in range(pps):
                vb = v_ref[...]
                pvs, alphas = [], []
                for h in range(2):
                    hh = 2 * pp + h
                    if wide:
                        qh = q_ref[:, h * LANES:(h + 1) * LANES]
                        kh = k_ref[:, h * LANES:(h + 1) * LANES]
                    else:
                        qh = jnp.where(msk[h], q_ref[:, pp * LANES:(pp + 1) * LANES], 0.0)
                        kh = k_ref[...]
                    sc = lax.dot_general(qh.astype(BF16), kh.astype(BF16), (((1,), (1,)), ((), ())),
                                         preferred_element_type=F32) * scale
                    if fox:
                        head = 2 * p_id + h
                        sc = sc + _lane_col(cum_ref[...], head) - _row_of(cumt_ref[...], head)
                    sc = jnp.where(ok, sc, NEG)
                    m_prev, l_prev = m_ref[hh], l_ref[hh]
                    m_next = jnp.maximum(m_prev, jnp.max(sc, axis=1, keepdims=True))
                    p = jnp.exp(sc - jnp.tile(m_next, (1, reps)))
                    alpha = jnp.exp(m_prev - m_next)
                    l_ref[hh] = alpha * l_prev + jnp.sum(p, axis=1, keepdims=True)
                    m_ref[hh] = m_next
                    vh = jnp.where(msk[h], vb, 0.0).astype(BF16)
                    pvs.append(jnp.dot(p.astype(BF16), vh, preferred_element_type=F32))
                    alphas.append(alpha)
                acc_ref[pp] = acc_ref[pp] * jnp.where(msk[0], alphas[0], alphas[1]) + pvs[0] + pvs[1]

        @pl.when(last)
        def _():
            for pp in range(pps):
                l0, l1 = l_ref[2 * pp], l_ref[2 * pp + 1]
                o_ref[:, pp * LANES:(pp + 1) * LANES] = acc_ref[pp] / jnp.where(msk[0], l0, l1)
                lse_ref[:, pp * LANES:(pp + 1) * LANES] = jnp.where(
                    msk[0], m_ref[2 * pp] + jnp.log(l0), m_ref[2 * pp + 1] + jnp.log(l1))

    if swa:
        kv_map = lambda g, i, j: (jnp.maximum(i - 1 + j, 0), g)
        grid = (n_outer, nq, 2)
    else:
        kv_map = lambda p, i, j: (jnp.minimum(j, i), p)
        grid = (n_outer, nq, nq)
    q_map = lambda p, i, j: (i, p)
    in_specs = [pl.BlockSpec((blk, qw), q_map), pl.BlockSpec((blk, kw), kv_map), pl.BlockSpec((blk, LANES), kv_map)]
    args = [q, k, v]
    if fox:
        cum, cumt = extra
        in_specs += [pl.BlockSpec((blk, LANES), lambda p, i, j: (i, 0)),
                     pl.BlockSpec((8, blk), lambda p, i, j: (0, jnp.minimum(j, i)))]
        args += [cum, cumt]
    if swa:
        in_specs += [pl.BlockSpec((8, LANES), lambda g, i, j: (g, 0))]
        args += [extra]
    n_pairs = n_outer * pps
    return pl.pallas_call(
        body, name=f"attn_fwd_{mode}", grid=grid, in_specs=in_specs,
        out_specs=[pl.BlockSpec((blk, ow), q_map), pl.BlockSpec((blk, ow), q_map)],
        out_shape=[jax.ShapeDtypeStruct((s, n_pairs * LANES), F32), jax.ShapeDtypeStruct((s, n_pairs * LANES), F32)],
        scratch_shapes=[pltpu.VMEM((2 * pps, blk, LANES), F32), pltpu.VMEM((2 * pps, blk, LANES), F32),
                        pltpu.VMEM((pps, blk, LANES), F32)],
        compiler_params=_params(dimension_semantics=("parallel", "parallel", "arbitrary")),
    )(*args)


def _rowdot_call(do, o):
    s, w = o.shape
    tr = _row_block(s)

    def body(do_ref, o_ref, d_ref):
        lane = lax.broadcasted_iota(jnp.int32, (1, LANES), 1)
        low = lane < HALF
        for cb in range(w // LANES):
            sl = slice(cb * LANES, (cb + 1) * LANES)
            prod = do_ref[:, sl] * o_ref[:, sl]
            d0 = jnp.sum(jnp.where(low, prod, 0.0), axis=1, keepdims=True)
            d1 = jnp.sum(jnp.where(low, 0.0, prod), axis=1, keepdims=True)
            d_ref[:, sl] = jnp.where(low, d0, d1)

    row = pl.BlockSpec((tr, w), lambda i: (i, 0))
    return pl.pallas_call(
        body, name=f"rowdot_{w}", grid=(s // tr,), in_specs=[row, row], out_specs=row,
        out_shape=jax.ShapeDtypeStruct((s, w), F32), compiler_params=_params(dimension_semantics=("parallel",)),
    )(do, o)


def _attn_bwd_call(mode, q, k, v, extra, lse, dd, do):
    s = q.shape[0]
    cfg = _attn_cfg(mode, s)
    blk, n_outer, pps, wide, scale = cfg["blk"], cfg["n_outer"], cfg["pps"], cfg["wide"], cfg["scale"]
    nq = s // blk
    swa, fox = mode == "swa", mode == "fox"
    qw = (2 * LANES if wide else LANES) * pps
    kw = 2 * LANES if wide else LANES
    ow = LANES * pps
    reps = blk // LANES

    def body(*refs):
        q_ref, k_ref, v_ref, lse_ref, dd_ref, do_ref = refs[:6]
        n_in = 6
        if fox:
            cum_ref, cumt_ref = refs[6:8]
            n_in = 8
        if swa:
            sink_ref = refs[6]
            n_in = 7
        dq_ref, dk_ref, dv_ref = refs[n_in:n_in + 3]
        n_out = n_in + 3
        if fox:
            dck_ref, dcq_ref = refs[n_out:n_out + 2]
            n_out += 2
        if swa:
            dsink_ref = refs[n_out]
            n_out += 1
        dk_acc, dv_acc = refs[n_out:n_out + 2]
        if fox:
            dck_acc = refs[n_out + 2]
        p_id, j, ii = pl.program_id(0), pl.program_id(1), pl.program_id(2)
        if swa:
            i, run, first_i, last_i = j + ii, (j + ii) < nq, ii == 0, ii == 1
        else:
            i, run, first_i, last_i = ii, ii >= j, ii == j, ii == nq - 1
        lane = lax.broadcasted_iota(jnp.int32, (1, LANES), 1)
        msk = [lane < HALF, lane >= HALF]

        @pl.when((j == 0) & (ii == 0))
        def _():
            dq_ref[...] = jnp.zeros_like(dq_ref)
            if swa:
                dsink_ref[...] = jnp.zeros_like(dsink_ref)
            if fox:
                dcq_ref[...] = jnp.zeros_like(dcq_ref)

        @pl.when(first_i)
        def _():
            dk_acc[...] = jnp.zeros_like(dk_acc)
            dv_acc[...] = jnp.zeros_like(dv_acc)
            if fox:
                dck_acc[...] = jnp.zeros_like(dck_acc)

        @pl.when(run)
        def _():
            ok = _valid_mask(mode, i, j, blk)
            rows = pl.ds(pl.multiple_of(i * blk, blk), blk)
            vb = v_ref[...].astype(BF16)
            for pp in range(pps):
                psl = slice(pp * LANES, (pp + 1) * LANES)
                lse_blk, dd_blk, do_blk = lse_ref[:, psl], dd_ref[:, psl], do_ref[:, psl]
                for h in range(2):
                    hh = 2 * pp + h
                    if wide:
                        hsl = slice(h * LANES, (h + 1) * LANES)
                        qh = q_ref[:, hsl].astype(BF16)
                        kh = k_ref[:, hsl].astype(BF16)
                    else:
                        qh = jnp.where(msk[h], q_ref[:, psl], 0.0).astype(BF16)
                        kh = k_ref[...].astype(BF16)
                    sc = lax.dot_general(qh, kh, (((1,), (1,)), ((), ())), preferred_element_type=F32) * scale
                    if fox:
                        head = 2 * p_id + h
                        sc = sc + _lane_col(cum_ref[...], head) - _row_of(cumt_ref[...], head)
                    lse_h = _lane_col(lse_blk, HALF * h)
                    d_h = _lane_col(dd_blk, HALF * h)
                    p = jnp.where(ok, jnp.exp(sc - lse_h), 0.0)
                    do_h = jnp.where(msk[h], do_blk, 0.0).astype(BF16)
                    dv_acc[...] += lax.dot_general(p.astype(BF16), do_h, (((0,), (0,)), ((), ())),
                                                   preferred_element_type=F32)
                    dp = lax.dot_general(do_h, vb, (((1,), (1,)), ((), ())), preferred_element_type=F32)
                    ds = p * (dp - d_h)
                    if fox:
                        dck_acc[h:h + 1, :] += -jnp.sum(ds, axis=0, keepdims=True)
                        dcq_ref[rows, :] += jnp.where(msk[h], jnp.sum(ds, axis=1, keepdims=True), 0.0)
                    dsb = (ds * scale).astype(BF16)
                    dq_h = jnp.dot(dsb, kh, preferred_element_type=F32)
                    dk_h = lax.dot_general(dsb, qh, (((0,), (0,)), ((), ())), preferred_element_type=F32)
                    if wide:
                        dq_ref[rows, hsl] += dq_h
                        dk_acc[:, hsl] += dk_h
                    else:
                        dq_ref[rows, psl] += jnp.where(msk[h], dq_h, 0.0)
                        dk_acc[...] += dk_h
                    if swa:
                        @pl.when(first_i)
                        def _():
                            e = jnp.exp(sink_ref[hh:hh + 1, :] - lse_h) * d_h
                            dsink_ref[hh:hh + 1, :] += -jnp.sum(e, axis=0, keepdims=True)

        @pl.when(last_i)
        def _():
            dk_ref[...] = dk_acc[...]
            dv_ref[...] = dv_acc[...]
            if fox:
                dck_ref[0] = dck_acc[...]

    if swa:
        q_map = lambda g, j, ii: (jnp.minimum(j + ii, nq - 1), g)
        grid = (n_outer, nq, 2)
    else:
        q_map = lambda p, j, ii: (jnp.maximum(ii, j), p)
        grid = (n_outer, nq, nq)
    kv_map = lambda p, j, ii: (j, p)
    in_specs = [pl.BlockSpec((blk, qw), q_map), pl.BlockSpec((blk, kw), kv_map), pl.BlockSpec((blk, LANES), kv_map),
                pl.BlockSpec((blk, ow), q_map), pl.BlockSpec((blk, ow), q_map), pl.BlockSpec((blk, ow), q_map)]
    args = [q, k, v, lse, dd, do]
    n_pairs = n_outer * pps
    out_specs = [pl.BlockSpec((s, qw), lambda p, j, ii: (0, p)), pl.BlockSpec((blk, kw), kv_map),
                 pl.BlockSpec((blk, LANES), kv_map)]
    out_shape = [jax.ShapeDtypeStruct((s, q.shape[1]), F32), jax.ShapeDtypeStruct((s, k.shape[1]), F32),
                 jax.ShapeDtypeStruct((s, v.shape[1]), F32)]
    scratch = [pltpu.VMEM((blk, kw), F32), pltpu.VMEM((blk, LANES), F32)]
    if fox:
        cum, cumt = extra
        if swa:
            raise AssertionError
        in_specs += [pl.BlockSpec((blk, LANES), lambda p, j, ii: (jnp.maximum(ii, j), 0)),
                     pl.BlockSpec((8, blk), lambda p, j, ii: (0, j))]
        args += [cum, cumt]
        out_specs += [pl.BlockSpec((1, 8, blk), lambda p, j, ii: (p, 0, j)),
                      pl.BlockSpec((s, LANES), lambda p, j, ii: (0, p))]
        out_shape += [jax.ShapeDtypeStruct((n_pairs, 8, s), F32), jax.ShapeDtypeStruct((s, n_pairs * LANES), F32)]
        scratch += [pltpu.VMEM((8, blk), F32)]
    if swa:
        in_specs += [pl.BlockSpec((8, LANES), lambda g, j, ii: (g, 0))]
        args += [extra]
        out_specs += [pl.BlockSpec((8, LANES), lambda g, j, ii: (g, 0))]
        out_shape += [jax.ShapeDtypeStruct((SWA_HEADS, LANES), F32)]
    return pl.pallas_call(
        body, name=f"attn_bwd_{mode}", grid=grid, in_specs=in_specs, out_specs=out_specs, out_shape=out_shape,
        scratch_shapes=scratch,
        compiler_params=_params(dimension_semantics=("parallel", "arbitrary", "arbitrary")),
    )(*args)


def _make_attn(mode):
    @jax.custom_vjp
    def attn(q, k, v, extra):
        return _attn_fwd_call(mode, q, k, v, extra)[0]

    def fwd(q, k, v, extra):
        o, lse = _attn_fwd_call(mode, q, k, v, extra)
        return o, (q, k, v, extra, o, lse)

    def bwd(res, do):
        q, k, v, extra, o, lse = res
        dd = _rowdot_call(do, o)
        outs = _attn_bwd_call(mode, q, k, v, extra, lse, dd, do)
        dq, dk, dv = outs[:3]
        if mode == "fox":
            cum, cumt = extra
            dck = outs[3]
            dcumt = dck[:, :2, :].reshape(FOX_HEADS, -1)
            dcq = outs[4].reshape(-1, FOX_HEADS, HALF)[:, :, 0]
            dextra = (jnp.pad(dcq, ((0, 0), (0, LANES - FOX_HEADS))), dcumt)
        elif mode == "swa":
            dextra = jnp.where(jnp.arange(LANES)[None, :] == 0, outs[3], 0.0)
        else:
            dextra = None
        return dq, dk, dv, dextra

    attn.defvjp(fwd, bwd)
    return attn


attn_mla = _make_attn("mla")
attn_fox = _make_attn("fox")
attn_swa = _make_attn("swa")


def _ukv_layout(w):
    r = w.shape[0]
    w3 = w.reshape(r, MLA_HEADS, MLA_NOPE + MLA_V)
    wk = jnp.pad(w3[:, :, :MLA_NOPE], ((0, 0), (0, 0), (0, LANES - MLA_NOPE))).reshape(r, MLA_HEADS * LANES)
    wv = w3[:, :, MLA_NOPE:].reshape(r, MLA_HEADS * MLA_V)
    return wk, wv


def _even_layer(x, w_in_cat, q_norm, w_uq_p, kv_norm, w_ukv, b_f, w_out, ln_g, ln_b, tabs_mla):
    cq, ckv, kpe, fq, fk, fv, fl, gate = even_in_proj(x, relayout(w_in_cat, "even"))
    tabs, hs = tabs_mla
    q = rope(mm(rms_norm(cq, q_norm), w_uq_p), tabs, hs)
    ckvn = rms_norm(ckv, kv_norm)
    wk, wv = _ukv_layout(w_ukv)
    kk = mm(ckvn, wk) + jnp.tile(rope(kpe, tabs, hs), (1, MLA_HEADS))
    o_mla = attn_mla(q, kk, mm(ckvn, wv), None)
    cum = fox_cum(fl, jnp.pad(b_f, (0, LANES - FOX_HEADS)).reshape(1, LANES))
    o_fox = attn_fox(fq, fk, fv, (cum, cum[:, :8].T))
    y = mm(gate_mul((o_mla, o_fox), gate), w_out)
    return ln_res(x, y, ln_g, ln_b)


def _odd_layer(x, w_in_cat, sinks, w_out, ln_g, ln_b, tabs_swa):
    q, kd, vd, gate = odd_in_proj(x, relayout(w_in_cat, "odd"))
    tabs, hs = tabs_swa
    q = rope(q, tabs, hs)
    kd = rope(kd, tabs, hs)
    o = attn_swa(q, kd, vd, jnp.broadcast_to(sinks[:, None], (SWA_HEADS, LANES)))
    y = mm(gate_mul((o,), gate), w_out)
    return ln_res(x, y, ln_g, ln_b)


def _local_loss(p, x, target):
    s = x.shape[0]
    tabs_mla = _rope_tables(s, "mla")
    tabs_swa = _rope_tables(s, "swa")
    for layer in range(DEPTH):
        j = layer // 2
        if layer % 2 == 0:
            x = _even_layer(x, p["even_w_in"][j], p["even_q_norm"][j], p["even_w_uq"][j], p["even_kv_norm"][j],
                            p["even_w_ukv"][j], p["even_b_f"][j], p["even_w_out"][j], p["even_ln_g"][j],
                            p["even_ln_b"][j], tabs_mla)
        else:
            x = _odd_layer(x, p["odd_w_in"][j], p["odd_sinks"][j], p["odd_w_out"][j], p["odd_ln_g"][j],
                           p["odd_ln_b"][j], tabs_swa)
    return mse_loss(x, target)


def _pad_rows(flat, mult):
    n = flat.shape[-1]
    per = mult * LANES
    padded = -(-n // per) * per
    if padded != n:
        flat = jnp.pad(flat, [(0, 0)] * (flat.ndim - 1) + [(0, padded - n)])
    return flat.reshape(flat.shape[:-1] + (padded // LANES, LANES))


def _pad_last(a, width):
    if a.shape[-1] == width:
        return a
    return jnp.pad(a, [(0, 0)] * (a.ndim - 1) + [(0, width - a.shape[-1])])


def _join(slots, axis):
    shp = list(slots.shape[1:])
    shp[axis] *= N_DEV
    return jnp.moveaxis(slots, 0, axis).reshape(shp)


def _split(full, axis):
    shp = full.shape
    t = full.reshape(shp[:axis] + (N_DEV, shp[axis] // N_DEV) + shp[axis + 1:])
    return jnp.moveaxis(t, axis, 0)


PAD_TO = {"even_w_in": SHARD_PAD, "even_w_uq": LANES, "odd_w_in": SHARD_PAD}


def kernel(x, even_w_in, even_q_norm, even_w_uq, even_kv_norm, even_w_ukv, even_b_f, even_w_out, even_ln_g, even_ln_b, odd_w_in, odd_sinks, odd_w_out, odd_ln_g, odd_ln_b, loss_target, m_even_w_in, m_even_q_norm, m_even_w_uq, m_even_kv_norm, m_even_w_ukv, m_even_b_f, m_even_w_out, m_even_ln_g, m_even_ln_b, m_odd_w_in, m_odd_sinks, m_odd_w_out, m_odd_ln_g, m_odd_ln_b, v_even_w_in, v_even_q_norm, v_even_w_uq, v_even_kv_norm, v_even_w_ukv, v_even_b_f, v_even_w_out, v_even_ln_g, v_even_ln_b, v_odd_w_in, v_odd_sinks, v_odd_w_out, v_odd_ln_g, v_odd_ln_b):
    w = dict(even_w_in=even_w_in, even_q_norm=even_q_norm, even_w_uq=even_w_uq, even_kv_norm=even_kv_norm,
             even_w_ukv=even_w_ukv, even_b_f=even_b_f, even_w_out=even_w_out, even_ln_g=even_ln_g, even_ln_b=even_ln_b,
             odd_w_in=odd_w_in, odd_sinks=odd_sinks, odd_w_out=odd_w_out, odd_ln_g=odd_ln_g, odd_ln_b=odd_ln_b)
    mom = dict(even_w_in=m_even_w_in, even_q_norm=m_even_q_norm, even_w_uq=m_even_w_uq, even_kv_norm=m_even_kv_norm,
               even_w_ukv=m_even_w_ukv, even_b_f=m_even_b_f, even_w_out=m_even_w_out, even_ln_g=m_even_ln_g,
               even_ln_b=m_even_ln_b, odd_w_in=m_odd_w_in, odd_sinks=m_odd_sinks, odd_w_out=m_odd_w_out,
               odd_ln_g=m_odd_ln_g, odd_ln_b=m_odd_ln_b)
    vel = dict(even_w_in=v_even_w_in, even_q_norm=v_even_q_norm, even_w_uq=v_even_w_uq, even_kv_norm=v_even_kv_norm,
               even_w_ukv=v_even_w_ukv, even_b_f=v_even_b_f, even_w_out=v_even_w_out, even_ln_g=v_even_ln_g,
               even_ln_b=v_even_ln_b, odd_w_in=v_odd_w_in, odd_sinks=v_odd_sinks, odd_w_out=v_odd_w_out,
               odd_ln_g=v_odd_ln_g, odd_ln_b=v_odd_ln_b)
    sharded = BIG + SMALL_SHARDED
    padded = lambda d, n: _pad_last(d[n], PAD_TO.get(n, d[n].shape[-1]))

    mine = [padded(w, n).astype(BF16) for n in BIG] + [w[n] for n in SMALL_SHARDED]
    gathered = _all_gather(mine, "all_gather")
    full = {n: _join(g, SHARD_AXIS[n]).astype(F32) for n, g in zip(sharded, gathered)}
    for n in REPL:
        full[n] = w[n]

    loss_local, (grads, grad_x) = jax.value_and_grad(_local_loss, argnums=(0, 1))(full, x[0], loss_target[0])
    loss = lax.psum(loss_local, AXES)

    repl_rows = _pad_rows(jnp.concatenate([grads[n].reshape(-1) for n in REPL]), 8)
    parts = [_split(grads[n], SHARD_AXIS[n]).astype(BF16 if n in BIG else F32) for n in sharded]
    parts.append(jnp.broadcast_to(repl_rows[None], (N_DEV,) + repl_rows.shape))
    recv = _exchange(parts, "grad_exchange")

    g_out, d_out, m_out, v_out = {}, {}, {}, {}
    for n, r in zip(sharded, recv[:-1]):
        cols = r.shape[-1]
        flat2 = lambda d: padded(d, n).reshape(-1, cols)
        outs = _sum_adamw(r.reshape(N_DEV, -1, cols), flat2(w), flat2(mom), flat2(vel))
        for dst, o in zip((g_out, d_out, m_out, v_out), outs):
            dst[n] = o.reshape(w[n].shape[:-1] + (cols,))[..., :w[n].shape[-1]]
    pack = lambda d: _pad_rows(jnp.concatenate([d[n].reshape(-1) for n in REPL]), 8)
    outs = _sum_adamw(recv[-1], pack(w), pack(mom), pack(vel))
    for dst, o in zip((g_out, d_out, m_out, v_out), outs):
        flat, off = o.reshape(-1), 0
        for n in REPL:
            size = math.prod(w[n].shape)
            dst[n] = flat[off:off + size].reshape(w[n].shape)
            off += size
    return (loss, grad_x[None], *[g_out[n] for n in WEIGHTS], *[d_out[n] for n in WEIGHTS],
            *[m_out[n] for n in WEIGHTS], *[v_out[n] for n in WEIGHTS])
```

```python
import functools
import math

import jax
import jax.numpy as jnp
from jax import lax
from jax.experimental import pallas as pl
from jax.experimental.pallas import tpu as pltpu

F32 = jnp.float32
BF16 = jnp.bfloat16
LANES = 128
HALF = 64
N_DEV = 8
AXES = ("x", "y", "c")
VMEM_LIMIT = 48 * 1024 * 1024

D_MODEL = 1024
DEPTH = 4
ROPE_THETA = 10000.0
MLA_HEADS, MLA_NOPE, MLA_ROPE, MLA_V, MLA_Q_RANK, MLA_KV_RANK = 8, 64, 32, 64, 256, 128
FOX_HEADS, FOX_DIM = 8, 64
SWA_HEADS, SWA_KV_HEADS, SWA_DIM, WINDOW = 16, 2, 64, 128
RMS_EPS, LN_EPS = 1e-6, 1e-5
ALPHA = (2 * DEPTH) ** 0.25
ADAM_LR, ADAM_B1, ADAM_B2, ADAM_EPS, ADAM_WD, ADAM_STEP = 0.001, 0.9, 0.999, 1e-08, 0.01, 10
NEG = -1e30

WEIGHTS = ["even_w_in", "even_q_norm", "even_w_uq", "even_kv_norm", "even_w_ukv", "even_b_f", "even_w_out",
           "even_ln_g", "even_ln_b", "odd_w_in", "odd_sinks", "odd_w_out", "odd_ln_g", "odd_ln_b"]
SHARD_AXIS = {"even_w_in": 2, "even_w_uq": 2, "even_w_ukv": 2, "even_w_out": 1, "odd_w_in": 2, "odd_w_out": 1,
              "odd_ln_g": 1, "odd_ln_b": 1, "even_q_norm": None, "even_kv_norm": None, "even_b_f": None,
              "even_ln_g": None, "even_ln_b": None, "odd_sinks": None}
BIG = ["even_w_in", "even_w_uq", "even_w_ukv", "even_w_out", "odd_w_in", "odd_w_out"]
SMALL_SHARDED = ["odd_ln_g", "odd_ln_b"]
REPL = [n for n in WEIGHTS if SHARD_AXIS[n] is None]


def _pick(n, cands):
    for c in cands:
        if n % c == 0:
            return c
    return n


def _params(**kw):
    return pltpu.CompilerParams(vmem_limit_bytes=VMEM_LIMIT, **kw)


def _me():
    return lax.axis_index("x"), lax.axis_index("y"), lax.axis_index("c")


def _peer(k):
    x, y, c = _me()
    px = 1 - x if (k >> 2) & 1 else x
    py = 1 - y if (k >> 1) & 1 else y
    pc = 1 - c if k & 1 else c
    return px, py, pc


def _lin(p):
    return 4 * p[0] + 2 * p[1] + p[2]


def _comm_call(body, n, out_shape, args, name):
    any_spec = pl.BlockSpec(memory_space=pl.ANY)
    return pl.pallas_call(
        body, name=name, out_shape=out_shape, in_specs=[any_spec] * n, out_specs=[any_spec] * n,
        scratch_shapes=[pltpu.SemaphoreType.DMA((n, N_DEV - 1)), pltpu.SemaphoreType.DMA((n, N_DEV - 1)),
                        pltpu.SemaphoreType.DMA((n,))],
    )(*args)


def _all_gather(xs, name):
    n = len(xs)

    def body(*refs):
        x_refs, out_refs = refs[:n], refs[n:2 * n]
        send_sems, recv_sems, local_sems = refs[2 * n:]
        me = _lin(_me())
        local = [pltpu.make_async_copy(x_refs[a], out_refs[a].at[me], local_sems.at[a]) for a in range(n)]
        for cp in local:
            cp.start()
        sends = []
        for k in range(1, N_DEV):
            for a in range(n):
                cp = pltpu.make_async_remote_copy(
                    src_ref=x_refs[a], dst_ref=out_refs[a].at[me], send_sem=send_sems.at[a, k - 1],
                    recv_sem=recv_sems.at[a, k - 1], device_id=_peer(k), device_id_type=pl.DeviceIdType.MESH)
                cp.start()
                sends.append(cp)
        for k in range(1, N_DEV):
            for a in range(n):
                pltpu.make_async_remote_copy(
                    src_ref=x_refs[a], dst_ref=out_refs[a].at[_lin(_peer(k))], send_sem=send_sems.at[a, k - 1],
                    recv_sem=recv_sems.at[a, k - 1], device_id=_peer(k),
                    device_id_type=pl.DeviceIdType.MESH).wait_recv()
        for cp in sends:
            cp.wait_send()
        for cp in local:
            cp.wait()

    out_shape = [jax.ShapeDtypeStruct((N_DEV,) + x.shape, x.dtype) for x in xs]
    return _comm_call(body, n, out_shape, xs, name)


def _exchange(parts, name):
    n = len(parts)

    def body(*refs):
        p_refs, out_refs = refs[:n], refs[n:2 * n]
        send_sems, recv_sems, local_sems = refs[2 * n:]
        me = _lin(_me())
        local = [pltpu.make_async_copy(p_refs[a].at[me], out_refs[a].at[me], local_sems.at[a]) for a in range(n)]
        for cp in local:
            cp.start()
        sends = []
        for k in range(1, N_DEV):
            peer = _peer(k)
            for a in range(n):
                cp = pltpu.make_async_remote_copy(
                    src_ref=p_refs[a].at[_lin(peer)], dst_ref=out_refs[a].at[me], send_sem=send_sems.at[a, k - 1],
                    recv_sem=recv_sems.at[a, k - 1], device_id=peer, device_id_type=pl.DeviceIdType.MESH)
                cp.start()
                sends.append(cp)
        for k in range(1, N_DEV):
            peer = _peer(k)
            for a in range(n):
                pltpu.make_async_remote_copy(
                    src_ref=p_refs[a].at[_lin(peer)], dst_ref=out_refs[a].at[_lin(peer)],
                    send_sem=send_sems.at[a, k - 1], recv_sem=recv_sems.at[a, k - 1], device_id=peer,
                    device_id_type=pl.DeviceIdType.MESH).wait_recv()
        for cp in sends:
            cp.wait_send()
        for cp in local:
            cp.wait()

    out_shape = [jax.ShapeDtypeStruct(p.shape, p.dtype) for p in parts]
    return _comm_call(body, n, out_shape, parts, name)


def _sum_adamw(recv, w, m, v):
    _, rows, lanes = recv.shape
    tr = _pick(rows, (256, 128, 64, 32, 16, 8))
    c1 = 1.0 - ADAM_B1 ** ADAM_STEP
    c2 = 1.0 - ADAM_B2 ** ADAM_STEP

    def body(r_ref, w_ref, m_ref, v_ref, g_out, d_out, m_out, v_out):
        g = r_ref[0].astype(F32)
        for s in range(1, N_DEV):
            g = g + r_ref[s].astype(F32)
        mn = ADAM_B1 * m_ref[...] + (1.0 - ADAM_B1) * g
        vn = ADAM_B2 * v_ref[...] + (1.0 - ADAM_B2) * (g * g)
        m_hat = mn / c1
        v_hat = vn / c2
        g_out[...] = g
        d_out[...] = -ADAM_LR * (m_hat / (jnp.sqrt(v_hat) + ADAM_EPS) + ADAM_WD * w_ref[...])
        m_out[...] = mn
        v_out[...] = vn

    blk = pl.BlockSpec((tr, lanes), lambda i: (i, 0))
    shp = jax.ShapeDtypeStruct((rows, lanes), F32)
    return pl.pallas_call(
        body, name=f"sum_adamw_{rows}x{lanes}", grid=(rows // tr,),
        in_specs=[pl.BlockSpec((N_DEV, tr, lanes), lambda i: (0, i, 0)), blk, blk, blk],
        out_specs=[blk, blk, blk, blk], out_shape=[shp, shp, shp, shp],
        compiler_params=_params(dimension_semantics=("parallel",)),
    )(recv, w, m, v)


def _mm_nn(a, b):
    m, k = a.shape
    _, n = b.shape
    tm = _pick(m, (1024, 512, 256, 128))
    tn = _pick(n, (1024, 640, 512, 256, 128))
    tk = _pick(k, (1024, 640, 512, 256, 128))
    nk = k // tk

    def body(a_ref, b_ref, o_ref, acc_ref):
        kk = pl.program_id(2)

        @pl.when(kk == 0)
        def _():
            acc_ref[...] = jnp.zeros_like(acc_ref)

        acc_ref[...] += jnp.dot(a_ref[...].astype(BF16), b_ref[...].astype(BF16), preferred_element_type=F32)

        @pl.when(kk == nk - 1)
        def _():
            o_ref[...] = acc_ref[...]

    return pl.pallas_call(
        body, name=f"mm_nn_{m}x{k}x{n}", grid=(m // tm, n // tn, nk),
        in_specs=[pl.BlockSpec((tm, tk), lambda i, j, kk: (i, kk)), pl.BlockSpec((tk, tn), lambda i, j, kk: (kk, j))],
        out_specs=pl.BlockSpec((tm, tn), lambda i, j, kk: (i, j)),
        out_shape=jax.ShapeDtypeStruct((m, n), F32),
        scratch_shapes=[pltpu.VMEM((tm, tn), F32)],
        compiler_params=_params(dimension_semantics=("parallel", "parallel", "arbitrary")),
    )(a, b)


def _mm_tn(a, g):
    s, k = a.shape
    _, n = g.shape
    tm = _pick(k, (1024, 512, 256, 128))
    tn = _pick(n, (1024, 640, 512, 256, 128))
    ts = _pick(s, (512, 256, 128))
    ns = s // ts

    def body(a_ref, g_ref, o_ref, acc_ref):
        ss = pl.program_id(2)

        @pl.when(ss == 0)
        def _():
            acc_ref[...] = jnp.zeros_like(acc_ref)

        acc_ref[...] += lax.dot_general(a_ref[...].astype(BF16), g_ref[...].astype(BF16),
                                        (((0,), (0,)), ((), ())), preferred_element_type=F32)

        @pl.when(ss == ns - 1)
        def _():
            o_ref[...] = acc_ref[...]

    return pl.pallas_call(
        body, name=f"mm_tn_{s}x{k}x{n}", grid=(k // tm, n // tn, ns),
        in_specs=[pl.BlockSpec((ts, tm), lambda i, j, ss: (ss, i)), pl.BlockSpec((ts, tn), lambda i, j, ss: (ss, j))],
        out_specs=pl.BlockSpec((tm, tn), lambda i, j, ss: (i, j)),
        out_shape=jax.ShapeDtypeStruct((k, n), F32),
        scratch_shapes=[pltpu.VMEM((tm, tn), F32)],
        compiler_params=_params(dimension_semantics=("parallel", "parallel", "arbitrary")),
    )(a, g)


@jax.custom_vjp
def mm(a, w):
    return _mm_nn(a, w.astype(BF16))


def _mm_fwd(a, w):
    wb = w.astype(BF16)
    return _mm_nn(a, wb), (a, wb)


def _mm_bwd(res, g):
    a, wb = res
    return _mm_nn(g, wb.T), _mm_tn(a, g)


mm.defvjp(_mm_fwd, _mm_bwd)


def _make_in_proj(widths):
    cuts = [sum(widths[:i]) for i in range(len(widths) + 1)]

    @jax.custom_vjp
    def in_proj(x, w):
        return fwd(x, w)[0]

    def fwd(x, w):
        xb, wb = x.astype(BF16), w.astype(BF16)
        return tuple(_mm_nn(xb, wb[:, a:b]) for a, b in zip(cuts[:-1], cuts[1:])), (xb, wb)

    def bwd(res, gs):
        xb, wb = res
        g = jnp.concatenate(gs, axis=1)
        return _mm_nn(g, wb.T), _mm_tn(xb, g)

    in_proj.defvjp(fwd, bwd)
    return in_proj


EVEN_GROUPS = (256, 128, 128, 512, 512, 512, 128, 1024)
ODD_GROUPS = (1024, 256, 256, 1024)
even_in_proj = _make_in_proj(EVEN_GROUPS)
odd_in_proj = _make_in_proj(ODD_GROUPS)

SHARD_PAD = 384


def _source_columns(kind):
    if kind == "even":
        src = [list(range(0, 384)), [-1] * 64, list(range(384, 416)), [-1] * 32, list(range(416, 1952)),
               list(range(1952, 1960)), [-1] * 120, list(range(1960, 2984))]
        return sum(src, []), 373
    q0, k0, v0, g0 = 0, 1024, 1152, 1280
    dup = lambda base: [base + 64 * g + c for g in range(SWA_KV_HEADS) for _ in range(2) for c in range(64)]
    return list(range(q0, k0)) + dup(k0) + dup(v0) + list(range(g0, 2304)), 288


def _selection(kind):
    src, shard = _source_columns(kind)
    cat = jnp.asarray([s + (SHARD_PAD - shard) * (s // shard) if s >= 0 else -1 for s in src], jnp.int32)
    rows = lax.broadcasted_iota(jnp.int32, (N_DEV * SHARD_PAD, len(src)), 0)
    return (rows == cat[None, :]).astype(BF16)


@functools.partial(jax.custom_vjp, nondiff_argnums=(1,))
def relayout(wcat, kind):
    return _mm_nn(wcat, _selection(kind))


def _relayout_bwd(kind, _, g):
    return (_mm_nn(g, _selection(kind).T),)


relayout.defvjp(lambda wcat, kind: (_mm_nn(wcat, _selection(kind)), None), _relayout_bwd)


def _row_block(s):
    return _pick(s, (512, 256, 128, 64, 32, 16, 8))


def _rms_fwd_call(x, g):
    s, k = x.shape
    tr = _row_block(s)

    def body(x_ref, g_ref, o_ref):
        xv = x_ref[...]
        r = lax.rsqrt(jnp.mean(xv * xv, axis=-1, keepdims=True) + RMS_EPS)
        o_ref[...] = xv * r * g_ref[...]

    return pl.pallas_call(
        body, name=f"rms_fwd_{k}", grid=(s // tr,),
        in_specs=[pl.BlockSpec((tr, k), lambda i: (i, 0)), pl.BlockSpec((1, k), lambda i: (0, 0))],
        out_specs=pl.BlockSpec((tr, k), lambda i: (i, 0)), out_shape=jax.ShapeDtypeStruct((s, k), F32),
        compiler_params=_params(dimension_semantics=("parallel",)),
    )(x, g.reshape(1, k))


def _rms_bwd_call(x, g, dy):
    s, k = x.shape
    tr = _row_block(s)

    def body(x_ref, g_ref, dy_ref, dx_ref, dg_ref):
        @pl.when(pl.program_id(0) == 0)
        def _():
            dg_ref[...] = jnp.zeros_like(dg_ref)

        xv = x_ref[...]
        r = lax.rsqrt(jnp.mean(xv * xv, axis=-1, keepdims=True) + RMS_EPS)
        xh = xv * r
        dyv = dy_ref[...]
        dg_ref[...] += jnp.sum(dyv * xh, axis=0, keepdims=True)
        dxh = dyv * g_ref[...]
        dx_ref[...] = r * (dxh - xh * jnp.mean(dxh * xh, axis=-1, keepdims=True))

    dx, dg = pl.pallas_call(
        body, name=f"rms_bwd_{k}", grid=(s // tr,),
        in_specs=[pl.BlockSpec((tr, k), lambda i: (i, 0)), pl.BlockSpec((1, k), lambda i: (0, 0)),
                  pl.BlockSpec((tr, k), lambda i: (i, 0))],
        out_specs=[pl.BlockSpec((tr, k), lambda i: (i, 0)), pl.BlockSpec((1, k), lambda i: (0, 0))],
        out_shape=[jax.ShapeDtypeStruct((s, k), F32), jax.ShapeDtypeStruct((1, k), F32)],
        compiler_params=_params(dimension_semantics=("arbitrary",)),
    )(x, g.reshape(1, k), dy)
    return dx, dg.reshape(k)


@jax.custom_vjp
def rms_norm(x, g):
    return _rms_fwd_call(x, g)


rms_norm.defvjp(lambda x, g: (_rms_fwd_call(x, g), (x, g)), lambda res, dy: _rms_bwd_call(res[0], res[1], dy))


def _ln_fwd_call(x, y, g, b):
    s, k = x.shape
    tr = _row_block(s)

    def body(x_ref, y_ref, g_ref, b_ref, o_ref):
        u = ALPHA * x_ref[...] + y_ref[...]
        mu = jnp.mean(u, axis=-1, keepdims=True)
        d = u - mu
        var = jnp.mean(d * d, axis=-1, keepdims=True)
        o_ref[...] = d * lax.rsqrt(var + LN_EPS) * g_ref[...] + b_ref[...]

    row = pl.BlockSpec((tr, k), lambda i: (i, 0))
    vec = pl.BlockSpec((1, k), lambda i: (0, 0))
    return pl.pallas_call(
        body, name="ln_fwd", grid=(s // tr,), in_specs=[row, row, vec, vec], out_specs=row,
        out_shape=jax.ShapeDtypeStruct((s, k), F32), compiler_params=_params(dimension_semantics=("parallel",)),
    )(x, y, g.reshape(1, k), b.reshape(1, k))


def _ln_bwd_call(x, y, g, do):
    s, k = x.shape
    tr = _row_block(s)

    def body(x_ref, y_ref, g_ref, do_ref, dx_ref, dy_ref, dg_ref, db_ref):
        @pl.when(pl.program_id(0) == 0)
        def _():
            dg_ref[...] = jnp.zeros_like(dg_ref)
            db_ref[...] = jnp.zeros_like(db_ref)

        u = ALPHA * x_ref[...] + y_ref[...]
        mu = jnp.mean(u, axis=-1, keepdims=True)
        d = u - mu
        r = lax.rsqrt(jnp.mean(d * d, axis=-1, keepdims=True) + LN_EPS)
        xh = d * r
        dov = do_ref[...]
        dg_ref[...] += jnp.sum(dov * xh, axis=0, keepdims=True)
        db_ref[...] += jnp.sum(dov, axis=0, keepdims=True)
        dxh = dov * g_ref[...]
        du = r * (dxh - jnp.mean(dxh, axis=-1, keepdims=True) - xh * jnp.mean(dxh * xh, axis=-1, keepdims=True))
        dy_ref[...] = du
        dx_ref[...] = ALPHA * du

    row = pl.BlockSpec((tr, k), lambda i: (i, 0))
    vec = pl.BlockSpec((1, k), lambda i: (0, 0))
    dx, dy, dg, db = pl.pallas_call(
        body, name="ln_bwd", grid=(s // tr,), in_specs=[row, row, vec, row], out_specs=[row, row, vec, vec],
        out_shape=[jax.ShapeDtypeStruct((s, k), F32), jax.ShapeDtypeStruct((s, k), F32),
                   jax.ShapeDtypeStruct((1, k), F32), jax.ShapeDtypeStruct((1, k), F32)],
        compiler_params=_params(dimension_semantics=("arbitrary",)),
    )(x, y, g.reshape(1, k), do)
    return dx, dy, dg.reshape(k), db.reshape(k)


@jax.custom_vjp
def ln_res(x, y, g, b):
    return _ln_fwd_call(x, y, g, b)


ln_res.defvjp(lambda x, y, g, b: (_ln_fwd_call(x, y, g, b), (x, y, g)),
              lambda res, do: _ln_bwd_call(res[0], res[1], res[2], do))


def _rope_call(x, c, s1, s2, hs):
    s, w = x.shape
    tr = _row_block(s)
    nb = w // LANES

    def body(x_ref, c_ref, s1_ref, s2_ref, o_ref):
        cv, s1v, s2v = c_ref[...], s1_ref[...], s2_ref[...]
        for cb in range(nb):
            xb = x_ref[:, cb * LANES:(cb + 1) * LANES]
            o_ref[:, cb * LANES:(cb + 1) * LANES] = (
                xb * cv + pltpu.roll(xb, LANES - hs, 1) * s1v + pltpu.roll(xb, hs, 1) * s2v)

    row = pl.BlockSpec((tr, w), lambda i: (i, 0))
    tab = pl.BlockSpec((tr, LANES), lambda i: (i, 0))
    return pl.pallas_call(
        body, name=f"rope_{w}_{hs}", grid=(s // tr,), in_specs=[row, tab, tab, tab], out_specs=row,
        out_shape=jax.ShapeDtypeStruct((s, w), F32), compiler_params=_params(dimension_semantics=("parallel",)),
    )(x, c, s1, s2)


@functools.partial(jax.custom_vjp, nondiff_argnums=(2,))
def rope(x, tabs, hs):
    return _rope_call(x, tabs[0], tabs[1], tabs[2], hs)


def _rope_fwd(x, tabs, hs):
    return _rope_call(x, tabs[0], tabs[1], tabs[2], hs), tabs


def _rope_bwd(hs, tabs, dy):
    return _rope_call(dy, tabs[0], -tabs[1], -tabs[2], hs), jax.tree.map(jnp.zeros_like, tabs)


rope.defvjp(_rope_fwd, _rope_bwd)


def _rope_tables(s, layout):
    pos = jnp.arange(s, dtype=F32)[:, None]
    lane = jnp.arange(LANES)
    if layout == "mla":
        dim, hs = MLA_ROPE, MLA_ROPE // 2
        r = lane - MLA_NOPE
        active = (r >= 0) & (r < MLA_ROPE)
    else:
        dim, hs = SWA_DIM, SWA_DIM // 2
        r = lane % SWA_DIM
        active = jnp.ones_like(lane, dtype=bool)
    f = jnp.where(active, r % hs, 0)
    inv = ROPE_THETA ** (-(2.0 * f.astype(F32)) / dim)
    ang = pos * inv[None, :]
    cos, sin = jnp.cos(ang), jnp.sin(ang)
    first = (active & (r < hs))[None, :]
    second = (active & (r >= hs))[None, :]
    c = jnp.where(active[None, :], cos, 1.0)
    s1 = jnp.where(first, -sin, 0.0)
    s2 = jnp.where(second, sin, 0.0)
    return (c, s1, s2), hs


def _gate_fwd_call(o_parts, gate):
    s, w = gate.shape
    tr = _row_block(s)
    widths = [o.shape[1] for o in o_parts]

    def body(*refs):
        o_refs, g_ref, z_ref = refs[:len(widths)], refs[len(widths)], refs[len(widths) + 1]
        off = 0
        for o_ref, wd in zip(o_refs, widths):
            gv = g_ref[:, off:off + wd]
            z_ref[:, off:off + wd] = o_ref[...] * (gv * jax.nn.sigmoid(gv))
            off += wd

    specs = [pl.BlockSpec((tr, wd), lambda i: (i, 0)) for wd in widths]
    row = pl.BlockSpec((tr, w), lambda i: (i, 0))
    return pl.pallas_call(
        body, name=f"gate_fwd_{len(widths)}", grid=(s // tr,), in_specs=specs + [row], out_specs=row,
        out_shape=jax.ShapeDtypeStruct((s, w), F32), compiler_params=_params(dimension_semantics=("parallel",)),
    )(*o_parts, gate)


def _gate_bwd_call(o_parts, gate, dz):
    s, w = gate.shape
    tr = _row_block(s)
    widths = [o.shape[1] for o in o_parts]
    n = len(widths)

    def body(*refs):
        o_refs, g_ref, dz_ref = refs[:n], refs[n], refs[n + 1]
        do_refs, dg_ref = refs[n + 2:2 * n + 2], refs[2 * n + 2]
        off = 0
        for o_ref, do_ref, wd in zip(o_refs, do_refs, widths):
            gv = g_ref[:, off:off + wd]
            sg = jax.nn.sigmoid(gv)
            dzv = dz_ref[:, off:off + wd]
            do_ref[...] = dzv * (gv * sg)
            dg_ref[:, off:off + wd] = dzv * o_ref[...] * (sg * (1.0 + gv * (1.0 - sg)))
            off += wd

    specs = [pl.BlockSpec((tr, wd), lambda i: (i, 0)) for wd in widths]
    row = pl.BlockSpec((tr, w), lambda i: (i, 0))
    outs = pl.pallas_call(
        body, name=f"gate_bwd_{n}", grid=(s // tr,), in_specs=specs + [row, row], out_specs=specs + [row],
        out_shape=[jax.ShapeDtypeStruct((s, wd), F32) for wd in widths] + [jax.ShapeDtypeStruct((s, w), F32)],
        compiler_params=_params(dimension_semantics=("parallel",)),
    )(*o_parts, gate, dz)
    return tuple(outs[:n]), outs[n]


@jax.custom_vjp
def gate_mul(o_parts, gate):
    return _gate_fwd_call(o_parts, gate)


gate_mul.defvjp(lambda o_parts, gate: (_gate_fwd_call(o_parts, gate), (o_parts, gate)),
                lambda res, dz: _gate_bwd_call(res[0], res[1], dz))


def _loss_call(y, t):
    s, k = y.shape
    tr = _row_block(s)
    nsteps = s // tr

    def body(y_ref, t_ref, l_ref, dy_ref, acc_ref):
        i = pl.program_id(0)

        @pl.when(i == 0)
        def _():
            acc_ref[...] = jnp.zeros_like(acc_ref)

        d = y_ref[...] - t_ref[...]
        dy_ref[...] = d / k
        acc_ref[...] += jnp.sum(d * d, axis=0, keepdims=True)

        @pl.when(i == nsteps - 1)
        def _():
            tot = jnp.sum(acc_ref[...], axis=1, keepdims=True) * (0.5 / k)
            l_ref[...] = jnp.broadcast_to(tot, l_ref.shape)

    row = pl.BlockSpec((tr, k), lambda i: (i, 0))
    return pl.pallas_call(
        body, name="loss", grid=(nsteps,), in_specs=[row, row],
        out_specs=[pl.BlockSpec((1, LANES), lambda i: (0, 0)), row],
        out_shape=[jax.ShapeDtypeStruct((1, LANES), F32), jax.ShapeDtypeStruct((s, k), F32)],
        scratch_shapes=[pltpu.VMEM((1, k), F32)], compiler_params=_params(dimension_semantics=("arbitrary",)),
    )(y, t)


@jax.custom_vjp
def mse_loss(y, t):
    return _loss_call(y, t)[0][0, 0]


def _mse_fwd(y, t):
    l, dy = _loss_call(y, t)
    return l[0, 0], (dy, t)


mse_loss.defvjp(_mse_fwd, lambda res, g: (g * res[0], jnp.zeros_like(res[1])))


def _scan_call(x, b, mode):
    s, w = x.shape
    nt = s // 8

    def tile_scan(t):
        row = lax.broadcasted_iota(jnp.int32, (8, w), 0)
        for sh in (1, 2, 4):
            t = t + jnp.where(row >= sh, pltpu.roll(t, sh, 0), 0.0)
        return t

    def body(x_ref, b_ref, o_ref):
        def step(i, carry):
            rows = pl.ds(pl.multiple_of(i * 8, 8), 8)
            t = x_ref[rows, :]
            if mode == "fwd":
                t = jax.nn.log_sigmoid(t + b_ref[...])
            t = tile_scan(t) + carry
            o_ref[rows, :] = t
            return t[7:8, :]

        total = lax.fori_loop(0, nt, step, jnp.zeros((1, w), F32))
        if mode == "rev":
            def fix(i, c):
                rows = pl.ds(pl.multiple_of(i * 8, 8), 8)
                o_ref[rows, :] = total - o_ref[rows, :] + x_ref[rows, :]
                return c
            lax.fori_loop(0, nt, fix, 0)

    full = pl.BlockSpec((s, w), lambda: (0, 0))
    return pl.pallas_call(
        body, name=f"scan_{mode}", in_specs=[full, pl.BlockSpec((1, w), lambda: (0, 0))], out_specs=full,
        out_shape=jax.ShapeDtypeStruct((s, w), F32), compiler_params=_params(),
    )(x, b)


def _fox_dlogit_call(x, b, dlogf):
    s, w = x.shape
    tr = _row_block(s)

    def body(x_ref, b_ref, d_ref, dx_ref, db_ref):
        @pl.when(pl.program_id(0) == 0)
        def _():
            db_ref[...] = jnp.zeros_like(db_ref)

        dx = d_ref[...] * jax.nn.sigmoid(-(x_ref[...] + b_ref[...]))
        dx_ref[...] = dx
        db_ref[...] += jnp.sum(dx, axis=0, keepdims=True)

    row = pl.BlockSpec((tr, w), lambda i: (i, 0))
    vec = pl.BlockSpec((1, w), lambda i: (0, 0))
    return pl.pallas_call(
        body, name="fox_dlogit", grid=(s // tr,), in_specs=[row, vec, row], out_specs=[row, vec],
        out_shape=[jax.ShapeDtypeStruct((s, w), F32), jax.ShapeDtypeStruct((1, w), F32)],
        compiler_params=_params(dimension_semantics=("arbitrary",)),
    )(x, b, dlogf)


@jax.custom_vjp
def fox_cum(fl, b):
    return _scan_call(fl, b, "fwd")


def _fox_cum_bwd(res, dcum):
    fl, b = res
    dlogf = _scan_call(dcum, b, "rev")
    return _fox_dlogit_call(fl, b, dlogf)


fox_cum.defvjp(lambda fl, b: (_scan_call(fl, b, "fwd"), (fl, b)), _fox_cum_bwd)


def _lane_col(x, lane_idx):
    lane = lax.broadcasted_iota(jnp.int32, (1, x.shape[1]), 1)
    return jnp.sum(jnp.where(lane == lane_idx, x, 0.0), axis=1, keepdims=True)


def _row_of(x, row_idx):
    row = lax.broadcasted_iota(jnp.int32, (x.shape[0], 1), 0)
    return jnp.sum(jnp.where(row == row_idx, x, 0.0), axis=0, keepdims=True)


def _attn_cfg(mode, s):
    if mode == "swa":
        blk = 256 if s >= 2048 else 128
        return dict(blk=blk, n_outer=SWA_KV_HEADS, pps=4, wide=False, scale=SWA_DIM ** -0.5)
    blk = 512 if s >= 2048 else 128
    if mode == "mla":
        return dict(blk=blk, n_outer=4, pps=1, wide=True, scale=(MLA_NOPE + MLA_ROPE) ** -0.5)
    return dict(blk=blk, n_outer=4, pps=1, wide=False, scale=FOX_DIM ** -0.5)


ROW_CHUNK = 32


def _unrolled(n, body, carry):
    for c in range(n):
        carry = body(c, carry)
    return carry


def _valid_rows(mode, i, jb, blk, r0, rc):
    qpos = i * blk + r0 + lax.broadcasted_iota(jnp.int32, (rc, blk), 0)
    kpos = jb * blk + lax.broadcasted_iota(jnp.int32, (rc, blk), 1)
    ok = kpos <= qpos
    if mode == "swa":
        ok = ok & (qpos - kpos < WINDOW)
    return ok


def _attn_fwd_call(mode, q, k, v, extra):
    s = q.shape[0]
    cfg = _attn_cfg(mode, s)
    blk, n_outer, pps, wide, scale = cfg["blk"], cfg["n_outer"], cfg["pps"], cfg["wide"], cfg["scale"]
    rc = ROW_CHUNK
    nq = s // blk
    swa, fox = mode == "swa", mode == "fox"
    qw = (2 * LANES if wide else LANES) * pps
    kw = 2 * LANES if wide else LANES
    ow = LANES * pps
    reps = blk // LANES

    def body(*refs):
        q_ref, k_ref, v_ref = refs[:3]
        n_in = 3
        if fox:
            cum_ref, cumt_ref = refs[3:5]
            n_in = 5
        if swa:
            sink_ref = refs[3]
            n_in = 4
        o_ref, lse_ref, m_ref, l_ref, acc_ref, a_ref, s_ref, p_ref, c_ref = refs[n_in:]
        p_id, i, j = pl.program_id(0), pl.program_id(1), pl.program_id(2)
        if swa:
            jb, run, first, last = i - 1 + j, (i - 1 + j) >= 0, j == 0, j == 1
        else:
            jb, run, first, last = j, j <= i, j == 0, j == i
        lane = lax.broadcasted_iota(jnp.int32, (1, LANES), 1)
        msk = [lane < HALF, lane >= HALF]

        @pl.when(first)
        def _():
            for hh in range(2 * pps):
                if swa:
                    m_ref[hh] = jnp.broadcast_to(sink_ref[hh:hh + 1, :], (blk, LANES))
                    l_ref[hh] = jnp.ones((blk, LANES), F32)
                else:
                    m_ref[hh] = jnp.full((blk, LANES), NEG, F32)
                    l_ref[hh] = jnp.zeros((blk, LANES), F32)
            acc_ref[...] = jnp.zeros_like(acc_ref)

        def process(masked):
            for pp in range(pps):
                vb = v_ref[...]
                pvs = []
                for h in range(2):
                    hh = 2 * pp + h
                    if wide:
                        qh = q_ref[:, h * LANES:(h + 1) * LANES] * scale
                        kh = k_ref[:, h * LANES:(h + 1) * LANES]
                    else:
                        qh = jnp.where(msk[h], q_ref[:, pp * LANES:(pp + 1) * LANES], 0.0) * scale
                        kh = k_ref[...]
                    s_ref[...] = lax.dot_general(qh.astype(BF16), kh.astype(BF16), (((1,), (1,)), ((), ())),
                                                 preferred_element_type=F32)
                    if fox:
                        head = 2 * p_id + h
                        c_ref[...] = jnp.broadcast_to(_lane_col(cum_ref[...], head), (blk, LANES))
                        ck = _row_of(cumt_ref[...], head)

                    def chunk(c, carry, hh=hh, h=h):
                        r0 = c * rc
                        rows = pl.ds(r0, rc)
                        u = s_ref[rows, :]
                        if fox:
                            u = u - ck
                        if masked:
                            u = jnp.where(_valid_rows(mode, i, jb, blk, r0, rc), u, NEG)
                        m_prev, l_prev = m_ref[hh, rows, :], l_ref[hh, rows, :]
                        m_cur = jnp.max(u, axis=1, keepdims=True)
                        if fox:
                            m_cur = m_cur + c_ref[rows, :]
                        m_next = jnp.maximum(m_prev, m_cur)
                        shift = m_next - c_ref[rows, :] if fox else m_next
                        p = jnp.exp(u - jnp.tile(shift, (1, reps)))
                        alpha = jnp.exp(m_prev - m_next)
                        l_ref[hh, rows, :] = alpha * l_prev + jnp.sum(p, axis=1, keepdims=True)
                        m_ref[hh, rows, :] = m_next
                        a_ref[h, rows, :] = alpha
                        p_ref[rows, :] = p.astype(BF16)
                        return carry

                    _unrolled(blk // rc, chunk, 0)
                    vh = jnp.where(msk[h], vb, 0.0).astype(BF16)
                    pvs.append(jnp.dot(p_ref[...], vh, preferred_element_type=F32))
                acc_ref[pp] = acc_ref[pp] * jnp.where(msk[0], a_ref[0], a_ref[1]) + pvs[0] + pvs[1]

        if swa:
            pl.when(run)(lambda: process(True))
        else:
            pl.when(j < i)(lambda: process(False))
            pl.when(j == i)(lambda: process(True))

        @pl.when(last)
        def _():
            for pp in range(pps):
                l0, l1 = l_ref[2 * pp], l_ref[2 * pp + 1]
                o_ref[:, pp * LANES:(pp + 1) * LANES] = acc_ref[pp] / jnp.where(msk[0], l0, l1)
                lse_ref[:, pp * LANES:(pp + 1) * LANES] = jnp.where(
                    msk[0], m_ref[2 * pp] + jnp.log(l0), m_ref[2 * pp + 1] + jnp.log(l1))

    if swa:
        kv_map = lambda g, i, j: (jnp.maximum(i - 1 + j, 0), g)
        grid = (n_outer, nq, 2)
    else:
        kv_map = lambda p, i, j: (jnp.minimum(j, i), p)
        grid = (n_outer, nq, nq)
    q_map = lambda p, i, j: (i, p)
    in_specs = [pl.BlockSpec((blk, qw), q_map), pl.BlockSpec((blk, kw), kv_map), pl.BlockSpec((blk, LANES), kv_map)]
    args = [q, k, v]
    if fox:
        cum, cumt = extra
        in_specs += [pl.BlockSpec((blk, LANES), lambda p, i, j: (i, 0)),
                     pl.BlockSpec((8, blk), lambda p, i, j: (0, jnp.minimum(j, i)))]
        args += [cum, cumt]
    if swa:
        in_specs += [pl.BlockSpec((8, LANES), lambda g, i, j: (g, 0))]
        args += [extra]
    n_pairs = n_outer * pps
    return pl.pallas_call(
        body, name=f"attn_fwd_{mode}", grid=grid, in_specs=in_specs,
        out_specs=[pl.BlockSpec((blk, ow), q_map), pl.BlockSpec((blk, ow), q_map)],
        out_shape=[jax.ShapeDtypeStruct((s, n_pairs * LANES), F32), jax.ShapeDtypeStruct((s, n_pairs * LANES), F32)],
        scratch_shapes=[pltpu.VMEM((2 * pps, blk, LANES), F32), pltpu.VMEM((2 * pps, blk, LANES), F32),
                        pltpu.VMEM((pps, blk, LANES), F32), pltpu.VMEM((2, blk, LANES), F32),
                        pltpu.VMEM((blk, blk), F32), pltpu.VMEM((blk, blk), BF16), pltpu.VMEM((blk, LANES), F32)],
        compiler_params=_params(dimension_semantics=("parallel", "parallel", "arbitrary")),
    )(*args)


def _rowdot_call(do, o):
    s, w = o.shape
    tr = _row_block(s)

    def body(do_ref, o_ref, d_ref):
        lane = lax.broadcasted_iota(jnp.int32, (1, LANES), 1)
        low = lane < HALF
        for cb in range(w // LANES):
            sl = slice(cb * LANES, (cb + 1) * LANES)
            prod = do_ref[:, sl] * o_ref[:, sl]
            d0 = jnp.sum(jnp.where(low, prod, 0.0), axis=1, keepdims=True)
            d1 = jnp.sum(jnp.where(low, 0.0, prod), axis=1, keepdims=True)
            d_ref[:, sl] = jnp.where(low, d0, d1)

    row = pl.BlockSpec((tr, w), lambda i: (i, 0))
    return pl.pallas_call(
        body, name=f"rowdot_{w}", grid=(s // tr,), in_specs=[row, row], out_specs=row,
        out_shape=jax.ShapeDtypeStruct((s, w), F32), compiler_params=_params(dimension_semantics=("parallel",)),
    )(do, o)


def _attn_bwd_call(mode, q, k, v, extra, lse, dd, do):
    s = q.shape[0]
    cfg = _attn_cfg(mode, s)
    blk, n_outer, pps, wide, scale = cfg["blk"], cfg["n_outer"], cfg["pps"], cfg["wide"], cfg["scale"]
    rc = ROW_CHUNK
    nq = s // blk
    swa, fox = mode == "swa", mode == "fox"
    qw = (2 * LANES if wide else LANES) * pps
    kw = 2 * LANES if wide else LANES
    ow = LANES * pps
    reps = blk // LANES

    def body(*refs):
        q_ref, k_ref, v_ref, lse_ref, dd_ref, do_ref = refs[:6]
        n_in = 6
        if fox:
            cum_ref, cumt_ref = refs[6:8]
            n_in = 8
        if swa:
            sink_ref = refs[6]
            n_in = 7
        dq_ref, dk_ref, dv_ref = refs[n_in:n_in + 3]
        n_out = n_in + 3
        if fox:
            dck_ref, dcq_ref = refs[n_out:n_out + 2]
            n_out += 2
        if swa:
            dsink_ref = refs[n_out]
            n_out += 1
        dk_acc, dv_acc, s_ref, dp_ref, p_ref, ds_ref, e_ref, d_ref = refs[n_out:n_out + 8]
        if fox:
            dck_acc, rs_ref = refs[n_out + 8:n_out + 10]
        p_id, j, ii = pl.program_id(0), pl.program_id(1), pl.program_id(2)
        if swa:
            i, run, first_i, last_i = j + ii, (j + ii) < nq, ii == 0, ii == 1
        else:
            i, run, first_i, last_i = ii, ii >= j, ii == j, ii == nq - 1
        lane = lax.broadcasted_iota(jnp.int32, (1, LANES), 1)
        msk = [lane < HALF, lane >= HALF]

        @pl.when((j == 0) & (ii == 0))
        def _():
            dq_ref[...] = jnp.zeros_like(dq_ref)
            if swa:
                dsink_ref[...] = jnp.zeros_like(dsink_ref)
            if fox:
                dcq_ref[...] = jnp.zeros_like(dcq_ref)

        @pl.when(first_i)
        def _():
            dk_acc[...] = jnp.zeros_like(dk_acc)
            dv_acc[...] = jnp.zeros_like(dv_acc)
            if fox:
                dck_acc[...] = jnp.zeros_like(dck_acc)

        def process(masked):
            rows = pl.ds(pl.multiple_of(i * blk, blk), blk)
            vb = v_ref[...].astype(BF16)
            for pp in range(pps):
                psl = slice(pp * LANES, (pp + 1) * LANES)
                lse_blk, dd_blk, do_blk = lse_ref[:, psl], dd_ref[:, psl], do_ref[:, psl]
                for h in range(2):
                    hh = 2 * pp + h
                    if wide:
                        hsl = slice(h * LANES, (h + 1) * LANES)
                        qh = (q_ref[:, hsl] * scale).astype(BF16)
                        kh = k_ref[:, hsl].astype(BF16)
                    else:
                        qh = (jnp.where(msk[h], q_ref[:, psl], 0.0) * scale).astype(BF16)
                        kh = k_ref[...].astype(BF16)
                    s_ref[...] = lax.dot_general(qh, kh, (((1,), (1,)), ((), ())), preferred_element_type=F32)
                    do_h = jnp.where(msk[h], do_blk, 0.0).astype(BF16)
                    dp_ref[...] = lax.dot_general(do_h, vb, (((1,), (1,)), ((), ())), preferred_element_type=F32)
                    lse_h = _lane_col(lse_blk, HALF * h)
                    d_h = _lane_col(dd_blk, HALF * h)
                    e_ref[...] = jnp.broadcast_to(lse_h, (blk, LANES))
                    d_ref[...] = jnp.broadcast_to(d_h, (blk, LANES))
                    if fox:
                        head = 2 * p_id + h
                        e_ref[...] = e_ref[...] - jnp.broadcast_to(_lane_col(cum_ref[...], head), (blk, LANES))
                        ck = _row_of(cumt_ref[...], head)

                    def chunk(c, colsum):
                        r0 = c * rc
                        cr = pl.ds(r0, rc)
                        u = s_ref[cr, :]
                        if fox:
                            u = u - ck
                        p = jnp.exp(u - jnp.tile(e_ref[cr, :], (1, reps)))
                        if masked:
                            p = jnp.where(_valid_rows(mode, i, j, blk, r0, rc), p, 0.0)
                        ds = p * (dp_ref[cr, :] - jnp.tile(d_ref[cr, :], (1, reps)))
                        p_ref[cr, :] = p.astype(BF16)
                        ds_ref[cr, :] = ds.astype(BF16)
                        if fox:
                            colsum = colsum + jnp.sum(ds, axis=0, keepdims=True)
                            rs_ref[cr, :] = jnp.broadcast_to(jnp.sum(ds, axis=1, keepdims=True), (rc, LANES))
                        return colsum

                    colsum = _unrolled(blk // rc, chunk, jnp.zeros((1, blk), F32))
                    dv_acc[...] += lax.dot_general(p_ref[...], do_h, (((0,), (0,)), ((), ())),
                                                   preferred_element_type=F32)
                    if fox:
                        dck_acc[h:h + 1, :] += -colsum
                        dcq_ref[rows, :] += jnp.where(msk[h], rs_ref[...], 0.0)
                    dq_h = jnp.dot(ds_ref[...], kh, preferred_element_type=F32) * scale
                    dk_h = lax.dot_general(ds_ref[...], qh, (((0,), (0,)), ((), ())), preferred_element_type=F32)
                    if wide:
                        dq_ref[rows, hsl] += dq_h
                        dk_acc[:, hsl] += dk_h
                    else:
                        dq_ref[rows, psl] += jnp.where(msk[h], dq_h, 0.0)
                        dk_acc[...] += dk_h
                    if swa:
                        @pl.when(first_i)
                        def _():
                            e = jnp.exp(sink_ref[hh:hh + 1, :] - lse_h) * d_h
                            dsink_ref[hh:hh + 1, :] += -jnp.sum(e, axis=0, keepdims=True)

        if swa:
            pl.when(run)(lambda: process(True))
        else:
            pl.when(ii > j)(lambda: process(False))
            pl.when(ii == j)(lambda: process(True))

        @pl.when(last_i)
        def _():
            dk_ref[...] = dk_acc[...]
            dv_ref[...] = dv_acc[...]
            if fox:
                dck_ref[0] = dck_acc[...]

    if swa:
        q_map = lambda g, j, ii: (jnp.minimum(j + ii, nq - 1), g)
        grid = (n_outer, nq, 2)
    else:
        q_map = lambda p, j, ii: (jnp.maximum(ii, j), p)
        grid = (n_outer, nq, nq)
    kv_map = lambda p, j, ii: (j, p)
    in_specs = [pl.BlockSpec((blk, qw), q_map), pl.BlockSpec((blk, kw), kv_map), pl.BlockSpec((blk, LANES), kv_map),
                pl.BlockSpec((blk, ow), q_map), pl.BlockSpec((blk, ow), q_map), pl.BlockSpec((blk, ow), q_map)]
    args = [q, k, v, lse, dd, do]
    n_pairs = n_outer * pps
    out_specs = [pl.BlockSpec((s, qw), lambda p, j, ii: (0, p)), pl.BlockSpec((blk, kw), kv_map),
                 pl.BlockSpec((blk, LANES), kv_map)]
    out_shape = [jax.ShapeDtypeStruct((s, q.shape[1]), F32), jax.ShapeDtypeStruct((s, k.shape[1]), F32),
                 jax.ShapeDtypeStruct((s, v.shape[1]), F32)]
    scratch = [pltpu.VMEM((blk, kw), F32), pltpu.VMEM((blk, LANES), F32), pltpu.VMEM((blk, blk), F32),
               pltpu.VMEM((blk, blk), F32), pltpu.VMEM((blk, blk), BF16), pltpu.VMEM((blk, blk), BF16),
               pltpu.VMEM((blk, LANES), F32), pltpu.VMEM((blk, LANES), F32)]
    if fox:
        cum, cumt = extra
        if swa:
            raise AssertionError
        in_specs += [pl.BlockSpec((blk, LANES), lambda p, j, ii: (jnp.maximum(ii, j), 0)),
                     pl.BlockSpec((8, blk), lambda p, j, ii: (0, j))]
        args += [cum, cumt]
        out_specs += [pl.BlockSpec((1, 8, blk), lambda p, j, ii: (p, 0, j)),
                      pl.BlockSpec((s, LANES), lambda p, j, ii: (0, p))]
        out_shape += [jax.ShapeDtypeStruct((n_pairs, 8, s), F32), jax.ShapeDtypeStruct((s, n_pairs * LANES), F32)]
        scratch += [pltpu.VMEM((8, blk), F32), pltpu.VMEM((blk, LANES), F32)]
    if swa:
        in_specs += [pl.BlockSpec((8, LANES), lambda g, j, ii: (g, 0))]
        args += [extra]
        out_specs += [pl.BlockSpec((8, LANES), lambda g, j, ii: (g, 0))]
        out_shape += [jax.ShapeDtypeStruct((SWA_HEADS, LANES), F32)]
    return pl.pallas_call(
        body, name=f"attn_bwd_{mode}", grid=grid, in_specs=in_specs, out_specs=out_specs, out_shape=out_shape,
        scratch_shapes=scratch,
        compiler_params=_params(dimension_semantics=("parallel", "arbitrary", "arbitrary")),
    )(*args)


def _make_attn(mode):
    @jax.custom_vjp
    def attn(q, k, v, extra):
        return _attn_fwd_call(mode, q, k, v, extra)[0]

    def fwd(q, k, v, extra):
        o, lse = _attn_fwd_call(mode, q, k, v, extra)
        return o, (q, k, v, extra, o, lse)

    def bwd(res, do):
        q, k, v, extra, o, lse = res
        dd = _rowdot_call(do, o)
        outs = _attn_bwd_call(mode, q, k, v, extra, lse, dd, do)
        dq, dk, dv = outs[:3]
        if mode == "fox":
            cum, cumt = extra
            dck = outs[3]
            dcumt = dck[:, :2, :].reshape(FOX_HEADS, -1)
            dcq = outs[4].reshape(-1, FOX_HEADS, HALF)[:, :, 0]
            dextra = (jnp.pad(dcq, ((0, 0), (0, LANES - FOX_HEADS))), dcumt)
        elif mode == "swa":
            dextra = jnp.where(jnp.arange(LANES)[None, :] == 0, outs[3], 0.0)
        else:
            dextra = None
        return dq, dk, dv, dextra

    attn.defvjp(fwd, bwd)
    return attn


attn_mla = _make_attn("mla")
attn_fox = _make_attn("fox")
attn_swa = _make_attn("swa")


def _ukv_layout(w):
    r = w.shape[0]
    w3 = w.reshape(r, MLA_HEADS, MLA_NOPE + MLA_V)
    wk = jnp.pad(w3[:, :, :MLA_NOPE], ((0, 0), (0, 0), (0, LANES - MLA_NOPE))).reshape(r, MLA_HEADS * LANES)
    wv = w3[:, :, MLA_NOPE:].reshape(r, MLA_HEADS * MLA_V)
    return wk, wv


def _even_layer(x, w_in_cat, q_norm, w_uq_p, kv_norm, w_ukv, b_f, w_out, ln_g, ln_b, tabs_mla):
    cq, ckv, kpe, fq, fk, fv, fl, gate = even_in_proj(x, relayout(w_in_cat, "even"))
    tabs, hs = tabs_mla
    q = rope(mm(rms_norm(cq, q_norm), w_uq_p), tabs, hs)
    ckvn = rms_norm(ckv, kv_norm)
    wk, wv = _ukv_layout(w_ukv)
    kk = mm(ckvn, wk) + jnp.tile(rope(kpe, tabs, hs), (1, MLA_HEADS))
    o_mla = attn_mla(q, kk, mm(ckvn, wv), None)
    cum = fox_cum(fl, jnp.pad(b_f, (0, LANES - FOX_HEADS)).reshape(1, LANES))
    o_fox = attn_fox(fq, fk, fv, (cum, cum[:, :8].T))
    y = mm(gate_mul((o_mla, o_fox), gate), w_out)
    return ln_res(x, y, ln_g, ln_b)


def _odd_layer(x, w_in_cat, sinks, w_out, ln_g, ln_b, tabs_swa):
    q, kd, vd, gate = odd_in_proj(x, relayout(w_in_cat, "odd"))
    tabs, hs = tabs_swa
    q = rope(q, tabs, hs)
    kd = rope(kd, tabs, hs)
    o = attn_swa(q, kd, vd, jnp.broadcast_to(sinks[:, None], (SWA_HEADS, LANES)))
    y = mm(gate_mul((o,), gate), w_out)
    return ln_res(x, y, ln_g, ln_b)


def _local_loss(p, x, target):
    s = x.shape[0]
    tabs_mla = _rope_tables(s, "mla")
    tabs_swa = _rope_tables(s, "swa")
    for layer in range(DEPTH):
        j = layer // 2
        if layer % 2 == 0:
            x = _even_layer(x, p["even_w_in"][j], p["even_q_norm"][j], p["even_w_uq"][j], p["even_kv_norm"][j],
                            p["even_w_ukv"][j], p["even_b_f"][j], p["even_w_out"][j], p["even_ln_g"][j],
                            p["even_ln_b"][j], tabs_mla)
        else:
            x = _odd_layer(x, p["odd_w_in"][j], p["odd_sinks"][j], p["odd_w_out"][j], p["odd_ln_g"][j],
                           p["odd_ln_b"][j], tabs_swa)
    return mse_loss(x, target)


def _pad_rows(flat, mult):
    n = flat.shape[-1]
    per = mult * LANES
    padded = -(-n // per) * per
    if padded != n:
        flat = jnp.pad(flat, [(0, 0)] * (flat.ndim - 1) + [(0, padded - n)])
    return flat.reshape(flat.shape[:-1] + (padded // LANES, LANES))


def _pad_last(a, width):
    if a.shape[-1] == width:
        return a
    return jnp.pad(a, [(0, 0)] * (a.ndim - 1) + [(0, width - a.shape[-1])])


def _join(slots, axis):
    shp = list(slots.shape[1:])
    shp[axis] *= N_DEV
    return jnp.moveaxis(slots, 0, axis).reshape(shp)


def _split(full, axis):
    shp = full.shape
    t = full.reshape(shp[:axis] + (N_DEV, shp[axis] // N_DEV) + shp[axis + 1:])
    return jnp.moveaxis(t, axis, 0)


PAD_TO = {"even_w_in": SHARD_PAD, "even_w_uq": LANES, "odd_w_in": SHARD_PAD}


def kernel(x, even_w_in, even_q_norm, even_w_uq, even_kv_norm, even_w_ukv, even_b_f, even_w_out, even_ln_g, even_ln_b, odd_w_in, odd_sinks, odd_w_out, odd_ln_g, odd_ln_b, loss_target, m_even_w_in, m_even_q_norm, m_even_w_uq, m_even_kv_norm, m_even_w_ukv, m_even_b_f, m_even_w_out, m_even_ln_g, m_even_ln_b, m_odd_w_in, m_odd_sinks, m_odd_w_out, m_odd_ln_g, m_odd_ln_b, v_even_w_in, v_even_q_norm, v_even_w_uq, v_even_kv_norm, v_even_w_ukv, v_even_b_f, v_even_w_out, v_even_ln_g, v_even_ln_b, v_odd_w_in, v_odd_sinks, v_odd_w_out, v_odd_ln_g, v_odd_ln_b):
    w = dict(even_w_in=even_w_in, even_q_norm=even_q_norm, even_w_uq=even_w_uq, even_kv_norm=even_kv_norm,
             even_w_ukv=even_w_ukv, even_b_f=even_b_f, even_w_out=even_w_out, even_ln_g=even_ln_g, even_ln_b=even_ln_b,
             odd_w_in=odd_w_in, odd_sinks=odd_sinks, odd_w_out=odd_w_out, odd_ln_g=odd_ln_g, odd_ln_b=odd_ln_b)
    mom = dict(even_w_in=m_even_w_in, even_q_norm=m_even_q_norm, even_w_uq=m_even_w_uq, even_kv_norm=m_even_kv_norm,
               even_w_ukv=m_even_w_ukv, even_b_f=m_even_b_f, even_w_out=m_even_w_out, even_ln_g=m_even_ln_g,
               even_ln_b=m_even_ln_b, odd_w_in=m_odd_w_in, odd_sinks=m_odd_sinks, odd_w_out=m_odd_w_out,
               odd_ln_g=m_odd_ln_g, odd_ln_b=m_odd_ln_b)
    vel = dict(even_w_in=v_even_w_in, even_q_norm=v_even_q_norm, even_w_uq=v_even_w_uq, even_kv_norm=v_even_kv_norm,
               even_w_ukv=v_even_w_ukv, even_b_f=v_even_b_f, even_w_out=v_even_w_out, even_ln_g=v_even_ln_g,
               even_ln_b=v_even_ln_b, odd_w_in=v_odd_w_in, odd_sinks=v_odd_sinks, odd_w_out=v_odd_w_out,
               odd_ln_g=v_odd_ln_g, odd_ln_b=v_odd_ln_b)
    sharded = BIG + SMALL_SHARDED
    padded = lambda d, n: _pad_last(d[n], PAD_TO.get(n, d[n].shape[-1]))

    mine = [padded(w, n).astype(BF16) for n in BIG] + [w[n] for n in SMALL_SHARDED]
    gathered = _all_gather(mine, "all_gather")
    full = {n: _join(g, SHARD_AXIS[n]).astype(F32) for n, g in zip(sharded, gathered)}
    for n in REPL:
        full[n] = w[n]

    loss_local, (grads, grad_x) = jax.value_and_grad(_local_loss, argnums=(0, 1))(full, x[0], loss_target[0])
    loss = lax.psum(loss_local, AXES)

    repl_rows = _pad_rows(jnp.concatenate([grads[n].reshape(-1) for n in REPL]), 8)
    parts = [_split(grads[n], SHARD_AXIS[n]).astype(BF16 if n in BIG else F32) for n in sharded]
    parts.append(jnp.broadcast_to(repl_rows[None], (N_DEV,) + repl_rows.shape))
    recv = _exchange(parts, "grad_exchange")

    g_out, d_out, m_out, v_out = {}, {}, {}, {}
    for n, r in zip(sharded, recv[:-1]):
        cols = r.shape[-1]
        flat2 = lambda d: padded(d, n).reshape(-1, cols)
        outs = _sum_adamw(r.reshape(N_DEV, -1, cols), flat2(w), flat2(mom), flat2(vel))
        for dst, o in zip((g_out, d_out, m_out, v_out), outs):
            dst[n] = o.reshape(w[n].shape[:-1] + (cols,))[..., :w[n].shape[-1]]
    pack = lambda d: _pad_rows(jnp.concatenate([d[n].reshape(-1) for n in REPL]), 8)
    outs = _sum_adamw(recv[-1], pack(w), pack(mom), pack(vel))
    for dst, o in zip((g_out, d_out, m_out, v_out), outs):
        flat, off = o.reshape(-1), 0
        for n in REPL:
            size = math.prod(w[n].shape)
            dst[n] = flat[off:off + size].reshape(w[n].shape)
            off += size
    return (loss, grad_x[None], *[g_out[n] for n in WEIGHTS], *[d_out[n] for n in WEIGHTS],
            *[m_out[n] for n in WEIGHTS], *[v_out[n] for n in WEIGHTS])
```

```python
import functools
import math

import jax
import jax.numpy as jnp
from jax import lax
from jax.experimental import pallas as pl
from jax.experimental.pallas import tpu as pltpu

F32 = jnp.float32
BF16 = jnp.bfloat16
LANES = 128
HALF = 64
N_DEV = 8
AXES = ("x", "y", "c")
VMEM_LIMIT = 48 * 1024 * 1024

D_MODEL = 1024
DEPTH = 4
ROPE_THETA = 10000.0
MLA_HEADS, MLA_NOPE, MLA_ROPE, MLA_V, MLA_Q_RANK, MLA_KV_RANK = 8, 64, 32, 64, 256, 128
FOX_HEADS, FOX_DIM = 8, 64
SWA_HEADS, SWA_KV_HEADS, SWA_DIM, WINDOW = 16, 2, 64, 128
RMS_EPS, LN_EPS = 1e-6, 1e-5
ALPHA = (2 * DEPTH) ** 0.25
ADAM_LR, ADAM_B1, ADAM_B2, ADAM_EPS, ADAM_WD, ADAM_STEP = 0.001, 0.9, 0.999, 1e-08, 0.01, 10
NEG = -1e30

WEIGHTS = ["even_w_in", "even_q_norm", "even_w_uq", "even_kv_norm", "even_w_ukv", "even_b_f", "even_w_out",
           "even_ln_g", "even_ln_b", "odd_w_in", "odd_sinks", "odd_w_out", "odd_ln_g", "odd_ln_b"]
SHARD_AXIS = {"even_w_in": 2, "even_w_uq": 2, "even_w_ukv": 2, "even_w_out": 1, "odd_w_in": 2, "odd_w_out": 1,
              "odd_ln_g": 1, "odd_ln_b": 1, "even_q_norm": None, "even_kv_norm": None, "even_b_f": None,
              "even_ln_g": None, "even_ln_b": None, "odd_sinks": None}
BIG = ["even_w_in", "even_w_uq", "even_w_ukv", "even_w_out", "odd_w_in", "odd_w_out"]
SMALL_SHARDED = ["odd_ln_g", "odd_ln_b"]
REPL = [n for n in WEIGHTS if SHARD_AXIS[n] is None]


def _pick(n, cands):
    for c in cands:
        if n % c == 0:
            return c
    return n


def _params(**kw):
    return pltpu.CompilerParams(vmem_limit_bytes=VMEM_LIMIT, **kw)


def _me():
    return lax.axis_index("x"), lax.axis_index("y"), lax.axis_index("c")


def _peer(k):
    x, y, c = _me()
    px = 1 - x if (k >> 2) & 1 else x
    py = 1 - y if (k >> 1) & 1 else y
    pc = 1 - c if k & 1 else c
    return px, py, pc


def _lin(p):
    return 4 * p[0] + 2 * p[1] + p[2]


def _comm_call(body, n, out_shape, args, name):
    any_spec = pl.BlockSpec(memory_space=pl.ANY)
    return pl.pallas_call(
        body, name=name, out_shape=out_shape, in_specs=[any_spec] * n, out_specs=[any_spec] * n,
        scratch_shapes=[pltpu.SemaphoreType.DMA((n, N_DEV - 1)), pltpu.SemaphoreType.DMA((n, N_DEV - 1)),
                        pltpu.SemaphoreType.DMA((n,))],
    )(*args)


def _all_gather(xs, name):
    n = len(xs)

    def body(*refs):
        x_refs, out_refs = refs[:n], refs[n:2 * n]
        send_sems, recv_sems, local_sems = refs[2 * n:]
        me = _lin(_me())
        local = [pltpu.make_async_copy(x_refs[a], out_refs[a].at[me], local_sems.at[a]) for a in range(n)]
        for cp in local:
            cp.start()
        sends = []
        for k in range(1, N_DEV):
            for a in range(n):
                cp = pltpu.make_async_remote_copy(
                    src_ref=x_refs[a], dst_ref=out_refs[a].at[me], send_sem=send_sems.at[a, k - 1],
                    recv_sem=recv_sems.at[a, k - 1], device_id=_peer(k), device_id_type=pl.DeviceIdType.MESH)
                cp.start()
                sends.append(cp)
        for k in range(1, N_DEV):
            for a in range(n):
                pltpu.make_async_remote_copy(
                    src_ref=x_refs[a], dst_ref=out_refs[a].at[_lin(_peer(k))], send_sem=send_sems.at[a, k - 1],
                    recv_sem=recv_sems.at[a, k - 1], device_id=_peer(k),
                    device_id_type=pl.DeviceIdType.MESH).wait_recv()
        for cp in sends:
            cp.wait_send()
        for cp in local:
            cp.wait()

    out_shape = [jax.ShapeDtypeStruct((N_DEV,) + x.shape, x.dtype) for x in xs]
    return _comm_call(body, n, out_shape, xs, name)


def _exchange(parts, name):
    n = len(parts)

    def body(*refs):
        p_refs, out_refs = refs[:n], refs[n:2 * n]
        send_sems, recv_sems, local_sems = refs[2 * n:]
        me = _lin(_me())
        local = [pltpu.make_async_copy(p_refs[a].at[me], out_refs[a].at[me], local_sems.at[a]) for a in range(n)]
        for cp in local:
            cp.start()
        sends = []
        for k in range(1, N_DEV):
            peer = _peer(k)
            for a in range(n):
                cp = pltpu.make_async_remote_copy(
                    src_ref=p_refs[a].at[_lin(peer)], dst_ref=out_refs[a].at[me], send_sem=send_sems.at[a, k - 1],
                    recv_sem=recv_sems.at[a, k - 1], device_id=peer, device_id_type=pl.DeviceIdType.MESH)
                cp.start()
                sends.append(cp)
        for k in range(1, N_DEV):
            peer = _peer(k)
            for a in range(n):
                pltpu.make_async_remote_copy(
                    src_ref=p_refs[a].at[_lin(peer)], dst_ref=out_refs[a].at[_lin(peer)],
                    send_sem=send_sems.at[a, k - 1], recv_sem=recv_sems.at[a, k - 1], device_id=peer,
                    device_id_type=pl.DeviceIdType.MESH).wait_recv()
        for cp in sends:
            cp.wait_send()
        for cp in local:
            cp.wait()

    out_shape = [jax.ShapeDtypeStruct(p.shape, p.dtype) for p in parts]
    return _comm_call(body, n, out_shape, parts, name)


def _sum_adamw(recv, w, m, v):
    _, rows, lanes = recv.shape
    tr = _pick(rows, (256, 128, 64, 32, 16, 8))
    c1 = 1.0 - ADAM_B1 ** ADAM_STEP
    c2 = 1.0 - ADAM_B2 ** ADAM_STEP

    def body(r_ref, w_ref, m_ref, v_ref, g_out, d_out, m_out, v_out):
        g = r_ref[0].astype(F32)
        for s in range(1, N_DEV):
            g = g + r_ref[s].astype(F32)
        mn = ADAM_B1 * m_ref[...] + (1.0 - ADAM_B1) * g
        vn = ADAM_B2 * v_ref[...] + (1.0 - ADAM_B2) * (g * g)
        m_hat = mn / c1
        v_hat = vn / c2
        g_out[...] = g
        d_out[...] = -ADAM_LR * (m_hat / (jnp.sqrt(v_hat) + ADAM_EPS) + ADAM_WD * w_ref[...])
        m_out[...] = mn
        v_out[...] = vn

    blk = pl.BlockSpec((tr, lanes), lambda i: (i, 0))
    shp = jax.ShapeDtypeStruct((rows, lanes), F32)
    return pl.pallas_call(
        body, name=f"sum_adamw_{rows}x{lanes}", grid=(rows // tr,),
        in_specs=[pl.BlockSpec((N_DEV, tr, lanes), lambda i: (0, i, 0)), blk, blk, blk],
        out_specs=[blk, blk, blk, blk], out_shape=[shp, shp, shp, shp],
        compiler_params=_params(dimension_semantics=("parallel",)),
    )(recv, w, m, v)


def _mm_nn(a, b):
    m, k = a.shape
    _, n = b.shape
    tm = _pick(m, (1024, 512, 256, 128))
    tn = _pick(n, (1024, 640, 512, 256, 128))
    tk = _pick(k, (1024, 640, 512, 256, 128))
    nk = k // tk

    def body(a_ref, b_ref, o_ref, acc_ref):
        kk = pl.program_id(2)

        @pl.when(kk == 0)
        def _():
            acc_ref[...] = jnp.zeros_like(acc_ref)

        acc_ref[...] += jnp.dot(a_ref[...].astype(BF16), b_ref[...].astype(BF16), preferred_element_type=F32)

        @pl.when(kk == nk - 1)
        def _():
            o_ref[...] = acc_ref[...]

    return pl.pallas_call(
        body, name=f"mm_nn_{m}x{k}x{n}", grid=(m // tm, n // tn, nk),
        in_specs=[pl.BlockSpec((tm, tk), lambda i, j, kk: (i, kk)), pl.BlockSpec((tk, tn), lambda i, j, kk: (kk, j))],
        out_specs=pl.BlockSpec((tm, tn), lambda i, j, kk: (i, j)),
        out_shape=jax.ShapeDtypeStruct((m, n), F32),
        scratch_shapes=[pltpu.VMEM((tm, tn), F32)],
        compiler_params=_params(dimension_semantics=("parallel", "parallel", "arbitrary")),
    )(a, b)


def _mm_tn(a, g):
    s, k = a.shape
    _, n = g.shape
    tm = _pick(k, (1024, 512, 256, 128))
    tn = _pick(n, (1024, 640, 512, 256, 128))
    ts = _pick(s, (512, 256, 128))
    ns = s // ts

    def body(a_ref, g_ref, o_ref, acc_ref):
        ss = pl.program_id(2)

        @pl.when(ss == 0)
        def _():
            acc_ref[...] = jnp.zeros_like(acc_ref)

        acc_ref[...] += lax.dot_general(a_ref[...].astype(BF16), g_ref[...].astype(BF16),
                                        (((0,), (0,)), ((), ())), preferred_element_type=F32)

        @pl.when(ss == ns - 1)
        def _():
            o_ref[...] = acc_ref[...]

    return pl.pallas_call(
        body, name=f"mm_tn_{s}x{k}x{n}", grid=(k // tm, n // tn, ns),
        in_specs=[pl.BlockSpec((ts, tm), lambda i, j, ss: (ss, i)), pl.BlockSpec((ts, tn), lambda i, j, ss: (ss, j))],
        out_specs=pl.BlockSpec((tm, tn), lambda i, j, ss: (i, j)),
        out_shape=jax.ShapeDtypeStruct((k, n), F32),
        scratch_shapes=[pltpu.VMEM((tm, tn), F32)],
        compiler_params=_params(dimension_semantics=("parallel", "parallel", "arbitrary")),
    )(a, g)


@jax.custom_vjp
def mm(a, w):
    return _mm_nn(a, w.astype(BF16))


def _mm_fwd(a, w):
    wb = w.astype(BF16)
    return _mm_nn(a, wb), (a, wb)


def _mm_bwd(res, g):
    a, wb = res
    return _mm_nn(g, wb.T), _mm_tn(a, g)


mm.defvjp(_mm_fwd, _mm_bwd)


def _make_in_proj(widths):
    cuts = [sum(widths[:i]) for i in range(len(widths) + 1)]

    @jax.custom_vjp
    def in_proj(x, w):
        return fwd(x, w)[0]

    def fwd(x, w):
        xb, wb = x.astype(BF16), w.astype(BF16)
        return tuple(_mm_nn(xb, wb[:, a:b]) for a, b in zip(cuts[:-1], cuts[1:])), (xb, wb)

    def bwd(res, gs):
        xb, wb = res
        g = jnp.concatenate(gs, axis=1)
        return _mm_nn(g, wb.T), _mm_tn(xb, g)

    in_proj.defvjp(fwd, bwd)
    return in_proj


EVEN_GROUPS = (256, 128, 128, 512, 512, 512, 128, 1024)
ODD_GROUPS = (1024, 256, 256, 1024)
even_in_proj = _make_in_proj(EVEN_GROUPS)
odd_in_proj = _make_in_proj(ODD_GROUPS)

SHARD_PAD = 384


def _source_columns(kind):
    if kind == "even":
        src = [list(range(0, 384)), [-1] * 64, list(range(384, 416)), [-1] * 32, list(range(416, 1952)),
               list(range(1952, 1960)), [-1] * 120, list(range(1960, 2984))]
        return sum(src, []), 373
    q0, k0, v0, g0 = 0, 1024, 1152, 1280
    dup = lambda base: [base + 64 * g + c for g in range(SWA_KV_HEADS) for _ in range(2) for c in range(64)]
    return list(range(q0, k0)) + dup(k0) + dup(v0) + list(range(g0, 2304)), 288


def _selection(kind):
    src, shard = _source_columns(kind)
    cat = jnp.asarray([s + (SHARD_PAD - shard) * (s // shard) if s >= 0 else -1 for s in src], jnp.int32)
    rows = lax.broadcasted_iota(jnp.int32, (N_DEV * SHARD_PAD, len(src)), 0)
    return (rows == cat[None, :]).astype(BF16)


@functools.partial(jax.custom_vjp, nondiff_argnums=(1,))
def relayout(wcat, kind):
    return _mm_nn(wcat, _selection(kind))


def _relayout_bwd(kind, _, g):
    return (_mm_nn(g, _selection(kind).T),)


relayout.defvjp(lambda wcat, kind: (_mm_nn(wcat, _selection(kind)), None), _relayout_bwd)


def _row_block(s):
    return _pick(s, (512, 256, 128, 64, 32, 16, 8))


def _rms_fwd_call(x, g):
    s, k = x.shape
    tr = _row_block(s)

    def body(x_ref, g_ref, o_ref):
        xv = x_ref[...]
        r = lax.rsqrt(jnp.mean(xv * xv, axis=-1, keepdims=True) + RMS_EPS)
        o_ref[...] = xv * r * g_ref[...]

    return pl.pallas_call(
        body, name=f"rms_fwd_{k}", grid=(s // tr,),
        in_specs=[pl.BlockSpec((tr, k), lambda i: (i, 0)), pl.BlockSpec((1, k), lambda i: (0, 0))],
        out_specs=pl.BlockSpec((tr, k), lambda i: (i, 0)), out_shape=jax.ShapeDtypeStruct((s, k), F32),
        compiler_params=_params(dimension_semantics=("parallel",)),
    )(x, g.reshape(1, k))


def _rms_bwd_call(x, g, dy):
    s, k = x.shape
    tr = _row_block(s)

    def body(x_ref, g_ref, dy_ref, dx_ref, dg_ref):
        @pl.when(pl.program_id(0) == 0)
        def _():
            dg_ref[...] = jnp.zeros_like(dg_ref)

        xv = x_ref[...]
        r = lax.rsqrt(jnp.mean(xv * xv, axis=-1, keepdims=True) + RMS_EPS)
        xh = xv * r
        dyv = dy_ref[...]
        dg_ref[...] += jnp.sum(dyv * xh, axis=0, keepdims=True)
        dxh = dyv * g_ref[...]
        dx_ref[...] = r * (dxh - xh * jnp.mean(dxh * xh, axis=-1, keepdims=True))

    dx, dg = pl.pallas_call(
        body, name=f"rms_bwd_{k}", grid=(s // tr,),
        in_specs=[pl.BlockSpec((tr, k), lambda i: (i, 0)), pl.BlockSpec((1, k), lambda i: (0, 0)),
                  pl.BlockSpec((tr, k), lambda i: (i, 0))],
        out_specs=[pl.BlockSpec((tr, k), lambda i: (i, 0)), pl.BlockSpec((1, k), lambda i: (0, 0))],
        out_shape=[jax.ShapeDtypeStruct((s, k), F32), jax.ShapeDtypeStruct((1, k), F32)],
        compiler_params=_params(dimension_semantics=("arbitrary",)),
    )(x, g.reshape(1, k), dy)
    return dx, dg.reshape(k)


@jax.custom_vjp
def rms_norm(x, g):
    return _rms_fwd_call(x, g)


rms_norm.defvjp(lambda x, g: (_rms_fwd_call(x, g), (x, g)), lambda res, dy: _rms_bwd_call(res[0], res[1], dy))


def _ln_fwd_call(x, y, g, b):
    s, k = x.shape
    tr = _row_block(s)

    def body(x_ref, y_ref, g_ref, b_ref, o_ref):
        u = ALPHA * x_ref[...] + y_ref[...]
        mu = jnp.mean(u, axis=-1, keepdims=True)
        d = u - mu
        var = jnp.mean(d * d, axis=-1, keepdims=True)
        o_ref[...] = d * lax.rsqrt(var + LN_EPS) * g_ref[...] + b_ref[...]

    row = pl.BlockSpec((tr, k), lambda i: (i, 0))
    vec = pl.BlockSpec((1, k), lambda i: (0, 0))
    return pl.pallas_call(
        body, name="ln_fwd", grid=(s // tr,), in_specs=[row, row, vec, vec], out_specs=row,
        out_shape=jax.ShapeDtypeStruct((s, k), F32), compiler_params=_params(dimension_semantics=("parallel",)),
    )(x, y, g.reshape(1, k), b.reshape(1, k))


def _ln_bwd_call(x, y, g, do):
    s, k = x.shape
    tr = _row_block(s)

    def body(x_ref, y_ref, g_ref, do_ref, dx_ref, dy_ref, dg_ref, db_ref):
        @pl.when(pl.program_id(0) == 0)
        def _():
            dg_ref[...] = jnp.zeros_like(dg_ref)
            db_ref[...] = jnp.zeros_like(db_ref)

        u = ALPHA * x_ref[...] + y_ref[...]
        mu = jnp.mean(u, axis=-1, keepdims=True)
        d = u - mu
        r = lax.rsqrt(jnp.mean(d * d, axis=-1, keepdims=True) + LN_EPS)
        xh = d * r
        dov = do_ref[...]
        dg_ref[...] += jnp.sum(dov * xh, axis=0, keepdims=True)
        db_ref[...] += jnp.sum(dov, axis=0, keepdims=True)
        dxh = dov * g_ref[...]
        du = r * (dxh - jnp.mean(dxh, axis=-1, keepdims=True) - xh * jnp.mean(dxh * xh, axis=-1, keepdims=True))
        dy_ref[...] = du
        dx_ref[...] = ALPHA * du

    row = pl.BlockSpec((tr, k), lambda i: (i, 0))
    vec = pl.BlockSpec((1, k), lambda i: (0, 0))
    dx, dy, dg, db = pl.pallas_call(
        body, name="ln_bwd", grid=(s // tr,), in_specs=[row, row, vec, row], out_specs=[row, row, vec, vec],
        out_shape=[jax.ShapeDtypeStruct((s, k), F32), jax.ShapeDtypeStruct((s, k), F32),
                   jax.ShapeDtypeStruct((1, k), F32), jax.ShapeDtypeStruct((1, k), F32)],
        compiler_params=_params(dimension_semantics=("arbitrary",)),
    )(x, y, g.reshape(1, k), do)
    return dx, dy, dg.reshape(k), db.reshape(k)


@jax.custom_vjp
def ln_res(x, y, g, b):
    return _ln_fwd_call(x, y, g, b)


ln_res.defvjp(lambda x, y, g, b: (_ln_fwd_call(x, y, g, b), (x, y, g)),
              lambda res, do: _ln_bwd_call(res[0], res[1], res[2], do))


def _rope_call(x, c, s1, s2, hs):
    s, w = x.shape
    tr = _row_block(s)
    nb = w // LANES

    def body(x_ref, c_ref, s1_ref, s2_ref, o_ref):
        cv, s1v, s2v = c_ref[...], s1_ref[...], s2_ref[...]
        for cb in range(nb):
            xb = x_ref[:, cb * LANES:(cb + 1) * LANES]
            o_ref[:, cb * LANES:(cb + 1) * LANES] = (
                xb * cv + pltpu.roll(xb, LANES - hs, 1) * s1v + pltpu.roll(xb, hs, 1) * s2v)

    row = pl.BlockSpec((tr, w), lambda i: (i, 0))
    tab = pl.BlockSpec((tr, LANES), lambda i: (i, 0))
    return pl.pallas_call(
        body, name=f"rope_{w}_{hs}", grid=(s // tr,), in_specs=[row, tab, tab, tab], out_specs=row,
        out_shape=jax.ShapeDtypeStruct((s, w), F32), compiler_params=_params(dimension_semantics=("parallel",)),
    )(x, c, s1, s2)


@functools.partial(jax.custom_vjp, nondiff_argnums=(2,))
def rope(x, tabs, hs):
    return _rope_call(x, tabs[0], tabs[1], tabs[2], hs)


def _rope_fwd(x, tabs, hs):
    return _rope_call(x, tabs[0], tabs[1], tabs[2], hs), tabs


def _rope_bwd(hs, tabs, dy):
    return _rope_call(dy, tabs[0], -tabs[1], -tabs[2], hs), jax.tree.map(jnp.zeros_like, tabs)


rope.defvjp(_rope_fwd, _rope_bwd)


def _rope_tables(s, layout):
    pos = jnp.arange(s, dtype=F32)[:, None]
    lane = jnp.arange(LANES)
    if layout == "mla":
        dim, hs = MLA_ROPE, MLA_ROPE // 2
        r = lane - MLA_NOPE
        active = (r >= 0) & (r < MLA_ROPE)
    else:
        dim, hs = SWA_DIM, SWA_DIM // 2
        r = lane % SWA_DIM
        active = jnp.ones_like(lane, dtype=bool)
    f = jnp.where(active, r % hs, 0)
    inv = ROPE_THETA ** (-(2.0 * f.astype(F32)) / dim)
    ang = pos * inv[None, :]
    cos, sin = jnp.cos(ang), jnp.sin(ang)
    first = (active & (r < hs))[None, :]
    second = (active & (r >= hs))[None, :]
    c = jnp.where(active[None, :], cos, 1.0)
    s1 = jnp.where(first, -sin, 0.0)
    s2 = jnp.where(second, sin, 0.0)
    return (c, s1, s2), hs


def _gate_fwd_call(o_parts, gate):
    s, w = gate.shape
    tr = _row_block(s)
    widths = [o.shape[1] for o in o_parts]

    def body(*refs):
        o_refs, g_ref, z_ref = refs[:len(widths)], refs[len(widths)], refs[len(widths) + 1]
        off = 0
        for o_ref, wd in zip(o_refs, widths):
            gv = g_ref[:, off:off + wd]
            z_ref[:, off:off + wd] = o_ref[...] * (gv * jax.nn.sigmoid(gv))
            off += wd

    specs = [pl.BlockSpec((tr, wd), lambda i: (i, 0)) for wd in widths]
    row = pl.BlockSpec((tr, w), lambda i: (i, 0))
    return pl.pallas_call(
        body, name=f"gate_fwd_{len(widths)}", grid=(s // tr,), in_specs=specs + [row], out_specs=row,
        out_shape=jax.ShapeDtypeStruct((s, w), F32), compiler_params=_params(dimension_semantics=("parallel",)),
    )(*o_parts, gate)


def _gate_bwd_call(o_parts, gate, dz):
    s, w = gate.shape
    tr = _row_block(s)
    widths = [o.shape[1] for o in o_parts]
    n = len(widths)

    def body(*refs):
        o_refs, g_ref, dz_ref = refs[:n], refs[n], refs[n + 1]
        do_refs, dg_ref = refs[n + 2:2 * n + 2], refs[2 * n + 2]
        off = 0
        for o_ref, do_ref, wd in zip(o_refs, do_refs, widths):
            gv = g_ref[:, off:off + wd]
            sg = jax.nn.sigmoid(gv)
            dzv = dz_ref[:, off:off + wd]
            do_ref[...] = dzv * (gv * sg)
            dg_ref[:, off:off + wd] = dzv * o_ref[...] * (sg * (1.0 + gv * (1.0 - sg)))
            off += wd

    specs = [pl.BlockSpec((tr, wd), lambda i: (i, 0)) for wd in widths]
    row = pl.BlockSpec((tr, w), lambda i: (i, 0))
    outs = pl.pallas_call(
        body, name=f"gate_bwd_{n}", grid=(s // tr,), in_specs=specs + [row, row], out_specs=specs + [row],
        out_shape=[jax.ShapeDtypeStruct((s, wd), F32) for wd in widths] + [jax.ShapeDtypeStruct((s, w), F32)],
        compiler_params=_params(dimension_semantics=("parallel",)),
    )(*o_parts, gate, dz)
    return tuple(outs[:n]), outs[n]


@jax.custom_vjp
def gate_mul(o_parts, gate):
    return _gate_fwd_call(o_parts, gate)


gate_mul.defvjp(lambda o_parts, gate: (_gate_fwd_call(o_parts, gate), (o_parts, gate)),
                lambda res, dz: _gate_bwd_call(res[0], res[1], dz))


def _loss_call(y, t):
    s, k = y.shape
    tr = _row_block(s)
    nsteps = s // tr

    def body(y_ref, t_ref, l_ref, dy_ref, acc_ref):
        i = pl.program_id(0)

        @pl.when(i == 0)
        def _():
            acc_ref[...] = jnp.zeros_like(acc_ref)

        d = y_ref[...] - t_ref[...]
        dy_ref[...] = d / k
        acc_ref[...] += jnp.sum(d * d, axis=0, keepdims=True)

        @pl.when(i == nsteps - 1)
        def _():
            tot = jnp.sum(acc_ref[...], axis=1, keepdims=True) * (0.5 / k)
            l_ref[...] = jnp.broadcast_to(tot, l_ref.shape)

    row = pl.BlockSpec((tr, k), lambda i: (i, 0))
    return pl.pallas_call(
        body, name="loss", grid=(nsteps,), in_specs=[row, row],
        out_specs=[pl.BlockSpec((1, LANES), lambda i: (0, 0)), row],
        out_shape=[jax.ShapeDtypeStruct((1, LANES), F32), jax.ShapeDtypeStruct((s, k), F32)],
        scratch_shapes=[pltpu.VMEM((1, k), F32)], compiler_params=_params(dimension_semantics=("arbitrary",)),
    )(y, t)


@jax.custom_vjp
def mse_loss(y, t):
    return _loss_call(y, t)[0][0, 0]


def _mse_fwd(y, t):
    l, dy = _loss_call(y, t)
    return l[0, 0], (dy, t)


mse_loss.defvjp(_mse_fwd, lambda res, g: (g * res[0], jnp.zeros_like(res[1])))


def _scan_call(x, b, mode):
    s, w = x.shape
    nt = s // 8

    def tile_scan(t):
        row = lax.broadcasted_iota(jnp.int32, (8, w), 0)
        for sh in (1, 2, 4):
            t = t + jnp.where(row >= sh, pltpu.roll(t, sh, 0), 0.0)
        return t

    def body(x_ref, b_ref, o_ref):
        def step(i, carry):
            rows = pl.ds(pl.multiple_of(i * 8, 8), 8)
            t = x_ref[rows, :]
            if mode == "fwd":
                t = jax.nn.log_sigmoid(t + b_ref[...])
            t = tile_scan(t) + carry
            o_ref[rows, :] = t
            return t[7:8, :]

        total = lax.fori_loop(0, nt, step, jnp.zeros((1, w), F32))
        if mode == "rev":
            def fix(i, c):
                rows = pl.ds(pl.multiple_of(i * 8, 8), 8)
                o_ref[rows, :] = total - o_ref[rows, :] + x_ref[rows, :]
                return c
            lax.fori_loop(0, nt, fix, 0)

    full = pl.BlockSpec((s, w), lambda: (0, 0))
    return pl.pallas_call(
        body, name=f"scan_{mode}", in_specs=[full, pl.BlockSpec((1, w), lambda: (0, 0))], out_specs=full,
        out_shape=jax.ShapeDtypeStruct((s, w), F32), compiler_params=_params(),
    )(x, b)


def _fox_dlogit_call(x, b, dlogf):
    s, w = x.shape
    tr = _row_block(s)

    def body(x_ref, b_ref, d_ref, dx_ref, db_ref):
        @pl.when(pl.program_id(0) == 0)
        def _():
            db_ref[...] = jnp.zeros_like(db_ref)

        dx = d_ref[...] * jax.nn.sigmoid(-(x_ref[...] + b_ref[...]))
        dx_ref[...] = dx
        db_ref[...] += jnp.sum(dx, axis=0, keepdims=True)

    row = pl.BlockSpec((tr, w), lambda i: (i, 0))
    vec = pl.BlockSpec((1, w), lambda i: (0, 0))
    return pl.pallas_call(
        body, name="fox_dlogit", grid=(s // tr,), in_specs=[row, vec, row], out_specs=[row, vec],
        out_shape=[jax.ShapeDtypeStruct((s, w), F32), jax.ShapeDtypeStruct((1, w), F32)],
        compiler_params=_params(dimension_semantics=("arbitrary",)),
    )(x, b, dlogf)


@jax.custom_vjp
def fox_cum(fl, b):
    return _scan_call(fl, b, "fwd")


def _fox_cum_bwd(res, dcum):
    fl, b = res
    dlogf = _scan_call(dcum, b, "rev")
    return _fox_dlogit_call(fl, b, dlogf)


fox_cum.defvjp(lambda fl, b: (_scan_call(fl, b, "fwd"), (fl, b)), _fox_cum_bwd)


def _lane_col(x, lane_idx):
    lane = lax.broadcasted_iota(jnp.int32, (1, x.shape[1]), 1)
    return jnp.sum(jnp.where(lane == lane_idx, x, 0.0), axis=1, keepdims=True)


def _row_of(x, row_idx):
    row = lax.broadcasted_iota(jnp.int32, (x.shape[0], 1), 0)
    return jnp.sum(jnp.where(row == row_idx, x, 0.0), axis=0, keepdims=True)


def _attn_cfg(mode, s):
    if mode == "swa":
        blk = 256 if s >= 2048 else 128
        return dict(blk=blk, n_outer=SWA_KV_HEADS, pps=4, wide=False, scale=SWA_DIM ** -0.5)
    blk = 512 if s >= 2048 else 128
    if mode == "mla":
        return dict(blk=blk, n_outer=4, pps=1, wide=True, scale=(MLA_NOPE + MLA_ROPE) ** -0.5)
    return dict(blk=blk, n_outer=4, pps=1, wide=False, scale=FOX_DIM ** -0.5)


ROW_CHUNK = 32


def _unrolled(n, body, carry):
    for c in range(n):
        carry = body(c, carry)
    return carry


def _valid_rows(mode, i, jb, blk, r0, rc):
    qpos = i * blk + r0 + lax.broadcasted_iota(jnp.int32, (rc, blk), 0)
    kpos = jb * blk + lax.broadcasted_iota(jnp.int32, (rc, blk), 1)
    ok = kpos <= qpos
    if mode == "swa":
        ok = ok & (qpos - kpos < WINDOW)
    return ok


def _attn_fwd_call(mode, q, k, v, extra):
    s = q.shape[0]
    cfg = _attn_cfg(mode, s)
    blk, n_outer, pps, wide, scale = cfg["blk"], cfg["n_outer"], cfg["pps"], cfg["wide"], cfg["scale"]
    rc = ROW_CHUNK
    nq = s // blk
    swa, fox = mode == "swa", mode == "fox"
    qw = (2 * LANES if wide else LANES) * pps
    kw = 2 * LANES if wide else LANES
    ow = LANES * pps
    reps = blk // LANES

    def body(*refs):
        q_ref, k_ref, v_ref = refs[:3]
        n_in = 3
        if fox:
            cum_ref, cumt_ref = refs[3:5]
            n_in = 5
        if swa:
            sink_ref = refs[3]
            n_in = 4
        o_ref, lse_ref, m_ref, l_ref, acc_ref, a_ref, s_all, p_all, c_all = refs[n_in:]
        p_id, i, j = pl.program_id(0), pl.program_id(1), pl.program_id(2)
        if swa:
            jb, run, first, last = i - 1 + j, (i - 1 + j) >= 0, j == 0, j == 1
        else:
            jb, run, first, last = j, j <= i, j == 0, j == i
        lane = lax.broadcasted_iota(jnp.int32, (1, LANES), 1)
        msk = [lane < HALF, lane >= HALF]

        @pl.when(first)
        def _():
            for hh in range(2 * pps):
                if swa:
                    m_ref[hh] = jnp.broadcast_to(sink_ref[hh:hh + 1, :], (blk, LANES))
                    l_ref[hh] = jnp.ones((blk, LANES), F32)
                else:
                    m_ref[hh] = jnp.full((blk, LANES), NEG, F32)
                    l_ref[hh] = jnp.zeros((blk, LANES), F32)
            acc_ref[...] = jnp.zeros_like(acc_ref)

        def process(masked):
            for pp in range(pps):
                vb = v_ref[...]
                pvs = []
                for h in range(2):
                    hh = 2 * pp + h
                    s_ref, p_ref, c_ref = s_all.at[hh], p_all.at[hh], c_all.at[hh]
                    if wide:
                        qh = q_ref[:, h * LANES:(h + 1) * LANES] * scale
                        kh = k_ref[:, h * LANES:(h + 1) * LANES]
                    else:
                        qh = jnp.where(msk[h], q_ref[:, pp * LANES:(pp + 1) * LANES], 0.0) * scale
                        kh = k_ref[...]
                    s_ref[...] = lax.dot_general(qh.astype(BF16), kh.astype(BF16), (((1,), (1,)), ((), ())),
                                                 preferred_element_type=F32)
                    if fox:
                        head = 2 * p_id + h
                        c_ref[...] = jnp.broadcast_to(_lane_col(cum_ref[...], head), (blk, LANES))
                        ck = _row_of(cumt_ref[...], head)

                    def chunk(c, carry, hh=hh, h=h):
                        r0 = c * rc
                        rows = pl.ds(r0, rc)
                        u = s_ref[rows, :]
                        if fox:
                            u = u - ck
                        if masked:
                            u = jnp.where(_valid_rows(mode, i, jb, blk, r0, rc), u, NEG)
                        m_prev, l_prev = m_ref[hh, rows, :], l_ref[hh, rows, :]
                        m_cur = jnp.max(u, axis=1, keepdims=True)
                        if fox:
                            m_cur = m_cur + c_ref[rows, :]
                        m_next = jnp.maximum(m_prev, m_cur)
                        shift = m_next - c_ref[rows, :] if fox else m_next
                        p = jnp.exp(u - jnp.tile(shift, (1, reps)))
                        alpha = jnp.exp(m_prev - m_next)
                        l_ref[hh, rows, :] = alpha * l_prev + jnp.sum(p, axis=1, keepdims=True)
                        m_ref[hh, rows, :] = m_next
                        a_ref[hh, rows, :] = alpha
                        p_ref[rows, :] = p.astype(BF16)
                        return carry

                    _unrolled(blk // rc, chunk, 0)
                    vh = jnp.where(msk[h], vb, 0.0).astype(BF16)
                    pvs.append(jnp.dot(p_ref[...], vh, preferred_element_type=F32))
                acc_ref[pp] = acc_ref[pp] * jnp.where(msk[0], a_ref[2 * pp], a_ref[2 * pp + 1]) + pvs[0] + pvs[1]

        if swa:
            pl.when(run)(lambda: process(True))
        else:
            pl.when(j < i)(lambda: process(False))
            pl.when(j == i)(lambda: process(True))

        @pl.when(last)
        def _():
            for pp in range(pps):
                l0, l1 = l_ref[2 * pp], l_ref[2 * pp + 1]
                o_ref[:, pp * LANES:(pp + 1) * LANES] = acc_ref[pp] / jnp.where(msk[0], l0, l1)
                lse_ref[:, pp * LANES:(pp + 1) * LANES] = jnp.where(
                    msk[0], m_ref[2 * pp] + jnp.log(l0), m_ref[2 * pp + 1] + jnp.log(l1))

    if swa:
        kv_map = lambda g, i, j: (jnp.maximum(i - 1 + j, 0), g)
        grid = (n_outer, nq, 2)
    else:
        kv_map = lambda p, i, j: (jnp.minimum(j, i), p)
        grid = (n_outer, nq, nq)
    q_map = lambda p, i, j: (i, p)
    in_specs = [pl.BlockSpec((blk, qw), q_map), pl.BlockSpec((blk, kw), kv_map), pl.BlockSpec((blk, LANES), kv_map)]
    args = [q, k, v]
    if fox:
        cum, cumt = extra
        in_specs += [pl.BlockSpec((blk, LANES), lambda p, i, j: (i, 0)),
                     pl.BlockSpec((8, blk), lambda p, i, j: (0, jnp.minimum(j, i)))]
        args += [cum, cumt]
    if swa:
        in_specs += [pl.BlockSpec((8, LANES), lambda g, i, j: (g, 0))]
        args += [extra]
    n_pairs = n_outer * pps
    return pl.pallas_call(
        body, name=f"attn_fwd_{mode}", grid=grid, in_specs=in_specs,
        out_specs=[pl.BlockSpec((blk, ow), q_map), pl.BlockSpec((blk, ow), q_map)],
        out_shape=[jax.ShapeDtypeStruct((s, n_pairs * LANES), F32), jax.ShapeDtypeStruct((s, n_pairs * LANES), F32)],
        scratch_shapes=[pltpu.VMEM((2 * pps, blk, LANES), F32), pltpu.VMEM((2 * pps, blk, LANES), F32),
                        pltpu.VMEM((pps, blk, LANES), F32), pltpu.VMEM((2 * pps, blk, LANES), F32),
                        pltpu.VMEM((2 * pps, blk, blk), F32), pltpu.VMEM((2 * pps, blk, blk), BF16),
                        pltpu.VMEM((2 * pps, blk, LANES), F32)],
        compiler_params=_params(dimension_semantics=("parallel", "parallel", "arbitrary")),
    )(*args)


def _rowdot_call(do, o):
    s, w = o.shape
    tr = _row_block(s)

    def body(do_ref, o_ref, d_ref):
        lane = lax.broadcasted_iota(jnp.int32, (1, LANES), 1)
        low = lane < HALF
        for cb in range(w // LANES):
            sl = slice(cb * LANES, (cb + 1) * LANES)
            prod = do_ref[:, sl] * o_ref[:, sl]
            d0 = jnp.sum(jnp.where(low, prod, 0.0), axis=1, keepdims=True)
            d1 = jnp.sum(jnp.where(low, 0.0, prod), axis=1, keepdims=True)
            d_ref[:, sl] = jnp.where(low, d0, d1)

    row = pl.BlockSpec((tr, w), lambda i: (i, 0))
    return pl.pallas_call(
        body, name=f"rowdot_{w}", grid=(s // tr,), in_specs=[row, row], out_specs=row,
        out_shape=jax.ShapeDtypeStruct((s, w), F32), compiler_params=_params(dimension_semantics=("parallel",)),
    )(do, o)


def _attn_bwd_call(mode, q, k, v, extra, lse, dd, do):
    s = q.shape[0]
    cfg = _attn_cfg(mode, s)
    blk, n_outer, pps, wide, scale = cfg["blk"], cfg["n_outer"], cfg["pps"], cfg["wide"], cfg["scale"]
    rc = ROW_CHUNK
    nq = s // blk
    swa, fox = mode == "swa", mode == "fox"
    qw = (2 * LANES if wide else LANES) * pps
    kw = 2 * LANES if wide else LANES
    ow = LANES * pps
    reps = blk // LANES

    def body(*refs):
        q_ref, k_ref, v_ref, lse_ref, dd_ref, do_ref = refs[:6]
        n_in = 6
        if fox:
            cum_ref, cumt_ref = refs[6:8]
            n_in = 8
        if swa:
            sink_ref = refs[6]
            n_in = 7
        dq_ref, dk_ref, dv_ref = refs[n_in:n_in + 3]
        n_out = n_in + 3
        if fox:
            dck_ref, dcq_ref = refs[n_out:n_out + 2]
            n_out += 2
        if swa:
            dsink_ref = refs[n_out]
            n_out += 1
        dk_acc, dv_acc, s_all, dp_all, p_all, ds_all, e_all, d_all = refs[n_out:n_out + 8]
        if fox:
            dck_acc, rs_all = refs[n_out + 8:n_out + 10]
        p_id, j, ii = pl.program_id(0), pl.program_id(1), pl.program_id(2)
        if swa:
            i, run, first_i, last_i = j + ii, (j + ii) < nq, ii == 0, ii == 1
        else:
            i, run, first_i, last_i = ii, ii >= j, ii == j, ii == nq - 1
        lane = lax.broadcasted_iota(jnp.int32, (1, LANES), 1)
        msk = [lane < HALF, lane >= HALF]

        @pl.when((j == 0) & (ii == 0))
        def _():
            dq_ref[...] = jnp.zeros_like(dq_ref)
            if swa:
                dsink_ref[...] = jnp.zeros_like(dsink_ref)
            if fox:
                dcq_ref[...] = jnp.zeros_like(dcq_ref)

        @pl.when(first_i)
        def _():
            dk_acc[...] = jnp.zeros_like(dk_acc)
            dv_acc[...] = jnp.zeros_like(dv_acc)
            if fox:
                dck_acc[...] = jnp.zeros_like(dck_acc)

        def process(masked):
            rows = pl.ds(pl.multiple_of(i * blk, blk), blk)
            vb = v_ref[...].astype(BF16)
            dv_parts, dk_parts = [], []
            for pp in range(pps):
                psl = slice(pp * LANES, (pp + 1) * LANES)
                lse_blk, dd_blk, do_blk = lse_ref[:, psl], dd_ref[:, psl], do_ref[:, psl]
                dq_pair = []
                for h in range(2):
                    hh = 2 * pp + h
                    s_ref, dp_ref, p_ref, ds_ref = s_all.at[hh], dp_all.at[hh], p_all.at[hh], ds_all.at[hh]
                    e_ref, d_ref = e_all.at[hh], d_all.at[hh]
                    if fox:
                        rs_ref = rs_all.at[hh]
                    if wide:
                        hsl = slice(h * LANES, (h + 1) * LANES)
                        qh = (q_ref[:, hsl] * scale).astype(BF16)
                        kh = k_ref[:, hsl].astype(BF16)
                    else:
                        qh = (jnp.where(msk[h], q_ref[:, psl], 0.0) * scale).astype(BF16)
                        kh = k_ref[...].astype(BF16)
                    s_ref[...] = lax.dot_general(qh, kh, (((1,), (1,)), ((), ())), preferred_element_type=F32)
                    do_h = jnp.where(msk[h], do_blk, 0.0).astype(BF16)
                    dp_ref[...] = lax.dot_general(do_h, vb, (((1,), (1,)), ((), ())), preferred_element_type=F32)
                    lse_h = _lane_col(lse_blk, HALF * h)
                    d_h = _lane_col(dd_blk, HALF * h)
                    e_ref[...] = jnp.broadcast_to(lse_h, (blk, LANES))
                    d_ref[...] = jnp.broadcast_to(d_h, (blk, LANES))
                    if fox:
                        head = 2 * p_id + h
                        e_ref[...] = e_ref[...] - jnp.broadcast_to(_lane_col(cum_ref[...], head), (blk, LANES))
                        ck = _row_of(cumt_ref[...], head)

                    def chunk(c, colsum):
                        r0 = c * rc
                        cr = pl.ds(r0, rc)
                        u = s_ref[cr, :]
                        if fox:
                            u = u - ck
                        p = jnp.exp(u - jnp.tile(e_ref[cr, :], (1, reps)))
                        if masked:
                            p = jnp.where(_valid_rows(mode, i, j, blk, r0, rc), p, 0.0)
                        ds = p * (dp_ref[cr, :] - jnp.tile(d_ref[cr, :], (1, reps)))
                        p_ref[cr, :] = p.astype(BF16)
                        ds_ref[cr, :] = ds.astype(BF16)
                        if fox:
                            colsum = colsum + jnp.sum(ds, axis=0, keepdims=True)
                            rs_ref[cr, :] = jnp.broadcast_to(jnp.sum(ds, axis=1, keepdims=True), (rc, LANES))
                        return colsum

                    colsum = _unrolled(blk // rc, chunk, jnp.zeros((1, blk), F32))
                    dv_parts.append(lax.dot_general(p_ref[...], do_h, (((0,), (0,)), ((), ())),
                                                    preferred_element_type=F32))
                    if fox:
                        dck_acc[h:h + 1, :] += -colsum
                        dcq_ref[rows, :] += jnp.where(msk[h], rs_ref[...], 0.0)
                    dq_h = jnp.dot(ds_ref[...], kh, preferred_element_type=F32) * scale
                    dk_h = lax.dot_general(ds_ref[...], qh, (((0,), (0,)), ((), ())), preferred_element_type=F32)
                    if wide:
                        dq_ref[rows, hsl] += dq_h
                        dk_acc[:, hsl] += dk_h
                    else:
                        dq_pair.append(jnp.where(msk[h], dq_h, 0.0))
                        dk_parts.append(dk_h)
                    if swa:
                        @pl.when(first_i)
                        def _():
                            e = jnp.exp(sink_ref[hh:hh + 1, :] - lse_h) * d_h
                            dsink_ref[hh:hh + 1, :] += -jnp.sum(e, axis=0, keepdims=True)
                if not wide:
                    dq_ref[rows, psl] += dq_pair[0] + dq_pair[1]
            dv_acc[...] += functools.reduce(lambda a, b: a + b, dv_parts)
            if not wide:
                dk_acc[...] += functools.reduce(lambda a, b: a + b, dk_parts)

        if swa:
            pl.when(run)(lambda: process(True))
        else:
            pl.when(ii > j)(lambda: process(False))
            pl.when(ii == j)(lambda: process(True))

        @pl.when(last_i)
        def _():
            dk_ref[...] = dk_acc[...]
            dv_ref[...] = dv_acc[...]
            if fox:
                dck_ref[0] = dck_acc[...]

    if swa:
        q_map = lambda g, j, ii: (jnp.minimum(j + ii, nq - 1), g)
        grid = (n_outer, nq, 2)
    else:
        q_map = lambda p, j, ii: (jnp.maximum(ii, j), p)
        grid = (n_outer, nq, nq)
    kv_map = lambda p, j, ii: (j, p)
    in_specs = [pl.BlockSpec((blk, qw), q_map), pl.BlockSpec((blk, kw), kv_map), pl.BlockSpec((blk, LANES), kv_map),
                pl.BlockSpec((blk, ow), q_map), pl.BlockSpec((blk, ow), q_map), pl.BlockSpec((blk, ow), q_map)]
    args = [q, k, v, lse, dd, do]
    n_pairs = n_outer * pps
    out_specs = [pl.BlockSpec((s, qw), lambda p, j, ii: (0, p)), pl.BlockSpec((blk, kw), kv_map),
                 pl.BlockSpec((blk, LANES), kv_map)]
    out_shape = [jax.ShapeDtypeStruct((s, q.shape[1]), F32), jax.ShapeDtypeStruct((s, k.shape[1]), F32),
                 jax.ShapeDtypeStruct((s, v.shape[1]), F32)]
    nh = 2 * pps
    scratch = [pltpu.VMEM((blk, kw), F32), pltpu.VMEM((blk, LANES), F32), pltpu.VMEM((nh, blk, blk), F32),
               pltpu.VMEM((nh, blk, blk), F32), pltpu.VMEM((nh, blk, blk), BF16), pltpu.VMEM((nh, blk, blk), BF16),
               pltpu.VMEM((nh, blk, LANES), F32), pltpu.VMEM((nh, blk, LANES), F32)]
    if fox:
        cum, cumt = extra
        if swa:
            raise AssertionError
        in_specs += [pl.BlockSpec((blk, LANES), lambda p, j, ii: (jnp.maximum(ii, j), 0)),
                     pl.BlockSpec((8, blk), lambda p, j, ii: (0, j))]
        args += [cum, cumt]
        out_specs += [pl.BlockSpec((1, 8, blk), lambda p, j, ii: (p, 0, j)),
                      pl.BlockSpec((s, LANES), lambda p, j, ii: (0, p))]
        out_shape += [jax.ShapeDtypeStruct((n_pairs, 8, s), F32), jax.ShapeDtypeStruct((s, n_pairs * LANES), F32)]
        scratch += [pltpu.VMEM((8, blk), F32), pltpu.VMEM((nh, blk, LANES), F32)]
    if swa:
        in_specs += [pl.BlockSpec((8, LANES), lambda g, j, ii: (g, 0))]
        args += [extra]
        out_specs += [pl.BlockSpec((8, LANES), lambda g, j, ii: (g, 0))]
        out_shape += [jax.ShapeDtypeStruct((SWA_HEADS, LANES), F32)]
    return pl.pallas_call(
        body, name=f"attn_bwd_{mode}", grid=grid, in_specs=in_specs, out_specs=out_specs, out_shape=out_shape,
        scratch_shapes=scratch,
        compiler_params=_params(dimension_semantics=("parallel", "arbitrary", "arbitrary")),
    )(*args)


def _swa_masks(i, blk):
    r = lax.broadcasted_iota(jnp.int32, (blk, blk), 0)
    c = lax.broadcasted_iota(jnp.int32, (blk, blk), 1)
    return (c > r) & (i > 0), c <= r


def _nt(a, b):
    return lax.dot_general(a, b, (((1,), (1,)), ((), ())), preferred_element_type=F32)


def _tn(a, b):
    return lax.dot_general(a, b, (((0,), (0,)), ((), ())), preferred_element_type=F32)


def _swa_fwd_call(q, k, v, sink):
    s = q.shape[0]
    blk, pps, scale = WINDOW, 4, SWA_DIM ** -0.5
    nq = s // blk

    def body(q_ref, kp_ref, ko_ref, vp_ref, vo_ref, sink_ref, o_ref, lse_ref):
        i = pl.program_id(1)
        lane = lax.broadcasted_iota(jnp.int32, (1, LANES), 1)
        msk = [lane < HALF, lane >= HALF]
        ok_prev, ok_own = _swa_masks(i, blk)
        kp, ko = kp_ref[...].astype(BF16), ko_ref[...].astype(BF16)
        vp, vo = vp_ref[...], vo_ref[...]
        for pp in range(pps):
            psl = slice(pp * LANES, (pp + 1) * LANES)
            qp = q_ref[:, psl]
            outs, lses = [], []
            for h in range(2):
                qh = (jnp.where(msk[h], qp, 0.0) * scale).astype(BF16)
                s_p = jnp.where(ok_prev, _nt(qh, kp), NEG)
                s_o = jnp.where(ok_own, _nt(qh, ko), NEG)
                sink_row = sink_ref[2 * pp + h:2 * pp + h + 1, :]
                m = jnp.maximum(jnp.max(jnp.maximum(s_p, s_o), axis=1, keepdims=True), sink_row)
                p_p, p_o = jnp.exp(s_p - m), jnp.exp(s_o - m)
                l = jnp.sum(p_p + p_o, axis=1, keepdims=True) + jnp.exp(sink_row - m)
                pv = (jnp.dot(p_p.astype(BF16), jnp.where(msk[h], vp, 0.0).astype(BF16), preferred_element_type=F32)
                      + jnp.dot(p_o.astype(BF16), jnp.where(msk[h], vo, 0.0).astype(BF16), preferred_element_type=F32))
                outs.append(pv / l)
                lses.append(m + jnp.log(l))
            o_ref[:, psl] = jnp.where(msk[0], outs[0], outs[1])
            lse_ref[:, psl] = jnp.where(msk[0], lses[0], lses[1])

    prev = lambda g, i: (jnp.maximum(i - 1, 0), g)
    own = lambda g, i: (i, g)
    qspec = pl.BlockSpec((blk, pps * LANES), own)
    kspec = lambda m: pl.BlockSpec((blk, LANES), m)
    return pl.pallas_call(
        body, name="swa_fwd", grid=(SWA_KV_HEADS, nq),
        in_specs=[qspec, kspec(prev), kspec(own), kspec(prev), kspec(own), pl.BlockSpec((8, LANES), lambda g, i: (g, 0))],
        out_specs=[qspec, qspec],
        out_shape=[jax.ShapeDtypeStruct(q.shape, F32), jax.ShapeDtypeStruct(q.shape, F32)],
        compiler_params=_params(dimension_semantics=("parallel", "parallel")),
    )(q, k, k, v, v, sink)


def _swa_bwd_call(q, k, v, sink, lse, dd, do):
    s = q.shape[0]
    blk, pps, scale = WINDOW, 4, SWA_DIM ** -0.5
    nq = s // blk

    def body(q_ref, kp_ref, ko_ref, vp_ref, vo_ref, sink_ref, lse_ref, dd_ref, do_ref,
             dq_ref, dk_ref, dv_ref, dsink_ref, ck_ref, cv_ref):
        i = pl.program_id(1)
        lane = lax.broadcasted_iota(jnp.int32, (1, LANES), 1)
        msk = [lane < HALF, lane >= HALF]

        @pl.when(i == 0)
        def _():
            ck_ref[...] = jnp.zeros_like(ck_ref)
            cv_ref[...] = jnp.zeros_like(cv_ref)
            dsink_ref[...] = jnp.zeros_like(dsink_ref)

        @pl.when(i < nq)
        def _():
            ok_prev, ok_own = _swa_masks(i, blk)
            kp, ko = kp_ref[...].astype(BF16), ko_ref[...].astype(BF16)
            vp, vo = vp_ref[...].astype(BF16), vo_ref[...].astype(BF16)
            dkp, dko, dvp, dvo = [], [], [], []
            for pp in range(pps):
                psl = slice(pp * LANES, (pp + 1) * LANES)
                qp, do_blk = q_ref[:, psl], do_ref[:, psl]
                dqs = []
                for h in range(2):
                    hh = 2 * pp + h
                    qh = (jnp.where(msk[h], qp, 0.0) * scale).astype(BF16)
                    lse_h = jnp.broadcast_to(_lane_col(lse_ref[:, psl], HALF * h), (blk, LANES))
                    d_h = jnp.broadcast_to(_lane_col(dd_ref[:, psl], HALF * h), (blk, LANES))
                    p_p = jnp.where(ok_prev, jnp.exp(_nt(qh, kp) - lse_h), 0.0)
                    p_o = jnp.where(ok_own, jnp.exp(_nt(qh, ko) - lse_h), 0.0)
                    do_h = jnp.where(msk[h], do_blk, 0.0).astype(BF16)
                    ds_p = (p_p * (_nt(do_h, vp) - d_h)).astype(BF16)
                    ds_o = (p_o * (_nt(do_h, vo) - d_h)).astype(BF16)
                    dq_h = (jnp.dot(ds_p, kp, preferred_element_type=F32)
                            + jnp.dot(ds_o, ko, preferred_element_type=F32)) * scale
                    dqs.append(jnp.where(msk[h], dq_h, 0.0))
                    dkp.append(_tn(ds_p, qh))
                    dko.append(_tn(ds_o, qh))
                    dvp.append(_tn(p_p.astype(BF16), do_h))
                    dvo.append(_tn(p_o.astype(BF16), do_h))
                    sink_row = sink_ref[hh:hh + 1, :]
                    dsink_ref[hh:hh + 1, :] += -jnp.sum(jnp.exp(sink_row - lse_h) * d_h, axis=0, keepdims=True)
                dq_ref[:, psl] = dqs[0] + dqs[1]
            total = lambda parts: functools.reduce(lambda a, b: a + b, parts)
            dk_ref[...] = ck_ref[...] + total(dkp)
            dv_ref[...] = cv_ref[...] + total(dvp)
            ck_ref[...] = total(dko)
            cv_ref[...] = total(dvo)

        @pl.when(i == nq)
        def _():
            dk_ref[...] = ck_ref[...]
            dv_ref[...] = cv_ref[...]

    last = nq - 1
    prev = lambda g, i: (jnp.maximum(i - 1, 0), g)
    own = lambda g, i: (jnp.minimum(i, last), g)
    qspec = pl.BlockSpec((blk, pps * LANES), own)
    kspec = lambda m: pl.BlockSpec((blk, LANES), m)
    sspec = pl.BlockSpec((8, LANES), lambda g, i: (g, 0))
    return pl.pallas_call(
        body, name="swa_bwd", grid=(SWA_KV_HEADS, nq + 1),
        in_specs=[qspec, kspec(prev), kspec(own), kspec(prev), kspec(own), sspec, qspec, qspec, qspec],
        out_specs=[qspec, kspec(prev), kspec(prev), sspec],
        out_shape=[jax.ShapeDtypeStruct(q.shape, F32), jax.ShapeDtypeStruct(k.shape, F32),
                   jax.ShapeDtypeStruct(v.shape, F32), jax.ShapeDtypeStruct((SWA_HEADS, LANES), F32)],
        scratch_shapes=[pltpu.VMEM((blk, LANES), F32), pltpu.VMEM((blk, LANES), F32)],
        compiler_params=_params(dimension_semantics=("parallel", "arbitrary")),
    )(q, k, k, v, v, sink, lse, dd, do)


def _make_attn(mode):
    swa = mode == "swa"

    @jax.custom_vjp
    def attn(q, k, v, extra):
        return fwd(q, k, v, extra)[0]

    def fwd(q, k, v, extra):
        o, lse = _swa_fwd_call(q, k, v, extra) if swa else _attn_fwd_call(mode, q, k, v, extra)
        return o, (q, k, v, extra, o, lse)

    def bwd(res, do):
        q, k, v, extra, o, lse = res
        dd = _rowdot_call(do, o)
        outs = (_swa_bwd_call(q, k, v, extra, lse, dd, do) if swa
                else _attn_bwd_call(mode, q, k, v, extra, lse, dd, do))
        dq, dk, dv = outs[:3]
        if mode == "fox":
            cum, cumt = extra
            dck = outs[3]
            dcumt = dck[:, :2, :].reshape(FOX_HEADS, -1)
            dcq = outs[4].reshape(-1, FOX_HEADS, HALF)[:, :, 0]
            dextra = (jnp.pad(dcq, ((0, 0), (0, LANES - FOX_HEADS))), dcumt)
        elif mode == "swa":
            dextra = jnp.where(jnp.arange(LANES)[None, :] == 0, outs[3], 0.0)
        else:
            dextra = None
        return dq, dk, dv, dextra

    attn.defvjp(fwd, bwd)
    return attn


attn_mla = _make_attn("mla")
attn_fox = _make_attn("fox")
attn_swa = _make_attn("swa")


def _ukv_layout(w):
    r = w.shape[0]
    w3 = w.reshape(r, MLA_HEADS, MLA_NOPE + MLA_V)
    wk = jnp.pad(w3[:, :, :MLA_NOPE], ((0, 0), (0, 0), (0, LANES - MLA_NOPE))).reshape(r, MLA_HEADS * LANES)
    wv = w3[:, :, MLA_NOPE:].reshape(r, MLA_HEADS * MLA_V)
    return wk, wv


def _even_layer(x, w_in_cat, q_norm, w_uq_p, kv_norm, w_ukv, b_f, w_out, ln_g, ln_b, tabs_mla):
    cq, ckv, kpe, fq, fk, fv, fl, gate = even_in_proj(x, relayout(w_in_cat, "even"))
    tabs, hs = tabs_mla
    q = rope(mm(rms_norm(cq, q_norm), w_uq_p), tabs, hs)
    ckvn = rms_norm(ckv, kv_norm)
    wk, wv = _ukv_layout(w_ukv)
    kk = mm(ckvn, wk) + jnp.tile(rope(kpe, tabs, hs), (1, MLA_HEADS))
    o_mla = attn_mla(q, kk, mm(ckvn, wv), None)
    cum = fox_cum(fl, jnp.pad(b_f, (0, LANES - FOX_HEADS)).reshape(1, LANES))
    o_fox = attn_fox(fq, fk, fv, (cum, cum[:, :8].T))
    y = mm(gate_mul((o_mla, o_fox), gate), w_out)
    return ln_res(x, y, ln_g, ln_b)


def _odd_layer(x, w_in_cat, sinks, w_out, ln_g, ln_b, tabs_swa):
    q, kd, vd, gate = odd_in_proj(x, relayout(w_in_cat, "odd"))
    tabs, hs = tabs_swa
    q = rope(q, tabs, hs)
    kd = rope(kd, tabs, hs)
    o = attn_swa(q, kd, vd, jnp.broadcast_to(sinks[:, None], (SWA_HEADS, LANES)))
    y = mm(gate_mul((o,), gate), w_out)
    return ln_res(x, y, ln_g, ln_b)


def _local_loss(p, x, target):
    s = x.shape[0]
    tabs_mla = _rope_tables(s, "mla")
    tabs_swa = _rope_tables(s, "swa")
    for layer in range(DEPTH):
        j = layer // 2
        if layer % 2 == 0:
            x = _even_layer(x, p["even_w_in"][j], p["even_q_norm"][j], p["even_w_uq"][j], p["even_kv_norm"][j],
                            p["even_w_ukv"][j], p["even_b_f"][j], p["even_w_out"][j], p["even_ln_g"][j],
                            p["even_ln_b"][j], tabs_mla)
        else:
            x = _odd_layer(x, p["odd_w_in"][j], p["odd_sinks"][j], p["odd_w_out"][j], p["odd_ln_g"][j],
                           p["odd_ln_b"][j], tabs_swa)
    return mse_loss(x, target)


def _pad_rows(flat, mult):
    n = flat.shape[-1]
    per = mult * LANES
    padded = -(-n // per) * per
    if padded != n:
        flat = jnp.pad(flat, [(0, 0)] * (flat.ndim - 1) + [(0, padded - n)])
    return flat.reshape(flat.shape[:-1] + (padded // LANES, LANES))


def _pad_last(a, width):
    if a.shape[-1] == width:
        return a
    return jnp.pad(a, [(0, 0)] * (a.ndim - 1) + [(0, width - a.shape[-1])])


def _join(slots, axis):
    shp = list(slots.shape[1:])
    shp[axis] *= N_DEV
    return jnp.moveaxis(slots, 0, axis).reshape(shp)


def _split(full, axis):
    shp = full.shape
    t = full.reshape(shp[:axis] + (N_DEV, shp[axis] // N_DEV) + shp[axis + 1:])
    return jnp.moveaxis(t, axis, 0)


PAD_TO = {"even_w_in": SHARD_PAD, "even_w_uq": LANES, "odd_w_in": SHARD_PAD}


def kernel(x, even_w_in, even_q_norm, even_w_uq, even_kv_norm, even_w_ukv, even_b_f, even_w_out, even_ln_g, even_ln_b, odd_w_in, odd_sinks, odd_w_out, odd_ln_g, odd_ln_b, loss_target, m_even_w_in, m_even_q_norm, m_even_w_uq, m_even_kv_norm, m_even_w_ukv, m_even_b_f, m_even_w_out, m_even_ln_g, m_even_ln_b, m_odd_w_in, m_odd_sinks, m_odd_w_out, m_odd_ln_g, m_odd_ln_b, v_even_w_in, v_even_q_norm, v_even_w_uq, v_even_kv_norm, v_even_w_ukv, v_even_b_f, v_even_w_out, v_even_ln_g, v_even_ln_b, v_odd_w_in, v_odd_sinks, v_odd_w_out, v_odd_ln_g, v_odd_ln_b):
    w = dict(even_w_in=even_w_in, even_q_norm=even_q_norm, even_w_uq=even_w_uq, even_kv_norm=even_kv_norm,
             even_w_ukv=even_w_ukv, even_b_f=even_b_f, even_w_out=even_w_out, even_ln_g=even_ln_g, even_ln_b=even_ln_b,
             odd_w_in=odd_w_in, odd_sinks=odd_sinks, odd_w_out=odd_w_out, odd_ln_g=odd_ln_g, odd_ln_b=odd_ln_b)
    mom = dict(even_w_in=m_even_w_in, even_q_norm=m_even_q_norm, even_w_uq=m_even_w_uq, even_kv_norm=m_even_kv_norm,
               even_w_ukv=m_even_w_ukv, even_b_f=m_even_b_f, even_w_out=m_even_w_out, even_ln_g=m_even_ln_g,
               even_ln_b=m_even_ln_b, odd_w_in=m_odd_w_in, odd_sinks=m_odd_sinks, odd_w_out=m_odd_w_out,
               odd_ln_g=m_odd_ln_g, odd_ln_b=m_odd_ln_b)
    vel = dict(even_w_in=v_even_w_in, even_q_norm=v_even_q_norm, even_w_uq=v_even_w_uq, even_kv_norm=v_even_kv_norm,
               even_w_ukv=v_even_w_ukv, even_b_f=v_even_b_f, even_w_out=v_even_w_out, even_ln_g=v_even_ln_g,
               even_ln_b=v_even_ln_b, odd_w_in=v_odd_w_in, odd_sinks=v_odd_sinks, odd_w_out=v_odd_w_out,
               odd_ln_g=v_odd_ln_g, odd_ln_b=v_odd_ln_b)
    sharded = BIG + SMALL_SHARDED
    padded = lambda d, n: _pad_last(d[n], PAD_TO.get(n, d[n].shape[-1]))

    mine = [padded(w, n).astype(BF16) for n in BIG] + [w[n] for n in SMALL_SHARDED]
    gathered = _all_gather(mine, "all_gather")
    full = {n: _join(g, SHARD_AXIS[n]).astype(F32) for n, g in zip(sharded, gathered)}
    for n in REPL:
        full[n] = w[n]

    loss_local, (grads, grad_x) = jax.value_and_grad(_local_loss, argnums=(0, 1))(full, x[0], loss_target[0])
    loss = lax.psum(loss_local, AXES)

    repl_rows = _pad_rows(jnp.concatenate([grads[n].reshape(-1) for n in REPL]), 8)
    parts = [_split(grads[n], SHARD_AXIS[n]).astype(BF16 if n in BIG else F32) for n in sharded]
    parts.append(jnp.broadcast_to(repl_rows[None], (N_DEV,) + repl_rows.shape))
    recv = _exchange(parts, "grad_exchange")

    g_out, d_out, m_out, v_out = {}, {}, {}, {}
    for n, r in zip(sharded, recv[:-1]):
        cols = r.shape[-1]
        flat2 = lambda d: padded(d, n).reshape(-1, cols)
        outs = _sum_adamw(r.reshape(N_DEV, -1, cols), flat2(w), flat2(mom), flat2(vel))
        for dst, o in zip((g_out, d_out, m_out, v_out), outs):
            dst[n] = o.reshape(w[n].shape[:-1] + (cols,))[..., :w[n].shape[-1]]
    pack = lambda d: _pad_rows(jnp.concatenate([d[n].reshape(-1) for n in REPL]), 8)
    outs = _sum_adamw(recv[-1], pack(w), pack(mom), pack(vel))
    for dst, o in zip((g_out, d_out, m_out, v_out), outs):
        flat, off = o.reshape(-1), 0
        for n in REPL:
            size = math.prod(w[n].shape)
            dst[n] = flat[off:off + size].reshape(w[n].shape)
            off += size
    return (loss, grad_x[None], *[g_out[n] for n in WEIGHTS], *[d_out[n] for n in WEIGHTS],
            *[m_out[n] for n in WEIGHTS], *[v_out[n] for n in WEIGHTS])
```

```python
import functools
import math

import jax
import jax.numpy as jnp
from jax import lax
from jax.experimental import pallas as pl
from jax.experimental.pallas import tpu as pltpu

F32 = jnp.float32
BF16 = jnp.bfloat16
LANES = 128
HALF = 64
N_DEV = 8
AXES = ("x", "y", "c")
VMEM_LIMIT = 48 * 1024 * 1024

D_MODEL = 1024
DEPTH = 4
ROPE_THETA = 10000.0
MLA_HEADS, MLA_NOPE, MLA_ROPE, MLA_V, MLA_Q_RANK, MLA_KV_RANK = 8, 64, 32, 64, 256, 128
FOX_HEADS, FOX_DIM = 8, 64
SWA_HEADS, SWA_KV_HEADS, SWA_DIM, WINDOW = 16, 2, 64, 128
RMS_EPS, LN_EPS = 1e-6, 1e-5
ALPHA = (2 * DEPTH) ** 0.25
ADAM_LR, ADAM_B1, ADAM_B2, ADAM_EPS, ADAM_WD, ADAM_STEP = 0.001, 0.9, 0.999, 1e-08, 0.01, 10
NEG = -1e30

WEIGHTS = ["even_w_in", "even_q_norm", "even_w_uq", "even_kv_norm", "even_w_ukv", "even_b_f", "even_w_out",
           "even_ln_g", "even_ln_b", "odd_w_in", "odd_sinks", "odd_w_out", "odd_ln_g", "odd_ln_b"]
SHARD_AXIS = {"even_w_in": 2, "even_w_uq": 2, "even_w_ukv": 2, "even_w_out": 1, "odd_w_in": 2, "odd_w_out": 1,
              "odd_ln_g": 1, "odd_ln_b": 1, "even_q_norm": None, "even_kv_norm": None, "even_b_f": None,
              "even_ln_g": None, "even_ln_b": None, "odd_sinks": None}
BIG = ["even_w_in", "even_w_uq", "even_w_ukv", "even_w_out", "odd_w_in", "odd_w_out"]
SMALL_SHARDED = ["odd_ln_g", "odd_ln_b"]
REPL = [n for n in WEIGHTS if SHARD_AXIS[n] is None]


def _pick(n, cands):
    for c in cands:
        if n % c == 0:
            return c
    return n


def _params(**kw):
    return pltpu.CompilerParams(vmem_limit_bytes=VMEM_LIMIT, **kw)


def _me():
    return lax.axis_index("x"), lax.axis_index("y"), lax.axis_index("c")


def _peer(k):
    x, y, c = _me()
    px = 1 - x if (k >> 2) & 1 else x
    py = 1 - y if (k >> 1) & 1 else y
    pc = 1 - c if k & 1 else c
    return px, py, pc


def _lin(p):
    return 4 * p[0] + 2 * p[1] + p[2]


def _comm_call(body, n, out_shape, args, name):
    any_spec = pl.BlockSpec(memory_space=pl.ANY)
    return pl.pallas_call(
        body, name=name, out_shape=out_shape, in_specs=[any_spec] * n, out_specs=[any_spec] * n,
        scratch_shapes=[pltpu.SemaphoreType.DMA((n, N_DEV - 1)), pltpu.SemaphoreType.DMA((n, N_DEV - 1)),
                        pltpu.SemaphoreType.DMA((n,))],
    )(*args)


def _all_gather(xs, name):
    n = len(xs)

    def body(*refs):
        x_refs, out_refs = refs[:n], refs[n:2 * n]
        send_sems, recv_sems, local_sems = refs[2 * n:]
        me = _lin(_me())
        local = [pltpu.make_async_copy(x_refs[a], out_refs[a].at[me], local_sems.at[a]) for a in range(n)]
        for cp in local:
            cp.start()
        sends = []
        for k in range(1, N_DEV):
            for a in range(n):
                cp = pltpu.make_async_remote_copy(
                    src_ref=x_refs[a], dst_ref=out_refs[a].at[me], send_sem=send_sems.at[a, k - 1],
                    recv_sem=recv_sems.at[a, k - 1], device_id=_peer(k), device_id_type=pl.DeviceIdType.MESH)
                cp.start()
                sends.append(cp)
        for k in range(1, N_DEV):
            for a in range(n):
                pltpu.make_async_remote_copy(
                    src_ref=x_refs[a], dst_ref=out_refs[a].at[_lin(_peer(k))], send_sem=send_sems.at[a, k - 1],
                    recv_sem=recv_sems.at[a, k - 1], device_id=_peer(k),
                    device_id_type=pl.DeviceIdType.MESH).wait_recv()
        for cp in sends:
            cp.wait_send()
        for cp in local:
            cp.wait()

    out_shape = [jax.ShapeDtypeStruct((N_DEV,) + x.shape, x.dtype) for x in xs]
    return _comm_call(body, n, out_shape, xs, name)


def _exchange(parts, name):
    n = len(parts)

    def body(*refs):
        p_refs, out_refs = refs[:n], refs[n:2 * n]
        send_sems, recv_sems, local_sems = refs[2 * n:]
        me = _lin(_me())
        local = [pltpu.make_async_copy(p_refs[a].at[me], out_refs[a].at[me], local_sems.at[a]) for a in range(n)]
        for cp in local:
            cp.start()
        sends = []
        for k in range(1, N_DEV):
            peer = _peer(k)
            for a in range(n):
                cp = pltpu.make_async_remote_copy(
                    src_ref=p_refs[a].at[_lin(peer)], dst_ref=out_refs[a].at[me], send_sem=send_sems.at[a, k - 1],
                    recv_sem=recv_sems.at[a, k - 1], device_id=peer, device_id_type=pl.DeviceIdType.MESH)
                cp.start()
                sends.append(cp)
        for k in range(1, N_DEV):
            peer = _peer(k)
            for a in range(n):
                pltpu.make_async_remote_copy(
                    src_ref=p_refs[a].at[_lin(peer)], dst_ref=out_refs[a].at[_lin(peer)],
                    send_sem=send_sems.at[a, k - 1], recv_sem=recv_sems.at[a, k - 1], device_id=peer,
                    device_id_type=pl.DeviceIdType.MESH).wait_recv()
        for cp in sends:
            cp.wait_send()
        for cp in local:
            cp.wait()

    out_shape = [jax.ShapeDtypeStruct(p.shape, p.dtype) for p in parts]
    return _comm_call(body, n, out_shape, parts, name)


_HBM = pl.BlockSpec(memory_space=pltpu.HBM)
_SEM = pl.BlockSpec(memory_space=pltpu.SEMAPHORE)
_EFFECT = pltpu.SideEffectType.DATAFLOW_SIDE_EFFECTING


def _split_start(srcs, slotted, name):
    n = len(srcs)
    lands = [lax.empty(s.shape if slotted else (N_DEV,) + s.shape, s.dtype) for s in srcs]

    def body(*refs):
        src_refs, land_refs = refs[:n], refs[n:2 * n]
        send_sems, recv_sems, token = refs[2 * n], refs[2 * n + 1], refs[-1]
        me = _lin(_me())
        for k in range(1, N_DEV):
            peer = _peer(k)
            for a in range(n):
                pltpu.make_async_remote_copy(
                    src_ref=src_refs[a].at[_lin(peer)] if slotted else src_refs[a], dst_ref=land_refs[a].at[me],
                    send_sem=send_sems.at[a * (N_DEV - 1) + k - 1], recv_sem=recv_sems.at[a * (N_DEV - 1) + k - 1],
                    device_id=peer, device_id_type=pl.DeviceIdType.MESH).start()
        token[...] = jnp.zeros_like(token)

    both = list(srcs) + lands
    outs = pl.pallas_call(
        body, name=name,
        out_shape=(pltpu.SemaphoreType.DMA((n * (N_DEV - 1),)), pltpu.SemaphoreType.DMA((n * (N_DEV - 1),)),
                   *[pltpu.HBM(b.shape, b.dtype) for b in both], jax.ShapeDtypeStruct((8, LANES), F32)),
        in_specs=[_HBM] * (2 * n), out_specs=(_SEM, _SEM, *[_HBM] * (2 * n), pl.BlockSpec(memory_space=pltpu.VMEM)),
        input_output_aliases={a: 2 + a for a in range(2 * n)},
        compiler_params=pltpu.CompilerParams(has_side_effects=_EFFECT),
    )(*[pltpu.with_memory_space_constraint(b, pltpu.HBM) for b in both])
    return outs[0], outs[1], list(outs[2:2 + n]), list(outs[2 + n:2 + 2 * n]), outs[-1]


def _split_wait(send_sems, recv_sems, srcs, lands, after, slotted, name):
    n = len(srcs)

    def body(*refs):
        src_refs, land_refs = refs[:n], refs[n:2 * n]
        send_sems, recv_sems = refs[2 * n], refs[2 * n + 1]
        for k in range(1, N_DEV):
            peer = _peer(k)
            for a in range(n):
                cp = pltpu.make_async_remote_copy(
                    src_ref=src_refs[a].at[_lin(peer)] if slotted else src_refs[a],
                    dst_ref=land_refs[a].at[_lin(peer)], send_sem=send_sems.at[a * (N_DEV - 1) + k - 1],
                    recv_sem=recv_sems.at[a * (N_DEV - 1) + k - 1], device_id=peer,
                    device_id_type=pl.DeviceIdType.MESH)
                cp.wait_send()
                cp.wait_recv()

    both = list(srcs) + list(lands)
    outs = pl.pallas_call(
        body, name=name, out_shape=[pltpu.HBM(b.shape, b.dtype) for b in both],
        in_specs=[_HBM] * (2 * n) + [_SEM, _SEM, pl.BlockSpec(memory_space=pl.ANY)], out_specs=[_HBM] * (2 * n),
        input_output_aliases={a: a for a in range(2 * n)},
        compiler_params=pltpu.CompilerParams(has_side_effects=_EFFECT),
    )(*both, send_sems, recv_sems, after)
    return list(outs[n:])


def _own_slot(lands, own):
    me = _lin(_me())
    return [lax.dynamic_update_slice_in_dim(l, o.astype(l.dtype), me, axis=0) for l, o in zip(lands, own)]


def _sum_adamw(recv, w, m, v):
    _, rows, lanes = recv.shape
    tr = _pick(rows, (256, 128, 64, 32, 16, 8))
    c1 = 1.0 - ADAM_B1 ** ADAM_STEP
    c2 = 1.0 - ADAM_B2 ** ADAM_STEP

    def body(r_ref, w_ref, m_ref, v_ref, g_out, d_out, m_out, v_out):
        g = r_ref[0].astype(F32)
        for s in range(1, N_DEV):
            g = g + r_ref[s].astype(F32)
        mn = ADAM_B1 * m_ref[...] + (1.0 - ADAM_B1) * g
        vn = ADAM_B2 * v_ref[...] + (1.0 - ADAM_B2) * (g * g)
        m_hat = mn / c1
        v_hat = vn / c2
        g_out[...] = g
        d_out[...] = -ADAM_LR * (m_hat / (jnp.sqrt(v_hat) + ADAM_EPS) + ADAM_WD * w_ref[...])
        m_out[...] = mn
        v_out[...] = vn

    blk = pl.BlockSpec((tr, lanes), lambda i: (i, 0))
    shp = jax.ShapeDtypeStruct((rows, lanes), F32)
    return pl.pallas_call(
        body, name=f"sum_adamw_{rows}x{lanes}", grid=(rows // tr,),
        in_specs=[pl.BlockSpec((N_DEV, tr, lanes), lambda i: (0, i, 0)), blk, blk, blk],
        out_specs=[blk, blk, blk, blk], out_shape=[shp, shp, shp, shp],
        compiler_params=_params(dimension_semantics=("parallel",)),
    )(recv, w, m, v)


def _mm_nn(a, b):
    m, k = a.shape
    _, n = b.shape
    tm = _pick(m, (1024, 512, 256, 128))
    tn = _pick(n, (1024, 640, 512, 256, 128))
    tk = _pick(k, (1024, 640, 512, 256, 128))
    nk = k // tk

    def body(a_ref, b_ref, o_ref, acc_ref):
        kk = pl.program_id(2)

        @pl.when(kk == 0)
        def _():
            acc_ref[...] = jnp.zeros_like(acc_ref)

        acc_ref[...] += jnp.dot(a_ref[...].astype(BF16), b_ref[...].astype(BF16), preferred_element_type=F32)

        @pl.when(kk == nk - 1)
        def _():
            o_ref[...] = acc_ref[...]

    return pl.pallas_call(
        body, name=f"mm_nn_{m}x{k}x{n}", grid=(m // tm, n // tn, nk),
        in_specs=[pl.BlockSpec((tm, tk), lambda i, j, kk: (i, kk)), pl.BlockSpec((tk, tn), lambda i, j, kk: (kk, j))],
        out_specs=pl.BlockSpec((tm, tn), lambda i, j, kk: (i, j)),
        out_shape=jax.ShapeDtypeStruct((m, n), F32),
        scratch_shapes=[pltpu.VMEM((tm, tn), F32)],
        compiler_params=_params(dimension_semantics=("parallel", "parallel", "arbitrary")),
    )(a, b)


def _mm_tn(a, g):
    s, k = a.shape
    _, n = g.shape
    tm = _pick(k, (1024, 512, 256, 128))
    tn = _pick(n, (1024, 640, 512, 256, 128))
    ts = _pick(s, (512, 256, 128))
    ns = s // ts

    def body(a_ref, g_ref, o_ref, acc_ref):
        ss = pl.program_id(2)

        @pl.when(ss == 0)
        def _():
            acc_ref[...] = jnp.zeros_like(acc_ref)

        acc_ref[...] += lax.dot_general(a_ref[...].astype(BF16), g_ref[...].astype(BF16),
                                        (((0,), (0,)), ((), ())), preferred_element_type=F32)

        @pl.when(ss == ns - 1)
        def _():
            o_ref[...] = acc_ref[...]

    return pl.pallas_call(
        body, name=f"mm_tn_{s}x{k}x{n}", grid=(k // tm, n // tn, ns),
        in_specs=[pl.BlockSpec((ts, tm), lambda i, j, ss: (ss, i)), pl.BlockSpec((ts, tn), lambda i, j, ss: (ss, j))],
        out_specs=pl.BlockSpec((tm, tn), lambda i, j, ss: (i, j)),
        out_shape=jax.ShapeDtypeStruct((k, n), F32),
        scratch_shapes=[pltpu.VMEM((tm, tn), F32)],
        compiler_params=_params(dimension_semantics=("parallel", "parallel", "arbitrary")),
    )(a, g)


@jax.custom_vjp
def mm(a, w):
    return _mm_nn(a, w.astype(BF16))


def _mm_fwd(a, w):
    wb = w.astype(BF16)
    return _mm_nn(a, wb), (a, wb)


def _mm_bwd(res, g):
    a, wb = res
    return _mm_nn(g, wb.T), _mm_tn(a, g)


mm.defvjp(_mm_fwd, _mm_bwd)


def _make_in_proj(widths):
    cuts = [sum(widths[:i]) for i in range(len(widths) + 1)]

    @jax.custom_vjp
    def in_proj(x, w):
        return fwd(x, w)[0]

    def fwd(x, w):
        xb, wb = x.astype(BF16), w.astype(BF16)
        return tuple(_mm_nn(xb, wb[:, a:b]) for a, b in zip(cuts[:-1], cuts[1:])), (xb, wb)

    def bwd(res, gs):
        xb, wb = res
        g = jnp.concatenate(gs, axis=1)
        return _mm_nn(g, wb.T), _mm_tn(xb, g)

    in_proj.defvjp(fwd, bwd)
    return in_proj


EVEN_GROUPS = (256, 128, 128, 512, 512, 512, 128, 1024)
ODD_GROUPS = (1024, 256, 256, 1024)
even_in_proj = _make_in_proj(EVEN_GROUPS)
odd_in_proj = _make_in_proj(ODD_GROUPS)

SHARD_PAD = 384


def _source_columns(kind):
    if kind == "even":
        src = [list(range(0, 384)), [-1] * 64, list(range(384, 416)), [-1] * 32, list(range(416, 1952)),
               list(range(1952, 1960)), [-1] * 120, list(range(1960, 2984))]
        return sum(src, []), 373
    q0, k0, v0, g0 = 0, 1024, 1152, 1280
    dup = lambda base: [base + 64 * g + c for g in range(SWA_KV_HEADS) for _ in range(2) for c in range(64)]
    return list(range(q0, k0)) + dup(k0) + dup(v0) + list(range(g0, 2304)), 288


def _selection(kind):
    src, shard = _source_columns(kind)
    cat = jnp.asarray([s + (SHARD_PAD - shard) * (s // shard) if s >= 0 else -1 for s in src], jnp.int32)
    rows = lax.broadcasted_iota(jnp.int32, (N_DEV * SHARD_PAD, len(src)), 0)
    return (rows == cat[None, :]).astype(BF16)


@functools.partial(jax.custom_vjp, nondiff_argnums=(1,))
def relayout(wcat, kind):
    return _mm_nn(wcat, _selection(kind))


def _relayout_bwd(kind, _, g):
    return (_mm_nn(g, _selection(kind).T),)


relayout.defvjp(lambda wcat, kind: (_mm_nn(wcat, _selection(kind)), None), _relayout_bwd)


def _row_block(s):
    return _pick(s, (512, 256, 128, 64, 32, 16, 8))


def _rms_fwd_call(x, g):
    s, k = x.shape
    tr = _row_block(s)

    def body(x_ref, g_ref, o_ref):
        xv = x_ref[...]
        r = lax.rsqrt(jnp.mean(xv * xv, axis=-1, keepdims=True) + RMS_EPS)
        o_ref[...] = xv * r * g_ref[...]

    return pl.pallas_call(
        body, name=f"rms_fwd_{k}", grid=(s // tr,),
        in_specs=[pl.BlockSpec((tr, k), lambda i: (i, 0)), pl.BlockSpec((1, k), lambda i: (0, 0))],
        out_specs=pl.BlockSpec((tr, k), lambda i: (i, 0)), out_shape=jax.ShapeDtypeStruct((s, k), F32),
        compiler_params=_params(dimension_semantics=("parallel",)),
    )(x, g.reshape(1, k))


def _rms_bwd_call(x, g, dy):
    s, k = x.shape
    tr = _row_block(s)

    def body(x_ref, g_ref, dy_ref, dx_ref, dg_ref):
        @pl.when(pl.program_id(0) == 0)
        def _():
            dg_ref[...] = jnp.zeros_like(dg_ref)

        xv = x_ref[...]
        r = lax.rsqrt(jnp.mean(xv * xv, axis=-1, keepdims=True) + RMS_EPS)
        xh = xv * r
        dyv = dy_ref[...]
        dg_ref[...] += jnp.sum(dyv * xh, axis=0, keepdims=True)
        dxh = dyv * g_ref[...]
        dx_ref[...] = r * (dxh - xh * jnp.mean(dxh * xh, axis=-1, keepdims=True))

    dx, dg = pl.pallas_call(
        body, name=f"rms_bwd_{k}", grid=(s // tr,),
        in_specs=[pl.BlockSpec((tr, k), lambda i: (i, 0)), pl.BlockSpec((1, k), lambda i: (0, 0)),
                  pl.BlockSpec((tr, k), lambda i: (i, 0))],
        out_specs=[pl.BlockSpec((tr, k), lambda i: (i, 0)), pl.BlockSpec((1, k), lambda i: (0, 0))],
        out_shape=[jax.ShapeDtypeStruct((s, k), F32), jax.ShapeDtypeStruct((1, k), F32)],
        compiler_params=_params(dimension_semantics=("arbitrary",)),
    )(x, g.reshape(1, k), dy)
    return dx, dg.reshape(k)


@jax.custom_vjp
def rms_norm(x, g):
    return _rms_fwd_call(x, g)


rms_norm.defvjp(lambda x, g: (_rms_fwd_call(x, g), (x, g)), lambda res, dy: _rms_bwd_call(res[0], res[1], dy))


def _ln_fwd_call(x, y, g, b):
    s, k = x.shape
    tr = _row_block(s)

    def body(x_ref, y_ref, g_ref, b_ref, o_ref):
        u = ALPHA * x_ref[...] + y_ref[...]
        mu = jnp.mean(u, axis=-1, keepdims=True)
        d = u - mu
        var = jnp.mean(d * d, axis=-1, keepdims=True)
        o_ref[...] = d * lax.rsqrt(var + LN_EPS) * g_ref[...] + b_ref[...]

    row = pl.BlockSpec((tr, k), lambda i: (i, 0))
    vec = pl.BlockSpec((1, k), lambda i: (0, 0))
    return pl.pallas_call(
        body, name="ln_fwd", grid=(s // tr,), in_specs=[row, row, vec, vec], out_specs=row,
        out_shape=jax.ShapeDtypeStruct((s, k), F32), compiler_params=_params(dimension_semantics=("parallel",)),
    )(x, y, g.reshape(1, k), b.reshape(1, k))


def _ln_bwd_call(x, y, g, do):
    s, k = x.shape
    tr = _row_block(s)

    def body(x_ref, y_ref, g_ref, do_ref, dx_ref, dy_ref, dg_ref, db_ref):
        @pl.when(pl.program_id(0) == 0)
        def _():
            dg_ref[...] = jnp.zeros_like(dg_ref)
            db_ref[...] = jnp.zeros_like(db_ref)

        u = ALPHA * x_ref[...] + y_ref[...]
        mu = jnp.mean(u, axis=-1, keepdims=True)
        d = u - mu
        r = lax.rsqrt(jnp.mean(d * d, axis=-1, keepdims=True) + LN_EPS)
        xh = d * r
        dov = do_ref[...]
        dg_ref[...] += jnp.sum(dov * xh, axis=0, keepdims=True)
        db_ref[...] += jnp.sum(dov, axis=0, keepdims=True)
        dxh = dov * g_ref[...]
        du = r * (dxh - jnp.mean(dxh, axis=-1, keepdims=True) - xh * jnp.mean(dxh * xh, axis=-1, keepdims=True))
        dy_ref[...] = du
        dx_ref[...] = ALPHA * du

    row = pl.BlockSpec((tr, k), lambda i: (i, 0))
    vec = pl.BlockSpec((1, k), lambda i: (0, 0))
    dx, dy, dg, db = pl.pallas_call(
        body, name="ln_bwd", grid=(s // tr,), in_specs=[row, row, vec, row], out_specs=[row, row, vec, vec],
        out_shape=[jax.ShapeDtypeStruct((s, k), F32), jax.ShapeDtypeStruct((s, k), F32),
                   jax.ShapeDtypeStruct((1, k), F32), jax.ShapeDtypeStruct((1, k), F32)],
        compiler_params=_params(dimension_semantics=("arbitrary",)),
    )(x, y, g.reshape(1, k), do)
    return dx, dy, dg.reshape(k), db.reshape(k)


@jax.custom_vjp
def ln_res(x, y, g, b):
    return _ln_fwd_call(x, y, g, b)


ln_res.defvjp(lambda x, y, g, b: (_ln_fwd_call(x, y, g, b), (x, y, g)),
              lambda res, do: _ln_bwd_call(res[0], res[1], res[2], do))


def _rope_call(x, c, s1, s2, hs):
    s, w = x.shape
    tr = _row_block(s)
    nb = w // LANES

    def body(x_ref, c_ref, s1_ref, s2_ref, o_ref):
        cv, s1v, s2v = c_ref[...], s1_ref[...], s2_ref[...]
        for cb in range(nb):
            xb = x_ref[:, cb * LANES:(cb + 1) * LANES]
            o_ref[:, cb * LANES:(cb + 1) * LANES] = (
                xb * cv + pltpu.roll(xb, LANES - hs, 1) * s1v + pltpu.roll(xb, hs, 1) * s2v)

    row = pl.BlockSpec((tr, w), lambda i: (i, 0))
    tab = pl.BlockSpec((tr, LANES), lambda i: (i, 0))
    return pl.pallas_call(
        body, name=f"rope_{w}_{hs}", grid=(s // tr,), in_specs=[row, tab, tab, tab], out_specs=row,
        out_shape=jax.ShapeDtypeStruct((s, w), F32), compiler_params=_params(dimension_semantics=("parallel",)),
    )(x, c, s1, s2)


@functools.partial(jax.custom_vjp, nondiff_argnums=(2,))
def rope(x, tabs, hs):
    return _rope_call(x, tabs[0], tabs[1], tabs[2], hs)


def _rope_fwd(x, tabs, hs):
    return _rope_call(x, tabs[0], tabs[1], tabs[2], hs), tabs


def _rope_bwd(hs, tabs, dy):
    return _rope_call(dy, tabs[0], -tabs[1], -tabs[2], hs), jax.tree.map(jnp.zeros_like, tabs)


rope.defvjp(_rope_fwd, _rope_bwd)


def _rope_tables(s, layout):
    pos = jnp.arange(s, dtype=F32)[:, None]
    lane = jnp.arange(LANES)
    if layout == "mla":
        dim, hs = MLA_ROPE, MLA_ROPE // 2
        r = lane - MLA_NOPE
        active = (r >= 0) & (r < MLA_ROPE)
    else:
        dim, hs = SWA_DIM, SWA_DIM // 2
        r = lane % SWA_DIM
        active = jnp.ones_like(lane, dtype=bool)
    f = jnp.where(active, r % hs, 0)
    inv = ROPE_THETA ** (-(2.0 * f.astype(F32)) / dim)
    ang = pos * inv[None, :]
    cos, sin = jnp.cos(ang), jnp.sin(ang)
    first = (active & (r < hs))[None, :]
    second = (active & (r >= hs))[None, :]
    c = jnp.where(active[None, :], cos, 1.0)
    s1 = jnp.where(first, -sin, 0.0)
    s2 = jnp.where(second, sin, 0.0)
    return (c, s1, s2), hs


def _gate_fwd_call(o_parts, gate):
    s, w = gate.shape
    tr = _row_block(s)
    widths = [o.shape[1] for o in o_parts]

    def body(*refs):
        o_refs, g_ref, z_ref = refs[:len(widths)], refs[len(widths)], refs[len(widths) + 1]
        off = 0
        for o_ref, wd in zip(o_refs, widths):
            gv = g_ref[:, off:off + wd]
            z_ref[:, off:off + wd] = o_ref[...] * (gv * jax.nn.sigmoid(gv))
            off += wd

    specs = [pl.BlockSpec((tr, wd), lambda i: (i, 0)) for wd in widths]
    row = pl.BlockSpec((tr, w), lambda i: (i, 0))
    return pl.pallas_call(
        body, name=f"gate_fwd_{len(widths)}", grid=(s // tr,), in_specs=specs + [row], out_specs=row,
        out_shape=jax.ShapeDtypeStruct((s, w), F32), compiler_params=_params(dimension_semantics=("parallel",)),
    )(*o_parts, gate)


def _gate_bwd_call(o_parts, gate, dz):
    s, w = gate.shape
    tr = _row_block(s)
    widths = [o.shape[1] for o in o_parts]
    n = len(widths)

    def body(*refs):
        o_refs, g_ref, dz_ref = refs[:n], refs[n], refs[n + 1]
        do_refs, dg_ref = refs[n + 2:2 * n + 2], refs[2 * n + 2]
        off = 0
        for o_ref, do_ref, wd in zip(o_refs, do_refs, widths):
            gv = g_ref[:, off:off + wd]
            sg = jax.nn.sigmoid(gv)
            dzv = dz_ref[:, off:off + wd]
            do_ref[...] = dzv * (gv * sg)
            dg_ref[:, off:off + wd] = dzv * o_ref[...] * (sg * (1.0 + gv * (1.0 - sg)))
            off += wd

    specs = [pl.BlockSpec((tr, wd), lambda i: (i, 0)) for wd in widths]
    row = pl.BlockSpec((tr, w), lambda i: (i, 0))
    outs = pl.pallas_call(
        body, name=f"gate_bwd_{n}", grid=(s // tr,), in_specs=specs + [row, row], out_specs=specs + [row],
        out_shape=[jax.ShapeDtypeStruct((s, wd), F32) for wd in widths] + [jax.ShapeDtypeStruct((s, w), F32)],
        compiler_params=_params(dimension_semantics=("parallel",)),
    )(*o_parts, gate, dz)
    return tuple(outs[:n]), outs[n]


@jax.custom_vjp
def gate_mul(o_parts, gate):
    return _gate_fwd_call(o_parts, gate)


gate_mul.defvjp(lambda o_parts, gate: (_gate_fwd_call(o_parts, gate), (o_parts, gate)),
                lambda res, dz: _gate_bwd_call(res[0], res[1], dz))


def _loss_call(y, t):
    s, k = y.shape
    tr = _row_block(s)
    nsteps = s // tr

    def body(y_ref, t_ref, l_ref, dy_ref, acc_ref):
        i = pl.program_id(0)

        @pl.when(i == 0)
        def _():
            acc_ref[...] = jnp.zeros_like(acc_ref)

        d = y_ref[...] - t_ref[...]
        dy_ref[...] = d / k
        acc_ref[...] += jnp.sum(d * d, axis=0, keepdims=True)

        @pl.when(i == nsteps - 1)
        def _():
            tot = jnp.sum(acc_ref[...], axis=1, keepdims=True) * (0.5 / k)
            l_ref[...] = jnp.broadcast_to(tot, l_ref.shape)

    row = pl.BlockSpec((tr, k), lambda i: (i, 0))
    return pl.pallas_call(
        body, name="loss", grid=(nsteps,), in_specs=[row, row],
        out_specs=[pl.BlockSpec((1, LANES), lambda i: (0, 0)), row],
        out_shape=[jax.ShapeDtypeStruct((1, LANES), F32), jax.ShapeDtypeStruct((s, k), F32)],
        scratch_shapes=[pltpu.VMEM((1, k), F32)], compiler_params=_params(dimension_semantics=("arbitrary",)),
    )(y, t)


@jax.custom_vjp
def mse_loss(y, t):
    return _loss_call(y, t)[0][0, 0]


def _mse_fwd(y, t):
    l, dy = _loss_call(y, t)
    return l[0, 0], (dy, t)


mse_loss.defvjp(_mse_fwd, lambda res, g: (g * res[0], jnp.zeros_like(res[1])))


def _scan_call(x, b, mode):
    s, w = x.shape
    nt = s // 8

    def tile_scan(t):
        row = lax.broadcasted_iota(jnp.int32, (8, w), 0)
        for sh in (1, 2, 4):
            t = t + jnp.where(row >= sh, pltpu.roll(t, sh, 0), 0.0)
        return t

    def body(x_ref, b_ref, o_ref):
        def step(i, carry):
            rows = pl.ds(pl.multiple_of(i * 8, 8), 8)
            t = x_ref[rows, :]
            if mode == "fwd":
                t = jax.nn.log_sigmoid(t + b_ref[...])
            t = tile_scan(t) + carry
            o_ref[rows, :] = t
            return t[7:8, :]

        total = lax.fori_loop(0, nt, step, jnp.zeros((1, w), F32))
        if mode == "rev":
            def fix(i, c):
                rows = pl.ds(pl.multiple_of(i * 8, 8), 8)
                o_ref[rows, :] = total - o_ref[rows, :] + x_ref[rows, :]
                return c
            lax.fori_loop(0, nt, fix, 0)

    full = pl.BlockSpec((s, w), lambda: (0, 0))
    return pl.pallas_call(
        body, name=f"scan_{mode}", in_specs=[full, pl.BlockSpec((1, w), lambda: (0, 0))], out_specs=full,
        out_shape=jax.ShapeDtypeStruct((s, w), F32), compiler_params=_params(),
    )(x, b)


def _fox_dlogit_call(x, b, dlogf):
    s, w = x.shape
    tr = _row_block(s)

    def body(x_ref, b_ref, d_ref, dx_ref, db_ref):
        @pl.when(pl.program_id(0) == 0)
        def _():
            db_ref[...] = jnp.zeros_like(db_ref)

        dx = d_ref[...] * jax.nn.sigmoid(-(x_ref[...] + b_ref[...]))
        dx_ref[...] = dx
        db_ref[...] += jnp.sum(dx, axis=0, keepdims=True)

    row = pl.BlockSpec((tr, w), lambda i: (i, 0))
    vec = pl.BlockSpec((1, w), lambda i: (0, 0))
    return pl.pallas_call(
        body, name="fox_dlogit", grid=(s // tr,), in_specs=[row, vec, row], out_specs=[row, vec],
        out_shape=[jax.ShapeDtypeStruct((s, w), F32), jax.ShapeDtypeStruct((1, w), F32)],
        compiler_params=_params(dimension_semantics=("arbitrary",)),
    )(x, b, dlogf)


@jax.custom_vjp
def fox_cum(fl, b):
    return _scan_call(fl, b, "fwd")


def _fox_cum_bwd(res, dcum):
    fl, b = res
    dlogf = _scan_call(dcum, b, "rev")
    return _fox_dlogit_call(fl, b, dlogf)


fox_cum.defvjp(lambda fl, b: (_scan_call(fl, b, "fwd"), (fl, b)), _fox_cum_bwd)


def _lane_col(x, lane_idx):
    lane = lax.broadcasted_iota(jnp.int32, (1, x.shape[1]), 1)
    return jnp.sum(jnp.where(lane == lane_idx, x, 0.0), axis=1, keepdims=True)


def _row_of(x, row_idx):
    row = lax.broadcasted_iota(jnp.int32, (x.shape[0], 1), 0)
    return jnp.sum(jnp.where(row == row_idx, x, 0.0), axis=0, keepdims=True)


def _attn_cfg(mode, s):
    if mode == "swa":
        blk = 256 if s >= 2048 else 128
        return dict(blk=blk, n_outer=SWA_KV_HEADS, pps=4, wide=False, scale=SWA_DIM ** -0.5)
    blk = 512 if s >= 2048 else 128
    if mode == "mla":
        return dict(blk=blk, n_outer=4, pps=1, wide=True, scale=(MLA_NOPE + MLA_ROPE) ** -0.5)
    return dict(blk=blk, n_outer=4, pps=1, wide=False, scale=FOX_DIM ** -0.5)


ROW_CHUNK = 32


def _unrolled(n, body, carry):
    for c in range(n):
        carry = body(c, carry)
    return carry


def _valid_rows(mode, i, jb, blk, r0, rc):
    qpos = i * blk + r0 + lax.broadcasted_iota(jnp.int32, (rc, blk), 0)
    kpos = jb * blk + lax.broadcasted_iota(jnp.int32, (rc, blk), 1)
    ok = kpos <= qpos
    if mode == "swa":
        ok = ok & (qpos - kpos < WINDOW)
    return ok


def _attn_fwd_call(mode, q, k, v, extra):
    s = q.shape[0]
    cfg = _attn_cfg(mode, s)
    blk, n_outer, pps, wide, scale = cfg["blk"], cfg["n_outer"], cfg["pps"], cfg["wide"], cfg["scale"]
    rc = ROW_CHUNK
    nq = s // blk
    swa, fox = mode == "swa", mode == "fox"
    qw = (2 * LANES if wide else LANES) * pps
    kw = 2 * LANES if wide else LANES
    ow = LANES * pps
    reps = blk // LANES

    def body(*refs):
        q_ref, k_ref, v_ref = refs[:3]
        n_in = 3
        if fox:
            cum_ref, cumt_ref = refs[3:5]
            n_in = 5
        if swa:
            sink_ref = refs[3]
            n_in = 4
        o_ref, lse_ref, m_ref, l_ref, acc_ref, a_ref, s_all, p_all, c_all = refs[n_in:]
        p_id, i, j = pl.program_id(0), pl.program_id(1), pl.program_id(2)
        if swa:
            jb, run, first, last = i - 1 + j, (i - 1 + j) >= 0, j == 0, j == 1
        else:
            jb, run, first, last = j, j <= i, j == 0, j == i
        lane = lax.broadcasted_iota(jnp.int32, (1, LANES), 1)
        msk = [lane < HALF, lane >= HALF]

        @pl.when(first)
        def _():
            for hh in range(2 * pps):
                if swa:
                    m_ref[hh] = jnp.broadcast_to(sink_ref[hh:hh + 1, :], (blk, LANES))
                    l_ref[hh] = jnp.ones((blk, LANES), F32)
                else:
                    m_ref[hh] = jnp.full((blk, LANES), NEG, F32)
                    l_ref[hh] = jnp.zeros((blk, LANES), F32)
            acc_ref[...] = jnp.zeros_like(acc_ref)

        def process(masked):
            for pp in range(pps):
                vb = v_ref[...]
                pvs = []
                for h in range(2):
                    hh = 2 * pp + h
                    s_ref, p_ref, c_ref = s_all.at[hh], p_all.at[hh], c_all.at[hh]
                    if wide:
                        qh = q_ref[:, h * LANES:(h + 1) * LANES] * scale
                        kh = k_ref[:, h * LANES:(h + 1) * LANES]
                    else:
                        qh = jnp.where(msk[h], q_ref[:, pp * LANES:(pp + 1) * LANES], 0.0) * scale
                        kh = k_ref[...]
                    s_ref[...] = lax.dot_general(qh.astype(BF16), kh.astype(BF16), (((1,), (1,)), ((), ())),
                                                 preferred_element_type=F32)
                    if fox:
                        head = 2 * p_id + h
                        c_ref[...] = jnp.broadcast_to(_lane_col(cum_ref[...], head), (blk, LANES))
                        ck = _row_of(cumt_ref[...], head)

                    def chunk(c, carry, hh=hh, h=h):
                        r0 = c * rc
                        rows = pl.ds(r0, rc)
                        u = s_ref[rows, :]
                        if fox:
                            u = u - ck
                        if masked:
                            u = jnp.where(_valid_rows(mode, i, jb, blk, r0, rc), u, NEG)
                        m_prev, l_prev = m_ref[hh, rows, :], l_ref[hh, rows, :]
                        m_cur = jnp.max(u, axis=1, keepdims=True)
                        if fox:
                            m_cur = m_cur + c_ref[rows, :]
                        m_next = jnp.maximum(m_prev, m_cur)
                        shift = m_next - c_ref[rows, :] if fox else m_next
                        p = jnp.exp(u - jnp.tile(shift, (1, reps)))
                        alpha = jnp.exp(m_prev - m_next)
                        l_ref[hh, rows, :] = alpha * l_prev + jnp.sum(p, axis=1, keepdims=True)
                        m_ref[hh, rows, :] = m_next
                        a_ref[hh, rows, :] = alpha
                        p_ref[rows, :] = p.astype(BF16)
                        return carry

                    _unrolled(blk // rc, chunk, 0)
                    vh = jnp.where(msk[h], vb, 0.0).astype(BF16)
                    pvs.append(jnp.dot(p_ref[...], vh, preferred_element_type=F32))
                acc_ref[pp] = acc_ref[pp] * jnp.where(msk[0], a_ref[2 * pp], a_ref[2 * pp + 1]) + pvs[0] + pvs[1]

        if swa:
            pl.when(run)(lambda: process(True))
        else:
            pl.when(j < i)(lambda: process(False))
            pl.when(j == i)(lambda: process(True))

        @pl.when(last)
        def _():
            for pp in range(pps):
                l0, l1 = l_ref[2 * pp], l_ref[2 * pp + 1]
                o_ref[:, pp * LANES:(pp + 1) * LANES] = acc_ref[pp] / jnp.where(msk[0], l0, l1)
                lse_ref[:, pp * LANES:(pp + 1) * LANES] = jnp.where(
                    msk[0], m_ref[2 * pp] + jnp.log(l0), m_ref[2 * pp + 1] + jnp.log(l1))

    if swa:
        kv_map = lambda g, i, j: (jnp.maximum(i - 1 + j, 0), g)
        grid = (n_outer, nq, 2)
    else:
        kv_map = lambda p, i, j: (jnp.minimum(j, i), p)
        grid = (n_outer, nq, nq)
    q_map = lambda p, i, j: (i, p)
    in_specs = [pl.BlockSpec((blk, qw), q_map), pl.BlockSpec((blk, kw), kv_map), pl.BlockSpec((blk, LANES), kv_map)]
    args = [q, k, v]
    if fox:
        cum, cumt = extra
        in_specs += [pl.BlockSpec((blk, LANES), lambda p, i, j: (i, 0)),
                     pl.BlockSpec((8, blk), lambda p, i, j: (0, jnp.minimum(j, i)))]
        args += [cum, cumt]
    if swa:
        in_specs += [pl.BlockSpec((8, LANES), lambda g, i, j: (g, 0))]
        args += [extra]
    n_pairs = n_outer * pps
    return pl.pallas_call(
        body, name=f"attn_fwd_{mode}", grid=grid, in_specs=in_specs,
        out_specs=[pl.BlockSpec((blk, ow), q_map), pl.BlockSpec((blk, ow), q_map)],
        out_shape=[jax.ShapeDtypeStruct((s, n_pairs * LANES), F32), jax.ShapeDtypeStruct((s, n_pairs * LANES), F32)],
        scratch_shapes=[pltpu.VMEM((2 * pps, blk, LANES), F32), pltpu.VMEM((2 * pps, blk, LANES), F32),
                        pltpu.VMEM((pps, blk, LANES), F32), pltpu.VMEM((2 * pps, blk, LANES), F32),
                        pltpu.VMEM((2 * pps, blk, blk), F32), pltpu.VMEM((2 * pps, blk, blk), BF16),
                        pltpu.VMEM((2 * pps, blk, LANES), F32)],
        compiler_params=_params(dimension_semantics=("parallel", "parallel", "arbitrary")),
    )(*args)


def _rowdot_call(do, o):
    s, w = o.shape
    tr = _row_block(s)

    def body(do_ref, o_ref, d_ref):
        lane = lax.broadcasted_iota(jnp.int32, (1, LANES), 1)
        low = lane < HALF
        for cb in range(w // LANES):
            sl = slice(cb * LANES, (cb + 1) * LANES)
            prod = do_ref[:, sl] * o_ref[:, sl]
            d0 = jnp.sum(jnp.where(low, prod, 0.0), axis=1, keepdims=True)
            d1 = jnp.sum(jnp.where(low, 0.0, prod), axis=1, keepdims=True)
            d_ref[:, sl] = jnp.where(low, d0, d1)

    row = pl.BlockSpec((tr, w), lambda i: (i, 0))
    return pl.pallas_call(
        body, name=f"rowdot_{w}", grid=(s // tr,), in_specs=[row, row], out_specs=row,
        out_shape=jax.ShapeDtypeStruct((s, w), F32), compiler_params=_params(dimension_semantics=("parallel",)),
    )(do, o)


def _attn_bwd_call(mode, q, k, v, extra, lse, dd, do):
    s = q.shape[0]
    cfg = _attn_cfg(mode, s)
    blk, n_outer, pps, wide, scale = cfg["blk"], cfg["n_outer"], cfg["pps"], cfg["wide"], cfg["scale"]
    rc = ROW_CHUNK
    nq = s // blk
    swa, fox = mode == "swa", mode == "fox"
    qw = (2 * LANES if wide else LANES) * pps
    kw = 2 * LANES if wide else LANES
    ow = LANES * pps
    reps = blk // LANES

    def body(*refs):
        q_ref, k_ref, v_ref, lse_ref, dd_ref, do_ref = refs[:6]
        n_in = 6
        if fox:
            cum_ref, cumt_ref = refs[6:8]
            n_in = 8
        if swa:
            sink_ref = refs[6]
            n_in = 7
        dq_ref, dk_ref, dv_ref = refs[n_in:n_in + 3]
        n_out = n_in + 3
        if fox:
            dck_ref, dcq_ref = refs[n_out:n_out + 2]
            n_out += 2
        if swa:
            dsink_ref = refs[n_out]
            n_out += 1
        dk_acc, dv_acc, s_all, dp_all, p_all, ds_all, e_all, d_all = refs[n_out:n_out + 8]
        if fox:
            dck_acc, rs_all = refs[n_out + 8:n_out + 10]
        p_id, j, ii = pl.program_id(0), pl.program_id(1), pl.program_id(2)
        if swa:
            i, run, first_i, last_i = j + ii, (j + ii) < nq, ii == 0, ii == 1
        else:
            i, run, first_i, last_i = ii, ii >= j, ii == j, ii == nq - 1
        lane = lax.broadcasted_iota(jnp.int32, (1, LANES), 1)
        msk = [lane < HALF, lane >= HALF]

        @pl.when((j == 0) & (ii == 0))
        def _():
            dq_ref[...] = jnp.zeros_like(dq_ref)
            if swa:
                dsink_ref[...] = jnp.zeros_like(dsink_ref)
            if fox:
                dcq_ref[...] = jnp.zeros_like(dcq_ref)

        @pl.when(first_i)
        def _():
            dk_acc[...] = jnp.zeros_like(dk_acc)
            dv_acc[...] = jnp.zeros_like(dv_acc)
            if fox:
                dck_acc[...] = jnp.zeros_like(dck_acc)

        def process(masked):
            rows = pl.ds(pl.multiple_of(i * blk, blk), blk)
            vb = v_ref[...].astype(BF16)
            dv_parts, dk_parts = [], []
            for pp in range(pps):
                psl = slice(pp * LANES, (pp + 1) * LANES)
                lse_blk, dd_blk, do_blk = lse_ref[:, psl], dd_ref[:, psl], do_ref[:, psl]
                dq_pair = []
                for h in range(2):
                    hh = 2 * pp + h
                    s_ref, dp_ref, p_ref, ds_ref = s_all.at[hh], dp_all.at[hh], p_all.at[hh], ds_all.at[hh]
                    e_ref, d_ref = e_all.at[hh], d_all.at[hh]
                    if fox:
                        rs_ref = rs_all.at[hh]
                    if wide:
                        hsl = slice(h * LANES, (h + 1) * LANES)
                        qh = (q_ref[:, hsl] * scale).astype(BF16)
                        kh = k_ref[:, hsl].astype(BF16)
                    else:
                        qh = (jnp.where(msk[h], q_ref[:, psl], 0.0) * scale).astype(BF16)
                        kh = k_ref[...].astype(BF16)
                    s_ref[...] = lax.dot_general(qh, kh, (((1,), (1,)), ((), ())), preferred_element_type=F32)
                    do_h = jnp.where(msk[h], do_blk, 0.0).astype(BF16)
                    dp_ref[...] = lax.dot_general(do_h, vb, (((1,), (1,)), ((), ())), preferred_element_type=F32)
                    lse_h = _lane_col(lse_blk, HALF * h)
                    d_h = _lane_col(dd_blk, HALF * h)
                    e_ref[...] = jnp.broadcast_to(lse_h, (blk, LANES))
                    d_ref[...] = jnp.broadcast_to(d_h, (blk, LANES))
                    if fox:
                        head = 2 * p_id + h
                        e_ref[...] = e_ref[...] - jnp.broadcast_to(_lane_col(cum_ref[...], head), (blk, LANES))
                        ck = _row_of(cumt_ref[...], head)

                    def chunk(c, colsum):
                        r0 = c * rc
                        cr = pl.ds(r0, rc)
                        u = s_ref[cr, :]
                        if fox:
                            u = u - ck
                        p = jnp.exp(u - jnp.tile(e_ref[cr, :], (1, reps)))
                        if masked:
                            p = jnp.where(_valid_rows(mode, i, j, blk, r0, rc), p, 0.0)
                        ds = p * (dp_ref[cr, :] - jnp.tile(d_ref[cr, :], (1, reps)))
                        p_ref[cr, :] = p.astype(BF16)
                        ds_ref[cr, :] = ds.astype(BF16)
                        if fox:
                            colsum = colsum + jnp.sum(ds, axis=0, keepdims=True)
                            rs_ref[cr, :] = jnp.broadcast_to(jnp.sum(ds, axis=1, keepdims=True), (rc, LANES))
                        return colsum

                    colsum = _unrolled(blk // rc, chunk, jnp.zeros((1, blk), F32))
                    dv_parts.append(lax.dot_general(p_ref[...], do_h, (((0,), (0,)), ((), ())),
                                                    preferred_element_type=F32))
                    if fox:
                        dck_acc[h:h + 1, :] += -colsum
                        dcq_ref[rows, :] += jnp.where(msk[h], rs_ref[...], 0.0)
                    dq_h = jnp.dot(ds_ref[...], kh, preferred_element_type=F32) * scale
                    dk_h = lax.dot_general(ds_ref[...], qh, (((0,), (0,)), ((), ())), preferred_element_type=F32)
                    if wide:
                        dq_ref[rows, hsl] += dq_h
                        dk_acc[:, hsl] += dk_h
                    else:
                        dq_pair.append(jnp.where(msk[h], dq_h, 0.0))
                        dk_parts.append(dk_h)
                    if swa:
                        @pl.when(first_i)
                        def _():
                            e = jnp.exp(sink_ref[hh:hh + 1, :] - lse_h) * d_h
                            dsink_ref[hh:hh + 1, :] += -jnp.sum(e, axis=0, keepdims=True)
                if not wide:
                    dq_ref[rows, psl] += dq_pair[0] + dq_pair[1]
            dv_acc[...] += functools.reduce(lambda a, b: a + b, dv_parts)
            if not wide:
                dk_acc[...] += functools.reduce(lambda a, b: a + b, dk_parts)

        if swa:
            pl.when(run)(lambda: process(True))
        else:
            pl.when(ii > j)(lambda: process(False))
            pl.when(ii == j)(lambda: process(True))

        @pl.when(last_i)
        def _():
            dk_ref[...] = dk_acc[...]
            dv_ref[...] = dv_acc[...]
            if fox:
                dck_ref[0] = dck_acc[...]

    if swa:
        q_map = lambda g, j, ii: (jnp.minimum(j + ii, nq - 1), g)
        grid = (n_outer, nq, 2)
    else:
        q_map = lambda p, j, ii: (jnp.maximum(ii, j), p)
        grid = (n_outer, nq, nq)
    kv_map = lambda p, j, ii: (j, p)
    in_specs = [pl.BlockSpec((blk, qw), q_map), pl.BlockSpec((blk, kw), kv_map), pl.BlockSpec((blk, LANES), kv_map),
                pl.BlockSpec((blk, ow), q_map), pl.BlockSpec((blk, ow), q_map), pl.BlockSpec((blk, ow), q_map)]
    args = [q, k, v, lse, dd, do]
    n_pairs = n_outer * pps
    out_specs = [pl.BlockSpec((s, qw), lambda p, j, ii: (0, p)), pl.BlockSpec((blk, kw), kv_map),
                 pl.BlockSpec((blk, LANES), kv_map)]
    out_shape = [jax.ShapeDtypeStruct((s, q.shape[1]), F32), jax.ShapeDtypeStruct((s, k.shape[1]), F32),
                 jax.ShapeDtypeStruct((s, v.shape[1]), F32)]
    nh = 2 * pps
    scratch = [pltpu.VMEM((blk, kw), F32), pltpu.VMEM((blk, LANES), F32), pltpu.VMEM((nh, blk, blk), F32),
               pltpu.VMEM((nh, blk, blk), F32), pltpu.VMEM((nh, blk, blk), BF16), pltpu.VMEM((nh, blk, blk), BF16),
               pltpu.VMEM((nh, blk, LANES), F32), pltpu.VMEM((nh, blk, LANES), F32)]
    if fox:
        cum, cumt = extra
        if swa:
            raise AssertionError
        in_specs += [pl.BlockSpec((blk, LANES), lambda p, j, ii: (jnp.maximum(ii, j), 0)),
                     pl.BlockSpec((8, blk), lambda p, j, ii: (0, j))]
        args += [cum, cumt]
        out_specs += [pl.BlockSpec((1, 8, blk), lambda p, j, ii: (p, 0, j)),
                      pl.BlockSpec((s, LANES), lambda p, j, ii: (0, p))]
        out_shape += [jax.ShapeDtypeStruct((n_pairs, 8, s), F32), jax.ShapeDtypeStruct((s, n_pairs * LANES), F32)]
        scratch += [pltpu.VMEM((8, blk), F32), pltpu.VMEM((nh, blk, LANES), F32)]
    if swa:
        in_specs += [pl.BlockSpec((8, LANES), lambda g, j, ii: (g, 0))]
        args += [extra]
        out_specs += [pl.BlockSpec((8, LANES), lambda g, j, ii: (g, 0))]
        out_shape += [jax.ShapeDtypeStruct((SWA_HEADS, LANES), F32)]
    return pl.pallas_call(
        body, name=f"attn_bwd_{mode}", grid=grid, in_specs=in_specs, out_specs=out_specs, out_shape=out_shape,
        scratch_shapes=scratch,
        compiler_params=_params(dimension_semantics=("parallel", "arbitrary", "arbitrary")),
    )(*args)


def _swa_masks(i, blk):
    r = lax.broadcasted_iota(jnp.int32, (blk, blk), 0)
    c = lax.broadcasted_iota(jnp.int32, (blk, blk), 1)
    return (c > r) & (i > 0), c <= r


def _nt(a, b):
    return lax.dot_general(a, b, (((1,), (1,)), ((), ())), preferred_element_type=F32)


def _tn(a, b):
    return lax.dot_general(a, b, (((0,), (0,)), ((), ())), preferred_element_type=F32)


def _swa_bwd_call(q, k, v, sink, lse, dd, do):
    s = q.shape[0]
    blk, pps, scale = WINDOW, 4, SWA_DIM ** -0.5
    nq = s // blk

    def body(q_ref, kp_ref, ko_ref, vp_ref, vo_ref, sink_ref, lse_ref, dd_ref, do_ref,
             dq_ref, dk_ref, dv_ref, dsink_ref, ck_ref, cv_ref):
        i = pl.program_id(1)
        lane = lax.broadcasted_iota(jnp.int32, (1, LANES), 1)
        msk = [lane < HALF, lane >= HALF]

        @pl.when(i == 0)
        def _():
            ck_ref[...] = jnp.zeros_like(ck_ref)
            cv_ref[...] = jnp.zeros_like(cv_ref)
            dsink_ref[...] = jnp.zeros_like(dsink_ref)

        @pl.when(i < nq)
        def _():
            ok_prev, ok_own = _swa_masks(i, blk)
            kp, ko = kp_ref[...].astype(BF16), ko_ref[...].astype(BF16)
            vp, vo = vp_ref[...].astype(BF16), vo_ref[...].astype(BF16)
            dkp, dko, dvp, dvo = [], [], [], []
            for pp in range(pps):
                psl = slice(pp * LANES, (pp + 1) * LANES)
                qp, do_blk = q_ref[:, psl], do_ref[:, psl]
                dqs = []
                for h in range(2):
                    hh = 2 * pp + h
                    qh = (jnp.where(msk[h], qp, 0.0) * scale).astype(BF16)
                    lse_h = jnp.broadcast_to(_lane_col(lse_ref[:, psl], HALF * h), (blk, LANES))
                    d_h = jnp.broadcast_to(_lane_col(dd_ref[:, psl], HALF * h), (blk, LANES))
                    p_p = jnp.where(ok_prev, jnp.exp(_nt(qh, kp) - lse_h), 0.0)
                    p_o = jnp.where(ok_own, jnp.exp(_nt(qh, ko) - lse_h), 0.0)
                    do_h = jnp.where(msk[h], do_blk, 0.0).astype(BF16)
                    ds_p = (p_p * (_nt(do_h, vp) - d_h)).astype(BF16)
                    ds_o = (p_o * (_nt(do_h, vo) - d_h)).astype(BF16)
                    dq_h = (jnp.dot(ds_p, kp, preferred_element_type=F32)
                            + jnp.dot(ds_o, ko, preferred_element_type=F32)) * scale
                    dqs.append(jnp.where(msk[h], dq_h, 0.0))
                    dkp.append(_tn(ds_p, qh))
                    dko.append(_tn(ds_o, qh))
                    dvp.append(_tn(p_p.astype(BF16), do_h))
                    dvo.append(_tn(p_o.astype(BF16), do_h))
                    sink_row = sink_ref[hh:hh + 1, :]
                    dsink_ref[hh:hh + 1, :] += -jnp.sum(jnp.exp(sink_row - lse_h) * d_h, axis=0, keepdims=True)
                dq_ref[:, psl] = dqs[0] + dqs[1]
            total = lambda parts: functools.reduce(lambda a, b: a + b, parts)
            dk_ref[...] = ck_ref[...] + total(dkp)
            dv_ref[...] = cv_ref[...] + total(dvp)
            ck_ref[...] = total(dko)
            cv_ref[...] = total(dvo)

        @pl.when(i == nq)
        def _():
            dk_ref[...] = ck_ref[...]
            dv_ref[...] = cv_ref[...]

    last = nq - 1
    prev = lambda g, i: (jnp.maximum(i - 1, 0), g)
    own = lambda g, i: (jnp.minimum(i, last), g)
    qspec = pl.BlockSpec((blk, pps * LANES), own)
    kspec = lambda m: pl.BlockSpec((blk, LANES), m)
    sspec = pl.BlockSpec((8, LANES), lambda g, i: (g, 0))
    return pl.pallas_call(
        body, name="swa_bwd", grid=(SWA_KV_HEADS, nq + 1),
        in_specs=[qspec, kspec(prev), kspec(own), kspec(prev), kspec(own), sspec, qspec, qspec, qspec],
        out_specs=[qspec, kspec(prev), kspec(prev), sspec],
        out_shape=[jax.ShapeDtypeStruct(q.shape, F32), jax.ShapeDtypeStruct(k.shape, F32),
                   jax.ShapeDtypeStruct(v.shape, F32), jax.ShapeDtypeStruct((SWA_HEADS, LANES), F32)],
        scratch_shapes=[pltpu.VMEM((blk, LANES), F32), pltpu.VMEM((blk, LANES), F32)],
        compiler_params=_params(dimension_semantics=("parallel", "arbitrary")),
    )(q, k, k, v, v, sink, lse, dd, do)


def _make_attn(mode):
    swa = mode == "swa"

    @jax.custom_vjp
    def attn(q, k, v, extra):
        return fwd(q, k, v, extra)[0]

    def fwd(q, k, v, extra):
        o, lse = _attn_fwd_call(mode, q, k, v, extra)
        return o, (q, k, v, extra, o, lse)

    def bwd(res, do):
        q, k, v, extra, o, lse = res
        dd = _rowdot_call(do, o)
        outs = (_swa_bwd_call(q, k, v, extra, lse, dd, do) if swa
                else _attn_bwd_call(mode, q, k, v, extra, lse, dd, do))
        dq, dk, dv = outs[:3]
        if mode == "fox":
            cum, cumt = extra
            dck = outs[3]
            dcumt = dck[:, :2, :].reshape(FOX_HEADS, -1)
            dcq = outs[4].reshape(-1, FOX_HEADS, HALF)[:, :, 0]
            dextra = (jnp.pad(dcq, ((0, 0), (0, LANES - FOX_HEADS))), dcumt)
        elif mode == "swa":
            dextra = jnp.where(jnp.arange(LANES)[None, :] == 0, outs[3], 0.0)
        else:
            dextra = None
        return dq, dk, dv, dextra

    attn.defvjp(fwd, bwd)
    return attn


attn_mla = _make_attn("mla")
attn_fox = _make_attn("fox")
attn_swa = _make_attn("swa")


def _ukv_layout(w):
    r = w.shape[0]
    w3 = w.reshape(r, MLA_HEADS, MLA_NOPE + MLA_V)
    wk = jnp.pad(w3[:, :, :MLA_NOPE], ((0, 0), (0, 0), (0, LANES - MLA_NOPE))).reshape(r, MLA_HEADS * LANES)
    wv = w3[:, :, MLA_NOPE:].reshape(r, MLA_HEADS * MLA_V)
    return wk, wv


def _even_layer(x, w_in_cat, q_norm, w_uq_p, kv_norm, w_ukv, b_f, w_out, ln_g, ln_b, tabs_mla):
    cq, ckv, kpe, fq, fk, fv, fl, gate = even_in_proj(x, relayout(w_in_cat, "even"))
    tabs, hs = tabs_mla
    q = rope(mm(rms_norm(cq, q_norm), w_uq_p), tabs, hs)
    ckvn = rms_norm(ckv, kv_norm)
    wk, wv = _ukv_layout(w_ukv)
    kk = mm(ckvn, wk) + jnp.tile(rope(kpe, tabs, hs), (1, MLA_HEADS))
    o_mla = attn_mla(q, kk, mm(ckvn, wv), None)
    cum = fox_cum(fl, jnp.pad(b_f, (0, LANES - FOX_HEADS)).reshape(1, LANES))
    o_fox = attn_fox(fq, fk, fv, (cum, cum[:, :8].T))
    y = mm(gate_mul((o_mla, o_fox), gate), w_out)
    return ln_res(x, y, ln_g, ln_b)


def _odd_layer(x, w_in_cat, sinks, w_out, ln_g, ln_b, tabs_swa):
    q, kd, vd, gate = odd_in_proj(x, relayout(w_in_cat, "odd"))
    tabs, hs = tabs_swa
    q = rope(q, tabs, hs)
    kd = rope(kd, tabs, hs)
    o = attn_swa(q, kd, vd, jnp.broadcast_to(sinks[:, None], (SWA_HEADS, LANES)))
    y = mm(gate_mul((o,), gate), w_out)
    return ln_res(x, y, ln_g, ln_b)


EVEN_SHARDED = ["even_w_in", "even_w_uq", "even_w_ukv", "even_w_out"]
ODD_SHARDED = ["odd_w_in", "odd_w_out", "odd_ln_g", "odd_ln_b"]
EVEN_REPL = ["even_q_norm", "even_kv_norm", "even_b_f", "even_ln_g", "even_ln_b"]
ODD_REPL = ["odd_sinks"]


def _layer_names(layer):
    return (EVEN_SHARDED, EVEN_REPL) if layer % 2 == 0 else (ODD_SHARDED, ODD_REPL)


def _layer_of(name, j):
    return 2 * j if name.startswith("even") else 2 * j + 1


def _layer_apply(layer, p, x, tabs):
    if layer % 2 == 0:
        return _even_layer(x, p["even_w_in"], p["even_q_norm"], p["even_w_uq"], p["even_kv_norm"], p["even_w_ukv"],
                           p["even_b_f"], p["even_w_out"], p["even_ln_g"], p["even_ln_b"], tabs["mla"])
    return _odd_layer(x, p["odd_w_in"], p["odd_sinks"], p["odd_w_out"], p["odd_ln_g"], p["odd_ln_b"], tabs["swa"])


def _pad_rows(flat, mult):
    n = flat.shape[-1]
    per = mult * LANES
    padded = -(-n // per) * per
    if padded != n:
        flat = jnp.pad(flat, [(0, 0)] * (flat.ndim - 1) + [(0, padded - n)])
    return flat.reshape(flat.shape[:-1] + (padded // LANES, LANES))


def _pad_last(a, width):
    if a.shape[-1] == width:
        return a
    return jnp.pad(a, [(0, 0)] * (a.ndim - 1) + [(0, width - a.shape[-1])])


def _join(slots, axis):
    shp = list(slots.shape[1:])
    shp[axis] *= N_DEV
    return jnp.moveaxis(slots, 0, axis).reshape(shp)


def _split(full, axis):
    shp = full.shape
    t = full.reshape(shp[:axis] + (N_DEV, shp[axis] // N_DEV) + shp[axis + 1:])
    return jnp.moveaxis(t, axis, 0)


PAD_TO = {"even_w_in": SHARD_PAD, "even_w_uq": LANES, "odd_w_in": SHARD_PAD}


def kernel(x, even_w_in, even_q_norm, even_w_uq, even_kv_norm, even_w_ukv, even_b_f, even_w_out, even_ln_g, even_ln_b, odd_w_in, odd_sinks, odd_w_out, odd_ln_g, odd_ln_b, loss_target, m_even_w_in, m_even_q_norm, m_even_w_uq, m_even_kv_norm, m_even_w_ukv, m_even_b_f, m_even_w_out, m_even_ln_g, m_even_ln_b, m_odd_w_in, m_odd_sinks, m_odd_w_out, m_odd_ln_g, m_odd_ln_b, v_even_w_in, v_even_q_norm, v_even_w_uq, v_even_kv_norm, v_even_w_ukv, v_even_b_f, v_even_w_out, v_even_ln_g, v_even_ln_b, v_odd_w_in, v_odd_sinks, v_odd_w_out, v_odd_ln_g, v_odd_ln_b):
    w = dict(even_w_in=even_w_in, even_q_norm=even_q_norm, even_w_uq=even_w_uq, even_kv_norm=even_kv_norm,
             even_w_ukv=even_w_ukv, even_b_f=even_b_f, even_w_out=even_w_out, even_ln_g=even_ln_g, even_ln_b=even_ln_b,
             odd_w_in=odd_w_in, odd_sinks=odd_sinks, odd_w_out=odd_w_out, odd_ln_g=odd_ln_g, odd_ln_b=odd_ln_b)
    mom = dict(even_w_in=m_even_w_in, even_q_norm=m_even_q_norm, even_w_uq=m_even_w_uq, even_kv_norm=m_even_kv_norm,
               even_w_ukv=m_even_w_ukv, even_b_f=m_even_b_f, even_w_out=m_even_w_out, even_ln_g=m_even_ln_g,
               even_ln_b=m_even_ln_b, odd_w_in=m_odd_w_in, odd_sinks=m_odd_sinks, odd_w_out=m_odd_w_out,
               odd_ln_g=m_odd_ln_g, odd_ln_b=m_odd_ln_b)
    vel = dict(even_w_in=v_even_w_in, even_q_norm=v_even_q_norm, even_w_uq=v_even_w_uq, even_kv_norm=v_even_kv_norm,
               even_w_ukv=v_even_w_ukv, even_b_f=v_even_b_f, even_w_out=v_even_w_out, even_ln_g=v_even_ln_g,
               even_ln_b=v_even_ln_b, odd_w_in=v_odd_w_in, odd_sinks=v_odd_sinks, odd_w_out=v_odd_w_out,
               odd_ln_g=v_odd_ln_g, odd_ln_b=v_odd_ln_b)
    sharded = BIG + SMALL_SHARDED
    padded = lambda d, n: _pad_last(d[n], PAD_TO.get(n, d[n].shape[-1]))

    tabs = {"mla": _rope_tables(x.shape[1], "mla"), "swa": _rope_tables(x.shape[1], "swa")}
    keys = lambda layers: [(n, layer // 2) for layer in layers for n in _layer_names(layer)[0]]
    first, rest = keys([0]), keys([1, 2, 3])
    me = _lin(_me())

    def shard(n, j):
        a = padded(w, n)[j]
        return a.astype(BF16) if n in BIG else a

    to_full = lambda n, g: _join(g, SHARD_AXIS[n] - 1).astype(F32)
    to_slots = lambda n, g: _split(g, SHARD_AXIS[n] - 1).astype(BF16 if n in BIG else F32)

    def layer_params(layer, full_of):
        shn, rpn = _layer_names(layer)
        p = {n: full_of(n) for n in shn}
        p.update({n: w[n][layer // 2] for n in rpn})
        return p

    got0 = dict(zip(first, _all_gather([shard(n, j) for n, j in first], "all_gather_first")))
    mine_rest = [shard(n, j) for n, j in rest]
    send_sems, recv_sems, srcs, lands, token = _split_start(mine_rest, False, "all_gather_rest_start")
    p0 = layer_params(0, lambda n: to_full(n, got0[(n, 0)]) + token[0, 0])
    x1, vjp0 = jax.vjp(lambda p, xx: _layer_apply(0, p, xx, tabs), p0, x[0])
    got = _split_wait(send_sems, recv_sems, srcs, lands, x1, False, "all_gather_rest_wait")
    got = dict(zip(rest, _own_slot(got, [m[None] for m in mine_rest])))

    xs, vjps = x1, [vjp0]
    for layer in (1, 2, 3):
        p = layer_params(layer, lambda n: to_full(n, got[(n, layer // 2)]))
        xs, vjp = jax.vjp(lambda p_, xx, layer=layer: _layer_apply(layer, p_, xx, tabs), p, xs)
        vjps.append(vjp)
    loss_local, vjp_loss = jax.vjp(lambda y: mse_loss(y, loss_target[0]), xs)
    (dy,) = vjp_loss(jnp.ones((), F32))
    loss = lax.psum(loss_local, AXES)
    grads = {}
    for layer in (3, 2, 1):
        grads[layer], dy = vjps[layer](dy)

    parts_rest = [to_slots(n, grads[_layer_of(n, j)][n]) for n, j in rest]
    send_sems, recv_sems, srcs, lands, token = _split_start(parts_rest, True, "grad_exchange_rest_start")
    grads[0], grad_x = vjps[0](dy + token[0, 0])
    recv_rest = _split_wait(send_sems, recv_sems, srcs, lands, grad_x, True, "grad_exchange_rest_wait")
    recv = dict(zip(rest, _own_slot(recv_rest, [lax.dynamic_slice_in_dim(p, me, 1, axis=0) for p in parts_rest])))
    repl_grad = lambda n: jnp.stack([grads[_layer_of(n, j)][n] for j in (0, 1)])
    repl_rows = _pad_rows(jnp.concatenate([repl_grad(n).reshape(-1) for n in REPL]), 8)
    parts_last = [to_slots(n, grads[0][n]) for n, j in first]
    parts_last.append(jnp.broadcast_to(repl_rows[None], (N_DEV,) + repl_rows.shape))
    recv_last = _exchange(parts_last, "grad_exchange_last")
    recv.update(zip(first, recv_last[:-1]))

    g_out, d_out, m_out, v_out = {}, {}, {}, {}
    for n in sharded:
        r = jnp.stack([recv[(n, 0)], recv[(n, 1)]], axis=1)
        cols = r.shape[-1]
        flat2 = lambda d: padded(d, n).reshape(-1, cols)
        outs = _sum_adamw(r.reshape(N_DEV, -1, cols), flat2(w), flat2(mom), flat2(vel))
        for dst, o in zip((g_out, d_out, m_out, v_out), outs):
            dst[n] = o.reshape(w[n].shape[:-1] + (cols,))[..., :w[n].shape[-1]]
    pack = lambda d: _pad_rows(jnp.concatenate([d[n].reshape(-1) for n in REPL]), 8)
    outs = _sum_adamw(recv_last[-1], pack(w), pack(mom), pack(vel))
    for dst, o in zip((g_out, d_out, m_out, v_out), outs):
        flat, off = o.reshape(-1), 0
        for n in REPL:
            size = math.prod(w[n].shape)
            dst[n] = flat[off:off + size].reshape(w[n].shape)
            off += size
    return (loss, grad_x[None], *[g_out[n] for n in WEIGHTS], *[d_out[n] for n in WEIGHTS],
            *[m_out[n] for n in WEIGHTS], *[v_out[n] for n in WEIGHTS])
```

```python
import functools
import math

import jax
import jax.numpy as jnp
from jax import lax
from jax.experimental import pallas as pl
from jax.experimental.pallas import tpu as pltpu

F32 = jnp.float32
BF16 = jnp.bfloat16
LANES = 128
HALF = 64
N_DEV = 8
AXES = ("x", "y", "c")
VMEM_LIMIT = 48 * 1024 * 1024

D_MODEL = 1024
DEPTH = 4
ROPE_THETA = 10000.0
MLA_HEADS, MLA_NOPE, MLA_ROPE, MLA_V, MLA_Q_RANK, MLA_KV_RANK = 8, 64, 32, 64, 256, 128
FOX_HEADS, FOX_DIM = 8, 64
SWA_HEADS, SWA_KV_HEADS, SWA_DIM, WINDOW = 16, 2, 64, 128
RMS_EPS, LN_EPS = 1e-6, 1e-5
ALPHA = (2 * DEPTH) ** 0.25
ADAM_LR, ADAM_B1, ADAM_B2, ADAM_EPS, ADAM_WD, ADAM_STEP = 0.001, 0.9, 0.999, 1e-08, 0.01, 10
NEG = -1e30

WEIGHTS = ["even_w_in", "even_q_norm", "even_w_uq", "even_kv_norm", "even_w_ukv", "even_b_f", "even_w_out",
           "even_ln_g", "even_ln_b", "odd_w_in", "odd_sinks", "odd_w_out", "odd_ln_g", "odd_ln_b"]
SHARD_AXIS = {"even_w_in": 2, "even_w_uq": 2, "even_w_ukv": 2, "even_w_out": 1, "odd_w_in": 2, "odd_w_out": 1,
              "odd_ln_g": 1, "odd_ln_b": 1, "even_q_norm": None, "even_kv_norm": None, "even_b_f": None,
              "even_ln_g": None, "even_ln_b": None, "odd_sinks": None}
BIG = ["even_w_in", "even_w_uq", "even_w_ukv", "even_w_out", "odd_w_in", "odd_w_out"]
SMALL_SHARDED = ["odd_ln_g", "odd_ln_b"]
REPL = [n for n in WEIGHTS if SHARD_AXIS[n] is None]


def _pick(n, cands):
    for c in cands:
        if n % c == 0:
            return c
    return n


def _params(**kw):
    return pltpu.CompilerParams(vmem_limit_bytes=VMEM_LIMIT, **kw)


def _me():
    return lax.axis_index("x"), lax.axis_index("y"), lax.axis_index("c")


def _peer(k):
    x, y, c = _me()
    px = 1 - x if (k >> 2) & 1 else x
    py = 1 - y if (k >> 1) & 1 else y
    pc = 1 - c if k & 1 else c
    return px, py, pc


def _lin(p):
    return 4 * p[0] + 2 * p[1] + p[2]


def _comm_call(body, n, out_shape, args, name):
    any_spec = pl.BlockSpec(memory_space=pl.ANY)
    return pl.pallas_call(
        body, name=name, out_shape=out_shape, in_specs=[any_spec] * n, out_specs=[any_spec] * n,
        scratch_shapes=[pltpu.SemaphoreType.DMA((n, N_DEV - 1)), pltpu.SemaphoreType.DMA((n, N_DEV - 1)),
                        pltpu.SemaphoreType.DMA((n,))],
    )(*args)


def _all_gather(xs, name):
    n = len(xs)

    def body(*refs):
        x_refs, out_refs = refs[:n], refs[n:2 * n]
        send_sems, recv_sems, local_sems = refs[2 * n:]
        me = _lin(_me())
        local = [pltpu.make_async_copy(x_refs[a], out_refs[a].at[me], local_sems.at[a]) for a in range(n)]
        for cp in local:
            cp.start()
        sends = []
        for k in range(1, N_DEV):
            for a in range(n):
                cp = pltpu.make_async_remote_copy(
                    src_ref=x_refs[a], dst_ref=out_refs[a].at[me], send_sem=send_sems.at[a, k - 1],
                    recv_sem=recv_sems.at[a, k - 1], device_id=_peer(k), device_id_type=pl.DeviceIdType.MESH)
                cp.start()
                sends.append(cp)
        for k in range(1, N_DEV):
            for a in range(n):
                pltpu.make_async_remote_copy(
                    src_ref=x_refs[a], dst_ref=out_refs[a].at[_lin(_peer(k))], send_sem=send_sems.at[a, k - 1],
                    recv_sem=recv_sems.at[a, k - 1], device_id=_peer(k),
                    device_id_type=pl.DeviceIdType.MESH).wait_recv()
        for cp in sends:
            cp.wait_send()
        for cp in local:
            cp.wait()

    out_shape = [jax.ShapeDtypeStruct((N_DEV,) + x.shape, x.dtype) for x in xs]
    return _comm_call(body, n, out_shape, xs, name)


def _exchange(parts, name):
    n = len(parts)

    def body(*refs):
        p_refs, out_refs = refs[:n], refs[n:2 * n]
        send_sems, recv_sems, local_sems = refs[2 * n:]
        me = _lin(_me())
        local = [pltpu.make_async_copy(p_refs[a].at[me], out_refs[a].at[me], local_sems.at[a]) for a in range(n)]
        for cp in local:
            cp.start()
        sends = []
        for k in range(1, N_DEV):
            peer = _peer(k)
            for a in range(n):
                cp = pltpu.make_async_remote_copy(
                    src_ref=p_refs[a].at[_lin(peer)], dst_ref=out_refs[a].at[me], send_sem=send_sems.at[a, k - 1],
                    recv_sem=recv_sems.at[a, k - 1], device_id=peer, device_id_type=pl.DeviceIdType.MESH)
                cp.start()
                sends.append(cp)
        for k in range(1, N_DEV):
            peer = _peer(k)
            for a in range(n):
                pltpu.make_async_remote_copy(
                    src_ref=p_refs[a].at[_lin(peer)], dst_ref=out_refs[a].at[_lin(peer)],
                    send_sem=send_sems.at[a, k - 1], recv_sem=recv_sems.at[a, k - 1], device_id=peer,
                    device_id_type=pl.DeviceIdType.MESH).wait_recv()
        for cp in sends:
            cp.wait_send()
        for cp in local:
            cp.wait()

    out_shape = [jax.ShapeDtypeStruct(p.shape, p.dtype) for p in parts]
    return _comm_call(body, n, out_shape, parts, name)


_HBM = pl.BlockSpec(memory_space=pltpu.HBM)
_SEM = pl.BlockSpec(memory_space=pltpu.SEMAPHORE)
_EFFECT = pltpu.SideEffectType.DATAFLOW_SIDE_EFFECTING


def _split_start(srcs, slotted, name):
    n = len(srcs)
    lands = [lax.empty(s.shape if slotted else (N_DEV,) + s.shape, s.dtype) for s in srcs]

    def body(*refs):
        src_refs, land_refs = refs[:n], refs[n:2 * n]
        send_sems, recv_sems, token = refs[2 * n], refs[2 * n + 1], refs[-1]
        me = _lin(_me())
        for k in range(1, N_DEV):
            peer = _peer(k)
            for a in range(n):
                pltpu.make_async_remote_copy(
                    src_ref=src_refs[a].at[_lin(peer)] if slotted else src_refs[a], dst_ref=land_refs[a].at[me],
                    send_sem=send_sems.at[a * (N_DEV - 1) + k - 1], recv_sem=recv_sems.at[a * (N_DEV - 1) + k - 1],
                    device_id=peer, device_id_type=pl.DeviceIdType.MESH).start()
        token[...] = jnp.zeros_like(token)

    both = list(srcs) + lands
    outs = pl.pallas_call(
        body, name=name,
        out_shape=(pltpu.SemaphoreType.DMA((n * (N_DEV - 1),)), pltpu.SemaphoreType.DMA((n * (N_DEV - 1),)),
                   *[pltpu.HBM(b.shape, b.dtype) for b in both], jax.ShapeDtypeStruct((8, LANES), F32)),
        in_specs=[_HBM] * (2 * n), out_specs=(_SEM, _SEM, *[_HBM] * (2 * n), pl.BlockSpec(memory_space=pltpu.VMEM)),
        input_output_aliases={a: 2 + a for a in range(2 * n)},
        compiler_params=pltpu.CompilerParams(has_side_effects=_EFFECT),
    )(*[pltpu.with_memory_space_constraint(b, pltpu.HBM) for b in both])
    return outs[0], outs[1], list(outs[2:2 + n]), list(outs[2 + n:2 + 2 * n]), outs[-1]


def _split_wait(send_sems, recv_sems, srcs, lands, after, slotted, name):
    n = len(srcs)

    def body(*refs):
        src_refs, land_refs = refs[:n], refs[n:2 * n]
        send_sems, recv_sems = refs[2 * n], refs[2 * n + 1]
        for k in range(1, N_DEV):
            peer = _peer(k)
            for a in range(n):
                cp = pltpu.make_async_remote_copy(
                    src_ref=src_refs[a].at[_lin(peer)] if slotted else src_refs[a],
                    dst_ref=land_refs[a].at[_lin(peer)], send_sem=send_sems.at[a * (N_DEV - 1) + k - 1],
                    recv_sem=recv_sems.at[a * (N_DEV - 1) + k - 1], device_id=peer,
                    device_id_type=pl.DeviceIdType.MESH)
                cp.wait_send()
                cp.wait_recv()

    both = list(srcs) + list(lands)
    outs = pl.pallas_call(
        body, name=name, out_shape=[pltpu.HBM(b.shape, b.dtype) for b in both],
        in_specs=[_HBM] * (2 * n) + [_SEM, _SEM, pl.BlockSpec(memory_space=pl.ANY)], out_specs=[_HBM] * (2 * n),
        input_output_aliases={a: a for a in range(2 * n)},
        compiler_params=pltpu.CompilerParams(has_side_effects=_EFFECT),
    )(*both, send_sems, recv_sems, after)
    return list(outs[n:])


def _own_slot(lands, own):
    me = _lin(_me())
    return [lax.dynamic_update_slice_in_dim(l, o.astype(l.dtype), me, axis=0) for l, o in zip(lands, own)]


def _sum_adamw(recv, w, m, v):
    _, rows, lanes = recv.shape
    tr = _pick(rows, (256, 128, 64, 32, 16, 8))
    c1 = 1.0 - ADAM_B1 ** ADAM_STEP
    c2 = 1.0 - ADAM_B2 ** ADAM_STEP

    def body(r_ref, w_ref, m_ref, v_ref, g_out, d_out, m_out, v_out):
        g = r_ref[0].astype(F32)
        for s in range(1, N_DEV):
            g = g + r_ref[s].astype(F32)
        mn = ADAM_B1 * m_ref[...] + (1.0 - ADAM_B1) * g
        vn = ADAM_B2 * v_ref[...] + (1.0 - ADAM_B2) * (g * g)
        m_hat = mn / c1
        v_hat = vn / c2
        g_out[...] = g
        d_out[...] = -ADAM_LR * (m_hat / (jnp.sqrt(v_hat) + ADAM_EPS) + ADAM_WD * w_ref[...])
        m_out[...] = mn
        v_out[...] = vn

    blk = pl.BlockSpec((tr, lanes), lambda i: (i, 0))
    shp = jax.ShapeDtypeStruct((rows, lanes), F32)
    return pl.pallas_call(
        body, name=f"sum_adamw_{rows}x{lanes}", grid=(rows // tr,),
        in_specs=[pl.BlockSpec((N_DEV, tr, lanes), lambda i: (0, i, 0)), blk, blk, blk],
        out_specs=[blk, blk, blk, blk], out_shape=[shp, shp, shp, shp],
        compiler_params=_params(dimension_semantics=("parallel",)),
    )(recv, w, m, v)


def _mm_nn(a, b):
    m, k = a.shape
    _, n = b.shape
    tm = _pick(m, (1024, 512, 256, 128))
    tn = _pick(n, (1024, 640, 512, 256, 128))
    tk = _pick(k, (1024, 640, 512, 256, 128))
    nk = k // tk

    def body(a_ref, b_ref, o_ref, acc_ref):
        kk = pl.program_id(2)

        @pl.when(kk == 0)
        def _():
            acc_ref[...] = jnp.zeros_like(acc_ref)

        acc_ref[...] += jnp.dot(a_ref[...].astype(BF16), b_ref[...].astype(BF16), preferred_element_type=F32)

        @pl.when(kk == nk - 1)
        def _():
            o_ref[...] = acc_ref[...]

    return pl.pallas_call(
        body, name=f"mm_nn_{m}x{k}x{n}", grid=(m // tm, n // tn, nk),
        in_specs=[pl.BlockSpec((tm, tk), lambda i, j, kk: (i, kk)), pl.BlockSpec((tk, tn), lambda i, j, kk: (kk, j))],
        out_specs=pl.BlockSpec((tm, tn), lambda i, j, kk: (i, j)),
        out_shape=jax.ShapeDtypeStruct((m, n), F32),
        scratch_shapes=[pltpu.VMEM((tm, tn), F32)],
        compiler_params=_params(dimension_semantics=("parallel", "parallel", "arbitrary")),
    )(a, b)


def _mm_tn(a, g):
    s, k = a.shape
    _, n = g.shape
    tm = _pick(k, (1024, 512, 256, 128))
    tn = _pick(n, (1024, 640, 512, 256, 128))
    ts = _pick(s, (512, 256, 128))
    ns = s // ts

    def body(a_ref, g_ref, o_ref, acc_ref):
        ss = pl.program_id(2)

        @pl.when(ss == 0)
        def _():
            acc_ref[...] = jnp.zeros_like(acc_ref)

        acc_ref[...] += lax.dot_general(a_ref[...].astype(BF16), g_ref[...].astype(BF16),
                                        (((0,), (0,)), ((), ())), preferred_element_type=F32)

        @pl.when(ss == ns - 1)
        def _():
            o_ref[...] = acc_ref[...]

    return pl.pallas_call(
        body, name=f"mm_tn_{s}x{k}x{n}", grid=(k // tm, n // tn, ns),
        in_specs=[pl.BlockSpec((ts, tm), lambda i, j, ss: (ss, i)), pl.BlockSpec((ts, tn), lambda i, j, ss: (ss, j))],
        out_specs=pl.BlockSpec((tm, tn), lambda i, j, ss: (i, j)),
        out_shape=jax.ShapeDtypeStruct((k, n), F32),
        scratch_shapes=[pltpu.VMEM((tm, tn), F32)],
        compiler_params=_params(dimension_semantics=("parallel", "parallel", "arbitrary")),
    )(a, g)


@jax.custom_vjp
def mm(a, w):
    return _mm_nn(a, w.astype(BF16))


def _mm_fwd(a, w):
    wb = w.astype(BF16)
    return _mm_nn(a, wb), (a, wb)


def _mm_bwd(res, g):
    a, wb = res
    return _mm_nn(g, wb.T), _mm_tn(a, g)


mm.defvjp(_mm_fwd, _mm_bwd)


def _make_in_proj(widths):
    cuts = [sum(widths[:i]) for i in range(len(widths) + 1)]

    @jax.custom_vjp
    def in_proj(x, w):
        return fwd(x, w)[0]

    def fwd(x, w):
        xb, wb = x.astype(BF16), w.astype(BF16)
        return tuple(_mm_nn(xb, wb[:, a:b]) for a, b in zip(cuts[:-1], cuts[1:])), (xb, wb)

    def bwd(res, gs):
        xb, wb = res
        g = jnp.concatenate(gs, axis=1)
        return _mm_nn(g, wb.T), _mm_tn(xb, g)

    in_proj.defvjp(fwd, bwd)
    return in_proj


EVEN_GROUPS = (256, 128, 128, 512, 512, 512, 128, 1024)
ODD_GROUPS = (1024, 256, 256, 1024)
even_in_proj = _make_in_proj(EVEN_GROUPS)
odd_in_proj = _make_in_proj(ODD_GROUPS)

SHARD_PAD = 384


def _source_columns(kind):
    if kind == "even":
        src = [list(range(0, 384)), [-1] * 64, list(range(384, 416)), [-1] * 32, list(range(416, 1952)),
               list(range(1952, 1960)), [-1] * 120, list(range(1960, 2984))]
        return sum(src, []), 373
    q0, k0, v0, g0 = 0, 1024, 1152, 1280
    dup = lambda base: [base + 64 * g + c for g in range(SWA_KV_HEADS) for _ in range(2) for c in range(64)]
    return list(range(q0, k0)) + dup(k0) + dup(v0) + list(range(g0, 2304)), 288


def _selection(kind):
    src, shard = _source_columns(kind)
    cat = jnp.asarray([s + (SHARD_PAD - shard) * (s // shard) if s >= 0 else -1 for s in src], jnp.int32)
    rows = lax.broadcasted_iota(jnp.int32, (N_DEV * SHARD_PAD, len(src)), 0)
    return (rows == cat[None, :]).astype(BF16)


@functools.partial(jax.custom_vjp, nondiff_argnums=(1,))
def relayout(wcat, kind):
    return _mm_nn(wcat, _selection(kind))


def _relayout_bwd(kind, _, g):
    return (_mm_nn(g, _selection(kind).T),)


relayout.defvjp(lambda wcat, kind: (_mm_nn(wcat, _selection(kind)), None), _relayout_bwd)


def _row_block(s):
    return _pick(s, (512, 256, 128, 64, 32, 16, 8))


def _rms_fwd_call(x, g):
    s, k = x.shape
    tr = _row_block(s)

    def body(x_ref, g_ref, o_ref):
        xv = x_ref[...]
        r = lax.rsqrt(jnp.mean(xv * xv, axis=-1, keepdims=True) + RMS_EPS)
        o_ref[...] = xv * r * g_ref[...]

    return pl.pallas_call(
        body, name=f"rms_fwd_{k}", grid=(s // tr,),
        in_specs=[pl.BlockSpec((tr, k), lambda i: (i, 0)), pl.BlockSpec((1, k), lambda i: (0, 0))],
        out_specs=pl.BlockSpec((tr, k), lambda i: (i, 0)), out_shape=jax.ShapeDtypeStruct((s, k), F32),
        compiler_params=_params(dimension_semantics=("parallel",)),
    )(x, g.reshape(1, k))


def _rms_bwd_call(x, g, dy):
    s, k = x.shape
    tr = _row_block(s)

    def body(x_ref, g_ref, dy_ref, dx_ref, dg_ref):
        @pl.when(pl.program_id(0) == 0)
        def _():
            dg_ref[...] = jnp.zeros_like(dg_ref)

        xv = x_ref[...]
        r = lax.rsqrt(jnp.mean(xv * xv, axis=-1, keepdims=True) + RMS_EPS)
        xh = xv * r
        dyv = dy_ref[...]
        dg_ref[...] += jnp.sum(dyv * xh, axis=0, keepdims=True)
        dxh = dyv * g_ref[...]
        dx_ref[...] = r * (dxh - xh * jnp.mean(dxh * xh, axis=-1, keepdims=True))

    dx, dg = pl.pallas_call(
        body, name=f"rms_bwd_{k}", grid=(s // tr,),
        in_specs=[pl.BlockSpec((tr, k), lambda i: (i, 0)), pl.BlockSpec((1, k), lambda i: (0, 0)),
                  pl.BlockSpec((tr, k), lambda i: (i, 0))],
        out_specs=[pl.BlockSpec((tr, k), lambda i: (i, 0)), pl.BlockSpec((1, k), lambda i: (0, 0))],
        out_shape=[jax.ShapeDtypeStruct((s, k), F32), jax.ShapeDtypeStruct((1, k), F32)],
        compiler_params=_params(dimension_semantics=("arbitrary",)),
    )(x, g.reshape(1, k), dy)
    return dx, dg.reshape(k)


@jax.custom_vjp
def rms_norm(x, g):
    return _rms_fwd_call(x, g)


rms_norm.defvjp(lambda x, g: (_rms_fwd_call(x, g), (x, g)), lambda res, dy: _rms_bwd_call(res[0], res[1], dy))


def _ln_fwd_call(x, y, g, b):
    s, k = x.shape
    tr = _row_block(s)

    def body(x_ref, y_ref, g_ref, b_ref, o_ref):
        u = ALPHA * x_ref[...] + y_ref[...]
        mu = jnp.mean(u, axis=-1, keepdims=True)
        d = u - mu
        var = jnp.mean(d * d, axis=-1, keepdims=True)
        o_ref[...] = d * lax.rsqrt(var + LN_EPS) * g_ref[...] + b_ref[...]

    row = pl.BlockSpec((tr, k), lambda i: (i, 0))
    vec = pl.BlockSpec((1, k), lambda i: (0, 0))
    return pl.pallas_call(
        body, name="ln_fwd", grid=(s // tr,), in_specs=[row, row, vec, vec], out_specs=row,
        out_shape=jax.ShapeDtypeStruct((s, k), F32), compiler_params=_params(dimension_semantics=("parallel",)),
    )(x, y, g.reshape(1, k), b.reshape(1, k))


def _ln_bwd_call(x, y, g, do):
    s, k = x.shape
    tr = _row_block(s)

    def body(x_ref, y_ref, g_ref, do_ref, dx_ref, dy_ref, dg_ref, db_ref):
        @pl.when(pl.program_id(0) == 0)
        def _():
            dg_ref[...] = jnp.zeros_like(dg_ref)
            db_ref[...] = jnp.zeros_like(db_ref)

        u = ALPHA * x_ref[...] + y_ref[...]
        mu = jnp.mean(u, axis=-1, keepdims=True)
        d = u - mu
        r = lax.rsqrt(jnp.mean(d * d, axis=-1, keepdims=True) + LN_EPS)
        xh = d * r
        dov = do_ref[...]
        dg_ref[...] += jnp.sum(dov * xh, axis=0, keepdims=True)
        db_ref[...] += jnp.sum(dov, axis=0, keepdims=True)
        dxh = dov * g_ref[...]
        du = r * (dxh - jnp.mean(dxh, axis=-1, keepdims=True) - xh * jnp.mean(dxh * xh, axis=-1, keepdims=True))
        dy_ref[...] = du
        dx_ref[...] = ALPHA * du

    row = pl.BlockSpec((tr, k), lambda i: (i, 0))
    vec = pl.BlockSpec((1, k), lambda i: (0, 0))
    dx, dy, dg, db = pl.pallas_call(
        body, name="ln_bwd", grid=(s // tr,), in_specs=[row, row, vec, row], out_specs=[row, row, vec, vec],
        out_shape=[jax.ShapeDtypeStruct((s, k), F32), jax.ShapeDtypeStruct((s, k), F32),
                   jax.ShapeDtypeStruct((1, k), F32), jax.ShapeDtypeStruct((1, k), F32)],
        compiler_params=_params(dimension_semantics=("arbitrary",)),
    )(x, y, g.reshape(1, k), do)
    return dx, dy, dg.reshape(k), db.reshape(k)


@jax.custom_vjp
def ln_res(x, y, g, b):
    return _ln_fwd_call(x, y, g, b)


ln_res.defvjp(lambda x, y, g, b: (_ln_fwd_call(x, y, g, b), (x, y, g)),
              lambda res, do: _ln_bwd_call(res[0], res[1], res[2], do))


def _rope_call(x, c, s1, s2, hs):
    s, w = x.shape
    tr = _row_block(s)
    nb = w // LANES

    def body(x_ref, c_ref, s1_ref, s2_ref, o_ref):
        cv, s1v, s2v = c_ref[...], s1_ref[...], s2_ref[...]
        for cb in range(nb):
            xb = x_ref[:, cb * LANES:(cb + 1) * LANES]
            o_ref[:, cb * LANES:(cb + 1) * LANES] = (
                xb * cv + pltpu.roll(xb, LANES - hs, 1) * s1v + pltpu.roll(xb, hs, 1) * s2v)

    row = pl.BlockSpec((tr, w), lambda i: (i, 0))
    tab = pl.BlockSpec((tr, LANES), lambda i: (i, 0))
    return pl.pallas_call(
        body, name=f"rope_{w}_{hs}", grid=(s // tr,), in_specs=[row, tab, tab, tab], out_specs=row,
        out_shape=jax.ShapeDtypeStruct((s, w), F32), compiler_params=_params(dimension_semantics=("parallel",)),
    )(x, c, s1, s2)


@functools.partial(jax.custom_vjp, nondiff_argnums=(2,))
def rope(x, tabs, hs):
    return _rope_call(x, tabs[0], tabs[1], tabs[2], hs)


def _rope_fwd(x, tabs, hs):
    return _rope_call(x, tabs[0], tabs[1], tabs[2], hs), tabs


def _rope_bwd(hs, tabs, dy):
    return _rope_call(dy, tabs[0], -tabs[1], -tabs[2], hs), jax.tree.map(jnp.zeros_like, tabs)


rope.defvjp(_rope_fwd, _rope_bwd)


def _rope_tables(s, layout):
    pos = jnp.arange(s, dtype=F32)[:, None]
    lane = jnp.arange(LANES)
    if layout == "mla":
        dim, hs = MLA_ROPE, MLA_ROPE // 2
        r = lane - MLA_NOPE
        active = (r >= 0) & (r < MLA_ROPE)
    else:
        dim, hs = SWA_DIM, SWA_DIM // 2
        r = lane % SWA_DIM
        active = jnp.ones_like(lane, dtype=bool)
    f = jnp.where(active, r % hs, 0)
    inv = ROPE_THETA ** (-(2.0 * f.astype(F32)) / dim)
    ang = pos * inv[None, :]
    cos, sin = jnp.cos(ang), jnp.sin(ang)
    first = (active & (r < hs))[None, :]
    second = (active & (r >= hs))[None, :]
    c = jnp.where(active[None, :], cos, 1.0)
    s1 = jnp.where(first, -sin, 0.0)
    s2 = jnp.where(second, sin, 0.0)
    return (c, s1, s2), hs


def _gate_fwd_call(o_parts, gate):
    s, w = gate.shape
    tr = _row_block(s)
    widths = [o.shape[1] for o in o_parts]

    def body(*refs):
        o_refs, g_ref, z_ref = refs[:len(widths)], refs[len(widths)], refs[len(widths) + 1]
        off = 0
        for o_ref, wd in zip(o_refs, widths):
            gv = g_ref[:, off:off + wd]
            z_ref[:, off:off + wd] = o_ref[...] * (gv * jax.nn.sigmoid(gv))
            off += wd

    specs = [pl.BlockSpec((tr, wd), lambda i: (i, 0)) for wd in widths]
    row = pl.BlockSpec((tr, w), lambda i: (i, 0))
    return pl.pallas_call(
        body, name=f"gate_fwd_{len(widths)}", grid=(s // tr,), in_specs=specs + [row], out_specs=row,
        out_shape=jax.ShapeDtypeStruct((s, w), F32), compiler_params=_params(dimension_semantics=("parallel",)),
    )(*o_parts, gate)


def _gate_bwd_call(o_parts, gate, dz):
    s, w = gate.shape
    tr = _row_block(s)
    widths = [o.shape[1] for o in o_parts]
    n = len(widths)

    def body(*refs):
        o_refs, g_ref, dz_ref = refs[:n], refs[n], refs[n + 1]
        do_refs, dg_ref = refs[n + 2:2 * n + 2], refs[2 * n + 2]
        off = 0
        for o_ref, do_ref, wd in zip(o_refs, do_refs, widths):
            gv = g_ref[:, off:off + wd]
            sg = jax.nn.sigmoid(gv)
            dzv = dz_ref[:, off:off + wd]
            do_ref[...] = dzv * (gv * sg)
            dg_ref[:, off:off + wd] = dzv * o_ref[...] * (sg * (1.0 + gv * (1.0 - sg)))
            off += wd

    specs = [pl.BlockSpec((tr, wd), lambda i: (i, 0)) for wd in widths]
    row = pl.BlockSpec((tr, w), lambda i: (i, 0))
    outs = pl.pallas_call(
        body, name=f"gate_bwd_{n}", grid=(s // tr,), in_specs=specs + [row, row], out_specs=specs + [row],
        out_shape=[jax.ShapeDtypeStruct((s, wd), F32) for wd in widths] + [jax.ShapeDtypeStruct((s, w), F32)],
        compiler_params=_params(dimension_semantics=("parallel",)),
    )(*o_parts, gate, dz)
    return tuple(outs[:n]), outs[n]


@jax.custom_vjp
def gate_mul(o_parts, gate):
    return _gate_fwd_call(o_parts, gate)


gate_mul.defvjp(lambda o_parts, gate: (_gate_fwd_call(o_parts, gate), (o_parts, gate)),
                lambda res, dz: _gate_bwd_call(res[0], res[1], dz))


def _loss_call(y, t):
    s, k = y.shape
    tr = _row_block(s)
    nsteps = s // tr

    def body(y_ref, t_ref, l_ref, dy_ref, acc_ref):
        i = pl.program_id(0)

        @pl.when(i == 0)
        def _():
            acc_ref[...] = jnp.zeros_like(acc_ref)

        d = y_ref[...] - t_ref[...]
        dy_ref[...] = d / k
        acc_ref[...] += jnp.sum(d * d, axis=0, keepdims=True)

        @pl.when(i == nsteps - 1)
        def _():
            tot = jnp.sum(acc_ref[...], axis=1, keepdims=True) * (0.5 / k)
            l_ref[...] = jnp.broadcast_to(tot, l_ref.shape)

    row = pl.BlockSpec((tr, k), lambda i: (i, 0))
    return pl.pallas_call(
        body, name="loss", grid=(nsteps,), in_specs=[row, row],
        out_specs=[pl.BlockSpec((1, LANES), lambda i: (0, 0)), row],
        out_shape=[jax.ShapeDtypeStruct((1, LANES), F32), jax.ShapeDtypeStruct((s, k), F32)],
        scratch_shapes=[pltpu.VMEM((1, k), F32)], compiler_params=_params(dimension_semantics=("arbitrary",)),
    )(y, t)


@jax.custom_vjp
def mse_loss(y, t):
    return _loss_call(y, t)[0][0, 0]


def _mse_fwd(y, t):
    l, dy = _loss_call(y, t)
    return l[0, 0], (dy, t)


mse_loss.defvjp(_mse_fwd, lambda res, g: (g * res[0], jnp.zeros_like(res[1])))


def _scan_call(x, b, mode):
    s, w = x.shape
    nt = s // 8

    def tile_scan(t):
        row = lax.broadcasted_iota(jnp.int32, (8, w), 0)
        for sh in (1, 2, 4):
            t = t + jnp.where(row >= sh, pltpu.roll(t, sh, 0), 0.0)
        return t

    def body(x_ref, b_ref, o_ref):
        def step(i, carry):
            rows = pl.ds(pl.multiple_of(i * 8, 8), 8)
            t = x_ref[rows, :]
            if mode == "fwd":
                t = jax.nn.log_sigmoid(t + b_ref[...])
            t = tile_scan(t) + carry
            o_ref[rows, :] = t
            return t[7:8, :]

        total = lax.fori_loop(0, nt, step, jnp.zeros((1, w), F32))
        if mode == "rev":
            def fix(i, c):
                rows = pl.ds(pl.multiple_of(i * 8, 8), 8)
                o_ref[rows, :] = total - o_ref[rows, :] + x_ref[rows, :]
                return c
            lax.fori_loop(0, nt, fix, 0)

    full = pl.BlockSpec((s, w), lambda: (0, 0))
    return pl.pallas_call(
        body, name=f"scan_{mode}", in_specs=[full, pl.BlockSpec((1, w), lambda: (0, 0))], out_specs=full,
        out_shape=jax.ShapeDtypeStruct((s, w), F32), compiler_params=_params(),
    )(x, b)


def _fox_dlogit_call(x, b, dlogf):
    s, w = x.shape
    tr = _row_block(s)

    def body(x_ref, b_ref, d_ref, dx_ref, db_ref):
        @pl.when(pl.program_id(0) == 0)
        def _():
            db_ref[...] = jnp.zeros_like(db_ref)

        dx = d_ref[...] * jax.nn.sigmoid(-(x_ref[...] + b_ref[...]))
        dx_ref[...] = dx
        db_ref[...] += jnp.sum(dx, axis=0, keepdims=True)

    row = pl.BlockSpec((tr, w), lambda i: (i, 0))
    vec = pl.BlockSpec((1, w), lambda i: (0, 0))
    return pl.pallas_call(
        body, name="fox_dlogit", grid=(s // tr,), in_specs=[row, vec, row], out_specs=[row, vec],
        out_shape=[jax.ShapeDtypeStruct((s, w), F32), jax.ShapeDtypeStruct((1, w), F32)],
        compiler_params=_params(dimension_semantics=("arbitrary",)),
    )(x, b, dlogf)


@jax.custom_vjp
def fox_cum(fl, b):
    return _scan_call(fl, b, "fwd")


def _fox_cum_bwd(res, dcum):
    fl, b = res
    dlogf = _scan_call(dcum, b, "rev")
    return _fox_dlogit_call(fl, b, dlogf)


fox_cum.defvjp(lambda fl, b: (_scan_call(fl, b, "fwd"), (fl, b)), _fox_cum_bwd)


def _lane_col(x, lane_idx):
    lane = lax.broadcasted_iota(jnp.int32, (1, x.shape[1]), 1)
    return jnp.sum(jnp.where(lane == lane_idx, x, 0.0), axis=1, keepdims=True)


def _row_of(x, row_idx):
    row = lax.broadcasted_iota(jnp.int32, (x.shape[0], 1), 0)
    return jnp.sum(jnp.where(row == row_idx, x, 0.0), axis=0, keepdims=True)


def _attn_cfg(mode, s):
    if mode == "swa":
        blk = 256 if s >= 2048 else 128
        return dict(blk=blk, n_outer=SWA_KV_HEADS, pps=4, wide=False, scale=SWA_DIM ** -0.5)
    blk = 512 if s >= 2048 else 128
    if mode == "mla":
        return dict(blk=blk, n_outer=4, pps=1, wide=True, scale=(MLA_NOPE + MLA_ROPE) ** -0.5)
    return dict(blk=blk, n_outer=4, pps=1, wide=False, scale=FOX_DIM ** -0.5)


ROW_CHUNK = 32


def _unrolled(n, body, carry):
    for c in range(n):
        carry = body(c, carry)
    return carry


def _valid_rows(mode, i, jb, blk, r0, rc):
    qpos = i * blk + r0 + lax.broadcasted_iota(jnp.int32, (rc, blk), 0)
    kpos = jb * blk + lax.broadcasted_iota(jnp.int32, (rc, blk), 1)
    ok = kpos <= qpos
    if mode == "swa":
        ok = ok & (qpos - kpos < WINDOW)
    return ok


def _attn_fwd_call(mode, q, k, v, extra):
    s = q.shape[0]
    cfg = _attn_cfg(mode, s)
    blk, n_outer, pps, wide, scale = cfg["blk"], cfg["n_outer"], cfg["pps"], cfg["wide"], cfg["scale"]
    rc = ROW_CHUNK
    nq = s // blk
    swa, fox = mode == "swa", mode == "fox"
    qw = (2 * LANES if wide else LANES) * pps
    kw = 2 * LANES if wide else LANES
    ow = LANES * pps
    reps = blk // LANES

    def body(*refs):
        if not swa:
            it_ref, jt_ref = refs[:2]
            refs = refs[2:]
        q_ref, k_ref, v_ref = refs[:3]
        n_in = 3
        if fox:
            cum_ref, cumt_ref = refs[3:5]
            n_in = 5
        if swa:
            sink_ref = refs[3]
            n_in = 4
        o_ref, lse_ref, m_ref, l_ref, acc_ref, a_ref, s_all, p_all, c_all = refs[n_in:]
        p_id = pl.program_id(0)
        if swa:
            i, j = pl.program_id(1), pl.program_id(2)
            jb, run, first, last = i - 1 + j, (i - 1 + j) >= 0, j == 0, j == 1
        else:
            i, j = it_ref[pl.program_id(1)], jt_ref[pl.program_id(1)]
            jb, first, last = j, j == 0, j == i
        lane = lax.broadcasted_iota(jnp.int32, (1, LANES), 1)
        msk = [lane < HALF, lane >= HALF]

        @pl.when(first)
        def _():
            for hh in range(2 * pps):
                if swa:
                    m_ref[hh] = jnp.broadcast_to(sink_ref[hh:hh + 1, :], (blk, LANES))
                    l_ref[hh] = jnp.ones((blk, LANES), F32)
                else:
                    m_ref[hh] = jnp.full((blk, LANES), NEG, F32)
                    l_ref[hh] = jnp.zeros((blk, LANES), F32)
            acc_ref[...] = jnp.zeros_like(acc_ref)

        def process(masked):
            for pp in range(pps):
                vb = v_ref[...]
                pvs = []
                for h in range(2):
                    hh = 2 * pp + h
                    s_ref, p_ref, c_ref = s_all.at[hh], p_all.at[hh], c_all.at[hh]
                    if wide:
                        qh = q_ref[:, h * LANES:(h + 1) * LANES] * scale
                        kh = k_ref[:, h * LANES:(h + 1) * LANES]
                    else:
                        qh = jnp.where(msk[h], q_ref[:, pp * LANES:(pp + 1) * LANES], 0.0) * scale
                        kh = k_ref[...]
                    s_ref[...] = lax.dot_general(qh.astype(BF16), kh.astype(BF16), (((1,), (1,)), ((), ())),
                                                 preferred_element_type=F32)
                    if fox:
                        head = 2 * p_id + h
                        c_ref[...] = jnp.broadcast_to(_lane_col(cum_ref[...], head), (blk, LANES))
                        ck = _row_of(cumt_ref[...], head)

                    def chunk(c, carry, hh=hh, h=h):
                        r0 = c * rc
                        rows = pl.ds(r0, rc)
                        u = s_ref[rows, :]
                        if fox:
                            u = u - ck
                        if masked:
                            u = jnp.where(_valid_rows(mode, i, jb, blk, r0, rc), u, NEG)
                        m_prev, l_prev = m_ref[hh, rows, :], l_ref[hh, rows, :]
                        m_cur = jnp.max(u, axis=1, keepdims=True)
                        if fox:
                            m_cur = m_cur + c_ref[rows, :]
                        m_next = jnp.maximum(m_prev, m_cur)
                        shift = m_next - c_ref[rows, :] if fox else m_next
                        p = jnp.exp(u - jnp.tile(shift, (1, reps)))
                        alpha = jnp.exp(m_prev - m_next)
                        l_ref[hh, rows, :] = alpha * l_prev + jnp.sum(p, axis=1, keepdims=True)
                        m_ref[hh, rows, :] = m_next
                        a_ref[hh, rows, :] = alpha
                        p_ref[rows, :] = p.astype(BF16)
                        return carry

                    _unrolled(blk // rc, chunk, 0)
                    vh = jnp.where(msk[h], vb, 0.0).astype(BF16)
                    pvs.append(jnp.dot(p_ref[...], vh, preferred_element_type=F32))
                acc_ref[pp] = acc_ref[pp] * jnp.where(msk[0], a_ref[2 * pp], a_ref[2 * pp + 1]) + pvs[0] + pvs[1]

        if swa:
            pl.when(run)(lambda: process(True))
        else:
            pl.when(j < i)(lambda: process(False))
            pl.when(j == i)(lambda: process(True))

        @pl.when(last)
        def _():
            for pp in range(pps):
                l0, l1 = l_ref[2 * pp], l_ref[2 * pp + 1]
                o_ref[:, pp * LANES:(pp + 1) * LANES] = acc_ref[pp] / jnp.where(msk[0], l0, l1)
                lse_ref[:, pp * LANES:(pp + 1) * LANES] = jnp.where(
                    msk[0], m_ref[2 * pp] + jnp.log(l0), m_ref[2 * pp + 1] + jnp.log(l1))

    if swa:
        kv_map = lambda g, i, j: (jnp.maximum(i - 1 + j, 0), g)
        q_map = lambda g, i, j: (i, g)
        grid, tables, sem = (n_outer, nq, 2), [], ("parallel", "parallel", "arbitrary")
    else:
        tri = [(i, j) for i in range(nq) for j in range(i + 1)]
        tables = [jnp.asarray([t[0] for t in tri], jnp.int32), jnp.asarray([t[1] for t in tri], jnp.int32)]
        kv_map = lambda p, t, it, jt: (jt[t], p)
        q_map = lambda p, t, it, jt: (it[t], p)
        grid, sem = (n_outer, len(tri)), ("parallel", "arbitrary")
    in_specs = [pl.BlockSpec((blk, qw), q_map), pl.BlockSpec((blk, kw), kv_map), pl.BlockSpec((blk, LANES), kv_map)]
    args = [q, k, v]
    if fox:
        cum, cumt = extra
        in_specs += [pl.BlockSpec((blk, LANES), lambda p, t, it, jt: (it[t], 0)),
                     pl.BlockSpec((8, blk), lambda p, t, it, jt: (0, jt[t]))]
        args += [cum, cumt]
    if swa:
        in_specs += [pl.BlockSpec((8, LANES), lambda g, i, j: (g, 0))]
        args += [extra]
    n_pairs = n_outer * pps
    return pl.pallas_call(
        body, name=f"attn_fwd_{mode}",
        grid_spec=pltpu.PrefetchScalarGridSpec(
            num_scalar_prefetch=len(tables), grid=grid, in_specs=in_specs,
            out_specs=[pl.BlockSpec((blk, ow), q_map), pl.BlockSpec((blk, ow), q_map)],
            scratch_shapes=[pltpu.VMEM((2 * pps, blk, LANES), F32), pltpu.VMEM((2 * pps, blk, LANES), F32),
                            pltpu.VMEM((pps, blk, LANES), F32), pltpu.VMEM((2 * pps, blk, LANES), F32),
                            pltpu.VMEM((2 * pps, blk, blk), F32), pltpu.VMEM((2 * pps, blk, blk), BF16),
                            pltpu.VMEM((2 * pps, blk, LANES), F32)]),
        out_shape=[jax.ShapeDtypeStruct((s, n_pairs * LANES), F32), jax.ShapeDtypeStruct((s, n_pairs * LANES), F32)],
        compiler_params=_params(dimension_semantics=sem),
    )(*tables, *args)


def _rowdot_call(do, o):
    s, w = o.shape
    tr = _row_block(s)

    def body(do_ref, o_ref, d_ref):
        lane = lax.broadcasted_iota(jnp.int32, (1, LANES), 1)
        low = lane < HALF
        for cb in range(w // LANES):
            sl = slice(cb * LANES, (cb + 1) * LANES)
            prod = do_ref[:, sl] * o_ref[:, sl]
            d0 = jnp.sum(jnp.where(low, prod, 0.0), axis=1, keepdims=True)
            d1 = jnp.sum(jnp.where(low, 0.0, prod), axis=1, keepdims=True)
            d_ref[:, sl] = jnp.where(low, d0, d1)

    row = pl.BlockSpec((tr, w), lambda i: (i, 0))
    return pl.pallas_call(
        body, name=f"rowdot_{w}", grid=(s // tr,), in_specs=[row, row], out_specs=row,
        out_shape=jax.ShapeDtypeStruct((s, w), F32), compiler_params=_params(dimension_semantics=("parallel",)),
    )(do, o)


def _attn_bwd_call(mode, q, k, v, extra, lse, dd, do):
    s = q.shape[0]
    cfg = _attn_cfg(mode, s)
    blk, n_outer, pps, wide, scale = cfg["blk"], cfg["n_outer"], cfg["pps"], cfg["wide"], cfg["scale"]
    rc = ROW_CHUNK
    nq = s // blk
    swa, fox = mode == "swa", mode == "fox"
    qw = (2 * LANES if wide else LANES) * pps
    kw = 2 * LANES if wide else LANES
    ow = LANES * pps
    reps = blk // LANES

    assert not swa

    def body(*refs):
        jt_ref, it_ref = refs[:2]
        refs = refs[2:]
        q_ref, k_ref, v_ref, lse_ref, dd_ref, do_ref = refs[:6]
        n_in = 6
        if fox:
            cum_ref, cumt_ref = refs[6:8]
            n_in = 8
        dq_ref, dk_ref, dv_ref = refs[n_in:n_in + 3]
        n_out = n_in + 3
        if fox:
            dck_ref, dcq_ref = refs[n_out:n_out + 2]
            n_out += 2
        dk_acc, dv_acc, s_all, dp_all, p_all, ds_all, e_all, d_all = refs[n_out:n_out + 8]
        if fox:
            dck_acc, rs_all = refs[n_out + 8:n_out + 10]
        p_id, t = pl.program_id(0), pl.program_id(1)
        j, ii = jt_ref[t], it_ref[t]
        i, first_i, last_i = ii, ii == j, ii == nq - 1
        lane = lax.broadcasted_iota(jnp.int32, (1, LANES), 1)
        msk = [lane < HALF, lane >= HALF]

        @pl.when(t == 0)
        def _():
            dq_ref[...] = jnp.zeros_like(dq_ref)
            if swa:
                dsink_ref[...] = jnp.zeros_like(dsink_ref)
            if fox:
                dcq_ref[...] = jnp.zeros_like(dcq_ref)

        @pl.when(first_i)
        def _():
            dk_acc[...] = jnp.zeros_like(dk_acc)
            dv_acc[...] = jnp.zeros_like(dv_acc)
            if fox:
                dck_acc[...] = jnp.zeros_like(dck_acc)

        def process(masked):
            rows = pl.ds(pl.multiple_of(i * blk, blk), blk)
            vb = v_ref[...].astype(BF16)
            dv_parts, dk_parts = [], []
            for pp in range(pps):
                psl = slice(pp * LANES, (pp + 1) * LANES)
                lse_blk, dd_blk, do_blk = lse_ref[:, psl], dd_ref[:, psl], do_ref[:, psl]
                dq_pair = []
                for h in range(2):
                    hh = 2 * pp + h
                    s_ref, dp_ref, p_ref, ds_ref = s_all.at[hh], dp_all.at[hh], p_all.at[hh], ds_all.at[hh]
                    e_ref, d_ref = e_all.at[hh], d_all.at[hh]
                    if fox:
                        rs_ref = rs_all.at[hh]
                    if wide:
                        hsl = slice(h * LANES, (h + 1) * LANES)
                        qh = (q_ref[:, hsl] * scale).astype(BF16)
                        kh = k_ref[:, hsl].astype(BF16)
                    else:
                        qh = (jnp.where(msk[h], q_ref[:, psl], 0.0) * scale).astype(BF16)
                        kh = k_ref[...].astype(BF16)
                    s_ref[...] = lax.dot_general(qh, kh, (((1,), (1,)), ((), ())), preferred_element_type=F32)
                    do_h = jnp.where(msk[h], do_blk, 0.0).astype(BF16)
                    dp_ref[...] = lax.dot_general(do_h, vb, (((1,), (1,)), ((), ())), preferred_element_type=F32)
                    lse_h = _lane_col(lse_blk, HALF * h)
                    d_h = _lane_col(dd_blk, HALF * h)
                    e_ref[...] = jnp.broadcast_to(lse_h, (blk, LANES))
                    d_ref[...] = jnp.broadcast_to(d_h, (blk, LANES))
                    if fox:
                        head = 2 * p_id + h
                        e_ref[...] = e_ref[...] - jnp.broadcast_to(_lane_col(cum_ref[...], head), (blk, LANES))
                        ck = _row_of(cumt_ref[...], head)

                    def chunk(c, colsum):
                        r0 = c * rc
                        cr = pl.ds(r0, rc)
                        u = s_ref[cr, :]
                        if fox:
                            u = u - ck
                        p = jnp.exp(u - jnp.tile(e_ref[cr, :], (1, reps)))
                        if masked:
                            p = jnp.where(_valid_rows(mode, i, j, blk, r0, rc), p, 0.0)
                        ds = p * (dp_ref[cr, :] - jnp.tile(d_ref[cr, :], (1, reps)))
                        p_ref[cr, :] = p.astype(BF16)
                        ds_ref[cr, :] = ds.astype(BF16)
                        if fox:
                            colsum = colsum + jnp.sum(ds, axis=0, keepdims=True)
                            rs_ref[cr, :] = jnp.broadcast_to(jnp.sum(ds, axis=1, keepdims=True), (rc, LANES))
                        return colsum

                    colsum = _unrolled(blk // rc, chunk, jnp.zeros((1, blk), F32))
                    dv_parts.append(lax.dot_general(p_ref[...], do_h, (((0,), (0,)), ((), ())),
                                                    preferred_element_type=F32))
                    if fox:
                        dck_acc[h:h + 1, :] += -colsum
                        dcq_ref[rows, :] += jnp.where(msk[h], rs_ref[...], 0.0)
                    dq_h = jnp.dot(ds_ref[...], kh, preferred_element_type=F32) * scale
                    dk_h = lax.dot_general(ds_ref[...], qh, (((0,), (0,)), ((), ())), preferred_element_type=F32)
                    if wide:
                        dq_ref[rows, hsl] += dq_h
                        dk_acc[:, hsl] += dk_h
                    else:
                        dq_pair.append(jnp.where(msk[h], dq_h, 0.0))
                        dk_parts.append(dk_h)
                if not wide:
                    dq_ref[rows, psl] += dq_pair[0] + dq_pair[1]
            dv_acc[...] += functools.reduce(lambda a, b: a + b, dv_parts)
            if not wide:
                dk_acc[...] += functools.reduce(lambda a, b: a + b, dk_parts)

        pl.when(ii > j)(lambda: process(False))
        pl.when(ii == j)(lambda: process(True))

        @pl.when(last_i)
        def _():
            dk_ref[...] = dk_acc[...]
            dv_ref[...] = dv_acc[...]
            if fox:
                dck_ref[0] = dck_acc[...]

    tri = [(j, i) for j in range(nq) for i in range(j, nq)]
    tables = [jnp.asarray([t[0] for t in tri], jnp.int32), jnp.asarray([t[1] for t in tri], jnp.int32)]
    q_map = lambda p, t, jt, it: (it[t], p)
    kv_map = lambda p, t, jt, it: (jt[t], p)
    in_specs = [pl.BlockSpec((blk, qw), q_map), pl.BlockSpec((blk, kw), kv_map), pl.BlockSpec((blk, LANES), kv_map),
                pl.BlockSpec((blk, ow), q_map), pl.BlockSpec((blk, ow), q_map), pl.BlockSpec((blk, ow), q_map)]
    args = [q, k, v, lse, dd, do]
    n_pairs = n_outer * pps
    out_specs = [pl.BlockSpec((s, qw), lambda p, t, jt, it: (0, p)), pl.BlockSpec((blk, kw), kv_map),
                 pl.BlockSpec((blk, LANES), kv_map)]
    out_shape = [jax.ShapeDtypeStruct((s, q.shape[1]), F32), jax.ShapeDtypeStruct((s, k.shape[1]), F32),
                 jax.ShapeDtypeStruct((s, v.shape[1]), F32)]
    nh = 2 * pps
    scratch = [pltpu.VMEM((blk, kw), F32), pltpu.VMEM((blk, LANES), F32), pltpu.VMEM((nh, blk, blk), F32),
               pltpu.VMEM((nh, blk, blk), F32), pltpu.VMEM((nh, blk, blk), BF16), pltpu.VMEM((nh, blk, blk), BF16),
               pltpu.VMEM((nh, blk, LANES), F32), pltpu.VMEM((nh, blk, LANES), F32)]
    if fox:
        cum, cumt = extra
        in_specs += [pl.BlockSpec((blk, LANES), lambda p, t, jt, it: (it[t], 0)),
                     pl.BlockSpec((8, blk), lambda p, t, jt, it: (0, jt[t]))]
        args += [cum, cumt]
        out_specs += [pl.BlockSpec((1, 8, blk), lambda p, t, jt, it: (p, 0, jt[t])),
                      pl.BlockSpec((s, LANES), lambda p, t, jt, it: (0, p))]
        out_shape += [jax.ShapeDtypeStruct((n_pairs, 8, s), F32), jax.ShapeDtypeStruct((s, n_pairs * LANES), F32)]
        scratch += [pltpu.VMEM((8, blk), F32), pltpu.VMEM((nh, blk, LANES), F32)]
    return pl.pallas_call(
        body, name=f"attn_bwd_{mode}",
        grid_spec=pltpu.PrefetchScalarGridSpec(num_scalar_prefetch=2, grid=(n_outer, len(tri)), in_specs=in_specs,
                                               out_specs=out_specs, scratch_shapes=scratch),
        out_shape=out_shape, compiler_params=_params(dimension_semantics=("parallel", "arbitrary")),
    )(*tables, *args)


def _swa_masks(i, blk):
    r = lax.broadcasted_iota(jnp.int32, (blk, blk), 0)
    c = lax.broadcasted_iota(jnp.int32, (blk, blk), 1)
    return (c > r) & (i > 0), c <= r


def _nt(a, b):
    return lax.dot_general(a, b, (((1,), (1,)), ((), ())), preferred_element_type=F32)


def _tn(a, b):
    return lax.dot_general(a, b, (((0,), (0,)), ((), ())), preferred_element_type=F32)


def _swa_bwd_call(q, k, v, sink, lse, dd, do):
    s = q.shape[0]
    blk, pps, scale = WINDOW, 4, SWA_DIM ** -0.5
    nq = s // blk

    def body(q_ref, kp_ref, ko_ref, vp_ref, vo_ref, sink_ref, lse_ref, dd_ref, do_ref,
             dq_ref, dk_ref, dv_ref, dsink_ref, ck_ref, cv_ref):
        i = pl.program_id(1)
        lane = lax.broadcasted_iota(jnp.int32, (1, LANES), 1)
        msk = [lane < HALF, lane >= HALF]

        @pl.when(i == 0)
        def _():
            ck_ref[...] = jnp.zeros_like(ck_ref)
            cv_ref[...] = jnp.zeros_like(cv_ref)
            dsink_ref[...] = jnp.zeros_like(dsink_ref)

        @pl.when(i < nq)
        def _():
            ok_prev, ok_own = _swa_masks(i, blk)
            kp, ko = kp_ref[...].astype(BF16), ko_ref[...].astype(BF16)
            vp, vo = vp_ref[...].astype(BF16), vo_ref[...].astype(BF16)
            dkp, dko, dvp, dvo = [], [], [], []
            for pp in range(pps):
                psl = slice(pp * LANES, (pp + 1) * LANES)
                qp, do_blk = q_ref[:, psl], do_ref[:, psl]
                dqs = []
                for h in range(2):
                    hh = 2 * pp + h
                    qh = (jnp.where(msk[h], qp, 0.0) * scale).astype(BF16)
                    lse_h = jnp.broadcast_to(_lane_col(lse_ref[:, psl], HALF * h), (blk, LANES))
                    d_h = jnp.broadcast_to(_lane_col(dd_ref[:, psl], HALF * h), (blk, LANES))
                    p_p = jnp.where(ok_prev, jnp.exp(_nt(qh, kp) - lse_h), 0.0)
                    p_o = jnp.where(ok_own, jnp.exp(_nt(qh, ko) - lse_h), 0.0)
                    do_h = jnp.where(msk[h], do_blk, 0.0).astype(BF16)
                    ds_p = (p_p * (_nt(do_h, vp) - d_h)).astype(BF16)
                    ds_o = (p_o * (_nt(do_h, vo) - d_h)).astype(BF16)
                    dq_h = (jnp.dot(ds_p, kp, preferred_element_type=F32)
                            + jnp.dot(ds_o, ko, preferred_element_type=F32)) * scale
                    dqs.append(jnp.where(msk[h], dq_h, 0.0))
                    dkp.append(_tn(ds_p, qh))
                    dko.append(_tn(ds_o, qh))
                    dvp.append(_tn(p_p.astype(BF16), do_h))
                    dvo.append(_tn(p_o.astype(BF16), do_h))
                    sink_row = sink_ref[hh:hh + 1, :]
                    dsink_ref[hh:hh + 1, :] += -jnp.sum(jnp.exp(sink_row - lse_h) * d_h, axis=0, keepdims=True)
                dq_ref[:, psl] = dqs[0] + dqs[1]
            total = lambda parts: functools.reduce(lambda a, b: a + b, parts)
            dk_ref[...] = ck_ref[...] + total(dkp)
            dv_ref[...] = cv_ref[...] + total(dvp)
            ck_ref[...] = total(dko)
            cv_ref[...] = total(dvo)

        @pl.when(i == nq)
        def _():
            dk_ref[...] = ck_ref[...]
            dv_ref[...] = cv_ref[...]

    last = nq - 1
    prev = lambda g, i: (jnp.maximum(i - 1, 0), g)
    own = lambda g, i: (jnp.minimum(i, last), g)
    qspec = pl.BlockSpec((blk, pps * LANES), own)
    kspec = lambda m: pl.BlockSpec((blk, LANES), m)
    sspec = pl.BlockSpec((8, LANES), lambda g, i: (g, 0))
    return pl.pallas_call(
        body, name="swa_bwd", grid=(SWA_KV_HEADS, nq + 1),
        in_specs=[qspec, kspec(prev), kspec(own), kspec(prev), kspec(own), sspec, qspec, qspec, qspec],
        out_specs=[qspec, kspec(prev), kspec(prev), sspec],
        out_shape=[jax.ShapeDtypeStruct(q.shape, F32), jax.ShapeDtypeStruct(k.shape, F32),
                   jax.ShapeDtypeStruct(v.shape, F32), jax.ShapeDtypeStruct((SWA_HEADS, LANES), F32)],
        scratch_shapes=[pltpu.VMEM((blk, LANES), F32), pltpu.VMEM((blk, LANES), F32)],
        compiler_params=_params(dimension_semantics=("parallel", "arbitrary")),
    )(q, k, k, v, v, sink, lse, dd, do)


def _make_attn(mode):
    swa = mode == "swa"

    @jax.custom_vjp
    def attn(q, k, v, extra):
        return fwd(q, k, v, extra)[0]

    def fwd(q, k, v, extra):
        o, lse = _attn_fwd_call(mode, q, k, v, extra)
        return o, (q, k, v, extra, o, lse)

    def bwd(res, do):
        q, k, v, extra, o, lse = res
        dd = _rowdot_call(do, o)
        outs = (_swa_bwd_call(q, k, v, extra, lse, dd, do) if swa
                else _attn_bwd_call(mode, q, k, v, extra, lse, dd, do))
        dq, dk, dv = outs[:3]
        if mode == "fox":
            cum, cumt = extra
            dck = outs[3]
            dcumt = dck[:, :2, :].reshape(FOX_HEADS, -1)
            dcq = outs[4].reshape(-1, FOX_HEADS, HALF)[:, :, 0]
            dextra = (jnp.pad(dcq, ((0, 0), (0, LANES - FOX_HEADS))), dcumt)
        elif mode == "swa":
            dextra = jnp.where(jnp.arange(LANES)[None, :] == 0, outs[3], 0.0)
        else:
            dextra = None
        return dq, dk, dv, dextra

    attn.defvjp(fwd, bwd)
    return attn


attn_mla = _make_attn("mla")
attn_fox = _make_attn("fox")
attn_swa = _make_attn("swa")


def _ukv_layout(w):
    r = w.shape[0]
    w3 = w.reshape(r, MLA_HEADS, MLA_NOPE + MLA_V)
    wk = jnp.pad(w3[:, :, :MLA_NOPE], ((0, 0), (0, 0), (0, LANES - MLA_NOPE))).reshape(r, MLA_HEADS * LANES)
    wv = w3[:, :, MLA_NOPE:].reshape(r, MLA_HEADS * MLA_V)
    return wk, wv


def _even_layer(x, w_in_cat, q_norm, w_uq_p, kv_norm, w_ukv, b_f, w_out, ln_g, ln_b, tabs_mla):
    cq, ckv, kpe, fq, fk, fv, fl, gate = even_in_proj(x, relayout(w_in_cat, "even"))
    tabs, hs = tabs_mla
    q = rope(mm(rms_norm(cq, q_norm), w_uq_p), tabs, hs)
    ckvn = rms_norm(ckv, kv_norm)
    wk, wv = _ukv_layout(w_ukv)
    kk = mm(ckvn, wk) + jnp.tile(rope(kpe, tabs, hs), (1, MLA_HEADS))
    o_mla = attn_mla(q, kk, mm(ckvn, wv), None)
    cum = fox_cum(fl, jnp.pad(b_f, (0, LANES - FOX_HEADS)).reshape(1, LANES))
    o_fox = attn_fox(fq, fk, fv, (cum, cum[:, :8].T))
    y = mm(gate_mul((o_mla, o_fox), gate), w_out)
    return ln_res(x, y, ln_g, ln_b)


def _odd_layer(x, w_in_cat, sinks, w_out, ln_g, ln_b, tabs_swa):
    q, kd, vd, gate = odd_in_proj(x, relayout(w_in_cat, "odd"))
    tabs, hs = tabs_swa
    q = rope(q, tabs, hs)
    kd = rope(kd, tabs, hs)
    o = attn_swa(q, kd, vd, jnp.broadcast_to(sinks[:, None], (SWA_HEADS, LANES)))
    y = mm(gate_mul((o,), gate), w_out)
    return ln_res(x, y, ln_g, ln_b)


EVEN_SHARDED = ["even_w_in", "even_w_uq", "even_w_ukv", "even_w_out"]
ODD_SHARDED = ["odd_w_in", "odd_w_out", "odd_ln_g", "odd_ln_b"]
EVEN_REPL = ["even_q_norm", "even_kv_norm", "even_b_f", "even_ln_g", "even_ln_b"]
ODD_REPL = ["odd_sinks"]


def _layer_names(layer):
    return (EVEN_SHARDED, EVEN_REPL) if layer % 2 == 0 else (ODD_SHARDED, ODD_REPL)


def _layer_of(name, j):
    return 2 * j if name.startswith("even") else 2 * j + 1


def _layer_apply(layer, p, x, tabs):
    if layer % 2 == 0:
        return _even_layer(x, p["even_w_in"], p["even_q_norm"], p["even_w_uq"], p["even_kv_norm"], p["even_w_ukv"],
                           p["even_b_f"], p["even_w_out"], p["even_ln_g"], p["even_ln_b"], tabs["mla"])
    return _odd_layer(x, p["odd_w_in"], p["odd_sinks"], p["odd_w_out"], p["odd_ln_g"], p["odd_ln_b"], tabs["swa"])


def _pad_rows(flat, mult):
    n = flat.shape[-1]
    per = mult * LANES
    padded = -(-n // per) * per
    if padded != n:
        flat = jnp.pad(flat, [(0, 0)] * (flat.ndim - 1) + [(0, padded - n)])
    return flat.reshape(flat.shape[:-1] + (padded // LANES, LANES))


def _pad_last(a, width):
    if a.shape[-1] == width:
        return a
    return jnp.pad(a, [(0, 0)] * (a.ndim - 1) + [(0, width - a.shape[-1])])


def _join(slots, axis):
    shp = list(slots.shape[1:])
    shp[axis] *= N_DEV
    return jnp.moveaxis(slots, 0, axis).reshape(shp)


def _split(full, axis):
    shp = full.shape
    t = full.reshape(shp[:axis] + (N_DEV, shp[axis] // N_DEV) + shp[axis + 1:])
    return jnp.moveaxis(t, axis, 0)


PAD_TO = {"even_w_in": SHARD_PAD, "even_w_uq": LANES, "odd_w_in": SHARD_PAD}


def kernel(x, even_w_in, even_q_norm, even_w_uq, even_kv_norm, even_w_ukv, even_b_f, even_w_out, even_ln_g, even_ln_b, odd_w_in, odd_sinks, odd_w_out, odd_ln_g, odd_ln_b, loss_target, m_even_w_in, m_even_q_norm, m_even_w_uq, m_even_kv_norm, m_even_w_ukv, m_even_b_f, m_even_w_out, m_even_ln_g, m_even_ln_b, m_odd_w_in, m_odd_sinks, m_odd_w_out, m_odd_ln_g, m_odd_ln_b, v_even_w_in, v_even_q_norm, v_even_w_uq, v_even_kv_norm, v_even_w_ukv, v_even_b_f, v_even_w_out, v_even_ln_g, v_even_ln_b, v_odd_w_in, v_odd_sinks, v_odd_w_out, v_odd_ln_g, v_odd_ln_b):
    w = dict(even_w_in=even_w_in, even_q_norm=even_q_norm, even_w_uq=even_w_uq, even_kv_norm=even_kv_norm,
             even_w_ukv=even_w_ukv, even_b_f=even_b_f, even_w_out=even_w_out, even_ln_g=even_ln_g, even_ln_b=even_ln_b,
             odd_w_in=odd_w_in, odd_sinks=odd_sinks, odd_w_out=odd_w_out, odd_ln_g=odd_ln_g, odd_ln_b=odd_ln_b)
    mom = dict(even_w_in=m_even_w_in, even_q_norm=m_even_q_norm, even_w_uq=m_even_w_uq, even_kv_norm=m_even_kv_norm,
               even_w_ukv=m_even_w_ukv, even_b_f=m_even_b_f, even_w_out=m_even_w_out, even_ln_g=m_even_ln_g,
               even_ln_b=m_even_ln_b, odd_w_in=m_odd_w_in, odd_sinks=m_odd_sinks, odd_w_out=m_odd_w_out,
               odd_ln_g=m_odd_ln_g, odd_ln_b=m_odd_ln_b)
    vel = dict(even_w_in=v_even_w_in, even_q_norm=v_even_q_norm, even_w_uq=v_even_w_uq, even_kv_norm=v_even_kv_norm,
               even_w_ukv=v_even_w_ukv, even_b_f=v_even_b_f, even_w_out=v_even_w_out, even_ln_g=v_even_ln_g,
               even_ln_b=v_even_ln_b, odd_w_in=v_odd_w_in, odd_sinks=v_odd_sinks, odd_w_out=v_odd_w_out,
               odd_ln_g=v_odd_ln_g, odd_ln_b=v_odd_ln_b)
    sharded = BIG + SMALL_SHARDED
    padded = lambda d, n: _pad_last(d[n], PAD_TO.get(n, d[n].shape[-1]))

    tabs = {"mla": _rope_tables(x.shape[1], "mla"), "swa": _rope_tables(x.shape[1], "swa")}
    keys = lambda layers: [(n, layer // 2) for layer in layers for n in _layer_names(layer)[0]]
    first, rest = keys([0]), keys([1, 2, 3])
    me = _lin(_me())

    def shard(n, j):
        a = padded(w, n)[j]
        return a.astype(BF16) if n in BIG else a

    to_full = lambda n, g: _join(g, SHARD_AXIS[n] - 1).astype(F32)
    to_slots = lambda n, g: _split(g, SHARD_AXIS[n] - 1).astype(BF16 if n in BIG else F32)

    def layer_params(layer, full_of):
        shn, rpn = _layer_names(layer)
        p = {n: full_of(n) for n in shn}
        p.update({n: w[n][layer // 2] for n in rpn})
        return p

    got0 = _all_gather([shard(n, j) for n, j in first], "all_gather_first")
    got0, mine_rest = lax.optimization_barrier((got0, [shard(n, j) for n, j in rest]))
    got0 = dict(zip(first, got0))
    send_sems, recv_sems, srcs, lands, token = _split_start(mine_rest, False, "all_gather_rest_start")
    p0 = layer_params(0, lambda n: to_full(n, got0[(n, 0)]) + token[0, 0])
    x1, vjp0 = jax.vjp(lambda p, xx: _layer_apply(0, p, xx, tabs), p0, x[0])
    got = _split_wait(send_sems, recv_sems, srcs, lands, x1, False, "all_gather_rest_wait")
    got = dict(zip(rest, _own_slot(got, [m[None] for m in mine_rest])))

    xs, vjps = x1, [vjp0]
    for layer in (1, 2, 3):
        p = layer_params(layer, lambda n: to_full(n, got[(n, layer // 2)]))
        xs, vjp = jax.vjp(lambda p_, xx, layer=layer: _layer_apply(layer, p_, xx, tabs), p, xs)
        vjps.append(vjp)
    loss_local, vjp_loss = jax.vjp(lambda y: mse_loss(y, loss_target[0]), xs)
    (dy,) = vjp_loss(jnp.ones((), F32))
    loss = lax.psum(loss_local, AXES)
    grads = {}
    for layer in (3, 2, 1):
        grads[layer], dy = vjps[layer](dy)

    parts_rest = [to_slots(n, grads[_layer_of(n, j)][n]) for n, j in rest]
    send_sems, recv_sems, srcs, lands, token = _split_start(parts_rest, True, "grad_exchange_rest_start")
    grads[0], grad_x = vjps[0](dy + token[0, 0])
    recv_rest = _split_wait(send_sems, recv_sems, srcs, lands, grad_x, True, "grad_exchange_rest_wait")
    recv = dict(zip(rest, _own_slot(recv_rest, [lax.dynamic_slice_in_dim(p, me, 1, axis=0) for p in parts_rest])))
    repl_grad = lambda n: jnp.stack([grads[_layer_of(n, j)][n] for j in (0, 1)])
    repl_rows = _pad_rows(jnp.concatenate([repl_grad(n).reshape(-1) for n in REPL]), 8)
    parts_last = [to_slots(n, grads[0][n]) for n, j in first]
    parts_last.append(jnp.broadcast_to(repl_rows[None], (N_DEV,) + repl_rows.shape))
    recv_last = _exchange(parts_last, "grad_exchange_last")
    recv.update(zip(first, recv_last[:-1]))

    g_out, d_out, m_out, v_out = {}, {}, {}, {}
    for n in sharded:
        r = jnp.stack([recv[(n, 0)], recv[(n, 1)]], axis=1)
        cols = r.shape[-1]
        flat2 = lambda d: padded(d, n).reshape(-1, cols)
        outs = _sum_adamw(r.reshape(N_DEV, -1, cols), flat2(w), flat2(mom), flat2(vel))
        for dst, o in zip((g_out, d_out, m_out, v_out), outs):
            dst[n] = o.reshape(w[n].shape[:-1] + (cols,))[..., :w[n].shape[-1]]
    pack = lambda d: _pad_rows(jnp.concatenate([d[n].reshape(-1) for n in REPL]), 8)
    outs = _sum_adamw(recv_last[-1], pack(w), pack(mom), pack(vel))
    for dst, o in zip((g_out, d_out, m_out, v_out), outs):
        flat, off = o.reshape(-1), 0
        for n in REPL:
            size = math.prod(w[n].shape)
            dst[n] = flat[off:off + size].reshape(w[n].shape)
            off += size
    return (loss, grad_x[None], *[g_out[n] for n in WEIGHTS], *[d_out[n] for n in WEIGHTS],
            *[m_out[n] for n in WEIGHTS], *[v_out[n] for n in WEIGHTS])
```

```python
import functools
import math

import jax
import jax.numpy as jnp
from jax import lax
from jax.experimental import pallas as pl
from jax.experimental.pallas import tpu as pltpu

F32 = jnp.float32
BF16 = jnp.bfloat16
LANES = 128
HALF = 64
N_DEV = 8
AXES = ("x", "y", "c")
VMEM_LIMIT = 48 * 1024 * 1024

D_MODEL = 1024
DEPTH = 4
ROPE_THETA = 10000.0
MLA_HEADS, MLA_NOPE, MLA_ROPE, MLA_V, MLA_Q_RANK, MLA_KV_RANK = 8, 64, 32, 64, 256, 128
FOX_HEADS, FOX_DIM = 8, 64
SWA_HEADS, SWA_KV_HEADS, SWA_DIM, WINDOW = 16, 2, 64, 128
RMS_EPS, LN_EPS = 1e-6, 1e-5
ALPHA = (2 * DEPTH) ** 0.25
ADAM_LR, ADAM_B1, ADAM_B2, ADAM_EPS, ADAM_WD, ADAM_STEP = 0.001, 0.9, 0.999, 1e-08, 0.01, 10
NEG = -1e30

WEIGHTS = ["even_w_in", "even_q_norm", "even_w_uq", "even_kv_norm", "even_w_ukv", "even_b_f", "even_w_out",
           "even_ln_g", "even_ln_b", "odd_w_in", "odd_sinks", "odd_w_out", "odd_ln_g", "odd_ln_b"]
SHARD_AXIS = {"even_w_in": 2, "even_w_uq": 2, "even_w_ukv": 2, "even_w_out": 1, "odd_w_in": 2, "odd_w_out": 1,
              "odd_ln_g": 1, "odd_ln_b": 1, "even_q_norm": None, "even_kv_norm": None, "even_b_f": None,
              "even_ln_g": None, "even_ln_b": None, "odd_sinks": None}
BIG = ["even_w_in", "even_w_uq", "even_w_ukv", "even_w_out", "odd_w_in", "odd_w_out"]
SMALL_SHARDED = ["odd_ln_g", "odd_ln_b"]
REPL = [n for n in WEIGHTS if SHARD_AXIS[n] is None]


def _pick(n, cands):
    for c in cands:
        if n % c == 0:
            return c
    return n


def _params(**kw):
    return pltpu.CompilerParams(vmem_limit_bytes=VMEM_LIMIT, **kw)


def _me():
    return lax.axis_index("x"), lax.axis_index("y"), lax.axis_index("c")


def _peer(k):
    x, y, c = _me()
    px = 1 - x if (k >> 2) & 1 else x
    py = 1 - y if (k >> 1) & 1 else y
    pc = 1 - c if k & 1 else c
    return px, py, pc


def _lin(p):
    return 4 * p[0] + 2 * p[1] + p[2]


def _comm_call(body, n, out_shape, args, name):
    any_spec = pl.BlockSpec(memory_space=pl.ANY)
    return pl.pallas_call(
        body, name=name, out_shape=out_shape, in_specs=[any_spec] * n, out_specs=[any_spec] * n,
        scratch_shapes=[pltpu.SemaphoreType.DMA((n, N_DEV - 1)), pltpu.SemaphoreType.DMA((n, N_DEV - 1)),
                        pltpu.SemaphoreType.DMA((n,))],
    )(*args)


def _all_gather(xs, name):
    n = len(xs)

    def body(*refs):
        x_refs, out_refs = refs[:n], refs[n:2 * n]
        send_sems, recv_sems, local_sems = refs[2 * n:]
        me = _lin(_me())
        local = [pltpu.make_async_copy(x_refs[a], out_refs[a].at[me], local_sems.at[a]) for a in range(n)]
        for cp in local:
            cp.start()
        sends = []
        for k in range(1, N_DEV):
            for a in range(n):
                cp = pltpu.make_async_remote_copy(
                    src_ref=x_refs[a], dst_ref=out_refs[a].at[me], send_sem=send_sems.at[a, k - 1],
                    recv_sem=recv_sems.at[a, k - 1], device_id=_peer(k), device_id_type=pl.DeviceIdType.MESH)
                cp.start()
                sends.append(cp)
        for k in range(1, N_DEV):
            for a in range(n):
                pltpu.make_async_remote_copy(
                    src_ref=x_refs[a], dst_ref=out_refs[a].at[_lin(_peer(k))], send_sem=send_sems.at[a, k - 1],
                    recv_sem=recv_sems.at[a, k - 1], device_id=_peer(k),
                    device_id_type=pl.DeviceIdType.MESH).wait_recv()
        for cp in sends:
            cp.wait_send()
        for cp in local:
            cp.wait()

    out_shape = [jax.ShapeDtypeStruct((N_DEV,) + x.shape, x.dtype) for x in xs]
    return _comm_call(body, n, out_shape, xs, name)


def _exchange(parts, name):
    n = len(parts)

    def body(*refs):
        p_refs, out_refs = refs[:n], refs[n:2 * n]
        send_sems, recv_sems, local_sems = refs[2 * n:]
        me = _lin(_me())
        local = [pltpu.make_async_copy(p_refs[a].at[me], out_refs[a].at[me], local_sems.at[a]) for a in range(n)]
        for cp in local:
            cp.start()
        sends = []
        for k in range(1, N_DEV):
            peer = _peer(k)
            for a in range(n):
                cp = pltpu.make_async_remote_copy(
                    src_ref=p_refs[a].at[_lin(peer)], dst_ref=out_refs[a].at[me], send_sem=send_sems.at[a, k - 1],
                    recv_sem=recv_sems.at[a, k - 1], device_id=peer, device_id_type=pl.DeviceIdType.MESH)
                cp.start()
                sends.append(cp)
        for k in range(1, N_DEV):
            peer = _peer(k)
            for a in range(n):
                pltpu.make_async_remote_copy(
                    src_ref=p_refs[a].at[_lin(peer)], dst_ref=out_refs[a].at[_lin(peer)],
                    send_sem=send_sems.at[a, k - 1], recv_sem=recv_sems.at[a, k - 1], device_id=peer,
                    device_id_type=pl.DeviceIdType.MESH).wait_recv()
        for cp in sends:
            cp.wait_send()
        for cp in local:
            cp.wait()

    out_shape = [jax.ShapeDtypeStruct(p.shape, p.dtype) for p in parts]
    return _comm_call(body, n, out_shape, parts, name)


_HBM = pl.BlockSpec(memory_space=pltpu.HBM)
_SEM = pl.BlockSpec(memory_space=pltpu.SEMAPHORE)
_EFFECT = pltpu.SideEffectType.DATAFLOW_SIDE_EFFECTING


def _split_start(srcs, slotted, name):
    n = len(srcs)
    lands = [lax.empty(s.shape if slotted else (N_DEV,) + s.shape, s.dtype) for s in srcs]

    def body(*refs):
        src_refs, land_refs = refs[:n], refs[n:2 * n]
        send_sems, recv_sems, token = refs[2 * n], refs[2 * n + 1], refs[-1]
        me = _lin(_me())
        for k in range(1, N_DEV):
            peer = _peer(k)
            for a in range(n):
                pltpu.make_async_remote_copy(
                    src_ref=src_refs[a].at[_lin(peer)] if slotted else src_refs[a], dst_ref=land_refs[a].at[me],
                    send_sem=send_sems.at[a * (N_DEV - 1) + k - 1], recv_sem=recv_sems.at[a * (N_DEV - 1) + k - 1],
                    device_id=peer, device_id_type=pl.DeviceIdType.MESH).start()
        token[...] = jnp.zeros_like(token)

    both = list(srcs) + lands
    outs = pl.pallas_call(
        body, name=name,
        out_shape=(pltpu.SemaphoreType.DMA((n * (N_DEV - 1),)), pltpu.SemaphoreType.DMA((n * (N_DEV - 1),)),
                   *[pltpu.HBM(b.shape, b.dtype) for b in both], jax.ShapeDtypeStruct((8, LANES), F32)),
        in_specs=[_HBM] * (2 * n), out_specs=(_SEM, _SEM, *[_HBM] * (2 * n), pl.BlockSpec(memory_space=pltpu.VMEM)),
        input_output_aliases={a: 2 + a for a in range(2 * n)},
        compiler_params=pltpu.CompilerParams(has_side_effects=_EFFECT),
    )(*[pltpu.with_memory_space_constraint(b, pltpu.HBM) for b in both])
    return outs[0], outs[1], list(outs[2:2 + n]), list(outs[2 + n:2 + 2 * n]), outs[-1]


def _split_wait(send_sems, recv_sems, srcs, lands, after, slotted, name):
    n = len(srcs)

    def body(*refs):
        src_refs, land_refs = refs[:n], refs[n:2 * n]
        send_sems, recv_sems = refs[2 * n], refs[2 * n + 1]
        for k in range(1, N_DEV):
            peer = _peer(k)
            for a in range(n):
                cp = pltpu.make_async_remote_copy(
                    src_ref=src_refs[a].at[_lin(peer)] if slotted else src_refs[a],
                    dst_ref=land_refs[a].at[_lin(peer)], send_sem=send_sems.at[a * (N_DEV - 1) + k - 1],
                    recv_sem=recv_sems.at[a * (N_DEV - 1) + k - 1], device_id=peer,
                    device_id_type=pl.DeviceIdType.MESH)
                cp.wait_send()
                cp.wait_recv()

    both = list(srcs) + list(lands)
    outs = pl.pallas_call(
        body, name=name, out_shape=[pltpu.HBM(b.shape, b.dtype) for b in both],
        in_specs=[_HBM] * (2 * n) + [_SEM, _SEM, pl.BlockSpec(memory_space=pl.ANY)], out_specs=[_HBM] * (2 * n),
        input_output_aliases={a: a for a in range(2 * n)},
        compiler_params=pltpu.CompilerParams(has_side_effects=_EFFECT),
    )(*both, send_sems, recv_sems, after)
    return list(outs[n:])


def _own_slot(lands, own):
    me = _lin(_me())
    slot = lambda l: lax.broadcasted_iota(jnp.int32, (N_DEV,) + (1,) * (l.ndim - 1), 0)
    return [jnp.where(slot(l) == me, o.astype(l.dtype), l) for l, o in zip(lands, own)]


def _sum_adamw(recv, w, m, v):
    _, rows, lanes = recv.shape
    tr = _pick(rows, (256, 128, 64, 32, 16, 8))
    c1 = 1.0 - ADAM_B1 ** ADAM_STEP
    c2 = 1.0 - ADAM_B2 ** ADAM_STEP

    def body(r_ref, w_ref, m_ref, v_ref, g_out, d_out, m_out, v_out):
        g = r_ref[0].astype(F32)
        for s in range(1, N_DEV):
            g = g + r_ref[s].astype(F32)
        mn = ADAM_B1 * m_ref[...] + (1.0 - ADAM_B1) * g
        vn = ADAM_B2 * v_ref[...] + (1.0 - ADAM_B2) * (g * g)
        m_hat = mn / c1
        v_hat = vn / c2
        g_out[...] = g
        d_out[...] = -ADAM_LR * (m_hat / (jnp.sqrt(v_hat) + ADAM_EPS) + ADAM_WD * w_ref[...])
        m_out[...] = mn
        v_out[...] = vn

    blk = pl.BlockSpec((tr, lanes), lambda i: (i, 0))
    shp = jax.ShapeDtypeStruct((rows, lanes), F32)
    return pl.pallas_call(
        body, name=f"sum_adamw_{rows}x{lanes}", grid=(rows // tr,),
        in_specs=[pl.BlockSpec((N_DEV, tr, lanes), lambda i: (0, i, 0)), blk, blk, blk],
        out_specs=[blk, blk, blk, blk], out_shape=[shp, shp, shp, shp],
        compiler_params=_params(dimension_semantics=("parallel",)),
    )(recv, w, m, v)


def _mm_nn(a, b):
    m, k = a.shape
    _, n = b.shape
    tm = _pick(m, (1024, 512, 256, 128))
    tn = _pick(n, (1024, 640, 512, 256, 128))
    tk = _pick(k, (1024, 640, 512, 256, 128))
    nk = k // tk

    def body(a_ref, b_ref, o_ref, acc_ref):
        kk = pl.program_id(2)

        @pl.when(kk == 0)
        def _():
            acc_ref[...] = jnp.zeros_like(acc_ref)

        acc_ref[...] += jnp.dot(a_ref[...].astype(BF16), b_ref[...].astype(BF16), preferred_element_type=F32)

        @pl.when(kk == nk - 1)
        def _():
            o_ref[...] = acc_ref[...]

    return pl.pallas_call(
        body, name=f"mm_nn_{m}x{k}x{n}", grid=(m // tm, n // tn, nk),
        in_specs=[pl.BlockSpec((tm, tk), lambda i, j, kk: (i, kk)), pl.BlockSpec((tk, tn), lambda i, j, kk: (kk, j))],
        out_specs=pl.BlockSpec((tm, tn), lambda i, j, kk: (i, j)),
        out_shape=jax.ShapeDtypeStruct((m, n), F32),
        scratch_shapes=[pltpu.VMEM((tm, tn), F32)],
        compiler_params=_params(dimension_semantics=("parallel", "parallel", "arbitrary")),
    )(a, b)


def _mm_tn(a, g):
    s, k = a.shape
    _, n = g.shape
    tm = _pick(k, (1024, 512, 256, 128))
    tn = _pick(n, (1024, 640, 512, 256, 128))
    ts = _pick(s, (512, 256, 128))
    ns = s // ts

    def body(a_ref, g_ref, o_ref, acc_ref):
        ss = pl.program_id(2)

        @pl.when(ss == 0)
        def _():
            acc_ref[...] = jnp.zeros_like(acc_ref)

        acc_ref[...] += lax.dot_general(a_ref[...].astype(BF16), g_ref[...].astype(BF16),
                                        (((0,), (0,)), ((), ())), preferred_element_type=F32)

        @pl.when(ss == ns - 1)
        def _():
            o_ref[...] = acc_ref[...]

    return pl.pallas_call(
        body, name=f"mm_tn_{s}x{k}x{n}", grid=(k // tm, n // tn, ns),
        in_specs=[pl.BlockSpec((ts, tm), lambda i, j, ss: (ss, i)), pl.BlockSpec((ts, tn), lambda i, j, ss: (ss, j))],
        out_specs=pl.BlockSpec((tm, tn), lambda i, j, ss: (i, j)),
        out_shape=jax.ShapeDtypeStruct((k, n), F32),
        scratch_shapes=[pltpu.VMEM((tm, tn), F32)],
        compiler_params=_params(dimension_semantics=("parallel", "parallel", "arbitrary")),
    )(a, g)


@jax.custom_vjp
def mm(a, w):
    return _mm_nn(a, w.astype(BF16))


def _mm_fwd(a, w):
    wb = w.astype(BF16)
    return _mm_nn(a, wb), (a, wb)


def _mm_bwd(res, g):
    a, wb = res
    return _mm_nn(g, wb.T), _mm_tn(a, g)


mm.defvjp(_mm_fwd, _mm_bwd)


def _make_in_proj(widths):
    cuts = [sum(widths[:i]) for i in range(len(widths) + 1)]

    @jax.custom_vjp
    def in_proj(x, w):
        return fwd(x, w)[0]

    def fwd(x, w):
        xb, wb = x.astype(BF16), w.astype(BF16)
        return tuple(_mm_nn(xb, wb[:, a:b]) for a, b in zip(cuts[:-1], cuts[1:])), (xb, wb)

    def bwd(res, gs):
        xb, wb = res
        g = jnp.concatenate(gs, axis=1)
        return _mm_nn(g, wb.T), _mm_tn(xb, g)

    in_proj.defvjp(fwd, bwd)
    return in_proj


EVEN_GROUPS = (256, 128, 128, 512, 512, 512, 128, 1024)
ODD_GROUPS = (1024, 256, 256, 1024)
even_in_proj = _make_in_proj(EVEN_GROUPS)
odd_in_proj = _make_in_proj(ODD_GROUPS)

SHARD_PAD = 384


def _source_columns(kind):
    if kind == "even":
        src = [list(range(0, 384)), [-1] * 64, list(range(384, 416)), [-1] * 32, list(range(416, 1952)),
               list(range(1952, 1960)), [-1] * 120, list(range(1960, 2984))]
        return sum(src, []), 373
    q0, k0, v0, g0 = 0, 1024, 1152, 1280
    dup = lambda base: [base + 64 * g + c for g in range(SWA_KV_HEADS) for _ in range(2) for c in range(64)]
    return list(range(q0, k0)) + dup(k0) + dup(v0) + list(range(g0, 2304)), 288


def _selection(kind, transposed):
    src, shard = _source_columns(kind)
    cat = [s + (SHARD_PAD - shard) * (s // shard) if s >= 0 else -1 for s in src]
    cat_arr = jnp.asarray(cat, jnp.int32)
    if transposed:
        cols = lax.broadcasted_iota(jnp.int32, (len(src), N_DEV * SHARD_PAD), 1)
        return (cols == cat_arr[:, None]).astype(BF16), [(c, r) for c, r in enumerate(cat) if r >= 0]
    rows = lax.broadcasted_iota(jnp.int32, (N_DEV * SHARD_PAD, len(src)), 0)
    return (rows == cat_arr[None, :]).astype(BF16), [(r, c) for c, r in enumerate(cat) if r >= 0]


def _mm_banded(a, b, nonzeros):
    m, k = a.shape
    _, n = b.shape
    tm = _pick(m, (1024, 512, 256, 128))
    tn = _pick(n, (1024, 640, 512, 256, 128))
    tk = _pick(k, (1024, 640, 512, 256, 128))
    lo, hi = [k // tk] * (n // tn), [-1] * (n // tn)
    for r, c in nonzeros:
        lo[c // tn], hi[c // tn] = min(lo[c // tn], r // tk), max(hi[c // tn], r // tk)
    first = [l if h >= 0 else 0 for l, h in zip(lo, hi)]
    count = [h - l + 1 if h >= 0 else 0 for l, h in zip(lo, hi)]
    steps = max(count)

    def body(first_ref, count_ref, a_ref, b_ref, o_ref, acc_ref):
        j, kk = pl.program_id(1), pl.program_id(2)

        @pl.when(kk == 0)
        def _():
            acc_ref[...] = jnp.zeros_like(acc_ref)

        @pl.when(kk < count_ref[j])
        def _():
            acc_ref[...] += jnp.dot(a_ref[...].astype(BF16), b_ref[...].astype(BF16), preferred_element_type=F32)

        @pl.when(kk == steps - 1)
        def _():
            o_ref[...] = acc_ref[...]

    kblk = lambda j, kk, f, c: jnp.minimum(f[j] + kk, f[j] + jnp.maximum(c[j], 1) - 1)
    return pl.pallas_call(
        body, name=f"mm_banded_{m}x{k}x{n}",
        grid_spec=pltpu.PrefetchScalarGridSpec(
            num_scalar_prefetch=2, grid=(m // tm, n // tn, steps),
            in_specs=[pl.BlockSpec((tm, tk), lambda i, j, kk, f, c: (i, kblk(j, kk, f, c))),
                      pl.BlockSpec((tk, tn), lambda i, j, kk, f, c: (kblk(j, kk, f, c), j))],
            out_specs=pl.BlockSpec((tm, tn), lambda i, j, kk, f, c: (i, j)),
            scratch_shapes=[pltpu.VMEM((tm, tn), F32)]),
        out_shape=jax.ShapeDtypeStruct((m, n), F32),
        compiler_params=_params(dimension_semantics=("parallel", "parallel", "arbitrary")),
    )(jnp.asarray(first, jnp.int32), jnp.asarray(count, jnp.int32), a, b)


@functools.partial(jax.custom_vjp, nondiff_argnums=(1,))
def relayout(wcat, kind):
    return _mm_banded(wcat, *_selection(kind, False))


def _relayout_bwd(kind, _, g):
    return (_mm_banded(g, *_selection(kind, True)),)


relayout.defvjp(lambda wcat, kind: (_mm_banded(wcat, *_selection(kind, False)), None), _relayout_bwd)


def _row_block(s):
    return _pick(s, (512, 256, 128, 64, 32, 16, 8))


def _rms_fwd_call(x, g):
    s, k = x.shape
    tr = _row_block(s)

    def body(x_ref, g_ref, o_ref):
        xv = x_ref[...]
        r = lax.rsqrt(jnp.mean(xv * xv, axis=-1, keepdims=True) + RMS_EPS)
        o_ref[...] = xv * r * g_ref[...]

    return pl.pallas_call(
        body, name=f"rms_fwd_{k}", grid=(s // tr,),
        in_specs=[pl.BlockSpec((tr, k), lambda i: (i, 0)), pl.BlockSpec((1, k), lambda i: (0, 0))],
        out_specs=pl.BlockSpec((tr, k), lambda i: (i, 0)), out_shape=jax.ShapeDtypeStruct((s, k), F32),
        compiler_params=_params(dimension_semantics=("parallel",)),
    )(x, g.reshape(1, k))


def _rms_bwd_call(x, g, dy):
    s, k = x.shape
    tr = _row_block(s)

    def body(x_ref, g_ref, dy_ref, dx_ref, dg_ref):
        @pl.when(pl.program_id(0) == 0)
        def _():
            dg_ref[...] = jnp.zeros_like(dg_ref)

        xv = x_ref[...]
        r = lax.rsqrt(jnp.mean(xv * xv, axis=-1, keepdims=True) + RMS_EPS)
        xh = xv * r
        dyv = dy_ref[...]
        dg_ref[...] += jnp.sum(dyv * xh, axis=0, keepdims=True)
        dxh = dyv * g_ref[...]
        dx_ref[...] = r * (dxh - xh * jnp.mean(dxh * xh, axis=-1, keepdims=True))

    dx, dg = pl.pallas_call(
        body, name=f"rms_bwd_{k}", grid=(s // tr,),
        in_specs=[pl.BlockSpec((tr, k), lambda i: (i, 0)), pl.BlockSpec((1, k), lambda i: (0, 0)),
                  pl.BlockSpec((tr, k), lambda i: (i, 0))],
        out_specs=[pl.BlockSpec((tr, k), lambda i: (i, 0)), pl.BlockSpec((1, k), lambda i: (0, 0))],
        out_shape=[jax.ShapeDtypeStruct((s, k), F32), jax.ShapeDtypeStruct((1, k), F32)],
        compiler_params=_params(dimension_semantics=("arbitrary",)),
    )(x, g.reshape(1, k), dy)
    return dx, dg.reshape(k)


@jax.custom_vjp
def rms_norm(x, g):
    return _rms_fwd_call(x, g)


rms_norm.defvjp(lambda x, g: (_rms_fwd_call(x, g), (x, g)), lambda res, dy: _rms_bwd_call(res[0], res[1], dy))


def _ln_fwd_call(x, y, g, b):
    s, k = x.shape
    tr = _row_block(s)

    def body(x_ref, y_ref, g_ref, b_ref, o_ref):
        u = ALPHA * x_ref[...] + y_ref[...]
        mu = jnp.mean(u, axis=-1, keepdims=True)
        d = u - mu
        var = jnp.mean(d * d, axis=-1, keepdims=True)
        o_ref[...] = d * lax.rsqrt(var + LN_EPS) * g_ref[...] + b_ref[...]

    row = pl.BlockSpec((tr, k), lambda i: (i, 0))
    vec = pl.BlockSpec((1, k), lambda i: (0, 0))
    return pl.pallas_call(
        body, name="ln_fwd", grid=(s // tr,), in_specs=[row, row, vec, vec], out_specs=row,
        out_shape=jax.ShapeDtypeStruct((s, k), F32), compiler_params=_params(dimension_semantics=("parallel",)),
    )(x, y, g.reshape(1, k), b.reshape(1, k))


def _ln_bwd_call(x, y, g, do):
    s, k = x.shape
    tr = _row_block(s)

    def body(x_ref, y_ref, g_ref, do_ref, dx_ref, dy_ref, dg_ref, db_ref):
        @pl.when(pl.program_id(0) == 0)
        def _():
            dg_ref[...] = jnp.zeros_like(dg_ref)
            db_ref[...] = jnp.zeros_like(db_ref)

        u = ALPHA * x_ref[...] + y_ref[...]
        mu = jnp.mean(u, axis=-1, keepdims=True)
        d = u - mu
        r = lax.rsqrt(jnp.mean(d * d, axis=-1, keepdims=True) + LN_EPS)
        xh = d * r
        dov = do_ref[...]
        dg_ref[...] += jnp.sum(dov * xh, axis=0, keepdims=True)
        db_ref[...] += jnp.sum(dov, axis=0, keepdims=True)
        dxh = dov * g_ref[...]
        du = r * (dxh - jnp.mean(dxh, axis=-1, keepdims=True) - xh * jnp.mean(dxh * xh, axis=-1, keepdims=True))
        dy_ref[...] = du
        dx_ref[...] = ALPHA * du

    row = pl.BlockSpec((tr, k), lambda i: (i, 0))
    vec = pl.BlockSpec((1, k), lambda i: (0, 0))
    dx, dy, dg, db = pl.pallas_call(
        body, name="ln_bwd", grid=(s // tr,), in_specs=[row, row, vec, row], out_specs=[row, row, vec, vec],
        out_shape=[jax.ShapeDtypeStruct((s, k), F32), jax.ShapeDtypeStruct((s, k), F32),
                   jax.ShapeDtypeStruct((1, k), F32), jax.ShapeDtypeStruct((1, k), F32)],
        compiler_params=_params(dimension_semantics=("arbitrary",)),
    )(x, y, g.reshape(1, k), do)
    return dx, dy, dg.reshape(k), db.reshape(k)


@jax.custom_vjp
def ln_res(x, y, g, b):
    return _ln_fwd_call(x, y, g, b)


ln_res.defvjp(lambda x, y, g, b: (_ln_fwd_call(x, y, g, b), (x, y, g)),
              lambda res, do: _ln_bwd_call(res[0], res[1], res[2], do))


def _rope_call(x, c, s1, s2, hs):
    s, w = x.shape
    tr = _row_block(s)
    nb = w // LANES

    def body(x_ref, c_ref, s1_ref, s2_ref, o_ref):
        cv, s1v, s2v = c_ref[...], s1_ref[...], s2_ref[...]
        for cb in range(nb):
            xb = x_ref[:, cb * LANES:(cb + 1) * LANES]
            o_ref[:, cb * LANES:(cb + 1) * LANES] = (
                xb * cv + pltpu.roll(xb, LANES - hs, 1) * s1v + pltpu.roll(xb, hs, 1) * s2v)

    row = pl.BlockSpec((tr, w), lambda i: (i, 0))
    tab = pl.BlockSpec((tr, LANES), lambda i: (i, 0))
    return pl.pallas_call(
        body, name=f"rope_{w}_{hs}", grid=(s // tr,), in_specs=[row, tab, tab, tab], out_specs=row,
        out_shape=jax.ShapeDtypeStruct((s, w), F32), compiler_params=_params(dimension_semantics=("parallel",)),
    )(x, c, s1, s2)


@functools.partial(jax.custom_vjp, nondiff_argnums=(2,))
def rope(x, tabs, hs):
    return _rope_call(x, tabs[0], tabs[1], tabs[2], hs)


def _rope_fwd(x, tabs, hs):
    return _rope_call(x, tabs[0], tabs[1], tabs[2], hs), tabs


def _rope_bwd(hs, tabs, dy):
    return _rope_call(dy, tabs[0], -tabs[1], -tabs[2], hs), jax.tree.map(jnp.zeros_like, tabs)


rope.defvjp(_rope_fwd, _rope_bwd)


def _rope_tables(s, layout):
    pos = jnp.arange(s, dtype=F32)[:, None]
    lane = jnp.arange(LANES)
    if layout == "mla":
        dim, hs = MLA_ROPE, MLA_ROPE // 2
        r = lane - MLA_NOPE
        active = (r >= 0) & (r < MLA_ROPE)
    else:
        dim, hs = SWA_DIM, SWA_DIM // 2
        r = lane % SWA_DIM
        active = jnp.ones_like(lane, dtype=bool)
    f = jnp.where(active, r % hs, 0)
    inv = ROPE_THETA ** (-(2.0 * f.astype(F32)) / dim)
    ang = pos * inv[None, :]
    cos, sin = jnp.cos(ang), jnp.sin(ang)
    first = (active & (r < hs))[None, :]
    second = (active & (r >= hs))[None, :]
    c = jnp.where(active[None, :], cos, 1.0)
    s1 = jnp.where(first, -sin, 0.0)
    s2 = jnp.where(second, sin, 0.0)
    return (c, s1, s2), hs


def _gate_fwd_call(o_parts, gate):
    s, w = gate.shape
    tr = _row_block(s)
    widths = [o.shape[1] for o in o_parts]

    def body(*refs):
        o_refs, g_ref, z_ref = refs[:len(widths)], refs[len(widths)], refs[len(widths) + 1]
        off = 0
        for o_ref, wd in zip(o_refs, widths):
            gv = g_ref[:, off:off + wd]
            z_ref[:, off:off + wd] = o_ref[...] * (gv * jax.nn.sigmoid(gv))
            off += wd

    specs = [pl.BlockSpec((tr, wd), lambda i: (i, 0)) for wd in widths]
    row = pl.BlockSpec((tr, w), lambda i: (i, 0))
    return pl.pallas_call(
        body, name=f"gate_fwd_{len(widths)}", grid=(s // tr,), in_specs=specs + [row], out_specs=row,
        out_shape=jax.ShapeDtypeStruct((s, w), F32), compiler_params=_params(dimension_semantics=("parallel",)),
    )(*o_parts, gate)


def _gate_bwd_call(o_parts, gate, dz):
    s, w = gate.shape
    tr = _row_block(s)
    widths = [o.shape[1] for o in o_parts]
    n = len(widths)

    def body(*refs):
        o_refs, g_ref, dz_ref = refs[:n], refs[n], refs[n + 1]
        do_refs, dg_ref = refs[n + 2:2 * n + 2], refs[2 * n + 2]
        off = 0
        for o_ref, do_ref, wd in zip(o_refs, do_refs, widths):
            gv = g_ref[:, off:off + wd]
            sg = jax.nn.sigmoid(gv)
            dzv = dz_ref[:, off:off + wd]
            do_ref[...] = dzv * (gv * sg)
            dg_ref[:, off:off + wd] = dzv * o_ref[...] * (sg * (1.0 + gv * (1.0 - sg)))
            off += wd

    specs = [pl.BlockSpec((tr, wd), lambda i: (i, 0)) for wd in widths]
    row = pl.BlockSpec((tr, w), lambda i: (i, 0))
    outs = pl.pallas_call(
        body, name=f"gate_bwd_{n}", grid=(s // tr,), in_specs=specs + [row, row], out_specs=specs + [row],
        out_shape=[jax.ShapeDtypeStruct((s, wd), F32) for wd in widths] + [jax.ShapeDtypeStruct((s, w), F32)],
        compiler_params=_params(dimension_semantics=("parallel",)),
    )(*o_parts, gate, dz)
    return tuple(outs[:n]), outs[n]


@jax.custom_vjp
def gate_mul(o_parts, gate):
    return _gate_fwd_call(o_parts, gate)


gate_mul.defvjp(lambda o_parts, gate: (_gate_fwd_call(o_parts, gate), (o_parts, gate)),
                lambda res, dz: _gate_bwd_call(res[0], res[1], dz))


def _loss_call(y, t):
    s, k = y.shape
    tr = _row_block(s)
    nsteps = s // tr

    def body(y_ref, t_ref, l_ref, dy_ref, acc_ref):
        i = pl.program_id(0)

        @pl.when(i == 0)
        def _():
            acc_ref[...] = jnp.zeros_like(acc_ref)

        d = y_ref[...] - t_ref[...]
        dy_ref[...] = d / k
        acc_ref[...] += jnp.sum(d * d, axis=0, keepdims=True)

        @pl.when(i == nsteps - 1)
        def _():
            tot = jnp.sum(acc_ref[...], axis=1, keepdims=True) * (0.5 / k)
            l_ref[...] = jnp.broadcast_to(tot, l_ref.shape)

    row = pl.BlockSpec((tr, k), lambda i: (i, 0))
    return pl.pallas_call(
        body, name="loss", grid=(nsteps,), in_specs=[row, row],
        out_specs=[pl.BlockSpec((1, LANES), lambda i: (0, 0)), row],
        out_shape=[jax.ShapeDtypeStruct((1, LANES), F32), jax.ShapeDtypeStruct((s, k), F32)],
        scratch_shapes=[pltpu.VMEM((1, k), F32)], compiler_params=_params(dimension_semantics=("arbitrary",)),
    )(y, t)


@jax.custom_vjp
def mse_loss(y, t):
    return _loss_call(y, t)[0][0, 0]


def _mse_fwd(y, t):
    l, dy = _loss_call(y, t)
    return l[0, 0], (dy, t)


mse_loss.defvjp(_mse_fwd, lambda res, g: (g * res[0], jnp.zeros_like(res[1])))


def _scan_call(x, b, mode):
    s, w = x.shape
    nt = s // 8

    def tile_scan(t):
        row = lax.broadcasted_iota(jnp.int32, (8, w), 0)
        for sh in (1, 2, 4):
            t = t + jnp.where(row >= sh, pltpu.roll(t, sh, 0), 0.0)
        return t

    def body(x_ref, b_ref, o_ref):
        def step(i, carry):
            rows = pl.ds(pl.multiple_of(i * 8, 8), 8)
            t = x_ref[rows, :]
            if mode == "fwd":
                t = jax.nn.log_sigmoid(t + b_ref[...])
            t = tile_scan(t) + carry
            o_ref[rows, :] = t
            return t[7:8, :]

        total = lax.fori_loop(0, nt, step, jnp.zeros((1, w), F32))
        if mode == "rev":
            def fix(i, c):
                rows = pl.ds(pl.multiple_of(i * 8, 8), 8)
                o_ref[rows, :] = total - o_ref[rows, :] + x_ref[rows, :]
                return c
            lax.fori_loop(0, nt, fix, 0)

    full = pl.BlockSpec((s, w), lambda: (0, 0))
    return pl.pallas_call(
        body, name=f"scan_{mode}", in_specs=[full, pl.BlockSpec((1, w), lambda: (0, 0))], out_specs=full,
        out_shape=jax.ShapeDtypeStruct((s, w), F32), compiler_params=_params(),
    )(x, b)


def _fox_dlogit_call(x, b, dlogf):
    s, w = x.shape
    tr = _row_block(s)

    def body(x_ref, b_ref, d_ref, dx_ref, db_ref):
        @pl.when(pl.program_id(0) == 0)
        def _():
            db_ref[...] = jnp.zeros_like(db_ref)

        dx = d_ref[...] * jax.nn.sigmoid(-(x_ref[...] + b_ref[...]))
        dx_ref[...] = dx
        db_ref[...] += jnp.sum(dx, axis=0, keepdims=True)

    row = pl.BlockSpec((tr, w), lambda i: (i, 0))
    vec = pl.BlockSpec((1, w), lambda i: (0, 0))
    return pl.pallas_call(
        body, name="fox_dlogit", grid=(s // tr,), in_specs=[row, vec, row], out_specs=[row, vec],
        out_shape=[jax.ShapeDtypeStruct((s, w), F32), jax.ShapeDtypeStruct((1, w), F32)],
        compiler_params=_params(dimension_semantics=("arbitrary",)),
    )(x, b, dlogf)


@jax.custom_vjp
def fox_cum(fl, b):
    return _scan_call(fl, b, "fwd")


def _fox_cum_bwd(res, dcum):
    fl, b = res
    dlogf = _scan_call(dcum, b, "rev")
    return _fox_dlogit_call(fl, b, dlogf)


fox_cum.defvjp(lambda fl, b: (_scan_call(fl, b, "fwd"), (fl, b)), _fox_cum_bwd)


def _lane_col(x, lane_idx):
    lane = lax.broadcasted_iota(jnp.int32, (1, x.shape[1]), 1)
    return jnp.sum(jnp.where(lane == lane_idx, x, 0.0), axis=1, keepdims=True)


def _row_of(x, row_idx):
    row = lax.broadcasted_iota(jnp.int32, (x.shape[0], 1), 0)
    return jnp.sum(jnp.where(row == row_idx, x, 0.0), axis=0, keepdims=True)


def _attn_cfg(mode, s):
    if mode == "swa":
        blk = 256 if s >= 2048 else 128
        return dict(blk=blk, n_outer=SWA_KV_HEADS, pps=4, wide=False, scale=SWA_DIM ** -0.5)
    blk = 512 if s >= 2048 else 128
    if mode == "mla":
        return dict(blk=blk, n_outer=4, pps=1, wide=True, scale=(MLA_NOPE + MLA_ROPE) ** -0.5)
    return dict(blk=blk, n_outer=4, pps=1, wide=False, scale=FOX_DIM ** -0.5)


ROW_CHUNK = 32


def _unrolled(n, body, carry):
    for c in range(n):
        carry = body(c, carry)
    return carry


def _valid_rows(mode, i, jb, blk, r0, rc):
    qpos = i * blk + r0 + lax.broadcasted_iota(jnp.int32, (rc, blk), 0)
    kpos = jb * blk + lax.broadcasted_iota(jnp.int32, (rc, blk), 1)
    ok = kpos <= qpos
    if mode == "swa":
        ok = ok & (qpos - kpos < WINDOW)
    return ok


def _attn_fwd_call(mode, q, k, v, extra):
    s = q.shape[0]
    cfg = _attn_cfg(mode, s)
    blk, n_outer, pps, wide, scale = cfg["blk"], cfg["n_outer"], cfg["pps"], cfg["wide"], cfg["scale"]
    rc = ROW_CHUNK
    nq = s // blk
    swa, fox = mode == "swa", mode == "fox"
    qw = (2 * LANES if wide else LANES) * pps
    kw = 2 * LANES if wide else LANES
    ow = LANES * pps
    reps = blk // LANES

    def body(*refs):
        if not swa:
            it_ref, jt_ref = refs[:2]
            refs = refs[2:]
        q_ref, k_ref, v_ref = refs[:3]
        n_in = 3
        if fox:
            cum_ref, cumt_ref = refs[3:5]
            n_in = 5
        if swa:
            sink_ref = refs[3]
            n_in = 4
        o_ref, lse_ref, m_ref, l_ref, acc_ref, a_ref, s_all, p_all, c_all = refs[n_in:]
        p_id = pl.program_id(0)
        if swa:
            i, j = pl.program_id(1), pl.program_id(2)
            jb, run, first, last = i - 1 + j, (i - 1 + j) >= 0, j == 0, j == 1
        else:
            i, j = it_ref[pl.program_id(1)], jt_ref[pl.program_id(1)]
            jb, first, last = j, j == 0, j == i
        lane = lax.broadcasted_iota(jnp.int32, (1, LANES), 1)
        msk = [lane < HALF, lane >= HALF]

        @pl.when(first)
        def _():
            for hh in range(2 * pps):
                if swa:
                    m_ref[hh] = jnp.broadcast_to(sink_ref[hh:hh + 1, :], (blk, LANES))
                    l_ref[hh] = jnp.ones((blk, LANES), F32)
                else:
                    m_ref[hh] = jnp.full((blk, LANES), NEG, F32)
                    l_ref[hh] = jnp.zeros((blk, LANES), F32)
            acc_ref[...] = jnp.zeros_like(acc_ref)

        def process(masked):
            for pp in range(pps):
                vb = v_ref[...]
                pvs = []
                for h in range(2):
                    hh = 2 * pp + h
                    s_ref, p_ref, c_ref = s_all.at[hh], p_all.at[hh], c_all.at[hh]
                    if wide:
                        qh = q_ref[:, h * LANES:(h + 1) * LANES] * scale
                        kh = k_ref[:, h * LANES:(h + 1) * LANES]
                    else:
                        qh = jnp.where(msk[h], q_ref[:, pp * LANES:(pp + 1) * LANES], 0.0) * scale
                        kh = k_ref[...]
                    s_ref[...] = lax.dot_general(qh.astype(BF16), kh.astype(BF16), (((1,), (1,)), ((), ())),
                                                 preferred_element_type=F32)
                    if fox:
                        head = 2 * p_id + h
                        c_ref[...] = jnp.broadcast_to(_lane_col(cum_ref[...], head), (blk, LANES))
                        ck = _row_of(cumt_ref[...], head)

                    def chunk(c, carry, hh=hh, h=h):
                        r0 = c * rc
                        rows = pl.ds(r0, rc)
                        u = s_ref[rows, :]
                        if fox:
                            u = u - ck
                        if masked:
                            u = jnp.where(_valid_rows(mode, i, jb, blk, r0, rc), u, NEG)
                        m_prev, l_prev = m_ref[hh, rows, :], l_ref[hh, rows, :]
                        m_cur = jnp.max(u, axis=1, keepdims=True)
                        if fox:
                            m_cur = m_cur + c_ref[rows, :]
                        m_next = jnp.maximum(m_prev, m_cur)
                        shift = m_next - c_ref[rows, :] if fox else m_next
                        p = jnp.exp(u - jnp.tile(shift, (1, reps)))
                        alpha = jnp.exp(m_prev - m_next)
                        l_ref[hh, rows, :] = alpha * l_prev + jnp.sum(p, axis=1, keepdims=True)
                        m_ref[hh, rows, :] = m_next
                        a_ref[hh, rows, :] = alpha
                        p_ref[rows, :] = p.astype(BF16)
                        return carry

                    _unrolled(blk // rc, chunk, 0)
                    vh = jnp.where(msk[h], vb, 0.0).astype(BF16)
                    pvs.append(jnp.dot(p_ref[...], vh, preferred_element_type=F32))
                acc_ref[pp] = acc_ref[pp] * jnp.where(msk[0], a_ref[2 * pp], a_ref[2 * pp + 1]) + pvs[0] + pvs[1]

        if swa:
            pl.when(run)(lambda: process(True))
        else:
            pl.when(j < i)(lambda: process(False))
            pl.when(j == i)(lambda: process(True))

        @pl.when(last)
        def _():
            for pp in range(pps):
                l0, l1 = l_ref[2 * pp], l_ref[2 * pp + 1]
                o_ref[:, pp * LANES:(pp + 1) * LANES] = acc_ref[pp] / jnp.where(msk[0], l0, l1)
                lse_ref[:, pp * LANES:(pp + 1) * LANES] = jnp.where(
                    msk[0], m_ref[2 * pp] + jnp.log(l0), m_ref[2 * pp + 1] + jnp.log(l1))

    if swa:
        kv_map = lambda g, i, j: (jnp.maximum(i - 1 + j, 0), g)
        q_map = lambda g, i, j: (i, g)
        grid, tables, sem = (n_outer, nq, 2), [], ("parallel", "parallel", "arbitrary")
    else:
        tri = [(i, j) for i in range(nq) for j in range(i + 1)]
        tables = [jnp.asarray([t[0] for t in tri], jnp.int32), jnp.asarray([t[1] for t in tri], jnp.int32)]
        kv_map = lambda p, t, it, jt: (jt[t], p)
        q_map = lambda p, t, it, jt: (it[t], p)
        grid, sem = (n_outer, len(tri)), ("parallel", "arbitrary")
    in_specs = [pl.BlockSpec((blk, qw), q_map), pl.BlockSpec((blk, kw), kv_map), pl.BlockSpec((blk, LANES), kv_map)]
    args = [q, k, v]
    if fox:
        cum, cumt = extra
        in_specs += [pl.BlockSpec((blk, LANES), lambda p, t, it, jt: (it[t], 0)),
                     pl.BlockSpec((8, blk), lambda p, t, it, jt: (0, jt[t]))]
        args += [cum, cumt]
    if swa:
        in_specs += [pl.BlockSpec((8, LANES), lambda g, i, j: (g, 0))]
        args += [extra]
    n_pairs = n_outer * pps
    return pl.pallas_call(
        body, name=f"attn_fwd_{mode}",
        grid_spec=pltpu.PrefetchScalarGridSpec(
            num_scalar_prefetch=len(tables), grid=grid, in_specs=in_specs,
            out_specs=[pl.BlockSpec((blk, ow), q_map), pl.BlockSpec((blk, ow), q_map)],
            scratch_shapes=[pltpu.VMEM((2 * pps, blk, LANES), F32), pltpu.VMEM((2 * pps, blk, LANES), F32),
                            pltpu.VMEM((pps, blk, LANES), F32), pltpu.VMEM((2 * pps, blk, LANES), F32),
                            pltpu.VMEM((2 * pps, blk, blk), F32), pltpu.VMEM((2 * pps, blk, blk), BF16),
                            pltpu.VMEM((2 * pps, blk, LANES), F32)]),
        out_shape=[jax.ShapeDtypeStruct((s, n_pairs * LANES), F32), jax.ShapeDtypeStruct((s, n_pairs * LANES), F32)],
        compiler_params=_params(dimension_semantics=sem),
    )(*tables, *args)


def _attn_bwd_call(mode, q, k, v, extra, lse, o, do):
    s = q.shape[0]
    cfg = _attn_cfg(mode, s)
    blk, n_outer, pps, wide, scale = cfg["blk"], cfg["n_outer"], cfg["pps"], cfg["wide"], cfg["scale"]
    rc = ROW_CHUNK
    nq = s // blk
    swa, fox = mode == "swa", mode == "fox"
    qw = (2 * LANES if wide else LANES) * pps
    kw = 2 * LANES if wide else LANES
    ow = LANES * pps
    reps = blk // LANES

    assert not swa

    def body(*refs):
        jt_ref, it_ref = refs[:2]
        refs = refs[2:]
        q_ref, k_ref, v_ref, lse_ref, o_ref, do_ref = refs[:6]
        n_in = 6
        if fox:
            cum_ref, cumt_ref = refs[6:8]
            n_in = 8
        dq_ref, dk_ref, dv_ref = refs[n_in:n_in + 3]
        n_out = n_in + 3
        if fox:
            dck_ref, dcq_ref = refs[n_out:n_out + 2]
            n_out += 2
        dk_acc, dv_acc, s_all, dp_all, p_all, ds_all, e_all, d_all = refs[n_out:n_out + 8]
        if fox:
            dck_acc, rs_all = refs[n_out + 8:n_out + 10]
        p_id, t = pl.program_id(0), pl.program_id(1)
        j, ii = jt_ref[t], it_ref[t]
        i, first_i, last_i = ii, ii == j, ii == nq - 1
        lane = lax.broadcasted_iota(jnp.int32, (1, LANES), 1)
        msk = [lane < HALF, lane >= HALF]

        @pl.when(t == 0)
        def _():
            dq_ref[...] = jnp.zeros_like(dq_ref)
            if swa:
                dsink_ref[...] = jnp.zeros_like(dsink_ref)
            if fox:
                dcq_ref[...] = jnp.zeros_like(dcq_ref)

        @pl.when(first_i)
        def _():
            dk_acc[...] = jnp.zeros_like(dk_acc)
            dv_acc[...] = jnp.zeros_like(dv_acc)
            if fox:
                dck_acc[...] = jnp.zeros_like(dck_acc)

        def process(masked):
            rows = pl.ds(pl.multiple_of(i * blk, blk), blk)
            vb = v_ref[...].astype(BF16)
            dv_parts, dk_parts = [], []
            for pp in range(pps):
                psl = slice(pp * LANES, (pp + 1) * LANES)
                lse_blk, do_blk = lse_ref[:, psl], do_ref[:, psl]
                doo = do_blk * o_ref[:, psl]
                dq_pair = []
                for h in range(2):
                    hh = 2 * pp + h
                    s_ref, dp_ref, p_ref, ds_ref = s_all.at[hh], dp_all.at[hh], p_all.at[hh], ds_all.at[hh]
                    e_ref, d_ref = e_all.at[hh], d_all.at[hh]
                    if fox:
                        rs_ref = rs_all.at[hh]
                    if wide:
                        hsl = slice(h * LANES, (h + 1) * LANES)
                        qh = (q_ref[:, hsl] * scale).astype(BF16)
                        kh = k_ref[:, hsl].astype(BF16)
                    else:
                        qh = (jnp.where(msk[h], q_ref[:, psl], 0.0) * scale).astype(BF16)
                        kh = k_ref[...].astype(BF16)
                    s_ref[...] = lax.dot_general(qh, kh, (((1,), (1,)), ((), ())), preferred_element_type=F32)
                    do_h = jnp.where(msk[h], do_blk, 0.0).astype(BF16)
                    dp_ref[...] = lax.dot_general(do_h, vb, (((1,), (1,)), ((), ())), preferred_element_type=F32)
                    lse_h = _lane_col(lse_blk, HALF * h)
                    d_h = jnp.sum(jnp.where(msk[h], doo, 0.0), axis=1, keepdims=True)
                    e_ref[...] = jnp.broadcast_to(lse_h, (blk, LANES))
                    d_ref[...] = jnp.broadcast_to(d_h, (blk, LANES))
                    if fox:
                        head = 2 * p_id + h
                        e_ref[...] = e_ref[...] - jnp.broadcast_to(_lane_col(cum_ref[...], head), (blk, LANES))
                        ck = _row_of(cumt_ref[...], head)

                    def chunk(c, colsum):
                        r0 = c * rc
                        cr = pl.ds(r0, rc)
                        u = s_ref[cr, :]
                        if fox:
                            u = u - ck
                        p = jnp.exp(u - jnp.tile(e_ref[cr, :], (1, reps)))
                        if masked:
                            p = jnp.where(_valid_rows(mode, i, j, blk, r0, rc), p, 0.0)
                        ds = p * (dp_ref[cr, :] - jnp.tile(d_ref[cr, :], (1, reps)))
                        p_ref[cr, :] = p.astype(BF16)
                        ds_ref[cr, :] = ds.astype(BF16)
                        if fox:
                            colsum = colsum + jnp.sum(ds, axis=0, keepdims=True)
                            rs_ref[cr, :] = jnp.broadcast_to(jnp.sum(ds, axis=1, keepdims=True), (rc, LANES))
                        return colsum

                    colsum = _unrolled(blk // rc, chunk, jnp.zeros((1, blk), F32))
                    dv_parts.append(lax.dot_general(p_ref[...], do_h, (((0,), (0,)), ((), ())),
                                                    preferred_element_type=F32))
                    if fox:
                        dck_acc[h:h + 1, :] += -colsum
                        dcq_ref[rows, :] += jnp.where(msk[h], rs_ref[...], 0.0)
                    dq_h = jnp.dot(ds_ref[...], kh, preferred_element_type=F32) * scale
                    dk_h = lax.dot_general(ds_ref[...], qh, (((0,), (0,)), ((), ())), preferred_element_type=F32)
                    if wide:
                        dq_ref[rows, hsl] += dq_h
                        dk_acc[:, hsl] += dk_h
                    else:
                        dq_pair.append(jnp.where(msk[h], dq_h, 0.0))
                        dk_parts.append(dk_h)
                if not wide:
                    dq_ref[rows, psl] += dq_pair[0] + dq_pair[1]
            dv_acc[...] += functools.reduce(lambda a, b: a + b, dv_parts)
            if not wide:
                dk_acc[...] += functools.reduce(lambda a, b: a + b, dk_parts)

        pl.when(ii > j)(lambda: process(False))
        pl.when(ii == j)(lambda: process(True))

        @pl.when(last_i)
        def _():
            dk_ref[...] = dk_acc[...]
            dv_ref[...] = dv_acc[...]
            if fox:
                dck_ref[0] = dck_acc[...]

    tri = [(j, i) for j in range(nq) for i in range(j, nq)]
    tables = [jnp.asarray([t[0] for t in tri], jnp.int32), jnp.asarray([t[1] for t in tri], jnp.int32)]
    q_map = lambda p, t, jt, it: (it[t], p)
    kv_map = lambda p, t, jt, it: (jt[t], p)
    in_specs = [pl.BlockSpec((blk, qw), q_map), pl.BlockSpec((blk, kw), kv_map), pl.BlockSpec((blk, LANES), kv_map),
                pl.BlockSpec((blk, ow), q_map), pl.BlockSpec((blk, ow), q_map), pl.BlockSpec((blk, ow), q_map)]
    args = [q, k, v, lse, o, do]
    n_pairs = n_outer * pps
    out_specs = [pl.BlockSpec((s, qw), lambda p, t, jt, it: (0, p)), pl.BlockSpec((blk, kw), kv_map),
                 pl.BlockSpec((blk, LANES), kv_map)]
    out_shape = [jax.ShapeDtypeStruct((s, q.shape[1]), F32), jax.ShapeDtypeStruct((s, k.shape[1]), F32),
                 jax.ShapeDtypeStruct((s, v.shape[1]), F32)]
    nh = 2 * pps
    scratch = [pltpu.VMEM((blk, kw), F32), pltpu.VMEM((blk, LANES), F32), pltpu.VMEM((nh, blk, blk), F32),
               pltpu.VMEM((nh, blk, blk), F32), pltpu.VMEM((nh, blk, blk), BF16), pltpu.VMEM((nh, blk, blk), BF16),
               pltpu.VMEM((nh, blk, LANES), F32), pltpu.VMEM((nh, blk, LANES), F32)]
    if fox:
        cum, cumt = extra
        in_specs += [pl.BlockSpec((blk, LANES), lambda p, t, jt, it: (it[t], 0)),
                     pl.BlockSpec((8, blk), lambda p, t, jt, it: (0, jt[t]))]
        args += [cum, cumt]
        out_specs += [pl.BlockSpec((1, 8, blk), lambda p, t, jt, it: (p, 0, jt[t])),
                      pl.BlockSpec((s, LANES), lambda p, t, jt, it: (0, p))]
        out_shape += [jax.ShapeDtypeStruct((n_pairs, 8, s), F32), jax.ShapeDtypeStruct((s, n_pairs * LANES), F32)]
        scratch += [pltpu.VMEM((8, blk), F32), pltpu.VMEM((nh, blk, LANES), F32)]
    return pl.pallas_call(
        body, name=f"attn_bwd_{mode}",
        grid_spec=pltpu.PrefetchScalarGridSpec(num_scalar_prefetch=2, grid=(n_outer, len(tri)), in_specs=in_specs,
                                               out_specs=out_specs, scratch_shapes=scratch),
        out_shape=out_shape, compiler_params=_params(dimension_semantics=("parallel", "arbitrary")),
    )(*tables, *args)


def _swa_masks(i, blk):
    r = lax.broadcasted_iota(jnp.int32, (blk, blk), 0)
    c = lax.broadcasted_iota(jnp.int32, (blk, blk), 1)
    return (c > r) & (i > 0), c <= r


def _nt(a, b):
    return lax.dot_general(a, b, (((1,), (1,)), ((), ())), preferred_element_type=F32)


def _tn(a, b):
    return lax.dot_general(a, b, (((0,), (0,)), ((), ())), preferred_element_type=F32)


def _swa_bwd_call(q, k, v, sink, lse, o, do):
    s = q.shape[0]
    blk, pps, scale = WINDOW, 4, SWA_DIM ** -0.5
    nq = s // blk

    def body(q_ref, kp_ref, ko_ref, vp_ref, vo_ref, sink_ref, lse_ref, o_ref, do_ref,
             dq_ref, dk_ref, dv_ref, dsink_ref, ck_ref, cv_ref):
        i = pl.program_id(1)
        lane = lax.broadcasted_iota(jnp.int32, (1, LANES), 1)
        msk = [lane < HALF, lane >= HALF]

        @pl.when(i == 0)
        def _():
            ck_ref[...] = jnp.zeros_like(ck_ref)
            cv_ref[...] = jnp.zeros_like(cv_ref)
            dsink_ref[...] = jnp.zeros_like(dsink_ref)

        @pl.when(i < nq)
        def _():
            ok_prev, ok_own = _swa_masks(i, blk)
            kp, ko = kp_ref[...].astype(BF16), ko_ref[...].astype(BF16)
            vp, vo = vp_ref[...].astype(BF16), vo_ref[...].astype(BF16)
            dkp, dko, dvp, dvo = [], [], [], []
            for pp in range(pps):
                psl = slice(pp * LANES, (pp + 1) * LANES)
                qp, do_blk = q_ref[:, psl], do_ref[:, psl]
                doo = do_blk * o_ref[:, psl]
                dqs = []
                for h in range(2):
                    hh = 2 * pp + h
                    qh = (jnp.where(msk[h], qp, 0.0) * scale).astype(BF16)
                    lse_h = jnp.broadcast_to(_lane_col(lse_ref[:, psl], HALF * h), (blk, LANES))
                    d_h = jnp.broadcast_to(jnp.sum(jnp.where(msk[h], doo, 0.0), axis=1, keepdims=True), (blk, LANES))
                    p_p = jnp.where(ok_prev, jnp.exp(_nt(qh, kp) - lse_h), 0.0)
                    p_o = jnp.where(ok_own, jnp.exp(_nt(qh, ko) - lse_h), 0.0)
                    do_h = jnp.where(msk[h], do_blk, 0.0).astype(BF16)
                    ds_p = (p_p * (_nt(do_h, vp) - d_h)).astype(BF16)
                    ds_o = (p_o * (_nt(do_h, vo) - d_h)).astype(BF16)
                    dq_h = (jnp.dot(ds_p, kp, preferred_element_type=F32)
                            + jnp.dot(ds_o, ko, preferred_element_type=F32)) * scale
                    dqs.append(jnp.where(msk[h], dq_h, 0.0))
                    dkp.append(_tn(ds_p, qh))
                    dko.append(_tn(ds_o, qh))
                    dvp.append(_tn(p_p.astype(BF16), do_h))
                    dvo.append(_tn(p_o.astype(BF16), do_h))
                    sink_row = sink_ref[hh:hh + 1, :]
                    dsink_ref[hh:hh + 1, :] += -jnp.sum(jnp.exp(sink_row - lse_h) * d_h, axis=0, keepdims=True)
                dq_ref[:, psl] = dqs[0] + dqs[1]
            total = lambda parts: functools.reduce(lambda a, b: a + b, parts)
            dk_ref[...] = ck_ref[...] + total(dkp)
            dv_ref[...] = cv_ref[...] + total(dvp)
            ck_ref[...] = total(dko)
            cv_ref[...] = total(dvo)

        @pl.when(i == nq)
        def _():
            dk_ref[...] = ck_ref[...]
            dv_ref[...] = cv_ref[...]

    last = nq - 1
    prev = lambda g, i: (jnp.maximum(i - 1, 0), g)
    own = lambda g, i: (jnp.minimum(i, last), g)
    qspec = pl.BlockSpec((blk, pps * LANES), own)
    kspec = lambda m: pl.BlockSpec((blk, LANES), m)
    sspec = pl.BlockSpec((8, LANES), lambda g, i: (g, 0))
    return pl.pallas_call(
        body, name="swa_bwd", grid=(SWA_KV_HEADS, nq + 1),
        in_specs=[qspec, kspec(prev), kspec(own), kspec(prev), kspec(own), sspec, qspec, qspec, qspec],
        out_specs=[qspec, kspec(prev), kspec(prev), sspec],
        out_shape=[jax.ShapeDtypeStruct(q.shape, F32), jax.ShapeDtypeStruct(k.shape, F32),
                   jax.ShapeDtypeStruct(v.shape, F32), jax.ShapeDtypeStruct((SWA_HEADS, LANES), F32)],
        scratch_shapes=[pltpu.VMEM((blk, LANES), F32), pltpu.VMEM((blk, LANES), F32)],
        compiler_params=_params(dimension_semantics=("parallel", "arbitrary")),
    )(q, k, k, v, v, sink, lse, o, do)


def _make_attn(mode):
    swa = mode == "swa"

    @jax.custom_vjp
    def attn(q, k, v, extra):
        return fwd(q, k, v, extra)[0]

    def fwd(q, k, v, extra):
        o, lse = _attn_fwd_call(mode, q, k, v, extra)
        return o, (q, k, v, extra, o, lse)

    def bwd(res, do):
        q, k, v, extra, o, lse = res
        outs = (_swa_bwd_call(q, k, v, extra, lse, o, do) if swa
                else _attn_bwd_call(mode, q, k, v, extra, lse, o, do))
        dq, dk, dv = outs[:3]
        if mode == "fox":
            cum, cumt = extra
            dck = outs[3]
            dcumt = dck[:, :2, :].reshape(FOX_HEADS, -1)
            dcq = outs[4].reshape(-1, FOX_HEADS, HALF)[:, :, 0]
            dextra = (jnp.pad(dcq, ((0, 0), (0, LANES - FOX_HEADS))), dcumt)
        elif mode == "swa":
            dextra = jnp.where(jnp.arange(LANES)[None, :] == 0, outs[3], 0.0)
        else:
            dextra = None
        return dq, dk, dv, dextra

    attn.defvjp(fwd, bwd)
    return attn


attn_mla = _make_attn("mla")
attn_fox = _make_attn("fox")
attn_swa = _make_attn("swa")


def _ukv_layout(w):
    r = w.shape[0]
    w3 = w.reshape(r, MLA_HEADS, MLA_NOPE + MLA_V)
    wk = jnp.pad(w3[:, :, :MLA_NOPE], ((0, 0), (0, 0), (0, LANES - MLA_NOPE))).reshape(r, MLA_HEADS * LANES)
    wv = w3[:, :, MLA_NOPE:].reshape(r, MLA_HEADS * MLA_V)
    return wk, wv


def _even_layer(x, w_in_cat, q_norm, w_uq_p, kv_norm, w_ukv, b_f, w_out, ln_g, ln_b, tabs_mla):
    cq, ckv, kpe, fq, fk, fv, fl, gate = even_in_proj(x, relayout(w_in_cat, "even"))
    tabs, hs = tabs_mla
    q = rope(mm(rms_norm(cq, q_norm), w_uq_p), tabs, hs)
    ckvn = rms_norm(ckv, kv_norm)
    wk, wv = _ukv_layout(w_ukv)
    kk = mm(ckvn, wk) + jnp.tile(rope(kpe, tabs, hs), (1, MLA_HEADS))
    o_mla = attn_mla(q, kk, mm(ckvn, wv), None)
    cum = fox_cum(fl, jnp.pad(b_f, (0, LANES - FOX_HEADS)).reshape(1, LANES))
    o_fox = attn_fox(fq, fk, fv, (cum, cum[:, :8].T))
    y = mm(gate_mul((o_mla, o_fox), gate), w_out)
    return ln_res(x, y, ln_g, ln_b)


def _odd_layer(x, w_in_cat, sinks, w_out, ln_g, ln_b, tabs_swa):
    q, kd, vd, gate = odd_in_proj(x, relayout(w_in_cat, "odd"))
    tabs, hs = tabs_swa
    q = rope(q, tabs, hs)
    kd = rope(kd, tabs, hs)
    o = attn_swa(q, kd, vd, jnp.broadcast_to(sinks[:, None], (SWA_HEADS, LANES)))
    y = mm(gate_mul((o,), gate), w_out)
    return ln_res(x, y, ln_g, ln_b)


EVEN_SHARDED = ["even_w_in", "even_w_uq", "even_w_ukv", "even_w_out"]
ODD_SHARDED = ["odd_w_in", "odd_w_out", "odd_ln_g", "odd_ln_b"]
EVEN_REPL = ["even_q_norm", "even_kv_norm", "even_b_f", "even_ln_g", "even_ln_b"]
ODD_REPL = ["odd_sinks"]


def _layer_names(layer):
    return (EVEN_SHARDED, EVEN_REPL) if layer % 2 == 0 else (ODD_SHARDED, ODD_REPL)


def _layer_of(name, j):
    return 2 * j if name.startswith("even") else 2 * j + 1


def _layer_apply(layer, p, x, tabs):
    if layer % 2 == 0:
        return _even_layer(x, p["even_w_in"], p["even_q_norm"], p["even_w_uq"], p["even_kv_norm"], p["even_w_ukv"],
                           p["even_b_f"], p["even_w_out"], p["even_ln_g"], p["even_ln_b"], tabs["mla"])
    return _odd_layer(x, p["odd_w_in"], p["odd_sinks"], p["odd_w_out"], p["odd_ln_g"], p["odd_ln_b"], tabs["swa"])


def _pad_rows(flat, mult):
    n = flat.shape[-1]
    per = mult * LANES
    padded = -(-n // per) * per
    if padded != n:
        flat = jnp.pad(flat, [(0, 0)] * (flat.ndim - 1) + [(0, padded - n)])
    return flat.reshape(flat.shape[:-1] + (padded // LANES, LANES))


def _pad_last(a, width):
    if a.shape[-1] == width:
        return a
    return jnp.pad(a, [(0, 0)] * (a.ndim - 1) + [(0, width - a.shape[-1])])


def _join(slots, axis):
    shp = list(slots.shape[1:])
    shp[axis] *= N_DEV
    return jnp.moveaxis(slots, 0, axis).reshape(shp)


def _split(full, axis):
    shp = full.shape
    t = full.reshape(shp[:axis] + (N_DEV, shp[axis] // N_DEV) + shp[axis + 1:])
    return jnp.moveaxis(t, axis, 0)


PAD_TO = {"even_w_in": SHARD_PAD, "even_w_uq": LANES, "odd_w_in": SHARD_PAD}


def kernel(x, even_w_in, even_q_norm, even_w_uq, even_kv_norm, even_w_ukv, even_b_f, even_w_out, even_ln_g, even_ln_b, odd_w_in, odd_sinks, odd_w_out, odd_ln_g, odd_ln_b, loss_target, m_even_w_in, m_even_q_norm, m_even_w_uq, m_even_kv_norm, m_even_w_ukv, m_even_b_f, m_even_w_out, m_even_ln_g, m_even_ln_b, m_odd_w_in, m_odd_sinks, m_odd_w_out, m_odd_ln_g, m_odd_ln_b, v_even_w_in, v_even_q_norm, v_even_w_uq, v_even_kv_norm, v_even_w_ukv, v_even_b_f, v_even_w_out, v_even_ln_g, v_even_ln_b, v_odd_w_in, v_odd_sinks, v_odd_w_out, v_odd_ln_g, v_odd_ln_b):
    w = dict(even_w_in=even_w_in, even_q_norm=even_q_norm, even_w_uq=even_w_uq, even_kv_norm=even_kv_norm,
             even_w_ukv=even_w_ukv, even_b_f=even_b_f, even_w_out=even_w_out, even_ln_g=even_ln_g, even_ln_b=even_ln_b,
             odd_w_in=odd_w_in, odd_sinks=odd_sinks, odd_w_out=odd_w_out, odd_ln_g=odd_ln_g, odd_ln_b=odd_ln_b)
    mom = dict(even_w_in=m_even_w_in, even_q_norm=m_even_q_norm, even_w_uq=m_even_w_uq, even_kv_norm=m_even_kv_norm,
               even_w_ukv=m_even_w_ukv, even_b_f=m_even_b_f, even_w_out=m_even_w_out, even_ln_g=m_even_ln_g,
               even_ln_b=m_even_ln_b, odd_w_in=m_odd_w_in, odd_sinks=m_odd_sinks, odd_w_out=m_odd_w_out,
               odd_ln_g=m_odd_ln_g, odd_ln_b=m_odd_ln_b)
    vel = dict(even_w_in=v_even_w_in, even_q_norm=v_even_q_norm, even_w_uq=v_even_w_uq, even_kv_norm=v_even_kv_norm,
               even_w_ukv=v_even_w_ukv, even_b_f=v_even_b_f, even_w_out=v_even_w_out, even_ln_g=v_even_ln_g,
               even_ln_b=v_even_ln_b, odd_w_in=v_odd_w_in, odd_sinks=v_odd_sinks, odd_w_out=v_odd_w_out,
               odd_ln_g=v_odd_ln_g, odd_ln_b=v_odd_ln_b)
    sharded = BIG + SMALL_SHARDED
    padded = lambda d, n: _pad_last(d[n], PAD_TO.get(n, d[n].shape[-1]))

    tabs = {"mla": _rope_tables(x.shape[1], "mla"), "swa": _rope_tables(x.shape[1], "swa")}
    keys = lambda layers: [(n, layer // 2) for layer in layers for n in _layer_names(layer)[0]]
    first, rest = keys([0]), keys([1, 2, 3])
    me = _lin(_me())

    def shard(n, j):
        a = padded(w, n)[j]
        return a.astype(BF16) if n in BIG else a

    to_full = lambda n, g: _join(g, SHARD_AXIS[n] - 1).astype(F32)
    to_slots = lambda n, g: _split(g, SHARD_AXIS[n] - 1).astype(BF16 if n in BIG else F32)

    def layer_params(layer, full_of):
        shn, rpn = _layer_names(layer)
        p = {n: full_of(n) for n in shn}
        p.update({n: w[n][layer // 2] for n in rpn})
        return p

    got0 = _all_gather([shard(n, j) for n, j in first], "all_gather_first")
    got0, mine_rest = lax.optimization_barrier((got0, [shard(n, j) for n, j in rest]))
    got0 = dict(zip(first, got0))
    send_sems, recv_sems, srcs, lands, token = _split_start(mine_rest, False, "all_gather_rest_start")
    p0 = layer_params(0, lambda n: to_full(n, got0[(n, 0)]) + token[0, 0])
    x1, vjp0 = jax.vjp(lambda p, xx: _layer_apply(0, p, xx, tabs), p0, x[0])
    got = _split_wait(send_sems, recv_sems, srcs, lands, x1, False, "all_gather_rest_wait")
    got = dict(zip(rest, _own_slot(got, [m[None] for m in mine_rest])))

    xs, vjps = x1, [vjp0]
    for layer in (1, 2, 3):
        p = layer_params(layer, lambda n: to_full(n, got[(n, layer // 2)]))
        xs, vjp = jax.vjp(lambda p_, xx, layer=layer: _layer_apply(layer, p_, xx, tabs), p, xs)
        vjps.append(vjp)
    loss_local, vjp_loss = jax.vjp(lambda y: mse_loss(y, loss_target[0]), xs)
    (dy,) = vjp_loss(jnp.ones((), F32))
    loss = lax.psum(loss_local, AXES)
    grads = {}
    for layer in (3, 2, 1):
        grads[layer], dy = vjps[layer](dy)

    parts_rest = [to_slots(n, grads[_layer_of(n, j)][n]) for n, j in rest]
    send_sems, recv_sems, srcs, lands, token = _split_start(parts_rest, True, "grad_exchange_rest_start")
    grads[0], grad_x = vjps[0](dy + token[0, 0])
    recv_rest = _split_wait(send_sems, recv_sems, srcs, lands, grad_x, True, "grad_exchange_rest_wait")
    recv = dict(zip(rest, _own_slot(recv_rest, [lax.dynamic_slice_in_dim(p, me, 1, axis=0) for p in parts_rest])))
    repl_grad = lambda n: jnp.stack([grads[_layer_of(n, j)][n] for j in (0, 1)])
    repl_rows = _pad_rows(jnp.concatenate([repl_grad(n).reshape(-1) for n in REPL]), 8)
    parts_last = [to_slots(n, grads[0][n]) for n, j in first]
    parts_last.append(jnp.broadcast_to(repl_rows[None], (N_DEV,) + repl_rows.shape))
    recv_last = _exchange(parts_last, "grad_exchange_last")
    recv.update(zip(first, recv_last[:-1]))

    g_out, d_out, m_out, v_out = {}, {}, {}, {}
    for n in sharded:
        r = jnp.stack([recv[(n, 0)], recv[(n, 1)]], axis=1)
        cols = r.shape[-1]
        flat2 = lambda d: padded(d, n).reshape(-1, cols)
        outs = _sum_adamw(r.reshape(N_DEV, -1, cols), flat2(w), flat2(mom), flat2(vel))
        for dst, o in zip((g_out, d_out, m_out, v_out), outs):
            dst[n] = o.reshape(w[n].shape[:-1] + (cols,))[..., :w[n].shape[-1]]
    pack = lambda d: _pad_rows(jnp.concatenate([d[n].reshape(-1) for n in REPL]), 8)
    outs = _sum_adamw(recv_last[-1], pack(w), pack(mom), pack(vel))
    for dst, o in zip((g_out, d_out, m_out, v_out), outs):
        flat, off = o.reshape(-1), 0
        for n in REPL:
            size = math.prod(w[n].shape)
            dst[n] = flat[off:off + size].reshape(w[n].shape)
            off += size
    return (loss, grad_x[None], *[g_out[n] for n in WEIGHTS], *[d_out[n] for n in WEIGHTS],
            *[m_out[n] for n in WEIGHTS], *[v_out[n] for n in WEIGHTS])
```

```python
import functools
import math

import jax
import jax.numpy as jnp
from jax import lax
from jax.experimental import pallas as pl
from jax.experimental.pallas import tpu as pltpu

F32 = jnp.float32
BF16 = jnp.bfloat16
LANES = 128
HALF = 64
N_DEV = 8
AXES = ("x", "y", "c")
VMEM_LIMIT = 48 * 1024 * 1024

D_MODEL = 1024
DEPTH = 4
ROPE_THETA = 10000.0
MLA_HEADS, MLA_NOPE, MLA_ROPE, MLA_V, MLA_Q_RANK, MLA_KV_RANK = 8, 64, 32, 64, 256, 128
FOX_HEADS, FOX_DIM = 8, 64
SWA_HEADS, SWA_KV_HEADS, SWA_DIM, WINDOW = 16, 2, 64, 128
RMS_EPS, LN_EPS = 1e-6, 1e-5
ALPHA = (2 * DEPTH) ** 0.25
ADAM_LR, ADAM_B1, ADAM_B2, ADAM_EPS, ADAM_WD, ADAM_STEP = 0.001, 0.9, 0.999, 1e-08, 0.01, 10
NEG = -1e30

WEIGHTS = ["even_w_in", "even_q_norm", "even_w_uq", "even_kv_norm", "even_w_ukv", "even_b_f", "even_w_out",
           "even_ln_g", "even_ln_b", "odd_w_in", "odd_sinks", "odd_w_out", "odd_ln_g", "odd_ln_b"]
SHARD_AXIS = {"even_w_in": 2, "even_w_uq": 2, "even_w_ukv": 2, "even_w_out": 1, "odd_w_in": 2, "odd_w_out": 1,
              "odd_ln_g": 1, "odd_ln_b": 1, "even_q_norm": None, "even_kv_norm": None, "even_b_f": None,
              "even_ln_g": None, "even_ln_b": None, "odd_sinks": None}
BIG = ["even_w_in", "even_w_uq", "even_w_ukv", "even_w_out", "odd_w_in", "odd_w_out"]
SMALL_SHARDED = ["odd_ln_g", "odd_ln_b"]
REPL = [n for n in WEIGHTS if SHARD_AXIS[n] is None]


def _pick(n, cands):
    for c in cands:
        if n % c == 0:
            return c
    return n


def _params(**kw):
    return pltpu.CompilerParams(vmem_limit_bytes=VMEM_LIMIT, **kw)


def _me():
    return lax.axis_index("x"), lax.axis_index("y"), lax.axis_index("c")


def _peer(k):
    x, y, c = _me()
    px = 1 - x if (k >> 2) & 1 else x
    py = 1 - y if (k >> 1) & 1 else y
    pc = 1 - c if k & 1 else c
    return px, py, pc


def _lin(p):
    return 4 * p[0] + 2 * p[1] + p[2]


def _comm_call(body, n, out_shape, args, name):
    any_spec = pl.BlockSpec(memory_space=pl.ANY)
    return pl.pallas_call(
        body, name=name, out_shape=out_shape, in_specs=[any_spec] * n, out_specs=[any_spec] * n,
        scratch_shapes=[pltpu.SemaphoreType.DMA((n, N_DEV - 1)), pltpu.SemaphoreType.DMA((n, N_DEV - 1)),
                        pltpu.SemaphoreType.DMA((n,))],
    )(*args)


def _all_gather(xs, name):
    n = len(xs)

    def body(*refs):
        x_refs, out_refs = refs[:n], refs[n:2 * n]
        send_sems, recv_sems, local_sems = refs[2 * n:]
        me = _lin(_me())
        local = [pltpu.make_async_copy(x_refs[a], out_refs[a].at[me], local_sems.at[a]) for a in range(n)]
        for cp in local:
            cp.start()
        sends = []
        for k in range(1, N_DEV):
            for a in range(n):
                cp = pltpu.make_async_remote_copy(
                    src_ref=x_refs[a], dst_ref=out_refs[a].at[me], send_sem=send_sems.at[a, k - 1],
                    recv_sem=recv_sems.at[a, k - 1], device_id=_peer(k), device_id_type=pl.DeviceIdType.MESH)
                cp.start()
                sends.append(cp)
        for k in range(1, N_DEV):
            for a in range(n):
                pltpu.make_async_remote_copy(
                    src_ref=x_refs[a], dst_ref=out_refs[a].at[_lin(_peer(k))], send_sem=send_sems.at[a, k - 1],
                    recv_sem=recv_sems.at[a, k - 1], device_id=_peer(k),
                    device_id_type=pl.DeviceIdType.MESH).wait_recv()
        for cp in sends:
            cp.wait_send()
        for cp in local:
            cp.wait()

    out_shape = [jax.ShapeDtypeStruct((N_DEV,) + x.shape, x.dtype) for x in xs]
    return _comm_call(body, n, out_shape, xs, name)


def _exchange(parts, name):
    n = len(parts)

    def body(*refs):
        p_refs, out_refs = refs[:n], refs[n:2 * n]
        send_sems, recv_sems, local_sems = refs[2 * n:]
        me = _lin(_me())
        local = [pltpu.make_async_copy(p_refs[a].at[me], out_refs[a].at[me], local_sems.at[a]) for a in range(n)]
        for cp in local:
            cp.start()
        sends = []
        for k in range(1, N_DEV):
            peer = _peer(k)
            for a in range(n):
                cp = pltpu.make_async_remote_copy(
                    src_ref=p_refs[a].at[_lin(peer)], dst_ref=out_refs[a].at[me], send_sem=send_sems.at[a, k - 1],
                    recv_sem=recv_sems.at[a, k - 1], device_id=peer, device_id_type=pl.DeviceIdType.MESH)
                cp.start()
                sends.append(cp)
        for k in range(1, N_DEV):
            peer = _peer(k)
            for a in range(n):
                pltpu.make_async_remote_copy(
                    src_ref=p_refs[a].at[_lin(peer)], dst_ref=out_refs[a].at[_lin(peer)],
                    send_sem=send_sems.at[a, k - 1], recv_sem=recv_sems.at[a, k - 1], device_id=peer,
                    device_id_type=pl.DeviceIdType.MESH).wait_recv()
        for cp in sends:
            cp.wait_send()
        for cp in local:
            cp.wait()

    out_shape = [jax.ShapeDtypeStruct(p.shape, p.dtype) for p in parts]
    return _comm_call(body, n, out_shape, parts, name)


_HBM = pl.BlockSpec(memory_space=pltpu.HBM)
_SEM = pl.BlockSpec(memory_space=pltpu.SEMAPHORE)
_EFFECT = pltpu.SideEffectType.DATAFLOW_SIDE_EFFECTING


def _split_start(srcs, slotted, name):
    n = len(srcs)
    lands = [lax.empty(s.shape if slotted else (N_DEV,) + s.shape, s.dtype) for s in srcs]

    def body(*refs):
        src_refs, land_refs = refs[:n], refs[n:2 * n]
        send_sems, recv_sems, token = refs[2 * n], refs[2 * n + 1], refs[-1]
        me = _lin(_me())
        for k in range(1, N_DEV):
            peer = _peer(k)
            for a in range(n):
                pltpu.make_async_remote_copy(
                    src_ref=src_refs[a].at[_lin(peer)] if slotted else src_refs[a], dst_ref=land_refs[a].at[me],
                    send_sem=send_sems.at[a * (N_DEV - 1) + k - 1], recv_sem=recv_sems.at[a * (N_DEV - 1) + k - 1],
                    device_id=peer, device_id_type=pl.DeviceIdType.MESH).start()
        token[...] = jnp.zeros_like(token)

    both = list(srcs) + lands
    outs = pl.pallas_call(
        body, name=name,
        out_shape=(pltpu.SemaphoreType.DMA((n * (N_DEV - 1),)), pltpu.SemaphoreType.DMA((n * (N_DEV - 1),)),
                   *[pltpu.HBM(b.shape, b.dtype) for b in both], jax.ShapeDtypeStruct((8, LANES), F32)),
        in_specs=[_HBM] * (2 * n), out_specs=(_SEM, _SEM, *[_HBM] * (2 * n), pl.BlockSpec(memory_space=pltpu.VMEM)),
        input_output_aliases={a: 2 + a for a in range(2 * n)},
        compiler_params=pltpu.CompilerParams(has_side_effects=_EFFECT),
    )(*[pltpu.with_memory_space_constraint(b, pltpu.HBM) for b in both])
    return outs[0], outs[1], list(outs[2:2 + n]), list(outs[2 + n:2 + 2 * n]), outs[-1]


def _split_wait(send_sems, recv_sems, srcs, lands, after, slotted, name):
    n = len(srcs)

    def body(*refs):
        src_refs, land_refs = refs[:n], refs[n:2 * n]
        send_sems, recv_sems = refs[2 * n], refs[2 * n + 1]
        for k in range(1, N_DEV):
            peer = _peer(k)
            for a in range(n):
                cp = pltpu.make_async_remote_copy(
                    src_ref=src_refs[a].at[_lin(peer)] if slotted else src_refs[a],
                    dst_ref=land_refs[a].at[_lin(peer)], send_sem=send_sems.at[a * (N_DEV - 1) + k - 1],
                    recv_sem=recv_sems.at[a * (N_DEV - 1) + k - 1], device_id=peer,
                    device_id_type=pl.DeviceIdType.MESH)
                cp.wait_send()
                cp.wait_recv()

    both = list(srcs) + list(lands)
    outs = pl.pallas_call(
        body, name=name, out_shape=[pltpu.HBM(b.shape, b.dtype) for b in both],
        in_specs=[_HBM] * (2 * n) + [_SEM, _SEM, pl.BlockSpec(memory_space=pl.ANY)], out_specs=[_HBM] * (2 * n),
        input_output_aliases={a: a for a in range(2 * n)},
        compiler_params=pltpu.CompilerParams(has_side_effects=_EFFECT),
    )(*both, send_sems, recv_sems, after)
    return list(outs[n:])


def _own_slot(lands, own):
    me = _lin(_me())
    slot = lambda l: lax.broadcasted_iota(jnp.int32, (N_DEV,) + (1,) * (l.ndim - 1), 0)
    return [jnp.where(slot(l) == me, o.astype(l.dtype), l) for l, o in zip(lands, own)]


def _sum_adamw(recv, w, m, v):
    _, rows, lanes = recv.shape
    tr = _pick(rows, (256, 128, 64, 32, 16, 8))
    c1 = 1.0 - ADAM_B1 ** ADAM_STEP
    c2 = 1.0 - ADAM_B2 ** ADAM_STEP

    def body(r_ref, w_ref, m_ref, v_ref, g_out, d_out, m_out, v_out):
        g = r_ref[0].astype(F32)
        for s in range(1, N_DEV):
            g = g + r_ref[s].astype(F32)
        mn = ADAM_B1 * m_ref[...] + (1.0 - ADAM_B1) * g
        vn = ADAM_B2 * v_ref[...] + (1.0 - ADAM_B2) * (g * g)
        m_hat = mn / c1
        v_hat = vn / c2
        g_out[...] = g
        d_out[...] = -ADAM_LR * (m_hat / (jnp.sqrt(v_hat) + ADAM_EPS) + ADAM_WD * w_ref[...])
        m_out[...] = mn
        v_out[...] = vn

    blk = pl.BlockSpec((tr, lanes), lambda i: (i, 0))
    shp = jax.ShapeDtypeStruct((rows, lanes), F32)
    return pl.pallas_call(
        body, name=f"sum_adamw_{rows}x{lanes}", grid=(rows // tr,),
        in_specs=[pl.BlockSpec((N_DEV, tr, lanes), lambda i: (0, i, 0)), blk, blk, blk],
        out_specs=[blk, blk, blk, blk], out_shape=[shp, shp, shp, shp],
        compiler_params=_params(dimension_semantics=("parallel",)),
    )(recv, w, m, v)


def _rope_block(xb, cv, s1v, s2v, hs):
    return xb * cv + pltpu.roll(xb, LANES - hs, 1) * s1v + pltpu.roll(xb, hs, 1) * s2v


def _mm_nn(a, b, b_transposed=False, rope=None):
    m, k = a.shape
    n = b.shape[0] if b_transposed else b.shape[1]
    tm = _pick(m, (1024, 512, 256, 128))
    tn = _pick(n, (1024, 640, 512, 256, 128))
    tk = _pick(k, (1024, 640, 512, 256, 128))
    nk = k // tk

    def body(*refs):
        a_ref, b_ref = refs[:2]
        o_ref, acc_ref = refs[-2:]
        kk = pl.program_id(2)

        @pl.when(kk == 0)
        def _():
            acc_ref[...] = jnp.zeros_like(acc_ref)

        dims = (((1,), (1,)), ((), ())) if b_transposed else (((1,), (0,)), ((), ()))
        acc_ref[...] += lax.dot_general(a_ref[...].astype(BF16), b_ref[...].astype(BF16), dims,
                                        preferred_element_type=F32)

        @pl.when(kk == nk - 1)
        def _():
            if rope is None:
                o_ref[...] = acc_ref[...]
            else:
                cv, s1v, s2v = refs[2][...], refs[3][...], refs[4][...]
                for cb in range(tn // LANES):
                    sl = slice(cb * LANES, (cb + 1) * LANES)
                    o_ref[:, sl] = _rope_block(acc_ref[:, sl], cv, s1v, s2v, rope[3])

    b_spec = (pl.BlockSpec((tn, tk), lambda i, j, kk: (j, kk)) if b_transposed
              else pl.BlockSpec((tk, tn), lambda i, j, kk: (kk, j)))
    in_specs, args = [pl.BlockSpec((tm, tk), lambda i, j, kk: (i, kk)), b_spec], [a, b]
    if rope is not None:
        in_specs += [pl.BlockSpec((tm, LANES), lambda i, j, kk: (i, 0))] * 3
        args += list(rope[:3])
    tag = ("t" if b_transposed else "n") + ("" if rope is None else f"_rope{rope[3]}")
    return pl.pallas_call(
        body, name=f"mm_n{tag}_{m}x{k}x{n}", grid=(m // tm, n // tn, nk), in_specs=in_specs,
        out_specs=pl.BlockSpec((tm, tn), lambda i, j, kk: (i, j)),
        out_shape=jax.ShapeDtypeStruct((m, n), F32),
        scratch_shapes=[pltpu.VMEM((tm, tn), F32)],
        compiler_params=_params(dimension_semantics=("parallel", "parallel", "arbitrary")),
    )(*args)


def _mm_tn(a, g):
    s, k = a.shape
    _, n = g.shape
    tm = _pick(k, (1024, 512, 256, 128))
    tn = _pick(n, (1024, 640, 512, 256, 128))
    ts = _pick(s, (512, 256, 128))
    ns = s // ts

    def body(a_ref, g_ref, o_ref, acc_ref):
        ss = pl.program_id(2)

        @pl.when(ss == 0)
        def _():
            acc_ref[...] = jnp.zeros_like(acc_ref)

        acc_ref[...] += lax.dot_general(a_ref[...].astype(BF16), g_ref[...].astype(BF16),
                                        (((0,), (0,)), ((), ())), preferred_element_type=F32)

        @pl.when(ss == ns - 1)
        def _():
            o_ref[...] = acc_ref[...]

    return pl.pallas_call(
        body, name=f"mm_tn_{s}x{k}x{n}", grid=(k // tm, n // tn, ns),
        in_specs=[pl.BlockSpec((ts, tm), lambda i, j, ss: (ss, i)), pl.BlockSpec((ts, tn), lambda i, j, ss: (ss, j))],
        out_specs=pl.BlockSpec((tm, tn), lambda i, j, ss: (i, j)),
        out_shape=jax.ShapeDtypeStruct((k, n), F32),
        scratch_shapes=[pltpu.VMEM((tm, tn), F32)],
        compiler_params=_params(dimension_semantics=("parallel", "parallel", "arbitrary")),
    )(a, g)


@jax.custom_vjp
def mm(a, w):
    return _mm_nn(a, w.astype(BF16))


def _mm_fwd(a, w):
    wb = w.astype(BF16)
    return _mm_nn(a, wb), (a, wb)


def _mm_bwd(res, g):
    a, wb = res
    return _mm_nn(g, wb, b_transposed=True), _mm_tn(a, g)


mm.defvjp(_mm_fwd, _mm_bwd)


def _unrope(g, tabs, hs):
    return _rope_call(g, tabs[0], -tabs[1], -tabs[2], hs)


@functools.partial(jax.custom_vjp, nondiff_argnums=(3,))
def mm_rope(a, w, tabs, hs):
    return _mm_nn(a, w.astype(BF16), rope=(*tabs, hs))


def _mm_rope_fwd(a, w, tabs, hs):
    wb = w.astype(BF16)
    return _mm_nn(a, wb, rope=(*tabs, hs)), (a, wb, tabs)


def _mm_rope_bwd(hs, res, g):
    a, wb, tabs = res
    g = _unrope(g, tabs, hs)
    return _mm_nn(g, wb, b_transposed=True), _mm_tn(a, g), jax.tree.map(jnp.zeros_like, tabs)


mm_rope.defvjp(_mm_rope_fwd, _mm_rope_bwd)


def _make_in_proj(widths, roped, hs):
    cuts = [sum(widths[:i]) for i in range(len(widths) + 1)]

    @jax.custom_vjp
    def in_proj(x, w, tabs):
        return fwd(x, w, tabs)[0]

    def fwd(x, w, tabs):
        xb, wb = x.astype(BF16), w.astype(BF16)
        outs = tuple(_mm_nn(xb, wb[:, a:b], rope=(*tabs, hs) if gi in roped else None)
                     for gi, (a, b) in enumerate(zip(cuts[:-1], cuts[1:])))
        return outs, (xb, wb, tabs)

    def bwd(res, gs):
        xb, wb, tabs = res
        g = jnp.concatenate([_unrope(gg, tabs, hs) if gi in roped else gg for gi, gg in enumerate(gs)], axis=1)
        return _mm_nn(g, wb, b_transposed=True), _mm_tn(xb, g), jax.tree.map(jnp.zeros_like, tabs)

    in_proj.defvjp(fwd, bwd)
    return in_proj


EVEN_GROUPS = (256, 128, 128, 512, 512, 512, 128, 1024)
ODD_GROUPS = (1024, 256, 256, 1024)
even_in_proj = _make_in_proj(EVEN_GROUPS, roped=(2,), hs=MLA_ROPE // 2)
odd_in_proj = _make_in_proj(ODD_GROUPS, roped=(0, 1), hs=SWA_DIM // 2)

SHARD_PAD = 384


def _source_columns(kind):
    if kind == "even":
        src = [list(range(0, 384)), [-1] * 64, list(range(384, 416)), [-1] * 32, list(range(416, 1952)),
               list(range(1952, 1960)), [-1] * 120, list(range(1960, 2984))]
        return sum(src, []), 373
    q0, k0, v0, g0 = 0, 1024, 1152, 1280
    dup = lambda base: [base + 64 * g + c for g in range(SWA_KV_HEADS) for _ in range(2) for c in range(64)]
    return list(range(q0, k0)) + dup(k0) + dup(v0) + list(range(g0, 2304)), 288


def _selection(kind, transposed):
    src, shard = _source_columns(kind)
    cat = [s + (SHARD_PAD - shard) * (s // shard) if s >= 0 else -1 for s in src]
    cat_arr = jnp.asarray(cat, jnp.int32)
    if transposed:
        cols = lax.broadcasted_iota(jnp.int32, (len(src), N_DEV * SHARD_PAD), 1)
        return (cols == cat_arr[:, None]).astype(BF16), [(c, r) for c, r in enumerate(cat) if r >= 0]
    rows = lax.broadcasted_iota(jnp.int32, (N_DEV * SHARD_PAD, len(src)), 0)
    return (rows == cat_arr[None, :]).astype(BF16), [(r, c) for c, r in enumerate(cat) if r >= 0]


def _mm_banded(a, b, nonzeros):
    m, k = a.shape
    _, n = b.shape
    tm = _pick(m, (1024, 512, 256, 128))
    tn = _pick(n, (1024, 640, 512, 256, 128))
    tk = _pick(k, (1024, 640, 512, 256, 128))
    lo, hi = [k // tk] * (n // tn), [-1] * (n // tn)
    for r, c in nonzeros:
        lo[c // tn], hi[c // tn] = min(lo[c // tn], r // tk), max(hi[c // tn], r // tk)
    first = [l if h >= 0 else 0 for l, h in zip(lo, hi)]
    count = [h - l + 1 if h >= 0 else 0 for l, h in zip(lo, hi)]
    steps = max(count)

    def body(first_ref, count_ref, a_ref, b_ref, o_ref, acc_ref):
        j, kk = pl.program_id(1), pl.program_id(2)

        @pl.when(kk == 0)
        def _():
            acc_ref[...] = jnp.zeros_like(acc_ref)

        @pl.when(kk < count_ref[j])
        def _():
            acc_ref[...] += jnp.dot(a_ref[...].astype(BF16), b_ref[...].astype(BF16), preferred_element_type=F32)

        @pl.when(kk == steps - 1)
        def _():
            o_ref[...] = acc_ref[...]

    kblk = lambda j, kk, f, c: jnp.minimum(f[j] + kk, f[j] + jnp.maximum(c[j], 1) - 1)
    return pl.pallas_call(
        body, name=f"mm_banded_{m}x{k}x{n}",
        grid_spec=pltpu.PrefetchScalarGridSpec(
            num_scalar_prefetch=2, grid=(m // tm, n // tn, steps),
            in_specs=[pl.BlockSpec((tm, tk), lambda i, j, kk, f, c: (i, kblk(j, kk, f, c))),
                      pl.BlockSpec((tk, tn), lambda i, j, kk, f, c: (kblk(j, kk, f, c), j))],
            out_specs=pl.BlockSpec((tm, tn), lambda i, j, kk, f, c: (i, j)),
            scratch_shapes=[pltpu.VMEM((tm, tn), F32)]),
        out_shape=jax.ShapeDtypeStruct((m, n), F32),
        compiler_params=_params(dimension_semantics=("parallel", "parallel", "arbitrary")),
    )(jnp.asarray(first, jnp.int32), jnp.asarray(count, jnp.int32), a, b)


@functools.partial(jax.custom_vjp, nondiff_argnums=(1,))
def relayout(wcat, kind):
    return _mm_banded(wcat, *_selection(kind, False))


def _relayout_bwd(kind, _, g):
    return (_mm_banded(g, *_selection(kind, True)),)


relayout.defvjp(lambda wcat, kind: (_mm_banded(wcat, *_selection(kind, False)), None), _relayout_bwd)


def _row_block(s):
    return _pick(s, (512, 256, 128, 64, 32, 16, 8))


def _rms_fwd_call(x, g):
    s, k = x.shape
    tr = _row_block(s)

    def body(x_ref, g_ref, o_ref):
        xv = x_ref[...]
        r = lax.rsqrt(jnp.mean(xv * xv, axis=-1, keepdims=True) + RMS_EPS)
        o_ref[...] = xv * r * g_ref[...]

    return pl.pallas_call(
        body, name=f"rms_fwd_{k}", grid=(s // tr,),
        in_specs=[pl.BlockSpec((tr, k), lambda i: (i, 0)), pl.BlockSpec((1, k), lambda i: (0, 0))],
        out_specs=pl.BlockSpec((tr, k), lambda i: (i, 0)), out_shape=jax.ShapeDtypeStruct((s, k), F32),
        compiler_params=_params(dimension_semantics=("parallel",)),
    )(x, g.reshape(1, k))


def _rms_bwd_call(x, g, dy):
    s, k = x.shape
    tr = _row_block(s)

    def body(x_ref, g_ref, dy_ref, dx_ref, dg_ref):
        @pl.when(pl.program_id(0) == 0)
        def _():
            dg_ref[...] = jnp.zeros_like(dg_ref)

        xv = x_ref[...]
        r = lax.rsqrt(jnp.mean(xv * xv, axis=-1, keepdims=True) + RMS_EPS)
        xh = xv * r
        dyv = dy_ref[...]
        dg_ref[...] += jnp.sum(dyv * xh, axis=0, keepdims=True)
        dxh = dyv * g_ref[...]
        dx_ref[...] = r * (dxh - xh * jnp.mean(dxh * xh, axis=-1, keepdims=True))

    dx, dg = pl.pallas_call(
        body, name=f"rms_bwd_{k}", grid=(s // tr,),
        in_specs=[pl.BlockSpec((tr, k), lambda i: (i, 0)), pl.BlockSpec((1, k), lambda i: (0, 0)),
                  pl.BlockSpec((tr, k), lambda i: (i, 0))],
        out_specs=[pl.BlockSpec((tr, k), lambda i: (i, 0)), pl.BlockSpec((1, k), lambda i: (0, 0))],
        out_shape=[jax.ShapeDtypeStruct((s, k), F32), jax.ShapeDtypeStruct((1, k), F32)],
        compiler_params=_params(dimension_semantics=("arbitrary",)),
    )(x, g.reshape(1, k), dy)
    return dx, dg.reshape(k)


@jax.custom_vjp
def rms_norm(x, g):
    return _rms_fwd_call(x, g)


rms_norm.defvjp(lambda x, g: (_rms_fwd_call(x, g), (x, g)), lambda res, dy: _rms_bwd_call(res[0], res[1], dy))


def _ln_fwd_call(x, y, g, b):
    s, k = x.shape
    tr = _row_block(s)

    def body(x_ref, y_ref, g_ref, b_ref, o_ref):
        u = ALPHA * x_ref[...] + y_ref[...]
        mu = jnp.mean(u, axis=-1, keepdims=True)
        d = u - mu
        var = jnp.mean(d * d, axis=-1, keepdims=True)
        o_ref[...] = d * lax.rsqrt(var + LN_EPS) * g_ref[...] + b_ref[...]

    row = pl.BlockSpec((tr, k), lambda i: (i, 0))
    vec = pl.BlockSpec((1, k), lambda i: (0, 0))
    return pl.pallas_call(
        body, name="ln_fwd", grid=(s // tr,), in_specs=[row, row, vec, vec], out_specs=row,
        out_shape=jax.ShapeDtypeStruct((s, k), F32), compiler_params=_params(dimension_semantics=("parallel",)),
    )(x, y, g.reshape(1, k), b.reshape(1, k))


def _ln_bwd_call(x, y, g, do):
    s, k = x.shape
    tr = _row_block(s)

    def body(x_ref, y_ref, g_ref, do_ref, dx_ref, dy_ref, dg_ref, db_ref):
        @pl.when(pl.program_id(0) == 0)
        def _():
            dg_ref[...] = jnp.zeros_like(dg_ref)
            db_ref[...] = jnp.zeros_like(db_ref)

        u = ALPHA * x_ref[...] + y_ref[...]
        mu = jnp.mean(u, axis=-1, keepdims=True)
        d = u - mu
        r = lax.rsqrt(jnp.mean(d * d, axis=-1, keepdims=True) + LN_EPS)
        xh = d * r
        dov = do_ref[...]
        dg_ref[...] += jnp.sum(dov * xh, axis=0, keepdims=True)
        db_ref[...] += jnp.sum(dov, axis=0, keepdims=True)
        dxh = dov * g_ref[...]
        du = r * (dxh - jnp.mean(dxh, axis=-1, keepdims=True) - xh * jnp.mean(dxh * xh, axis=-1, keepdims=True))
        dy_ref[...] = du
        dx_ref[...] = ALPHA * du

    row = pl.BlockSpec((tr, k), lambda i: (i, 0))
    vec = pl.BlockSpec((1, k), lambda i: (0, 0))
    dx, dy, dg, db = pl.pallas_call(
        body, name="ln_bwd", grid=(s // tr,), in_specs=[row, row, vec, row], out_specs=[row, row, vec, vec],
        out_shape=[jax.ShapeDtypeStruct((s, k), F32), jax.ShapeDtypeStruct((s, k), F32),
                   jax.ShapeDtypeStruct((1, k), F32), jax.ShapeDtypeStruct((1, k), F32)],
        compiler_params=_params(dimension_semantics=("arbitrary",)),
    )(x, y, g.reshape(1, k), do)
    return dx, dy, dg.reshape(k), db.reshape(k)


@jax.custom_vjp
def ln_res(x, y, g, b):
    return _ln_fwd_call(x, y, g, b)


ln_res.defvjp(lambda x, y, g, b: (_ln_fwd_call(x, y, g, b), (x, y, g)),
              lambda res, do: _ln_bwd_call(res[0], res[1], res[2], do))


def _rope_call(x, c, s1, s2, hs):
    s, w = x.shape
    tr = _row_block(s)
    nb = w // LANES

    def body(x_ref, c_ref, s1_ref, s2_ref, o_ref):
        cv, s1v, s2v = c_ref[...], s1_ref[...], s2_ref[...]
        for cb in range(nb):
            xb = x_ref[:, cb * LANES:(cb + 1) * LANES]
            o_ref[:, cb * LANES:(cb + 1) * LANES] = (
                xb * cv + pltpu.roll(xb, LANES - hs, 1) * s1v + pltpu.roll(xb, hs, 1) * s2v)

    row = pl.BlockSpec((tr, w), lambda i: (i, 0))
    tab = pl.BlockSpec((tr, LANES), lambda i: (i, 0))
    return pl.pallas_call(
        body, name=f"rope_{w}_{hs}", grid=(s // tr,), in_specs=[row, tab, tab, tab], out_specs=row,
        out_shape=jax.ShapeDtypeStruct((s, w), F32), compiler_params=_params(dimension_semantics=("parallel",)),
    )(x, c, s1, s2)


def _rope_tables(s, layout):
    pos = jnp.arange(s, dtype=F32)[:, None]
    lane = jnp.arange(LANES)
    if layout == "mla":
        dim, hs = MLA_ROPE, MLA_ROPE // 2
        r = lane - MLA_NOPE
        active = (r >= 0) & (r < MLA_ROPE)
    else:
        dim, hs = SWA_DIM, SWA_DIM // 2
        r = lane % SWA_DIM
        active = jnp.ones_like(lane, dtype=bool)
    f = jnp.where(active, r % hs, 0)
    inv = ROPE_THETA ** (-(2.0 * f.astype(F32)) / dim)
    ang = pos * inv[None, :]
    cos, sin = jnp.cos(ang), jnp.sin(ang)
    first = (active & (r < hs))[None, :]
    second = (active & (r >= hs))[None, :]
    c = jnp.where(active[None, :], cos, 1.0)
    s1 = jnp.where(first, -sin, 0.0)
    s2 = jnp.where(second, sin, 0.0)
    return (c, s1, s2), hs


def _gate_fwd_call(o_parts, gate):
    s, w = gate.shape
    tr = _row_block(s)
    widths = [o.shape[1] for o in o_parts]

    def body(*refs):
        o_refs, g_ref, z_ref = refs[:len(widths)], refs[len(widths)], refs[len(widths) + 1]
        off = 0
        for o_ref, wd in zip(o_refs, widths):
            gv = g_ref[:, off:off + wd]
            z_ref[:, off:off + wd] = o_ref[...] * (gv * jax.nn.sigmoid(gv))
            off += wd

    specs = [pl.BlockSpec((tr, wd), lambda i: (i, 0)) for wd in widths]
    row = pl.BlockSpec((tr, w), lambda i: (i, 0))
    return pl.pallas_call(
        body, name=f"gate_fwd_{len(widths)}", grid=(s // tr,), in_specs=specs + [row], out_specs=row,
        out_shape=jax.ShapeDtypeStruct((s, w), F32), compiler_params=_params(dimension_semantics=("parallel",)),
    )(*o_parts, gate)


def _gate_bwd_call(o_parts, gate, dz):
    s, w = gate.shape
    tr = _row_block(s)
    widths = [o.shape[1] for o in o_parts]
    n = len(widths)

    def body(*refs):
        o_refs, g_ref, dz_ref = refs[:n], refs[n], refs[n + 1]
        do_refs, dg_ref = refs[n + 2:2 * n + 2], refs[2 * n + 2]
        off = 0
        for o_ref, do_ref, wd in zip(o_refs, do_refs, widths):
            gv = g_ref[:, off:off + wd]
            sg = jax.nn.sigmoid(gv)
            dzv = dz_ref[:, off:off + wd]
            do_ref[...] = dzv * (gv * sg)
            dg_ref[:, off:off + wd] = dzv * o_ref[...] * (sg * (1.0 + gv * (1.0 - sg)))
            off += wd

    specs = [pl.BlockSpec((tr, wd), lambda i: (i, 0)) for wd in widths]
    row = pl.BlockSpec((tr, w), lambda i: (i, 0))
    outs = pl.pallas_call(
        body, name=f"gate_bwd_{n}", grid=(s // tr,), in_specs=specs + [row, row], out_specs=specs + [row],
        out_shape=[jax.ShapeDtypeStruct((s, wd), F32) for wd in widths] + [jax.ShapeDtypeStruct((s, w), F32)],
        compiler_params=_params(dimension_semantics=("parallel",)),
    )(*o_parts, gate, dz)
    return tuple(outs[:n]), outs[n]


@jax.custom_vjp
def gate_mul(o_parts, gate):
    return _gate_fwd_call(o_parts, gate)


gate_mul.defvjp(lambda o_parts, gate: (_gate_fwd_call(o_parts, gate), (o_parts, gate)),
                lambda res, dz: _gate_bwd_call(res[0], res[1], dz))


def _loss_call(y, t):
    s, k = y.shape
    tr = _row_block(s)
    nsteps = s // tr

    def body(y_ref, t_ref, l_ref, dy_ref, acc_ref):
        i = pl.program_id(0)

        @pl.when(i == 0)
        def _():
            acc_ref[...] = jnp.zeros_like(acc_ref)

        d = y_ref[...] - t_ref[...]
        dy_ref[...] = d / k
        acc_ref[...] += jnp.sum(d * d, axis=0, keepdims=True)

        @pl.when(i == nsteps - 1)
        def _():
            tot = jnp.sum(acc_ref[...], axis=1, keepdims=True) * (0.5 / k)
            l_ref[...] = jnp.broadcast_to(tot, l_ref.shape)

    row = pl.BlockSpec((tr, k), lambda i: (i, 0))
    return pl.pallas_call(
        body, name="loss", grid=(nsteps,), in_specs=[row, row],
        out_specs=[pl.BlockSpec((1, LANES), lambda i: (0, 0)), row],
        out_shape=[jax.ShapeDtypeStruct((1, LANES), F32), jax.ShapeDtypeStruct((s, k), F32)],
        scratch_shapes=[pltpu.VMEM((1, k), F32)], compiler_params=_params(dimension_semantics=("arbitrary",)),
    )(y, t)


@jax.custom_vjp
def mse_loss(y, t):
    return _loss_call(y, t)[0][0, 0]


def _mse_fwd(y, t):
    l, dy = _loss_call(y, t)
    return l[0, 0], (dy, t)


mse_loss.defvjp(_mse_fwd, lambda res, g: (g * res[0], jnp.zeros_like(res[1])))


def _scan_call(x, b, mode):
    s, w = x.shape
    nt = s // 8

    def tile_scan(t):
        row = lax.broadcasted_iota(jnp.int32, (8, w), 0)
        for sh in (1, 2, 4):
            t = t + jnp.where(row >= sh, pltpu.roll(t, sh, 0), 0.0)
        return t

    def body(x_ref, b_ref, o_ref):
        def step(i, carry):
            rows = pl.ds(pl.multiple_of(i * 8, 8), 8)
            t = x_ref[rows, :]
            if mode == "fwd":
                t = jax.nn.log_sigmoid(t + b_ref[...])
            t = tile_scan(t) + carry
            o_ref[rows, :] = t
            return t[7:8, :]

        total = lax.fori_loop(0, nt, step, jnp.zeros((1, w), F32))
        if mode == "rev":
            def fix(i, c):
                rows = pl.ds(pl.multiple_of(i * 8, 8), 8)
                o_ref[rows, :] = total - o_ref[rows, :] + x_ref[rows, :]
                return c
            lax.fori_loop(0, nt, fix, 0)

    full = pl.BlockSpec((s, w), lambda: (0, 0))
    return pl.pallas_call(
        body, name=f"scan_{mode}", in_specs=[full, pl.BlockSpec((1, w), lambda: (0, 0))], out_specs=full,
        out_shape=jax.ShapeDtypeStruct((s, w), F32), compiler_params=_params(),
    )(x, b)


def _fox_dlogit_call(x, b, dlogf):
    s, w = x.shape
    tr = _row_block(s)

    def body(x_ref, b_ref, d_ref, dx_ref, db_ref):
        @pl.when(pl.program_id(0) == 0)
        def _():
            db_ref[...] = jnp.zeros_like(db_ref)

        dx = d_ref[...] * jax.nn.sigmoid(-(x_ref[...] + b_ref[...]))
        dx_ref[...] = dx
        db_ref[...] += jnp.sum(dx, axis=0, keepdims=True)

    row = pl.BlockSpec((tr, w), lambda i: (i, 0))
    vec = pl.BlockSpec((1, w), lambda i: (0, 0))
    return pl.pallas_call(
        body, name="fox_dlogit", grid=(s // tr,), in_specs=[row, vec, row], out_specs=[row, vec],
        out_shape=[jax.ShapeDtypeStruct((s, w), F32), jax.ShapeDtypeStruct((1, w), F32)],
        compiler_params=_params(dimension_semantics=("arbitrary",)),
    )(x, b, dlogf)


@jax.custom_vjp
def fox_cum(fl, b):
    return _scan_call(fl, b, "fwd")


def _fox_cum_bwd(res, dcum):
    fl, b = res
    dlogf = _scan_call(dcum, b, "rev")
    return _fox_dlogit_call(fl, b, dlogf)


fox_cum.defvjp(lambda fl, b: (_scan_call(fl, b, "fwd"), (fl, b)), _fox_cum_bwd)


def _lane_col(x, lane_idx):
    lane = lax.broadcasted_iota(jnp.int32, (1, x.shape[1]), 1)
    return jnp.sum(jnp.where(lane == lane_idx, x, 0.0), axis=1, keepdims=True)


def _row_of(x, row_idx):
    row = lax.broadcasted_iota(jnp.int32, (x.shape[0], 1), 0)
    return jnp.sum(jnp.where(row == row_idx, x, 0.0), axis=0, keepdims=True)


def _attn_cfg(mode, s):
    if mode == "swa":
        blk = 256 if s >= 2048 else 128
        return dict(blk=blk, n_outer=SWA_KV_HEADS, pps=4, wide=False, scale=SWA_DIM ** -0.5)
    blk = 512 if s >= 2048 else 128
    if mode == "mla":
        return dict(blk=blk, n_outer=4, pps=1, wide=True, scale=(MLA_NOPE + MLA_ROPE) ** -0.5)
    return dict(blk=blk, n_outer=4, pps=1, wide=False, scale=FOX_DIM ** -0.5)


ROW_CHUNK = 32


def _unrolled(n, body, carry):
    for c in range(n):
        carry = body(c, carry)
    return carry


def _valid_rows(mode, i, jb, blk, r0, rc):
    qpos = i * blk + r0 + lax.broadcasted_iota(jnp.int32, (rc, blk), 0)
    kpos = jb * blk + lax.broadcasted_iota(jnp.int32, (rc, blk), 1)
    ok = kpos <= qpos
    if mode == "swa":
        ok = ok & (qpos - kpos < WINDOW)
    return ok


def _attn_fwd_call(mode, q, k, v, extra):
    s = q.shape[0]
    cfg = _attn_cfg(mode, s)
    blk, n_outer, pps, wide, scale = cfg["blk"], cfg["n_outer"], cfg["pps"], cfg["wide"], cfg["scale"]
    rc = ROW_CHUNK
    nq = s // blk
    swa, fox = mode == "swa", mode == "fox"
    qw = (2 * LANES if wide else LANES) * pps
    kw = 2 * LANES if wide else LANES
    ow = LANES * pps
    reps = blk // LANES

    def body(*refs):
        if not swa:
            it_ref, jt_ref = refs[:2]
            refs = refs[2:]
        q_ref, k_ref, v_ref = refs[:3]
        n_in = 3
        if fox:
            cum_ref, cumt_ref = refs[3:5]
            n_in = 5
        if swa:
            sink_ref = refs[3]
            n_in = 4
        o_ref, lse_ref, m_ref, l_ref, acc_ref, a_ref, s_all, p_all, c_all = refs[n_in:]
        p_id = pl.program_id(0)
        if swa:
            i, j = pl.program_id(1), pl.program_id(2)
            jb, run, first, last = i - 1 + j, (i - 1 + j) >= 0, j == 0, j == 1
        else:
            i, j = it_ref[pl.program_id(1)], jt_ref[pl.program_id(1)]
            jb, first, last = j, j == 0, j == i
        lane = lax.broadcasted_iota(jnp.int32, (1, LANES), 1)
        msk = [lane < HALF, lane >= HALF]

        @pl.when(first)
        def _():
            for hh in range(2 * pps):
                if swa:
                    m_ref[hh] = jnp.broadcast_to(sink_ref[hh:hh + 1, :], (blk, LANES))
                    l_ref[hh] = jnp.ones((blk, LANES), F32)
                else:
                    m_ref[hh] = jnp.full((blk, LANES), NEG, F32)
                    l_ref[hh] = jnp.zeros((blk, LANES), F32)
            acc_ref[...] = jnp.zeros_like(acc_ref)

        def process(masked):
            for pp in range(pps):
                vb = v_ref[...]
                pvs = []
                for h in range(2):
                    hh = 2 * pp + h
                    s_ref, p_ref, c_ref = s_all.at[hh], p_all.at[hh], c_all.at[hh]
                    if wide:
                        qh = q_ref[:, h * LANES:(h + 1) * LANES] * scale
                        kh = k_ref[:, h * LANES:(h + 1) * LANES]
                    else:
                        qh = jnp.where(msk[h], q_ref[:, pp * LANES:(pp + 1) * LANES], 0.0) * scale
                        kh = k_ref[...]
                    s_ref[...] = lax.dot_general(qh.astype(BF16), kh.astype(BF16), (((1,), (1,)), ((), ())),
                                                 preferred_element_type=F32)
                    if fox:
                        head = 2 * p_id + h
                        c_ref[...] = jnp.broadcast_to(_lane_col(cum_ref[...], head), (blk, LANES))
                        ck = _row_of(cumt_ref[...], head)

                    def chunk(c, carry, hh=hh, h=h):
                        r0 = c * rc
                        rows = pl.ds(r0, rc)
                        u = s_ref[rows, :]
                        if fox:
                            u = u - ck
                        if masked:
                            u = jnp.where(_valid_rows(mode, i, jb, blk, r0, rc), u, NEG)
                        m_prev, l_prev = m_ref[hh, rows, :], l_ref[hh, rows, :]
                        m_cur = jnp.max(u, axis=1, keepdims=True)
                        if fox:
                            m_cur = m_cur + c_ref[rows, :]
                        m_next = jnp.maximum(m_prev, m_cur)
                        shift = m_next - c_ref[rows, :] if fox else m_next
                        p = jnp.exp(u - jnp.tile(shift, (1, reps)))
                        alpha = jnp.exp(m_prev - m_next)
                        l_ref[hh, rows, :] = alpha * l_prev + jnp.sum(p, axis=1, keepdims=True)
                        m_ref[hh, rows, :] = m_next
                        a_ref[hh, rows, :] = alpha
                        p_ref[rows, :] = p.astype(BF16)
                        return carry

                    _unrolled(blk // rc, chunk, 0)
                    vh = jnp.where(msk[h], vb, 0.0).astype(BF16)
                    pvs.append(jnp.dot(p_ref[...], vh, preferred_element_type=F32))
                acc_ref[pp] = acc_ref[pp] * jnp.where(msk[0], a_ref[2 * pp], a_ref[2 * pp + 1]) + pvs[0] + pvs[1]

        if swa:
            pl.when(run)(lambda: process(True))
        else:
            pl.when(j < i)(lambda: process(False))
            pl.when(j == i)(lambda: process(True))

        @pl.when(last)
        def _():
            for pp in range(pps):
                l0, l1 = l_ref[2 * pp], l_ref[2 * pp + 1]
                o_ref[:, pp * LANES:(pp + 1) * LANES] = acc_ref[pp] / jnp.where(msk[0], l0, l1)
                lse_ref[:, pp * LANES:(pp + 1) * LANES] = jnp.where(
                    msk[0], m_ref[2 * pp] + jnp.log(l0), m_ref[2 * pp + 1] + jnp.log(l1))

    if swa:
        kv_map = lambda g, i, j: (jnp.maximum(i - 1 + j, 0), g)
        q_map = lambda g, i, j: (i, g)
        grid, tables, sem = (n_outer, nq, 2), [], ("parallel", "parallel", "arbitrary")
    else:
        tri = [(i, j) for i in range(nq) for j in range(i + 1)]
        tables = [jnp.asarray([t[0] for t in tri], jnp.int32), jnp.asarray([t[1] for t in tri], jnp.int32)]
        kv_map = lambda p, t, it, jt: (jt[t], p)
        q_map = lambda p, t, it, jt: (it[t], p)
        grid, sem = (n_outer, len(tri)), ("parallel", "arbitrary")
    in_specs = [pl.BlockSpec((blk, qw), q_map), pl.BlockSpec((blk, kw), kv_map), pl.BlockSpec((blk, LANES), kv_map)]
    args = [q, k, v]
    if fox:
        cum, cumt = extra
        in_specs += [pl.BlockSpec((blk, LANES), lambda p, t, it, jt: (it[t], 0)),
                     pl.BlockSpec((8, blk), lambda p, t, it, jt: (0, jt[t]))]
        args += [cum, cumt]
    if swa:
        in_specs += [pl.BlockSpec((8, LANES), lambda g, i, j: (g, 0))]
        args += [extra]
    n_pairs = n_outer * pps
    return pl.pallas_call(
        body, name=f"attn_fwd_{mode}",
        grid_spec=pltpu.PrefetchScalarGridSpec(
            num_scalar_prefetch=len(tables), grid=grid, in_specs=in_specs,
            out_specs=[pl.BlockSpec((blk, ow), q_map), pl.BlockSpec((blk, ow), q_map)],
            scratch_shapes=[pltpu.VMEM((2 * pps, blk, LANES), F32), pltpu.VMEM((2 * pps, blk, LANES), F32),
                            pltpu.VMEM((pps, blk, LANES), F32), pltpu.VMEM((2 * pps, blk, LANES), F32),
                            pltpu.VMEM((2 * pps, blk, blk), F32), pltpu.VMEM((2 * pps, blk, blk), BF16),
                            pltpu.VMEM((2 * pps, blk, LANES), F32)]),
        out_shape=[jax.ShapeDtypeStruct((s, n_pairs * LANES), F32), jax.ShapeDtypeStruct((s, n_pairs * LANES), F32)],
        compiler_params=_params(dimension_semantics=sem),
    )(*tables, *args)


def _attn_bwd_call(mode, q, k, v, extra, lse, o, do):
    s = q.shape[0]
    cfg = _attn_cfg(mode, s)
    blk, n_outer, pps, wide, scale = cfg["blk"], cfg["n_outer"], cfg["pps"], cfg["wide"], cfg["scale"]
    rc = ROW_CHUNK
    nq = s // blk
    swa, fox = mode == "swa", mode == "fox"
    qw = (2 * LANES if wide else LANES) * pps
    kw = 2 * LANES if wide else LANES
    ow = LANES * pps
    reps = blk // LANES

    assert not swa

    def body(*refs):
        jt_ref, it_ref = refs[:2]
        refs = refs[2:]
        q_ref, k_ref, v_ref, lse_ref, o_ref, do_ref = refs[:6]
        n_in = 6
        if fox:
            cum_ref, cumt_ref = refs[6:8]
            n_in = 8
        dq_ref, dk_ref, dv_ref = refs[n_in:n_in + 3]
        n_out = n_in + 3
        if fox:
            dck_ref, dcq_ref = refs[n_out:n_out + 2]
            n_out += 2
        dk_acc, dv_acc, s_all, dp_all, p_all, ds_all, e_all, d_all = refs[n_out:n_out + 8]
        if fox:
            dck_acc, rs_all = refs[n_out + 8:n_out + 10]
        p_id, t = pl.program_id(0), pl.program_id(1)
        j, ii = jt_ref[t], it_ref[t]
        i, first_i, last_i = ii, ii == j, ii == nq - 1
        lane = lax.broadcasted_iota(jnp.int32, (1, LANES), 1)
        msk = [lane < HALF, lane >= HALF]

        @pl.when(t == 0)
        def _():
            dq_ref[...] = jnp.zeros_like(dq_ref)
            if swa:
                dsink_ref[...] = jnp.zeros_like(dsink_ref)
            if fox:
                dcq_ref[...] = jnp.zeros_like(dcq_ref)

        @pl.when(first_i)
        def _():
            dk_acc[...] = jnp.zeros_like(dk_acc)
            dv_acc[...] = jnp.zeros_like(dv_acc)
            if fox:
                dck_acc[...] = jnp.zeros_like(dck_acc)

        def process(masked):
            rows = pl.ds(pl.multiple_of(i * blk, blk), blk)
            vb = v_ref[...].astype(BF16)
            dv_parts, dk_parts = [], []
            for pp in range(pps):
                psl = slice(pp * LANES, (pp + 1) * LANES)
                lse_blk, do_blk = lse_ref[:, psl], do_ref[:, psl]
                doo = do_blk * o_ref[:, psl]
                dq_pair = []
                for h in range(2):
                    hh = 2 * pp + h
                    s_ref, dp_ref, p_ref, ds_ref = s_all.at[hh], dp_all.at[hh], p_all.at[hh], ds_all.at[hh]
                    e_ref, d_ref = e_all.at[hh], d_all.at[hh]
                    if fox:
                        rs_ref = rs_all.at[hh]
                    if wide:
                        hsl = slice(h * LANES, (h + 1) * LANES)
                        qh = (q_ref[:, hsl] * scale).astype(BF16)
                        kh = k_ref[:, hsl].astype(BF16)
                    else:
                        qh = (jnp.where(msk[h], q_ref[:, psl], 0.0) * scale).astype(BF16)
                        kh = k_ref[...].astype(BF16)
                    s_ref[...] = lax.dot_general(qh, kh, (((1,), (1,)), ((), ())), preferred_element_type=F32)
                    do_h = jnp.where(msk[h], do_blk, 0.0).astype(BF16)
                    dp_ref[...] = lax.dot_general(do_h, vb, (((1,), (1,)), ((), ())), preferred_element_type=F32)
                    lse_h = _lane_col(lse_blk, HALF * h)
                    d_h = jnp.sum(jnp.where(msk[h], doo, 0.0), axis=1, keepdims=True)
                    e_ref[...] = jnp.broadcast_to(lse_h, (blk, LANES))
                    d_ref[...] = jnp.broadcast_to(d_h, (blk, LANES))
                    if fox:
                        head = 2 * p_id + h
                        e_ref[...] = e_ref[...] - jnp.broadcast_to(_lane_col(cum_ref[...], head), (blk, LANES))
                        ck = _row_of(cumt_ref[...], head)

                    def chunk(c, colsum):
                        r0 = c * rc
                        cr = pl.ds(r0, rc)
                        u = s_ref[cr, :]
                        if fox:
                            u = u - ck
                        p = jnp.exp(u - jnp.tile(e_ref[cr, :], (1, reps)))
                        if masked:
                            p = jnp.where(_valid_rows(mode, i, j, blk, r0, rc), p, 0.0)
                        ds = p * (dp_ref[cr, :] - jnp.tile(d_ref[cr, :], (1, reps)))
                        p_ref[cr, :] = p.astype(BF16)
                        ds_ref[cr, :] = ds.astype(BF16)
                        if fox:
                            colsum = colsum + jnp.sum(ds, axis=0, keepdims=True)
                            rs_ref[cr, :] = jnp.broadcast_to(jnp.sum(ds, axis=1, keepdims=True), (rc, LANES))
                        return colsum

                    colsum = _unrolled(blk // rc, chunk, jnp.zeros((1, blk), F32))
                    dv_parts.append(lax.dot_general(p_ref[...], do_h, (((0,), (0,)), ((), ())),
                                                    preferred_element_type=F32))
                    if fox:
                        dck_acc[h:h + 1, :] += -colsum
                        dcq_ref[rows, :] += jnp.where(msk[h], rs_ref[...], 0.0)
                    dq_h = jnp.dot(ds_ref[...], kh, preferred_element_type=F32) * scale
                    dk_h = lax.dot_general(ds_ref[...], qh, (((0,), (0,)), ((), ())), preferred_element_type=F32)
                    if wide:
                        dq_ref[rows, hsl] += dq_h
                        dk_acc[:, hsl] += dk_h
                    else:
                        dq_pair.append(jnp.where(msk[h], dq_h, 0.0))
                        dk_parts.append(dk_h)
                if not wide:
                    dq_ref[rows, psl] += dq_pair[0] + dq_pair[1]
            dv_acc[...] += functools.reduce(lambda a, b: a + b, dv_parts)
            if not wide:
                dk_acc[...] += functools.reduce(lambda a, b: a + b, dk_parts)

        pl.when(ii > j)(lambda: process(False))
        pl.when(ii == j)(lambda: process(True))

        @pl.when(last_i)
        def _():
            dk_ref[...] = dk_acc[...]
            dv_ref[...] = dv_acc[...]
            if fox:
                dck_ref[0] = dck_acc[...]

    tri = [(j, i) for j in range(nq) for i in range(j, nq)]
    tables = [jnp.asarray([t[0] for t in tri], jnp.int32), jnp.asarray([t[1] for t in tri], jnp.int32)]
    q_map = lambda p, t, jt, it: (it[t], p)
    kv_map = lambda p, t, jt, it: (jt[t], p)
    in_specs = [pl.BlockSpec((blk, qw), q_map), pl.BlockSpec((blk, kw), kv_map), pl.BlockSpec((blk, LANES), kv_map),
                pl.BlockSpec((blk, ow), q_map), pl.BlockSpec((blk, ow), q_map), pl.BlockSpec((blk, ow), q_map)]
    args = [q, k, v, lse, o, do]
    n_pairs = n_outer * pps
    out_specs = [pl.BlockSpec((s, qw), lambda p, t, jt, it: (0, p)), pl.BlockSpec((blk, kw), kv_map),
                 pl.BlockSpec((blk, LANES), kv_map)]
    out_shape = [jax.ShapeDtypeStruct((s, q.shape[1]), F32), jax.ShapeDtypeStruct((s, k.shape[1]), F32),
                 jax.ShapeDtypeStruct((s, v.shape[1]), F32)]
    nh = 2 * pps
    scratch = [pltpu.VMEM((blk, kw), F32), pltpu.VMEM((blk, LANES), F32), pltpu.VMEM((nh, blk, blk), F32),
               pltpu.VMEM((nh, blk, blk), F32), pltpu.VMEM((nh, blk, blk), BF16), pltpu.VMEM((nh, blk, blk), BF16),
               pltpu.VMEM((nh, blk, LANES), F32), pltpu.VMEM((nh, blk, LANES), F32)]
    if fox:
        cum, cumt = extra
        in_specs += [pl.BlockSpec((blk, LANES), lambda p, t, jt, it: (it[t], 0)),
                     pl.BlockSpec((8, blk), lambda p, t, jt, it: (0, jt[t]))]
        args += [cum, cumt]
        out_specs += [pl.BlockSpec((1, 8, blk), lambda p, t, jt, it: (p, 0, jt[t])),
                      pl.BlockSpec((s, LANES), lambda p, t, jt, it: (0, p))]
        out_shape += [jax.ShapeDtypeStruct((n_pairs, 8, s), F32), jax.ShapeDtypeStruct((s, n_pairs * LANES), F32)]
        scratch += [pltpu.VMEM((8, blk), F32), pltpu.VMEM((nh, blk, LANES), F32)]
    return pl.pallas_call(
        body, name=f"attn_bwd_{mode}",
        grid_spec=pltpu.PrefetchScalarGridSpec(num_scalar_prefetch=2, grid=(n_outer, len(tri)), in_specs=in_specs,
                                               out_specs=out_specs, scratch_shapes=scratch),
        out_shape=out_shape, compiler_params=_params(dimension_semantics=("parallel", "arbitrary")),
    )(*tables, *args)


def _swa_masks(i, blk):
    r = lax.broadcasted_iota(jnp.int32, (blk, blk), 0)
    c = lax.broadcasted_iota(jnp.int32, (blk, blk), 1)
    return (c > r) & (i > 0), c <= r


def _nt(a, b):
    return lax.dot_general(a, b, (((1,), (1,)), ((), ())), preferred_element_type=F32)


def _tn(a, b):
    return lax.dot_general(a, b, (((0,), (0,)), ((), ())), preferred_element_type=F32)


def _swa_bwd_call(q, k, v, sink, lse, o, do):
    s = q.shape[0]
    blk, pps, scale = WINDOW, 4, SWA_DIM ** -0.5
    nq = s // blk

    def body(q_ref, kp_ref, ko_ref, vp_ref, vo_ref, sink_ref, lse_ref, o_ref, do_ref,
             dq_ref, dk_ref, dv_ref, dsink_ref, ck_ref, cv_ref):
        i = pl.program_id(1)
        lane = lax.broadcasted_iota(jnp.int32, (1, LANES), 1)
        msk = [lane < HALF, lane >= HALF]

        @pl.when(i == 0)
        def _():
            ck_ref[...] = jnp.zeros_like(ck_ref)
            cv_ref[...] = jnp.zeros_like(cv_ref)
            dsink_ref[...] = jnp.zeros_like(dsink_ref)

        @pl.when(i < nq)
        def _():
            ok_prev, ok_own = _swa_masks(i, blk)
            kp, ko = kp_ref[...].astype(BF16), ko_ref[...].astype(BF16)
            vp, vo = vp_ref[...].astype(BF16), vo_ref[...].astype(BF16)
            dkp, dko, dvp, dvo = [], [], [], []
            for pp in range(pps):
                psl = slice(pp * LANES, (pp + 1) * LANES)
                qp, do_blk = q_ref[:, psl], do_ref[:, psl]
                doo = do_blk * o_ref[:, psl]
                dqs = []
                for h in range(2):
                    hh = 2 * pp + h
                    qh = (jnp.where(msk[h], qp, 0.0) * scale).astype(BF16)
                    lse_h = jnp.broadcast_to(_lane_col(lse_ref[:, psl], HALF * h), (blk, LANES))
                    d_h = jnp.broadcast_to(jnp.sum(jnp.where(msk[h], doo, 0.0), axis=1, keepdims=True), (blk, LANES))
                    p_p = jnp.where(ok_prev, jnp.exp(_nt(qh, kp) - lse_h), 0.0)
                    p_o = jnp.where(ok_own, jnp.exp(_nt(qh, ko) - lse_h), 0.0)
                    do_h = jnp.where(msk[h], do_blk, 0.0).astype(BF16)
                    ds_p = (p_p * (_nt(do_h, vp) - d_h)).astype(BF16)
                    ds_o = (p_o * (_nt(do_h, vo) - d_h)).astype(BF16)
                    dq_h = (jnp.dot(ds_p, kp, preferred_element_type=F32)
                            + jnp.dot(ds_o, ko, preferred_element_type=F32)) * scale
                    dqs.append(jnp.where(msk[h], dq_h, 0.0))
                    dkp.append(_tn(ds_p, qh))
                    dko.append(_tn(ds_o, qh))
                    dvp.append(_tn(p_p.astype(BF16), do_h))
                    dvo.append(_tn(p_o.astype(BF16), do_h))
                    sink_row = sink_ref[hh:hh + 1, :]
                    dsink_ref[hh:hh + 1, :] += -jnp.sum(jnp.exp(sink_row - lse_h) * d_h, axis=0, keepdims=True)
                dq_ref[:, psl] = dqs[0] + dqs[1]
            total = lambda parts: functools.reduce(lambda a, b: a + b, parts)
            dk_ref[...] = ck_ref[...] + total(dkp)
            dv_ref[...] = cv_ref[...] + total(dvp)
            ck_ref[...] = total(dko)
            cv_ref[...] = total(dvo)

        @pl.when(i == nq)
        def _():
            dk_ref[...] = ck_ref[...]
            dv_ref[...] = cv_ref[...]

    last = nq - 1
    prev = lambda g, i: (jnp.maximum(i - 1, 0), g)
    own = lambda g, i: (jnp.minimum(i, last), g)
    qspec = pl.BlockSpec((blk, pps * LANES), own)
    kspec = lambda m: pl.BlockSpec((blk, LANES), m)
    sspec = pl.BlockSpec((8, LANES), lambda g, i: (g, 0))
    return pl.pallas_call(
        body, name="swa_bwd", grid=(SWA_KV_HEADS, nq + 1),
        in_specs=[qspec, kspec(prev), kspec(own), kspec(prev), kspec(own), sspec, qspec, qspec, qspec],
        out_specs=[qspec, kspec(prev), kspec(prev), sspec],
        out_shape=[jax.ShapeDtypeStruct(q.shape, F32), jax.ShapeDtypeStruct(k.shape, F32),
                   jax.ShapeDtypeStruct(v.shape, F32), jax.ShapeDtypeStruct((SWA_HEADS, LANES), F32)],
        scratch_shapes=[pltpu.VMEM((blk, LANES), F32), pltpu.VMEM((blk, LANES), F32)],
        compiler_params=_params(dimension_semantics=("parallel", "arbitrary")),
    )(q, k, k, v, v, sink, lse, o, do)


def _make_attn(mode):
    swa = mode == "swa"

    @jax.custom_vjp
    def attn(q, k, v, extra):
        return fwd(q, k, v, extra)[0]

    def fwd(q, k, v, extra):
        o, lse = _attn_fwd_call(mode, q, k, v, extra)
        return o, (q, k, v, extra, o, lse)

    def bwd(res, do):
        q, k, v, extra, o, lse = res
        outs = (_swa_bwd_call(q, k, v, extra, lse, o, do) if swa
                else _attn_bwd_call(mode, q, k, v, extra, lse, o, do))
        dq, dk, dv = outs[:3]
        if mode == "fox":
            cum, cumt = extra
            dck = outs[3]
            dcumt = dck[:, :2, :].reshape(FOX_HEADS, -1)
            dcq = outs[4].reshape(-1, FOX_HEADS, HALF)[:, :, 0]
            dextra = (jnp.pad(dcq, ((0, 0), (0, LANES - FOX_HEADS))), dcumt)
        elif mode == "swa":
            dextra = jnp.where(jnp.arange(LANES)[None, :] == 0, outs[3], 0.0)
        else:
            dextra = None
        return dq, dk, dv, dextra

    attn.defvjp(fwd, bwd)
    return attn


attn_mla = _make_attn("mla")
attn_fox = _make_attn("fox")
attn_swa = _make_attn("swa")


def _ukv_layout(w):
    r = w.shape[0]
    w3 = w.reshape(r, MLA_HEADS, MLA_NOPE + MLA_V)
    wk = jnp.pad(w3[:, :, :MLA_NOPE], ((0, 0), (0, 0), (0, LANES - MLA_NOPE))).reshape(r, MLA_HEADS * LANES)
    wv = w3[:, :, MLA_NOPE:].reshape(r, MLA_HEADS * MLA_V)
    return wk, wv


def _even_layer(x, w_in_cat, q_norm, w_uq_p, kv_norm, w_ukv, b_f, w_out, ln_g, ln_b, tabs_mla):
    tabs, hs = tabs_mla
    cq, ckv, kpe, fq, fk, fv, fl, gate = even_in_proj(x, relayout(w_in_cat, "even"), tabs)
    q = mm_rope(rms_norm(cq, q_norm), w_uq_p, tabs, hs)
    ckvn = rms_norm(ckv, kv_norm)
    wk, wv = _ukv_layout(w_ukv)
    kk = mm(ckvn, wk) + jnp.tile(kpe, (1, MLA_HEADS))
    o_mla = attn_mla(q, kk, mm(ckvn, wv), None)
    cum = fox_cum(fl, jnp.pad(b_f, (0, LANES - FOX_HEADS)).reshape(1, LANES))
    o_fox = attn_fox(fq, fk, fv, (cum, cum[:, :8].T))
    y = mm(gate_mul((o_mla, o_fox), gate), w_out)
    return ln_res(x, y, ln_g, ln_b)


def _odd_layer(x, w_in_cat, sinks, w_out, ln_g, ln_b, tabs_swa):
    q, kd, vd, gate = odd_in_proj(x, relayout(w_in_cat, "odd"), tabs_swa[0])
    o = attn_swa(q, kd, vd, jnp.broadcast_to(sinks[:, None], (SWA_HEADS, LANES)))
    y = mm(gate_mul((o,), gate), w_out)
    return ln_res(x, y, ln_g, ln_b)


EVEN_SHARDED = ["even_w_in", "even_w_uq", "even_w_ukv", "even_w_out"]
ODD_SHARDED = ["odd_w_in", "odd_w_out", "odd_ln_g", "odd_ln_b"]
EVEN_REPL = ["even_q_norm", "even_kv_norm", "even_b_f", "even_ln_g", "even_ln_b"]
ODD_REPL = ["odd_sinks"]


def _layer_names(layer):
    return (EVEN_SHARDED, EVEN_REPL) if layer % 2 == 0 else (ODD_SHARDED, ODD_REPL)


def _layer_of(name, j):
    return 2 * j if name.startswith("even") else 2 * j + 1


def _layer_apply(layer, p, x, tabs):
    if layer % 2 == 0:
        return _even_layer(x, p["even_w_in"], p["even_q_norm"], p["even_w_uq"], p["even_kv_norm"], p["even_w_ukv"],
                           p["even_b_f"], p["even_w_out"], p["even_ln_g"], p["even_ln_b"], tabs["mla"])
    return _odd_layer(x, p["odd_w_in"], p["odd_sinks"], p["odd_w_out"], p["odd_ln_g"], p["odd_ln_b"], tabs["swa"])


def _pad_rows(flat, mult):
    n = flat.shape[-1]
    per = mult * LANES
    padded = -(-n // per) * per
    if padded != n:
        flat = jnp.pad(flat, [(0, 0)] * (flat.ndim - 1) + [(0, padded - n)])
    return flat.reshape(flat.shape[:-1] + (padded // LANES, LANES))


def _pad_last(a, width):
    if a.shape[-1] == width:
        return a
    return jnp.pad(a, [(0, 0)] * (a.ndim - 1) + [(0, width - a.shape[-1])])


def _join(slots, axis):
    shp = list(slots.shape[1:])
    shp[axis] *= N_DEV
    return jnp.moveaxis(slots, 0, axis).reshape(shp)


def _split(full, axis):
    shp = full.shape
    t = full.reshape(shp[:axis] + (N_DEV, shp[axis] // N_DEV) + shp[axis + 1:])
    return jnp.moveaxis(t, axis, 0)


PAD_TO = {"even_w_in": SHARD_PAD, "even_w_uq": LANES, "odd_w_in": SHARD_PAD}


def kernel(x, even_w_in, even_q_norm, even_w_uq, even_kv_norm, even_w_ukv, even_b_f, even_w_out, even_ln_g, even_ln_b, odd_w_in, odd_sinks, odd_w_out, odd_ln_g, odd_ln_b, loss_target, m_even_w_in, m_even_q_norm, m_even_w_uq, m_even_kv_norm, m_even_w_ukv, m_even_b_f, m_even_w_out, m_even_ln_g, m_even_ln_b, m_odd_w_in, m_odd_sinks, m_odd_w_out, m_odd_ln_g, m_odd_ln_b, v_even_w_in, v_even_q_norm, v_even_w_uq, v_even_kv_norm, v_even_w_ukv, v_even_b_f, v_even_w_out, v_even_ln_g, v_even_ln_b, v_odd_w_in, v_odd_sinks, v_odd_w_out, v_odd_ln_g, v_odd_ln_b):
    w = dict(even_w_in=even_w_in, even_q_norm=even_q_norm, even_w_uq=even_w_uq, even_kv_norm=even_kv_norm,
             even_w_ukv=even_w_ukv, even_b_f=even_b_f, even_w_out=even_w_out, even_ln_g=even_ln_g, even_ln_b=even_ln_b,
             odd_w_in=odd_w_in, odd_sinks=odd_sinks, odd_w_out=odd_w_out, odd_ln_g=odd_ln_g, odd_ln_b=odd_ln_b)
    mom = dict(even_w_in=m_even_w_in, even_q_norm=m_even_q_norm, even_w_uq=m_even_w_uq, even_kv_norm=m_even_kv_norm,
               even_w_ukv=m_even_w_ukv, even_b_f=m_even_b_f, even_w_out=m_even_w_out, even_ln_g=m_even_ln_g,
               even_ln_b=m_even_ln_b, odd_w_in=m_odd_w_in, odd_sinks=m_odd_sinks, odd_w_out=m_odd_w_out,
               odd_ln_g=m_odd_ln_g, odd_ln_b=m_odd_ln_b)
    vel = dict(even_w_in=v_even_w_in, even_q_norm=v_even_q_norm, even_w_uq=v_even_w_uq, even_kv_norm=v_even_kv_norm,
               even_w_ukv=v_even_w_ukv, even_b_f=v_even_b_f, even_w_out=v_even_w_out, even_ln_g=v_even_ln_g,
               even_ln_b=v_even_ln_b, odd_w_in=v_odd_w_in, odd_sinks=v_odd_sinks, odd_w_out=v_odd_w_out,
               odd_ln_g=v_odd_ln_g, odd_ln_b=v_odd_ln_b)
    sharded = BIG + SMALL_SHARDED
    padded = lambda d, n: _pad_last(d[n], PAD_TO.get(n, d[n].shape[-1]))

    tabs = {"mla": _rope_tables(x.shape[1], "mla"), "swa": _rope_tables(x.shape[1], "swa")}
    keys = lambda layers: [(n, layer // 2) for layer in layers for n in _layer_names(layer)[0]]
    first, rest = keys([0]), keys([1, 2, 3])
    me = _lin(_me())

    def shard(n, j):
        a = padded(w, n)[j]
        return a.astype(BF16) if n in BIG else a

    to_full = lambda n, g: _join(g, SHARD_AXIS[n] - 1).astype(F32)
    to_slots = lambda n, g: _split(g, SHARD_AXIS[n] - 1).astype(BF16 if n in BIG else F32)

    def layer_params(layer, full_of):
        shn, rpn = _layer_names(layer)
        p = {n: full_of(n) for n in shn}
        p.update({n: w[n][layer // 2] for n in rpn})
        return p

    got0 = _all_gather([shard(n, j) for n, j in first], "all_gather_first")
    got0, mine_rest = lax.optimization_barrier((got0, [shard(n, j) for n, j in rest]))
    got0 = dict(zip(first, got0))
    send_sems, recv_sems, srcs, lands, token = _split_start(mine_rest, False, "all_gather_rest_start")
    p0 = layer_params(0, lambda n: to_full(n, got0[(n, 0)]) + token[0, 0])
    x1, vjp0 = jax.vjp(lambda p, xx: _layer_apply(0, p, xx, tabs), p0, x[0])
    got = _split_wait(send_sems, recv_sems, srcs, lands, x1, False, "all_gather_rest_wait")
    got = dict(zip(rest, _own_slot(got, [m[None] for m in mine_rest])))

    xs, vjps = x1, [vjp0]
    for layer in (1, 2, 3):
        p = layer_params(layer, lambda n: to_full(n, got[(n, layer // 2)]))
        xs, vjp = jax.vjp(lambda p_, xx, layer=layer: _layer_apply(layer, p_, xx, tabs), p, xs)
        vjps.append(vjp)
    loss_local, vjp_loss = jax.vjp(lambda y: mse_loss(y, loss_target[0]), xs)
    (dy,) = vjp_loss(jnp.ones((), F32))
    loss = lax.psum(loss_local, AXES)
    grads = {}
    for layer in (3, 2, 1):
        grads[layer], dy = vjps[layer](dy)

    parts_rest = [to_slots(n, grads[_layer_of(n, j)][n]) for n, j in rest]
    send_sems, recv_sems, srcs, lands, token = _split_start(parts_rest, True, "grad_exchange_rest_start")
    grads[0], grad_x = vjps[0](dy + token[0, 0])
    recv_rest = _split_wait(send_sems, recv_sems, srcs, lands, grad_x, True, "grad_exchange_rest_wait")
    recv = dict(zip(rest, _own_slot(recv_rest, [lax.dynamic_slice_in_dim(p, me, 1, axis=0) for p in parts_rest])))
    repl_grad = lambda n: jnp.stack([grads[_layer_of(n, j)][n] for j in (0, 1)])
    repl_rows = _pad_rows(jnp.concatenate([repl_grad(n).reshape(-1) for n in REPL]), 8)
    parts_last = [to_slots(n, grads[0][n]) for n, j in first]
    parts_last.append(jnp.broadcast_to(repl_rows[None], (N_DEV,) + repl_rows.shape))
    recv_last = _exchange(parts_last, "grad_exchange_last")
    recv.update(zip(first, recv_last[:-1]))

    g_out, d_out, m_out, v_out = {}, {}, {}, {}
    for n in sharded:
        r = jnp.stack([recv[(n, 0)], recv[(n, 1)]], axis=1)
        cols = r.shape[-1]
        flat2 = lambda d: padded(d, n).reshape(-1, cols)
        outs = _sum_adamw(r.reshape(N_DEV, -1, cols), flat2(w), flat2(mom), flat2(vel))
        for dst, o in zip((g_out, d_out, m_out, v_out), outs):
            dst[n] = o.reshape(w[n].shape[:-1] + (cols,))[..., :w[n].shape[-1]]
    pack = lambda d: _pad_rows(jnp.concatenate([d[n].reshape(-1) for n in REPL]), 8)
    outs = _sum_adamw(recv_last[-1], pack(w), pack(mom), pack(vel))
    for dst, o in zip((g_out, d_out, m_out, v_out), outs):
        flat, off = o.reshape(-1), 0
        for n in REPL:
            size = math.prod(w[n].shape)
            dst[n] = flat[off:off + size].reshape(w[n].shape)
            off += size
    return (loss, grad_x[None], *[g_out[n] for n in WEIGHTS], *[d_out[n] for n in WEIGHTS],
            *[m_out[n] for n in WEIGHTS], *[v_out[n] for n in WEIGHTS])
```

```python
import functools
import math

import jax
import jax.numpy as jnp
from jax import lax
from jax.experimental import pallas as pl
from jax.experimental.pallas import tpu as pltpu

F32 = jnp.float32
BF16 = jnp.bfloat16
LANES = 128
HALF = 64
N_DEV = 8
AXES = ("x", "y", "c")
VMEM_LIMIT = 48 * 1024 * 1024

D_MODEL = 1024
DEPTH = 4
ROPE_THETA = 10000.0
MLA_HEADS, MLA_NOPE, MLA_ROPE, MLA_V, MLA_Q_RANK, MLA_KV_RANK = 8, 64, 32, 64, 256, 128
FOX_HEADS, FOX_DIM = 8, 64
SWA_HEADS, SWA_KV_HEADS, SWA_DIM, WINDOW = 16, 2, 64, 128
RMS_EPS, LN_EPS = 1e-6, 1e-5
ALPHA = (2 * DEPTH) ** 0.25
ADAM_LR, ADAM_B1, ADAM_B2, ADAM_EPS, ADAM_WD, ADAM_STEP = 0.001, 0.9, 0.999, 1e-08, 0.01, 10
NEG = -1e30

WEIGHTS = ["even_w_in", "even_q_norm", "even_w_uq", "even_kv_norm", "even_w_ukv", "even_b_f", "even_w_out",
           "even_ln_g", "even_ln_b", "odd_w_in", "odd_sinks", "odd_w_out", "odd_ln_g", "odd_ln_b"]
SHARD_AXIS = {"even_w_in": 2, "even_w_uq": 2, "even_w_ukv": 2, "even_w_out": 1, "odd_w_in": 2, "odd_w_out": 1,
              "odd_ln_g": 1, "odd_ln_b": 1, "even_q_norm": None, "even_kv_norm": None, "even_b_f": None,
              "even_ln_g": None, "even_ln_b": None, "odd_sinks": None}
BIG = ["even_w_in", "even_w_uq", "even_w_ukv", "even_w_out", "odd_w_in", "odd_w_out"]
SMALL_SHARDED = ["odd_ln_g", "odd_ln_b"]
REPL = [n for n in WEIGHTS if SHARD_AXIS[n] is None]


def _pick(n, cands):
    for c in cands:
        if n % c == 0:
            return c
    return n


def _params(**kw):
    return pltpu.CompilerParams(vmem_limit_bytes=VMEM_LIMIT, **kw)


def _me():
    return lax.axis_index("x"), lax.axis_index("y"), lax.axis_index("c")


def _peer(k):
    x, y, c = _me()
    px = 1 - x if (k >> 2) & 1 else x
    py = 1 - y if (k >> 1) & 1 else y
    pc = 1 - c if k & 1 else c
    return px, py, pc


def _lin(p):
    return 4 * p[0] + 2 * p[1] + p[2]


def _comm_call(body, n, out_shape, args, name):
    any_spec = pl.BlockSpec(memory_space=pl.ANY)
    return pl.pallas_call(
        body, name=name, out_shape=out_shape, in_specs=[any_spec] * n, out_specs=[any_spec] * n,
        scratch_shapes=[pltpu.SemaphoreType.DMA((n, N_DEV - 1)), pltpu.SemaphoreType.DMA((n, N_DEV - 1)),
                        pltpu.SemaphoreType.DMA((n,))],
    )(*args)


def _all_gather(xs, name):
    n = len(xs)

    def body(*refs):
        x_refs, out_refs = refs[:n], refs[n:2 * n]
        send_sems, recv_sems, local_sems = refs[2 * n:]
        me = _lin(_me())
        local = [pltpu.make_async_copy(x_refs[a], out_refs[a].at[me], local_sems.at[a]) for a in range(n)]
        for cp in local:
            cp.start()
        sends = []
        for k in range(1, N_DEV):
            for a in range(n):
                cp = pltpu.make_async_remote_copy(
                    src_ref=x_refs[a], dst_ref=out_refs[a].at[me], send_sem=send_sems.at[a, k - 1],
                    recv_sem=recv_sems.at[a, k - 1], device_id=_peer(k), device_id_type=pl.DeviceIdType.MESH)
                cp.start()
                sends.append(cp)
        for k in range(1, N_DEV):
            for a in range(n):
                pltpu.make_async_remote_copy(
                    src_ref=x_refs[a], dst_ref=out_refs[a].at[_lin(_peer(k))], send_sem=send_sems.at[a, k - 1],
                    recv_sem=recv_sems.at[a, k - 1], device_id=_peer(k),
                    device_id_type=pl.DeviceIdType.MESH).wait_recv()
        for cp in sends:
            cp.wait_send()
        for cp in local:
            cp.wait()

    out_shape = [jax.ShapeDtypeStruct((N_DEV,) + x.shape, x.dtype) for x in xs]
    return _comm_call(body, n, out_shape, xs, name)


def _exchange(parts, name):
    n = len(parts)

    def body(*refs):
        p_refs, out_refs = refs[:n], refs[n:2 * n]
        send_sems, recv_sems, local_sems = refs[2 * n:]
        me = _lin(_me())
        local = [pltpu.make_async_copy(p_refs[a].at[me], out_refs[a].at[me], local_sems.at[a]) for a in range(n)]
        for cp in local:
            cp.start()
        sends = []
        for k in range(1, N_DEV):
            peer = _peer(k)
            for a in range(n):
                cp = pltpu.make_async_remote_copy(
                    src_ref=p_refs[a].at[_lin(peer)], dst_ref=out_refs[a].at[me], send_sem=send_sems.at[a, k - 1],
                    recv_sem=recv_sems.at[a, k - 1], device_id=peer, device_id_type=pl.DeviceIdType.MESH)
                cp.start()
                sends.append(cp)
        for k in range(1, N_DEV):
            peer = _peer(k)
            for a in range(n):
                pltpu.make_async_remote_copy(
                    src_ref=p_refs[a].at[_lin(peer)], dst_ref=out_refs[a].at[_lin(peer)],
                    send_sem=send_sems.at[a, k - 1], recv_sem=recv_sems.at[a, k - 1], device_id=peer,
                    device_id_type=pl.DeviceIdType.MESH).wait_recv()
        for cp in sends:
            cp.wait_send()
        for cp in local:
            cp.wait()

    out_shape = [jax.ShapeDtypeStruct(p.shape, p.dtype) for p in parts]
    return _comm_call(body, n, out_shape, parts, name)


_HBM = pl.BlockSpec(memory_space=pltpu.HBM)
_SEM = pl.BlockSpec(memory_space=pltpu.SEMAPHORE)
_EFFECT = pltpu.SideEffectType.DATAFLOW_SIDE_EFFECTING


def _split_start(srcs, slotted, name):
    n = len(srcs)
    lands = [lax.empty(s.shape if slotted else (N_DEV,) + s.shape, s.dtype) for s in srcs]

    def body(*refs):
        src_refs, land_refs = refs[:n], refs[n:2 * n]
        send_sems, recv_sems, token = refs[2 * n], refs[2 * n + 1], refs[-1]
        me = _lin(_me())
        for k in range(1, N_DEV):
            peer = _peer(k)
            for a in range(n):
                pltpu.make_async_remote_copy(
                    src_ref=src_refs[a].at[_lin(peer)] if slotted else src_refs[a], dst_ref=land_refs[a].at[me],
                    send_sem=send_sems.at[a * (N_DEV - 1) + k - 1], recv_sem=recv_sems.at[a * (N_DEV - 1) + k - 1],
                    device_id=peer, device_id_type=pl.DeviceIdType.MESH).start()
        token[...] = jnp.zeros_like(token)

    both = list(srcs) + lands
    outs = pl.pallas_call(
        body, name=name,
        out_shape=(pltpu.SemaphoreType.DMA((n * (N_DEV - 1),)), pltpu.SemaphoreType.DMA((n * (N_DEV - 1),)),
                   *[pltpu.HBM(b.shape, b.dtype) for b in both], jax.ShapeDtypeStruct((8, LANES), F32)),
        in_specs=[_HBM] * (2 * n), out_specs=(_SEM, _SEM, *[_HBM] * (2 * n), pl.BlockSpec(memory_space=pltpu.VMEM)),
        input_output_aliases={a: 2 + a for a in range(2 * n)},
        compiler_params=pltpu.CompilerParams(has_side_effects=_EFFECT),
    )(*[pltpu.with_memory_space_constraint(b, pltpu.HBM) for b in both])
    return outs[0], outs[1], list(outs[2:2 + n]), list(outs[2 + n:2 + 2 * n]), outs[-1]


def _split_wait(send_sems, recv_sems, srcs, lands, after, slotted, name):
    n = len(srcs)

    def body(*refs):
        src_refs, land_refs = refs[:n], refs[n:2 * n]
        send_sems, recv_sems = refs[2 * n], refs[2 * n + 1]
        for k in range(1, N_DEV):
            peer = _peer(k)
            for a in range(n):
                cp = pltpu.make_async_remote_copy(
                    src_ref=src_refs[a].at[_lin(peer)] if slotted else src_refs[a],
                    dst_ref=land_refs[a].at[_lin(peer)], send_sem=send_sems.at[a * (N_DEV - 1) + k - 1],
                    recv_sem=recv_sems.at[a * (N_DEV - 1) + k - 1], device_id=peer,
                    device_id_type=pl.DeviceIdType.MESH)
                cp.wait_send()
                cp.wait_recv()

    both = list(srcs) + list(lands)
    outs = pl.pallas_call(
        body, name=name, out_shape=[pltpu.HBM(b.shape, b.dtype) for b in both],
        in_specs=[_HBM] * (2 * n) + [_SEM, _SEM, pl.BlockSpec(memory_space=pl.ANY)], out_specs=[_HBM] * (2 * n),
        input_output_aliases={a: a for a in range(2 * n)},
        compiler_params=pltpu.CompilerParams(has_side_effects=_EFFECT),
    )(*both, send_sems, recv_sems, after)
    return list(outs[n:])


def _own_slot(lands, own):
    me = _lin(_me())
    slot = lambda l: lax.broadcasted_iota(jnp.int32, (N_DEV,) + (1,) * (l.ndim - 1), 0)
    return [jnp.where(slot(l) == me, o.astype(l.dtype), l) for l, o in zip(lands, own)]


def _sum_adamw(recv, w, m, v):
    _, rows, lanes = recv.shape
    tr = _pick(rows, (256, 128, 64, 32, 16, 8))
    c1 = 1.0 - ADAM_B1 ** ADAM_STEP
    c2 = 1.0 - ADAM_B2 ** ADAM_STEP

    def body(r_ref, w_ref, m_ref, v_ref, g_out, d_out, m_out, v_out):
        g = r_ref[0].astype(F32)
        for s in range(1, N_DEV):
            g = g + r_ref[s].astype(F32)
        mn = ADAM_B1 * m_ref[...] + (1.0 - ADAM_B1) * g
        vn = ADAM_B2 * v_ref[...] + (1.0 - ADAM_B2) * (g * g)
        m_hat = mn / c1
        v_hat = vn / c2
        g_out[...] = g
        d_out[...] = -ADAM_LR * (m_hat / (jnp.sqrt(v_hat) + ADAM_EPS) + ADAM_WD * w_ref[...])
        m_out[...] = mn
        v_out[...] = vn

    blk = pl.BlockSpec((tr, lanes), lambda i: (i, 0))
    shp = jax.ShapeDtypeStruct((rows, lanes), F32)
    return pl.pallas_call(
        body, name=f"sum_adamw_{rows}x{lanes}", grid=(rows // tr,),
        in_specs=[pl.BlockSpec((N_DEV, tr, lanes), lambda i: (0, i, 0)), blk, blk, blk],
        out_specs=[blk, blk, blk, blk], out_shape=[shp, shp, shp, shp],
        compiler_params=_params(dimension_semantics=("parallel",)),
    )(recv, w, m, v)


def _rope_block(xb, cv, s1v, s2v, hs):
    return xb * cv + pltpu.roll(xb, LANES - hs, 1) * s1v + pltpu.roll(xb, hs, 1) * s2v


def _mm_nn(a, b, b_transposed=False, rope=None):
    m, k = a.shape
    n = b.shape[0] if b_transposed else b.shape[1]
    tm = _pick(m, (1024, 512, 256, 128))
    tn = _pick(n, (1024, 640, 512, 256, 128))
    tk = _pick(k, (1024, 640, 512, 256, 128))
    nk = k // tk

    def body(*refs):
        a_ref, b_ref = refs[:2]
        o_ref, acc_ref = refs[-2:]
        kk = pl.program_id(2)

        @pl.when(kk == 0)
        def _():
            acc_ref[...] = jnp.zeros_like(acc_ref)

        dims = (((1,), (1,)), ((), ())) if b_transposed else (((1,), (0,)), ((), ()))
        acc_ref[...] += lax.dot_general(a_ref[...].astype(BF16), b_ref[...].astype(BF16), dims,
                                        preferred_element_type=F32)

        @pl.when(kk == nk - 1)
        def _():
            if rope is None:
                o_ref[...] = acc_ref[...]
            else:
                cv, s1v, s2v = refs[2][...], refs[3][...], refs[4][...]
                for cb in range(tn // LANES):
                    sl = slice(cb * LANES, (cb + 1) * LANES)
                    o_ref[:, sl] = _rope_block(acc_ref[:, sl], cv, s1v, s2v, rope[3])

    b_spec = (pl.BlockSpec((tn, tk), lambda i, j, kk: (j, kk)) if b_transposed
              else pl.BlockSpec((tk, tn), lambda i, j, kk: (kk, j)))
    in_specs, args = [pl.BlockSpec((tm, tk), lambda i, j, kk: (i, kk)), b_spec], [a, b]
    if rope is not None:
        in_specs += [pl.BlockSpec((tm, LANES), lambda i, j, kk: (i, 0))] * 3
        args += list(rope[:3])
    tag = ("t" if b_transposed else "n") + ("" if rope is None else f"_rope{rope[3]}")
    return pl.pallas_call(
        body, name=f"mm_n{tag}_{m}x{k}x{n}", grid=(m // tm, n // tn, nk), in_specs=in_specs,
        out_specs=pl.BlockSpec((tm, tn), lambda i, j, kk: (i, j)),
        out_shape=jax.ShapeDtypeStruct((m, n), F32),
        scratch_shapes=[pltpu.VMEM((tm, tn), F32)],
        compiler_params=_params(dimension_semantics=("parallel", "parallel", "arbitrary")),
    )(*args)


def _mm_tn(a, g):
    s, k = a.shape
    _, n = g.shape
    tm = _pick(k, (1024, 512, 256, 128))
    tn = _pick(n, (1024, 640, 512, 256, 128))
    ts = _pick(s, (512, 256, 128))
    ns = s // ts

    def body(a_ref, g_ref, o_ref, acc_ref):
        ss = pl.program_id(2)

        @pl.when(ss == 0)
        def _():
            acc_ref[...] = jnp.zeros_like(acc_ref)

        acc_ref[...] += lax.dot_general(a_ref[...].astype(BF16), g_ref[...].astype(BF16),
                                        (((0,), (0,)), ((), ())), preferred_element_type=F32)

        @pl.when(ss == ns - 1)
        def _():
            o_ref[...] = acc_ref[...]

    return pl.pallas_call(
        body, name=f"mm_tn_{s}x{k}x{n}", grid=(k // tm, n // tn, ns),
        in_specs=[pl.BlockSpec((ts, tm), lambda i, j, ss: (ss, i)), pl.BlockSpec((ts, tn), lambda i, j, ss: (ss, j))],
        out_specs=pl.BlockSpec((tm, tn), lambda i, j, ss: (i, j)),
        out_shape=jax.ShapeDtypeStruct((k, n), F32),
        scratch_shapes=[pltpu.VMEM((tm, tn), F32)],
        compiler_params=_params(dimension_semantics=("parallel", "parallel", "arbitrary")),
    )(a, g)


@jax.custom_vjp
def mm(a, w):
    return _mm_nn(a, w.astype(BF16))


def _mm_fwd(a, w):
    wb = w.astype(BF16)
    return _mm_nn(a, wb), (a, wb)


def _mm_bwd(res, g):
    a, wb = res
    return _mm_nn(g, wb, b_transposed=True), _mm_tn(a, g)


mm.defvjp(_mm_fwd, _mm_bwd)


def _unrope(g, tabs, hs):
    return _rope_call(g, tabs[0], -tabs[1], -tabs[2], hs)


@functools.partial(jax.custom_vjp, nondiff_argnums=(3,))
def mm_rope(a, w, tabs, hs):
    return _mm_nn(a, w.astype(BF16), rope=(*tabs, hs))


def _mm_rope_fwd(a, w, tabs, hs):
    wb = w.astype(BF16)
    return _mm_nn(a, wb, rope=(*tabs, hs)), (a, wb, tabs)


def _mm_rope_bwd(hs, res, g):
    a, wb, tabs = res
    g = _unrope(g, tabs, hs)
    return _mm_nn(g, wb, b_transposed=True), _mm_tn(a, g), jax.tree.map(jnp.zeros_like, tabs)


mm_rope.defvjp(_mm_rope_fwd, _mm_rope_bwd)


def _make_in_proj(widths, roped, hs):
    cuts = [sum(widths[:i]) for i in range(len(widths) + 1)]

    @jax.custom_vjp
    def in_proj(x, w, tabs):
        return fwd(x, w, tabs)[0]

    def fwd(x, w, tabs):
        xb, wb = x.astype(BF16), w.astype(BF16)
        outs = tuple(_mm_nn(xb, wb[:, a:b], rope=(*tabs, hs) if gi in roped else None)
                     for gi, (a, b) in enumerate(zip(cuts[:-1], cuts[1:])))
        return outs, (xb, wb, tabs)

    def bwd(res, gs):
        xb, wb, tabs = res
        g = jnp.concatenate([_unrope(gg, tabs, hs) if gi in roped else gg for gi, gg in enumerate(gs)], axis=1)
        return _mm_nn(g, wb, b_transposed=True), _mm_tn(xb, g), jax.tree.map(jnp.zeros_like, tabs)

    in_proj.defvjp(fwd, bwd)
    return in_proj


EVEN_GROUPS = (256, 128, 128, 512, 512, 512, 128, 1024)
ODD_GROUPS = (1024, 256, 256, 1024)
even_in_proj = _make_in_proj(EVEN_GROUPS, roped=(2,), hs=MLA_ROPE // 2)
odd_in_proj = _make_in_proj(ODD_GROUPS, roped=(0, 1), hs=SWA_DIM // 2)

SHARD_PAD = 384


def _source_columns(kind):
    if kind == "even":
        src = [list(range(0, 384)), [-1] * 64, list(range(384, 416)), [-1] * 32, list(range(416, 1952)),
               list(range(1952, 1960)), [-1] * 120, list(range(1960, 2984))]
        return sum(src, []), 373
    q0, k0, v0, g0 = 0, 1024, 1152, 1280
    dup = lambda base: [base + 64 * g + c for g in range(SWA_KV_HEADS) for _ in range(2) for c in range(64)]
    return list(range(q0, k0)) + dup(k0) + dup(v0) + list(range(g0, 2304)), 288


def _selection(kind, transposed):
    src, shard = _source_columns(kind)
    cat = [s + (SHARD_PAD - shard) * (s // shard) if s >= 0 else -1 for s in src]
    cat_arr = jnp.asarray(cat, jnp.int32)
    if transposed:
        cols = lax.broadcasted_iota(jnp.int32, (len(src), N_DEV * SHARD_PAD), 1)
        return (cols == cat_arr[:, None]).astype(BF16), [(c, r) for c, r in enumerate(cat) if r >= 0]
    rows = lax.broadcasted_iota(jnp.int32, (N_DEV * SHARD_PAD, len(src)), 0)
    return (rows == cat_arr[None, :]).astype(BF16), [(r, c) for c, r in enumerate(cat) if r >= 0]


def _mm_banded(a, b, nonzeros):
    m, k = a.shape
    _, n = b.shape
    tm = _pick(m, (1024, 512, 256, 128))
    tn = _pick(n, (1024, 640, 512, 256, 128))
    tk = _pick(k, (1024, 640, 512, 256, 128))
    lo, hi = [k // tk] * (n // tn), [-1] * (n // tn)
    for r, c in nonzeros:
        lo[c // tn], hi[c // tn] = min(lo[c // tn], r // tk), max(hi[c // tn], r // tk)
    first = [l if h >= 0 else 0 for l, h in zip(lo, hi)]
    count = [h - l + 1 if h >= 0 else 0 for l, h in zip(lo, hi)]
    steps = max(count)

    def body(first_ref, count_ref, a_ref, b_ref, o_ref, acc_ref):
        j, kk = pl.program_id(1), pl.program_id(2)

        @pl.when(kk == 0)
        def _():
            acc_ref[...] = jnp.zeros_like(acc_ref)

        @pl.when(kk < count_ref[j])
        def _():
            acc_ref[...] += jnp.dot(a_ref[...].astype(BF16), b_ref[...].astype(BF16), preferred_element_type=F32)

        @pl.when(kk == steps - 1)
        def _():
            o_ref[...] = acc_ref[...]

    kblk = lambda j, kk, f, c: jnp.minimum(f[j] + kk, f[j] + jnp.maximum(c[j], 1) - 1)
    return pl.pallas_call(
        body, name=f"mm_banded_{m}x{k}x{n}",
        grid_spec=pltpu.PrefetchScalarGridSpec(
            num_scalar_prefetch=2, grid=(m // tm, n // tn, steps),
            in_specs=[pl.BlockSpec((tm, tk), lambda i, j, kk, f, c: (i, kblk(j, kk, f, c))),
                      pl.BlockSpec((tk, tn), lambda i, j, kk, f, c: (kblk(j, kk, f, c), j))],
            out_specs=pl.BlockSpec((tm, tn), lambda i, j, kk, f, c: (i, j)),
            scratch_shapes=[pltpu.VMEM((tm, tn), F32)]),
        out_shape=jax.ShapeDtypeStruct((m, n), F32),
        compiler_params=_params(dimension_semantics=("parallel", "parallel", "arbitrary")),
    )(jnp.asarray(first, jnp.int32), jnp.asarray(count, jnp.int32), a, b)


@functools.partial(jax.custom_vjp, nondiff_argnums=(1,))
def relayout(wcat, kind):
    return _mm_banded(wcat, *_selection(kind, False))


def _relayout_bwd(kind, _, g):
    return (_mm_banded(g, *_selection(kind, True)),)


relayout.defvjp(lambda wcat, kind: (_mm_banded(wcat, *_selection(kind, False)), None), _relayout_bwd)


def _row_block(s):
    return _pick(s, (512, 256, 128, 64, 32, 16, 8))


def _rms_fwd_call(x, g):
    s, k = x.shape
    tr = _row_block(s)

    def body(x_ref, g_ref, o_ref):
        xv = x_ref[...]
        r = lax.rsqrt(jnp.mean(xv * xv, axis=-1, keepdims=True) + RMS_EPS)
        o_ref[...] = xv * r * g_ref[...]

    return pl.pallas_call(
        body, name=f"rms_fwd_{k}", grid=(s // tr,),
        in_specs=[pl.BlockSpec((tr, k), lambda i: (i, 0)), pl.BlockSpec((1, k), lambda i: (0, 0))],
        out_specs=pl.BlockSpec((tr, k), lambda i: (i, 0)), out_shape=jax.ShapeDtypeStruct((s, k), F32),
        compiler_params=_params(dimension_semantics=("parallel",)),
    )(x, g.reshape(1, k))


def _rms_bwd_call(x, g, dy):
    s, k = x.shape
    tr = _row_block(s)

    def body(x_ref, g_ref, dy_ref, dx_ref, dg_ref):
        @pl.when(pl.program_id(0) == 0)
        def _():
            dg_ref[...] = jnp.zeros_like(dg_ref)

        xv = x_ref[...]
        r = lax.rsqrt(jnp.mean(xv * xv, axis=-1, keepdims=True) + RMS_EPS)
        xh = xv * r
        dyv = dy_ref[...]
        dg_ref[...] += jnp.sum(dyv * xh, axis=0, keepdims=True)
        dxh = dyv * g_ref[...]
        dx_ref[...] = r * (dxh - xh * jnp.mean(dxh * xh, axis=-1, keepdims=True))

    dx, dg = pl.pallas_call(
        body, name=f"rms_bwd_{k}", grid=(s // tr,),
        in_specs=[pl.BlockSpec((tr, k), lambda i: (i, 0)), pl.BlockSpec((1, k), lambda i: (0, 0)),
                  pl.BlockSpec((tr, k), lambda i: (i, 0))],
        out_specs=[pl.BlockSpec((tr, k), lambda i: (i, 0)), pl.BlockSpec((1, k), lambda i: (0, 0))],
        out_shape=[jax.ShapeDtypeStruct((s, k), F32), jax.ShapeDtypeStruct((1, k), F32)],
        compiler_params=_params(dimension_semantics=("arbitrary",)),
    )(x, g.reshape(1, k), dy)
    return dx, dg.reshape(k)


@jax.custom_vjp
def rms_norm(x, g):
    return _rms_fwd_call(x, g)


rms_norm.defvjp(lambda x, g: (_rms_fwd_call(x, g), (x, g)), lambda res, dy: _rms_bwd_call(res[0], res[1], dy))


def _ln_fwd_call(x, y, g, b):
    s, k = x.shape
    tr = _row_block(s)

    def body(x_ref, y_ref, g_ref, b_ref, o_ref):
        u = ALPHA * x_ref[...] + y_ref[...]
        mu = jnp.mean(u, axis=-1, keepdims=True)
        d = u - mu
        var = jnp.mean(d * d, axis=-1, keepdims=True)
        o_ref[...] = d * lax.rsqrt(var + LN_EPS) * g_ref[...] + b_ref[...]

    row = pl.BlockSpec((tr, k), lambda i: (i, 0))
    vec = pl.BlockSpec((1, k), lambda i: (0, 0))
    return pl.pallas_call(
        body, name="ln_fwd", grid=(s // tr,), in_specs=[row, row, vec, vec], out_specs=row,
        out_shape=jax.ShapeDtypeStruct((s, k), F32), compiler_params=_params(dimension_semantics=("parallel",)),
    )(x, y, g.reshape(1, k), b.reshape(1, k))


def _ln_bwd_call(x, y, g, do):
    s, k = x.shape
    tr = _row_block(s)

    def body(x_ref, y_ref, g_ref, do_ref, dx_ref, dy_ref, dg_ref, db_ref):
        @pl.when(pl.program_id(0) == 0)
        def _():
            dg_ref[...] = jnp.zeros_like(dg_ref)
            db_ref[...] = jnp.zeros_like(db_ref)

        u = ALPHA * x_ref[...] + y_ref[...]
        mu = jnp.mean(u, axis=-1, keepdims=True)
        d = u - mu
        r = lax.rsqrt(jnp.mean(d * d, axis=-1, keepdims=True) + LN_EPS)
        xh = d * r
        dov = do_ref[...]
        dg_ref[...] += jnp.sum(dov * xh, axis=0, keepdims=True)
        db_ref[...] += jnp.sum(dov, axis=0, keepdims=True)
        dxh = dov * g_ref[...]
        du = r * (dxh - jnp.mean(dxh, axis=-1, keepdims=True) - xh * jnp.mean(dxh * xh, axis=-1, keepdims=True))
        dy_ref[...] = du
        dx_ref[...] = ALPHA * du

    row = pl.BlockSpec((tr, k), lambda i: (i, 0))
    vec = pl.BlockSpec((1, k), lambda i: (0, 0))
    dx, dy, dg, db = pl.pallas_call(
        body, name="ln_bwd", grid=(s // tr,), in_specs=[row, row, vec, row], out_specs=[row, row, vec, vec],
        out_shape=[jax.ShapeDtypeStruct((s, k), F32), jax.ShapeDtypeStruct((s, k), F32),
                   jax.ShapeDtypeStruct((1, k), F32), jax.ShapeDtypeStruct((1, k), F32)],
        compiler_params=_params(dimension_semantics=("arbitrary",)),
    )(x, y, g.reshape(1, k), do)
    return dx, dy, dg.reshape(k), db.reshape(k)


@jax.custom_vjp
def ln_res(x, y, g, b):
    return _ln_fwd_call(x, y, g, b)


ln_res.defvjp(lambda x, y, g, b: (_ln_fwd_call(x, y, g, b), (x, y, g)),
              lambda res, do: _ln_bwd_call(res[0], res[1], res[2], do))


def _rope_call(x, c, s1, s2, hs):
    s, w = x.shape
    tr = _row_block(s)
    nb = w // LANES

    def body(x_ref, c_ref, s1_ref, s2_ref, o_ref):
        cv, s1v, s2v = c_ref[...], s1_ref[...], s2_ref[...]
        for cb in range(nb):
            xb = x_ref[:, cb * LANES:(cb + 1) * LANES]
            o_ref[:, cb * LANES:(cb + 1) * LANES] = (
                xb * cv + pltpu.roll(xb, LANES - hs, 1) * s1v + pltpu.roll(xb, hs, 1) * s2v)

    row = pl.BlockSpec((tr, w), lambda i: (i, 0))
    tab = pl.BlockSpec((tr, LANES), lambda i: (i, 0))
    return pl.pallas_call(
        body, name=f"rope_{w}_{hs}", grid=(s // tr,), in_specs=[row, tab, tab, tab], out_specs=row,
        out_shape=jax.ShapeDtypeStruct((s, w), F32), compiler_params=_params(dimension_semantics=("parallel",)),
    )(x, c, s1, s2)


def _rope_tables(s, layout):
    pos = jnp.arange(s, dtype=F32)[:, None]
    lane = jnp.arange(LANES)
    if layout == "mla":
        dim, hs = MLA_ROPE, MLA_ROPE // 2
        r = lane - MLA_NOPE
        active = (r >= 0) & (r < MLA_ROPE)
    else:
        dim, hs = SWA_DIM, SWA_DIM // 2
        r = lane % SWA_DIM
        active = jnp.ones_like(lane, dtype=bool)
    f = jnp.where(active, r % hs, 0)
    inv = ROPE_THETA ** (-(2.0 * f.astype(F32)) / dim)
    ang = pos * inv[None, :]
    cos, sin = jnp.cos(ang), jnp.sin(ang)
    first = (active & (r < hs))[None, :]
    second = (active & (r >= hs))[None, :]
    c = jnp.where(active[None, :], cos, 1.0)
    s1 = jnp.where(first, -sin, 0.0)
    s2 = jnp.where(second, sin, 0.0)
    return (c, s1, s2), hs


def _gated(o_refs, g_ref):
    gv = g_ref[...]
    o = o_refs[0][...] if len(o_refs) == 1 else jnp.concatenate([r[...] for r in o_refs], axis=1)
    return (o * (gv * jax.nn.sigmoid(gv))).astype(BF16)


def _gated_mm_call(o_parts, gate, wb):
    s, k = gate.shape
    n = wb.shape[1]
    tm = _pick(s, (1024, 512, 256, 128))
    tn = _pick(n, (1024, 512, 256, 128))
    widths = [o.shape[1] for o in o_parts]
    no = len(widths)

    def body(*refs):
        refs[-1][...] = jnp.dot(_gated(refs[:no], refs[no]), refs[no + 1][...], preferred_element_type=F32)

    specs = [pl.BlockSpec((tm, wd), lambda i, j: (i, 0)) for wd in widths]
    return pl.pallas_call(
        body, name=f"gated_mm_{no}", grid=(s // tm, n // tn),
        in_specs=specs + [pl.BlockSpec((tm, k), lambda i, j: (i, 0)), pl.BlockSpec((k, tn), lambda i, j: (0, j))],
        out_specs=pl.BlockSpec((tm, tn), lambda i, j: (i, j)), out_shape=jax.ShapeDtypeStruct((s, n), F32),
        compiler_params=_params(dimension_semantics=("parallel", "parallel")),
    )(*o_parts, gate, wb)


def _gated_mm_tn_call(o_parts, gate, g):
    s, k = gate.shape
    n = g.shape[1]
    ts = _pick(s, (512, 256, 128))
    ns = s // ts
    widths = [o.shape[1] for o in o_parts]
    no = len(widths)

    def body(*refs):
        g_ref, o_ref, acc_ref = refs[no + 1], refs[no + 2], refs[no + 3]
        ss = pl.program_id(0)

        @pl.when(ss == 0)
        def _():
            acc_ref[...] = jnp.zeros_like(acc_ref)

        acc_ref[...] += lax.dot_general(_gated(refs[:no], refs[no]), g_ref[...].astype(BF16),
                                        (((0,), (0,)), ((), ())), preferred_element_type=F32)

        @pl.when(ss == ns - 1)
        def _():
            o_ref[...] = acc_ref[...]

    specs = [pl.BlockSpec((ts, wd), lambda ss: (ss, 0)) for wd in widths]
    return pl.pallas_call(
        body, name=f"gated_mm_tn_{no}", grid=(ns,),
        in_specs=specs + [pl.BlockSpec((ts, k), lambda ss: (ss, 0)), pl.BlockSpec((ts, n), lambda ss: (ss, 0))],
        out_specs=pl.BlockSpec((k, n), lambda ss: (0, 0)), out_shape=jax.ShapeDtypeStruct((k, n), F32),
        scratch_shapes=[pltpu.VMEM((k, n), F32)], compiler_params=_params(dimension_semantics=("arbitrary",)),
    )(*o_parts, gate, g)


def _gate_bwd_call(o_parts, gate, dz):
    s, w = gate.shape
    tr = _row_block(s)
    widths = [o.shape[1] for o in o_parts]
    n = len(widths)

    def body(*refs):
        o_refs, g_ref, dz_ref = refs[:n], refs[n], refs[n + 1]
        do_refs, dg_ref = refs[n + 2:2 * n + 2], refs[2 * n + 2]
        off = 0
        for o_ref, do_ref, wd in zip(o_refs, do_refs, widths):
            gv = g_ref[:, off:off + wd]
            sg = jax.nn.sigmoid(gv)
            dzv = dz_ref[:, off:off + wd]
            do_ref[...] = dzv * (gv * sg)
            dg_ref[:, off:off + wd] = dzv * o_ref[...] * (sg * (1.0 + gv * (1.0 - sg)))
            off += wd

    specs = [pl.BlockSpec((tr, wd), lambda i: (i, 0)) for wd in widths]
    row = pl.BlockSpec((tr, w), lambda i: (i, 0))
    outs = pl.pallas_call(
        body, name=f"gate_bwd_{n}", grid=(s // tr,), in_specs=specs + [row, row], out_specs=specs + [row],
        out_shape=[jax.ShapeDtypeStruct((s, wd), F32) for wd in widths] + [jax.ShapeDtypeStruct((s, w), F32)],
        compiler_params=_params(dimension_semantics=("parallel",)),
    )(*o_parts, gate, dz)
    return tuple(outs[:n]), outs[n]


@jax.custom_vjp
def gated_mm(o_parts, gate, w):
    return _gated_mm_call(o_parts, gate, w.astype(BF16))


def _gated_mm_fwd(o_parts, gate, w):
    wb = w.astype(BF16)
    return _gated_mm_call(o_parts, gate, wb), (o_parts, gate, wb)


def _gated_mm_bwd(res, g):
    o_parts, gate, wb = res
    do_parts, dgate = _gate_bwd_call(o_parts, gate, _mm_nn(g, wb, b_transposed=True))
    return do_parts, dgate, _gated_mm_tn_call(o_parts, gate, g)


gated_mm.defvjp(_gated_mm_fwd, _gated_mm_bwd)


def _loss_call(y, t):
    s, k = y.shape
    tr = _row_block(s)
    nsteps = s // tr

    def body(y_ref, t_ref, l_ref, dy_ref, acc_ref):
        i = pl.program_id(0)

        @pl.when(i == 0)
        def _():
            acc_ref[...] = jnp.zeros_like(acc_ref)

        d = y_ref[...] - t_ref[...]
        dy_ref[...] = d / k
        acc_ref[...] += jnp.sum(d * d, axis=0, keepdims=True)

        @pl.when(i == nsteps - 1)
        def _():
            tot = jnp.sum(acc_ref[...], axis=1, keepdims=True) * (0.5 / k)
            l_ref[...] = jnp.broadcast_to(tot, l_ref.shape)

    row = pl.BlockSpec((tr, k), lambda i: (i, 0))
    return pl.pallas_call(
        body, name="loss", grid=(nsteps,), in_specs=[row, row],
        out_specs=[pl.BlockSpec((1, LANES), lambda i: (0, 0)), row],
        out_shape=[jax.ShapeDtypeStruct((1, LANES), F32), jax.ShapeDtypeStruct((s, k), F32)],
        scratch_shapes=[pltpu.VMEM((1, k), F32)], compiler_params=_params(dimension_semantics=("arbitrary",)),
    )(y, t)


@jax.custom_vjp
def mse_loss(y, t):
    return _loss_call(y, t)[0][0, 0]


def _mse_fwd(y, t):
    l, dy = _loss_call(y, t)
    return l[0, 0], (dy, t)


mse_loss.defvjp(_mse_fwd, lambda res, g: (g * res[0], jnp.zeros_like(res[1])))


def _scan_call(x, b, mode):
    s, w = x.shape
    nt = s // 8

    def tile_scan(t):
        row = lax.broadcasted_iota(jnp.int32, (8, w), 0)
        for sh in (1, 2, 4):
            t = t + jnp.where(row >= sh, pltpu.roll(t, sh, 0), 0.0)
        return t

    def body(x_ref, b_ref, o_ref):
        def step(i, carry):
            rows = pl.ds(pl.multiple_of(i * 8, 8), 8)
            t = x_ref[rows, :]
            if mode == "fwd":
                t = jax.nn.log_sigmoid(t + b_ref[...])
            t = tile_scan(t) + carry
            o_ref[rows, :] = t
            return t[7:8, :]

        total = lax.fori_loop(0, nt, step, jnp.zeros((1, w), F32))
        if mode == "rev":
            def fix(i, c):
                rows = pl.ds(pl.multiple_of(i * 8, 8), 8)
                o_ref[rows, :] = total - o_ref[rows, :] + x_ref[rows, :]
                return c
            lax.fori_loop(0, nt, fix, 0)

    full = pl.BlockSpec((s, w), lambda: (0, 0))
    return pl.pallas_call(
        body, name=f"scan_{mode}", in_specs=[full, pl.BlockSpec((1, w), lambda: (0, 0))], out_specs=full,
        out_shape=jax.ShapeDtypeStruct((s, w), F32), compiler_params=_params(),
    )(x, b)


def _fox_dlogit_call(x, b, dlogf):
    s, w = x.shape
    tr = _row_block(s)

    def body(x_ref, b_ref, d_ref, dx_ref, db_ref):
        @pl.when(pl.program_id(0) == 0)
        def _():
            db_ref[...] = jnp.zeros_like(db_ref)

        dx = d_ref[...] * jax.nn.sigmoid(-(x_ref[...] + b_ref[...]))
        dx_ref[...] = dx
        db_ref[...] += jnp.sum(dx, axis=0, keepdims=True)

    row = pl.BlockSpec((tr, w), lambda i: (i, 0))
    vec = pl.BlockSpec((1, w), lambda i: (0, 0))
    return pl.pallas_call(
        body, name="fox_dlogit", grid=(s // tr,), in_specs=[row, vec, row], out_specs=[row, vec],
        out_shape=[jax.ShapeDtypeStruct((s, w), F32), jax.ShapeDtypeStruct((1, w), F32)],
        compiler_params=_params(dimension_semantics=("arbitrary",)),
    )(x, b, dlogf)


@jax.custom_vjp
def fox_cum(fl, b):
    return _scan_call(fl, b, "fwd")


def _fox_cum_bwd(res, dcum):
    fl, b = res
    dlogf = _scan_call(dcum, b, "rev")
    return _fox_dlogit_call(fl, b, dlogf)


fox_cum.defvjp(lambda fl, b: (_scan_call(fl, b, "fwd"), (fl, b)), _fox_cum_bwd)


def _lane_col(x, lane_idx):
    lane = lax.broadcasted_iota(jnp.int32, (1, x.shape[1]), 1)
    return jnp.sum(jnp.where(lane == lane_idx, x, 0.0), axis=1, keepdims=True)


def _row_of(x, row_idx):
    row = lax.broadcasted_iota(jnp.int32, (x.shape[0], 1), 0)
    return jnp.sum(jnp.where(row == row_idx, x, 0.0), axis=0, keepdims=True)


def _attn_cfg(mode, s):
    if mode == "swa":
        blk = 256 if s >= 2048 else 128
        return dict(blk=blk, n_outer=SWA_KV_HEADS, pps=4, wide=False, scale=SWA_DIM ** -0.5)
    blk = 512 if s >= 2048 else 128
    if mode == "mla":
        return dict(blk=blk, n_outer=4, pps=1, wide=True, scale=(MLA_NOPE + MLA_ROPE) ** -0.5)
    return dict(blk=blk, n_outer=4, pps=1, wide=False, scale=FOX_DIM ** -0.5)


ROW_CHUNK = 32


def _unrolled(n, body, carry):
    for c in range(n):
        carry = body(c, carry)
    return carry


def _valid_rows(mode, i, jb, blk, r0, rc):
    qpos = i * blk + r0 + lax.broadcasted_iota(jnp.int32, (rc, blk), 0)
    kpos = jb * blk + lax.broadcasted_iota(jnp.int32, (rc, blk), 1)
    ok = kpos <= qpos
    if mode == "swa":
        ok = ok & (qpos - kpos < WINDOW)
    return ok


def _attn_fwd_call(mode, q, k, v, extra):
    s = q.shape[0]
    cfg = _attn_cfg(mode, s)
    blk, n_outer, pps, wide, scale = cfg["blk"], cfg["n_outer"], cfg["pps"], cfg["wide"], cfg["scale"]
    rc = ROW_CHUNK
    nq = s // blk
    swa, fox = mode == "swa", mode == "fox"
    qw = (2 * LANES if wide else LANES) * pps
    kw = 2 * LANES if wide else LANES
    ow = LANES * pps
    reps = blk // LANES

    def body(*refs):
        if not swa:
            it_ref, jt_ref = refs[:2]
            refs = refs[2:]
        q_ref, k_ref, v_ref = refs[:3]
        n_in = 3
        if fox:
            cum_ref, cumt_ref = refs[3:5]
            n_in = 5
        if swa:
            sink_ref = refs[3]
            n_in = 4
        o_ref, lse_ref, m_ref, l_ref, acc_ref, a_ref, s_all, p_all, c_all = refs[n_in:]
        p_id = pl.program_id(0)
        if swa:
            i, j = pl.program_id(1), pl.program_id(2)
            jb, run, first, last = i - 1 + j, (i - 1 + j) >= 0, j == 0, j == 1
        else:
            i, j = it_ref[pl.program_id(1)], jt_ref[pl.program_id(1)]
            jb, first, last = j, j == 0, j == i
        lane = lax.broadcasted_iota(jnp.int32, (1, LANES), 1)
        msk = [lane < HALF, lane >= HALF]

        @pl.when(first)
        def _():
            for hh in range(2 * pps):
                if swa:
                    m_ref[hh] = jnp.broadcast_to(sink_ref[hh:hh + 1, :], (blk, LANES))
                    l_ref[hh] = jnp.ones((blk, LANES), F32)
                else:
                    m_ref[hh] = jnp.full((blk, LANES), NEG, F32)
                    l_ref[hh] = jnp.zeros((blk, LANES), F32)
            acc_ref[...] = jnp.zeros_like(acc_ref)

        def process(masked):
            for pp in range(pps):
                vb = v_ref[...]
                pvs = []
                for h in range(2):
                    hh = 2 * pp + h
                    s_ref, p_ref, c_ref = s_all.at[hh], p_all.at[hh], c_all.at[hh]
                    if wide:
                        qh = q_ref[:, h * LANES:(h + 1) * LANES] * scale
                        kh = k_ref[:, h * LANES:(h + 1) * LANES]
                    else:
                        qh = jnp.where(msk[h], q_ref[:, pp * LANES:(pp + 1) * LANES], 0.0) * scale
                        kh = k_ref[...]
                    s_ref[...] = lax.dot_general(qh.astype(BF16), kh.astype(BF16), (((1,), (1,)), ((), ())),
                                                 preferred_element_type=F32)
                    if fox:
                        head = 2 * p_id + h
                        c_ref[...] = jnp.broadcast_to(_lane_col(cum_ref[...], head), (blk, LANES))
                        ck = _row_of(cumt_ref[...], head)

                    def chunk(c, carry, hh=hh, h=h):
                        r0 = c * rc
                        rows = pl.ds(r0, rc)
                        u = s_ref[rows, :]
                        if fox:
                            u = u - ck
                        if masked:
                            u = jnp.where(_valid_rows(mode, i, jb, blk, r0, rc), u, NEG)
                        m_prev, l_prev = m_ref[hh, rows, :], l_ref[hh, rows, :]
                        m_cur = jnp.max(u, axis=1, keepdims=True)
                        if fox:
                            m_cur = m_cur + c_ref[rows, :]
                        m_next = jnp.maximum(m_prev, m_cur)
                        shift = m_next - c_ref[rows, :] if fox else m_next
                        p = jnp.exp(u - jnp.tile(shift, (1, reps)))
                        alpha = jnp.exp(m_prev - m_next)
                        l_ref[hh, rows, :] = alpha * l_prev + jnp.sum(p, axis=1, keepdims=True)
                        m_ref[hh, rows, :] = m_next
                        a_ref[hh, rows, :] = alpha
                        p_ref[rows, :] = p.astype(BF16)
                        return carry

                    _unrolled(blk // rc, chunk, 0)
                    vh = jnp.where(msk[h], vb, 0.0).astype(BF16)
                    pvs.append(jnp.dot(p_ref[...], vh, preferred_element_type=F32))
                acc_ref[pp] = acc_ref[pp] * jnp.where(msk[0], a_ref[2 * pp], a_ref[2 * pp + 1]) + pvs[0] + pvs[1]

        if swa:
            pl.when(run)(lambda: process(True))
        else:
            pl.when(j < i)(lambda: process(False))
            pl.when(j == i)(lambda: process(True))

        @pl.when(last)
        def _():
            for pp in range(pps):
                l0, l1 = l_ref[2 * pp], l_ref[2 * pp + 1]
                o_ref[:, pp * LANES:(pp + 1) * LANES] = acc_ref[pp] / jnp.where(msk[0], l0, l1)
                lse_ref[:, pp * LANES:(pp + 1) * LANES] = jnp.where(
                    msk[0], m_ref[2 * pp] + jnp.log(l0), m_ref[2 * pp + 1] + jnp.log(l1))

    if swa:
        kv_map = lambda g, i, j: (jnp.maximum(i - 1 + j, 0), g)
        q_map = lambda g, i, j: (i, g)
        grid, tables, sem = (n_outer, nq, 2), [], ("parallel", "parallel", "arbitrary")
    else:
        tri = [(i, j) for i in range(nq) for j in range(i + 1)]
        tables = [jnp.asarray([t[0] for t in tri], jnp.int32), jnp.asarray([t[1] for t in tri], jnp.int32)]
        kv_map = lambda p, t, it, jt: (jt[t], p)
        q_map = lambda p, t, it, jt: (it[t], p)
        grid, sem = (n_outer, len(tri)), ("parallel", "arbitrary")
    in_specs = [pl.BlockSpec((blk, qw), q_map), pl.BlockSpec((blk, kw), kv_map), pl.BlockSpec((blk, LANES), kv_map)]
    args = [q, k, v]
    if fox:
        cum, cumt = extra
        in_specs += [pl.BlockSpec((blk, LANES), lambda p, t, it, jt: (it[t], 0)),
                     pl.BlockSpec((8, blk), lambda p, t, it, jt: (0, jt[t]))]
        args += [cum, cumt]
    if swa:
        in_specs += [pl.BlockSpec((8, LANES), lambda g, i, j: (g, 0))]
        args += [extra]
    n_pairs = n_outer * pps
    return pl.pallas_call(
        body, name=f"attn_fwd_{mode}",
        grid_spec=pltpu.PrefetchScalarGridSpec(
            num_scalar_prefetch=len(tables), grid=grid, in_specs=in_specs,
            out_specs=[pl.BlockSpec((blk, ow), q_map), pl.BlockSpec((blk, ow), q_map)],
            scratch_shapes=[pltpu.VMEM((2 * pps, blk, LANES), F32), pltpu.VMEM((2 * pps, blk, LANES), F32),
                            pltpu.VMEM((pps, blk, LANES), F32), pltpu.VMEM((2 * pps, blk, LANES), F32),
                            pltpu.VMEM((2 * pps, blk, blk), F32), pltpu.VMEM((2 * pps, blk, blk), BF16),
                            pltpu.VMEM((2 * pps, blk, LANES), F32)]),
        out_shape=[jax.ShapeDtypeStruct((s, n_pairs * LANES), F32), jax.ShapeDtypeStruct((s, n_pairs * LANES), F32)],
        compiler_params=_params(dimension_semantics=sem),
    )(*tables, *args)


def _attn_bwd_call(mode, q, k, v, extra, lse, o, do):
    s = q.shape[0]
    cfg = _attn_cfg(mode, s)
    blk, n_outer, pps, wide, scale = cfg["blk"], cfg["n_outer"], cfg["pps"], cfg["wide"], cfg["scale"]
    rc = ROW_CHUNK
    nq = s // blk
    swa, fox = mode == "swa", mode == "fox"
    qw = (2 * LANES if wide else LANES) * pps
    kw = 2 * LANES if wide else LANES
    ow = LANES * pps
    reps = blk // LANES

    assert not swa

    def body(*refs):
        jt_ref, it_ref = refs[:2]
        refs = refs[2:]
        q_ref, k_ref, v_ref, lse_ref, o_ref, do_ref = refs[:6]
        n_in = 6
        if fox:
            cum_ref, cumt_ref = refs[6:8]
            n_in = 8
        dq_ref, dk_ref, dv_ref = refs[n_in:n_in + 3]
        n_out = n_in + 3
        if fox:
            dck_ref, dcq_ref = refs[n_out:n_out + 2]
            n_out += 2
        dk_acc, dv_acc, s_all, dp_all, p_all, ds_all, e_all, d_all = refs[n_out:n_out + 8]
        if fox:
            dck_acc, rs_all = refs[n_out + 8:n_out + 10]
        p_id, t = pl.program_id(0), pl.program_id(1)
        j, ii = jt_ref[t], it_ref[t]
        i, first_i, last_i = ii, ii == j, ii == nq - 1
        lane = lax.broadcasted_iota(jnp.int32, (1, LANES), 1)
        msk = [lane < HALF, lane >= HALF]

        @pl.when(t == 0)
        def _():
            dq_ref[...] = jnp.zeros_like(dq_ref)
            if swa:
                dsink_ref[...] = jnp.zeros_like(dsink_ref)
            if fox:
                dcq_ref[...] = jnp.zeros_like(dcq_ref)

        @pl.when(first_i)
        def _():
            dk_acc[...] = jnp.zeros_like(dk_acc)
            dv_acc[...] = jnp.zeros_like(dv_acc)
            if fox:
                dck_acc[...] = jnp.zeros_like(dck_acc)

        def process(masked):
            rows = pl.ds(pl.multiple_of(i * blk, blk), blk)
            vb = v_ref[...].astype(BF16)
            dv_parts, dk_parts = [], []
            for pp in range(pps):
                psl = slice(pp * LANES, (pp + 1) * LANES)
                lse_blk, do_blk = lse_ref[:, psl], do_ref[:, psl]
                doo = do_blk * o_ref[:, psl]
                dq_pair = []
                for h in range(2):
                    hh = 2 * pp + h
                    s_ref, dp_ref, p_ref, ds_ref = s_all.at[hh], dp_all.at[hh], p_all.at[hh], ds_all.at[hh]
                    e_ref, d_ref = e_all.at[hh], d_all.at[hh]
                    if fox:
                        rs_ref = rs_all.at[hh]
                    if wide:
                        hsl = slice(h * LANES, (h + 1) * LANES)
                        qh = (q_ref[:, hsl] * scale).astype(BF16)
                        kh = k_ref[:, hsl].astype(BF16)
                    else:
                        qh = (jnp.where(msk[h], q_ref[:, psl], 0.0) * scale).astype(BF16)
                        kh = k_ref[...].astype(BF16)
                    s_ref[...] = lax.dot_general(qh, kh, (((1,), (1,)), ((), ())), preferred_element_type=F32)
                    do_h = jnp.where(msk[h], do_blk, 0.0).astype(BF16)
                    dp_ref[...] = lax.dot_general(do_h, vb, (((1,), (1,)), ((), ())), preferred_element_type=F32)
                    lse_h = _lane_col(lse_blk, HALF * h)
                    d_h = jnp.sum(jnp.where(msk[h], doo, 0.0), axis=1, keepdims=True)
                    e_ref[...] = jnp.broadcast_to(lse_h, (blk, LANES))
                    d_ref[...] = jnp.broadcast_to(d_h, (blk, LANES))
                    if fox:
                        head = 2 * p_id + h
                        e_ref[...] = e_ref[...] - jnp.broadcast_to(_lane_col(cum_ref[...], head), (blk, LANES))
                        ck = _row_of(cumt_ref[...], head)

                    def chunk(c, colsum):
                        r0 = c * rc
                        cr = pl.ds(r0, rc)
                        u = s_ref[cr, :]
                        if fox:
                            u = u - ck
                        p = jnp.exp(u - jnp.tile(e_ref[cr, :], (1, reps)))
                        if masked:
                            p = jnp.where(_valid_rows(mode, i, j, blk, r0, rc), p, 0.0)
                        ds = p * (dp_ref[cr, :] - jnp.tile(d_ref[cr, :], (1, reps)))
                        p_ref[cr, :] = p.astype(BF16)
                        ds_ref[cr, :] = ds.astype(BF16)
                        if fox:
                            colsum = colsum + jnp.sum(ds, axis=0, keepdims=True)
                            rs_ref[cr, :] = jnp.broadcast_to(jnp.sum(ds, axis=1, keepdims=True), (rc, LANES))
                        return colsum

                    colsum = _unrolled(blk // rc, chunk, jnp.zeros((1, blk), F32))
                    dv_parts.append(lax.dot_general(p_ref[...], do_h, (((0,), (0,)), ((), ())),
                                                    preferred_element_type=F32))
                    if fox:
                        dck_acc[h:h + 1, :] += -colsum
                        dcq_ref[rows, :] += jnp.where(msk[h], rs_ref[...], 0.0)
                    dq_h = jnp.dot(ds_ref[...], kh, preferred_element_type=F32) * scale
                    dk_h = lax.dot_general(ds_ref[...], qh, (((0,), (0,)), ((), ())), preferred_element_type=F32)
                    if wide:
                        dq_ref[rows, hsl] += dq_h
                        dk_acc[:, hsl] += dk_h
                    else:
                        dq_pair.append(jnp.where(msk[h], dq_h, 0.0))
                        dk_parts.append(dk_h)
                if not wide:
                    dq_ref[rows, psl] += dq_pair[0] + dq_pair[1]
            dv_acc[...] += functools.reduce(lambda a, b: a + b, dv_parts)
            if not wide:
                dk_acc[...] += functools.reduce(lambda a, b: a + b, dk_parts)

        pl.when(ii > j)(lambda: process(False))
        pl.when(ii == j)(lambda: process(True))

        @pl.when(last_i)
        def _():
            dk_ref[...] = dk_acc[...]
            dv_ref[...] = dv_acc[...]
            if fox:
                dck_ref[0] = dck_acc[...]

    tri = [(j, i) for j in range(nq) for i in range(j, nq)]
    tables = [jnp.asarray([t[0] for t in tri], jnp.int32), jnp.asarray([t[1] for t in tri], jnp.int32)]
    q_map = lambda p, t, jt, it: (it[t], p)
    kv_map = lambda p, t, jt, it: (jt[t], p)
    in_specs = [pl.BlockSpec((blk, qw), q_map), pl.BlockSpec((blk, kw), kv_map), pl.BlockSpec((blk, LANES), kv_map),
                pl.BlockSpec((blk, ow), q_map), pl.BlockSpec((blk, ow), q_map), pl.BlockSpec((blk, ow), q_map)]
    args = [q, k, v, lse, o, do]
    n_pairs = n_outer * pps
    out_specs = [pl.BlockSpec((s, qw), lambda p, t, jt, it: (0, p)), pl.BlockSpec((blk, kw), kv_map),
                 pl.BlockSpec((blk, LANES), kv_map)]
    out_shape = [jax.ShapeDtypeStruct((s, q.shape[1]), F32), jax.ShapeDtypeStruct((s, k.shape[1]), F32),
                 jax.ShapeDtypeStruct((s, v.shape[1]), F32)]
    nh = 2 * pps
    scratch = [pltpu.VMEM((blk, kw), F32), pltpu.VMEM((blk, LANES), F32), pltpu.VMEM((nh, blk, blk), F32),
               pltpu.VMEM((nh, blk, blk), F32), pltpu.VMEM((nh, blk, blk), BF16), pltpu.VMEM((nh, blk, blk), BF16),
               pltpu.VMEM((nh, blk, LANES), F32), pltpu.VMEM((nh, blk, LANES), F32)]
    if fox:
        cum, cumt = extra
        in_specs += [pl.BlockSpec((blk, LANES), lambda p, t, jt, it: (it[t], 0)),
                     pl.BlockSpec((8, blk), lambda p, t, jt, it: (0, jt[t]))]
        args += [cum, cumt]
        out_specs += [pl.BlockSpec((1, 8, blk), lambda p, t, jt, it: (p, 0, jt[t])),
                      pl.BlockSpec((s, LANES), lambda p, t, jt, it: (0, p))]
        out_shape += [jax.ShapeDtypeStruct((n_pairs, 8, s), F32), jax.ShapeDtypeStruct((s, n_pairs * LANES), F32)]
        scratch += [pltpu.VMEM((8, blk), F32), pltpu.VMEM((nh, blk, LANES), F32)]
    return pl.pallas_call(
        body, name=f"attn_bwd_{mode}",
        grid_spec=pltpu.PrefetchScalarGridSpec(num_scalar_prefetch=2, grid=(n_outer, len(tri)), in_specs=in_specs,
                                               out_specs=out_specs, scratch_shapes=scratch),
        out_shape=out_shape, compiler_params=_params(dimension_semantics=("parallel", "arbitrary")),
    )(*tables, *args)


def _swa_masks(i, blk):
    r = lax.broadcasted_iota(jnp.int32, (blk, blk), 0)
    c = lax.broadcasted_iota(jnp.int32, (blk, blk), 1)
    return (c > r) & (i > 0), c <= r


def _nt(a, b):
    return lax.dot_general(a, b, (((1,), (1,)), ((), ())), preferred_element_type=F32)


def _tn(a, b):
    return lax.dot_general(a, b, (((0,), (0,)), ((), ())), preferred_element_type=F32)


def _swa_bwd_call(q, k, v, sink, lse, o, do):
    s = q.shape[0]
    blk, pps, scale = WINDOW, 4, SWA_DIM ** -0.5
    nq = s // blk

    def body(q_ref, kp_ref, ko_ref, vp_ref, vo_ref, sink_ref, lse_ref, o_ref, do_ref,
             dq_ref, dk_ref, dv_ref, dsink_ref, ck_ref, cv_ref):
        i = pl.program_id(1)
        lane = lax.broadcasted_iota(jnp.int32, (1, LANES), 1)
        msk = [lane < HALF, lane >= HALF]

        @pl.when(i == 0)
        def _():
            ck_ref[...] = jnp.zeros_like(ck_ref)
            cv_ref[...] = jnp.zeros_like(cv_ref)
            dsink_ref[...] = jnp.zeros_like(dsink_ref)

        @pl.when(i < nq)
        def _():
            ok_prev, ok_own = _swa_masks(i, blk)
            kp, ko = kp_ref[...].astype(BF16), ko_ref[...].astype(BF16)
            vp, vo = vp_ref[...].astype(BF16), vo_ref[...].astype(BF16)
            dkp, dko, dvp, dvo = [], [], [], []
            for pp in range(pps):
                psl = slice(pp * LANES, (pp + 1) * LANES)
                qp, do_blk = q_ref[:, psl], do_ref[:, psl]
                doo = do_blk * o_ref[:, psl]
                dqs = []
                for h in range(2):
                    hh = 2 * pp + h
                    qh = (jnp.where(msk[h], qp, 0.0) * scale).astype(BF16)
                    lse_h = jnp.broadcast_to(_lane_col(lse_ref[:, psl], HALF * h), (blk, LANES))
                    d_h = jnp.broadcast_to(jnp.sum(jnp.where(msk[h], doo, 0.0), axis=1, keepdims=True), (blk, LANES))
                    p_p = jnp.where(ok_prev, jnp.exp(_nt(qh, kp) - lse_h), 0.0)
                    p_o = jnp.where(ok_own, jnp.exp(_nt(qh, ko) - lse_h), 0.0)
                    do_h = jnp.where(msk[h], do_blk, 0.0).astype(BF16)
                    ds_p = (p_p * (_nt(do_h, vp) - d_h)).astype(BF16)
                    ds_o = (p_o * (_nt(do_h, vo) - d_h)).astype(BF16)
                    dq_h = (jnp.dot(ds_p, kp, preferred_element_type=F32)
                            + jnp.dot(ds_o, ko, preferred_element_type=F32)) * scale
                    dqs.append(jnp.where(msk[h], dq_h, 0.0))
                    dkp.append(_tn(ds_p, qh))
                    dko.append(_tn(ds_o, qh))
                    dvp.append(_tn(p_p.astype(BF16), do_h))
                    dvo.append(_tn(p_o.astype(BF16), do_h))
                    sink_row = sink_ref[hh:hh + 1, :]
                    dsink_ref[hh:hh + 1, :] += -jnp.sum(jnp.exp(sink_row - lse_h) * d_h, axis=0, keepdims=True)
                dq_ref[:, psl] = dqs[0] + dqs[1]
            total = lambda parts: functools.reduce(lambda a, b: a + b, parts)
            dk_ref[...] = ck_ref[...] + total(dkp)
            dv_ref[...] = cv_ref[...] + total(dvp)
            ck_ref[...] = total(dko)
            cv_ref[...] = total(dvo)

        @pl.when(i == nq)
        def _():
            dk_ref[...] = ck_ref[...]
            dv_ref[...] = cv_ref[...]

    last = nq - 1
    prev = lambda g, i: (jnp.maximum(i - 1, 0), g)
    own = lambda g, i: (jnp.minimum(i, last), g)
    qspec = pl.BlockSpec((blk, pps * LANES), own)
    kspec = lambda m: pl.BlockSpec((blk, LANES), m)
    sspec = pl.BlockSpec((8, LANES), lambda g, i: (g, 0))
    return pl.pallas_call(
        body, name="swa_bwd", grid=(SWA_KV_HEADS, nq + 1),
        in_specs=[qspec, kspec(prev), kspec(own), kspec(prev), kspec(own), sspec, qspec, qspec, qspec],
        out_specs=[qspec, kspec(prev), kspec(prev), sspec],
        out_shape=[jax.ShapeDtypeStruct(q.shape, F32), jax.ShapeDtypeStruct(k.shape, F32),
                   jax.ShapeDtypeStruct(v.shape, F32), jax.ShapeDtypeStruct((SWA_HEADS, LANES), F32)],
        scratch_shapes=[pltpu.VMEM((blk, LANES), F32), pltpu.VMEM((blk, LANES), F32)],
        compiler_params=_params(dimension_semantics=("parallel", "arbitrary")),
    )(q, k, k, v, v, sink, lse, o, do)


def _make_attn(mode):
    swa = mode == "swa"

    @jax.custom_vjp
    def attn(q, k, v, extra):
        return fwd(q, k, v, extra)[0]

    def fwd(q, k, v, extra):
        o, lse = _attn_fwd_call(mode, q, k, v, extra)
        return o, (q, k, v, extra, o, lse)

    def bwd(res, do):
        q, k, v, extra, o, lse = res
        outs = (_swa_bwd_call(q, k, v, extra, lse, o, do) if swa
                else _attn_bwd_call(mode, q, k, v, extra, lse, o, do))
        dq, dk, dv = outs[:3]
        if mode == "fox":
            cum, cumt = extra
            dck = outs[3]
            dcumt = dck[:, :2, :].reshape(FOX_HEADS, -1)
            dcq = outs[4].reshape(-1, FOX_HEADS, HALF)[:, :, 0]
            dextra = (jnp.pad(dcq, ((0, 0), (0, LANES - FOX_HEADS))), dcumt)
        elif mode == "swa":
            dextra = jnp.where(jnp.arange(LANES)[None, :] == 0, outs[3], 0.0)
        else:
            dextra = None
        return dq, dk, dv, dextra

    attn.defvjp(fwd, bwd)
    return attn


attn_mla = _make_attn("mla")
attn_fox = _make_attn("fox")
attn_swa = _make_attn("swa")


def _ukv_layout(w):
    r = w.shape[0]
    w3 = w.reshape(r, MLA_HEADS, MLA_NOPE + MLA_V)
    wk = jnp.pad(w3[:, :, :MLA_NOPE], ((0, 0), (0, 0), (0, LANES - MLA_NOPE))).reshape(r, MLA_HEADS * LANES)
    wv = w3[:, :, MLA_NOPE:].reshape(r, MLA_HEADS * MLA_V)
    return wk, wv


def _even_layer(x, w_in_cat, q_norm, w_uq_p, kv_norm, w_ukv, b_f, w_out, ln_g, ln_b, tabs_mla):
    tabs, hs = tabs_mla
    cq, ckv, kpe, fq, fk, fv, fl, gate = even_in_proj(x, relayout(w_in_cat, "even"), tabs)
    q = mm_rope(rms_norm(cq, q_norm), w_uq_p, tabs, hs)
    ckvn = rms_norm(ckv, kv_norm)
    wk, wv = _ukv_layout(w_ukv)
    kk = mm(ckvn, wk) + jnp.tile(kpe, (1, MLA_HEADS))
    o_mla = attn_mla(q, kk, mm(ckvn, wv), None)
    cum = fox_cum(fl, jnp.pad(b_f, (0, LANES - FOX_HEADS)).reshape(1, LANES))
    o_fox = attn_fox(fq, fk, fv, (cum, cum[:, :8].T))
    y = gated_mm((o_mla, o_fox), gate, w_out)
    return ln_res(x, y, ln_g, ln_b)


def _odd_layer(x, w_in_cat, sinks, w_out, ln_g, ln_b, tabs_swa):
    q, kd, vd, gate = odd_in_proj(x, relayout(w_in_cat, "odd"), tabs_swa[0])
    o = attn_swa(q, kd, vd, jnp.broadcast_to(sinks[:, None], (SWA_HEADS, LANES)))
    y = gated_mm((o,), gate, w_out)
    return ln_res(x, y, ln_g, ln_b)


EVEN_SHARDED = ["even_w_in", "even_w_uq", "even_w_ukv", "even_w_out"]
ODD_SHARDED = ["odd_w_in", "odd_w_out", "odd_ln_g", "odd_ln_b"]
EVEN_REPL = ["even_q_norm", "even_kv_norm", "even_b_f", "even_ln_g", "even_ln_b"]
ODD_REPL = ["odd_sinks"]


def _layer_names(layer):
    return (EVEN_SHARDED, EVEN_REPL) if layer % 2 == 0 else (ODD_SHARDED, ODD_REPL)


def _layer_of(name, j):
    return 2 * j if name.startswith("even") else 2 * j + 1


def _layer_apply(layer, p, x, tabs):
    if layer % 2 == 0:
        return _even_layer(x, p["even_w_in"], p["even_q_norm"], p["even_w_uq"], p["even_kv_norm"], p["even_w_ukv"],
                           p["even_b_f"], p["even_w_out"], p["even_ln_g"], p["even_ln_b"], tabs["mla"])
    return _odd_layer(x, p["odd_w_in"], p["odd_sinks"], p["odd_w_out"], p["odd_ln_g"], p["odd_ln_b"], tabs["swa"])


def _pad_rows(flat, mult):
    n = flat.shape[-1]
    per = mult * LANES
    padded = -(-n // per) * per
    if padded != n:
        flat = jnp.pad(flat, [(0, 0)] * (flat.ndim - 1) + [(0, padded - n)])
    return flat.reshape(flat.shape[:-1] + (padded // LANES, LANES))


def _pad_last(a, width):
    if a.shape[-1] == width:
        return a
    return jnp.pad(a, [(0, 0)] * (a.ndim - 1) + [(0, width - a.shape[-1])])


def _join(slots, axis):
    shp = list(slots.shape[1:])
    shp[axis] *= N_DEV
    return jnp.moveaxis(slots, 0, axis).reshape(shp)


def _split(full, axis):
    shp = full.shape
    t = full.reshape(shp[:axis] + (N_DEV, shp[axis] // N_DEV) + shp[axis + 1:])
    return jnp.moveaxis(t, axis, 0)


PAD_TO = {"even_w_in": SHARD_PAD, "even_w_uq": LANES, "odd_w_in": SHARD_PAD}


def kernel(x, even_w_in, even_q_norm, even_w_uq, even_kv_norm, even_w_ukv, even_b_f, even_w_out, even_ln_g, even_ln_b, odd_w_in, odd_sinks, odd_w_out, odd_ln_g, odd_ln_b, loss_target, m_even_w_in, m_even_q_norm, m_even_w_uq, m_even_kv_norm, m_even_w_ukv, m_even_b_f, m_even_w_out, m_even_ln_g, m_even_ln_b, m_odd_w_in, m_odd_sinks, m_odd_w_out, m_odd_ln_g, m_odd_ln_b, v_even_w_in, v_even_q_norm, v_even_w_uq, v_even_kv_norm, v_even_w_ukv, v_even_b_f, v_even_w_out, v_even_ln_g, v_even_ln_b, v_odd_w_in, v_odd_sinks, v_odd_w_out, v_odd_ln_g, v_odd_ln_b):
    w = dict(even_w_in=even_w_in, even_q_norm=even_q_norm, even_w_uq=even_w_uq, even_kv_norm=even_kv_norm,
             even_w_ukv=even_w_ukv, even_b_f=even_b_f, even_w_out=even_w_out, even_ln_g=even_ln_g, even_ln_b=even_ln_b,
             odd_w_in=odd_w_in, odd_sinks=odd_sinks, odd_w_out=odd_w_out, odd_ln_g=odd_ln_g, odd_ln_b=odd_ln_b)
    mom = dict(even_w_in=m_even_w_in, even_q_norm=m_even_q_norm, even_w_uq=m_even_w_uq, even_kv_norm=m_even_kv_norm,
               even_w_ukv=m_even_w_ukv, even_b_f=m_even_b_f, even_w_out=m_even_w_out, even_ln_g=m_even_ln_g,
               even_ln_b=m_even_ln_b, odd_w_in=m_odd_w_in, odd_sinks=m_odd_sinks, odd_w_out=m_odd_w_out,
               odd_ln_g=m_odd_ln_g, odd_ln_b=m_odd_ln_b)
    vel = dict(even_w_in=v_even_w_in, even_q_norm=v_even_q_norm, even_w_uq=v_even_w_uq, even_kv_norm=v_even_kv_norm,
               even_w_ukv=v_even_w_ukv, even_b_f=v_even_b_f, even_w_out=v_even_w_out, even_ln_g=v_even_ln_g,
               even_ln_b=v_even_ln_b, odd_w_in=v_odd_w_in, odd_sinks=v_odd_sinks, odd_w_out=v_odd_w_out,
               odd_ln_g=v_odd_ln_g, odd_ln_b=v_odd_ln_b)
    sharded = BIG + SMALL_SHARDED
    padded = lambda d, n: _pad_last(d[n], PAD_TO.get(n, d[n].shape[-1]))

    tabs = {"mla": _rope_tables(x.shape[1], "mla"), "swa": _rope_tables(x.shape[1], "swa")}
    keys = lambda layers: [(n, layer // 2) for layer in layers for n in _layer_names(layer)[0]]
    first, rest = keys([0]), keys([1, 2, 3])
    me = _lin(_me())

    def shard(n, j):
        a = padded(w, n)[j]
        return a.astype(BF16) if n in BIG else a

    to_full = lambda n, g: _join(g, SHARD_AXIS[n] - 1).astype(F32)
    to_slots = lambda n, g: _split(g, SHARD_AXIS[n] - 1).astype(BF16 if n in BIG else F32)

    def layer_params(layer, full_of):
        shn, rpn = _layer_names(layer)
        p = {n: full_of(n) for n in shn}
        p.update({n: w[n][layer // 2] for n in rpn})
        return p

    got0 = _all_gather([shard(n, j) for n, j in first], "all_gather_first")
    got0, mine_rest = lax.optimization_barrier((got0, [shard(n, j) for n, j in rest]))
    got0 = dict(zip(first, got0))
    send_sems, recv_sems, srcs, lands, token = _split_start(mine_rest, False, "all_gather_rest_start")
    p0 = layer_params(0, lambda n: to_full(n, got0[(n, 0)]) + token[0, 0])
    x1, vjp0 = jax.vjp(lambda p, xx: _layer_apply(0, p, xx, tabs), p0, x[0])
    got = _split_wait(send_sems, recv_sems, srcs, lands, x1, False, "all_gather_rest_wait")
    got = dict(zip(rest, _own_slot(got, [m[None] for m in mine_rest])))

    xs, vjps = x1, [vjp0]
    for layer in (1, 2, 3):
        p = layer_params(layer, lambda n: to_full(n, got[(n, layer // 2)]))
        xs, vjp = jax.vjp(lambda p_, xx, layer=layer: _layer_apply(layer, p_, xx, tabs), p, xs)
        vjps.append(vjp)
    loss_local, vjp_loss = jax.vjp(lambda y: mse_loss(y, loss_target[0]), xs)
    (dy,) = vjp_loss(jnp.ones((), F32))
    loss = lax.psum(loss_local, AXES)
    grads = {}
    for layer in (3, 2, 1):
        grads[layer], dy = vjps[layer](dy)

    parts_rest = [to_slots(n, grads[_layer_of(n, j)][n]) for n, j in rest]
    send_sems, recv_sems, srcs, lands, token = _split_start(parts_rest, True, "grad_exchange_rest_start")
    grads[0], grad_x = vjps[0](dy + token[0, 0])
    recv_rest = _split_wait(send_sems, recv_sems, srcs, lands, grad_x, True, "grad_exchange_rest_wait")
    recv = dict(zip(rest, _own_slot(recv_rest, [lax.dynamic_slice_in_dim(p, me, 1, axis=0) for p in parts_rest])))
    repl_grad = lambda n: jnp.stack([grads[_layer_of(n, j)][n] for j in (0, 1)])
    repl_rows = _pad_rows(jnp.concatenate([repl_grad(n).reshape(-1) for n in REPL]), 8)
    parts_last = [to_slots(n, grads[0][n]) for n, j in first]
    parts_last.append(jnp.broadcast_to(repl_rows[None], (N_DEV,) + repl_rows.shape))
    recv_last = _exchange(parts_last, "grad_exchange_last")
    recv.update(zip(first, recv_last[:-1]))

    g_out, d_out, m_out, v_out = {}, {}, {}, {}
    for n in sharded:
        r = jnp.stack([recv[(n, 0)], recv[(n, 1)]], axis=1)
        cols = r.shape[-1]
        flat2 = lambda d: padded(d, n).reshape(-1, cols)
        outs = _sum_adamw(r.reshape(N_DEV, -1, cols), flat2(w), flat2(mom), flat2(vel))
        for dst, o in zip((g_out, d_out, m_out, v_out), outs):
            dst[n] = o.reshape(w[n].shape[:-1] + (cols,))[..., :w[n].shape[-1]]
    pack = lambda d: _pad_rows(jnp.concatenate([d[n].reshape(-1) for n in REPL]), 8)
    outs = _sum_adamw(recv_last[-1], pack(w), pack(mom), pack(vel))
    for dst, o in zip((g_out, d_out, m_out, v_out), outs):
        flat, off = o.reshape(-1), 0
        for n in REPL:
            size = math.prod(w[n].shape)
            dst[n] = flat[off:off + size].reshape(w[n].shape)
            off += size
    return (loss, grad_x[None], *[g_out[n] for n in WEIGHTS], *[d_out[n] for n in WEIGHTS],
            *[m_out[n] for n in WEIGHTS], *[v_out[n] for n in WEIGHTS])
```

```python
import functools
import math

import jax
import jax.numpy as jnp
from jax import lax
from jax.experimental import pallas as pl
from jax.experimental.pallas import tpu as pltpu

F32 = jnp.float32
BF16 = jnp.bfloat16
LANES = 128
HALF = 64
N_DEV = 8
AXES = ("x", "y", "c")
VMEM_LIMIT = 48 * 1024 * 1024

D_MODEL = 1024
DEPTH = 4
ROPE_THETA = 10000.0
MLA_HEADS, MLA_NOPE, MLA_ROPE, MLA_V, MLA_Q_RANK, MLA_KV_RANK = 8, 64, 32, 64, 256, 128
FOX_HEADS, FOX_DIM = 8, 64
SWA_HEADS, SWA_KV_HEADS, SWA_DIM, WINDOW = 16, 2, 64, 128
RMS_EPS, LN_EPS = 1e-6, 1e-5
ALPHA = (2 * DEPTH) ** 0.25
ADAM_LR, ADAM_B1, ADAM_B2, ADAM_EPS, ADAM_WD, ADAM_STEP = 0.001, 0.9, 0.999, 1e-08, 0.01, 10
NEG = -1e30

WEIGHTS = ["even_w_in", "even_q_norm", "even_w_uq", "even_kv_norm", "even_w_ukv", "even_b_f", "even_w_out",
           "even_ln_g", "even_ln_b", "odd_w_in", "odd_sinks", "odd_w_out", "odd_ln_g", "odd_ln_b"]
SHARD_AXIS = {"even_w_in": 2, "even_w_uq": 2, "even_w_ukv": 2, "even_w_out": 1, "odd_w_in": 2, "odd_w_out": 1,
              "odd_ln_g": 1, "odd_ln_b": 1, "even_q_norm": None, "even_kv_norm": None, "even_b_f": None,
              "even_ln_g": None, "even_ln_b": None, "odd_sinks": None}
BIG = ["even_w_in", "even_w_uq", "even_w_ukv", "even_w_out", "odd_w_in", "odd_w_out"]
SMALL_SHARDED = ["odd_ln_g", "odd_ln_b"]
REPL = [n for n in WEIGHTS if SHARD_AXIS[n] is None]


def _pick(n, cands):
    for c in cands:
        if n % c == 0:
            return c
    return n


def _params(**kw):
    return pltpu.CompilerParams(vmem_limit_bytes=VMEM_LIMIT, **kw)


def _me():
    return lax.axis_index("x"), lax.axis_index("y"), lax.axis_index("c")


def _peer(k):
    x, y, c = _me()
    px = 1 - x if (k >> 2) & 1 else x
    py = 1 - y if (k >> 1) & 1 else y
    pc = 1 - c if k & 1 else c
    return px, py, pc


def _lin(p):
    return 4 * p[0] + 2 * p[1] + p[2]


def _comm_call(body, n, out_shape, args, name):
    any_spec = pl.BlockSpec(memory_space=pl.ANY)
    return pl.pallas_call(
        body, name=name, out_shape=out_shape, in_specs=[any_spec] * n, out_specs=[any_spec] * n,
        scratch_shapes=[pltpu.SemaphoreType.DMA((n, N_DEV - 1)), pltpu.SemaphoreType.DMA((n, N_DEV - 1)),
                        pltpu.SemaphoreType.DMA((n,))],
    )(*args)


def _all_gather(xs, name):
    n = len(xs)

    def body(*refs):
        x_refs, out_refs = refs[:n], refs[n:2 * n]
        send_sems, recv_sems, local_sems = refs[2 * n:]
        me = _lin(_me())
        local = [pltpu.make_async_copy(x_refs[a], out_refs[a].at[me], local_sems.at[a]) for a in range(n)]
        for cp in local:
            cp.start()
        sends = []
        for k in range(1, N_DEV):
            for a in range(n):
                cp = pltpu.make_async_remote_copy(
                    src_ref=x_refs[a], dst_ref=out_refs[a].at[me], send_sem=send_sems.at[a, k - 1],
                    recv_sem=recv_sems.at[a, k - 1], device_id=_peer(k), device_id_type=pl.DeviceIdType.MESH)
                cp.start()
                sends.append(cp)
        for k in range(1, N_DEV):
            for a in range(n):
                pltpu.make_async_remote_copy(
                    src_ref=x_refs[a], dst_ref=out_refs[a].at[_lin(_peer(k))], send_sem=send_sems.at[a, k - 1],
                    recv_sem=recv_sems.at[a, k - 1], device_id=_peer(k),
                    device_id_type=pl.DeviceIdType.MESH).wait_recv()
        for cp in sends:
            cp.wait_send()
        for cp in local:
            cp.wait()

    out_shape = [jax.ShapeDtypeStruct((N_DEV,) + x.shape, x.dtype) for x in xs]
    return _comm_call(body, n, out_shape, xs, name)


def _exchange(parts, name):
    n = len(parts)

    def body(*refs):
        p_refs, out_refs = refs[:n], refs[n:2 * n]
        send_sems, recv_sems, local_sems = refs[2 * n:]
        me = _lin(_me())
        local = [pltpu.make_async_copy(p_refs[a].at[me], out_refs[a].at[me], local_sems.at[a]) for a in range(n)]
        for cp in local:
            cp.start()
        sends = []
        for k in range(1, N_DEV):
            peer = _peer(k)
            for a in range(n):
                cp = pltpu.make_async_remote_copy(
                    src_ref=p_refs[a].at[_lin(peer)], dst_ref=out_refs[a].at[me], send_sem=send_sems.at[a, k - 1],
                    recv_sem=recv_sems.at[a, k - 1], device_id=peer, device_id_type=pl.DeviceIdType.MESH)
                cp.start()
                sends.append(cp)
        for k in range(1, N_DEV):
            peer = _peer(k)
            for a in range(n):
                pltpu.make_async_remote_copy(
                    src_ref=p_refs[a].at[_lin(peer)], dst_ref=out_refs[a].at[_lin(peer)],
                    send_sem=send_sems.at[a, k - 1], recv_sem=recv_sems.at[a, k - 1], device_id=peer,
                    device_id_type=pl.DeviceIdType.MESH).wait_recv()
        for cp in sends:
            cp.wait_send()
        for cp in local:
            cp.wait()

    out_shape = [jax.ShapeDtypeStruct(p.shape, p.dtype) for p in parts]
    return _comm_call(body, n, out_shape, parts, name)


_HBM = pl.BlockSpec(memory_space=pltpu.HBM)
_SEM = pl.BlockSpec(memory_space=pltpu.SEMAPHORE)
_EFFECT = pltpu.SideEffectType.DATAFLOW_SIDE_EFFECTING


def _split_start(srcs, slotted, name):
    n = len(srcs)
    lands = [lax.empty(s.shape if slotted else (N_DEV,) + s.shape, s.dtype) for s in srcs]

    def body(*refs):
        src_refs, land_refs = refs[:n], refs[n:2 * n]
        send_sems, recv_sems, token = refs[2 * n], refs[2 * n + 1], refs[-1]
        me = _lin(_me())
        for k in range(1, N_DEV):
            peer = _peer(k)
            for a in range(n):
                pltpu.make_async_remote_copy(
                    src_ref=src_refs[a].at[_lin(peer)] if slotted else src_refs[a], dst_ref=land_refs[a].at[me],
                    send_sem=send_sems.at[a * (N_DEV - 1) + k - 1], recv_sem=recv_sems.at[a * (N_DEV - 1) + k - 1],
                    device_id=peer, device_id_type=pl.DeviceIdType.MESH).start()
        token[...] = jnp.zeros_like(token)

    both = list(srcs) + lands
    outs = pl.pallas_call(
        body, name=name,
        out_shape=(pltpu.SemaphoreType.DMA((n * (N_DEV - 1),)), pltpu.SemaphoreType.DMA((n * (N_DEV - 1),)),
                   *[pltpu.HBM(b.shape, b.dtype) for b in both], jax.ShapeDtypeStruct((8, LANES), F32)),
        in_specs=[_HBM] * (2 * n), out_specs=(_SEM, _SEM, *[_HBM] * (2 * n), pl.BlockSpec(memory_space=pltpu.VMEM)),
        input_output_aliases={a: 2 + a for a in range(2 * n)},
        compiler_params=pltpu.CompilerParams(has_side_effects=_EFFECT),
    )(*[pltpu.with_memory_space_constraint(b, pltpu.HBM) for b in both])
    return outs[0], outs[1], list(outs[2:2 + n]), list(outs[2 + n:2 + 2 * n]), outs[-1]


def _split_wait(send_sems, recv_sems, srcs, lands, after, slotted, name):
    n = len(srcs)

    def body(*refs):
        src_refs, land_refs = refs[:n], refs[n:2 * n]
        send_sems, recv_sems = refs[2 * n], refs[2 * n + 1]
        for k in range(1, N_DEV):
            peer = _peer(k)
            for a in range(n):
                cp = pltpu.make_async_remote_copy(
                    src_ref=src_refs[a].at[_lin(peer)] if slotted else src_refs[a],
                    dst_ref=land_refs[a].at[_lin(peer)], send_sem=send_sems.at[a * (N_DEV - 1) + k - 1],
                    recv_sem=recv_sems.at[a * (N_DEV - 1) + k - 1], device_id=peer,
                    device_id_type=pl.DeviceIdType.MESH)
                cp.wait_send()
                cp.wait_recv()

    both = list(srcs) + list(lands)
    outs = pl.pallas_call(
        body, name=name, out_shape=[pltpu.HBM(b.shape, b.dtype) for b in both],
        in_specs=[_HBM] * (2 * n) + [_SEM, _SEM, pl.BlockSpec(memory_space=pl.ANY)], out_specs=[_HBM] * (2 * n),
        input_output_aliases={a: a for a in range(2 * n)},
        compiler_params=pltpu.CompilerParams(has_side_effects=_EFFECT),
    )(*both, send_sems, recv_sems, after)
    return list(outs[n:])


def _own_slot(lands, own):
    me = _lin(_me())
    slot = lambda l: lax.broadcasted_iota(jnp.int32, (N_DEV,) + (1,) * (l.ndim - 1), 0)
    return [jnp.where(slot(l) == me, o.astype(l.dtype), l) for l, o in zip(lands, own)]


def _sum_adamw(recv, w, m, v):
    _, rows, lanes = recv.shape
    tr = _pick(rows, (256, 128, 64, 32, 16, 8))
    c1 = 1.0 - ADAM_B1 ** ADAM_STEP
    c2 = 1.0 - ADAM_B2 ** ADAM_STEP

    def body(r_ref, w_ref, m_ref, v_ref, g_out, d_out, m_out, v_out):
        g = r_ref[0].astype(F32)
        for s in range(1, N_DEV):
            g = g + r_ref[s].astype(F32)
        mn = ADAM_B1 * m_ref[...] + (1.0 - ADAM_B1) * g
        vn = ADAM_B2 * v_ref[...] + (1.0 - ADAM_B2) * (g * g)
        m_hat = mn / c1
        v_hat = vn / c2
        g_out[...] = g
        d_out[...] = -ADAM_LR * (m_hat / (jnp.sqrt(v_hat) + ADAM_EPS) + ADAM_WD * w_ref[...])
        m_out[...] = mn
        v_out[...] = vn

    blk = pl.BlockSpec((tr, lanes), lambda i: (i, 0))
    shp = jax.ShapeDtypeStruct((rows, lanes), F32)
    return pl.pallas_call(
        body, name=f"sum_adamw_{rows}x{lanes}", grid=(rows // tr,),
        in_specs=[pl.BlockSpec((N_DEV, tr, lanes), lambda i: (0, i, 0)), blk, blk, blk],
        out_specs=[blk, blk, blk, blk], out_shape=[shp, shp, shp, shp],
        compiler_params=_params(dimension_semantics=("parallel",)),
    )(recv, w, m, v)


def _rope_block(xb, cv, s1v, s2v, hs):
    return xb * cv + pltpu.roll(xb, LANES - hs, 1) * s1v + pltpu.roll(xb, hs, 1) * s2v


def _mm_nn(a, b, b_transposed=False, rope=None):
    m, k = a.shape
    n = b.shape[0] if b_transposed else b.shape[1]
    tm = _pick(m, (1024, 512, 256, 128))
    tn = _pick(n, (1024, 640, 512, 256, 128))
    tk = _pick(k, (1024, 640, 512, 256, 128))
    nk = k // tk

    def body(*refs):
        a_ref, b_ref = refs[:2]
        o_ref, acc_ref = refs[-2:]
        kk = pl.program_id(2)

        @pl.when(kk == 0)
        def _():
            acc_ref[...] = jnp.zeros_like(acc_ref)

        dims = (((1,), (1,)), ((), ())) if b_transposed else (((1,), (0,)), ((), ()))
        acc_ref[...] += lax.dot_general(a_ref[...].astype(BF16), b_ref[...].astype(BF16), dims,
                                        preferred_element_type=F32)

        @pl.when(kk == nk - 1)
        def _():
            if rope is None:
                o_ref[...] = acc_ref[...]
            else:
                cv, s1v, s2v = refs[2][...], refs[3][...], refs[4][...]
                for cb in range(tn // LANES):
                    sl = slice(cb * LANES, (cb + 1) * LANES)
                    o_ref[:, sl] = _rope_block(acc_ref[:, sl], cv, s1v, s2v, rope[3])

    b_spec = (pl.BlockSpec((tn, tk), lambda i, j, kk: (j, kk)) if b_transposed
              else pl.BlockSpec((tk, tn), lambda i, j, kk: (kk, j)))
    in_specs, args = [pl.BlockSpec((tm, tk), lambda i, j, kk: (i, kk)), b_spec], [a, b]
    if rope is not None:
        in_specs += [pl.BlockSpec((tm, LANES), lambda i, j, kk: (i, 0))] * 3
        args += list(rope[:3])
    tag = ("t" if b_transposed else "n") + ("" if rope is None else f"_rope{rope[3]}")
    return pl.pallas_call(
        body, name=f"mm_n{tag}_{m}x{k}x{n}", grid=(m // tm, n // tn, nk), in_specs=in_specs,
        out_specs=pl.BlockSpec((tm, tn), lambda i, j, kk: (i, j)),
        out_shape=jax.ShapeDtypeStruct((m, n), F32),
        scratch_shapes=[pltpu.VMEM((tm, tn), F32)],
        compiler_params=_params(dimension_semantics=("parallel", "parallel", "arbitrary")),
    )(*args)


def _mm_tn(a, g):
    s, k = a.shape
    _, n = g.shape
    tm = _pick(k, (1024, 512, 256, 128))
    tn = _pick(n, (1024, 640, 512, 256, 128))
    ts = _pick(s, (512, 256, 128))
    ns = s // ts

    def body(a_ref, g_ref, o_ref, acc_ref):
        ss = pl.program_id(2)

        @pl.when(ss == 0)
        def _():
            acc_ref[...] = jnp.zeros_like(acc_ref)

        acc_ref[...] += lax.dot_general(a_ref[...].astype(BF16), g_ref[...].astype(BF16),
                                        (((0,), (0,)), ((), ())), preferred_element_type=F32)

        @pl.when(ss == ns - 1)
        def _():
            o_ref[...] = acc_ref[...]

    return pl.pallas_call(
        body, name=f"mm_tn_{s}x{k}x{n}", grid=(k // tm, n // tn, ns),
        in_specs=[pl.BlockSpec((ts, tm), lambda i, j, ss: (ss, i)), pl.BlockSpec((ts, tn), lambda i, j, ss: (ss, j))],
        out_specs=pl.BlockSpec((tm, tn), lambda i, j, ss: (i, j)),
        out_shape=jax.ShapeDtypeStruct((k, n), F32),
        scratch_shapes=[pltpu.VMEM((tm, tn), F32)],
        compiler_params=_params(dimension_semantics=("parallel", "parallel", "arbitrary")),
    )(a, g)


@jax.custom_vjp
def mm(a, w):
    return _mm_nn(a, w.astype(BF16))


def _mm_fwd(a, w):
    wb = w.astype(BF16)
    return _mm_nn(a, wb), (a, wb)


def _mm_bwd(res, g):
    a, wb = res
    return _mm_nn(g, wb, b_transposed=True), _mm_tn(a, g)


mm.defvjp(_mm_fwd, _mm_bwd)


def _unrope(g, tabs, hs):
    return _rope_call(g, tabs[0], -tabs[1], -tabs[2], hs)


@functools.partial(jax.custom_vjp, nondiff_argnums=(3,))
def mm_rope(a, w, tabs, hs):
    return _mm_nn(a, w.astype(BF16), rope=(*tabs, hs))


def _mm_rope_fwd(a, w, tabs, hs):
    wb = w.astype(BF16)
    return _mm_nn(a, wb, rope=(*tabs, hs)), (a, wb, tabs)


def _mm_rope_bwd(hs, res, g):
    a, wb, tabs = res
    g = _unrope(g, tabs, hs)
    return _mm_nn(g, wb, b_transposed=True), _mm_tn(a, g), jax.tree.map(jnp.zeros_like, tabs)


mm_rope.defvjp(_mm_rope_fwd, _mm_rope_bwd)


def _proj_dx_call(gs, wb, cuts):
    s, (d, n), ng = gs[0].shape[0], wb.shape, len(gs)
    tm = _pick(s, (512, 256, 128))

    def body(*refs):
        w_ref, o_ref = refs[ng], refs[ng + 1]
        for gi in range(ng):
            part = lax.dot_general(refs[gi][...].astype(BF16), w_ref[:, cuts[gi]:cuts[gi + 1]],
                                   (((1,), (1,)), ((), ())), preferred_element_type=F32)
            if gi == 0:
                o_ref[...] = part
            else:
                o_ref[...] += part

    return pl.pallas_call(
        body, name=f"proj_dx_{ng}", grid=(s // tm,),
        in_specs=[pl.BlockSpec((tm, g.shape[1]), lambda i: (i, 0)) for g in gs] + [pl.BlockSpec((d, n), lambda i: (0, 0))],
        out_specs=pl.BlockSpec((tm, d), lambda i: (i, 0)), out_shape=jax.ShapeDtypeStruct((s, d), F32),
        compiler_params=_params(dimension_semantics=("parallel",)),
    )(*gs, wb)


def _proj_dw_call(xb, gs, cuts):
    (s, d), ng, n = xb.shape, len(gs), cuts[-1]
    ts = _pick(s, (256, 128))
    ns = s // ts

    def body(*refs):
        x_ref, o_ref = refs[0], refs[ng + 1]
        ss = pl.program_id(0)
        xt = x_ref[...].T
        for gi in range(ng):
            cols = slice(cuts[gi], cuts[gi + 1])
            part = jnp.dot(xt, refs[1 + gi][...].astype(BF16), preferred_element_type=F32)

            @pl.when(ss == 0)
            def _():
                o_ref[:, cols] = part

            @pl.when(ss > 0)
            def _():
                o_ref[:, cols] += part

    return pl.pallas_call(
        body, name=f"proj_dw_{ng}", grid=(ns,),
        in_specs=[pl.BlockSpec((ts, d), lambda ss: (ss, 0))] + [pl.BlockSpec((ts, g.shape[1]), lambda ss: (ss, 0)) for g in gs],
        out_specs=pl.BlockSpec((d, n), lambda ss: (0, 0)), out_shape=jax.ShapeDtypeStruct((d, n), F32),
        compiler_params=_params(dimension_semantics=("arbitrary",)),
    )(xb, *gs)


def _make_in_proj(widths, roped, hs):
    cuts = [sum(widths[:i]) for i in range(len(widths) + 1)]

    @jax.custom_vjp
    def in_proj(x, w, tabs):
        return fwd(x, w, tabs)[0]

    def fwd(x, w, tabs):
        xb, wb = x.astype(BF16), w.astype(BF16)
        outs = tuple(_mm_nn(xb, wb[:, a:b], rope=(*tabs, hs) if gi in roped else None)
                     for gi, (a, b) in enumerate(zip(cuts[:-1], cuts[1:])))
        return outs, (xb, wb, tabs)

    def bwd(res, gs):
        xb, wb, tabs = res
        gs = [_unrope(gg, tabs, hs) if gi in roped else gg for gi, gg in enumerate(gs)]
        return _proj_dx_call(gs, wb, cuts), _proj_dw_call(xb, gs, cuts), jax.tree.map(jnp.zeros_like, tabs)

    in_proj.defvjp(fwd, bwd)
    return in_proj


EVEN_GROUPS = (256, 128, 128, 512, 512, 512, 128, 1024)
ODD_GROUPS = (1024, 256, 256, 1024)
even_in_proj = _make_in_proj(EVEN_GROUPS, roped=(2,), hs=MLA_ROPE // 2)
odd_in_proj = _make_in_proj(ODD_GROUPS, roped=(0, 1), hs=SWA_DIM // 2)

SHARD_PAD = 384


def _source_columns(kind):
    if kind == "even":
        src = [list(range(0, 384)), [-1] * 64, list(range(384, 416)), [-1] * 32, list(range(416, 1952)),
               list(range(1952, 1960)), [-1] * 120, list(range(1960, 2984))]
        return sum(src, []), 373
    q0, k0, v0, g0 = 0, 1024, 1152, 1280
    dup = lambda base: [base + 64 * g + c for g in range(SWA_KV_HEADS) for _ in range(2) for c in range(64)]
    return list(range(q0, k0)) + dup(k0) + dup(v0) + list(range(g0, 2304)), 288


def _selection(kind, transposed):
    src, shard = _source_columns(kind)
    cat = [s + (SHARD_PAD - shard) * (s // shard) if s >= 0 else -1 for s in src]
    cat_arr = jnp.asarray(cat, jnp.int32)
    if transposed:
        cols = lax.broadcasted_iota(jnp.int32, (len(src), N_DEV * SHARD_PAD), 1)
        return (cols == cat_arr[:, None]).astype(BF16), [(c, r) for c, r in enumerate(cat) if r >= 0]
    rows = lax.broadcasted_iota(jnp.int32, (N_DEV * SHARD_PAD, len(src)), 0)
    return (rows == cat_arr[None, :]).astype(BF16), [(r, c) for c, r in enumerate(cat) if r >= 0]


def _mm_banded(a, b, nonzeros):
    m, k = a.shape
    _, n = b.shape
    tm = _pick(m, (1024, 512, 256, 128))
    tn = _pick(n, (1024, 640, 512, 256, 128))
    tk = _pick(k, (1024, 640, 512, 256, 128))
    lo, hi = [k // tk] * (n // tn), [-1] * (n // tn)
    for r, c in nonzeros:
        lo[c // tn], hi[c // tn] = min(lo[c // tn], r // tk), max(hi[c // tn], r // tk)
    first = [l if h >= 0 else 0 for l, h in zip(lo, hi)]
    count = [h - l + 1 if h >= 0 else 0 for l, h in zip(lo, hi)]
    steps = max(count)

    def body(first_ref, count_ref, a_ref, b_ref, o_ref, acc_ref):
        j, kk = pl.program_id(1), pl.program_id(2)

        @pl.when(kk == 0)
        def _():
            acc_ref[...] = jnp.zeros_like(acc_ref)

        @pl.when(kk < count_ref[j])
        def _():
            acc_ref[...] += jnp.dot(a_ref[...].astype(BF16), b_ref[...].astype(BF16), preferred_element_type=F32)

        @pl.when(kk == steps - 1)
        def _():
            o_ref[...] = acc_ref[...]

    kblk = lambda j, kk, f, c: jnp.minimum(f[j] + kk, f[j] + jnp.maximum(c[j], 1) - 1)
    return pl.pallas_call(
        body, name=f"mm_banded_{m}x{k}x{n}",
        grid_spec=pltpu.PrefetchScalarGridSpec(
            num_scalar_prefetch=2, grid=(m // tm, n // tn, steps),
            in_specs=[pl.BlockSpec((tm, tk), lambda i, j, kk, f, c: (i, kblk(j, kk, f, c))),
                      pl.BlockSpec((tk, tn), lambda i, j, kk, f, c: (kblk(j, kk, f, c), j))],
            out_specs=pl.BlockSpec((tm, tn), lambda i, j, kk, f, c: (i, j)),
            scratch_shapes=[pltpu.VMEM((tm, tn), F32)]),
        out_shape=jax.ShapeDtypeStruct((m, n), F32),
        compiler_params=_params(dimension_semantics=("parallel", "parallel", "arbitrary")),
    )(jnp.asarray(first, jnp.int32), jnp.asarray(count, jnp.int32), a, b)


@functools.partial(jax.custom_vjp, nondiff_argnums=(1,))
def relayout(wcat, kind):
    return _mm_banded(wcat, *_selection(kind, False))


def _relayout_bwd(kind, _, g):
    return (_mm_banded(g, *_selection(kind, True)),)


relayout.defvjp(lambda wcat, kind: (_mm_banded(wcat, *_selection(kind, False)), None), _relayout_bwd)


def _row_block(s):
    return _pick(s, (512, 256, 128, 64, 32, 16, 8))


def _rms_fwd_call(x, g):
    s, k = x.shape
    tr = _row_block(s)

    def body(x_ref, g_ref, o_ref):
        xv = x_ref[...]
        r = lax.rsqrt(jnp.mean(xv * xv, axis=-1, keepdims=True) + RMS_EPS)
        o_ref[...] = xv * r * g_ref[...]

    return pl.pallas_call(
        body, name=f"rms_fwd_{k}", grid=(s // tr,),
        in_specs=[pl.BlockSpec((tr, k), lambda i: (i, 0)), pl.BlockSpec((1, k), lambda i: (0, 0))],
        out_specs=pl.BlockSpec((tr, k), lambda i: (i, 0)), out_shape=jax.ShapeDtypeStruct((s, k), F32),
        compiler_params=_params(dimension_semantics=("parallel",)),
    )(x, g.reshape(1, k))


def _rms_bwd_call(x, g, dy):
    s, k = x.shape
    tr = _row_block(s)

    def body(x_ref, g_ref, dy_ref, dx_ref, dg_ref):
        @pl.when(pl.program_id(0) == 0)
        def _():
            dg_ref[...] = jnp.zeros_like(dg_ref)

        xv = x_ref[...]
        r = lax.rsqrt(jnp.mean(xv * xv, axis=-1, keepdims=True) + RMS_EPS)
        xh = xv * r
        dyv = dy_ref[...]
        dg_ref[...] += jnp.sum(dyv * xh, axis=0, keepdims=True)
        dxh = dyv * g_ref[...]
        dx_ref[...] = r * (dxh - xh * jnp.mean(dxh * xh, axis=-1, keepdims=True))

    dx, dg = pl.pallas_call(
        body, name=f"rms_bwd_{k}", grid=(s // tr,),
        in_specs=[pl.BlockSpec((tr, k), lambda i: (i, 0)), pl.BlockSpec((1, k), lambda i: (0, 0)),
                  pl.BlockSpec((tr, k), lambda i: (i, 0))],
        out_specs=[pl.BlockSpec((tr, k), lambda i: (i, 0)), pl.BlockSpec((1, k), lambda i: (0, 0))],
        out_shape=[jax.ShapeDtypeStruct((s, k), F32), jax.ShapeDtypeStruct((1, k), F32)],
        compiler_params=_params(dimension_semantics=("arbitrary",)),
    )(x, g.reshape(1, k), dy)
    return dx, dg.reshape(k)


@jax.custom_vjp
def rms_norm(x, g):
    return _rms_fwd_call(x, g)


rms_norm.defvjp(lambda x, g: (_rms_fwd_call(x, g), (x, g)), lambda res, dy: _rms_bwd_call(res[0], res[1], dy))


def _ln_fwd_call(x, y, g, b):
    s, k = x.shape
    tr = _row_block(s)

    def body(x_ref, y_ref, g_ref, b_ref, o_ref):
        u = ALPHA * x_ref[...] + y_ref[...]
        mu = jnp.mean(u, axis=-1, keepdims=True)
        d = u - mu
        var = jnp.mean(d * d, axis=-1, keepdims=True)
        o_ref[...] = d * lax.rsqrt(var + LN_EPS) * g_ref[...] + b_ref[...]

    row = pl.BlockSpec((tr, k), lambda i: (i, 0))
    vec = pl.BlockSpec((1, k), lambda i: (0, 0))
    return pl.pallas_call(
        body, name="ln_fwd", grid=(s // tr,), in_specs=[row, row, vec, vec], out_specs=row,
        out_shape=jax.ShapeDtypeStruct((s, k), F32), compiler_params=_params(dimension_semantics=("parallel",)),
    )(x, y, g.reshape(1, k), b.reshape(1, k))


def _ln_bwd_call(x, y, g, do):
    s, k = x.shape
    tr = _row_block(s)

    def body(x_ref, y_ref, g_ref, do_ref, dx_ref, dy_ref, dg_ref, db_ref):
        @pl.when(pl.program_id(0) == 0)
        def _():
            dg_ref[...] = jnp.zeros_like(dg_ref)
            db_ref[...] = jnp.zeros_like(db_ref)

        u = ALPHA * x_ref[...] + y_ref[...]
        mu = jnp.mean(u, axis=-1, keepdims=True)
        d = u - mu
        r = lax.rsqrt(jnp.mean(d * d, axis=-1, keepdims=True) + LN_EPS)
        xh = d * r
        dov = do_ref[...]
        dg_ref[...] += jnp.sum(dov * xh, axis=0, keepdims=True)
        db_ref[...] += jnp.sum(dov, axis=0, keepdims=True)
        dxh = dov * g_ref[...]
        du = r * (dxh - jnp.mean(dxh, axis=-1, keepdims=True) - xh * jnp.mean(dxh * xh, axis=-1, keepdims=True))
        dy_ref[...] = du
        dx_ref[...] = ALPHA * du

    row = pl.BlockSpec((tr, k), lambda i: (i, 0))
    vec = pl.BlockSpec((1, k), lambda i: (0, 0))
    dx, dy, dg, db = pl.pallas_call(
        body, name="ln_bwd", grid=(s // tr,), in_specs=[row, row, vec, row], out_specs=[row, row, vec, vec],
        out_shape=[jax.ShapeDtypeStruct((s, k), F32), jax.ShapeDtypeStruct((s, k), F32),
                   jax.ShapeDtypeStruct((1, k), F32), jax.ShapeDtypeStruct((1, k), F32)],
        compiler_params=_params(dimension_semantics=("arbitrary",)),
    )(x, y, g.reshape(1, k), do)
    return dx, dy, dg.reshape(k), db.reshape(k)


@jax.custom_vjp
def ln_res(x, y, g, b):
    return _ln_fwd_call(x, y, g, b)


ln_res.defvjp(lambda x, y, g, b: (_ln_fwd_call(x, y, g, b), (x, y, g)),
              lambda res, do: _ln_bwd_call(res[0], res[1], res[2], do))


def _rope_call(x, c, s1, s2, hs):
    s, w = x.shape
    tr = _row_block(s)
    nb = w // LANES

    def body(x_ref, c_ref, s1_ref, s2_ref, o_ref):
        cv, s1v, s2v = c_ref[...], s1_ref[...], s2_ref[...]
        for cb in range(nb):
            xb = x_ref[:, cb * LANES:(cb + 1) * LANES]
            o_ref[:, cb * LANES:(cb + 1) * LANES] = (
                xb * cv + pltpu.roll(xb, LANES - hs, 1) * s1v + pltpu.roll(xb, hs, 1) * s2v)

    row = pl.BlockSpec((tr, w), lambda i: (i, 0))
    tab = pl.BlockSpec((tr, LANES), lambda i: (i, 0))
    return pl.pallas_call(
        body, name=f"rope_{w}_{hs}", grid=(s // tr,), in_specs=[row, tab, tab, tab], out_specs=row,
        out_shape=jax.ShapeDtypeStruct((s, w), F32), compiler_params=_params(dimension_semantics=("parallel",)),
    )(x, c, s1, s2)


def _rope_tables(s, layout):
    pos = jnp.arange(s, dtype=F32)[:, None]
    lane = jnp.arange(LANES)
    if layout == "mla":
        dim, hs = MLA_ROPE, MLA_ROPE // 2
        r = lane - MLA_NOPE
        active = (r >= 0) & (r < MLA_ROPE)
    else:
        dim, hs = SWA_DIM, SWA_DIM // 2
        r = lane % SWA_DIM
        active = jnp.ones_like(lane, dtype=bool)
    f = jnp.where(active, r % hs, 0)
    inv = ROPE_THETA ** (-(2.0 * f.astype(F32)) / dim)
    ang = pos * inv[None, :]
    cos, sin = jnp.cos(ang), jnp.sin(ang)
    first = (active & (r < hs))[None, :]
    second = (active & (r >= hs))[None, :]
    c = jnp.where(active[None, :], cos, 1.0)
    s1 = jnp.where(first, -sin, 0.0)
    s2 = jnp.where(second, sin, 0.0)
    return (c, s1, s2), hs


def _gated(o_refs, g_ref):
    gv = g_ref[...]
    o = o_refs[0][...] if len(o_refs) == 1 else jnp.concatenate([r[...] for r in o_refs], axis=1)
    return (o * (gv * jax.nn.sigmoid(gv))).astype(BF16)


def _gated_mm_call(o_parts, gate, wb):
    s, k = gate.shape
    n = wb.shape[1]
    tm = _pick(s, (1024, 512, 256, 128))
    tn = _pick(n, (1024, 512, 256, 128))
    widths = [o.shape[1] for o in o_parts]
    no = len(widths)

    def body(*refs):
        refs[-1][...] = jnp.dot(_gated(refs[:no], refs[no]), refs[no + 1][...], preferred_element_type=F32)

    specs = [pl.BlockSpec((tm, wd), lambda i, j: (i, 0)) for wd in widths]
    return pl.pallas_call(
        body, name=f"gated_mm_{no}", grid=(s // tm, n // tn),
        in_specs=specs + [pl.BlockSpec((tm, k), lambda i, j: (i, 0)), pl.BlockSpec((k, tn), lambda i, j: (0, j))],
        out_specs=pl.BlockSpec((tm, tn), lambda i, j: (i, j)), out_shape=jax.ShapeDtypeStruct((s, n), F32),
        compiler_params=_params(dimension_semantics=("parallel", "parallel")),
    )(*o_parts, gate, wb)


def _gated_mm_tn_call(o_parts, gate, g):
    s, k = gate.shape
    n = g.shape[1]
    ts = _pick(s, (512, 256, 128))
    ns = s // ts
    widths = [o.shape[1] for o in o_parts]
    no = len(widths)

    def body(*refs):
        g_ref, o_ref, acc_ref = refs[no + 1], refs[no + 2], refs[no + 3]
        ss = pl.program_id(0)

        @pl.when(ss == 0)
        def _():
            acc_ref[...] = jnp.zeros_like(acc_ref)

        acc_ref[...] += lax.dot_general(_gated(refs[:no], refs[no]), g_ref[...].astype(BF16),
                                        (((0,), (0,)), ((), ())), preferred_element_type=F32)

        @pl.when(ss == ns - 1)
        def _():
            o_ref[...] = acc_ref[...]

    specs = [pl.BlockSpec((ts, wd), lambda ss: (ss, 0)) for wd in widths]
    return pl.pallas_call(
        body, name=f"gated_mm_tn_{no}", grid=(ns,),
        in_specs=specs + [pl.BlockSpec((ts, k), lambda ss: (ss, 0)), pl.BlockSpec((ts, n), lambda ss: (ss, 0))],
        out_specs=pl.BlockSpec((k, n), lambda ss: (0, 0)), out_shape=jax.ShapeDtypeStruct((k, n), F32),
        scratch_shapes=[pltpu.VMEM((k, n), F32)], compiler_params=_params(dimension_semantics=("arbitrary",)),
    )(*o_parts, gate, g)


def _gate_bwd_call(o_parts, gate, dz):
    s, w = gate.shape
    tr = _row_block(s)
    widths = [o.shape[1] for o in o_parts]
    n = len(widths)

    def body(*refs):
        o_refs, g_ref, dz_ref = refs[:n], refs[n], refs[n + 1]
        do_refs, dg_ref = refs[n + 2:2 * n + 2], refs[2 * n + 2]
        off = 0
        for o_ref, do_ref, wd in zip(o_refs, do_refs, widths):
            gv = g_ref[:, off:off + wd]
            sg = jax.nn.sigmoid(gv)
            dzv = dz_ref[:, off:off + wd]
            do_ref[...] = dzv * (gv * sg)
            dg_ref[:, off:off + wd] = dzv * o_ref[...] * (sg * (1.0 + gv * (1.0 - sg)))
            off += wd

    specs = [pl.BlockSpec((tr, wd), lambda i: (i, 0)) for wd in widths]
    row = pl.BlockSpec((tr, w), lambda i: (i, 0))
    outs = pl.pallas_call(
        body, name=f"gate_bwd_{n}", grid=(s // tr,), in_specs=specs + [row, row], out_specs=specs + [row],
        out_shape=[jax.ShapeDtypeStruct((s, wd), F32) for wd in widths] + [jax.ShapeDtypeStruct((s, w), F32)],
        compiler_params=_params(dimension_semantics=("parallel",)),
    )(*o_parts, gate, dz)
    return tuple(outs[:n]), outs[n]


@jax.custom_vjp
def gated_mm(o_parts, gate, w):
    return _gated_mm_call(o_parts, gate, w.astype(BF16))


def _gated_mm_fwd(o_parts, gate, w):
    wb = w.astype(BF16)
    return _gated_mm_call(o_parts, gate, wb), (o_parts, gate, wb)


def _gated_mm_bwd(res, g):
    o_parts, gate, wb = res
    do_parts, dgate = _gate_bwd_call(o_parts, gate, _mm_nn(g, wb, b_transposed=True))
    return do_parts, dgate, _gated_mm_tn_call(o_parts, gate, g)


gated_mm.defvjp(_gated_mm_fwd, _gated_mm_bwd)


def _loss_call(y, t):
    s, k = y.shape
    tr = _row_block(s)
    nsteps = s // tr

    def body(y_ref, t_ref, l_ref, dy_ref, acc_ref):
        i = pl.program_id(0)

        @pl.when(i == 0)
        def _():
            acc_ref[...] = jnp.zeros_like(acc_ref)

        d = y_ref[...] - t_ref[...]
        dy_ref[...] = d / k
        acc_ref[...] += jnp.sum(d * d, axis=0, keepdims=True)

        @pl.when(i == nsteps - 1)
        def _():
            tot = jnp.sum(acc_ref[...], axis=1, keepdims=True) * (0.5 / k)
            l_ref[...] = jnp.broadcast_to(tot, l_ref.shape)

    row = pl.BlockSpec((tr, k), lambda i: (i, 0))
    return pl.pallas_call(
        body, name="loss", grid=(nsteps,), in_specs=[row, row],
        out_specs=[pl.BlockSpec((1, LANES), lambda i: (0, 0)), row],
        out_shape=[jax.ShapeDtypeStruct((1, LANES), F32), jax.ShapeDtypeStruct((s, k), F32)],
        scratch_shapes=[pltpu.VMEM((1, k), F32)], compiler_params=_params(dimension_semantics=("arbitrary",)),
    )(y, t)


@jax.custom_vjp
def mse_loss(y, t):
    return _loss_call(y, t)[0][0, 0]


def _mse_fwd(y, t):
    l, dy = _loss_call(y, t)
    return l[0, 0], (dy, t)


mse_loss.defvjp(_mse_fwd, lambda res, g: (g * res[0], jnp.zeros_like(res[1])))


def _scan_call(x, b, mode):
    s, w = x.shape
    nt = s // 8

    def tile_scan(t):
        row = lax.broadcasted_iota(jnp.int32, (8, w), 0)
        for sh in (1, 2, 4):
            t = t + jnp.where(row >= sh, pltpu.roll(t, sh, 0), 0.0)
        return t

    def body(x_ref, b_ref, o_ref):
        def step(i, carry):
            rows = pl.ds(pl.multiple_of(i * 8, 8), 8)
            t = x_ref[rows, :]
            if mode == "fwd":
                t = jax.nn.log_sigmoid(t + b_ref[...])
            t = tile_scan(t) + carry
            o_ref[rows, :] = t
            return t[7:8, :]

        total = lax.fori_loop(0, nt, step, jnp.zeros((1, w), F32))
        if mode == "rev":
            def fix(i, c):
                rows = pl.ds(pl.multiple_of(i * 8, 8), 8)
                o_ref[rows, :] = total - o_ref[rows, :] + x_ref[rows, :]
                return c
            lax.fori_loop(0, nt, fix, 0)

    full = pl.BlockSpec((s, w), lambda: (0, 0))
    return pl.pallas_call(
        body, name=f"scan_{mode}", in_specs=[full, pl.BlockSpec((1, w), lambda: (0, 0))], out_specs=full,
        out_shape=jax.ShapeDtypeStruct((s, w), F32), compiler_params=_params(),
    )(x, b)


def _fox_dlogit_call(x, b, dlogf):
    s, w = x.shape
    tr = _row_block(s)

    def body(x_ref, b_ref, d_ref, dx_ref, db_ref):
        @pl.when(pl.program_id(0) == 0)
        def _():
            db_ref[...] = jnp.zeros_like(db_ref)

        dx = d_ref[...] * jax.nn.sigmoid(-(x_ref[...] + b_ref[...]))
        dx_ref[...] = dx
        db_ref[...] += jnp.sum(dx, axis=0, keepdims=True)

    row = pl.BlockSpec((tr, w), lambda i: (i, 0))
    vec = pl.BlockSpec((1, w), lambda i: (0, 0))
    return pl.pallas_call(
        body, name="fox_dlogit", grid=(s // tr,), in_specs=[row, vec, row], out_specs=[row, vec],
        out_shape=[jax.ShapeDtypeStruct((s, w), F32), jax.ShapeDtypeStruct((1, w), F32)],
        compiler_params=_params(dimension_semantics=("arbitrary",)),
    )(x, b, dlogf)


@jax.custom_vjp
def fox_cum(fl, b):
    return _scan_call(fl, b, "fwd")


def _fox_cum_bwd(res, dcum):
    fl, b = res
    dlogf = _scan_call(dcum, b, "rev")
    return _fox_dlogit_call(fl, b, dlogf)


fox_cum.defvjp(lambda fl, b: (_scan_call(fl, b, "fwd"), (fl, b)), _fox_cum_bwd)


def _lane_col(x, lane_idx):
    lane = lax.broadcasted_iota(jnp.int32, (1, x.shape[1]), 1)
    return jnp.sum(jnp.where(lane == lane_idx, x, 0.0), axis=1, keepdims=True)


def _row_of(x, row_idx):
    row = lax.broadcasted_iota(jnp.int32, (x.shape[0], 1), 0)
    return jnp.sum(jnp.where(row == row_idx, x, 0.0), axis=0, keepdims=True)


def _attn_cfg(mode, s):
    if mode == "swa":
        blk = 256 if s >= 2048 else 128
        return dict(blk=blk, n_outer=SWA_KV_HEADS, pps=4, wide=False, scale=SWA_DIM ** -0.5)
    blk = 512 if s >= 2048 else 128
    if mode == "mla":
        return dict(blk=blk, n_outer=4, pps=1, wide=True, scale=(MLA_NOPE + MLA_ROPE) ** -0.5)
    return dict(blk=blk, n_outer=4, pps=1, wide=False, scale=FOX_DIM ** -0.5)


ROW_CHUNK = 32


def _unrolled(n, body, carry):
    for c in range(n):
        carry = body(c, carry)
    return carry


def _valid_rows(mode, i, jb, blk, r0, rc):
    qpos = i * blk + r0 + lax.broadcasted_iota(jnp.int32, (rc, blk), 0)
    kpos = jb * blk + lax.broadcasted_iota(jnp.int32, (rc, blk), 1)
    ok = kpos <= qpos
    if mode == "swa":
        ok = ok & (qpos - kpos < WINDOW)
    return ok


def _attn_fwd_call(mode, q, k, v, extra):
    s = q.shape[0]
    cfg = _attn_cfg(mode, s)
    blk, n_outer, pps, wide, scale = cfg["blk"], cfg["n_outer"], cfg["pps"], cfg["wide"], cfg["scale"]
    rc = ROW_CHUNK
    nq = s // blk
    swa, fox = mode == "swa", mode == "fox"
    qw = (2 * LANES if wide else LANES) * pps
    kw = 2 * LANES if wide else LANES
    ow = LANES * pps
    reps = blk // LANES

    def body(*refs):
        if not swa:
            it_ref, jt_ref = refs[:2]
            refs = refs[2:]
        q_ref, k_ref, v_ref = refs[:3]
        n_in = 3
        if fox:
            cum_ref, cumt_ref = refs[3:5]
            n_in = 5
        if swa:
            sink_ref = refs[3]
            n_in = 4
        o_ref, lse_ref, m_ref, l_ref, acc_ref, a_ref, s_all, p_all, c_all = refs[n_in:]
        p_id = pl.program_id(0)
        if swa:
            i, j = pl.program_id(1), pl.program_id(2)
            jb, run, first, last = i - 1 + j, (i - 1 + j) >= 0, j == 0, j == 1
        else:
            i, j = it_ref[pl.program_id(1)], jt_ref[pl.program_id(1)]
            jb, first, last = j, j == 0, j == i
        lane = lax.broadcasted_iota(jnp.int32, (1, LANES), 1)
        msk = [lane < HALF, lane >= HALF]

        @pl.when(first)
        def _():
            for hh in range(2 * pps):
                if swa:
                    m_ref[hh] = jnp.broadcast_to(sink_ref[hh:hh + 1, :], (blk, LANES))
                    l_ref[hh] = jnp.ones((blk, LANES), F32)
                else:
                    m_ref[hh] = jnp.full((blk, LANES), NEG, F32)
                    l_ref[hh] = jnp.zeros((blk, LANES), F32)
            acc_ref[...] = jnp.zeros_like(acc_ref)

        def process(masked):
            for pp in range(pps):
                vb = v_ref[...]
                pvs = []
                for h in range(2):
                    hh = 2 * pp + h
                    s_ref, p_ref, c_ref = s_all.at[hh], p_all.at[hh], c_all.at[hh]
                    if wide:
                        qh = q_ref[:, h * LANES:(h + 1) * LANES] * scale
                        kh = k_ref[:, h * LANES:(h + 1) * LANES]
                    else:
                        qh = jnp.where(msk[h], q_ref[:, pp * LANES:(pp + 1) * LANES], 0.0) * scale
                        kh = k_ref[...]
                    s_ref[...] = lax.dot_general(qh.astype(BF16), kh.astype(BF16), (((1,), (1,)), ((), ())),
                                                 preferred_element_type=F32)
                    if fox:
                        head = 2 * p_id + h
                        c_ref[...] = jnp.broadcast_to(_lane_col(cum_ref[...], head), (blk, LANES))
                        ck = _row_of(cumt_ref[...], head)

                    def chunk(c, carry, hh=hh, h=h):
                        r0 = c * rc
                        rows = pl.ds(r0, rc)
                        u = s_ref[rows, :]
                        if fox:
                            u = u - ck
                        if masked:
                            u = jnp.where(_valid_rows(mode, i, jb, blk, r0, rc), u, NEG)
                        m_prev, l_prev = m_ref[hh, rows, :], l_ref[hh, rows, :]
                        m_cur = jnp.max(u, axis=1, keepdims=True)
                        if fox:
                            m_cur = m_cur + c_ref[rows, :]
                        m_next = jnp.maximum(m_prev, m_cur)
                        shift = m_next - c_ref[rows, :] if fox else m_next
                        p = jnp.exp(u - jnp.tile(shift, (1, reps)))
                        alpha = jnp.exp(m_prev - m_next)
                        l_ref[hh, rows, :] = alpha * l_prev + jnp.sum(p, axis=1, keepdims=True)
                        m_ref[hh, rows, :] = m_next
                        a_ref[hh, rows, :] = alpha
                        p_ref[rows, :] = p.astype(BF16)
                        return carry

                    _unrolled(blk // rc, chunk, 0)
                    vh = jnp.where(msk[h], vb, 0.0).astype(BF16)
                    pvs.append(jnp.dot(p_ref[...], vh, preferred_element_type=F32))
                acc_ref[pp] = acc_ref[pp] * jnp.where(msk[0], a_ref[2 * pp], a_ref[2 * pp + 1]) + pvs[0] + pvs[1]

        if swa:
            pl.when(run)(lambda: process(True))
        else:
            pl.when(j < i)(lambda: process(False))
            pl.when(j == i)(lambda: process(True))

        @pl.when(last)
        def _():
            for pp in range(pps):
                l0, l1 = l_ref[2 * pp], l_ref[2 * pp + 1]
                o_ref[:, pp * LANES:(pp + 1) * LANES] = acc_ref[pp] / jnp.where(msk[0], l0, l1)
                lse_ref[:, pp * LANES:(pp + 1) * LANES] = jnp.where(
                    msk[0], m_ref[2 * pp] + jnp.log(l0), m_ref[2 * pp + 1] + jnp.log(l1))

    if swa:
        kv_map = lambda g, i, j: (jnp.maximum(i - 1 + j, 0), g)
        q_map = lambda g, i, j: (i, g)
        grid, tables, sem = (n_outer, nq, 2), [], ("parallel", "parallel", "arbitrary")
    else:
        tri = [(i, j) for i in range(nq) for j in range(i + 1)]
        tables = [jnp.asarray([t[0] for t in tri], jnp.int32), jnp.asarray([t[1] for t in tri], jnp.int32)]
        kv_map = lambda p, t, it, jt: (jt[t], p)
        q_map = lambda p, t, it, jt: (it[t], p)
        grid, sem = (n_outer, len(tri)), ("parallel", "arbitrary")
    in_specs = [pl.BlockSpec((blk, qw), q_map), pl.BlockSpec((blk, kw), kv_map), pl.BlockSpec((blk, LANES), kv_map)]
    args = [q, k, v]
    if fox:
        cum, cumt = extra
        in_specs += [pl.BlockSpec((blk, LANES), lambda p, t, it, jt: (it[t], 0)),
                     pl.BlockSpec((8, blk), lambda p, t, it, jt: (0, jt[t]))]
        args += [cum, cumt]
    if swa:
        in_specs += [pl.BlockSpec((8, LANES), lambda g, i, j: (g, 0))]
        args += [extra]
    n_pairs = n_outer * pps
    return pl.pallas_call(
        body, name=f"attn_fwd_{mode}",
        grid_spec=pltpu.PrefetchScalarGridSpec(
            num_scalar_prefetch=len(tables), grid=grid, in_specs=in_specs,
            out_specs=[pl.BlockSpec((blk, ow), q_map), pl.BlockSpec((blk, ow), q_map)],
            scratch_shapes=[pltpu.VMEM((2 * pps, blk, LANES), F32), pltpu.VMEM((2 * pps, blk, LANES), F32),
                            pltpu.VMEM((pps, blk, LANES), F32), pltpu.VMEM((2 * pps, blk, LANES), F32),
                            pltpu.VMEM((2 * pps, blk, blk), F32), pltpu.VMEM((2 * pps, blk, blk), BF16),
                            pltpu.VMEM((2 * pps, blk, LANES), F32)]),
        out_shape=[jax.ShapeDtypeStruct((s, n_pairs * LANES), F32), jax.ShapeDtypeStruct((s, n_pairs * LANES), F32)],
        compiler_params=_params(dimension_semantics=sem),
    )(*tables, *args)


def _attn_bwd_call(mode, q, k, v, extra, lse, o, do):
    s = q.shape[0]
    cfg = _attn_cfg(mode, s)
    blk, n_outer, pps, wide, scale = cfg["blk"], cfg["n_outer"], cfg["pps"], cfg["wide"], cfg["scale"]
    rc = ROW_CHUNK
    nq = s // blk
    swa, fox = mode == "swa", mode == "fox"
    qw = (2 * LANES if wide else LANES) * pps
    kw = 2 * LANES if wide else LANES
    ow = LANES * pps
    reps = blk // LANES

    assert not swa

    def body(*refs):
        jt_ref, it_ref = refs[:2]
        refs = refs[2:]
        q_ref, k_ref, v_ref, lse_ref, o_ref, do_ref = refs[:6]
        n_in = 6
        if fox:
            cum_ref, cumt_ref = refs[6:8]
            n_in = 8
        dq_ref, dk_ref, dv_ref = refs[n_in:n_in + 3]
        n_out = n_in + 3
        if fox:
            dck_ref, dcq_ref = refs[n_out:n_out + 2]
            n_out += 2
        dk_acc, dv_acc, s_all, dp_all, p_all, ds_all, e_all, d_all = refs[n_out:n_out + 8]
        if fox:
            dck_acc, rs_all = refs[n_out + 8:n_out + 10]
        p_id, t = pl.program_id(0), pl.program_id(1)
        j, ii = jt_ref[t], it_ref[t]
        i, first_i, last_i = ii, ii == j, ii == nq - 1
        lane = lax.broadcasted_iota(jnp.int32, (1, LANES), 1)
        msk = [lane < HALF, lane >= HALF]

        @pl.when(t == 0)
        def _():
            dq_ref[...] = jnp.zeros_like(dq_ref)
            if swa:
                dsink_ref[...] = jnp.zeros_like(dsink_ref)
            if fox:
                dcq_ref[...] = jnp.zeros_like(dcq_ref)

        @pl.when(first_i)
        def _():
            dk_acc[...] = jnp.zeros_like(dk_acc)
            dv_acc[...] = jnp.zeros_like(dv_acc)
            if fox:
                dck_acc[...] = jnp.zeros_like(dck_acc)

        def process(masked):
            rows = pl.ds(pl.multiple_of(i * blk, blk), blk)
            vb = v_ref[...].astype(BF16)
            dv_parts, dk_parts = [], []
            for pp in range(pps):
                psl = slice(pp * LANES, (pp + 1) * LANES)
                lse_blk, do_blk = lse_ref[:, psl], do_ref[:, psl]
                doo = do_blk * o_ref[:, psl]
                dq_pair = []
                for h in range(2):
                    hh = 2 * pp + h
                    s_ref, dp_ref, p_ref, ds_ref = s_all.at[hh], dp_all.at[hh], p_all.at[hh], ds_all.at[hh]
                    e_ref, d_ref = e_all.at[hh], d_all.at[hh]
                    if fox:
                        rs_ref = rs_all.at[hh]
                    if wide:
                        hsl = slice(h * LANES, (h + 1) * LANES)
                        qh = (q_ref[:, hsl] * scale).astype(BF16)
                        kh = k_ref[:, hsl].astype(BF16)
                    else:
                        qh = (jnp.where(msk[h], q_ref[:, psl], 0.0) * scale).astype(BF16)
                        kh = k_ref[...].astype(BF16)
                    s_ref[...] = lax.dot_general(qh, kh, (((1,), (1,)), ((), ())), preferred_element_type=F32)
                    do_h = jnp.where(msk[h], do_blk, 0.0).astype(BF16)
                    dp_ref[...] = lax.dot_general(do_h, vb, (((1,), (1,)), ((), ())), preferred_element_type=F32)
                    lse_h = _lane_col(lse_blk, HALF * h)
                    d_h = jnp.sum(jnp.where(msk[h], doo, 0.0), axis=1, keepdims=True)
                    e_ref[...] = jnp.broadcast_to(lse_h, (blk, LANES))
                    d_ref[...] = jnp.broadcast_to(d_h, (blk, LANES))
                    if fox:
                        head = 2 * p_id + h
                        e_ref[...] = e_ref[...] - jnp.broadcast_to(_lane_col(cum_ref[...], head), (blk, LANES))
                        ck = _row_of(cumt_ref[...], head)

                    def chunk(c, colsum):
                        r0 = c * rc
                        cr = pl.ds(r0, rc)
                        u = s_ref[cr, :]
                        if fox:
                            u = u - ck
                        p = jnp.exp(u - jnp.tile(e_ref[cr, :], (1, reps)))
                        if masked:
                            p = jnp.where(_valid_rows(mode, i, j, blk, r0, rc), p, 0.0)
                        ds = p * (dp_ref[cr, :] - jnp.tile(d_ref[cr, :], (1, reps)))
                        p_ref[cr, :] = p.astype(BF16)
                        ds_ref[cr, :] = ds.astype(BF16)
                        if fox:
                            colsum = colsum + jnp.sum(ds, axis=0, keepdims=True)
                            rs_ref[cr, :] = jnp.broadcast_to(jnp.sum(ds, axis=1, keepdims=True), (rc, LANES))
                        return colsum

                    colsum = _unrolled(blk // rc, chunk, jnp.zeros((1, blk), F32))
                    dv_parts.append(lax.dot_general(p_ref[...], do_h, (((0,), (0,)), ((), ())),
                                                    preferred_element_type=F32))
                    if fox:
                        dck_acc[h:h + 1, :] += -colsum
                        dcq_ref[rows, :] += jnp.where(msk[h], rs_ref[...], 0.0)
                    dq_h = jnp.dot(ds_ref[...], kh, preferred_element_type=F32) * scale
                    dk_h = lax.dot_general(ds_ref[...], qh, (((0,), (0,)), ((), ())), preferred_element_type=F32)
                    if wide:
                        dq_ref[rows, hsl] += dq_h
                        dk_acc[:, hsl] += dk_h
                    else:
                        dq_pair.append(jnp.where(msk[h], dq_h, 0.0))
                        dk_parts.append(dk_h)
                if not wide:
                    dq_ref[rows, psl] += dq_pair[0] + dq_pair[1]
            dv_acc[...] += functools.reduce(lambda a, b: a + b, dv_parts)
            if not wide:
                dk_acc[...] += functools.reduce(lambda a, b: a + b, dk_parts)

        pl.when(ii > j)(lambda: process(False))
        pl.when(ii == j)(lambda: process(True))

        @pl.when(last_i)
        def _():
            dk_ref[...] = dk_acc[...]
            dv_ref[...] = dv_acc[...]
            if fox:
                dck_ref[0] = dck_acc[...]

    tri = [(j, i) for j in range(nq) for i in range(j, nq)]
    tables = [jnp.asarray([t[0] for t in tri], jnp.int32), jnp.asarray([t[1] for t in tri], jnp.int32)]
    q_map = lambda p, t, jt, it: (it[t], p)
    kv_map = lambda p, t, jt, it: (jt[t], p)
    in_specs = [pl.BlockSpec((blk, qw), q_map), pl.BlockSpec((blk, kw), kv_map), pl.BlockSpec((blk, LANES), kv_map),
                pl.BlockSpec((blk, ow), q_map), pl.BlockSpec((blk, ow), q_map), pl.BlockSpec((blk, ow), q_map)]
    args = [q, k, v, lse, o, do]
    n_pairs = n_outer * pps
    out_specs = [pl.BlockSpec((s, qw), lambda p, t, jt, it: (0, p)), pl.BlockSpec((blk, kw), kv_map),
                 pl.BlockSpec((blk, LANES), kv_map)]
    out_shape = [jax.ShapeDtypeStruct((s, q.shape[1]), F32), jax.ShapeDtypeStruct((s, k.shape[1]), F32),
                 jax.ShapeDtypeStruct((s, v.shape[1]), F32)]
    nh = 2 * pps
    scratch = [pltpu.VMEM((blk, kw), F32), pltpu.VMEM((blk, LANES), F32), pltpu.VMEM((nh, blk, blk), F32),
               pltpu.VMEM((nh, blk, blk), F32), pltpu.VMEM((nh, blk, blk), BF16), pltpu.VMEM((nh, blk, blk), BF16),
               pltpu.VMEM((nh, blk, LANES), F32), pltpu.VMEM((nh, blk, LANES), F32)]
    if fox:
        cum, cumt = extra
        in_specs += [pl.BlockSpec((blk, LANES), lambda p, t, jt, it: (it[t], 0)),
                     pl.BlockSpec((8, blk), lambda p, t, jt, it: (0, jt[t]))]
        args += [cum, cumt]
        out_specs += [pl.BlockSpec((1, 8, blk), lambda p, t, jt, it: (p, 0, jt[t])),
                      pl.BlockSpec((s, LANES), lambda p, t, jt, it: (0, p))]
        out_shape += [jax.ShapeDtypeStruct((n_pairs, 8, s), F32), jax.ShapeDtypeStruct((s, n_pairs * LANES), F32)]
        scratch += [pltpu.VMEM((8, blk), F32), pltpu.VMEM((nh, blk, LANES), F32)]
    return pl.pallas_call(
        body, name=f"attn_bwd_{mode}",
        grid_spec=pltpu.PrefetchScalarGridSpec(num_scalar_prefetch=2, grid=(n_outer, len(tri)), in_specs=in_specs,
                                               out_specs=out_specs, scratch_shapes=scratch),
        out_shape=out_shape, compiler_params=_params(dimension_semantics=("parallel", "arbitrary")),
    )(*tables, *args)


def _swa_masks(i, blk):
    r = lax.broadcasted_iota(jnp.int32, (blk, blk), 0)
    c = lax.broadcasted_iota(jnp.int32, (blk, blk), 1)
    return (c > r) & (i > 0), c <= r


def _nt(a, b):
    return lax.dot_general(a, b, (((1,), (1,)), ((), ())), preferred_element_type=F32)


def _tn(a, b):
    return lax.dot_general(a, b, (((0,), (0,)), ((), ())), preferred_element_type=F32)


def _swa_bwd_call(q, k, v, sink, lse, o, do):
    s = q.shape[0]
    blk, pps, scale = WINDOW, 4, SWA_DIM ** -0.5
    nq = s // blk

    def body(q_ref, kp_ref, ko_ref, vp_ref, vo_ref, sink_ref, lse_ref, o_ref, do_ref,
             dq_ref, dk_ref, dv_ref, dsink_ref, ck_ref, cv_ref):
        i = pl.program_id(1)
        lane = lax.broadcasted_iota(jnp.int32, (1, LANES), 1)
        msk = [lane < HALF, lane >= HALF]

        @pl.when(i == 0)
        def _():
            ck_ref[...] = jnp.zeros_like(ck_ref)
            cv_ref[...] = jnp.zeros_like(cv_ref)
            dsink_ref[...] = jnp.zeros_like(dsink_ref)

        @pl.when(i < nq)
        def _():
            ok_prev, ok_own = _swa_masks(i, blk)
            kp, ko = kp_ref[...].astype(BF16), ko_ref[...].astype(BF16)
            vp, vo = vp_ref[...].astype(BF16), vo_ref[...].astype(BF16)
            dkp, dko, dvp, dvo = [], [], [], []
            for pp in range(pps):
                psl = slice(pp * LANES, (pp + 1) * LANES)
                qp, do_blk = q_ref[:, psl], do_ref[:, psl]
                doo = do_blk * o_ref[:, psl]
                dqs = []
                for h in range(2):
                    hh = 2 * pp + h
                    qh = (jnp.where(msk[h], qp, 0.0) * scale).astype(BF16)
                    lse_h = jnp.broadcast_to(_lane_col(lse_ref[:, psl], HALF * h), (blk, LANES))
                    d_h = jnp.broadcast_to(jnp.sum(jnp.where(msk[h], doo, 0.0), axis=1, keepdims=True), (blk, LANES))
                    p_p = jnp.where(ok_prev, jnp.exp(_nt(qh, kp) - lse_h), 0.0)
                    p_o = jnp.where(ok_own, jnp.exp(_nt(qh, ko) - lse_h), 0.0)
                    do_h = jnp.where(msk[h], do_blk, 0.0).astype(BF16)
                    ds_p = (p_p * (_nt(do_h, vp) - d_h)).astype(BF16)
                    ds_o = (p_o * (_nt(do_h, vo) - d_h)).astype(BF16)
                    dq_h = (jnp.dot(ds_p, kp, preferred_element_type=F32)
                            + jnp.dot(ds_o, ko, preferred_element_type=F32)) * scale
                    dqs.append(jnp.where(msk[h], dq_h, 0.0))
                    dkp.append(_tn(ds_p, qh))
                    dko.append(_tn(ds_o, qh))
                    dvp.append(_tn(p_p.astype(BF16), do_h))
                    dvo.append(_tn(p_o.astype(BF16), do_h))
                    sink_row = sink_ref[hh:hh + 1, :]
                    dsink_ref[hh:hh + 1, :] += -jnp.sum(jnp.exp(sink_row - lse_h) * d_h, axis=0, keepdims=True)
                dq_ref[:, psl] = dqs[0] + dqs[1]
            total = lambda parts: functools.reduce(lambda a, b: a + b, parts)
            dk_ref[...] = ck_ref[...] + total(dkp)
            dv_ref[...] = cv_ref[...] + total(dvp)
            ck_ref[...] = total(dko)
            cv_ref[...] = total(dvo)

        @pl.when(i == nq)
        def _():
            dk_ref[...] = ck_ref[...]
            dv_ref[...] = cv_ref[...]

    last = nq - 1
    prev = lambda g, i: (jnp.maximum(i - 1, 0), g)
    own = lambda g, i: (jnp.minimum(i, last), g)
    qspec = pl.BlockSpec((blk, pps * LANES), own)
    kspec = lambda m: pl.BlockSpec((blk, LANES), m)
    sspec = pl.BlockSpec((8, LANES), lambda g, i: (g, 0))
    return pl.pallas_call(
        body, name="swa_bwd", grid=(SWA_KV_HEADS, nq + 1),
        in_specs=[qspec, kspec(prev), kspec(own), kspec(prev), kspec(own), sspec, qspec, qspec, qspec],
        out_specs=[qspec, kspec(prev), kspec(prev), sspec],
        out_shape=[jax.ShapeDtypeStruct(q.shape, F32), jax.ShapeDtypeStruct(k.shape, F32),
                   jax.ShapeDtypeStruct(v.shape, F32), jax.ShapeDtypeStruct((SWA_HEADS, LANES), F32)],
        scratch_shapes=[pltpu.VMEM((blk, LANES), F32), pltpu.VMEM((blk, LANES), F32)],
        compiler_params=_params(dimension_semantics=("parallel", "arbitrary")),
    )(q, k, k, v, v, sink, lse, o, do)


def _make_attn(mode):
    swa = mode == "swa"

    @jax.custom_vjp
    def attn(q, k, v, extra):
        return fwd(q, k, v, extra)[0]

    def fwd(q, k, v, extra):
        o, lse = _attn_fwd_call(mode, q, k, v, extra)
        return o, (q, k, v, extra, o, lse)

    def bwd(res, do):
        q, k, v, extra, o, lse = res
        outs = (_swa_bwd_call(q, k, v, extra, lse, o, do) if swa
                else _attn_bwd_call(mode, q, k, v, extra, lse, o, do))
        dq, dk, dv = outs[:3]
        if mode == "fox":
            cum, cumt = extra
            dck = outs[3]
            dcumt = dck[:, :2, :].reshape(FOX_HEADS, -1)
            dcq = outs[4].reshape(-1, FOX_HEADS, HALF)[:, :, 0]
            dextra = (jnp.pad(dcq, ((0, 0), (0, LANES - FOX_HEADS))), dcumt)
        elif mode == "swa":
            dextra = jnp.where(jnp.arange(LANES)[None, :] == 0, outs[3], 0.0)
        else:
            dextra = None
        return dq, dk, dv, dextra

    attn.defvjp(fwd, bwd)
    return attn


attn_mla = _make_attn("mla")
attn_fox = _make_attn("fox")
attn_swa = _make_attn("swa")


def _ukv_layout(w):
    r = w.shape[0]
    w3 = w.reshape(r, MLA_HEADS, MLA_NOPE + MLA_V)
    wk = jnp.pad(w3[:, :, :MLA_NOPE], ((0, 0), (0, 0), (0, LANES - MLA_NOPE))).reshape(r, MLA_HEADS * LANES)
    wv = w3[:, :, MLA_NOPE:].reshape(r, MLA_HEADS * MLA_V)
    return wk, wv


def _even_layer(x, w_in_cat, q_norm, w_uq_p, kv_norm, w_ukv, b_f, w_out, ln_g, ln_b, tabs_mla):
    tabs, hs = tabs_mla
    cq, ckv, kpe, fq, fk, fv, fl, gate = even_in_proj(x, relayout(w_in_cat, "even"), tabs)
    q = mm_rope(rms_norm(cq, q_norm), w_uq_p, tabs, hs)
    ckvn = rms_norm(ckv, kv_norm)
    wk, wv = _ukv_layout(w_ukv)
    kk = mm(ckvn, wk) + jnp.tile(kpe, (1, MLA_HEADS))
    o_mla = attn_mla(q, kk, mm(ckvn, wv), None)
    cum = fox_cum(fl, jnp.pad(b_f, (0, LANES - FOX_HEADS)).reshape(1, LANES))
    o_fox = attn_fox(fq, fk, fv, (cum, cum[:, :8].T))
    y = gated_mm((o_mla, o_fox), gate, w_out)
    return ln_res(x, y, ln_g, ln_b)


def _odd_layer(x, w_in_cat, sinks, w_out, ln_g, ln_b, tabs_swa):
    q, kd, vd, gate = odd_in_proj(x, relayout(w_in_cat, "odd"), tabs_swa[0])
    o = attn_swa(q, kd, vd, jnp.broadcast_to(sinks[:, None], (SWA_HEADS, LANES)))
    y = gated_mm((o,), gate, w_out)
    return ln_res(x, y, ln_g, ln_b)


EVEN_SHARDED = ["even_w_in", "even_w_uq", "even_w_ukv", "even_w_out"]
ODD_SHARDED = ["odd_w_in", "odd_w_out", "odd_ln_g", "odd_ln_b"]
EVEN_REPL = ["even_q_norm", "even_kv_norm", "even_b_f", "even_ln_g", "even_ln_b"]
ODD_REPL = ["odd_sinks"]


def _layer_names(layer):
    return (EVEN_SHARDED, EVEN_REPL) if layer % 2 == 0 else (ODD_SHARDED, ODD_REPL)


def _layer_of(name, j):
    return 2 * j if name.startswith("even") else 2 * j + 1


def _layer_apply(layer, p, x, tabs):
    if layer % 2 == 0:
        return _even_layer(x, p["even_w_in"], p["even_q_norm"], p["even_w_uq"], p["even_kv_norm"], p["even_w_ukv"],
                           p["even_b_f"], p["even_w_out"], p["even_ln_g"], p["even_ln_b"], tabs["mla"])
    return _odd_layer(x, p["odd_w_in"], p["odd_sinks"], p["odd_w_out"], p["odd_ln_g"], p["odd_ln_b"], tabs["swa"])


def _pad_rows(flat, mult):
    n = flat.shape[-1]
    per = mult * LANES
    padded = -(-n // per) * per
    if padded != n:
        flat = jnp.pad(flat, [(0, 0)] * (flat.ndim - 1) + [(0, padded - n)])
    return flat.reshape(flat.shape[:-1] + (padded // LANES, LANES))


def _pad_last(a, width):
    if a.shape[-1] == width:
        return a
    return jnp.pad(a, [(0, 0)] * (a.ndim - 1) + [(0, width - a.shape[-1])])


def _join(slots, axis):
    shp = list(slots.shape[1:])
    shp[axis] *= N_DEV
    return jnp.moveaxis(slots, 0, axis).reshape(shp)


def _split(full, axis):
    shp = full.shape
    t = full.reshape(shp[:axis] + (N_DEV, shp[axis] // N_DEV) + shp[axis + 1:])
    return jnp.moveaxis(t, axis, 0)


PAD_TO = {"even_w_in": SHARD_PAD, "even_w_uq": LANES, "odd_w_in": SHARD_PAD}


def kernel(x, even_w_in, even_q_norm, even_w_uq, even_kv_norm, even_w_ukv, even_b_f, even_w_out, even_ln_g, even_ln_b, odd_w_in, odd_sinks, odd_w_out, odd_ln_g, odd_ln_b, loss_target, m_even_w_in, m_even_q_norm, m_even_w_uq, m_even_kv_norm, m_even_w_ukv, m_even_b_f, m_even_w_out, m_even_ln_g, m_even_ln_b, m_odd_w_in, m_odd_sinks, m_odd_w_out, m_odd_ln_g, m_odd_ln_b, v_even_w_in, v_even_q_norm, v_even_w_uq, v_even_kv_norm, v_even_w_ukv, v_even_b_f, v_even_w_out, v_even_ln_g, v_even_ln_b, v_odd_w_in, v_odd_sinks, v_odd_w_out, v_odd_ln_g, v_odd_ln_b):
    w = dict(even_w_in=even_w_in, even_q_norm=even_q_norm, even_w_uq=even_w_uq, even_kv_norm=even_kv_norm,
             even_w_ukv=even_w_ukv, even_b_f=even_b_f, even_w_out=even_w_out, even_ln_g=even_ln_g, even_ln_b=even_ln_b,
             odd_w_in=odd_w_in, odd_sinks=odd_sinks, odd_w_out=odd_w_out, odd_ln_g=odd_ln_g, odd_ln_b=odd_ln_b)
    mom = dict(even_w_in=m_even_w_in, even_q_norm=m_even_q_norm, even_w_uq=m_even_w_uq, even_kv_norm=m_even_kv_norm,
               even_w_ukv=m_even_w_ukv, even_b_f=m_even_b_f, even_w_out=m_even_w_out, even_ln_g=m_even_ln_g,
               even_ln_b=m_even_ln_b, odd_w_in=m_odd_w_in, odd_sinks=m_odd_sinks, odd_w_out=m_odd_w_out,
               odd_ln_g=m_odd_ln_g, odd_ln_b=m_odd_ln_b)
    vel = dict(even_w_in=v_even_w_in, even_q_norm=v_even_q_norm, even_w_uq=v_even_w_uq, even_kv_norm=v_even_kv_norm,
               even_w_ukv=v_even_w_ukv, even_b_f=v_even_b_f, even_w_out=v_even_w_out, even_ln_g=v_even_ln_g,
               even_ln_b=v_even_ln_b, odd_w_in=v_odd_w_in, odd_sinks=v_odd_sinks, odd_w_out=v_odd_w_out,
               odd_ln_g=v_odd_ln_g, odd_ln_b=v_odd_ln_b)
    sharded = BIG + SMALL_SHARDED
    padded = lambda d, n: _pad_last(d[n], PAD_TO.get(n, d[n].shape[-1]))

    tabs = {"mla": _rope_tables(x.shape[1], "mla"), "swa": _rope_tables(x.shape[1], "swa")}
    keys = lambda layers: [(n, layer // 2) for layer in layers for n in _layer_names(layer)[0]]
    first, rest = keys([0]), keys([1, 2, 3])
    me = _lin(_me())

    def shard(n, j):
        a = padded(w, n)[j]
        return a.astype(BF16) if n in BIG else a

    to_full = lambda n, g: _join(g, SHARD_AXIS[n] - 1).astype(F32)
    to_slots = lambda n, g: _split(g, SHARD_AXIS[n] - 1).astype(BF16 if n in BIG else F32)

    def layer_params(layer, full_of):
        shn, rpn = _layer_names(layer)
        p = {n: full_of(n) for n in shn}
        p.update({n: w[n][layer // 2] for n in rpn})
        return p

    got0 = _all_gather([shard(n, j) for n, j in first], "all_gather_first")
    got0, mine_rest = lax.optimization_barrier((got0, [shard(n, j) for n, j in rest]))
    got0 = dict(zip(first, got0))
    send_sems, recv_sems, srcs, lands, token = _split_start(mine_rest, False, "all_gather_rest_start")
    p0 = layer_params(0, lambda n: to_full(n, got0[(n, 0)]) + token[0, 0])
    x1, vjp0 = jax.vjp(lambda p, xx: _layer_apply(0, p, xx, tabs), p0, x[0])
    got = _split_wait(send_sems, recv_sems, srcs, lands, x1, False, "all_gather_rest_wait")
    got = dict(zip(rest, _own_slot(got, [m[None] for m in mine_rest])))

    xs, vjps = x1, [vjp0]
    for layer in (1, 2, 3):
        p = layer_params(layer, lambda n: to_full(n, got[(n, layer // 2)]))
        xs, vjp = jax.vjp(lambda p_, xx, layer=layer: _layer_apply(layer, p_, xx, tabs), p, xs)
        vjps.append(vjp)
    loss_local, vjp_loss = jax.vjp(lambda y: mse_loss(y, loss_target[0]), xs)
    (dy,) = vjp_loss(jnp.ones((), F32))
    loss = lax.psum(loss_local, AXES)
    grads = {}
    for layer in (3, 2, 1):
        grads[layer], dy = vjps[layer](dy)

    parts_rest = [to_slots(n, grads[_layer_of(n, j)][n]) for n, j in rest]
    send_sems, recv_sems, srcs, lands, token = _split_start(parts_rest, True, "grad_exchange_rest_start")
    grads[0], grad_x = vjps[0](dy + token[0, 0])
    recv_rest = _split_wait(send_sems, recv_sems, srcs, lands, grad_x, True, "grad_exchange_rest_wait")
    recv = dict(zip(rest, _own_slot(recv_rest, [lax.dynamic_slice_in_dim(p, me, 1, axis=0) for p in parts_rest])))
    repl_grad = lambda n: jnp.stack([grads[_layer_of(n, j)][n] for j in (0, 1)])
    repl_rows = _pad_rows(jnp.concatenate([repl_grad(n).reshape(-1) for n in REPL]), 8)
    parts_last = [to_slots(n, grads[0][n]) for n, j in first]
    parts_last.append(jnp.broadcast_to(repl_rows[None], (N_DEV,) + repl_rows.shape))
    recv_last = _exchange(parts_last, "grad_exchange_last")
    recv.update(zip(first, recv_last[:-1]))

    g_out, d_out, m_out, v_out = {}, {}, {}, {}
    for n in sharded:
        r = jnp.stack([recv[(n, 0)], recv[(n, 1)]], axis=1)
        cols = r.shape[-1]
        flat2 = lambda d: padded(d, n).reshape(-1, cols)
        outs = _sum_adamw(r.reshape(N_DEV, -1, cols), flat2(w), flat2(mom), flat2(vel))
        for dst, o in zip((g_out, d_out, m_out, v_out), outs):
            dst[n] = o.reshape(w[n].shape[:-1] + (cols,))[..., :w[n].shape[-1]]
    pack = lambda d: _pad_rows(jnp.concatenate([d[n].reshape(-1) for n in REPL]), 8)
    outs = _sum_adamw(recv_last[-1], pack(w), pack(mom), pack(vel))
    for dst, o in zip((g_out, d_out, m_out, v_out), outs):
        flat, off = o.reshape(-1), 0
        for n in REPL:
            size = math.prod(w[n].shape)
            dst[n] = flat[off:off + size].reshape(w[n].shape)
            off += size
    return (loss, grad_x[None], *[g_out[n] for n in WEIGHTS], *[d_out[n] for n in WEIGHTS],
            *[m_out[n] for n in WEIGHTS], *[v_out[n] for n in WEIGHTS])
```

```python
import functools
import math

import jax
import jax.numpy as jnp
from jax import lax
from jax.experimental import pallas as pl
from jax.experimental.pallas import tpu as pltpu

F32 = jnp.float32
BF16 = jnp.bfloat16
LANES = 128
HALF = 64
N_DEV = 8
AXES = ("x", "y", "c")
VMEM_LIMIT = 48 * 1024 * 1024
VMEM_LIMIT_TALL = 56 * 1024 * 1024

D_MODEL = 1024
DEPTH = 4
ROPE_THETA = 10000.0
MLA_HEADS, MLA_NOPE, MLA_ROPE, MLA_V, MLA_Q_RANK, MLA_KV_RANK = 8, 64, 32, 64, 256, 128
FOX_HEADS, FOX_DIM = 8, 64
SWA_HEADS, SWA_KV_HEADS, SWA_DIM, WINDOW = 16, 2, 64, 128
RMS_EPS, LN_EPS = 1e-6, 1e-5
ALPHA = (2 * DEPTH) ** 0.25
ADAM_LR, ADAM_B1, ADAM_B2, ADAM_EPS, ADAM_WD, ADAM_STEP = 0.001, 0.9, 0.999, 1e-08, 0.01, 10
NEG = -1e30

WEIGHTS = ["even_w_in", "even_q_norm", "even_w_uq", "even_kv_norm", "even_w_ukv", "even_b_f", "even_w_out",
           "even_ln_g", "even_ln_b", "odd_w_in", "odd_sinks", "odd_w_out", "odd_ln_g", "odd_ln_b"]
SHARD_AXIS = {"even_w_in": 2, "even_w_uq": 2, "even_w_ukv": 2, "even_w_out": 1, "odd_w_in": 2, "odd_w_out": 1,
              "odd_ln_g": 1, "odd_ln_b": 1, "even_q_norm": None, "even_kv_norm": None, "even_b_f": None,
              "even_ln_g": None, "even_ln_b": None, "odd_sinks": None}
BIG = ["even_w_in", "even_w_uq", "even_w_ukv", "even_w_out", "odd_w_in", "odd_w_out"]
SMALL_SHARDED = ["odd_ln_g", "odd_ln_b"]
REPL = [n for n in WEIGHTS if SHARD_AXIS[n] is None]


def _pick(n, cands):
    for c in cands:
        if n % c == 0:
            return c
    return n


def _params(**kw):
    return pltpu.CompilerParams(vmem_limit_bytes=VMEM_LIMIT, **kw)


def _me():
    return lax.axis_index("x"), lax.axis_index("y"), lax.axis_index("c")


def _peer(k):
    x, y, c = _me()
    px = 1 - x if (k >> 2) & 1 else x
    py = 1 - y if (k >> 1) & 1 else y
    pc = 1 - c if k & 1 else c
    return px, py, pc


def _lin(p):
    return 4 * p[0] + 2 * p[1] + p[2]


def _comm_call(body, n, out_shape, args, name):
    any_spec = pl.BlockSpec(memory_space=pl.ANY)
    return pl.pallas_call(
        body, name=name, out_shape=out_shape, in_specs=[any_spec] * n, out_specs=[any_spec] * n,
        scratch_shapes=[pltpu.SemaphoreType.DMA((n, N_DEV - 1)), pltpu.SemaphoreType.DMA((n, N_DEV - 1)),
                        pltpu.SemaphoreType.DMA((n,))],
    )(*args)


def _all_gather(xs, name):
    n = len(xs)

    def body(*refs):
        x_refs, out_refs = refs[:n], refs[n:2 * n]
        send_sems, recv_sems, local_sems = refs[2 * n:]
        me = _lin(_me())
        local = [pltpu.make_async_copy(x_refs[a], out_refs[a].at[me], local_sems.at[a]) for a in range(n)]
        for cp in local:
            cp.start()
        sends = []
        for k in range(1, N_DEV):
            for a in range(n):
                cp = pltpu.make_async_remote_copy(
                    src_ref=x_refs[a], dst_ref=out_refs[a].at[me], send_sem=send_sems.at[a, k - 1],
                    recv_sem=recv_sems.at[a, k - 1], device_id=_peer(k), device_id_type=pl.DeviceIdType.MESH)
                cp.start()
                sends.append(cp)
        for k in range(1, N_DEV):
            for a in range(n):
                pltpu.make_async_remote_copy(
                    src_ref=x_refs[a], dst_ref=out_refs[a].at[_lin(_peer(k))], send_sem=send_sems.at[a, k - 1],
                    recv_sem=recv_sems.at[a, k - 1], device_id=_peer(k),
                    device_id_type=pl.DeviceIdType.MESH).wait_recv()
        for cp in sends:
            cp.wait_send()
        for cp in local:
            cp.wait()

    out_shape = [jax.ShapeDtypeStruct((N_DEV,) + x.shape, x.dtype) for x in xs]
    return _comm_call(body, n, out_shape, xs, name)


def _exchange(parts, name):
    n = len(parts)

    def body(*refs):
        p_refs, out_refs = refs[:n], refs[n:2 * n]
        send_sems, recv_sems, local_sems = refs[2 * n:]
        me = _lin(_me())
        local = [pltpu.make_async_copy(p_refs[a].at[me], out_refs[a].at[me], local_sems.at[a]) for a in range(n)]
        for cp in local:
            cp.start()
        sends = []
        for k in range(1, N_DEV):
            peer = _peer(k)
            for a in range(n):
                cp = pltpu.make_async_remote_copy(
                    src_ref=p_refs[a].at[_lin(peer)], dst_ref=out_refs[a].at[me], send_sem=send_sems.at[a, k - 1],
                    recv_sem=recv_sems.at[a, k - 1], device_id=peer, device_id_type=pl.DeviceIdType.MESH)
                cp.start()
                sends.append(cp)
        for k in range(1, N_DEV):
            peer = _peer(k)
            for a in range(n):
                pltpu.make_async_remote_copy(
                    src_ref=p_refs[a].at[_lin(peer)], dst_ref=out_refs[a].at[_lin(peer)],
                    send_sem=send_sems.at[a, k - 1], recv_sem=recv_sems.at[a, k - 1], device_id=peer,
                    device_id_type=pl.DeviceIdType.MESH).wait_recv()
        for cp in sends:
            cp.wait_send()
        for cp in local:
            cp.wait()

    out_shape = [jax.ShapeDtypeStruct(p.shape, p.dtype) for p in parts]
    return _comm_call(body, n, out_shape, parts, name)


_HBM = pl.BlockSpec(memory_space=pltpu.HBM)
_SEM = pl.BlockSpec(memory_space=pltpu.SEMAPHORE)
_EFFECT = pltpu.SideEffectType.DATAFLOW_SIDE_EFFECTING


def _split_start(srcs, slotted, name):
    n = len(srcs)
    lands = [lax.empty(s.shape if slotted else (N_DEV,) + s.shape, s.dtype) for s in srcs]

    def body(*refs):
        src_refs, land_refs = refs[:n], refs[n:2 * n]
        send_sems, recv_sems, token = refs[2 * n], refs[2 * n + 1], refs[-1]
        me = _lin(_me())
        for k in range(1, N_DEV):
            peer = _peer(k)
            for a in range(n):
                pltpu.make_async_remote_copy(
                    src_ref=src_refs[a].at[_lin(peer)] if slotted else src_refs[a], dst_ref=land_refs[a].at[me],
                    send_sem=send_sems.at[a * (N_DEV - 1) + k - 1], recv_sem=recv_sems.at[a * (N_DEV - 1) + k - 1],
                    device_id=peer, device_id_type=pl.DeviceIdType.MESH).start()
        token[...] = jnp.zeros_like(token)

    both = list(srcs) + lands
    outs = pl.pallas_call(
        body, name=name,
        out_shape=(pltpu.SemaphoreType.DMA((n * (N_DEV - 1),)), pltpu.SemaphoreType.DMA((n * (N_DEV - 1),)),
                   *[pltpu.HBM(b.shape, b.dtype) for b in both], jax.ShapeDtypeStruct((8, LANES), F32)),
        in_specs=[_HBM] * (2 * n), out_specs=(_SEM, _SEM, *[_HBM] * (2 * n), pl.BlockSpec(memory_space=pltpu.VMEM)),
        input_output_aliases={a: 2 + a for a in range(2 * n)},
        compiler_params=pltpu.CompilerParams(has_side_effects=_EFFECT),
    )(*[pltpu.with_memory_space_constraint(b, pltpu.HBM) for b in both])
    return outs[0], outs[1], list(outs[2:2 + n]), list(outs[2 + n:2 + 2 * n]), outs[-1]


def _split_wait(send_sems, recv_sems, srcs, lands, after, slotted, name):
    n = len(srcs)

    def body(*refs):
        src_refs, land_refs = refs[:n], refs[n:2 * n]
        send_sems, recv_sems = refs[2 * n], refs[2 * n + 1]
        for k in range(1, N_DEV):
            peer = _peer(k)
            for a in range(n):
                cp = pltpu.make_async_remote_copy(
                    src_ref=src_refs[a].at[_lin(peer)] if slotted else src_refs[a],
                    dst_ref=land_refs[a].at[_lin(peer)], send_sem=send_sems.at[a * (N_DEV - 1) + k - 1],
                    recv_sem=recv_sems.at[a * (N_DEV - 1) + k - 1], device_id=peer,
                    device_id_type=pl.DeviceIdType.MESH)
                cp.wait_send()
                cp.wait_recv()

    both = list(srcs) + list(lands)
    outs = pl.pallas_call(
        body, name=name, out_shape=[pltpu.HBM(b.shape, b.dtype) for b in both],
        in_specs=[_HBM] * (2 * n) + [_SEM, _SEM, pl.BlockSpec(memory_space=pl.ANY)], out_specs=[_HBM] * (2 * n),
        input_output_aliases={a: a for a in range(2 * n)},
        compiler_params=pltpu.CompilerParams(has_side_effects=_EFFECT),
    )(*both, send_sems, recv_sems, after)
    return list(outs[n:])


def _own_slot(lands, own):
    me = _lin(_me())
    slot = lambda l: lax.broadcasted_iota(jnp.int32, (N_DEV,) + (1,) * (l.ndim - 1), 0)
    return [jnp.where(slot(l) == me, o.astype(l.dtype), l) for l, o in zip(lands, own)]


def _sum_adamw(recv, w, m, v):
    _, rows, lanes = recv.shape
    tr = _pick(rows, (256, 128, 64, 32, 16, 8))
    c1 = 1.0 - ADAM_B1 ** ADAM_STEP
    c2 = 1.0 - ADAM_B2 ** ADAM_STEP

    def body(r_ref, w_ref, m_ref, v_ref, g_out, d_out, m_out, v_out):
        g = r_ref[0].astype(F32)
        for s in range(1, N_DEV):
            g = g + r_ref[s].astype(F32)
        mn = ADAM_B1 * m_ref[...] + (1.0 - ADAM_B1) * g
        vn = ADAM_B2 * v_ref[...] + (1.0 - ADAM_B2) * (g * g)
        m_hat = mn / c1
        v_hat = vn / c2
        g_out[...] = g
        d_out[...] = -ADAM_LR * (m_hat / (jnp.sqrt(v_hat) + ADAM_EPS) + ADAM_WD * w_ref[...])
        m_out[...] = mn
        v_out[...] = vn

    blk = pl.BlockSpec((tr, lanes), lambda i: (i, 0))
    shp = jax.ShapeDtypeStruct((rows, lanes), F32)
    return pl.pallas_call(
        body, name=f"sum_adamw_{rows}x{lanes}", grid=(rows // tr,),
        in_specs=[pl.BlockSpec((N_DEV, tr, lanes), lambda i: (0, i, 0)), blk, blk, blk],
        out_specs=[blk, blk, blk, blk], out_shape=[shp, shp, shp, shp],
        compiler_params=_params(dimension_semantics=("parallel",)),
    )(recv, w, m, v)


def _rope_block(xb, cv, s1v, s2v, hs):
    return xb * cv + pltpu.roll(xb, LANES - hs, 1) * s1v + pltpu.roll(xb, hs, 1) * s2v


def _mm_nn(a, b, b_transposed=False, rope=None):
    m, k = a.shape
    n = b.shape[0] if b_transposed else b.shape[1]
    tm = _pick(m, (1024, 512, 256, 128))
    tn = _pick(n, (1024, 640, 512, 256, 128))
    tk = _pick(k, (1024, 640, 512, 256, 128))
    nk = k // tk

    def body(*refs):
        a_ref, b_ref = refs[:2]
        o_ref, acc_ref = refs[-2:]
        kk = pl.program_id(2)

        @pl.when(kk == 0)
        def _():
            acc_ref[...] = jnp.zeros_like(acc_ref)

        dims = (((1,), (1,)), ((), ())) if b_transposed else (((1,), (0,)), ((), ()))
        acc_ref[...] += lax.dot_general(a_ref[...].astype(BF16), b_ref[...].astype(BF16), dims,
                                        preferred_element_type=F32)

        @pl.when(kk == nk - 1)
        def _():
            if rope is None:
                o_ref[...] = acc_ref[...]
            else:
                cv, s1v, s2v = refs[2][...], refs[3][...], refs[4][...]
                for cb in range(tn // LANES):
                    sl = slice(cb * LANES, (cb + 1) * LANES)
                    o_ref[:, sl] = _rope_block(acc_ref[:, sl], cv, s1v, s2v, rope[3])

    b_spec = (pl.BlockSpec((tn, tk), lambda i, j, kk: (j, kk)) if b_transposed
              else pl.BlockSpec((tk, tn), lambda i, j, kk: (kk, j)))
    in_specs, args = [pl.BlockSpec((tm, tk), lambda i, j, kk: (i, kk)), b_spec], [a, b]
    if rope is not None:
        in_specs += [pl.BlockSpec((tm, LANES), lambda i, j, kk: (i, 0))] * 3
        args += list(rope[:3])
    tag = ("t" if b_transposed else "n") + ("" if rope is None else f"_rope{rope[3]}")
    return pl.pallas_call(
        body, name=f"mm_n{tag}_{m}x{k}x{n}", grid=(m // tm, n // tn, nk), in_specs=in_specs,
        out_specs=pl.BlockSpec((tm, tn), lambda i, j, kk: (i, j)),
        out_shape=jax.ShapeDtypeStruct((m, n), F32),
        scratch_shapes=[pltpu.VMEM((tm, tn), F32)],
        compiler_params=_params(dimension_semantics=("parallel", "parallel", "arbitrary")),
    )(*args)


def _mm_tn(a, g):
    s, k = a.shape
    _, n = g.shape
    tm = _pick(k, (1024, 512, 256, 128))
    tn = _pick(n, (1024, 640, 512, 256, 128))
    ts = _pick(s, (512, 256, 128))
    ns = s // ts

    def body(a_ref, g_ref, o_ref, acc_ref):
        ss = pl.program_id(2)

        @pl.when(ss == 0)
        def _():
            acc_ref[...] = jnp.zeros_like(acc_ref)

        acc_ref[...] += lax.dot_general(a_ref[...].astype(BF16), g_ref[...].astype(BF16),
                                        (((0,), (0,)), ((), ())), preferred_element_type=F32)

        @pl.when(ss == ns - 1)
        def _():
            o_ref[...] = acc_ref[...]

    return pl.pallas_call(
        body, name=f"mm_tn_{s}x{k}x{n}", grid=(k // tm, n // tn, ns),
        in_specs=[pl.BlockSpec((ts, tm), lambda i, j, ss: (ss, i)), pl.BlockSpec((ts, tn), lambda i, j, ss: (ss, j))],
        out_specs=pl.BlockSpec((tm, tn), lambda i, j, ss: (i, j)),
        out_shape=jax.ShapeDtypeStruct((k, n), F32),
        scratch_shapes=[pltpu.VMEM((tm, tn), F32)],
        compiler_params=_params(dimension_semantics=("parallel", "parallel", "arbitrary")),
    )(a, g)


@jax.custom_vjp
def mm(a, w):
    return _mm_nn(a, w.astype(BF16))


def _mm_fwd(a, w):
    wb = w.astype(BF16)
    return _mm_nn(a, wb), (a, wb)


def _mm_bwd(res, g):
    a, wb = res
    return _mm_nn(g, wb, b_transposed=True), _mm_tn(a, g)


mm.defvjp(_mm_fwd, _mm_bwd)


def _unrope(g, tabs, hs):
    return _rope_call(g, tabs[0], -tabs[1], -tabs[2], hs)


@functools.partial(jax.custom_vjp, nondiff_argnums=(3,))
def mm_rope(a, w, tabs, hs):
    return _mm_nn(a, w.astype(BF16), rope=(*tabs, hs))


def _mm_rope_fwd(a, w, tabs, hs):
    wb = w.astype(BF16)
    return _mm_nn(a, wb, rope=(*tabs, hs)), (a, wb, tabs)


def _mm_rope_bwd(hs, res, g):
    a, wb, tabs = res
    g = _unrope(g, tabs, hs)
    return _mm_nn(g, wb, b_transposed=True), _mm_tn(a, g), jax.tree.map(jnp.zeros_like, tabs)


mm_rope.defvjp(_mm_rope_fwd, _mm_rope_bwd)


def _proj_dx_call(gs, wb, cuts):
    s, (d, n), ng = gs[0].shape[0], wb.shape, len(gs)
    tm = _pick(s, (512, 256, 128))

    def body(*refs):
        w_ref, o_ref = refs[ng], refs[ng + 1]
        for gi in range(ng):
            part = lax.dot_general(refs[gi][...].astype(BF16), w_ref[:, cuts[gi]:cuts[gi + 1]],
                                   (((1,), (1,)), ((), ())), preferred_element_type=F32)
            if gi == 0:
                o_ref[...] = part
            else:
                o_ref[...] += part

    return pl.pallas_call(
        body, name=f"proj_dx_{ng}", grid=(s // tm,),
        in_specs=[pl.BlockSpec((tm, g.shape[1]), lambda i: (i, 0)) for g in gs] + [pl.BlockSpec((d, n), lambda i: (0, 0))],
        out_specs=pl.BlockSpec((tm, d), lambda i: (i, 0)), out_shape=jax.ShapeDtypeStruct((s, d), F32),
        compiler_params=_params(dimension_semantics=("parallel",)),
    )(*gs, wb)


def _proj_dw_call(xb, gs, cuts):
    (s, d), ng, n = xb.shape, len(gs), cuts[-1]
    ts = _pick(s, (512, 256, 128))
    ns = s // ts

    def body(*refs):
        x_ref, o_ref = refs[0], refs[ng + 1]
        ss = pl.program_id(0)
        xt = x_ref[...].T
        for gi in range(ng):
            cols = slice(cuts[gi], cuts[gi + 1])
            part = jnp.dot(xt, refs[1 + gi][...].astype(BF16), preferred_element_type=F32)

            @pl.when(ss == 0)
            def _():
                o_ref[:, cols] = part

            @pl.when(ss > 0)
            def _():
                o_ref[:, cols] += part

    return pl.pallas_call(
        body, name=f"proj_dw_{ng}", grid=(ns,),
        in_specs=[pl.BlockSpec((ts, d), lambda ss: (ss, 0))] + [pl.BlockSpec((ts, g.shape[1]), lambda ss: (ss, 0)) for g in gs],
        out_specs=pl.BlockSpec((d, n), lambda ss: (0, 0)), out_shape=jax.ShapeDtypeStruct((d, n), F32),
        compiler_params=pltpu.CompilerParams(vmem_limit_bytes=VMEM_LIMIT_TALL, dimension_semantics=("arbitrary",)),
    )(xb, *gs)


def _make_in_proj(widths, roped, hs):
    cuts = [sum(widths[:i]) for i in range(len(widths) + 1)]

    @jax.custom_vjp
    def in_proj(x, w, tabs):
        return fwd(x, w, tabs)[0]

    def fwd(x, w, tabs):
        xb, wb = x.astype(BF16), w.astype(BF16)
        outs = tuple(_mm_nn(xb, wb[:, a:b], rope=(*tabs, hs) if gi in roped else None)
                     for gi, (a, b) in enumerate(zip(cuts[:-1], cuts[1:])))
        return outs, (xb, wb, tabs)

    def bwd(res, gs):
        xb, wb, tabs = res
        gs = [_unrope(gg, tabs, hs) if gi in roped else gg for gi, gg in enumerate(gs)]
        return _proj_dx_call(gs, wb, cuts), _proj_dw_call(xb, gs, cuts), jax.tree.map(jnp.zeros_like, tabs)

    in_proj.defvjp(fwd, bwd)
    return in_proj


EVEN_GROUPS = (256, 128, 128, 512, 512, 512, 128, 1024)
ODD_GROUPS = (1024, 256, 256, 1024)
even_in_proj = _make_in_proj(EVEN_GROUPS, roped=(2,), hs=MLA_ROPE // 2)
odd_in_proj = _make_in_proj(ODD_GROUPS, roped=(0, 1), hs=SWA_DIM // 2)

SHARD_PAD = 384


def _source_columns(kind):
    if kind == "even":
        src = [list(range(0, 384)), [-1] * 64, list(range(384, 416)), [-1] * 32, list(range(416, 1952)),
               list(range(1952, 1960)), [-1] * 120, list(range(1960, 2984))]
        return sum(src, []), 373
    q0, k0, v0, g0 = 0, 1024, 1152, 1280
    dup = lambda base: [base + 64 * g + c for g in range(SWA_KV_HEADS) for _ in range(2) for c in range(64)]
    return list(range(q0, k0)) + dup(k0) + dup(v0) + list(range(g0, 2304)), 288


def _selection(kind, transposed):
    src, shard = _source_columns(kind)
    cat = [s + (SHARD_PAD - shard) * (s // shard) if s >= 0 else -1 for s in src]
    cat_arr = jnp.asarray(cat, jnp.int32)
    if transposed:
        cols = lax.broadcasted_iota(jnp.int32, (len(src), N_DEV * SHARD_PAD), 1)
        return (cols == cat_arr[:, None]).astype(BF16), [(c, r) for c, r in enumerate(cat) if r >= 0]
    rows = lax.broadcasted_iota(jnp.int32, (N_DEV * SHARD_PAD, len(src)), 0)
    return (rows == cat_arr[None, :]).astype(BF16), [(r, c) for c, r in enumerate(cat) if r >= 0]


def _mm_banded(a, b, nonzeros):
    m, k = a.shape
    _, n = b.shape
    tm = _pick(m, (1024, 512, 256, 128))
    tn = _pick(n, (1024, 640, 512, 256, 128))
    tk = _pick(k, (1024, 640, 512, 256, 128))
    lo, hi = [k // tk] * (n // tn), [-1] * (n // tn)
    for r, c in nonzeros:
        lo[c // tn], hi[c // tn] = min(lo[c // tn], r // tk), max(hi[c // tn], r // tk)
    first = [l if h >= 0 else 0 for l, h in zip(lo, hi)]
    count = [h - l + 1 if h >= 0 else 0 for l, h in zip(lo, hi)]
    steps = max(count)

    def body(first_ref, count_ref, a_ref, b_ref, o_ref, acc_ref):
        j, kk = pl.program_id(1), pl.program_id(2)

        @pl.when(kk == 0)
        def _():
            acc_ref[...] = jnp.zeros_like(acc_ref)

        @pl.when(kk < count_ref[j])
        def _():
            acc_ref[...] += jnp.dot(a_ref[...].astype(BF16), b_ref[...].astype(BF16), preferred_element_type=F32)

        @pl.when(kk == steps - 1)
        def _():
            o_ref[...] = acc_ref[...]

    kblk = lambda j, kk, f, c: jnp.minimum(f[j] + kk, f[j] + jnp.maximum(c[j], 1) - 1)
    return pl.pallas_call(
        body, name=f"mm_banded_{m}x{k}x{n}",
        grid_spec=pltpu.PrefetchScalarGridSpec(
            num_scalar_prefetch=2, grid=(m // tm, n // tn, steps),
            in_specs=[pl.BlockSpec((tm, tk), lambda i, j, kk, f, c: (i, kblk(j, kk, f, c))),
                      pl.BlockSpec((tk, tn), lambda i, j, kk, f, c: (kblk(j, kk, f, c), j))],
            out_specs=pl.BlockSpec((tm, tn), lambda i, j, kk, f, c: (i, j)),
            scratch_shapes=[pltpu.VMEM((tm, tn), F32)]),
        out_shape=jax.ShapeDtypeStruct((m, n), F32),
        compiler_params=_params(dimension_semantics=("parallel", "parallel", "arbitrary")),
    )(jnp.asarray(first, jnp.int32), jnp.asarray(count, jnp.int32), a, b)


@functools.partial(jax.custom_vjp, nondiff_argnums=(1,))
def relayout(wcat, kind):
    return _mm_banded(wcat, *_selection(kind, False))


def _relayout_bwd(kind, _, g):
    return (_mm_banded(g, *_selection(kind, True)),)


relayout.defvjp(lambda wcat, kind: (_mm_banded(wcat, *_selection(kind, False)), None), _relayout_bwd)


def _row_block(s):
    return _pick(s, (512, 256, 128, 64, 32, 16, 8))


def _rms_fwd_call(x, g):
    s, k = x.shape
    tr = _row_block(s)

    def body(x_ref, g_ref, o_ref):
        xv = x_ref[...]
        r = lax.rsqrt(jnp.mean(xv * xv, axis=-1, keepdims=True) + RMS_EPS)
        o_ref[...] = xv * r * g_ref[...]

    return pl.pallas_call(
        body, name=f"rms_fwd_{k}", grid=(s // tr,),
        in_specs=[pl.BlockSpec((tr, k), lambda i: (i, 0)), pl.BlockSpec((1, k), lambda i: (0, 0))],
        out_specs=pl.BlockSpec((tr, k), lambda i: (i, 0)), out_shape=jax.ShapeDtypeStruct((s, k), F32),
        compiler_params=_params(dimension_semantics=("parallel",)),
    )(x, g.reshape(1, k))


def _rms_bwd_call(x, g, dy):
    s, k = x.shape
    tr = _row_block(s)

    def body(x_ref, g_ref, dy_ref, dx_ref, dg_ref):
        @pl.when(pl.program_id(0) == 0)
        def _():
            dg_ref[...] = jnp.zeros_like(dg_ref)

        xv = x_ref[...]
        r = lax.rsqrt(jnp.mean(xv * xv, axis=-1, keepdims=True) + RMS_EPS)
        xh = xv * r
        dyv = dy_ref[...]
        dg_ref[...] += jnp.sum(dyv * xh, axis=0, keepdims=True)
        dxh = dyv * g_ref[...]
        dx_ref[...] = r * (dxh - xh * jnp.mean(dxh * xh, axis=-1, keepdims=True))

    dx, dg = pl.pallas_call(
        body, name=f"rms_bwd_{k}", grid=(s // tr,),
        in_specs=[pl.BlockSpec((tr, k), lambda i: (i, 0)), pl.BlockSpec((1, k), lambda i: (0, 0)),
                  pl.BlockSpec((tr, k), lambda i: (i, 0))],
        out_specs=[pl.BlockSpec((tr, k), lambda i: (i, 0)), pl.BlockSpec((1, k), lambda i: (0, 0))],
        out_shape=[jax.ShapeDtypeStruct((s, k), F32), jax.ShapeDtypeStruct((1, k), F32)],
        compiler_params=_params(dimension_semantics=("arbitrary",)),
    )(x, g.reshape(1, k), dy)
    return dx, dg.reshape(k)


@jax.custom_vjp
def rms_norm(x, g):
    return _rms_fwd_call(x, g)


rms_norm.defvjp(lambda x, g: (_rms_fwd_call(x, g), (x, g)), lambda res, dy: _rms_bwd_call(res[0], res[1], dy))


def _ln_fwd_call(x, y, g, b):
    s, k = x.shape
    tr = _row_block(s)

    def body(x_ref, y_ref, g_ref, b_ref, o_ref):
        u = ALPHA * x_ref[...] + y_ref[...]
        mu = jnp.mean(u, axis=-1, keepdims=True)
        d = u - mu
        var = jnp.mean(d * d, axis=-1, keepdims=True)
        o_ref[...] = d * lax.rsqrt(var + LN_EPS) * g_ref[...] + b_ref[...]

    row = pl.BlockSpec((tr, k), lambda i: (i, 0))
    vec = pl.BlockSpec((1, k), lambda i: (0, 0))
    return pl.pallas_call(
        body, name="ln_fwd", grid=(s // tr,), in_specs=[row, row, vec, vec], out_specs=row,
        out_shape=jax.ShapeDtypeStruct((s, k), F32), compiler_params=_params(dimension_semantics=("parallel",)),
    )(x, y, g.reshape(1, k), b.reshape(1, k))


def _ln_bwd_call(x, y, g, do):
    s, k = x.shape
    tr = _row_block(s)

    def body(x_ref, y_ref, g_ref, do_ref, dx_ref, dy_ref, dg_ref, db_ref):
        @pl.when(pl.program_id(0) == 0)
        def _():
            dg_ref[...] = jnp.zeros_like(dg_ref)
            db_ref[...] = jnp.zeros_like(db_ref)

        u = ALPHA * x_ref[...] + y_ref[...]
        mu = jnp.mean(u, axis=-1, keepdims=True)
        d = u - mu
        r = lax.rsqrt(jnp.mean(d * d, axis=-1, keepdims=True) + LN_EPS)
        xh = d * r
        dov = do_ref[...]
        dg_ref[...] += jnp.sum(dov * xh, axis=0, keepdims=True)
        db_ref[...] += jnp.sum(dov, axis=0, keepdims=True)
        dxh = dov * g_ref[...]
        du = r * (dxh - jnp.mean(dxh, axis=-1, keepdims=True) - xh * jnp.mean(dxh * xh, axis=-1, keepdims=True))
        dy_ref[...] = du
        dx_ref[...] = ALPHA * du

    row = pl.BlockSpec((tr, k), lambda i: (i, 0))
    vec = pl.BlockSpec((1, k), lambda i: (0, 0))
    dx, dy, dg, db = pl.pallas_call(
        body, name="ln_bwd", grid=(s // tr,), in_specs=[row, row, vec, row], out_specs=[row, row, vec, vec],
        out_shape=[jax.ShapeDtypeStruct((s, k), F32), jax.ShapeDtypeStruct((s, k), F32),
                   jax.ShapeDtypeStruct((1, k), F32), jax.ShapeDtypeStruct((1, k), F32)],
        compiler_params=_params(dimension_semantics=("arbitrary",)),
    )(x, y, g.reshape(1, k), do)
    return dx, dy, dg.reshape(k), db.reshape(k)


@jax.custom_vjp
def ln_res(x, y, g, b):
    return _ln_fwd_call(x, y, g, b)


ln_res.defvjp(lambda x, y, g, b: (_ln_fwd_call(x, y, g, b), (x, y, g)),
              lambda res, do: _ln_bwd_call(res[0], res[1], res[2], do))


def _rope_call(x, c, s1, s2, hs):
    s, w = x.shape
    tr = _row_block(s)
    nb = w // LANES

    def body(x_ref, c_ref, s1_ref, s2_ref, o_ref):
        cv, s1v, s2v = c_ref[...], s1_ref[...], s2_ref[...]
        for cb in range(nb):
            xb = x_ref[:, cb * LANES:(cb + 1) * LANES]
            o_ref[:, cb * LANES:(cb + 1) * LANES] = (
                xb * cv + pltpu.roll(xb, LANES - hs, 1) * s1v + pltpu.roll(xb, hs, 1) * s2v)

    row = pl.BlockSpec((tr, w), lambda i: (i, 0))
    tab = pl.BlockSpec((tr, LANES), lambda i: (i, 0))
    return pl.pallas_call(
        body, name=f"rope_{w}_{hs}", grid=(s // tr,), in_specs=[row, tab, tab, tab], out_specs=row,
        out_shape=jax.ShapeDtypeStruct((s, w), F32), compiler_params=_params(dimension_semantics=("parallel",)),
    )(x, c, s1, s2)


def _rope_tables(s, layout):
    pos = jnp.arange(s, dtype=F32)[:, None]
    lane = jnp.arange(LANES)
    if layout == "mla":
        dim, hs = MLA_ROPE, MLA_ROPE // 2
        r = lane - MLA_NOPE
        active = (r >= 0) & (r < MLA_ROPE)
    else:
        dim, hs = SWA_DIM, SWA_DIM // 2
        r = lane % SWA_DIM
        active = jnp.ones_like(lane, dtype=bool)
    f = jnp.where(active, r % hs, 0)
    inv = ROPE_THETA ** (-(2.0 * f.astype(F32)) / dim)
    ang = pos * inv[None, :]
    cos, sin = jnp.cos(ang), jnp.sin(ang)
    first = (active & (r < hs))[None, :]
    second = (active & (r >= hs))[None, :]
    c = jnp.where(active[None, :], cos, 1.0)
    s1 = jnp.where(first, -sin, 0.0)
    s2 = jnp.where(second, sin, 0.0)
    return (c, s1, s2), hs


def _gated(o_refs, g_ref):
    gv = g_ref[...]
    o = o_refs[0][...] if len(o_refs) == 1 else jnp.concatenate([r[...] for r in o_refs], axis=1)
    return (o * (gv * jax.nn.sigmoid(gv))).astype(BF16)


def _gated_mm_call(o_parts, gate, wb):
    s, k = gate.shape
    n = wb.shape[1]
    tm = _pick(s, (1024, 512, 256, 128))
    tn = _pick(n, (1024, 512, 256, 128))
    widths = [o.shape[1] for o in o_parts]
    no = len(widths)

    def body(*refs):
        refs[-1][...] = jnp.dot(_gated(refs[:no], refs[no]), refs[no + 1][...], preferred_element_type=F32)

    specs = [pl.BlockSpec((tm, wd), lambda i, j: (i, 0)) for wd in widths]
    return pl.pallas_call(
        body, name=f"gated_mm_{no}", grid=(s // tm, n // tn),
        in_specs=specs + [pl.BlockSpec((tm, k), lambda i, j: (i, 0)), pl.BlockSpec((k, tn), lambda i, j: (0, j))],
        out_specs=pl.BlockSpec((tm, tn), lambda i, j: (i, j)), out_shape=jax.ShapeDtypeStruct((s, n), F32),
        compiler_params=_params(dimension_semantics=("parallel", "parallel")),
    )(*o_parts, gate, wb)


def _gated_mm_tn_call(o_parts, gate, g):
    s, k = gate.shape
    n = g.shape[1]
    ts = _pick(s, (512, 256, 128))
    ns = s // ts
    widths = [o.shape[1] for o in o_parts]
    no = len(widths)

    def body(*refs):
        g_ref, o_ref, acc_ref = refs[no + 1], refs[no + 2], refs[no + 3]
        ss = pl.program_id(0)

        @pl.when(ss == 0)
        def _():
            acc_ref[...] = jnp.zeros_like(acc_ref)

        acc_ref[...] += lax.dot_general(_gated(refs[:no], refs[no]), g_ref[...].astype(BF16),
                                        (((0,), (0,)), ((), ())), preferred_element_type=F32)

        @pl.when(ss == ns - 1)
        def _():
            o_ref[...] = acc_ref[...]

    specs = [pl.BlockSpec((ts, wd), lambda ss: (ss, 0)) for wd in widths]
    return pl.pallas_call(
        body, name=f"gated_mm_tn_{no}", grid=(ns,),
        in_specs=specs + [pl.BlockSpec((ts, k), lambda ss: (ss, 0)), pl.BlockSpec((ts, n), lambda ss: (ss, 0))],
        out_specs=pl.BlockSpec((k, n), lambda ss: (0, 0)), out_shape=jax.ShapeDtypeStruct((k, n), F32),
        scratch_shapes=[pltpu.VMEM((k, n), F32)], compiler_params=_params(dimension_semantics=("arbitrary",)),
    )(*o_parts, gate, g)


def _gate_bwd_call(o_parts, gate, dz):
    s, w = gate.shape
    tr = _row_block(s)
    widths = [o.shape[1] for o in o_parts]
    n = len(widths)

    def body(*refs):
        o_refs, g_ref, dz_ref = refs[:n], refs[n], refs[n + 1]
        do_refs, dg_ref = refs[n + 2:2 * n + 2], refs[2 * n + 2]
        off = 0
        for o_ref, do_ref, wd in zip(o_refs, do_refs, widths):
            gv = g_ref[:, off:off + wd]
            sg = jax.nn.sigmoid(gv)
            dzv = dz_ref[:, off:off + wd]
            do_ref[...] = dzv * (gv * sg)
            dg_ref[:, off:off + wd] = dzv * o_ref[...] * (sg * (1.0 + gv * (1.0 - sg)))
            off += wd

    specs = [pl.BlockSpec((tr, wd), lambda i: (i, 0)) for wd in widths]
    row = pl.BlockSpec((tr, w), lambda i: (i, 0))
    outs = pl.pallas_call(
        body, name=f"gate_bwd_{n}", grid=(s // tr,), in_specs=specs + [row, row], out_specs=specs + [row],
        out_shape=[jax.ShapeDtypeStruct((s, wd), F32) for wd in widths] + [jax.ShapeDtypeStruct((s, w), F32)],
        compiler_params=_params(dimension_semantics=("parallel",)),
    )(*o_parts, gate, dz)
    return tuple(outs[:n]), outs[n]


@jax.custom_vjp
def gated_mm(o_parts, gate, w):
    return _gated_mm_call(o_parts, gate, w.astype(BF16))


def _gated_mm_fwd(o_parts, gate, w):
    wb = w.astype(BF16)
    return _gated_mm_call(o_parts, gate, wb), (o_parts, gate, wb)


def _gated_mm_bwd(res, g):
    o_parts, gate, wb = res
    do_parts, dgate = _gate_bwd_call(o_parts, gate, _mm_nn(g, wb, b_transposed=True))
    return do_parts, dgate, _gated_mm_tn_call(o_parts, gate, g)


gated_mm.defvjp(_gated_mm_fwd, _gated_mm_bwd)


def _loss_call(y, t):
    s, k = y.shape
    tr = _row_block(s)
    nsteps = s // tr

    def body(y_ref, t_ref, l_ref, dy_ref, acc_ref):
        i = pl.program_id(0)

        @pl.when(i == 0)
        def _():
            acc_ref[...] = jnp.zeros_like(acc_ref)

        d = y_ref[...] - t_ref[...]
        dy_ref[...] = d / k
        acc_ref[...] += jnp.sum(d * d, axis=0, keepdims=True)

        @pl.when(i == nsteps - 1)
        def _():
            tot = jnp.sum(acc_ref[...], axis=1, keepdims=True) * (0.5 / k)
            l_ref[...] = jnp.broadcast_to(tot, l_ref.shape)

    row = pl.BlockSpec((tr, k), lambda i: (i, 0))
    return pl.pallas_call(
        body, name="loss", grid=(nsteps,), in_specs=[row, row],
        out_specs=[pl.BlockSpec((1, LANES), lambda i: (0, 0)), row],
        out_shape=[jax.ShapeDtypeStruct((1, LANES), F32), jax.ShapeDtypeStruct((s, k), F32)],
        scratch_shapes=[pltpu.VMEM((1, k), F32)], compiler_params=_params(dimension_semantics=("arbitrary",)),
    )(y, t)


@jax.custom_vjp
def mse_loss(y, t):
    return _loss_call(y, t)[0][0, 0]


def _mse_fwd(y, t):
    l, dy = _loss_call(y, t)
    return l[0, 0], (dy, t)


mse_loss.defvjp(_mse_fwd, lambda res, g: (g * res[0], jnp.zeros_like(res[1])))


def _scan_call(x, b, mode):
    s, w = x.shape
    nt = s // 8

    def tile_scan(t):
        row = lax.broadcasted_iota(jnp.int32, (8, w), 0)
        for sh in (1, 2, 4):
            t = t + jnp.where(row >= sh, pltpu.roll(t, sh, 0), 0.0)
        return t

    def body(x_ref, b_ref, o_ref):
        def step(i, carry):
            rows = pl.ds(pl.multiple_of(i * 8, 8), 8)
            t = x_ref[rows, :]
            if mode == "fwd":
                t = jax.nn.log_sigmoid(t + b_ref[...])
            t = tile_scan(t) + carry
            o_ref[rows, :] = t
            return t[7:8, :]

        total = lax.fori_loop(0, nt, step, jnp.zeros((1, w), F32))
        if mode == "rev":
            def fix(i, c):
                rows = pl.ds(pl.multiple_of(i * 8, 8), 8)
                o_ref[rows, :] = total - o_ref[rows, :] + x_ref[rows, :]
                return c
            lax.fori_loop(0, nt, fix, 0)

    full = pl.BlockSpec((s, w), lambda: (0, 0))
    return pl.pallas_call(
        body, name=f"scan_{mode}", in_specs=[full, pl.BlockSpec((1, w), lambda: (0, 0))], out_specs=full,
        out_shape=jax.ShapeDtypeStruct((s, w), F32), compiler_params=_params(),
    )(x, b)


def _fox_dlogit_call(x, b, dlogf):
    s, w = x.shape
    tr = _row_block(s)

    def body(x_ref, b_ref, d_ref, dx_ref, db_ref):
        @pl.when(pl.program_id(0) == 0)
        def _():
            db_ref[...] = jnp.zeros_like(db_ref)

        dx = d_ref[...] * jax.nn.sigmoid(-(x_ref[...] + b_ref[...]))
        dx_ref[...] = dx
        db_ref[...] += jnp.sum(dx, axis=0, keepdims=True)

    row = pl.BlockSpec((tr, w), lambda i: (i, 0))
    vec = pl.BlockSpec((1, w), lambda i: (0, 0))
    return pl.pallas_call(
        body, name="fox_dlogit", grid=(s // tr,), in_specs=[row, vec, row], out_specs=[row, vec],
        out_shape=[jax.ShapeDtypeStruct((s, w), F32), jax.ShapeDtypeStruct((1, w), F32)],
        compiler_params=_params(dimension_semantics=("arbitrary",)),
    )(x, b, dlogf)


@jax.custom_vjp
def fox_cum(fl, b):
    return _scan_call(fl, b, "fwd")


def _fox_cum_bwd(res, dcum):
    fl, b = res
    dlogf = _scan_call(dcum, b, "rev")
    return _fox_dlogit_call(fl, b, dlogf)


fox_cum.defvjp(lambda fl, b: (_scan_call(fl, b, "fwd"), (fl, b)), _fox_cum_bwd)


def _lane_col(x, lane_idx):
    lane = lax.broadcasted_iota(jnp.int32, (1, x.shape[1]), 1)
    return jnp.sum(jnp.where(lane == lane_idx, x, 0.0), axis=1, keepdims=True)


def _row_of(x, row_idx):
    row = lax.broadcasted_iota(jnp.int32, (x.shape[0], 1), 0)
    return jnp.sum(jnp.where(row == row_idx, x, 0.0), axis=0, keepdims=True)


def _attn_cfg(mode, s):
    if mode == "swa":
        blk = 256 if s >= 2048 else 128
        return dict(blk=blk, n_outer=SWA_KV_HEADS, pps=4, wide=False, scale=SWA_DIM ** -0.5)
    blk = 512 if s >= 2048 else 128
    if mode == "mla":
        return dict(blk=blk, n_outer=4, pps=1, wide=True, scale=(MLA_NOPE + MLA_ROPE) ** -0.5)
    return dict(blk=blk, n_outer=4, pps=1, wide=False, scale=FOX_DIM ** -0.5)


ROW_CHUNK = 32


def _unrolled(n, body, carry):
    for c in range(n):
        carry = body(c, carry)
    return carry


def _valid_rows(mode, i, jb, blk, r0, rc):
    qpos = i * blk + r0 + lax.broadcasted_iota(jnp.int32, (rc, blk), 0)
    kpos = jb * blk + lax.broadcasted_iota(jnp.int32, (rc, blk), 1)
    ok = kpos <= qpos
    if mode == "swa":
        ok = ok & (qpos - kpos < WINDOW)
    return ok


def _attn_fwd_call(mode, q, k, v, extra):
    s = q.shape[0]
    cfg = _attn_cfg(mode, s)
    blk, n_outer, pps, wide, scale = cfg["blk"], cfg["n_outer"], cfg["pps"], cfg["wide"], cfg["scale"]
    rc = ROW_CHUNK
    nq = s // blk
    swa, fox = mode == "swa", mode == "fox"
    qw = (2 * LANES if wide else LANES) * pps
    kw = 2 * LANES if wide else LANES
    ow = LANES * pps
    reps = blk // LANES

    def body(*refs):
        if not swa:
            it_ref, jt_ref = refs[:2]
            refs = refs[2:]
        q_ref, k_ref, v_ref = refs[:3]
        n_in = 3
        if fox:
            cum_ref, cumt_ref = refs[3:5]
            n_in = 5
        if swa:
            sink_ref = refs[3]
            n_in = 4
        o_ref, lse_ref, m_ref, l_ref, acc_ref, a_ref, s_all, p_all, c_all = refs[n_in:]
        p_id = pl.program_id(0)
        if swa:
            i, j = pl.program_id(1), pl.program_id(2)
            jb, run, first, last = i - 1 + j, (i - 1 + j) >= 0, j == 0, j == 1
        else:
            i, j = it_ref[pl.program_id(1)], jt_ref[pl.program_id(1)]
            jb, first, last = j, j == 0, j == i
        lane = lax.broadcasted_iota(jnp.int32, (1, LANES), 1)
        msk = [lane < HALF, lane >= HALF]

        @pl.when(first)
        def _():
            for hh in range(2 * pps):
                if swa:
                    m_ref[hh] = jnp.broadcast_to(sink_ref[hh:hh + 1, :], (blk, LANES))
                    l_ref[hh] = jnp.ones((blk, LANES), F32)
                else:
                    m_ref[hh] = jnp.full((blk, LANES), NEG, F32)
                    l_ref[hh] = jnp.zeros((blk, LANES), F32)
            acc_ref[...] = jnp.zeros_like(acc_ref)

        def process(masked):
            for pp in range(pps):
                vb = v_ref[...]
                pvs = []
                for h in range(2):
                    hh = 2 * pp + h
                    s_ref, p_ref, c_ref = s_all.at[hh], p_all.at[hh], c_all.at[hh]
                    if wide:
                        qh = q_ref[:, h * LANES:(h + 1) * LANES] * scale
                        kh = k_ref[:, h * LANES:(h + 1) * LANES]
                    else:
                        qh = jnp.where(msk[h], q_ref[:, pp * LANES:(pp + 1) * LANES], 0.0) * scale
                        kh = k_ref[...]
                    s_ref[...] = lax.dot_general(qh.astype(BF16), kh.astype(BF16), (((1,), (1,)), ((), ())),
                                                 preferred_element_type=F32)
                    if fox:
                        head = 2 * p_id + h
                        c_ref[...] = jnp.broadcast_to(_lane_col(cum_ref[...], head), (blk, LANES))
                        ck = _row_of(cumt_ref[...], head)

                    def chunk(c, carry, hh=hh, h=h):
                        r0 = c * rc
                        rows = pl.ds(r0, rc)
                        u = s_ref[rows, :]
                        if fox:
                            u = u - ck
                        if masked:
                            u = jnp.where(_valid_rows(mode, i, jb, blk, r0, rc), u, NEG)
                        m_prev, l_prev = m_ref[hh, rows, :], l_ref[hh, rows, :]
                        m_cur = jnp.max(u, axis=1, keepdims=True)
                        if fox:
                            m_cur = m_cur + c_ref[rows, :]
                        m_next = jnp.maximum(m_prev, m_cur)
                        shift = m_next - c_ref[rows, :] if fox else m_next
                        p = jnp.exp(u - jnp.tile(shift, (1, reps)))
                        alpha = jnp.exp(m_prev - m_next)
                        l_ref[hh, rows, :] = alpha * l_prev + jnp.sum(p, axis=1, keepdims=True)
                        m_ref[hh, rows, :] = m_next
                        a_ref[hh, rows, :] = alpha
                        p_ref[rows, :] = p.astype(BF16)
                        return carry

                    _unrolled(blk // rc, chunk, 0)
                    vh = jnp.where(msk[h], vb, 0.0).astype(BF16)
                    pvs.append(jnp.dot(p_ref[...], vh, preferred_element_type=F32))
                acc_ref[pp] = acc_ref[pp] * jnp.where(msk[0], a_ref[2 * pp], a_ref[2 * pp + 1]) + pvs[0] + pvs[1]

        if swa:
            pl.when(run)(lambda: process(True))
        else:
            pl.when(j < i)(lambda: process(False))
            pl.when(j == i)(lambda: process(True))

        @pl.when(last)
        def _():
            for pp in range(pps):
                l0, l1 = l_ref[2 * pp], l_ref[2 * pp + 1]
                o_ref[:, pp * LANES:(pp + 1) * LANES] = acc_ref[pp] / jnp.where(msk[0], l0, l1)
                lse_ref[:, pp * LANES:(pp + 1) * LANES] = jnp.where(
                    msk[0], m_ref[2 * pp] + jnp.log(l0), m_ref[2 * pp + 1] + jnp.log(l1))

    if swa:
        kv_map = lambda g, i, j: (jnp.maximum(i - 1 + j, 0), g)
        q_map = lambda g, i, j: (i, g)
        grid, tables, sem = (n_outer, nq, 2), [], ("parallel", "parallel", "arbitrary")
    else:
        tri = [(i, j) for i in range(nq) for j in range(i + 1)]
        tables = [jnp.asarray([t[0] for t in tri], jnp.int32), jnp.asarray([t[1] for t in tri], jnp.int32)]
        kv_map = lambda p, t, it, jt: (jt[t], p)
        q_map = lambda p, t, it, jt: (it[t], p)
        grid, sem = (n_outer, len(tri)), ("parallel", "arbitrary")
    in_specs = [pl.BlockSpec((blk, qw), q_map), pl.BlockSpec((blk, kw), kv_map), pl.BlockSpec((blk, LANES), kv_map)]
    args = [q, k, v]
    if fox:
        cum, cumt = extra
        in_specs += [pl.BlockSpec((blk, LANES), lambda p, t, it, jt: (it[t], 0)),
                     pl.BlockSpec((8, blk), lambda p, t, it, jt: (0, jt[t]))]
        args += [cum, cumt]
    if swa:
        in_specs += [pl.BlockSpec((8, LANES), lambda g, i, j: (g, 0))]
        args += [extra]
    n_pairs = n_outer * pps
    return pl.pallas_call(
        body, name=f"attn_fwd_{mode}",
        grid_spec=pltpu.PrefetchScalarGridSpec(
            num_scalar_prefetch=len(tables), grid=grid, in_specs=in_specs,
            out_specs=[pl.BlockSpec((blk, ow), q_map), pl.BlockSpec((blk, ow), q_map)],
            scratch_shapes=[pltpu.VMEM((2 * pps, blk, LANES), F32), pltpu.VMEM((2 * pps, blk, LANES), F32),
                            pltpu.VMEM((pps, blk, LANES), F32), pltpu.VMEM((2 * pps, blk, LANES), F32),
                            pltpu.VMEM((2 * pps, blk, blk), F32), pltpu.VMEM((2 * pps, blk, blk), BF16),
                            pltpu.VMEM((2 * pps, blk, LANES), F32)]),
        out_shape=[jax.ShapeDtypeStruct((s, n_pairs * LANES), F32), jax.ShapeDtypeStruct((s, n_pairs * LANES), F32)],
        compiler_params=_params(dimension_semantics=sem),
    )(*tables, *args)


def _attn_bwd_call(mode, q, k, v, extra, lse, o, do):
    s = q.shape[0]
    cfg = _attn_cfg(mode, s)
    blk, n_outer, pps, wide, scale = cfg["blk"], cfg["n_outer"], cfg["pps"], cfg["wide"], cfg["scale"]
    rc = ROW_CHUNK
    nq = s // blk
    swa, fox = mode == "swa", mode == "fox"
    qw = (2 * LANES if wide else LANES) * pps
    kw = 2 * LANES if wide else LANES
    ow = LANES * pps
    reps = blk // LANES

    assert not swa

    def body(*refs):
        jt_ref, it_ref = refs[:2]
        refs = refs[2:]
        q_ref, k_ref, v_ref, lse_ref, o_ref, do_ref = refs[:6]
        n_in = 6
        if fox:
            cum_ref, cumt_ref = refs[6:8]
            n_in = 8
        dq_ref, dk_ref, dv_ref = refs[n_in:n_in + 3]
        n_out = n_in + 3
        if fox:
            dck_ref, dcq_ref = refs[n_out:n_out + 2]
            n_out += 2
        dk_acc, dv_acc, s_all, dp_all, p_all, ds_all, e_all, d_all = refs[n_out:n_out + 8]
        if fox:
            dck_acc, rs_all = refs[n_out + 8:n_out + 10]
        p_id, t = pl.program_id(0), pl.program_id(1)
        j, ii = jt_ref[t], it_ref[t]
        i, first_i, last_i = ii, ii == j, ii == nq - 1
        lane = lax.broadcasted_iota(jnp.int32, (1, LANES), 1)
        msk = [lane < HALF, lane >= HALF]

        @pl.when(t == 0)
        def _():
            dq_ref[...] = jnp.zeros_like(dq_ref)
            if fox:
                dcq_ref[...] = jnp.zeros_like(dcq_ref)

        @pl.when(first_i)
        def _():
            dk_acc[...] = jnp.zeros_like(dk_acc)
            dv_acc[...] = jnp.zeros_like(dv_acc)
            if fox:
                dck_acc[...] = jnp.zeros_like(dck_acc)

        def process(masked):
            rows = pl.ds(pl.multiple_of(i * blk, blk), blk)
            vb = v_ref[...].astype(BF16)
            dv_parts, dk_parts = [], []
            for pp in range(pps):
                psl = slice(pp * LANES, (pp + 1) * LANES)
                lse_blk, do_blk = lse_ref[:, psl], do_ref[:, psl]
                doo = do_blk * o_ref[:, psl]
                dq_pair = []
                for h in range(2):
                    hh = 2 * pp + h
                    s_ref, dp_ref, p_ref, ds_ref = s_all.at[hh], dp_all.at[hh], p_all.at[hh], ds_all.at[hh]
                    e_ref, d_ref = e_all.at[hh], d_all.at[hh]
                    if fox:
                        rs_ref = rs_all.at[hh]
                    if wide:
                        hsl = slice(h * LANES, (h + 1) * LANES)
                        qh = (q_ref[:, hsl] * scale).astype(BF16)
                        kh = k_ref[:, hsl].astype(BF16)
                    else:
                        qh = (jnp.where(msk[h], q_ref[:, psl], 0.0) * scale).astype(BF16)
                        kh = k_ref[...].astype(BF16)
                    s_ref[...] = lax.dot_general(qh, kh, (((1,), (1,)), ((), ())), preferred_element_type=F32)
                    do_h = jnp.where(msk[h], do_blk, 0.0).astype(BF16)
                    dp_ref[...] = lax.dot_general(do_h, vb, (((1,), (1,)), ((), ())), preferred_element_type=F32)
                    lse_h = _lane_col(lse_blk, HALF * h)
                    d_h = jnp.sum(jnp.where(msk[h], doo, 0.0), axis=1, keepdims=True)
                    e_ref[...] = jnp.broadcast_to(lse_h, (blk, LANES))
                    d_ref[...] = jnp.broadcast_to(d_h, (blk, LANES))
                    if fox:
                        head = 2 * p_id + h
                        e_ref[...] = e_ref[...] - jnp.broadcast_to(_lane_col(cum_ref[...], head), (blk, LANES))
                        ck = _row_of(cumt_ref[...], head)

                    def chunk(c, colsum):
                        r0 = c * rc
                        cr = pl.ds(r0, rc)
                        u = s_ref[cr, :]
                        if fox:
                            u = u - ck
                        p = jnp.exp(u - jnp.tile(e_ref[cr, :], (1, reps)))
                        if masked:
                            p = jnp.where(_valid_rows(mode, i, j, blk, r0, rc), p, 0.0)
                        ds = p * (dp_ref[cr, :] - jnp.tile(d_ref[cr, :], (1, reps)))
                        p_ref[cr, :] = p.astype(BF16)
                        ds_ref[cr, :] = ds.astype(BF16)
                        if fox:
                            colsum = colsum + jnp.sum(ds, axis=0, keepdims=True)
                            rs_ref[cr, :] = jnp.broadcast_to(jnp.sum(ds, axis=1, keepdims=True), (rc, LANES))
                        return colsum

                    colsum = _unrolled(blk // rc, chunk, jnp.zeros((1, blk), F32))
                    dv_parts.append(lax.dot_general(p_ref[...], do_h, (((0,), (0,)), ((), ())),
                                                    preferred_element_type=F32))
                    if fox:
                        dck_acc[h:h + 1, :] += -colsum
                        dcq_ref[rows, :] += jnp.where(msk[h], rs_ref[...], 0.0)
                    dq_h = jnp.dot(ds_ref[...], kh, preferred_element_type=F32) * scale
                    dk_h = lax.dot_general(ds_ref[...], qh, (((0,), (0,)), ((), ())), preferred_element_type=F32)
                    if wide:
                        dq_ref[rows, hsl] += dq_h
                        dk_acc[:, hsl] += dk_h
                    else:
                        dq_pair.append(jnp.where(msk[h], dq_h, 0.0))
                        dk_parts.append(dk_h)
                if not wide:
                    dq_ref[rows, psl] += dq_pair[0] + dq_pair[1]
            dv_acc[...] += functools.reduce(lambda a, b: a + b, dv_parts)
            if not wide:
                dk_acc[...] += functools.reduce(lambda a, b: a + b, dk_parts)

        pl.when(ii > j)(lambda: process(False))
        pl.when(ii == j)(lambda: process(True))

        @pl.when(last_i)
        def _():
            dk_ref[...] = dk_acc[...]
            dv_ref[...] = dv_acc[...]
            if fox:
                dck_ref[0] = dck_acc[...]

    tri = [(j, i) for j in range(nq) for i in range(j, nq)]
    tables = [jnp.asarray([t[0] for t in tri], jnp.int32), jnp.asarray([t[1] for t in tri], jnp.int32)]
    q_map = lambda p, t, jt, it: (it[t], p)
    kv_map = lambda p, t, jt, it: (jt[t], p)
    in_specs = [pl.BlockSpec((blk, qw), q_map), pl.BlockSpec((blk, kw), kv_map), pl.BlockSpec((blk, LANES), kv_map),
                pl.BlockSpec((blk, ow), q_map), pl.BlockSpec((blk, ow), q_map), pl.BlockSpec((blk, ow), q_map)]
    args = [q, k, v, lse, o, do]
    n_pairs = n_outer * pps
    out_specs = [pl.BlockSpec((s, qw), lambda p, t, jt, it: (0, p)), pl.BlockSpec((blk, kw), kv_map),
                 pl.BlockSpec((blk, LANES), kv_map)]
    out_shape = [jax.ShapeDtypeStruct((s, q.shape[1]), F32), jax.ShapeDtypeStruct((s, k.shape[1]), F32),
                 jax.ShapeDtypeStruct((s, v.shape[1]), F32)]
    nh = 2 * pps
    scratch = [pltpu.VMEM((blk, kw), F32), pltpu.VMEM((blk, LANES), F32), pltpu.VMEM((nh, blk, blk), F32),
               pltpu.VMEM((nh, blk, blk), F32), pltpu.VMEM((nh, blk, blk), BF16), pltpu.VMEM((nh, blk, blk), BF16),
               pltpu.VMEM((nh, blk, LANES), F32), pltpu.VMEM((nh, blk, LANES), F32)]
    if fox:
        cum, cumt = extra
        in_specs += [pl.BlockSpec((blk, LANES), lambda p, t, jt, it: (it[t], 0)),
                     pl.BlockSpec((8, blk), lambda p, t, jt, it: (0, jt[t]))]
        args += [cum, cumt]
        out_specs += [pl.BlockSpec((1, 8, blk), lambda p, t, jt, it: (p, 0, jt[t])),
                      pl.BlockSpec((s, LANES), lambda p, t, jt, it: (0, p))]
        out_shape += [jax.ShapeDtypeStruct((n_pairs, 8, s), F32), jax.ShapeDtypeStruct((s, n_pairs * LANES), F32)]
        scratch += [pltpu.VMEM((8, blk), F32), pltpu.VMEM((nh, blk, LANES), F32)]
    return pl.pallas_call(
        body, name=f"attn_bwd_{mode}",
        grid_spec=pltpu.PrefetchScalarGridSpec(num_scalar_prefetch=2, grid=(n_outer, len(tri)), in_specs=in_specs,
                                               out_specs=out_specs, scratch_shapes=scratch),
        out_shape=out_shape, compiler_params=_params(dimension_semantics=("parallel", "arbitrary")),
    )(*tables, *args)


def _swa_masks(i, blk):
    r = lax.broadcasted_iota(jnp.int32, (blk, blk), 0)
    c = lax.broadcasted_iota(jnp.int32, (blk, blk), 1)
    return (c > r) & (i > 0), c <= r


def _nt(a, b):
    return lax.dot_general(a, b, (((1,), (1,)), ((), ())), preferred_element_type=F32)


def _tn(a, b):
    return lax.dot_general(a, b, (((0,), (0,)), ((), ())), preferred_element_type=F32)


def _swa_bwd_call(q, k, v, sink, lse, o, do):
    s = q.shape[0]
    blk, pps, scale = WINDOW, 4, SWA_DIM ** -0.5
    nq = s // blk

    def body(q_ref, kp_ref, ko_ref, vp_ref, vo_ref, sink_ref, lse_ref, o_ref, do_ref,
             dq_ref, dk_ref, dv_ref, dsink_ref, ck_ref, cv_ref):
        i = pl.program_id(1)
        lane = lax.broadcasted_iota(jnp.int32, (1, LANES), 1)
        msk = [lane < HALF, lane >= HALF]

        @pl.when(i == 0)
        def _():
            ck_ref[...] = jnp.zeros_like(ck_ref)
            cv_ref[...] = jnp.zeros_like(cv_ref)
            dsink_ref[...] = jnp.zeros_like(dsink_ref)

        @pl.when(i < nq)
        def _():
            ok_prev, ok_own = _swa_masks(i, blk)
            kp, ko = kp_ref[...].astype(BF16), ko_ref[...].astype(BF16)
            vp, vo = vp_ref[...].astype(BF16), vo_ref[...].astype(BF16)
            dkp, dko, dvp, dvo = [], [], [], []
            for pp in range(pps):
                psl = slice(pp * LANES, (pp + 1) * LANES)
                qp, do_blk = q_ref[:, psl], do_ref[:, psl]
                doo = do_blk * o_ref[:, psl]
                dqs = []
                for h in range(2):
                    hh = 2 * pp + h
                    qh = (jnp.where(msk[h], qp, 0.0) * scale).astype(BF16)
                    lse_h = jnp.broadcast_to(_lane_col(lse_ref[:, psl], HALF * h), (blk, LANES))
                    d_h = jnp.broadcast_to(jnp.sum(jnp.where(msk[h], doo, 0.0), axis=1, keepdims=True), (blk, LANES))
                    p_p = jnp.where(ok_prev, jnp.exp(_nt(qh, kp) - lse_h), 0.0)
                    p_o = jnp.where(ok_own, jnp.exp(_nt(qh, ko) - lse_h), 0.0)
                    do_h = jnp.where(msk[h], do_blk, 0.0).astype(BF16)
                    ds_p = (p_p * (_nt(do_h, vp) - d_h)).astype(BF16)
                    ds_o = (p_o * (_nt(do_h, vo) - d_h)).astype(BF16)
                    dq_h = (jnp.dot(ds_p, kp, preferred_element_type=F32)
                            + jnp.dot(ds_o, ko, preferred_element_type=F32)) * scale
                    dqs.append(jnp.where(msk[h], dq_h, 0.0))
                    dkp.append(_tn(ds_p, qh))
                    dko.append(_tn(ds_o, qh))
                    dvp.append(_tn(p_p.astype(BF16), do_h))
                    dvo.append(_tn(p_o.astype(BF16), do_h))
                    sink_row = sink_ref[hh:hh + 1, :]
                    dsink_ref[hh:hh + 1, :] += -jnp.sum(jnp.exp(sink_row - lse_h) * d_h, axis=0, keepdims=True)
                dq_ref[:, psl] = dqs[0] + dqs[1]
            total = lambda parts: functools.reduce(lambda a, b: a + b, parts)
            dk_ref[...] = ck_ref[...] + total(dkp)
            dv_ref[...] = cv_ref[...] + total(dvp)
            ck_ref[...] = total(dko)
            cv_ref[...] = total(dvo)

        @pl.when(i == nq)
        def _():
            dk_ref[...] = ck_ref[...]
            dv_ref[...] = cv_ref[...]

    last = nq - 1
    prev = lambda g, i: (jnp.maximum(i - 1, 0), g)
    own = lambda g, i: (jnp.minimum(i, last), g)
    qspec = pl.BlockSpec((blk, pps * LANES), own)
    kspec = lambda m: pl.BlockSpec((blk, LANES), m)
    sspec = pl.BlockSpec((8, LANES), lambda g, i: (g, 0))
    return pl.pallas_call(
        body, name="swa_bwd", grid=(SWA_KV_HEADS, nq + 1),
        in_specs=[qspec, kspec(prev), kspec(own), kspec(prev), kspec(own), sspec, qspec, qspec, qspec],
        out_specs=[qspec, kspec(prev), kspec(prev), sspec],
        out_shape=[jax.ShapeDtypeStruct(q.shape, F32), jax.ShapeDtypeStruct(k.shape, F32),
                   jax.ShapeDtypeStruct(v.shape, F32), jax.ShapeDtypeStruct((SWA_HEADS, LANES), F32)],
        scratch_shapes=[pltpu.VMEM((blk, LANES), F32), pltpu.VMEM((blk, LANES), F32)],
        compiler_params=_params(dimension_semantics=("parallel", "arbitrary")),
    )(q, k, k, v, v, sink, lse, o, do)


def _make_attn(mode):
    swa = mode == "swa"

    @jax.custom_vjp
    def attn(q, k, v, extra):
        return fwd(q, k, v, extra)[0]

    def fwd(q, k, v, extra):
        o, lse = _attn_fwd_call(mode, q, k, v, extra)
        return o, (q, k, v, extra, o, lse)

    def bwd(res, do):
        q, k, v, extra, o, lse = res
        outs = (_swa_bwd_call(q, k, v, extra, lse, o, do) if swa
                else _attn_bwd_call(mode, q, k, v, extra, lse, o, do))
        dq, dk, dv = outs[:3]
        if mode == "fox":
            cum, cumt = extra
            dck = outs[3]
            dcumt = dck[:, :2, :].reshape(FOX_HEADS, -1)
            dcq = outs[4].reshape(-1, FOX_HEADS, HALF)[:, :, 0]
            dextra = (jnp.pad(dcq, ((0, 0), (0, LANES - FOX_HEADS))), dcumt)
        elif mode == "swa":
            dextra = jnp.where(jnp.arange(LANES)[None, :] == 0, outs[3], 0.0)
        else:
            dextra = None
        return dq, dk, dv, dextra

    attn.defvjp(fwd, bwd)
    return attn


attn_mla = _make_attn("mla")
attn_fox = _make_attn("fox")
attn_swa = _make_attn("swa")


def _ukv_layout(w):
    r = w.shape[0]
    w3 = w.reshape(r, MLA_HEADS, MLA_NOPE + MLA_V)
    wk = jnp.pad(w3[:, :, :MLA_NOPE], ((0, 0), (0, 0), (0, LANES - MLA_NOPE))).reshape(r, MLA_HEADS * LANES)
    wv = w3[:, :, MLA_NOPE:].reshape(r, MLA_HEADS * MLA_V)
    return wk, wv


def _even_layer(x, w_in_cat, q_norm, w_uq_p, kv_norm, w_ukv, b_f, w_out, ln_g, ln_b, tabs_mla):
    tabs, hs = tabs_mla
    cq, ckv, kpe, fq, fk, fv, fl, gate = even_in_proj(x, relayout(w_in_cat, "even"), tabs)
    q = mm_rope(rms_norm(cq, q_norm), w_uq_p, tabs, hs)
    ckvn = rms_norm(ckv, kv_norm)
    wk, wv = _ukv_layout(w_ukv)
    kk = mm(ckvn, wk) + jnp.tile(kpe, (1, MLA_HEADS))
    o_mla = attn_mla(q, kk, mm(ckvn, wv), None)
    cum = fox_cum(fl, jnp.pad(b_f, (0, LANES - FOX_HEADS)).reshape(1, LANES))
    o_fox = attn_fox(fq, fk, fv, (cum, cum[:, :8].T))
    y = gated_mm((o_mla, o_fox), gate, w_out)
    return ln_res(x, y, ln_g, ln_b)


def _odd_layer(x, w_in_cat, sinks, w_out, ln_g, ln_b, tabs_swa):
    q, kd, vd, gate = odd_in_proj(x, relayout(w_in_cat, "odd"), tabs_swa[0])
    o = attn_swa(q, kd, vd, jnp.broadcast_to(sinks[:, None], (SWA_HEADS, LANES)))
    y = gated_mm((o,), gate, w_out)
    return ln_res(x, y, ln_g, ln_b)


EVEN_SHARDED = ["even_w_in", "even_w_uq", "even_w_ukv", "even_w_out"]
ODD_SHARDED = ["odd_w_in", "odd_w_out", "odd_ln_g", "odd_ln_b"]
EVEN_REPL = ["even_q_norm", "even_kv_norm", "even_b_f", "even_ln_g", "even_ln_b"]
ODD_REPL = ["odd_sinks"]


def _layer_names(layer):
    return (EVEN_SHARDED, EVEN_REPL) if layer % 2 == 0 else (ODD_SHARDED, ODD_REPL)


def _layer_of(name, j):
    return 2 * j if name.startswith("even") else 2 * j + 1


def _layer_apply(layer, p, x, tabs):
    if layer % 2 == 0:
        return _even_layer(x, p["even_w_in"], p["even_q_norm"], p["even_w_uq"], p["even_kv_norm"], p["even_w_ukv"],
                           p["even_b_f"], p["even_w_out"], p["even_ln_g"], p["even_ln_b"], tabs["mla"])
    return _odd_layer(x, p["odd_w_in"], p["odd_sinks"], p["odd_w_out"], p["odd_ln_g"], p["odd_ln_b"], tabs["swa"])


def _pad_rows(flat, mult):
    n = flat.shape[-1]
    per = mult * LANES
    padded = -(-n // per) * per
    if padded != n:
        flat = jnp.pad(flat, [(0, 0)] * (flat.ndim - 1) + [(0, padded - n)])
    return flat.reshape(flat.shape[:-1] + (padded // LANES, LANES))


def _pad_last(a, width):
    if a.shape[-1] == width:
        return a
    return jnp.pad(a, [(0, 0)] * (a.ndim - 1) + [(0, width - a.shape[-1])])


def _join(slots, axis):
    shp = list(slots.shape[1:])
    shp[axis] *= N_DEV
    return jnp.moveaxis(slots, 0, axis).reshape(shp)


def _split(full, axis):
    shp = full.shape
    t = full.reshape(shp[:axis] + (N_DEV, shp[axis] // N_DEV) + shp[axis + 1:])
    return jnp.moveaxis(t, axis, 0)


PAD_TO = {"even_w_in": SHARD_PAD, "even_w_uq": LANES, "odd_w_in": SHARD_PAD}


def kernel(x, even_w_in, even_q_norm, even_w_uq, even_kv_norm, even_w_ukv, even_b_f, even_w_out, even_ln_g, even_ln_b, odd_w_in, odd_sinks, odd_w_out, odd_ln_g, odd_ln_b, loss_target, m_even_w_in, m_even_q_norm, m_even_w_uq, m_even_kv_norm, m_even_w_ukv, m_even_b_f, m_even_w_out, m_even_ln_g, m_even_ln_b, m_odd_w_in, m_odd_sinks, m_odd_w_out, m_odd_ln_g, m_odd_ln_b, v_even_w_in, v_even_q_norm, v_even_w_uq, v_even_kv_norm, v_even_w_ukv, v_even_b_f, v_even_w_out, v_even_ln_g, v_even_ln_b, v_odd_w_in, v_odd_sinks, v_odd_w_out, v_odd_ln_g, v_odd_ln_b):
    w = dict(even_w_in=even_w_in, even_q_norm=even_q_norm, even_w_uq=even_w_uq, even_kv_norm=even_kv_norm,
             even_w_ukv=even_w_ukv, even_b_f=even_b_f, even_w_out=even_w_out, even_ln_g=even_ln_g, even_ln_b=even_ln_b,
             odd_w_in=odd_w_in, odd_sinks=odd_sinks, odd_w_out=odd_w_out, odd_ln_g=odd_ln_g, odd_ln_b=odd_ln_b)
    mom = dict(even_w_in=m_even_w_in, even_q_norm=m_even_q_norm, even_w_uq=m_even_w_uq, even_kv_norm=m_even_kv_norm,
               even_w_ukv=m_even_w_ukv, even_b_f=m_even_b_f, even_w_out=m_even_w_out, even_ln_g=m_even_ln_g,
               even_ln_b=m_even_ln_b, odd_w_in=m_odd_w_in, odd_sinks=m_odd_sinks, odd_w_out=m_odd_w_out,
               odd_ln_g=m_odd_ln_g, odd_ln_b=m_odd_ln_b)
    vel = dict(even_w_in=v_even_w_in, even_q_norm=v_even_q_norm, even_w_uq=v_even_w_uq, even_kv_norm=v_even_kv_norm,
               even_w_ukv=v_even_w_ukv, even_b_f=v_even_b_f, even_w_out=v_even_w_out, even_ln_g=v_even_ln_g,
               even_ln_b=v_even_ln_b, odd_w_in=v_odd_w_in, odd_sinks=v_odd_sinks, odd_w_out=v_odd_w_out,
               odd_ln_g=v_odd_ln_g, odd_ln_b=v_odd_ln_b)
    sharded = BIG + SMALL_SHARDED
    padded = lambda d, n: _pad_last(d[n], PAD_TO.get(n, d[n].shape[-1]))

    tabs = {"mla": _rope_tables(x.shape[1], "mla"), "swa": _rope_tables(x.shape[1], "swa")}
    keys = lambda layers: [(n, layer // 2) for layer in layers for n in _layer_names(layer)[0]]
    first, rest = keys([0]), keys([1, 2, 3])
    me = _lin(_me())

    def shard(n, j):
        a = padded(w, n)[j]
        return a.astype(BF16) if n in BIG else a

    to_full = lambda n, g: _join(g, SHARD_AXIS[n] - 1).astype(F32)
    to_slots = lambda n, g: _split(g, SHARD_AXIS[n] - 1).astype(BF16 if n in BIG else F32)

    def layer_params(layer, full_of):
        shn, rpn = _layer_names(layer)
        p = {n: full_of(n) for n in shn}
        p.update({n: w[n][layer // 2] for n in rpn})
        return p

    got0 = _all_gather([shard(n, j) for n, j in first], "all_gather_first")
    got0, mine_rest = lax.optimization_barrier((got0, [shard(n, j) for n, j in rest]))
    got0 = dict(zip(first, got0))
    send_sems, recv_sems, srcs, lands, token = _split_start(mine_rest, False, "all_gather_rest_start")
    p0 = layer_params(0, lambda n: to_full(n, got0[(n, 0)]) + token[0, 0])
    x1, vjp0 = jax.vjp(lambda p, xx: _layer_apply(0, p, xx, tabs), p0, x[0])
    got = _split_wait(send_sems, recv_sems, srcs, lands, x1, False, "all_gather_rest_wait")
    got = dict(zip(rest, _own_slot(got, [m[None] for m in mine_rest])))

    xs, vjps = x1, [vjp0]
    for layer in (1, 2, 3):
        p = layer_params(layer, lambda n: to_full(n, got[(n, layer // 2)]))
        xs, vjp = jax.vjp(lambda p_, xx, layer=layer: _layer_apply(layer, p_, xx, tabs), p, xs)
        vjps.append(vjp)
    loss_local, vjp_loss = jax.vjp(lambda y: mse_loss(y, loss_target[0]), xs)
    (dy,) = vjp_loss(jnp.ones((), F32))
    loss = lax.psum(loss_local, AXES)
    grads = {}
    for layer in (3, 2, 1):
        grads[layer], dy = vjps[layer](dy)

    parts_rest = [to_slots(n, grads[_layer_of(n, j)][n]) for n, j in rest]
    send_sems, recv_sems, srcs, lands, token = _split_start(parts_rest, True, "grad_exchange_rest_start")
    grads[0], grad_x = vjps[0](dy + token[0, 0])
    recv_rest = _split_wait(send_sems, recv_sems, srcs, lands, grad_x, True, "grad_exchange_rest_wait")
    recv = dict(zip(rest, _own_slot(recv_rest, [lax.dynamic_slice_in_dim(p, me, 1, axis=0) for p in parts_rest])))
    repl_grad = lambda n: jnp.stack([grads[_layer_of(n, j)][n] for j in (0, 1)])
    repl_rows = _pad_rows(jnp.concatenate([repl_grad(n).reshape(-1) for n in REPL]), 8)
    parts_last = [to_slots(n, grads[0][n]) for n, j in first]
    parts_last.append(jnp.broadcast_to(repl_rows[None], (N_DEV,) + repl_rows.shape))
    recv_last = _exchange(parts_last, "grad_exchange_last")
    recv.update(zip(first, recv_last[:-1]))

    g_out, d_out, m_out, v_out = {}, {}, {}, {}
    for n in sharded:
        r = jnp.stack([recv[(n, 0)], recv[(n, 1)]], axis=1)
        cols = r.shape[-1]
        flat2 = lambda d: padded(d, n).reshape(-1, cols)
        outs = _sum_adamw(r.reshape(N_DEV, -1, cols), flat2(w), flat2(mom), flat2(vel))
        for dst, o in zip((g_out, d_out, m_out, v_out), outs):
            dst[n] = o.reshape(w[n].shape[:-1] + (cols,))[..., :w[n].shape[-1]]
    pack = lambda d: _pad_rows(jnp.concatenate([d[n].reshape(-1) for n in REPL]), 8)
    outs = _sum_adamw(recv_last[-1], pack(w), pack(mom), pack(vel))
    for dst, o in zip((g_out, d_out, m_out, v_out), outs):
        flat, off = o.reshape(-1), 0
        for n in REPL:
            size = math.prod(w[n].shape)
            dst[n] = flat[off:off + size].reshape(w[n].shape)
            off += size
    return (loss, grad_x[None], *[g_out[n] for n in WEIGHTS], *[d_out[n] for n in WEIGHTS],
            *[m_out[n] for n in WEIGHTS], *[v_out[n] for n in WEIGHTS])
```

```python
import functools
import math

import jax
import jax.numpy as jnp
from jax import lax
from jax.experimental import pallas as pl
from jax.experimental.pallas import tpu as pltpu

F32 = jnp.float32
BF16 = jnp.bfloat16
LANES = 128
HALF = 64
N_DEV = 8
AXES = ("x", "y", "c")
VMEM_LIMIT = 48 * 1024 * 1024
VMEM_LIMIT_TALL = 56 * 1024 * 1024

D_MODEL = 1024
DEPTH = 4
ROPE_THETA = 10000.0
MLA_HEADS, MLA_NOPE, MLA_ROPE, MLA_V, MLA_Q_RANK, MLA_KV_RANK = 8, 64, 32, 64, 256, 128
FOX_HEADS, FOX_DIM = 8, 64
SWA_HEADS, SWA_KV_HEADS, SWA_DIM, WINDOW = 16, 2, 64, 128
RMS_EPS, LN_EPS = 1e-6, 1e-5
ALPHA = (2 * DEPTH) ** 0.25
ADAM_LR, ADAM_B1, ADAM_B2, ADAM_EPS, ADAM_WD, ADAM_STEP = 0.001, 0.9, 0.999, 1e-08, 0.01, 10
NEG = -1e30

WEIGHTS = ["even_w_in", "even_q_norm", "even_w_uq", "even_kv_norm", "even_w_ukv", "even_b_f", "even_w_out",
           "even_ln_g", "even_ln_b", "odd_w_in", "odd_sinks", "odd_w_out", "odd_ln_g", "odd_ln_b"]
SHARD_AXIS = {"even_w_in": 2, "even_w_uq": 2, "even_w_ukv": 2, "even_w_out": 1, "odd_w_in": 2, "odd_w_out": 1,
              "odd_ln_g": 1, "odd_ln_b": 1, "even_q_norm": None, "even_kv_norm": None, "even_b_f": None,
              "even_ln_g": None, "even_ln_b": None, "odd_sinks": None}
BIG = ["even_w_in", "even_w_uq", "even_w_ukv", "even_w_out", "odd_w_in", "odd_w_out"]
SMALL_SHARDED = ["odd_ln_g", "odd_ln_b"]
REPL = [n for n in WEIGHTS if SHARD_AXIS[n] is None]


def _pick(n, cands):
    for c in cands:
        if n % c == 0:
            return c
    return n


def _params(**kw):
    return pltpu.CompilerParams(vmem_limit_bytes=VMEM_LIMIT, **kw)


def _me():
    return lax.axis_index("x"), lax.axis_index("y"), lax.axis_index("c")


def _peer(k):
    x, y, c = _me()
    px = 1 - x if (k >> 2) & 1 else x
    py = 1 - y if (k >> 1) & 1 else y
    pc = 1 - c if k & 1 else c
    return px, py, pc


def _lin(p):
    return 4 * p[0] + 2 * p[1] + p[2]


def _comm_call(body, n, out_shape, args, name):
    any_spec = pl.BlockSpec(memory_space=pl.ANY)
    return pl.pallas_call(
        body, name=name, out_shape=out_shape, in_specs=[any_spec] * n, out_specs=[any_spec] * n,
        scratch_shapes=[pltpu.SemaphoreType.DMA((n, N_DEV - 1)), pltpu.SemaphoreType.DMA((n, N_DEV - 1)),
                        pltpu.SemaphoreType.DMA((n,))],
    )(*args)


def _all_gather(xs, name):
    n = len(xs)

    def body(*refs):
        x_refs, out_refs = refs[:n], refs[n:2 * n]
        send_sems, recv_sems, local_sems = refs[2 * n:]
        me = _lin(_me())
        local = [pltpu.make_async_copy(x_refs[a], out_refs[a].at[me], local_sems.at[a]) for a in range(n)]
        for cp in local:
            cp.start()
        sends = []
        for k in range(1, N_DEV):
            for a in range(n):
                cp = pltpu.make_async_remote_copy(
                    src_ref=x_refs[a], dst_ref=out_refs[a].at[me], send_sem=send_sems.at[a, k - 1],
                    recv_sem=recv_sems.at[a, k - 1], device_id=_peer(k), device_id_type=pl.DeviceIdType.MESH)
                cp.start()
                sends.append(cp)
        for k in range(1, N_DEV):
            for a in range(n):
                pltpu.make_async_remote_copy(
                    src_ref=x_refs[a], dst_ref=out_refs[a].at[_lin(_peer(k))], send_sem=send_sems.at[a, k - 1],
                    recv_sem=recv_sems.at[a, k - 1], device_id=_peer(k),
                    device_id_type=pl.DeviceIdType.MESH).wait_recv()
        for cp in sends:
            cp.wait_send()
        for cp in local:
            cp.wait()

    out_shape = [jax.ShapeDtypeStruct((N_DEV,) + x.shape, x.dtype) for x in xs]
    return _comm_call(body, n, out_shape, xs, name)


def _exchange(parts, name):
    n = len(parts)

    def body(*refs):
        p_refs, out_refs = refs[:n], refs[n:2 * n]
        send_sems, recv_sems, local_sems = refs[2 * n:]
        me = _lin(_me())
        local = [pltpu.make_async_copy(p_refs[a].at[me], out_refs[a].at[me], local_sems.at[a]) for a in range(n)]
        for cp in local:
            cp.start()
        sends = []
        for k in range(1, N_DEV):
            peer = _peer(k)
            for a in range(n):
                cp = pltpu.make_async_remote_copy(
                    src_ref=p_refs[a].at[_lin(peer)], dst_ref=out_refs[a].at[me], send_sem=send_sems.at[a, k - 1],
                    recv_sem=recv_sems.at[a, k - 1], device_id=peer, device_id_type=pl.DeviceIdType.MESH)
                cp.start()
                sends.append(cp)
        for k in range(1, N_DEV):
            peer = _peer(k)
            for a in range(n):
                pltpu.make_async_remote_copy(
                    src_ref=p_refs[a].at[_lin(peer)], dst_ref=out_refs[a].at[_lin(peer)],
                    send_sem=send_sems.at[a, k - 1], recv_sem=recv_sems.at[a, k - 1], device_id=peer,
                    device_id_type=pl.DeviceIdType.MESH).wait_recv()
        for cp in sends:
            cp.wait_send()
        for cp in local:
            cp.wait()

    out_shape = [jax.ShapeDtypeStruct(p.shape, p.dtype) for p in parts]
    return _comm_call(body, n, out_shape, parts, name)


_HBM = pl.BlockSpec(memory_space=pltpu.HBM)
_SEM = pl.BlockSpec(memory_space=pltpu.SEMAPHORE)
_EFFECT = pltpu.SideEffectType.DATAFLOW_SIDE_EFFECTING


def _split_start(srcs, slotted, name):
    n = len(srcs)
    lands = [lax.empty(s.shape if slotted else (N_DEV,) + s.shape, s.dtype) for s in srcs]

    def body(*refs):
        src_refs, land_refs = refs[:n], refs[n:2 * n]
        send_sems, recv_sems, token = refs[2 * n], refs[2 * n + 1], refs[-1]
        me = _lin(_me())
        for k in range(1, N_DEV):
            peer = _peer(k)
            for a in range(n):
                pltpu.make_async_remote_copy(
                    src_ref=src_refs[a].at[_lin(peer)] if slotted else src_refs[a], dst_ref=land_refs[a].at[me],
                    send_sem=send_sems.at[a * (N_DEV - 1) + k - 1], recv_sem=recv_sems.at[a * (N_DEV - 1) + k - 1],
                    device_id=peer, device_id_type=pl.DeviceIdType.MESH).start()
        token[...] = jnp.zeros_like(token)

    both = list(srcs) + lands
    outs = pl.pallas_call(
        body, name=name,
        out_shape=(pltpu.SemaphoreType.DMA((n * (N_DEV - 1),)), pltpu.SemaphoreType.DMA((n * (N_DEV - 1),)),
                   *[pltpu.HBM(b.shape, b.dtype) for b in both], jax.ShapeDtypeStruct((8, LANES), F32)),
        in_specs=[_HBM] * (2 * n), out_specs=(_SEM, _SEM, *[_HBM] * (2 * n), pl.BlockSpec(memory_space=pltpu.VMEM)),
        input_output_aliases={a: 2 + a for a in range(2 * n)},
        compiler_params=pltpu.CompilerParams(has_side_effects=_EFFECT),
    )(*[pltpu.with_memory_space_constraint(b, pltpu.HBM) for b in both])
    return outs[0], outs[1], list(outs[2:2 + n]), list(outs[2 + n:2 + 2 * n]), outs[-1]


def _split_wait(send_sems, recv_sems, srcs, lands, after, slotted, name):
    n = len(srcs)

    def body(*refs):
        src_refs, land_refs = refs[:n], refs[n:2 * n]
        send_sems, recv_sems = refs[2 * n], refs[2 * n + 1]
        for k in range(1, N_DEV):
            peer = _peer(k)
            for a in range(n):
                cp = pltpu.make_async_remote_copy(
                    src_ref=src_refs[a].at[_lin(peer)] if slotted else src_refs[a],
                    dst_ref=land_refs[a].at[_lin(peer)], send_sem=send_sems.at[a * (N_DEV - 1) + k - 1],
                    recv_sem=recv_sems.at[a * (N_DEV - 1) + k - 1], device_id=peer,
                    device_id_type=pl.DeviceIdType.MESH)
                cp.wait_send()
                cp.wait_recv()

    both = list(srcs) + list(lands)
    outs = pl.pallas_call(
        body, name=name, out_shape=[pltpu.HBM(b.shape, b.dtype) for b in both],
        in_specs=[_HBM] * (2 * n) + [_SEM, _SEM, pl.BlockSpec(memory_space=pl.ANY)], out_specs=[_HBM] * (2 * n),
        input_output_aliases={a: a for a in range(2 * n)},
        compiler_params=pltpu.CompilerParams(has_side_effects=_EFFECT),
    )(*both, send_sems, recv_sems, after)
    return list(outs[n:])


def _own_slot(lands, own):
    me = _lin(_me())
    slot = lambda l: lax.broadcasted_iota(jnp.int32, (N_DEV,) + (1,) * (l.ndim - 1), 0)
    return [jnp.where(slot(l) == me, o.astype(l.dtype), l) for l, o in zip(lands, own)]


def _sum_adamw(recv, w, m, v):
    _, rows, lanes = recv.shape
    tr = _pick(rows, (256, 128, 64, 32, 16, 8))
    c1 = 1.0 - ADAM_B1 ** ADAM_STEP
    c2 = 1.0 - ADAM_B2 ** ADAM_STEP

    def body(r_ref, w_ref, m_ref, v_ref, g_out, d_out, m_out, v_out):
        g = r_ref[0].astype(F32)
        for s in range(1, N_DEV):
            g = g + r_ref[s].astype(F32)
        mn = ADAM_B1 * m_ref[...] + (1.0 - ADAM_B1) * g
        vn = ADAM_B2 * v_ref[...] + (1.0 - ADAM_B2) * (g * g)
        m_hat = mn / c1
        v_hat = vn / c2
        g_out[...] = g
        d_out[...] = -ADAM_LR * (m_hat / (jnp.sqrt(v_hat) + ADAM_EPS) + ADAM_WD * w_ref[...])
        m_out[...] = mn
        v_out[...] = vn

    blk = pl.BlockSpec((tr, lanes), lambda i: (i, 0))
    shp = jax.ShapeDtypeStruct((rows, lanes), F32)
    return pl.pallas_call(
        body, name=f"sum_adamw_{rows}x{lanes}", grid=(rows // tr,),
        in_specs=[pl.BlockSpec((N_DEV, tr, lanes), lambda i: (0, i, 0)), blk, blk, blk],
        out_specs=[blk, blk, blk, blk], out_shape=[shp, shp, shp, shp],
        compiler_params=_params(dimension_semantics=("parallel",)),
    )(recv, w, m, v)


def _rope_block(xb, cv, s1v, s2v, hs):
    return xb * cv + pltpu.roll(xb, LANES - hs, 1) * s1v + pltpu.roll(xb, hs, 1) * s2v


def _mm_nn(a, b, b_transposed=False, rope=None):
    m, k = a.shape
    n = b.shape[0] if b_transposed else b.shape[1]
    tm = _pick(m, (1024, 512, 256, 128))
    tn = _pick(n, (1024, 640, 512, 256, 128))
    tk = _pick(k, (1024, 640, 512, 256, 128))
    nk = k // tk

    def body(*refs):
        a_ref, b_ref = refs[:2]
        o_ref, acc_ref = refs[-2:]
        kk = pl.program_id(2)

        @pl.when(kk == 0)
        def _():
            acc_ref[...] = jnp.zeros_like(acc_ref)

        dims = (((1,), (1,)), ((), ())) if b_transposed else (((1,), (0,)), ((), ()))
        acc_ref[...] += lax.dot_general(a_ref[...].astype(BF16), b_ref[...].astype(BF16), dims,
                                        preferred_element_type=F32)

        @pl.when(kk == nk - 1)
        def _():
            if rope is None:
                o_ref[...] = acc_ref[...]
            else:
                cv, s1v, s2v = refs[2][...], refs[3][...], refs[4][...]
                for cb in range(tn // LANES):
                    sl = slice(cb * LANES, (cb + 1) * LANES)
                    o_ref[:, sl] = _rope_block(acc_ref[:, sl], cv, s1v, s2v, rope[3])

    b_spec = (pl.BlockSpec((tn, tk), lambda i, j, kk: (j, kk)) if b_transposed
              else pl.BlockSpec((tk, tn), lambda i, j, kk: (kk, j)))
    in_specs, args = [pl.BlockSpec((tm, tk), lambda i, j, kk: (i, kk)), b_spec], [a, b]
    if rope is not None:
        in_specs += [pl.BlockSpec((tm, LANES), lambda i, j, kk: (i, 0))] * 3
        args += list(rope[:3])
    tag = ("t" if b_transposed else "n") + ("" if rope is None else f"_rope{rope[3]}")
    return pl.pallas_call(
        body, name=f"mm_n{tag}_{m}x{k}x{n}", grid=(m // tm, n // tn, nk), in_specs=in_specs,
        out_specs=pl.BlockSpec((tm, tn), lambda i, j, kk: (i, j)),
        out_shape=jax.ShapeDtypeStruct((m, n), F32),
        scratch_shapes=[pltpu.VMEM((tm, tn), F32)],
        compiler_params=_params(dimension_semantics=("parallel", "parallel", "arbitrary")),
    )(*args)


def _mm_tn(a, g):
    s, k = a.shape
    _, n = g.shape
    tm = _pick(k, (1024, 512, 256, 128))
    tn = _pick(n, (1024, 640, 512, 256, 128))
    ts = _pick(s, (512, 256, 128))
    ns = s // ts

    def body(a_ref, g_ref, o_ref, acc_ref):
        ss = pl.program_id(2)

        @pl.when(ss == 0)
        def _():
            acc_ref[...] = jnp.zeros_like(acc_ref)

        acc_ref[...] += lax.dot_general(a_ref[...].astype(BF16), g_ref[...].astype(BF16),
                                        (((0,), (0,)), ((), ())), preferred_element_type=F32)

        @pl.when(ss == ns - 1)
        def _():
            o_ref[...] = acc_ref[...]

    return pl.pallas_call(
        body, name=f"mm_tn_{s}x{k}x{n}", grid=(k // tm, n // tn, ns),
        in_specs=[pl.BlockSpec((ts, tm), lambda i, j, ss: (ss, i)), pl.BlockSpec((ts, tn), lambda i, j, ss: (ss, j))],
        out_specs=pl.BlockSpec((tm, tn), lambda i, j, ss: (i, j)),
        out_shape=jax.ShapeDtypeStruct((k, n), F32),
        scratch_shapes=[pltpu.VMEM((tm, tn), F32)],
        compiler_params=_params(dimension_semantics=("parallel", "parallel", "arbitrary")),
    )(a, g)


@jax.custom_vjp
def mm(a, w):
    return _mm_nn(a, w.astype(BF16))


def _mm_fwd(a, w):
    wb = w.astype(BF16)
    return _mm_nn(a, wb), (a, wb)


def _mm_bwd(res, g):
    a, wb = res
    return _mm_nn(g, wb, b_transposed=True), _mm_tn(a, g)


mm.defvjp(_mm_fwd, _mm_bwd)


def _unrope(g, tabs, hs):
    return _rope_call(g, tabs[0], -tabs[1], -tabs[2], hs)


@functools.partial(jax.custom_vjp, nondiff_argnums=(3,))
def mm_rope(a, w, tabs, hs):
    return _mm_nn(a, w.astype(BF16), rope=(*tabs, hs))


def _mm_rope_fwd(a, w, tabs, hs):
    wb = w.astype(BF16)
    return _mm_nn(a, wb, rope=(*tabs, hs)), (a, wb, tabs)


def _mm_rope_bwd(hs, res, g):
    a, wb, tabs = res
    g = _unrope(g, tabs, hs)
    return _mm_nn(g, wb, b_transposed=True), _mm_tn(a, g), jax.tree.map(jnp.zeros_like, tabs)


mm_rope.defvjp(_mm_rope_fwd, _mm_rope_bwd)


def _proj_dx_call(gs, wb, cuts):
    s, (d, n), ng = gs[0].shape[0], wb.shape, len(gs)
    tm = _pick(s, (512, 256, 128))

    def body(*refs):
        w_ref, o_ref = refs[ng], refs[ng + 1]
        for gi in range(ng):
            part = lax.dot_general(refs[gi][...].astype(BF16), w_ref[:, cuts[gi]:cuts[gi + 1]],
                                   (((1,), (1,)), ((), ())), preferred_element_type=F32)
            if gi == 0:
                o_ref[...] = part
            else:
                o_ref[...] += part

    return pl.pallas_call(
        body, name=f"proj_dx_{ng}", grid=(s // tm,),
        in_specs=[pl.BlockSpec((tm, g.shape[1]), lambda i: (i, 0)) for g in gs] + [pl.BlockSpec((d, n), lambda i: (0, 0))],
        out_specs=pl.BlockSpec((tm, d), lambda i: (i, 0)), out_shape=jax.ShapeDtypeStruct((s, d), F32),
        compiler_params=_params(dimension_semantics=("parallel",)),
    )(*gs, wb)


def _proj_dw_call(xb, gs, cuts):
    (s, d), ng, n = xb.shape, len(gs), cuts[-1]
    ts = _pick(s, (512, 256, 128))
    ns = s // ts

    def body(*refs):
        x_ref, o_ref = refs[0], refs[ng + 1]
        ss = pl.program_id(0)
        xt = x_ref[...].T
        for gi in range(ng):
            cols = slice(cuts[gi], cuts[gi + 1])
            part = jnp.dot(xt, refs[1 + gi][...].astype(BF16), preferred_element_type=F32)

            @pl.when(ss == 0)
            def _():
                o_ref[:, cols] = part

            @pl.when(ss > 0)
            def _():
                o_ref[:, cols] += part

    return pl.pallas_call(
        body, name=f"proj_dw_{ng}", grid=(ns,),
        in_specs=[pl.BlockSpec((ts, d), lambda ss: (ss, 0))] + [pl.BlockSpec((ts, g.shape[1]), lambda ss: (ss, 0)) for g in gs],
        out_specs=pl.BlockSpec((d, n), lambda ss: (0, 0)), out_shape=jax.ShapeDtypeStruct((d, n), F32),
        compiler_params=pltpu.CompilerParams(vmem_limit_bytes=VMEM_LIMIT_TALL, dimension_semantics=("arbitrary",)),
    )(xb, *gs)


def _make_in_proj(widths, roped, hs):
    cuts = [sum(widths[:i]) for i in range(len(widths) + 1)]

    @jax.custom_vjp
    def in_proj(x, w, tabs):
        return fwd(x, w, tabs)[0]

    def fwd(x, w, tabs):
        xb, wb = x.astype(BF16), w.astype(BF16)
        outs = tuple(_mm_nn(xb, wb[:, a:b], rope=(*tabs, hs) if gi in roped else None)
                     for gi, (a, b) in enumerate(zip(cuts[:-1], cuts[1:])))
        return outs, (xb, wb, tabs)

    def bwd(res, gs):
        xb, wb, tabs = res
        gs = [_unrope(gg, tabs, hs) if gi in roped else gg for gi, gg in enumerate(gs)]
        return _proj_dx_call(gs, wb, cuts), _proj_dw_call(xb, gs, cuts), jax.tree.map(jnp.zeros_like, tabs)

    in_proj.defvjp(fwd, bwd)
    return in_proj


EVEN_GROUPS = (256, 128, 128, 512, 512, 512, 128, 1024)
ODD_GROUPS = (1024, 256, 256, 1024)
even_in_proj = _make_in_proj(EVEN_GROUPS, roped=(2,), hs=MLA_ROPE // 2)
odd_in_proj = _make_in_proj(ODD_GROUPS, roped=(0, 1), hs=SWA_DIM // 2)

SHARD_PAD = 384


def _source_columns(kind):
    if kind == "even":
        src = [list(range(0, 384)), [-1] * 64, list(range(384, 416)), [-1] * 32, list(range(416, 1952)),
               list(range(1952, 1960)), [-1] * 120, list(range(1960, 2984))]
        return sum(src, []), 373
    q0, k0, v0, g0 = 0, 1024, 1152, 1280
    dup = lambda base: [base + 64 * g + c for g in range(SWA_KV_HEADS) for _ in range(2) for c in range(64)]
    return list(range(q0, k0)) + dup(k0) + dup(v0) + list(range(g0, 2304)), 288


def _selection(kind, transposed):
    src, shard = _source_columns(kind)
    cat = [s + (SHARD_PAD - shard) * (s // shard) if s >= 0 else -1 for s in src]
    cat_arr = jnp.asarray(cat, jnp.int32)
    if transposed:
        cols = lax.broadcasted_iota(jnp.int32, (len(src), N_DEV * SHARD_PAD), 1)
        return (cols == cat_arr[:, None]).astype(BF16), [(c, r) for c, r in enumerate(cat) if r >= 0]
    rows = lax.broadcasted_iota(jnp.int32, (N_DEV * SHARD_PAD, len(src)), 0)
    return (rows == cat_arr[None, :]).astype(BF16), [(r, c) for c, r in enumerate(cat) if r >= 0]


def _mm_banded(a, b, nonzeros, a_slots=False, out_slots=False):
    k, n = b.shape
    m = a.shape[1] if a_slots else a.shape[0]
    tm = _pick(m, (1024, 512, 256, 128))
    tn = SHARD_PAD if out_slots else _pick(n, (1024, 640, 512, 256, 128))
    tk = SHARD_PAD if a_slots else _pick(k, (1024, 640, 512, 256, 128))
    lo, hi = [k // tk] * (n // tn), [-1] * (n // tn)
    for r, c in nonzeros:
        lo[c // tn], hi[c // tn] = min(lo[c // tn], r // tk), max(hi[c // tn], r // tk)
    first = [l if h >= 0 else 0 for l, h in zip(lo, hi)]
    count = [h - l + 1 if h >= 0 else 0 for l, h in zip(lo, hi)]
    steps = max(count)

    def body(first_ref, count_ref, a_ref, b_ref, o_ref, acc_ref):
        j, kk = pl.program_id(1), pl.program_id(2)

        @pl.when(kk == 0)
        def _():
            acc_ref[...] = jnp.zeros_like(acc_ref)

        @pl.when(kk < count_ref[j])
        def _():
            acc_ref[...] += jnp.dot(a_ref[...].astype(BF16), b_ref[...].astype(BF16), preferred_element_type=F32)

        @pl.when(kk == steps - 1)
        def _():
            o_ref[...] = acc_ref[...].astype(o_ref.dtype)

    kblk = lambda j, kk, f, c: jnp.minimum(f[j] + kk, f[j] + jnp.maximum(c[j], 1) - 1)
    a_spec = (pl.BlockSpec((None, tm, tk), lambda i, j, kk, f, c: (kblk(j, kk, f, c), i, 0)) if a_slots
              else pl.BlockSpec((tm, tk), lambda i, j, kk, f, c: (i, kblk(j, kk, f, c))))
    o_spec = (pl.BlockSpec((None, tm, tn), lambda i, j, kk, f, c: (j, i, 0)) if out_slots
              else pl.BlockSpec((tm, tn), lambda i, j, kk, f, c: (i, j)))
    o_shape = jax.ShapeDtypeStruct((n // tn, m, tn), BF16) if out_slots else jax.ShapeDtypeStruct((m, n), F32)
    return pl.pallas_call(
        body, name=f"mm_banded_{m}x{k}x{n}",
        grid_spec=pltpu.PrefetchScalarGridSpec(
            num_scalar_prefetch=2, grid=(m // tm, n // tn, steps),
            in_specs=[a_spec, pl.BlockSpec((tk, tn), lambda i, j, kk, f, c: (kblk(j, kk, f, c), j))],
            out_specs=o_spec, scratch_shapes=[pltpu.VMEM((tm, tn), F32)]),
        out_shape=o_shape,
        compiler_params=_params(dimension_semantics=("parallel", "parallel", "arbitrary")),
    )(jnp.asarray(first, jnp.int32), jnp.asarray(count, jnp.int32), a, b)


@functools.partial(jax.custom_vjp, nondiff_argnums=(1,))
def relayout(wslots, kind):
    return _mm_banded(wslots, *_selection(kind, False), a_slots=True)


def _relayout_bwd(kind, _, g):
    return (_mm_banded(g, *_selection(kind, True), out_slots=True),)


relayout.defvjp(lambda wslots, kind: (_mm_banded(wslots, *_selection(kind, False), a_slots=True), None),
                _relayout_bwd)


def _row_block(s):
    return _pick(s, (512, 256, 128, 64, 32, 16, 8))


def _rms_fwd_call(x, g):
    s, k = x.shape
    tr = _row_block(s)

    def body(x_ref, g_ref, o_ref):
        xv = x_ref[...]
        r = lax.rsqrt(jnp.mean(xv * xv, axis=-1, keepdims=True) + RMS_EPS)
        o_ref[...] = xv * r * g_ref[...]

    return pl.pallas_call(
        body, name=f"rms_fwd_{k}", grid=(s // tr,),
        in_specs=[pl.BlockSpec((tr, k), lambda i: (i, 0)), pl.BlockSpec((1, k), lambda i: (0, 0))],
        out_specs=pl.BlockSpec((tr, k), lambda i: (i, 0)), out_shape=jax.ShapeDtypeStruct((s, k), F32),
        compiler_params=_params(dimension_semantics=("parallel",)),
    )(x, g.reshape(1, k))


def _rms_bwd_call(x, g, dy):
    s, k = x.shape
    tr = _row_block(s)

    def body(x_ref, g_ref, dy_ref, dx_ref, dg_ref):
        @pl.when(pl.program_id(0) == 0)
        def _():
            dg_ref[...] = jnp.zeros_like(dg_ref)

        xv = x_ref[...]
        r = lax.rsqrt(jnp.mean(xv * xv, axis=-1, keepdims=True) + RMS_EPS)
        xh = xv * r
        dyv = dy_ref[...]
        dg_ref[...] += jnp.sum(dyv * xh, axis=0, keepdims=True)
        dxh = dyv * g_ref[...]
        dx_ref[...] = r * (dxh - xh * jnp.mean(dxh * xh, axis=-1, keepdims=True))

    dx, dg = pl.pallas_call(
        body, name=f"rms_bwd_{k}", grid=(s // tr,),
        in_specs=[pl.BlockSpec((tr, k), lambda i: (i, 0)), pl.BlockSpec((1, k), lambda i: (0, 0)),
                  pl.BlockSpec((tr, k), lambda i: (i, 0))],
        out_specs=[pl.BlockSpec((tr, k), lambda i: (i, 0)), pl.BlockSpec((1, k), lambda i: (0, 0))],
        out_shape=[jax.ShapeDtypeStruct((s, k), F32), jax.ShapeDtypeStruct((1, k), F32)],
        compiler_params=_params(dimension_semantics=("arbitrary",)),
    )(x, g.reshape(1, k), dy)
    return dx, dg.reshape(k)


@jax.custom_vjp
def rms_norm(x, g):
    return _rms_fwd_call(x, g)


rms_norm.defvjp(lambda x, g: (_rms_fwd_call(x, g), (x, g)), lambda res, dy: _rms_bwd_call(res[0], res[1], dy))


def _ln_fwd_call(x, y, g, b):
    s, k = x.shape
    tr = _row_block(s)

    def body(x_ref, y_ref, g_ref, b_ref, o_ref):
        u = ALPHA * x_ref[...] + y_ref[...]
        mu = jnp.mean(u, axis=-1, keepdims=True)
        d = u - mu
        var = jnp.mean(d * d, axis=-1, keepdims=True)
        o_ref[...] = d * lax.rsqrt(var + LN_EPS) * g_ref[...] + b_ref[...]

    row = pl.BlockSpec((tr, k), lambda i: (i, 0))
    vec = pl.BlockSpec((1, k), lambda i: (0, 0))
    return pl.pallas_call(
        body, name="ln_fwd", grid=(s // tr,), in_specs=[row, row, vec, vec], out_specs=row,
        out_shape=jax.ShapeDtypeStruct((s, k), F32), compiler_params=_params(dimension_semantics=("parallel",)),
    )(x, y, g.reshape(1, k), b.reshape(1, k))


def _ln_bwd_call(x, y, g, do):
    s, k = x.shape
    tr = _row_block(s)

    def body(x_ref, y_ref, g_ref, do_ref, dx_ref, dy_ref, dg_ref, db_ref):
        @pl.when(pl.program_id(0) == 0)
        def _():
            dg_ref[...] = jnp.zeros_like(dg_ref)
            db_ref[...] = jnp.zeros_like(db_ref)

        u = ALPHA * x_ref[...] + y_ref[...]
        mu = jnp.mean(u, axis=-1, keepdims=True)
        d = u - mu
        r = lax.rsqrt(jnp.mean(d * d, axis=-1, keepdims=True) + LN_EPS)
        xh = d * r
        dov = do_ref[...]
        dg_ref[...] += jnp.sum(dov * xh, axis=0, keepdims=True)
        db_ref[...] += jnp.sum(dov, axis=0, keepdims=True)
        dxh = dov * g_ref[...]
        du = r * (dxh - jnp.mean(dxh, axis=-1, keepdims=True) - xh * jnp.mean(dxh * xh, axis=-1, keepdims=True))
        dy_ref[...] = du
        dx_ref[...] = ALPHA * du

    row = pl.BlockSpec((tr, k), lambda i: (i, 0))
    vec = pl.BlockSpec((1, k), lambda i: (0, 0))
    dx, dy, dg, db = pl.pallas_call(
        body, name="ln_bwd", grid=(s // tr,), in_specs=[row, row, vec, row], out_specs=[row, row, vec, vec],
        out_shape=[jax.ShapeDtypeStruct((s, k), F32), jax.ShapeDtypeStruct((s, k), F32),
                   jax.ShapeDtypeStruct((1, k), F32), jax.ShapeDtypeStruct((1, k), F32)],
        compiler_params=_params(dimension_semantics=("arbitrary",)),
    )(x, y, g.reshape(1, k), do)
    return dx, dy, dg.reshape(k), db.reshape(k)


@jax.custom_vjp
def ln_res(x, y, g, b):
    return _ln_fwd_call(x, y, g, b)


ln_res.defvjp(lambda x, y, g, b: (_ln_fwd_call(x, y, g, b), (x, y, g)),
              lambda res, do: _ln_bwd_call(res[0], res[1], res[2], do))


def _rope_call(x, c, s1, s2, hs):
    s, w = x.shape
    tr = _row_block(s)
    nb = w // LANES

    def body(x_ref, c_ref, s1_ref, s2_ref, o_ref):
        cv, s1v, s2v = c_ref[...], s1_ref[...], s2_ref[...]
        for cb in range(nb):
            xb = x_ref[:, cb * LANES:(cb + 1) * LANES]
            o_ref[:, cb * LANES:(cb + 1) * LANES] = (
                xb * cv + pltpu.roll(xb, LANES - hs, 1) * s1v + pltpu.roll(xb, hs, 1) * s2v)

    row = pl.BlockSpec((tr, w), lambda i: (i, 0))
    tab = pl.BlockSpec((tr, LANES), lambda i: (i, 0))
    return pl.pallas_call(
        body, name=f"rope_{w}_{hs}", grid=(s // tr,), in_specs=[row, tab, tab, tab], out_specs=row,
        out_shape=jax.ShapeDtypeStruct((s, w), F32), compiler_params=_params(dimension_semantics=("parallel",)),
    )(x, c, s1, s2)


def _rope_tables(s, layout):
    pos = jnp.arange(s, dtype=F32)[:, None]
    lane = jnp.arange(LANES)
    if layout == "mla":
        dim, hs = MLA_ROPE, MLA_ROPE // 2
        r = lane - MLA_NOPE
        active = (r >= 0) & (r < MLA_ROPE)
    else:
        dim, hs = SWA_DIM, SWA_DIM // 2
        r = lane % SWA_DIM
        active = jnp.ones_like(lane, dtype=bool)
    f = jnp.where(active, r % hs, 0)
    inv = ROPE_THETA ** (-(2.0 * f.astype(F32)) / dim)
    ang = pos * inv[None, :]
    cos, sin = jnp.cos(ang), jnp.sin(ang)
    first = (active & (r < hs))[None, :]
    second = (active & (r >= hs))[None, :]
    c = jnp.where(active[None, :], cos, 1.0)
    s1 = jnp.where(first, -sin, 0.0)
    s2 = jnp.where(second, sin, 0.0)
    return (c, s1, s2), hs


def _gated(o_refs, g_ref):
    gv = g_ref[...]
    o = o_refs[0][...] if len(o_refs) == 1 else jnp.concatenate([r[...] for r in o_refs], axis=1)
    return (o * (gv * jax.nn.sigmoid(gv))).astype(BF16)


def _gated_mm_call(o_parts, gate, wb):
    s, k = gate.shape
    n = wb.shape[1]
    tm = _pick(s, (1024, 512, 256, 128))
    tn = _pick(n, (1024, 512, 256, 128))
    widths = [o.shape[1] for o in o_parts]
    no = len(widths)

    def body(*refs):
        refs[-1][...] = jnp.dot(_gated(refs[:no], refs[no]), refs[no + 1][...], preferred_element_type=F32)

    specs = [pl.BlockSpec((tm, wd), lambda i, j: (i, 0)) for wd in widths]
    return pl.pallas_call(
        body, name=f"gated_mm_{no}", grid=(s // tm, n // tn),
        in_specs=specs + [pl.BlockSpec((tm, k), lambda i, j: (i, 0)), pl.BlockSpec((k, tn), lambda i, j: (0, j))],
        out_specs=pl.BlockSpec((tm, tn), lambda i, j: (i, j)), out_shape=jax.ShapeDtypeStruct((s, n), F32),
        compiler_params=_params(dimension_semantics=("parallel", "parallel")),
    )(*o_parts, gate, wb)


def _gated_mm_tn_call(o_parts, gate, g):
    s, k = gate.shape
    n = g.shape[1]
    ts = _pick(s, (512, 256, 128))
    ns = s // ts
    widths = [o.shape[1] for o in o_parts]
    no = len(widths)

    def body(*refs):
        g_ref, o_ref, acc_ref = refs[no + 1], refs[no + 2], refs[no + 3]
        ss = pl.program_id(0)

        @pl.when(ss == 0)
        def _():
            acc_ref[...] = jnp.zeros_like(acc_ref)

        acc_ref[...] += lax.dot_general(_gated(refs[:no], refs[no]), g_ref[...].astype(BF16),
                                        (((0,), (0,)), ((), ())), preferred_element_type=F32)

        @pl.when(ss == ns - 1)
        def _():
            o_ref[...] = acc_ref[...]

    specs = [pl.BlockSpec((ts, wd), lambda ss: (ss, 0)) for wd in widths]
    return pl.pallas_call(
        body, name=f"gated_mm_tn_{no}", grid=(ns,),
        in_specs=specs + [pl.BlockSpec((ts, k), lambda ss: (ss, 0)), pl.BlockSpec((ts, n), lambda ss: (ss, 0))],
        out_specs=pl.BlockSpec((k, n), lambda ss: (0, 0)), out_shape=jax.ShapeDtypeStruct((k, n), F32),
        scratch_shapes=[pltpu.VMEM((k, n), F32)], compiler_params=_params(dimension_semantics=("arbitrary",)),
    )(*o_parts, gate, g)


def _gate_bwd_call(o_parts, gate, dz):
    s, w = gate.shape
    tr = _row_block(s)
    widths = [o.shape[1] for o in o_parts]
    n = len(widths)

    def body(*refs):
        o_refs, g_ref, dz_ref = refs[:n], refs[n], refs[n + 1]
        do_refs, dg_ref = refs[n + 2:2 * n + 2], refs[2 * n + 2]
        off = 0
        for o_ref, do_ref, wd in zip(o_refs, do_refs, widths):
            gv = g_ref[:, off:off + wd]
            sg = jax.nn.sigmoid(gv)
            dzv = dz_ref[:, off:off + wd]
            do_ref[...] = dzv * (gv * sg)
            dg_ref[:, off:off + wd] = dzv * o_ref[...] * (sg * (1.0 + gv * (1.0 - sg)))
            off += wd

    specs = [pl.BlockSpec((tr, wd), lambda i: (i, 0)) for wd in widths]
    row = pl.BlockSpec((tr, w), lambda i: (i, 0))
    outs = pl.pallas_call(
        body, name=f"gate_bwd_{n}", grid=(s // tr,), in_specs=specs + [row, row], out_specs=specs + [row],
        out_shape=[jax.ShapeDtypeStruct((s, wd), F32) for wd in widths] + [jax.ShapeDtypeStruct((s, w), F32)],
        compiler_params=_params(dimension_semantics=("parallel",)),
    )(*o_parts, gate, dz)
    return tuple(outs[:n]), outs[n]


@jax.custom_vjp
def gated_mm(o_parts, gate, w):
    return _gated_mm_call(o_parts, gate, w.astype(BF16))


def _gated_mm_fwd(o_parts, gate, w):
    wb = w.astype(BF16)
    return _gated_mm_call(o_parts, gate, wb), (o_parts, gate, wb)


def _gated_mm_bwd(res, g):
    o_parts, gate, wb = res
    do_parts, dgate = _gate_bwd_call(o_parts, gate, _mm_nn(g, wb, b_transposed=True))
    return do_parts, dgate, _gated_mm_tn_call(o_parts, gate, g)


gated_mm.defvjp(_gated_mm_fwd, _gated_mm_bwd)


def _loss_call(y, t):
    s, k = y.shape
    tr = _row_block(s)
    nsteps = s // tr

    def body(y_ref, t_ref, l_ref, dy_ref, acc_ref):
        i = pl.program_id(0)

        @pl.when(i == 0)
        def _():
            acc_ref[...] = jnp.zeros_like(acc_ref)

        d = y_ref[...] - t_ref[...]
        dy_ref[...] = d / k
        acc_ref[...] += jnp.sum(d * d, axis=0, keepdims=True)

        @pl.when(i == nsteps - 1)
        def _():
            tot = jnp.sum(acc_ref[...], axis=1, keepdims=True) * (0.5 / k)
            l_ref[...] = jnp.broadcast_to(tot, l_ref.shape)

    row = pl.BlockSpec((tr, k), lambda i: (i, 0))
    return pl.pallas_call(
        body, name="loss", grid=(nsteps,), in_specs=[row, row],
        out_specs=[pl.BlockSpec((1, LANES), lambda i: (0, 0)), row],
        out_shape=[jax.ShapeDtypeStruct((1, LANES), F32), jax.ShapeDtypeStruct((s, k), F32)],
        scratch_shapes=[pltpu.VMEM((1, k), F32)], compiler_params=_params(dimension_semantics=("arbitrary",)),
    )(y, t)


@jax.custom_vjp
def mse_loss(y, t):
    return _loss_call(y, t)[0][0, 0]


def _mse_fwd(y, t):
    l, dy = _loss_call(y, t)
    return l[0, 0], (dy, t)


mse_loss.defvjp(_mse_fwd, lambda res, g: (g * res[0], jnp.zeros_like(res[1])))


def _scan_call(x, b, mode):
    s, w = x.shape
    nt = s // 8

    def tile_scan(t):
        row = lax.broadcasted_iota(jnp.int32, (8, w), 0)
        for sh in (1, 2, 4):
            t = t + jnp.where(row >= sh, pltpu.roll(t, sh, 0), 0.0)
        return t

    def body(x_ref, b_ref, o_ref):
        def step(i, carry):
            rows = pl.ds(pl.multiple_of(i * 8, 8), 8)
            t = x_ref[rows, :]
            if mode == "fwd":
                t = jax.nn.log_sigmoid(t + b_ref[...])
            t = tile_scan(t) + carry
            o_ref[rows, :] = t
            return t[7:8, :]

        total = lax.fori_loop(0, nt, step, jnp.zeros((1, w), F32))
        if mode == "rev":
            def fix(i, c):
                rows = pl.ds(pl.multiple_of(i * 8, 8), 8)
                o_ref[rows, :] = total - o_ref[rows, :] + x_ref[rows, :]
                return c
            lax.fori_loop(0, nt, fix, 0)

    full = pl.BlockSpec((s, w), lambda: (0, 0))
    return pl.pallas_call(
        body, name=f"scan_{mode}", in_specs=[full, pl.BlockSpec((1, w), lambda: (0, 0))], out_specs=full,
        out_shape=jax.ShapeDtypeStruct((s, w), F32), compiler_params=_params(),
    )(x, b)


def _fox_dlogit_call(x, b, dlogf):
    s, w = x.shape
    tr = _row_block(s)

    def body(x_ref, b_ref, d_ref, dx_ref, db_ref):
        @pl.when(pl.program_id(0) == 0)
        def _():
            db_ref[...] = jnp.zeros_like(db_ref)

        dx = d_ref[...] * jax.nn.sigmoid(-(x_ref[...] + b_ref[...]))
        dx_ref[...] = dx
        db_ref[...] += jnp.sum(dx, axis=0, keepdims=True)

    row = pl.BlockSpec((tr, w), lambda i: (i, 0))
    vec = pl.BlockSpec((1, w), lambda i: (0, 0))
    return pl.pallas_call(
        body, name="fox_dlogit", grid=(s // tr,), in_specs=[row, vec, row], out_specs=[row, vec],
        out_shape=[jax.ShapeDtypeStruct((s, w), F32), jax.ShapeDtypeStruct((1, w), F32)],
        compiler_params=_params(dimension_semantics=("arbitrary",)),
    )(x, b, dlogf)


@jax.custom_vjp
def fox_cum(fl, b):
    return _scan_call(fl, b, "fwd")


def _fox_cum_bwd(res, dcum):
    fl, b = res
    dlogf = _scan_call(dcum, b, "rev")
    return _fox_dlogit_call(fl, b, dlogf)


fox_cum.defvjp(lambda fl, b: (_scan_call(fl, b, "fwd"), (fl, b)), _fox_cum_bwd)


def _lane_col(x, lane_idx):
    lane = lax.broadcasted_iota(jnp.int32, (1, x.shape[1]), 1)
    return jnp.sum(jnp.where(lane == lane_idx, x, 0.0), axis=1, keepdims=True)


def _row_of(x, row_idx):
    row = lax.broadcasted_iota(jnp.int32, (x.shape[0], 1), 0)
    return jnp.sum(jnp.where(row == row_idx, x, 0.0), axis=0, keepdims=True)


def _attn_cfg(mode, s):
    if mode == "swa":
        blk = 256 if s >= 2048 else 128
        return dict(blk=blk, n_outer=SWA_KV_HEADS, pps=4, wide=False, scale=SWA_DIM ** -0.5)
    blk = 512 if s >= 2048 else 128
    if mode == "mla":
        return dict(blk=blk, n_outer=4, pps=1, wide=True, scale=(MLA_NOPE + MLA_ROPE) ** -0.5)
    return dict(blk=blk, n_outer=4, pps=1, wide=False, scale=FOX_DIM ** -0.5)


ROW_CHUNK = 32


def _unrolled(n, body, carry):
    for c in range(n):
        carry = body(c, carry)
    return carry


def _valid_rows(mode, i, jb, blk, r0, rc):
    qpos = i * blk + r0 + lax.broadcasted_iota(jnp.int32, (rc, blk), 0)
    kpos = jb * blk + lax.broadcasted_iota(jnp.int32, (rc, blk), 1)
    ok = kpos <= qpos
    if mode == "swa":
        ok = ok & (qpos - kpos < WINDOW)
    return ok


def _attn_fwd_call(mode, q, k, v, extra):
    s = q.shape[0]
    cfg = _attn_cfg(mode, s)
    blk, n_outer, pps, wide, scale = cfg["blk"], cfg["n_outer"], cfg["pps"], cfg["wide"], cfg["scale"]
    rc = ROW_CHUNK
    nq = s // blk
    swa, fox = mode == "swa", mode == "fox"
    qw = (2 * LANES if wide else LANES) * pps
    kw = 2 * LANES if wide else LANES
    ow = LANES * pps
    reps = blk // LANES

    def body(*refs):
        if not swa:
            it_ref, jt_ref = refs[:2]
            refs = refs[2:]
        q_ref, k_ref, v_ref = refs[:3]
        n_in = 3
        if fox:
            cum_ref, cumt_ref = refs[3:5]
            n_in = 5
        if swa:
            sink_ref = refs[3]
            n_in = 4
        o_ref, lse_ref, m_ref, l_ref, acc_ref, a_ref, s_all, p_all, c_all = refs[n_in:]
        p_id = pl.program_id(0)
        if swa:
            i, j = pl.program_id(1), pl.program_id(2)
            jb, run, first, last = i - 1 + j, (i - 1 + j) >= 0, j == 0, j == 1
        else:
            i, j = it_ref[pl.program_id(1)], jt_ref[pl.program_id(1)]
            jb, first, last = j, j == 0, j == i
        lane = lax.broadcasted_iota(jnp.int32, (1, LANES), 1)
        msk = [lane < HALF, lane >= HALF]

        @pl.when(first)
        def _():
            for hh in range(2 * pps):
                if swa:
                    m_ref[hh] = jnp.broadcast_to(sink_ref[hh:hh + 1, :], (blk, LANES))
                    l_ref[hh] = jnp.ones((blk, LANES), F32)
                else:
                    m_ref[hh] = jnp.full((blk, LANES), NEG, F32)
                    l_ref[hh] = jnp.zeros((blk, LANES), F32)
            acc_ref[...] = jnp.zeros_like(acc_ref)

        def process(masked):
            for pp in range(pps):
                vb = v_ref[...]
                pvs = []
                for h in range(2):
                    hh = 2 * pp + h
                    s_ref, p_ref, c_ref = s_all.at[hh], p_all.at[hh], c_all.at[hh]
                    if wide:
                        qh = q_ref[:, h * LANES:(h + 1) * LANES] * scale
                        kh = k_ref[:, h * LANES:(h + 1) * LANES]
                    else:
                        qh = jnp.where(msk[h], q_ref[:, pp * LANES:(pp + 1) * LANES], 0.0) * scale
                        kh = k_ref[...]
                    s_ref[...] = lax.dot_general(qh.astype(BF16), kh.astype(BF16), (((1,), (1,)), ((), ())),
                                                 preferred_element_type=F32)
                    if fox:
                        head = 2 * p_id + h
                        c_ref[...] = jnp.broadcast_to(_lane_col(cum_ref[...], head), (blk, LANES))
                        ck = _row_of(cumt_ref[...], head)

                    def chunk(c, carry, hh=hh, h=h):
                        r0 = c * rc
                        rows = pl.ds(r0, rc)
                        u = s_ref[rows, :]
                        if fox:
                            u = u - ck
                        if masked:
                            u = jnp.where(_valid_rows(mode, i, jb, blk, r0, rc), u, NEG)
                        m_prev, l_prev = m_ref[hh, rows, :], l_ref[hh, rows, :]
                        m_cur = jnp.max(u, axis=1, keepdims=True)
                        if fox:
                            m_cur = m_cur + c_ref[rows, :]
                        m_next = jnp.maximum(m_prev, m_cur)
                        shift = m_next - c_ref[rows, :] if fox else m_next
                        p = jnp.exp(u - jnp.tile(shift, (1, reps)))
                        alpha = jnp.exp(m_prev - m_next)
                        l_ref[hh, rows, :] = alpha * l_prev + jnp.sum(p, axis=1, keepdims=True)
                        m_ref[hh, rows, :] = m_next
                        a_ref[hh, rows, :] = alpha
                        p_ref[rows, :] = p.astype(BF16)
                        return carry

                    _unrolled(blk // rc, chunk, 0)
                    vh = jnp.where(msk[h], vb, 0.0).astype(BF16)
                    pvs.append(jnp.dot(p_ref[...], vh, preferred_element_type=F32))
                acc_ref[pp] = acc_ref[pp] * jnp.where(msk[0], a_ref[2 * pp], a_ref[2 * pp + 1]) + pvs[0] + pvs[1]

        if swa:
            pl.when(run)(lambda: process(True))
        else:
            pl.when(j < i)(lambda: process(False))
            pl.when(j == i)(lambda: process(True))

        @pl.when(last)
        def _():
            for pp in range(pps):
                l0, l1 = l_ref[2 * pp], l_ref[2 * pp + 1]
                o_ref[:, pp * LANES:(pp + 1) * LANES] = acc_ref[pp] / jnp.where(msk[0], l0, l1)
                lse_ref[:, pp * LANES:(pp + 1) * LANES] = jnp.where(
                    msk[0], m_ref[2 * pp] + jnp.log(l0), m_ref[2 * pp + 1] + jnp.log(l1))

    if swa:
        kv_map = lambda g, i, j: (jnp.maximum(i - 1 + j, 0), g)
        q_map = lambda g, i, j: (i, g)
        grid, tables, sem = (n_outer, nq, 2), [], ("parallel", "parallel", "arbitrary")
    else:
        tri = [(i, j) for i in range(nq) for j in range(i + 1)]
        tables = [jnp.asarray([t[0] for t in tri], jnp.int32), jnp.asarray([t[1] for t in tri], jnp.int32)]
        kv_map = lambda p, t, it, jt: (jt[t], p)
        q_map = lambda p, t, it, jt: (it[t], p)
        grid, sem = (n_outer, len(tri)), ("parallel", "arbitrary")
    in_specs = [pl.BlockSpec((blk, qw), q_map), pl.BlockSpec((blk, kw), kv_map), pl.BlockSpec((blk, LANES), kv_map)]
    args = [q, k, v]
    if fox:
        cum, cumt = extra
        in_specs += [pl.BlockSpec((blk, LANES), lambda p, t, it, jt: (it[t], 0)),
                     pl.BlockSpec((8, blk), lambda p, t, it, jt: (0, jt[t]))]
        args += [cum, cumt]
    if swa:
        in_specs += [pl.BlockSpec((8, LANES), lambda g, i, j: (g, 0))]
        args += [extra]
    n_pairs = n_outer * pps
    return pl.pallas_call(
        body, name=f"attn_fwd_{mode}",
        grid_spec=pltpu.PrefetchScalarGridSpec(
            num_scalar_prefetch=len(tables), grid=grid, in_specs=in_specs,
            out_specs=[pl.BlockSpec((blk, ow), q_map), pl.BlockSpec((blk, ow), q_map)],
            scratch_shapes=[pltpu.VMEM((2 * pps, blk, LANES), F32), pltpu.VMEM((2 * pps, blk, LANES), F32),
                            pltpu.VMEM((pps, blk, LANES), F32), pltpu.VMEM((2 * pps, blk, LANES), F32),
                            pltpu.VMEM((2 * pps, blk, blk), F32), pltpu.VMEM((2 * pps, blk, blk), BF16),
                            pltpu.VMEM((2 * pps, blk, LANES), F32)]),
        out_shape=[jax.ShapeDtypeStruct((s, n_pairs * LANES), F32), jax.ShapeDtypeStruct((s, n_pairs * LANES), F32)],
        compiler_params=_params(dimension_semantics=sem),
    )(*tables, *args)


def _attn_bwd_call(mode, q, k, v, extra, lse, o, do):
    s = q.shape[0]
    cfg = _attn_cfg(mode, s)
    blk, n_outer, pps, wide, scale = cfg["blk"], cfg["n_outer"], cfg["pps"], cfg["wide"], cfg["scale"]
    rc = ROW_CHUNK
    nq = s // blk
    swa, fox = mode == "swa", mode == "fox"
    qw = (2 * LANES if wide else LANES) * pps
    kw = 2 * LANES if wide else LANES
    ow = LANES * pps
    reps = blk // LANES

    assert not swa

    def body(*refs):
        jt_ref, it_ref = refs[:2]
        refs = refs[2:]
        q_ref, k_ref, v_ref, lse_ref, o_ref, do_ref = refs[:6]
        n_in = 6
        if fox:
            cum_ref, cumt_ref = refs[6:8]
            n_in = 8
        dq_ref, dk_ref, dv_ref = refs[n_in:n_in + 3]
        n_out = n_in + 3
        if fox:
            dck_ref, dcq_ref = refs[n_out:n_out + 2]
            n_out += 2
        dk_acc, dv_acc, s_all, dp_all, p_all, ds_all, e_all, d_all = refs[n_out:n_out + 8]
        if fox:
            dck_acc, rs_all = refs[n_out + 8:n_out + 10]
        p_id, t = pl.program_id(0), pl.program_id(1)
        j, ii = jt_ref[t], it_ref[t]
        i, first_i, last_i = ii, ii == j, ii == nq - 1
        lane = lax.broadcasted_iota(jnp.int32, (1, LANES), 1)
        msk = [lane < HALF, lane >= HALF]

        @pl.when(t == 0)
        def _():
            dq_ref[...] = jnp.zeros_like(dq_ref)
            if fox:
                dcq_ref[...] = jnp.zeros_like(dcq_ref)

        @pl.when(first_i)
        def _():
            dk_acc[...] = jnp.zeros_like(dk_acc)
            dv_acc[...] = jnp.zeros_like(dv_acc)
            if fox:
                dck_acc[...] = jnp.zeros_like(dck_acc)

        def process(masked):
            rows = pl.ds(pl.multiple_of(i * blk, blk), blk)
            vb = v_ref[...].astype(BF16)
            dv_parts, dk_parts = [], []
            for pp in range(pps):
                psl = slice(pp * LANES, (pp + 1) * LANES)
                lse_blk, do_blk = lse_ref[:, psl], do_ref[:, psl]
                doo = do_blk * o_ref[:, psl]
                dq_pair = []
                for h in range(2):
                    hh = 2 * pp + h
                    s_ref, dp_ref, p_ref, ds_ref = s_all.at[hh], dp_all.at[hh], p_all.at[hh], ds_all.at[hh]
                    e_ref, d_ref = e_all.at[hh], d_all.at[hh]
                    if fox:
                        rs_ref = rs_all.at[hh]
                    if wide:
                        hsl = slice(h * LANES, (h + 1) * LANES)
                        qh = (q_ref[:, hsl] * scale).astype(BF16)
                        kh = k_ref[:, hsl].astype(BF16)
                    else:
                        qh = (jnp.where(msk[h], q_ref[:, psl], 0.0) * scale).astype(BF16)
                        kh = k_ref[...].astype(BF16)
                    s_ref[...] = lax.dot_general(qh, kh, (((1,), (1,)), ((), ())), preferred_element_type=F32)
                    do_h = jnp.where(msk[h], do_blk, 0.0).astype(BF16)
                    dp_ref[...] = lax.dot_general(do_h, vb, (((1,), (1,)), ((), ())), preferred_element_type=F32)
                    lse_h = _lane_col(lse_blk, HALF * h)
                    d_h = jnp.sum(jnp.where(msk[h], doo, 0.0), axis=1, keepdims=True)
                    e_ref[...] = jnp.broadcast_to(lse_h, (blk, LANES))
                    d_ref[...] = jnp.broadcast_to(d_h, (blk, LANES))
                    if fox:
                        head = 2 * p_id + h
                        e_ref[...] = e_ref[...] - jnp.broadcast_to(_lane_col(cum_ref[...], head), (blk, LANES))
                        ck = _row_of(cumt_ref[...], head)

                    def chunk(c, colsum):
                        r0 = c * rc
                        cr = pl.ds(r0, rc)
                        u = s_ref[cr, :]
                        if fox:
                            u = u - ck
                        p = jnp.exp(u - jnp.tile(e_ref[cr, :], (1, reps)))
                        if masked:
                            p = jnp.where(_valid_rows(mode, i, j, blk, r0, rc), p, 0.0)
                        ds = p * (dp_ref[cr, :] - jnp.tile(d_ref[cr, :], (1, reps)))
                        p_ref[cr, :] = p.astype(BF16)
                        ds_ref[cr, :] = ds.astype(BF16)
                        if fox:
                            colsum = colsum + jnp.sum(ds, axis=0, keepdims=True)
                            rs_ref[cr, :] = jnp.broadcast_to(jnp.sum(ds, axis=1, keepdims=True), (rc, LANES))
                        return colsum

                    colsum = _unrolled(blk // rc, chunk, jnp.zeros((1, blk), F32))
                    dv_parts.append(lax.dot_general(p_ref[...], do_h, (((0,), (0,)), ((), ())),
                                                    preferred_element_type=F32))
                    if fox:
                        dck_acc[h:h + 1, :] += -colsum
                        dcq_ref[rows, :] += jnp.where(msk[h], rs_ref[...], 0.0)
                    dq_h = jnp.dot(ds_ref[...], kh, preferred_element_type=F32) * scale
                    dk_h = lax.dot_general(ds_ref[...], qh, (((0,), (0,)), ((), ())), preferred_element_type=F32)
                    if wide:
                        dq_ref[rows, hsl] += dq_h
                        dk_acc[:, hsl] += dk_h
                    else:
                        dq_pair.append(jnp.where(msk[h], dq_h, 0.0))
                        dk_parts.append(dk_h)
                if not wide:
                    dq_ref[rows, psl] += dq_pair[0] + dq_pair[1]
            dv_acc[...] += functools.reduce(lambda a, b: a + b, dv_parts)
            if not wide:
                dk_acc[...] += functools.reduce(lambda a, b: a + b, dk_parts)

        pl.when(ii > j)(lambda: process(False))
        pl.when(ii == j)(lambda: process(True))

        @pl.when(last_i)
        def _():
            dk_ref[...] = dk_acc[...]
            dv_ref[...] = dv_acc[...]
            if fox:
                dck_ref[0] = dck_acc[...]

    tri = [(j, i) for j in range(nq) for i in range(j, nq)]
    tables = [jnp.asarray([t[0] for t in tri], jnp.int32), jnp.asarray([t[1] for t in tri], jnp.int32)]
    q_map = lambda p, t, jt, it: (it[t], p)
    kv_map = lambda p, t, jt, it: (jt[t], p)
    in_specs = [pl.BlockSpec((blk, qw), q_map), pl.BlockSpec((blk, kw), kv_map), pl.BlockSpec((blk, LANES), kv_map),
                pl.BlockSpec((blk, ow), q_map), pl.BlockSpec((blk, ow), q_map), pl.BlockSpec((blk, ow), q_map)]
    args = [q, k, v, lse, o, do]
    n_pairs = n_outer * pps
    out_specs = [pl.BlockSpec((s, qw), lambda p, t, jt, it: (0, p)), pl.BlockSpec((blk, kw), kv_map),
                 pl.BlockSpec((blk, LANES), kv_map)]
    out_shape = [jax.ShapeDtypeStruct((s, q.shape[1]), F32), jax.ShapeDtypeStruct((s, k.shape[1]), F32),
                 jax.ShapeDtypeStruct((s, v.shape[1]), F32)]
    nh = 2 * pps
    scratch = [pltpu.VMEM((blk, kw), F32), pltpu.VMEM((blk, LANES), F32), pltpu.VMEM((nh, blk, blk), F32),
               pltpu.VMEM((nh, blk, blk), F32), pltpu.VMEM((nh, blk, blk), BF16), pltpu.VMEM((nh, blk, blk), BF16),
               pltpu.VMEM((nh, blk, LANES), F32), pltpu.VMEM((nh, blk, LANES), F32)]
    if fox:
        cum, cumt = extra
        in_specs += [pl.BlockSpec((blk, LANES), lambda p, t, jt, it: (it[t], 0)),
                     pl.BlockSpec((8, blk), lambda p, t, jt, it: (0, jt[t]))]
        args += [cum, cumt]
        out_specs += [pl.BlockSpec((1, 8, blk), lambda p, t, jt, it: (p, 0, jt[t])),
                      pl.BlockSpec((s, LANES), lambda p, t, jt, it: (0, p))]
        out_shape += [jax.ShapeDtypeStruct((n_pairs, 8, s), F32), jax.ShapeDtypeStruct((s, n_pairs * LANES), F32)]
        scratch += [pltpu.VMEM((8, blk), F32), pltpu.VMEM((nh, blk, LANES), F32)]
    return pl.pallas_call(
        body, name=f"attn_bwd_{mode}",
        grid_spec=pltpu.PrefetchScalarGridSpec(num_scalar_prefetch=2, grid=(n_outer, len(tri)), in_specs=in_specs,
                                               out_specs=out_specs, scratch_shapes=scratch),
        out_shape=out_shape, compiler_params=_params(dimension_semantics=("parallel", "arbitrary")),
    )(*tables, *args)


def _swa_masks(i, blk):
    r = lax.broadcasted_iota(jnp.int32, (blk, blk), 0)
    c = lax.broadcasted_iota(jnp.int32, (blk, blk), 1)
    return (c > r) & (i > 0), c <= r


def _nt(a, b):
    return lax.dot_general(a, b, (((1,), (1,)), ((), ())), preferred_element_type=F32)


def _tn(a, b):
    return lax.dot_general(a, b, (((0,), (0,)), ((), ())), preferred_element_type=F32)


def _swa_bwd_call(q, k, v, sink, lse, o, do):
    s = q.shape[0]
    blk, pps, scale = WINDOW, 4, SWA_DIM ** -0.5
    nq = s // blk

    def body(q_ref, kp_ref, ko_ref, vp_ref, vo_ref, sink_ref, lse_ref, o_ref, do_ref,
             dq_ref, dk_ref, dv_ref, dsink_ref, ck_ref, cv_ref):
        i = pl.program_id(1)
        lane = lax.broadcasted_iota(jnp.int32, (1, LANES), 1)
        msk = [lane < HALF, lane >= HALF]

        @pl.when(i == 0)
        def _():
            ck_ref[...] = jnp.zeros_like(ck_ref)
            cv_ref[...] = jnp.zeros_like(cv_ref)
            dsink_ref[...] = jnp.zeros_like(dsink_ref)

        @pl.when(i < nq)
        def _():
            ok_prev, ok_own = _swa_masks(i, blk)
            kp, ko = kp_ref[...].astype(BF16), ko_ref[...].astype(BF16)
            vp, vo = vp_ref[...].astype(BF16), vo_ref[...].astype(BF16)
            dkp, dko, dvp, dvo = [], [], [], []
            for pp in range(pps):
                psl = slice(pp * LANES, (pp + 1) * LANES)
                qp, do_blk = q_ref[:, psl], do_ref[:, psl]
                doo = do_blk * o_ref[:, psl]
                dqs = []
                for h in range(2):
                    hh = 2 * pp + h
                    qh = (jnp.where(msk[h], qp, 0.0) * scale).astype(BF16)
                    lse_h = jnp.broadcast_to(_lane_col(lse_ref[:, psl], HALF * h), (blk, LANES))
                    d_h = jnp.broadcast_to(jnp.sum(jnp.where(msk[h], doo, 0.0), axis=1, keepdims=True), (blk, LANES))
                    p_p = jnp.where(ok_prev, jnp.exp(_nt(qh, kp) - lse_h), 0.0)
                    p_o = jnp.where(ok_own, jnp.exp(_nt(qh, ko) - lse_h), 0.0)
                    do_h = jnp.where(msk[h], do_blk, 0.0).astype(BF16)
                    ds_p = (p_p * (_nt(do_h, vp) - d_h)).astype(BF16)
                    ds_o = (p_o * (_nt(do_h, vo) - d_h)).astype(BF16)
                    dq_h = (jnp.dot(ds_p, kp, preferred_element_type=F32)
                            + jnp.dot(ds_o, ko, preferred_element_type=F32)) * scale
                    dqs.append(jnp.where(msk[h], dq_h, 0.0))
                    dkp.append(_tn(ds_p, qh))
                    dko.append(_tn(ds_o, qh))
                    dvp.append(_tn(p_p.astype(BF16), do_h))
                    dvo.append(_tn(p_o.astype(BF16), do_h))
                    sink_row = sink_ref[hh:hh + 1, :]
                    dsink_ref[hh:hh + 1, :] += -jnp.sum(jnp.exp(sink_row - lse_h) * d_h, axis=0, keepdims=True)
                dq_ref[:, psl] = dqs[0] + dqs[1]
            total = lambda parts: functools.reduce(lambda a, b: a + b, parts)
            dk_ref[...] = ck_ref[...] + total(dkp)
            dv_ref[...] = cv_ref[...] + total(dvp)
            ck_ref[...] = total(dko)
            cv_ref[...] = total(dvo)

        @pl.when(i == nq)
        def _():
            dk_ref[...] = ck_ref[...]
            dv_ref[...] = cv_ref[...]

    last = nq - 1
    prev = lambda g, i: (jnp.maximum(i - 1, 0), g)
    own = lambda g, i: (jnp.minimum(i, last), g)
    qspec = pl.BlockSpec((blk, pps * LANES), own)
    kspec = lambda m: pl.BlockSpec((blk, LANES), m)
    sspec = pl.BlockSpec((8, LANES), lambda g, i: (g, 0))
    return pl.pallas_call(
        body, name="swa_bwd", grid=(SWA_KV_HEADS, nq + 1),
        in_specs=[qspec, kspec(prev), kspec(own), kspec(prev), kspec(own), sspec, qspec, qspec, qspec],
        out_specs=[qspec, kspec(prev), kspec(prev), sspec],
        out_shape=[jax.ShapeDtypeStruct(q.shape, F32), jax.ShapeDtypeStruct(k.shape, F32),
                   jax.ShapeDtypeStruct(v.shape, F32), jax.ShapeDtypeStruct((SWA_HEADS, LANES), F32)],
        scratch_shapes=[pltpu.VMEM((blk, LANES), F32), pltpu.VMEM((blk, LANES), F32)],
        compiler_params=_params(dimension_semantics=("parallel", "arbitrary")),
    )(q, k, k, v, v, sink, lse, o, do)


def _make_attn(mode):
    swa = mode == "swa"

    @jax.custom_vjp
    def attn(q, k, v, extra):
        return fwd(q, k, v, extra)[0]

    def fwd(q, k, v, extra):
        o, lse = _attn_fwd_call(mode, q, k, v, extra)
        return o, (q, k, v, extra, o, lse)

    def bwd(res, do):
        q, k, v, extra, o, lse = res
        outs = (_swa_bwd_call(q, k, v, extra, lse, o, do) if swa
                else _attn_bwd_call(mode, q, k, v, extra, lse, o, do))
        dq, dk, dv = outs[:3]
        if mode == "fox":
            cum, cumt = extra
            dck = outs[3]
            dcumt = dck[:, :2, :].reshape(FOX_HEADS, -1)
            dcq = outs[4].reshape(-1, FOX_HEADS, HALF)[:, :, 0]
            dextra = (jnp.pad(dcq, ((0, 0), (0, LANES - FOX_HEADS))), dcumt)
        elif mode == "swa":
            dextra = jnp.where(jnp.arange(LANES)[None, :] == 0, outs[3], 0.0)
        else:
            dextra = None
        return dq, dk, dv, dextra

    attn.defvjp(fwd, bwd)
    return attn


attn_mla = _make_attn("mla")
attn_fox = _make_attn("fox")
attn_swa = _make_attn("swa")


def _ukv_layout(w):
    r = w.shape[0]
    w3 = w.reshape(r, MLA_HEADS, MLA_NOPE + MLA_V)
    wk = jnp.pad(w3[:, :, :MLA_NOPE], ((0, 0), (0, 0), (0, LANES - MLA_NOPE))).reshape(r, MLA_HEADS * LANES)
    wv = w3[:, :, MLA_NOPE:].reshape(r, MLA_HEADS * MLA_V)
    return wk, wv


def _even_layer(x, w_in_cat, q_norm, w_uq_p, kv_norm, w_ukv, b_f, w_out, ln_g, ln_b, tabs_mla):
    tabs, hs = tabs_mla
    cq, ckv, kpe, fq, fk, fv, fl, gate = even_in_proj(x, relayout(w_in_cat, "even"), tabs)
    q = mm_rope(rms_norm(cq, q_norm), w_uq_p, tabs, hs)
    ckvn = rms_norm(ckv, kv_norm)
    wk, wv = _ukv_layout(w_ukv)
    kk = mm(ckvn, wk) + jnp.tile(kpe, (1, MLA_HEADS))
    o_mla = attn_mla(q, kk, mm(ckvn, wv), None)
    cum = fox_cum(fl, jnp.pad(b_f, (0, LANES - FOX_HEADS)).reshape(1, LANES))
    o_fox = attn_fox(fq, fk, fv, (cum, cum[:, :8].T))
    y = gated_mm((o_mla, o_fox), gate, w_out)
    return ln_res(x, y, ln_g, ln_b)


def _odd_layer(x, w_in_cat, sinks, w_out, ln_g, ln_b, tabs_swa):
    q, kd, vd, gate = odd_in_proj(x, relayout(w_in_cat, "odd"), tabs_swa[0])
    o = attn_swa(q, kd, vd, jnp.broadcast_to(sinks[:, None], (SWA_HEADS, LANES)))
    y = gated_mm((o,), gate, w_out)
    return ln_res(x, y, ln_g, ln_b)


EVEN_SHARDED = ["even_w_in", "even_w_uq", "even_w_ukv", "even_w_out"]
ODD_SHARDED = ["odd_w_in", "odd_w_out", "odd_ln_g", "odd_ln_b"]
EVEN_REPL = ["even_q_norm", "even_kv_norm", "even_b_f", "even_ln_g", "even_ln_b"]
ODD_REPL = ["odd_sinks"]


def _layer_names(layer):
    return (EVEN_SHARDED, EVEN_REPL) if layer % 2 == 0 else (ODD_SHARDED, ODD_REPL)


def _layer_of(name, j):
    return 2 * j if name.startswith("even") else 2 * j + 1


def _layer_apply(layer, p, x, tabs):
    if layer % 2 == 0:
        return _even_layer(x, p["even_w_in"], p["even_q_norm"], p["even_w_uq"], p["even_kv_norm"], p["even_w_ukv"],
                           p["even_b_f"], p["even_w_out"], p["even_ln_g"], p["even_ln_b"], tabs["mla"])
    return _odd_layer(x, p["odd_w_in"], p["odd_sinks"], p["odd_w_out"], p["odd_ln_g"], p["odd_ln_b"], tabs["swa"])


def _pad_rows(flat, mult):
    n = flat.shape[-1]
    per = mult * LANES
    padded = -(-n // per) * per
    if padded != n:
        flat = jnp.pad(flat, [(0, 0)] * (flat.ndim - 1) + [(0, padded - n)])
    return flat.reshape(flat.shape[:-1] + (padded // LANES, LANES))


def _pad_last(a, width):
    if a.shape[-1] == width:
        return a
    return jnp.pad(a, [(0, 0)] * (a.ndim - 1) + [(0, width - a.shape[-1])])


def _join(slots, axis):
    shp = list(slots.shape[1:])
    shp[axis] *= N_DEV
    return jnp.moveaxis(slots, 0, axis).reshape(shp)


def _split(full, axis):
    shp = full.shape
    t = full.reshape(shp[:axis] + (N_DEV, shp[axis] // N_DEV) + shp[axis + 1:])
    return jnp.moveaxis(t, axis, 0)


PAD_TO = {"even_w_in": SHARD_PAD, "even_w_uq": LANES, "odd_w_in": SHARD_PAD}


def kernel(x, even_w_in, even_q_norm, even_w_uq, even_kv_norm, even_w_ukv, even_b_f, even_w_out, even_ln_g, even_ln_b, odd_w_in, odd_sinks, odd_w_out, odd_ln_g, odd_ln_b, loss_target, m_even_w_in, m_even_q_norm, m_even_w_uq, m_even_kv_norm, m_even_w_ukv, m_even_b_f, m_even_w_out, m_even_ln_g, m_even_ln_b, m_odd_w_in, m_odd_sinks, m_odd_w_out, m_odd_ln_g, m_odd_ln_b, v_even_w_in, v_even_q_norm, v_even_w_uq, v_even_kv_norm, v_even_w_ukv, v_even_b_f, v_even_w_out, v_even_ln_g, v_even_ln_b, v_odd_w_in, v_odd_sinks, v_odd_w_out, v_odd_ln_g, v_odd_ln_b):
    w = dict(even_w_in=even_w_in, even_q_norm=even_q_norm, even_w_uq=even_w_uq, even_kv_norm=even_kv_norm,
             even_w_ukv=even_w_ukv, even_b_f=even_b_f, even_w_out=even_w_out, even_ln_g=even_ln_g, even_ln_b=even_ln_b,
             odd_w_in=odd_w_in, odd_sinks=odd_sinks, odd_w_out=odd_w_out, odd_ln_g=odd_ln_g, odd_ln_b=odd_ln_b)
    mom = dict(even_w_in=m_even_w_in, even_q_norm=m_even_q_norm, even_w_uq=m_even_w_uq, even_kv_norm=m_even_kv_norm,
               even_w_ukv=m_even_w_ukv, even_b_f=m_even_b_f, even_w_out=m_even_w_out, even_ln_g=m_even_ln_g,
               even_ln_b=m_even_ln_b, odd_w_in=m_odd_w_in, odd_sinks=m_odd_sinks, odd_w_out=m_odd_w_out,
               odd_ln_g=m_odd_ln_g, odd_ln_b=m_odd_ln_b)
    vel = dict(even_w_in=v_even_w_in, even_q_norm=v_even_q_norm, even_w_uq=v_even_w_uq, even_kv_norm=v_even_kv_norm,
               even_w_ukv=v_even_w_ukv, even_b_f=v_even_b_f, even_w_out=v_even_w_out, even_ln_g=v_even_ln_g,
               even_ln_b=v_even_ln_b, odd_w_in=v_odd_w_in, odd_sinks=v_odd_sinks, odd_w_out=v_odd_w_out,
               odd_ln_g=v_odd_ln_g, odd_ln_b=v_odd_ln_b)
    sharded = BIG + SMALL_SHARDED
    padded = lambda d, n: _pad_last(d[n], PAD_TO.get(n, d[n].shape[-1]))

    tabs = {"mla": _rope_tables(x.shape[1], "mla"), "swa": _rope_tables(x.shape[1], "swa")}
    keys = lambda layers: [(n, layer // 2) for layer in layers for n in _layer_names(layer)[0]]
    first, rest = keys([0]), keys([1, 2, 3])
    me = _lin(_me())

    def shard(n, j):
        a = padded(w, n)[j]
        return a.astype(BF16) if n in BIG else a

    in_slots = ("even_w_in", "odd_w_in")
    to_full = lambda n, g: g if n in in_slots else _join(g, SHARD_AXIS[n] - 1).astype(F32)
    to_slots = lambda n, g: g if n in in_slots else _split(g, SHARD_AXIS[n] - 1).astype(BF16 if n in BIG else F32)

    def layer_params(layer, full_of):
        shn, rpn = _layer_names(layer)
        p = {n: full_of(n) for n in shn}
        p.update({n: w[n][layer // 2] for n in rpn})
        return p

    got0 = _all_gather([shard(n, j) for n, j in first], "all_gather_first")
    got0, mine_rest = lax.optimization_barrier((got0, [shard(n, j) for n, j in rest]))
    got0 = dict(zip(first, got0))
    send_sems, recv_sems, srcs, lands, token = _split_start(mine_rest, False, "all_gather_rest_start")
    got0, _ = lax.optimization_barrier((got0, token))
    p0 = layer_params(0, lambda n: to_full(n, got0[(n, 0)]))
    x1, vjp0 = jax.vjp(lambda p, xx: _layer_apply(0, p, xx, tabs), p0, x[0])
    got = _split_wait(send_sems, recv_sems, srcs, lands, x1, False, "all_gather_rest_wait")
    got = dict(zip(rest, _own_slot(got, [m[None] for m in mine_rest])))

    xs, vjps = x1, [vjp0]
    for layer in (1, 2, 3):
        p = layer_params(layer, lambda n: to_full(n, got[(n, layer // 2)]))
        xs, vjp = jax.vjp(lambda p_, xx, layer=layer: _layer_apply(layer, p_, xx, tabs), p, xs)
        vjps.append(vjp)
    loss_local, vjp_loss = jax.vjp(lambda y: mse_loss(y, loss_target[0]), xs)
    (dy,) = vjp_loss(jnp.ones((), F32))
    loss = lax.psum(loss_local, AXES)
    grads = {}
    for layer in (3, 2, 1):
        grads[layer], dy = vjps[layer](dy)

    parts_rest = [to_slots(n, grads[_layer_of(n, j)][n]) for n, j in rest]
    send_sems, recv_sems, srcs, lands, token = _split_start(parts_rest, True, "grad_exchange_rest_start")
    grads[0], grad_x = vjps[0](dy + token[0, 0])
    recv_rest = _split_wait(send_sems, recv_sems, srcs, lands, grad_x, True, "grad_exchange_rest_wait")
    recv = dict(zip(rest, _own_slot(recv_rest, [lax.dynamic_slice_in_dim(p, me, 1, axis=0) for p in parts_rest])))
    repl_grad = lambda n: jnp.stack([grads[_layer_of(n, j)][n] for j in (0, 1)])
    repl_rows = _pad_rows(jnp.concatenate([repl_grad(n).reshape(-1) for n in REPL]), 8)
    parts_last = [to_slots(n, grads[0][n]) for n, j in first]
    parts_last.append(jnp.broadcast_to(repl_rows[None], (N_DEV,) + repl_rows.shape))
    recv_last = _exchange(parts_last, "grad_exchange_last")
    recv.update(zip(first, recv_last[:-1]))

    g_out, d_out, m_out, v_out = {}, {}, {}, {}
    for n in sharded:
        r = jnp.stack([recv[(n, 0)], recv[(n, 1)]], axis=1)
        cols = r.shape[-1]
        flat2 = lambda d: padded(d, n).reshape(-1, cols)
        outs = _sum_adamw(r.reshape(N_DEV, -1, cols), flat2(w), flat2(mom), flat2(vel))
        for dst, o in zip((g_out, d_out, m_out, v_out), outs):
            dst[n] = o.reshape(w[n].shape[:-1] + (cols,))[..., :w[n].shape[-1]]
    pack = lambda d: _pad_rows(jnp.concatenate([d[n].reshape(-1) for n in REPL]), 8)
    outs = _sum_adamw(recv_last[-1], pack(w), pack(mom), pack(vel))
    for dst, o in zip((g_out, d_out, m_out, v_out), outs):
        flat, off = o.reshape(-1), 0
        for n in REPL:
            size = math.prod(w[n].shape)
            dst[n] = flat[off:off + size].reshape(w[n].shape)
            off += size
    return (loss, grad_x[None], *[g_out[n] for n in WEIGHTS], *[d_out[n] for n in WEIGHTS],
            *[m_out[n] for n in WEIGHTS], *[v_out[n] for n in WEIGHTS])
```

```python
import functools
import math

import jax
import jax.numpy as jnp
from jax import lax
from jax.experimental import pallas as pl
from jax.experimental.pallas import tpu as pltpu

F32 = jnp.float32
BF16 = jnp.bfloat16
LANES = 128
HALF = 64
N_DEV = 8
AXES = ("x", "y", "c")
VMEM_LIMIT = 48 * 1024 * 1024
VMEM_LIMIT_TALL = 56 * 1024 * 1024

D_MODEL = 1024
DEPTH = 4
ROPE_THETA = 10000.0
MLA_HEADS, MLA_NOPE, MLA_ROPE, MLA_V, MLA_Q_RANK, MLA_KV_RANK = 8, 64, 32, 64, 256, 128
FOX_HEADS, FOX_DIM = 8, 64
SWA_HEADS, SWA_KV_HEADS, SWA_DIM, WINDOW = 16, 2, 64, 128
RMS_EPS, LN_EPS = 1e-6, 1e-5
ALPHA = (2 * DEPTH) ** 0.25
ADAM_LR, ADAM_B1, ADAM_B2, ADAM_EPS, ADAM_WD, ADAM_STEP = 0.001, 0.9, 0.999, 1e-08, 0.01, 10
NEG = -1e30

WEIGHTS = ["even_w_in", "even_q_norm", "even_w_uq", "even_kv_norm", "even_w_ukv", "even_b_f", "even_w_out",
           "even_ln_g", "even_ln_b", "odd_w_in", "odd_sinks", "odd_w_out", "odd_ln_g", "odd_ln_b"]
SHARD_AXIS = {"even_w_in": 2, "even_w_uq": 2, "even_w_ukv": 2, "even_w_out": 1, "odd_w_in": 2, "odd_w_out": 1,
              "odd_ln_g": 1, "odd_ln_b": 1, "even_q_norm": None, "even_kv_norm": None, "even_b_f": None,
              "even_ln_g": None, "even_ln_b": None, "odd_sinks": None}
BIG = ["even_w_in", "even_w_uq", "even_w_ukv", "even_w_out", "odd_w_in", "odd_w_out"]
SMALL_SHARDED = ["odd_ln_g", "odd_ln_b"]
REPL = [n for n in WEIGHTS if SHARD_AXIS[n] is None]


def _pick(n, cands):
    for c in cands:
        if n % c == 0:
            return c
    return n


def _params(**kw):
    return pltpu.CompilerParams(vmem_limit_bytes=VMEM_LIMIT, **kw)


def _me():
    return lax.axis_index("x"), lax.axis_index("y"), lax.axis_index("c")


def _peer(k):
    x, y, c = _me()
    px = 1 - x if (k >> 2) & 1 else x
    py = 1 - y if (k >> 1) & 1 else y
    pc = 1 - c if k & 1 else c
    return px, py, pc


def _lin(p):
    return 4 * p[0] + 2 * p[1] + p[2]


def _comm_call(body, n, out_shape, args, name):
    any_spec = pl.BlockSpec(memory_space=pl.ANY)
    return pl.pallas_call(
        body, name=name, out_shape=out_shape, in_specs=[any_spec] * n, out_specs=[any_spec] * n,
        scratch_shapes=[pltpu.SemaphoreType.DMA((n, N_DEV - 1)), pltpu.SemaphoreType.DMA((n, N_DEV - 1)),
                        pltpu.SemaphoreType.DMA((n,))],
    )(*args)


def _all_gather(xs, name):
    n = len(xs)

    def body(*refs):
        x_refs, out_refs = refs[:n], refs[n:2 * n]
        send_sems, recv_sems, local_sems = refs[2 * n:]
        me = _lin(_me())
        local = [pltpu.make_async_copy(x_refs[a], out_refs[a].at[me], local_sems.at[a]) for a in range(n)]
        for cp in local:
            cp.start()
        sends = []
        for k in range(1, N_DEV):
            for a in range(n):
                cp = pltpu.make_async_remote_copy(
                    src_ref=x_refs[a], dst_ref=out_refs[a].at[me], send_sem=send_sems.at[a, k - 1],
                    recv_sem=recv_sems.at[a, k - 1], device_id=_peer(k), device_id_type=pl.DeviceIdType.MESH)
                cp.start()
                sends.append(cp)
        for k in range(1, N_DEV):
            for a in range(n):
                pltpu.make_async_remote_copy(
                    src_ref=x_refs[a], dst_ref=out_refs[a].at[_lin(_peer(k))], send_sem=send_sems.at[a, k - 1],
                    recv_sem=recv_sems.at[a, k - 1], device_id=_peer(k),
                    device_id_type=pl.DeviceIdType.MESH).wait_recv()
        for cp in sends:
            cp.wait_send()
        for cp in local:
            cp.wait()

    out_shape = [jax.ShapeDtypeStruct((N_DEV,) + x.shape, x.dtype) for x in xs]
    return _comm_call(body, n, out_shape, xs, name)


def _exchange(parts, name):
    n = len(parts)

    def body(*refs):
        p_refs, out_refs = refs[:n], refs[n:2 * n]
        send_sems, recv_sems, local_sems = refs[2 * n:]
        me = _lin(_me())
        local = [pltpu.make_async_copy(p_refs[a].at[me], out_refs[a].at[me], local_sems.at[a]) for a in range(n)]
        for cp in local:
            cp.start()
        sends = []
        for k in range(1, N_DEV):
            peer = _peer(k)
            for a in range(n):
                cp = pltpu.make_async_remote_copy(
                    src_ref=p_refs[a].at[_lin(peer)], dst_ref=out_refs[a].at[me], send_sem=send_sems.at[a, k - 1],
                    recv_sem=recv_sems.at[a, k - 1], device_id=peer, device_id_type=pl.DeviceIdType.MESH)
                cp.start()
                sends.append(cp)
        for k in range(1, N_DEV):
            peer = _peer(k)
            for a in range(n):
                pltpu.make_async_remote_copy(
                    src_ref=p_refs[a].at[_lin(peer)], dst_ref=out_refs[a].at[_lin(peer)],
                    send_sem=send_sems.at[a, k - 1], recv_sem=recv_sems.at[a, k - 1], device_id=peer,
                    device_id_type=pl.DeviceIdType.MESH).wait_recv()
        for cp in sends:
            cp.wait_send()
        for cp in local:
            cp.wait()

    out_shape = [jax.ShapeDtypeStruct(p.shape, p.dtype) for p in parts]
    return _comm_call(body, n, out_shape, parts, name)


_HBM = pl.BlockSpec(memory_space=pltpu.HBM)
_SEM = pl.BlockSpec(memory_space=pltpu.SEMAPHORE)
_EFFECT = pltpu.SideEffectType.DATAFLOW_SIDE_EFFECTING


def _split_start(srcs, slotted, name):
    n = len(srcs)
    lands = [lax.empty(s.shape if slotted else (N_DEV,) + s.shape, s.dtype) for s in srcs]

    def body(*refs):
        src_refs, land_refs = refs[:n], refs[n:2 * n]
        send_sems, recv_sems, token = refs[2 * n], refs[2 * n + 1], refs[-1]
        me = _lin(_me())
        for k in range(1, N_DEV):
            peer = _peer(k)
            for a in range(n):
                pltpu.make_async_remote_copy(
                    src_ref=src_refs[a].at[_lin(peer)] if slotted else src_refs[a], dst_ref=land_refs[a].at[me],
                    send_sem=send_sems.at[a * (N_DEV - 1) + k - 1], recv_sem=recv_sems.at[a * (N_DEV - 1) + k - 1],
                    device_id=peer, device_id_type=pl.DeviceIdType.MESH).start()
        token[...] = jnp.zeros_like(token)

    both = list(srcs) + lands
    outs = pl.pallas_call(
        body, name=name,
        out_shape=(pltpu.SemaphoreType.DMA((n * (N_DEV - 1),)), pltpu.SemaphoreType.DMA((n * (N_DEV - 1),)),
                   *[pltpu.HBM(b.shape, b.dtype) for b in both], jax.ShapeDtypeStruct((8, LANES), F32)),
        in_specs=[_HBM] * (2 * n), out_specs=(_SEM, _SEM, *[_HBM] * (2 * n), pl.BlockSpec(memory_space=pltpu.VMEM)),
        input_output_aliases={a: 2 + a for a in range(2 * n)},
        compiler_params=pltpu.CompilerParams(has_side_effects=_EFFECT),
    )(*[pltpu.with_memory_space_constraint(b, pltpu.HBM) for b in both])
    return outs[0], outs[1], list(outs[2:2 + n]), list(outs[2 + n:2 + 2 * n]), outs[-1]


def _split_wait(send_sems, recv_sems, srcs, lands, after, slotted, name):
    n = len(srcs)

    def body(*refs):
        src_refs, land_refs = refs[:n], refs[n:2 * n]
        send_sems, recv_sems = refs[2 * n], refs[2 * n + 1]
        for k in range(1, N_DEV):
            peer = _peer(k)
            for a in range(n):
                cp = pltpu.make_async_remote_copy(
                    src_ref=src_refs[a].at[_lin(peer)] if slotted else src_refs[a],
                    dst_ref=land_refs[a].at[_lin(peer)], send_sem=send_sems.at[a * (N_DEV - 1) + k - 1],
                    recv_sem=recv_sems.at[a * (N_DEV - 1) + k - 1], device_id=peer,
                    device_id_type=pl.DeviceIdType.MESH)
                cp.wait_send()
                cp.wait_recv()

    both = list(srcs) + list(lands)
    outs = pl.pallas_call(
        body, name=name, out_shape=[pltpu.HBM(b.shape, b.dtype) for b in both],
        in_specs=[_HBM] * (2 * n) + [_SEM, _SEM, pl.BlockSpec(memory_space=pl.ANY)], out_specs=[_HBM] * (2 * n),
        input_output_aliases={a: a for a in range(2 * n)},
        compiler_params=pltpu.CompilerParams(has_side_effects=_EFFECT),
    )(*both, send_sems, recv_sems, after)
    return list(outs[n:])


def _own_slot(lands, own):
    me = _lin(_me())
    slot = lambda l: lax.broadcasted_iota(jnp.int32, (N_DEV,) + (1,) * (l.ndim - 1), 0)
    return [jnp.where(slot(l) == me, o.astype(l.dtype), l) for l, o in zip(lands, own)]


def _sum_adamw(recv, w, m, v):
    _, rows, lanes = recv.shape
    tr = _pick(rows, (256, 128, 64, 32, 16, 8))
    c1 = 1.0 - ADAM_B1 ** ADAM_STEP
    c2 = 1.0 - ADAM_B2 ** ADAM_STEP

    def body(r_ref, w_ref, m_ref, v_ref, g_out, d_out, m_out, v_out):
        g = r_ref[0].astype(F32)
        for s in range(1, N_DEV):
            g = g + r_ref[s].astype(F32)
        mn = ADAM_B1 * m_ref[...] + (1.0 - ADAM_B1) * g
        vn = ADAM_B2 * v_ref[...] + (1.0 - ADAM_B2) * (g * g)
        m_hat = mn / c1
        v_hat = vn / c2
        g_out[...] = g
        d_out[...] = -ADAM_LR * (m_hat / (jnp.sqrt(v_hat) + ADAM_EPS) + ADAM_WD * w_ref[...])
        m_out[...] = mn
        v_out[...] = vn

    blk = pl.BlockSpec((tr, lanes), lambda i: (i, 0))
    shp = jax.ShapeDtypeStruct((rows, lanes), F32)
    return pl.pallas_call(
        body, name=f"sum_adamw_{rows}x{lanes}", grid=(rows // tr,),
        in_specs=[pl.BlockSpec((N_DEV, tr, lanes), lambda i: (0, i, 0)), blk, blk, blk],
        out_specs=[blk, blk, blk, blk], out_shape=[shp, shp, shp, shp],
        compiler_params=_params(dimension_semantics=("parallel",)),
    )(recv, w, m, v)


def _rope_block(xb, cv, s1v, s2v, hs):
    return xb * cv + pltpu.roll(xb, LANES - hs, 1) * s1v + pltpu.roll(xb, hs, 1) * s2v


def _mm_nn(a, b, b_transposed=False, rope=None):
    m, k = a.shape
    n = b.shape[0] if b_transposed else b.shape[1]
    tm = _pick(m, (1024, 512, 256, 128))
    tn = _pick(n, (1024, 640, 512, 256, 128))
    tk = _pick(k, (1024, 640, 512, 256, 128))
    nk = k // tk

    def body(*refs):
        a_ref, b_ref = refs[:2]
        o_ref, acc_ref = refs[-2:]
        kk = pl.program_id(2)

        @pl.when(kk == 0)
        def _():
            acc_ref[...] = jnp.zeros_like(acc_ref)

        dims = (((1,), (1,)), ((), ())) if b_transposed else (((1,), (0,)), ((), ()))
        acc_ref[...] += lax.dot_general(a_ref[...].astype(BF16), b_ref[...].astype(BF16), dims,
                                        preferred_element_type=F32)

        @pl.when(kk == nk - 1)
        def _():
            if rope is None:
                o_ref[...] = acc_ref[...]
            else:
                cv, s1v, s2v = refs[2][...], refs[3][...], refs[4][...]
                for cb in range(tn // LANES):
                    sl = slice(cb * LANES, (cb + 1) * LANES)
                    o_ref[:, sl] = _rope_block(acc_ref[:, sl], cv, s1v, s2v, rope[3])

    b_spec = (pl.BlockSpec((tn, tk), lambda i, j, kk: (j, kk)) if b_transposed
              else pl.BlockSpec((tk, tn), lambda i, j, kk: (kk, j)))
    in_specs, args = [pl.BlockSpec((tm, tk), lambda i, j, kk: (i, kk)), b_spec], [a, b]
    if rope is not None:
        in_specs += [pl.BlockSpec((tm, LANES), lambda i, j, kk: (i, 0))] * 3
        args += list(rope[:3])
    tag = ("t" if b_transposed else "n") + ("" if rope is None else f"_rope{rope[3]}")
    return pl.pallas_call(
        body, name=f"mm_n{tag}_{m}x{k}x{n}", grid=(m // tm, n // tn, nk), in_specs=in_specs,
        out_specs=pl.BlockSpec((tm, tn), lambda i, j, kk: (i, j)),
        out_shape=jax.ShapeDtypeStruct((m, n), F32),
        scratch_shapes=[pltpu.VMEM((tm, tn), F32)],
        compiler_params=_params(dimension_semantics=("parallel", "parallel", "arbitrary")),
    )(*args)


def _mm_tn(a, g):
    s, k = a.shape
    _, n = g.shape
    tm = _pick(k, (1024, 512, 256, 128))
    tn = _pick(n, (1024, 640, 512, 256, 128))
    ts = _pick(s, (512, 256, 128))
    ns = s // ts

    def body(a_ref, g_ref, o_ref, acc_ref):
        ss = pl.program_id(2)

        @pl.when(ss == 0)
        def _():
            acc_ref[...] = jnp.zeros_like(acc_ref)

        acc_ref[...] += lax.dot_general(a_ref[...].astype(BF16), g_ref[...].astype(BF16),
                                        (((0,), (0,)), ((), ())), preferred_element_type=F32)

        @pl.when(ss == ns - 1)
        def _():
            o_ref[...] = acc_ref[...]

    return pl.pallas_call(
        body, name=f"mm_tn_{s}x{k}x{n}", grid=(k // tm, n // tn, ns),
        in_specs=[pl.BlockSpec((ts, tm), lambda i, j, ss: (ss, i)), pl.BlockSpec((ts, tn), lambda i, j, ss: (ss, j))],
        out_specs=pl.BlockSpec((tm, tn), lambda i, j, ss: (i, j)),
        out_shape=jax.ShapeDtypeStruct((k, n), F32),
        scratch_shapes=[pltpu.VMEM((tm, tn), F32)],
        compiler_params=_params(dimension_semantics=("parallel", "parallel", "arbitrary")),
    )(a, g)


@jax.custom_vjp
def mm(a, w):
    return _mm_nn(a, w.astype(BF16))


def _mm_fwd(a, w):
    wb = w.astype(BF16)
    return _mm_nn(a, wb), (a, wb)


def _mm_bwd(res, g):
    a, wb = res
    return _mm_nn(g, wb, b_transposed=True), _mm_tn(a, g)


mm.defvjp(_mm_fwd, _mm_bwd)


def _unrope(g, tabs, hs):
    return _rope_call(g, tabs[0], -tabs[1], -tabs[2], hs)


@functools.partial(jax.custom_vjp, nondiff_argnums=(3,))
def mm_rope(a, w, tabs, hs):
    return _mm_nn(a, w.astype(BF16), rope=(*tabs, hs))


def _mm_rope_fwd(a, w, tabs, hs):
    wb = w.astype(BF16)
    return _mm_nn(a, wb, rope=(*tabs, hs)), (a, wb, tabs)


def _mm_rope_bwd(hs, res, g):
    a, wb, tabs = res
    g = _unrope(g, tabs, hs)
    return _mm_nn(g, wb, b_transposed=True), _mm_tn(a, g), jax.tree.map(jnp.zeros_like, tabs)


mm_rope.defvjp(_mm_rope_fwd, _mm_rope_bwd)


def _proj_dx_call(gs, wb, cuts):
    s, (d, n), ng = gs[0].shape[0], wb.shape, len(gs)
    tm = _pick(s, (512, 256, 128))

    def body(*refs):
        w_ref, o_ref = refs[ng], refs[ng + 1]
        for gi in range(ng):
            part = lax.dot_general(refs[gi][...].astype(BF16), w_ref[:, cuts[gi]:cuts[gi + 1]],
                                   (((1,), (1,)), ((), ())), preferred_element_type=F32)
            if gi == 0:
                o_ref[...] = part
            else:
                o_ref[...] += part

    return pl.pallas_call(
        body, name=f"proj_dx_{ng}", grid=(s // tm,),
        in_specs=[pl.BlockSpec((tm, g.shape[1]), lambda i: (i, 0)) for g in gs] + [pl.BlockSpec((d, n), lambda i: (0, 0))],
        out_specs=pl.BlockSpec((tm, d), lambda i: (i, 0)), out_shape=jax.ShapeDtypeStruct((s, d), F32),
        compiler_params=_params(dimension_semantics=("parallel",)),
    )(*gs, wb)


def _proj_dw_call(xb, gs, cuts):
    (s, d), ng, n = xb.shape, len(gs), cuts[-1]
    ts = _pick(s, (512, 256, 128))
    ns = s // ts

    def body(*refs):
        x_ref, o_ref = refs[0], refs[ng + 1]
        ss = pl.program_id(0)
        xt = x_ref[...].T
        for gi in range(ng):
            cols = slice(cuts[gi], cuts[gi + 1])
            part = jnp.dot(xt, refs[1 + gi][...].astype(BF16), preferred_element_type=F32)

            @pl.when(ss == 0)
            def _():
                o_ref[:, cols] = part

            @pl.when(ss > 0)
            def _():
                o_ref[:, cols] += part

    return pl.pallas_call(
        body, name=f"proj_dw_{ng}", grid=(ns,),
        in_specs=[pl.BlockSpec((ts, d), lambda ss: (ss, 0))] + [pl.BlockSpec((ts, g.shape[1]), lambda ss: (ss, 0)) for g in gs],
        out_specs=pl.BlockSpec((d, n), lambda ss: (0, 0)), out_shape=jax.ShapeDtypeStruct((d, n), F32),
        compiler_params=pltpu.CompilerParams(vmem_limit_bytes=VMEM_LIMIT_TALL, dimension_semantics=("arbitrary",)),
    )(xb, *gs)


def _make_in_proj(widths, roped, hs):
    cuts = [sum(widths[:i]) for i in range(len(widths) + 1)]

    @jax.custom_vjp
    def in_proj(x, w, tabs):
        return fwd(x, w, tabs)[0]

    def fwd(x, w, tabs):
        xb, wb = x.astype(BF16), w.astype(BF16)
        outs = tuple(_mm_nn(xb, wb[:, a:b], rope=(*tabs, hs) if gi in roped else None)
                     for gi, (a, b) in enumerate(zip(cuts[:-1], cuts[1:])))
        return outs, (xb, wb, tabs)

    def bwd(res, gs):
        xb, wb, tabs = res
        gs = [_unrope(gg, tabs, hs) if gi in roped else gg for gi, gg in enumerate(gs)]
        return _proj_dx_call(gs, wb, cuts), _proj_dw_call(xb, gs, cuts), jax.tree.map(jnp.zeros_like, tabs)

    in_proj.defvjp(fwd, bwd)
    return in_proj


EVEN_GROUPS = (256, 128, 128, 512, 512, 512, 128, 1024)
ODD_GROUPS = (1024, 256, 256, 1024)
even_in_proj = _make_in_proj(EVEN_GROUPS, roped=(2,), hs=MLA_ROPE // 2)
odd_in_proj = _make_in_proj(ODD_GROUPS, roped=(0, 1), hs=SWA_DIM // 2)

SHARD_PAD = 384


def _source_columns(kind):
    if kind == "even":
        src = [list(range(0, 384)), [-1] * 64, list(range(384, 416)), [-1] * 32, list(range(416, 1952)),
               list(range(1952, 1960)), [-1] * 120, list(range(1960, 2984))]
        return sum(src, []), 373
    q0, k0, v0, g0 = 0, 1024, 1152, 1280
    dup = lambda base: [base + 64 * g + c for g in range(SWA_KV_HEADS) for _ in range(2) for c in range(64)]
    return list(range(q0, k0)) + dup(k0) + dup(v0) + list(range(g0, 2304)), 288


def _selection(kind, transposed):
    src, shard = _source_columns(kind)
    cat = [s + (SHARD_PAD - shard) * (s // shard) if s >= 0 else -1 for s in src]
    cat_arr = jnp.asarray(cat, jnp.int32)
    if transposed:
        cols = lax.broadcasted_iota(jnp.int32, (len(src), N_DEV * SHARD_PAD), 1)
        return (cols == cat_arr[:, None]).astype(BF16), [(c, r) for c, r in enumerate(cat) if r >= 0]
    rows = lax.broadcasted_iota(jnp.int32, (N_DEV * SHARD_PAD, len(src)), 0)
    return (rows == cat_arr[None, :]).astype(BF16), [(r, c) for c, r in enumerate(cat) if r >= 0]


def _mm_banded(a, b, nonzeros, a_slots=False, out_slots=False):
    k, n = b.shape
    m = a.shape[1] if a_slots else a.shape[0]
    tm = _pick(m, (1024, 512, 256, 128))
    tn = SHARD_PAD if out_slots else _pick(n, (1024, 640, 512, 256, 128))
    tk = SHARD_PAD if a_slots else _pick(k, (1024, 640, 512, 256, 128))
    lo, hi = [k // tk] * (n // tn), [-1] * (n // tn)
    for r, c in nonzeros:
        lo[c // tn], hi[c // tn] = min(lo[c // tn], r // tk), max(hi[c // tn], r // tk)
    first = [l if h >= 0 else 0 for l, h in zip(lo, hi)]
    count = [h - l + 1 if h >= 0 else 0 for l, h in zip(lo, hi)]
    steps = max(count)

    def body(first_ref, count_ref, a_ref, b_ref, o_ref, acc_ref):
        j, kk = pl.program_id(1), pl.program_id(2)

        @pl.when(kk == 0)
        def _():
            acc_ref[...] = jnp.zeros_like(acc_ref)

        @pl.when(kk < count_ref[j])
        def _():
            acc_ref[...] += jnp.dot(a_ref[...].astype(BF16), b_ref[...].astype(BF16), preferred_element_type=F32)

        @pl.when(kk == steps - 1)
        def _():
            o_ref[...] = acc_ref[...].astype(o_ref.dtype)

    kblk = lambda j, kk, f, c: jnp.minimum(f[j] + kk, f[j] + jnp.maximum(c[j], 1) - 1)
    a_spec = (pl.BlockSpec((None, tm, tk), lambda i, j, kk, f, c: (kblk(j, kk, f, c), i, 0)) if a_slots
              else pl.BlockSpec((tm, tk), lambda i, j, kk, f, c: (i, kblk(j, kk, f, c))))
    o_spec = (pl.BlockSpec((None, tm, tn), lambda i, j, kk, f, c: (j, i, 0)) if out_slots
              else pl.BlockSpec((tm, tn), lambda i, j, kk, f, c: (i, j)))
    o_shape = jax.ShapeDtypeStruct((n // tn, m, tn), BF16) if out_slots else jax.ShapeDtypeStruct((m, n), F32)
    return pl.pallas_call(
        body, name=f"mm_banded_{m}x{k}x{n}",
        grid_spec=pltpu.PrefetchScalarGridSpec(
            num_scalar_prefetch=2, grid=(m // tm, n // tn, steps),
            in_specs=[a_spec, pl.BlockSpec((tk, tn), lambda i, j, kk, f, c: (kblk(j, kk, f, c), j))],
            out_specs=o_spec, scratch_shapes=[pltpu.VMEM((tm, tn), F32)]),
        out_shape=o_shape,
        compiler_params=_params(dimension_semantics=("parallel", "parallel", "arbitrary")),
    )(jnp.asarray(first, jnp.int32), jnp.asarray(count, jnp.int32), a, b)


@functools.partial(jax.custom_vjp, nondiff_argnums=(1,))
def relayout(wslots, kind):
    return _mm_banded(wslots, *_selection(kind, False), a_slots=True)


def _relayout_bwd(kind, _, g):
    return (_mm_banded(g, *_selection(kind, True), out_slots=True),)


relayout.defvjp(lambda wslots, kind: (_mm_banded(wslots, *_selection(kind, False), a_slots=True), None),
                _relayout_bwd)


def _row_block(s):
    return _pick(s, (512, 256, 128, 64, 32, 16, 8))


def _rms_fwd_call(x, g):
    s, k = x.shape
    tr = _row_block(s)

    def body(x_ref, g_ref, o_ref):
        xv = x_ref[...]
        r = lax.rsqrt(jnp.mean(xv * xv, axis=-1, keepdims=True) + RMS_EPS)
        o_ref[...] = xv * r * g_ref[...]

    return pl.pallas_call(
        body, name=f"rms_fwd_{k}", grid=(s // tr,),
        in_specs=[pl.BlockSpec((tr, k), lambda i: (i, 0)), pl.BlockSpec((1, k), lambda i: (0, 0))],
        out_specs=pl.BlockSpec((tr, k), lambda i: (i, 0)), out_shape=jax.ShapeDtypeStruct((s, k), F32),
        compiler_params=_params(dimension_semantics=("parallel",)),
    )(x, g.reshape(1, k))


def _rms_bwd_call(x, g, dy):
    s, k = x.shape
    tr = _row_block(s)

    def body(x_ref, g_ref, dy_ref, dx_ref, dg_ref):
        @pl.when(pl.program_id(0) == 0)
        def _():
            dg_ref[...] = jnp.zeros_like(dg_ref)

        xv = x_ref[...]
        r = lax.rsqrt(jnp.mean(xv * xv, axis=-1, keepdims=True) + RMS_EPS)
        xh = xv * r
        dyv = dy_ref[...]
        dg_ref[...] += jnp.sum(dyv * xh, axis=0, keepdims=True)
        dxh = dyv * g_ref[...]
        dx_ref[...] = r * (dxh - xh * jnp.mean(dxh * xh, axis=-1, keepdims=True))

    dx, dg = pl.pallas_call(
        body, name=f"rms_bwd_{k}", grid=(s // tr,),
        in_specs=[pl.BlockSpec((tr, k), lambda i: (i, 0)), pl.BlockSpec((1, k), lambda i: (0, 0)),
                  pl.BlockSpec((tr, k), lambda i: (i, 0))],
        out_specs=[pl.BlockSpec((tr, k), lambda i: (i, 0)), pl.BlockSpec((1, k), lambda i: (0, 0))],
        out_shape=[jax.ShapeDtypeStruct((s, k), F32), jax.ShapeDtypeStruct((1, k), F32)],
        compiler_params=_params(dimension_semantics=("arbitrary",)),
    )(x, g.reshape(1, k), dy)
    return dx, dg.reshape(k)


@jax.custom_vjp
def rms_norm(x, g):
    return _rms_fwd_call(x, g)


rms_norm.defvjp(lambda x, g: (_rms_fwd_call(x, g), (x, g)), lambda res, dy: _rms_bwd_call(res[0], res[1], dy))


def _ln_fwd_call(x, y, g, b):
    s, k = x.shape
    tr = _row_block(s)

    def body(x_ref, y_ref, g_ref, b_ref, o_ref):
        u = ALPHA * x_ref[...] + y_ref[...]
        mu = jnp.mean(u, axis=-1, keepdims=True)
        d = u - mu
        var = jnp.mean(d * d, axis=-1, keepdims=True)
        o_ref[...] = d * lax.rsqrt(var + LN_EPS) * g_ref[...] + b_ref[...]

    row = pl.BlockSpec((tr, k), lambda i: (i, 0))
    vec = pl.BlockSpec((1, k), lambda i: (0, 0))
    return pl.pallas_call(
        body, name="ln_fwd", grid=(s // tr,), in_specs=[row, row, vec, vec], out_specs=row,
        out_shape=jax.ShapeDtypeStruct((s, k), F32), compiler_params=_params(dimension_semantics=("parallel",)),
    )(x, y, g.reshape(1, k), b.reshape(1, k))


def _ln_bwd_call(x, y, g, do):
    s, k = x.shape
    tr = _row_block(s)

    def body(x_ref, y_ref, g_ref, do_ref, dx_ref, dy_ref, dg_ref, db_ref):
        @pl.when(pl.program_id(0) == 0)
        def _():
            dg_ref[...] = jnp.zeros_like(dg_ref)
            db_ref[...] = jnp.zeros_like(db_ref)

        u = ALPHA * x_ref[...] + y_ref[...]
        mu = jnp.mean(u, axis=-1, keepdims=True)
        d = u - mu
        r = lax.rsqrt(jnp.mean(d * d, axis=-1, keepdims=True) + LN_EPS)
        xh = d * r
        dov = do_ref[...]
        dg_ref[...] += jnp.sum(dov * xh, axis=0, keepdims=True)
        db_ref[...] += jnp.sum(dov, axis=0, keepdims=True)
        dxh = dov * g_ref[...]
        du = r * (dxh - jnp.mean(dxh, axis=-1, keepdims=True) - xh * jnp.mean(dxh * xh, axis=-1, keepdims=True))
        dy_ref[...] = du
        dx_ref[...] = ALPHA * du

    row = pl.BlockSpec((tr, k), lambda i: (i, 0))
    vec = pl.BlockSpec((1, k), lambda i: (0, 0))
    dx, dy, dg, db = pl.pallas_call(
        body, name="ln_bwd", grid=(s // tr,), in_specs=[row, row, vec, row], out_specs=[row, row, vec, vec],
        out_shape=[jax.ShapeDtypeStruct((s, k), F32), jax.ShapeDtypeStruct((s, k), F32),
                   jax.ShapeDtypeStruct((1, k), F32), jax.ShapeDtypeStruct((1, k), F32)],
        compiler_params=_params(dimension_semantics=("arbitrary",)),
    )(x, y, g.reshape(1, k), do)
    return dx, dy, dg.reshape(k), db.reshape(k)


@jax.custom_vjp
def ln_res(x, y, g, b):
    return _ln_fwd_call(x, y, g, b)


ln_res.defvjp(lambda x, y, g, b: (_ln_fwd_call(x, y, g, b), (x, y, g)),
              lambda res, do: _ln_bwd_call(res[0], res[1], res[2], do))


def _rope_call(x, c, s1, s2, hs):
    s, w = x.shape
    tr = _row_block(s)
    nb = w // LANES

    def body(x_ref, c_ref, s1_ref, s2_ref, o_ref):
        cv, s1v, s2v = c_ref[...], s1_ref[...], s2_ref[...]
        for cb in range(nb):
            xb = x_ref[:, cb * LANES:(cb + 1) * LANES]
            o_ref[:, cb * LANES:(cb + 1) * LANES] = (
                xb * cv + pltpu.roll(xb, LANES - hs, 1) * s1v + pltpu.roll(xb, hs, 1) * s2v)

    row = pl.BlockSpec((tr, w), lambda i: (i, 0))
    tab = pl.BlockSpec((tr, LANES), lambda i: (i, 0))
    return pl.pallas_call(
        body, name=f"rope_{w}_{hs}", grid=(s // tr,), in_specs=[row, tab, tab, tab], out_specs=row,
        out_shape=jax.ShapeDtypeStruct((s, w), F32), compiler_params=_params(dimension_semantics=("parallel",)),
    )(x, c, s1, s2)


def _rope_tables(s, layout):
    pos = jnp.arange(s, dtype=F32)[:, None]
    lane = jnp.arange(LANES)
    if layout == "mla":
        dim, hs = MLA_ROPE, MLA_ROPE // 2
        r = lane - MLA_NOPE
        active = (r >= 0) & (r < MLA_ROPE)
    else:
        dim, hs = SWA_DIM, SWA_DIM // 2
        r = lane % SWA_DIM
        active = jnp.ones_like(lane, dtype=bool)
    f = jnp.where(active, r % hs, 0)
    inv = ROPE_THETA ** (-(2.0 * f.astype(F32)) / dim)
    ang = pos * inv[None, :]
    cos, sin = jnp.cos(ang), jnp.sin(ang)
    first = (active & (r < hs))[None, :]
    second = (active & (r >= hs))[None, :]
    c = jnp.where(active[None, :], cos, 1.0)
    s1 = jnp.where(first, -sin, 0.0)
    s2 = jnp.where(second, sin, 0.0)
    return (c, s1, s2), hs


def _gated(o_refs, g_ref):
    gv = g_ref[...]
    o = o_refs[0][...] if len(o_refs) == 1 else jnp.concatenate([r[...] for r in o_refs], axis=1)
    return (o * (gv * jax.nn.sigmoid(gv))).astype(BF16)


def _gated_mm_call(o_parts, gate, wb):
    s, k = gate.shape
    n = wb.shape[1]
    tm = _pick(s, (1024, 512, 256, 128))
    tn = _pick(n, (1024, 512, 256, 128))
    widths = [o.shape[1] for o in o_parts]
    no = len(widths)

    def body(*refs):
        refs[-1][...] = jnp.dot(_gated(refs[:no], refs[no]), refs[no + 1][...], preferred_element_type=F32)

    specs = [pl.BlockSpec((tm, wd), lambda i, j: (i, 0)) for wd in widths]
    return pl.pallas_call(
        body, name=f"gated_mm_{no}", grid=(s // tm, n // tn),
        in_specs=specs + [pl.BlockSpec((tm, k), lambda i, j: (i, 0)), pl.BlockSpec((k, tn), lambda i, j: (0, j))],
        out_specs=pl.BlockSpec((tm, tn), lambda i, j: (i, j)), out_shape=jax.ShapeDtypeStruct((s, n), F32),
        compiler_params=_params(dimension_semantics=("parallel", "parallel")),
    )(*o_parts, gate, wb)


def _gated_mm_tn_call(o_parts, gate, g):
    s, k = gate.shape
    n = g.shape[1]
    ts = _pick(s, (512, 256, 128))
    ns = s // ts
    widths = [o.shape[1] for o in o_parts]
    no = len(widths)

    def body(*refs):
        g_ref, o_ref, acc_ref = refs[no + 1], refs[no + 2], refs[no + 3]
        ss = pl.program_id(0)

        @pl.when(ss == 0)
        def _():
            acc_ref[...] = jnp.zeros_like(acc_ref)

        acc_ref[...] += lax.dot_general(_gated(refs[:no], refs[no]), g_ref[...].astype(BF16),
                                        (((0,), (0,)), ((), ())), preferred_element_type=F32)

        @pl.when(ss == ns - 1)
        def _():
            o_ref[...] = acc_ref[...]

    specs = [pl.BlockSpec((ts, wd), lambda ss: (ss, 0)) for wd in widths]
    return pl.pallas_call(
        body, name=f"gated_mm_tn_{no}", grid=(ns,),
        in_specs=specs + [pl.BlockSpec((ts, k), lambda ss: (ss, 0)), pl.BlockSpec((ts, n), lambda ss: (ss, 0))],
        out_specs=pl.BlockSpec((k, n), lambda ss: (0, 0)), out_shape=jax.ShapeDtypeStruct((k, n), F32),
        scratch_shapes=[pltpu.VMEM((k, n), F32)], compiler_params=_params(dimension_semantics=("arbitrary",)),
    )(*o_parts, gate, g)


def _gate_bwd_call(o_parts, gate, dz):
    s, w = gate.shape
    tr = _row_block(s)
    widths = [o.shape[1] for o in o_parts]
    n = len(widths)

    def body(*refs):
        o_refs, g_ref, dz_ref = refs[:n], refs[n], refs[n + 1]
        do_refs, dg_ref = refs[n + 2:2 * n + 2], refs[2 * n + 2]
        off = 0
        for o_ref, do_ref, wd in zip(o_refs, do_refs, widths):
            gv = g_ref[:, off:off + wd]
            sg = jax.nn.sigmoid(gv)
            dzv = dz_ref[:, off:off + wd]
            do_ref[...] = dzv * (gv * sg)
            dg_ref[:, off:off + wd] = dzv * o_ref[...] * (sg * (1.0 + gv * (1.0 - sg)))
            off += wd

    specs = [pl.BlockSpec((tr, wd), lambda i: (i, 0)) for wd in widths]
    row = pl.BlockSpec((tr, w), lambda i: (i, 0))
    outs = pl.pallas_call(
        body, name=f"gate_bwd_{n}", grid=(s // tr,), in_specs=specs + [row, row], out_specs=specs + [row],
        out_shape=[jax.ShapeDtypeStruct((s, wd), F32) for wd in widths] + [jax.ShapeDtypeStruct((s, w), F32)],
        compiler_params=_params(dimension_semantics=("parallel",)),
    )(*o_parts, gate, dz)
    return tuple(outs[:n]), outs[n]


@jax.custom_vjp
def gated_mm(o_parts, gate, w):
    return _gated_mm_call(o_parts, gate, w.astype(BF16))


def _gated_mm_fwd(o_parts, gate, w):
    wb = w.astype(BF16)
    return _gated_mm_call(o_parts, gate, wb), (o_parts, gate, wb)


def _gated_mm_bwd(res, g):
    o_parts, gate, wb = res
    do_parts, dgate = _gate_bwd_call(o_parts, gate, _mm_nn(g, wb, b_transposed=True))
    return do_parts, dgate, _gated_mm_tn_call(o_parts, gate, g)


gated_mm.defvjp(_gated_mm_fwd, _gated_mm_bwd)


def _loss_call(y, t):
    s, k = y.shape
    tr = _row_block(s)
    nsteps = s // tr

    def body(y_ref, t_ref, l_ref, dy_ref, acc_ref):
        i = pl.program_id(0)

        @pl.when(i == 0)
        def _():
            acc_ref[...] = jnp.zeros_like(acc_ref)

        d = y_ref[...] - t_ref[...]
        dy_ref[...] = d / k
        acc_ref[...] += jnp.sum(d * d, axis=0, keepdims=True)

        @pl.when(i == nsteps - 1)
        def _():
            tot = jnp.sum(acc_ref[...], axis=1, keepdims=True) * (0.5 / k)
            l_ref[...] = jnp.broadcast_to(tot, l_ref.shape)

    row = pl.BlockSpec((tr, k), lambda i: (i, 0))
    return pl.pallas_call(
        body, name="loss", grid=(nsteps,), in_specs=[row, row],
        out_specs=[pl.BlockSpec((1, LANES), lambda i: (0, 0)), row],
        out_shape=[jax.ShapeDtypeStruct((1, LANES), F32), jax.ShapeDtypeStruct((s, k), F32)],
        scratch_shapes=[pltpu.VMEM((1, k), F32)], compiler_params=_params(dimension_semantics=("arbitrary",)),
    )(y, t)


@jax.custom_vjp
def mse_loss(y, t):
    return _loss_call(y, t)[0][0, 0]


def _mse_fwd(y, t):
    l, dy = _loss_call(y, t)
    return l[0, 0], (dy, t)


mse_loss.defvjp(_mse_fwd, lambda res, g: (g * res[0], jnp.zeros_like(res[1])))


def _scan_call(x, b, mode):
    s, w = x.shape
    nt = s // 8

    def tile_scan(t):
        row = lax.broadcasted_iota(jnp.int32, (8, w), 0)
        for sh in (1, 2, 4):
            t = t + jnp.where(row >= sh, pltpu.roll(t, sh, 0), 0.0)
        return t

    def body(x_ref, b_ref, o_ref):
        def step(i, carry):
            rows = pl.ds(pl.multiple_of(i * 8, 8), 8)
            t = x_ref[rows, :]
            if mode == "fwd":
                t = jax.nn.log_sigmoid(t + b_ref[...])
            t = tile_scan(t) + carry
            o_ref[rows, :] = t
            return t[7:8, :]

        total = lax.fori_loop(0, nt, step, jnp.zeros((1, w), F32))
        if mode == "rev":
            def fix(i, c):
                rows = pl.ds(pl.multiple_of(i * 8, 8), 8)
                o_ref[rows, :] = total - o_ref[rows, :] + x_ref[rows, :]
                return c
            lax.fori_loop(0, nt, fix, 0)

    full = pl.BlockSpec((s, w), lambda: (0, 0))
    return pl.pallas_call(
        body, name=f"scan_{mode}", in_specs=[full, pl.BlockSpec((1, w), lambda: (0, 0))], out_specs=full,
        out_shape=jax.ShapeDtypeStruct((s, w), F32), compiler_params=_params(),
    )(x, b)


def _fox_dlogit_call(x, b, dlogf):
    s, w = x.shape
    tr = _row_block(s)

    def body(x_ref, b_ref, d_ref, dx_ref, db_ref):
        @pl.when(pl.program_id(0) == 0)
        def _():
            db_ref[...] = jnp.zeros_like(db_ref)

        dx = d_ref[...] * jax.nn.sigmoid(-(x_ref[...] + b_ref[...]))
        dx_ref[...] = dx
        db_ref[...] += jnp.sum(dx, axis=0, keepdims=True)

    row = pl.BlockSpec((tr, w), lambda i: (i, 0))
    vec = pl.BlockSpec((1, w), lambda i: (0, 0))
    return pl.pallas_call(
        body, name="fox_dlogit", grid=(s // tr,), in_specs=[row, vec, row], out_specs=[row, vec],
        out_shape=[jax.ShapeDtypeStruct((s, w), F32), jax.ShapeDtypeStruct((1, w), F32)],
        compiler_params=_params(dimension_semantics=("arbitrary",)),
    )(x, b, dlogf)


@jax.custom_vjp
def fox_cum(fl, b):
    return _scan_call(fl, b, "fwd")


def _fox_cum_bwd(res, dcum):
    fl, b = res
    dlogf = _scan_call(dcum, b, "rev")
    return _fox_dlogit_call(fl, b, dlogf)


fox_cum.defvjp(lambda fl, b: (_scan_call(fl, b, "fwd"), (fl, b)), _fox_cum_bwd)


def _lane_col(x, lane_idx):
    lane = lax.broadcasted_iota(jnp.int32, (1, x.shape[1]), 1)
    return jnp.sum(jnp.where(lane == lane_idx, x, 0.0), axis=1, keepdims=True)


def _row_of(x, row_idx):
    row = lax.broadcasted_iota(jnp.int32, (x.shape[0], 1), 0)
    return jnp.sum(jnp.where(row == row_idx, x, 0.0), axis=0, keepdims=True)


def _attn_cfg(mode, s):
    if mode == "swa":
        blk = 256 if s >= 2048 else 128
        return dict(blk=blk, n_outer=SWA_KV_HEADS, pps=4, wide=False, scale=SWA_DIM ** -0.5)
    blk = 512 if s >= 2048 else 128
    if mode == "mla":
        return dict(blk=blk, n_outer=4, pps=1, wide=True, scale=(MLA_NOPE + MLA_ROPE) ** -0.5)
    return dict(blk=blk, n_outer=4, pps=1, wide=False, scale=FOX_DIM ** -0.5)


ROW_CHUNK = 32


def _unrolled(n, body, carry):
    for c in range(n):
        carry = body(c, carry)
    return carry


def _valid_rows(mode, i, jb, blk, r0, rc):
    qpos = i * blk + r0 + lax.broadcasted_iota(jnp.int32, (rc, blk), 0)
    kpos = jb * blk + lax.broadcasted_iota(jnp.int32, (rc, blk), 1)
    ok = kpos <= qpos
    if mode == "swa":
        ok = ok & (qpos - kpos < WINDOW)
    return ok


def _attn_fwd_call(mode, q, k, v, extra):
    s = q.shape[0]
    cfg = _attn_cfg(mode, s)
    blk, n_outer, pps, wide, scale = cfg["blk"], cfg["n_outer"], cfg["pps"], cfg["wide"], cfg["scale"]
    rc = ROW_CHUNK
    nq = s // blk
    swa, fox = mode == "swa", mode == "fox"
    qw = (2 * LANES if wide else LANES) * pps
    kw = 2 * LANES if wide else LANES
    ow = LANES * pps
    reps = blk // LANES

    def body(*refs):
        if not swa:
            it_ref, jt_ref = refs[:2]
            refs = refs[2:]
        q_ref, k_ref, v_ref = refs[:3]
        n_in = 3
        if fox:
            cum_ref, cumt_ref = refs[3:5]
            n_in = 5
        if swa:
            sink_ref = refs[3]
            n_in = 4
        o_ref, lse_ref, m_ref, l_ref, acc_ref, a_ref, s_all, p_all, c_all = refs[n_in:]
        p_id = pl.program_id(0)
        if swa:
            i, j = pl.program_id(1), pl.program_id(2)
            jb, run, first, last = i - 1 + j, (i - 1 + j) >= 0, j == 0, j == 1
        else:
            i, j = it_ref[pl.program_id(1)], jt_ref[pl.program_id(1)]
            jb, first, last = j, j == 0, j == i
        lane = lax.broadcasted_iota(jnp.int32, (1, LANES), 1)
        msk = [lane < HALF, lane >= HALF]

        @pl.when(first)
        def _():
            for hh in range(2 * pps):
                if swa:
                    m_ref[hh] = jnp.broadcast_to(sink_ref[hh:hh + 1, :], (blk, LANES))
                    l_ref[hh] = jnp.ones((blk, LANES), F32)
                else:
                    m_ref[hh] = jnp.full((blk, LANES), NEG, F32)
                    l_ref[hh] = jnp.zeros((blk, LANES), F32)
            acc_ref[...] = jnp.zeros_like(acc_ref)

        def process(masked):
            for pp in range(pps):
                vb = v_ref[...]
                pvs = []
                for h in range(2):
                    hh = 2 * pp + h
                    s_ref, p_ref, c_ref = s_all.at[hh], p_all.at[hh], c_all.at[hh]
                    if wide:
                        qh = q_ref[:, h * LANES:(h + 1) * LANES] * scale
                        kh = k_ref[:, h * LANES:(h + 1) * LANES]
                    else:
                        qh = jnp.where(msk[h], q_ref[:, pp * LANES:(pp + 1) * LANES], 0.0) * scale
                        kh = k_ref[...]
                    s_ref[...] = lax.dot_general(qh.astype(BF16), kh.astype(BF16), (((1,), (1,)), ((), ())),
                                                 preferred_element_type=F32)
                    if fox:
                        head = 2 * p_id + h
                        c_ref[...] = jnp.broadcast_to(_lane_col(cum_ref[...], head), (blk, LANES))
                        ck = _row_of(cumt_ref[...], head)

                    def chunk(c, carry, hh=hh, h=h):
                        r0 = c * rc
                        rows = pl.ds(r0, rc)
                        u = s_ref[rows, :]
                        if fox:
                            u = u - ck
                        if masked:
                            u = jnp.where(_valid_rows(mode, i, jb, blk, r0, rc), u, NEG)
                        m_prev, l_prev = m_ref[hh, rows, :], l_ref[hh, rows, :]
                        m_cur = jnp.max(u, axis=1, keepdims=True)
                        if fox:
                            m_cur = m_cur + c_ref[rows, :]
                        m_next = jnp.maximum(m_prev, m_cur)
                        shift = m_next - c_ref[rows, :] if fox else m_next
                        p = jnp.exp(u - jnp.tile(shift, (1, reps)))
                        alpha = jnp.exp(m_prev - m_next)
                        l_ref[hh, rows, :] = alpha * l_prev + jnp.sum(p, axis=1, keepdims=True)
                        m_ref[hh, rows, :] = m_next
                        a_ref[hh, rows, :] = alpha
                        p_ref[rows, :] = p.astype(BF16)
                        return carry

                    _unrolled(blk // rc, chunk, 0)
                    vh = jnp.where(msk[h], vb, 0.0).astype(BF16)
                    pvs.append(jnp.dot(p_ref[...], vh, preferred_element_type=F32))
                acc_ref[pp] = acc_ref[pp] * jnp.where(msk[0], a_ref[2 * pp], a_ref[2 * pp + 1]) + pvs[0] + pvs[1]

        if swa:
            pl.when(run)(lambda: process(True))
        else:
            pl.when(j < i)(lambda: process(False))
            pl.when(j == i)(lambda: process(True))

        @pl.when(last)
        def _():
            for pp in range(pps):
                l0, l1 = l_ref[2 * pp], l_ref[2 * pp + 1]
                o_ref[:, pp * LANES:(pp + 1) * LANES] = acc_ref[pp] / jnp.where(msk[0], l0, l1)
                lse_ref[:, pp * LANES:(pp + 1) * LANES] = jnp.where(
                    msk[0], m_ref[2 * pp] + jnp.log(l0), m_ref[2 * pp + 1] + jnp.log(l1))

    if swa:
        kv_map = lambda g, i, j: (jnp.maximum(i - 1 + j, 0), g)
        q_map = lambda g, i, j: (i, g)
        grid, tables, sem = (n_outer, nq, 2), [], ("parallel", "parallel", "arbitrary")
    else:
        tri = [(i, j) for i in range(nq) for j in range(i + 1)]
        tables = [jnp.asarray([t[0] for t in tri], jnp.int32), jnp.asarray([t[1] for t in tri], jnp.int32)]
        kv_map = lambda p, t, it, jt: (jt[t], p)
        q_map = lambda p, t, it, jt: (it[t], p)
        grid, sem = (n_outer, len(tri)), ("parallel", "arbitrary")
    in_specs = [pl.BlockSpec((blk, qw), q_map), pl.BlockSpec((blk, kw), kv_map), pl.BlockSpec((blk, LANES), kv_map)]
    args = [q, k, v]
    if fox:
        cum, cumt = extra
        in_specs += [pl.BlockSpec((blk, LANES), lambda p, t, it, jt: (it[t], 0)),
                     pl.BlockSpec((8, blk), lambda p, t, it, jt: (0, jt[t]))]
        args += [cum, cumt]
    if swa:
        in_specs += [pl.BlockSpec((8, LANES), lambda g, i, j: (g, 0))]
        args += [extra]
    n_pairs = n_outer * pps
    return pl.pallas_call(
        body, name=f"attn_fwd_{mode}",
        grid_spec=pltpu.PrefetchScalarGridSpec(
            num_scalar_prefetch=len(tables), grid=grid, in_specs=in_specs,
            out_specs=[pl.BlockSpec((blk, ow), q_map), pl.BlockSpec((blk, ow), q_map)],
            scratch_shapes=[pltpu.VMEM((2 * pps, blk, LANES), F32), pltpu.VMEM((2 * pps, blk, LANES), F32),
                            pltpu.VMEM((pps, blk, LANES), F32), pltpu.VMEM((2 * pps, blk, LANES), F32),
                            pltpu.VMEM((2 * pps, blk, blk), F32), pltpu.VMEM((2 * pps, blk, blk), BF16),
                            pltpu.VMEM((2 * pps, blk, LANES), F32)]),
        out_shape=[jax.ShapeDtypeStruct((s, n_pairs * LANES), F32), jax.ShapeDtypeStruct((s, n_pairs * LANES), F32)],
        compiler_params=_params(dimension_semantics=sem),
    )(*tables, *args)


def _attn_bwd_call(mode, q, k, v, extra, lse, o, do):
    s = q.shape[0]
    cfg = _attn_cfg(mode, s)
    blk, n_outer, pps, wide, scale = cfg["blk"], cfg["n_outer"], cfg["pps"], cfg["wide"], cfg["scale"]
    rc = ROW_CHUNK
    nq = s // blk
    swa, fox = mode == "swa", mode == "fox"
    qw = (2 * LANES if wide else LANES) * pps
    kw = 2 * LANES if wide else LANES
    ow = LANES * pps
    reps = blk // LANES

    assert not swa

    def body(*refs):
        jt_ref, it_ref = refs[:2]
        refs = refs[2:]
        q_ref, k_ref, v_ref, lse_ref, o_ref, do_ref = refs[:6]
        n_in = 6
        if fox:
            cum_ref, cumt_ref = refs[6:8]
            n_in = 8
        dq_ref, dk_ref, dv_ref = refs[n_in:n_in + 3]
        n_out = n_in + 3
        if fox:
            dck_ref, dcq_ref = refs[n_out:n_out + 2]
            n_out += 2
        dk_acc, dv_acc, s_all, dp_all, p_all, ds_all, e_all, d_all = refs[n_out:n_out + 8]
        if fox:
            dck_acc, rs_all = refs[n_out + 8:n_out + 10]
        p_id, t = pl.program_id(0), pl.program_id(1)
        j, ii = jt_ref[t], it_ref[t]
        i, first_i, last_i = ii, ii == j, ii == nq - 1
        lane = lax.broadcasted_iota(jnp.int32, (1, LANES), 1)
        msk = [lane < HALF, lane >= HALF]

        @pl.when(t == 0)
        def _():
            dq_ref[...] = jnp.zeros_like(dq_ref)
            if fox:
                dcq_ref[...] = jnp.zeros_like(dcq_ref)

        @pl.when(first_i)
        def _():
            dk_acc[...] = jnp.zeros_like(dk_acc)
            dv_acc[...] = jnp.zeros_like(dv_acc)
            if fox:
                dck_acc[...] = jnp.zeros_like(dck_acc)

        def process(masked):
            rows = pl.ds(pl.multiple_of(i * blk, blk), blk)
            vb = v_ref[...].astype(BF16)
            dv_parts, dk_parts = [], []
            for pp in range(pps):
                psl = slice(pp * LANES, (pp + 1) * LANES)
                lse_blk, do_blk = lse_ref[:, psl], do_ref[:, psl]
                doo = do_blk * o_ref[:, psl]
                dq_pair = []
                for h in range(2):
                    hh = 2 * pp + h
                    s_ref, dp_ref, p_ref, ds_ref = s_all.at[hh], dp_all.at[hh], p_all.at[hh], ds_all.at[hh]
                    e_ref, d_ref = e_all.at[hh], d_all.at[hh]
                    if fox:
                        rs_ref = rs_all.at[hh]
                    if wide:
                        hsl = slice(h * LANES, (h + 1) * LANES)
                        qh = (q_ref[:, hsl] * scale).astype(BF16)
                        kh = k_ref[:, hsl].astype(BF16)
                    else:
                        qh = (jnp.where(msk[h], q_ref[:, psl], 0.0) * scale).astype(BF16)
                        kh = k_ref[...].astype(BF16)
                    s_ref[...] = lax.dot_general(qh, kh, (((1,), (1,)), ((), ())), preferred_element_type=F32)
                    do_h = jnp.where(msk[h], do_blk, 0.0).astype(BF16)
                    dp_ref[...] = lax.dot_general(do_h, vb, (((1,), (1,)), ((), ())), preferred_element_type=F32)
                    lse_h = _lane_col(lse_blk, HALF * h)
                    d_h = jnp.sum(jnp.where(msk[h], doo, 0.0), axis=1, keepdims=True)
                    e_ref[...] = jnp.broadcast_to(lse_h, (blk, LANES))
                    d_ref[...] = jnp.broadcast_to(d_h, (blk, LANES))
                    if fox:
                        head = 2 * p_id + h
                        e_ref[...] = e_ref[...] - jnp.broadcast_to(_lane_col(cum_ref[...], head), (blk, LANES))
                        ck = _row_of(cumt_ref[...], head)

                    def chunk(c, colsum):
                        r0 = c * rc
                        cr = pl.ds(r0, rc)
                        u = s_ref[cr, :]
                        if fox:
                            u = u - ck
                        p = jnp.exp(u - jnp.tile(e_ref[cr, :], (1, reps)))
                        if masked:
                            p = jnp.where(_valid_rows(mode, i, j, blk, r0, rc), p, 0.0)
                        ds = p * (dp_ref[cr, :] - jnp.tile(d_ref[cr, :], (1, reps)))
                        p_ref[cr, :] = p.astype(BF16)
                        ds_ref[cr, :] = ds.astype(BF16)
                        if fox:
                            colsum = colsum + jnp.sum(ds, axis=0, keepdims=True)
                            rs_ref[cr, :] = jnp.broadcast_to(jnp.sum(ds, axis=1, keepdims=True), (rc, LANES))
                        return colsum

                    colsum = _unrolled(blk // rc, chunk, jnp.zeros((1, blk), F32))
                    dv_parts.append(lax.dot_general(p_ref[...], do_h, (((0,), (0,)), ((), ())),
                                                    preferred_element_type=F32))
                    if fox:
                        dck_acc[h:h + 1, :] += -colsum
                        dcq_ref[rows, :] += jnp.where(msk[h], rs_ref[...], 0.0)
                    dq_h = jnp.dot(ds_ref[...], kh, preferred_element_type=F32) * scale
                    dk_h = lax.dot_general(ds_ref[...], qh, (((0,), (0,)), ((), ())), preferred_element_type=F32)
                    if wide:
                        dq_ref[rows, hsl] += dq_h
                        dk_acc[:, hsl] += dk_h
                    else:
                        dq_pair.append(jnp.where(msk[h], dq_h, 0.0))
                        dk_parts.append(dk_h)
                if not wide:
                    dq_ref[rows, psl] += dq_pair[0] + dq_pair[1]
            dv_acc[...] += functools.reduce(lambda a, b: a + b, dv_parts)
            if not wide:
                dk_acc[...] += functools.reduce(lambda a, b: a + b, dk_parts)

        pl.when(ii > j)(lambda: process(False))
        pl.when(ii == j)(lambda: process(True))

        @pl.when(last_i)
        def _():
            dk_ref[...] = dk_acc[...]
            dv_ref[...] = dv_acc[...]
            if fox:
                dck_ref[0] = dck_acc[...]

    tri = [(j, i) for j in range(nq) for i in range(j, nq)]
    tables = [jnp.asarray([t[0] for t in tri], jnp.int32), jnp.asarray([t[1] for t in tri], jnp.int32)]
    q_map = lambda p, t, jt, it: (it[t], p)
    kv_map = lambda p, t, jt, it: (jt[t], p)
    in_specs = [pl.BlockSpec((blk, qw), q_map), pl.BlockSpec((blk, kw), kv_map), pl.BlockSpec((blk, LANES), kv_map),
                pl.BlockSpec((blk, ow), q_map), pl.BlockSpec((blk, ow), q_map), pl.BlockSpec((blk, ow), q_map)]
    args = [q, k, v, lse, o, do]
    n_pairs = n_outer * pps
    out_specs = [pl.BlockSpec((s, qw), lambda p, t, jt, it: (0, p)), pl.BlockSpec((blk, kw), kv_map),
                 pl.BlockSpec((blk, LANES), kv_map)]
    out_shape = [jax.ShapeDtypeStruct((s, q.shape[1]), F32), jax.ShapeDtypeStruct((s, k.shape[1]), F32),
                 jax.ShapeDtypeStruct((s, v.shape[1]), F32)]
    nh = 2 * pps
    scratch = [pltpu.VMEM((blk, kw), F32), pltpu.VMEM((blk, LANES), F32), pltpu.VMEM((nh, blk, blk), F32),
               pltpu.VMEM((nh, blk, blk), F32), pltpu.VMEM((nh, blk, blk), BF16), pltpu.VMEM((nh, blk, blk), BF16),
               pltpu.VMEM((nh, blk, LANES), F32), pltpu.VMEM((nh, blk, LANES), F32)]
    if fox:
        cum, cumt = extra
        in_specs += [pl.BlockSpec((blk, LANES), lambda p, t, jt, it: (it[t], 0)),
                     pl.BlockSpec((8, blk), lambda p, t, jt, it: (0, jt[t]))]
        args += [cum, cumt]
        out_specs += [pl.BlockSpec((1, 8, blk), lambda p, t, jt, it: (p, 0, jt[t])),
                      pl.BlockSpec((s, LANES), lambda p, t, jt, it: (0, p))]
        out_shape += [jax.ShapeDtypeStruct((n_pairs, 8, s), F32), jax.ShapeDtypeStruct((s, n_pairs * LANES), F32)]
        scratch += [pltpu.VMEM((8, blk), F32), pltpu.VMEM((nh, blk, LANES), F32)]
    return pl.pallas_call(
        body, name=f"attn_bwd_{mode}",
        grid_spec=pltpu.PrefetchScalarGridSpec(num_scalar_prefetch=2, grid=(n_outer, len(tri)), in_specs=in_specs,
                                               out_specs=out_specs, scratch_shapes=scratch),
        out_shape=out_shape, compiler_params=_params(dimension_semantics=("parallel", "arbitrary")),
    )(*tables, *args)


def _swa_masks(i, blk):
    r = lax.broadcasted_iota(jnp.int32, (blk, blk), 0)
    c = lax.broadcasted_iota(jnp.int32, (blk, blk), 1)
    return (c > r) & (i > 0), c <= r


def _nt(a, b):
    return lax.dot_general(a, b, (((1,), (1,)), ((), ())), preferred_element_type=F32)


def _tn(a, b):
    return lax.dot_general(a, b, (((0,), (0,)), ((), ())), preferred_element_type=F32)


def _swa_bwd_call(q, k, v, sink, lse, o, do):
    s = q.shape[0]
    blk, pps, scale = WINDOW, 4, SWA_DIM ** -0.5
    nq = s // blk

    def body(q_ref, kp_ref, ko_ref, vp_ref, vo_ref, sink_ref, lse_ref, o_ref, do_ref,
             dq_ref, dk_ref, dv_ref, dsink_ref, ck_ref, cv_ref):
        i = pl.program_id(1)
        lane = lax.broadcasted_iota(jnp.int32, (1, LANES), 1)
        msk = [lane < HALF, lane >= HALF]

        @pl.when(i == 0)
        def _():
            ck_ref[...] = jnp.zeros_like(ck_ref)
            cv_ref[...] = jnp.zeros_like(cv_ref)
            dsink_ref[...] = jnp.zeros_like(dsink_ref)

        @pl.when(i < nq)
        def _():
            ok_prev, ok_own = _swa_masks(i, blk)
            kp, ko = kp_ref[...].astype(BF16), ko_ref[...].astype(BF16)
            vp, vo = vp_ref[...].astype(BF16), vo_ref[...].astype(BF16)
            dkp, dko, dvp, dvo = [], [], [], []
            for pp in range(pps):
                psl = slice(pp * LANES, (pp + 1) * LANES)
                qp, do_blk = q_ref[:, psl], do_ref[:, psl]
                doo = do_blk * o_ref[:, psl]
                dqs = []
                for h in range(2):
                    hh = 2 * pp + h
                    qh = (jnp.where(msk[h], qp, 0.0) * scale).astype(BF16)
                    lse_h = jnp.broadcast_to(_lane_col(lse_ref[:, psl], HALF * h), (blk, LANES))
                    d_h = jnp.broadcast_to(jnp.sum(jnp.where(msk[h], doo, 0.0), axis=1, keepdims=True), (blk, LANES))
                    p_p = jnp.where(ok_prev, jnp.exp(_nt(qh, kp) - lse_h), 0.0)
                    p_o = jnp.where(ok_own, jnp.exp(_nt(qh, ko) - lse_h), 0.0)
                    do_h = jnp.where(msk[h], do_blk, 0.0).astype(BF16)
                    ds_p = (p_p * (_nt(do_h, vp) - d_h)).astype(BF16)
                    ds_o = (p_o * (_nt(do_h, vo) - d_h)).astype(BF16)
                    dq_h = (jnp.dot(ds_p, kp, preferred_element_type=F32)
                            + jnp.dot(ds_o, ko, preferred_element_type=F32)) * scale
                    dqs.append(jnp.where(msk[h], dq_h, 0.0))
                    dkp.append(_tn(ds_p, qh))
                    dko.append(_tn(ds_o, qh))
                    dvp.append(_tn(p_p.astype(BF16), do_h))
                    dvo.append(_tn(p_o.astype(BF16), do_h))
                    sink_row = sink_ref[hh:hh + 1, :]
                    dsink_ref[hh:hh + 1, :] += -jnp.sum(jnp.exp(sink_row - lse_h) * d_h, axis=0, keepdims=True)
                dq_ref[:, psl] = dqs[0] + dqs[1]
            total = lambda parts: functools.reduce(lambda a, b: a + b, parts)
            dk_ref[...] = ck_ref[...] + total(dkp)
            dv_ref[...] = cv_ref[...] + total(dvp)
            ck_ref[...] = total(dko)
            cv_ref[...] = total(dvo)

        @pl.when(i == nq)
        def _():
            dk_ref[...] = ck_ref[...]
            dv_ref[...] = cv_ref[...]

    last = nq - 1
    prev = lambda g, i: (jnp.maximum(i - 1, 0), g)
    own = lambda g, i: (jnp.minimum(i, last), g)
    qspec = pl.BlockSpec((blk, pps * LANES), own)
    kspec = lambda m: pl.BlockSpec((blk, LANES), m)
    sspec = pl.BlockSpec((8, LANES), lambda g, i: (g, 0))
    return pl.pallas_call(
        body, name="swa_bwd", grid=(SWA_KV_HEADS, nq + 1),
        in_specs=[qspec, kspec(prev), kspec(own), kspec(prev), kspec(own), sspec, qspec, qspec, qspec],
        out_specs=[qspec, kspec(prev), kspec(prev), sspec],
        out_shape=[jax.ShapeDtypeStruct(q.shape, F32), jax.ShapeDtypeStruct(k.shape, F32),
                   jax.ShapeDtypeStruct(v.shape, F32), jax.ShapeDtypeStruct((SWA_HEADS, LANES), F32)],
        scratch_shapes=[pltpu.VMEM((blk, LANES), F32), pltpu.VMEM((blk, LANES), F32)],
        compiler_params=_params(dimension_semantics=("parallel", "arbitrary")),
    )(q, k, k, v, v, sink, lse, o, do)


def _make_attn(mode):
    swa = mode == "swa"

    @jax.custom_vjp
    def attn(q, k, v, extra):
        return fwd(q, k, v, extra)[0]

    def fwd(q, k, v, extra):
        o, lse = _attn_fwd_call(mode, q, k, v, extra)
        return o, (q, k, v, extra, o, lse)

    def bwd(res, do):
        q, k, v, extra, o, lse = res
        outs = (_swa_bwd_call(q, k, v, extra, lse, o, do) if swa
                else _attn_bwd_call(mode, q, k, v, extra, lse, o, do))
        dq, dk, dv = outs[:3]
        if mode == "fox":
            cum, cumt = extra
            dck = outs[3]
            dcumt = dck[:, :2, :].reshape(FOX_HEADS, -1)
            dcq = outs[4].reshape(-1, FOX_HEADS, HALF)[:, :, 0]
            dextra = (jnp.pad(dcq, ((0, 0), (0, LANES - FOX_HEADS))), dcumt)
        elif mode == "swa":
            dextra = jnp.where(jnp.arange(LANES)[None, :] == 0, outs[3], 0.0)
        else:
            dextra = None
        return dq, dk, dv, dextra

    attn.defvjp(fwd, bwd)
    return attn


attn_mla = _make_attn("mla")
attn_fox = _make_attn("fox")
attn_swa = _make_attn("swa")


def _ukv_layout(w):
    r = w.shape[0]
    w3 = w.reshape(r, MLA_HEADS, MLA_NOPE + MLA_V)
    wk = jnp.pad(w3[:, :, :MLA_NOPE], ((0, 0), (0, 0), (0, LANES - MLA_NOPE))).reshape(r, MLA_HEADS * LANES)
    wv = w3[:, :, MLA_NOPE:].reshape(r, MLA_HEADS * MLA_V)
    return wk, wv


def _even_layer(x, w_in_cat, q_norm, w_uq_p, kv_norm, w_ukv, b_f, w_out, ln_g, ln_b, tabs_mla):
    tabs, hs = tabs_mla
    cq, ckv, kpe, fq, fk, fv, fl, gate = even_in_proj(x, relayout(w_in_cat, "even"), tabs)
    q = mm_rope(rms_norm(cq, q_norm), w_uq_p, tabs, hs)
    ckvn = rms_norm(ckv, kv_norm)
    wk, wv = _ukv_layout(w_ukv)
    kk = mm(ckvn, wk) + jnp.tile(kpe, (1, MLA_HEADS))
    o_mla = attn_mla(q, kk, mm(ckvn, wv), None)
    cum = fox_cum(fl, jnp.pad(b_f, (0, LANES - FOX_HEADS)).reshape(1, LANES))
    o_fox = attn_fox(fq, fk, fv, (cum, cum[:, :8].T))
    y = gated_mm((o_mla, o_fox), gate, w_out)
    return ln_res(x, y, ln_g, ln_b)


def _odd_layer(x, w_in_cat, sinks, w_out, ln_g, ln_b, tabs_swa):
    q, kd, vd, gate = odd_in_proj(x, relayout(w_in_cat, "odd"), tabs_swa[0])
    o = attn_swa(q, kd, vd, jnp.broadcast_to(sinks[:, None], (SWA_HEADS, LANES)))
    y = gated_mm((o,), gate, w_out)
    return ln_res(x, y, ln_g, ln_b)


EVEN_SHARDED = ["even_w_in", "even_w_uq", "even_w_ukv", "even_w_out"]
ODD_SHARDED = ["odd_w_in", "odd_w_out", "odd_ln_g", "odd_ln_b"]
EVEN_REPL = ["even_q_norm", "even_kv_norm", "even_b_f", "even_ln_g", "even_ln_b"]
ODD_REPL = ["odd_sinks"]


def _layer_names(layer):
    return (EVEN_SHARDED, EVEN_REPL) if layer % 2 == 0 else (ODD_SHARDED, ODD_REPL)


def _layer_of(name, j):
    return 2 * j if name.startswith("even") else 2 * j + 1


def _layer_apply(layer, p, x, tabs):
    if layer % 2 == 0:
        return _even_layer(x, p["even_w_in"], p["even_q_norm"], p["even_w_uq"], p["even_kv_norm"], p["even_w_ukv"],
                           p["even_b_f"], p["even_w_out"], p["even_ln_g"], p["even_ln_b"], tabs["mla"])
    return _odd_layer(x, p["odd_w_in"], p["odd_sinks"], p["odd_w_out"], p["odd_ln_g"], p["odd_ln_b"], tabs["swa"])


def _pad_rows(flat, mult):
    n = flat.shape[-1]
    per = mult * LANES
    padded = -(-n // per) * per
    if padded != n:
        flat = jnp.pad(flat, [(0, 0)] * (flat.ndim - 1) + [(0, padded - n)])
    return flat.reshape(flat.shape[:-1] + (padded // LANES, LANES))


def _pad_last(a, width):
    if a.shape[-1] == width:
        return a
    return jnp.pad(a, [(0, 0)] * (a.ndim - 1) + [(0, width - a.shape[-1])])


def _join(slots, axis):
    shp = list(slots.shape[1:])
    shp[axis] *= N_DEV
    return jnp.moveaxis(slots, 0, axis).reshape(shp)


def _split(full, axis):
    shp = full.shape
    t = full.reshape(shp[:axis] + (N_DEV, shp[axis] // N_DEV) + shp[axis + 1:])
    return jnp.moveaxis(t, axis, 0)


PAD_TO = {"even_w_in": SHARD_PAD, "even_w_uq": LANES, "odd_w_in": SHARD_PAD}


def kernel(x, even_w_in, even_q_norm, even_w_uq, even_kv_norm, even_w_ukv, even_b_f, even_w_out, even_ln_g, even_ln_b, odd_w_in, odd_sinks, odd_w_out, odd_ln_g, odd_ln_b, loss_target, m_even_w_in, m_even_q_norm, m_even_w_uq, m_even_kv_norm, m_even_w_ukv, m_even_b_f, m_even_w_out, m_even_ln_g, m_even_ln_b, m_odd_w_in, m_odd_sinks, m_odd_w_out, m_odd_ln_g, m_odd_ln_b, v_even_w_in, v_even_q_norm, v_even_w_uq, v_even_kv_norm, v_even_w_ukv, v_even_b_f, v_even_w_out, v_even_ln_g, v_even_ln_b, v_odd_w_in, v_odd_sinks, v_odd_w_out, v_odd_ln_g, v_odd_ln_b):
    w = dict(even_w_in=even_w_in, even_q_norm=even_q_norm, even_w_uq=even_w_uq, even_kv_norm=even_kv_norm,
             even_w_ukv=even_w_ukv, even_b_f=even_b_f, even_w_out=even_w_out, even_ln_g=even_ln_g, even_ln_b=even_ln_b,
             odd_w_in=odd_w_in, odd_sinks=odd_sinks, odd_w_out=odd_w_out, odd_ln_g=odd_ln_g, odd_ln_b=odd_ln_b)
    mom = dict(even_w_in=m_even_w_in, even_q_norm=m_even_q_norm, even_w_uq=m_even_w_uq, even_kv_norm=m_even_kv_norm,
               even_w_ukv=m_even_w_ukv, even_b_f=m_even_b_f, even_w_out=m_even_w_out, even_ln_g=m_even_ln_g,
               even_ln_b=m_even_ln_b, odd_w_in=m_odd_w_in, odd_sinks=m_odd_sinks, odd_w_out=m_odd_w_out,
               odd_ln_g=m_odd_ln_g, odd_ln_b=m_odd_ln_b)
    vel = dict(even_w_in=v_even_w_in, even_q_norm=v_even_q_norm, even_w_uq=v_even_w_uq, even_kv_norm=v_even_kv_norm,
               even_w_ukv=v_even_w_ukv, even_b_f=v_even_b_f, even_w_out=v_even_w_out, even_ln_g=v_even_ln_g,
               even_ln_b=v_even_ln_b, odd_w_in=v_odd_w_in, odd_sinks=v_odd_sinks, odd_w_out=v_odd_w_out,
               odd_ln_g=v_odd_ln_g, odd_ln_b=v_odd_ln_b)
    sharded = BIG + SMALL_SHARDED
    padded = lambda d, n: _pad_last(d[n], PAD_TO.get(n, d[n].shape[-1]))

    tabs = {"mla": _rope_tables(x.shape[1], "mla"), "swa": _rope_tables(x.shape[1], "swa")}
    keys = lambda layers: [(n, layer // 2) for layer in layers for n in _layer_names(layer)[0]]
    first, rest = keys([0]), keys([1, 2, 3])
    me = _lin(_me())

    def shard(n, j):
        a = padded(w, n)[j]
        return a.astype(BF16) if n in BIG else a

    in_slots = ("even_w_in", "odd_w_in")
    to_full = lambda n, g: g if n in in_slots else _join(g, SHARD_AXIS[n] - 1).astype(F32)
    to_slots = lambda n, g: g if n in in_slots else _split(g, SHARD_AXIS[n] - 1).astype(BF16 if n in BIG else F32)

    def layer_params(layer, full_of):
        shn, rpn = _layer_names(layer)
        p = {n: full_of(n) for n in shn}
        p.update({n: w[n][layer // 2] for n in rpn})
        return p

    got0 = _all_gather([shard(n, j) for n, j in first], "all_gather_first")
    got0, mine_rest = lax.optimization_barrier((got0, [shard(n, j) for n, j in rest]))
    got0 = dict(zip(first, got0))
    send_sems, recv_sems, srcs, lands, token = _split_start(mine_rest, False, "all_gather_rest_start")
    tie = lambda a: a + token[0, 0].astype(a.dtype)
    p0 = layer_params(0, lambda n: tie(to_full(n, got0[(n, 0)])))
    x1, vjp0 = jax.vjp(lambda p, xx: _layer_apply(0, p, xx, tabs), p0, x[0])
    got = _split_wait(send_sems, recv_sems, srcs, lands, x1, False, "all_gather_rest_wait")
    got = dict(zip(rest, _own_slot(got, [m[None] for m in mine_rest])))

    xs, vjps = x1, [vjp0]
    for layer in (1, 2, 3):
        p = layer_params(layer, lambda n: to_full(n, got[(n, layer // 2)]))
        xs, vjp = jax.vjp(lambda p_, xx, layer=layer: _layer_apply(layer, p_, xx, tabs), p, xs)
        vjps.append(vjp)
    loss_local, vjp_loss = jax.vjp(lambda y: mse_loss(y, loss_target[0]), xs)
    (dy,) = vjp_loss(jnp.ones((), F32))
    loss = lax.psum(loss_local, AXES)
    grads = {}
    for layer in (3, 2, 1):
        grads[layer], dy = vjps[layer](dy)

    parts_rest = [to_slots(n, grads[_layer_of(n, j)][n]) for n, j in rest]
    send_sems, recv_sems, srcs, lands, token = _split_start(parts_rest, True, "grad_exchange_rest_start")
    grads[0], grad_x = vjps[0](dy + token[0, 0])
    recv_rest = _split_wait(send_sems, recv_sems, srcs, lands, grad_x, True, "grad_exchange_rest_wait")
    recv = dict(zip(rest, _own_slot(recv_rest, [lax.dynamic_slice_in_dim(p, me, 1, axis=0) for p in parts_rest])))
    repl_grad = lambda n: jnp.stack([grads[_layer_of(n, j)][n] for j in (0, 1)])
    repl_rows = _pad_rows(jnp.concatenate([repl_grad(n).reshape(-1) for n in REPL]), 8)
    parts_last = [to_slots(n, grads[0][n]) for n, j in first]
    parts_last.append(jnp.broadcast_to(repl_rows[None], (N_DEV,) + repl_rows.shape))
    recv_last = _exchange(parts_last, "grad_exchange_last")
    recv.update(zip(first, recv_last[:-1]))

    g_out, d_out, m_out, v_out = {}, {}, {}, {}
    for n in sharded:
        r = jnp.stack([recv[(n, 0)], recv[(n, 1)]], axis=1)
        cols = r.shape[-1]
        flat2 = lambda d: padded(d, n).reshape(-1, cols)
        outs = _sum_adamw(r.reshape(N_DEV, -1, cols), flat2(w), flat2(mom), flat2(vel))
        for dst, o in zip((g_out, d_out, m_out, v_out), outs):
            dst[n] = o.reshape(w[n].shape[:-1] + (cols,))[..., :w[n].shape[-1]]
    pack = lambda d: _pad_rows(jnp.concatenate([d[n].reshape(-1) for n in REPL]), 8)
    outs = _sum_adamw(recv_last[-1], pack(w), pack(mom), pack(vel))
    for dst, o in zip((g_out, d_out, m_out, v_out), outs):
        flat, off = o.reshape(-1), 0
        for n in REPL:
            size = math.prod(w[n].shape)
            dst[n] = flat[off:off + size].reshape(w[n].shape)
            off += size
    return (loss, grad_x[None], *[g_out[n] for n in WEIGHTS], *[d_out[n] for n in WEIGHTS],
            *[m_out[n] for n in WEIGHTS], *[v_out[n] for n in WEIGHTS])
```

```python
import functools
import math

import jax
import jax.numpy as jnp
from jax import lax
from jax.experimental import pallas as pl
from jax.experimental.pallas import tpu as pltpu

F32 = jnp.float32
BF16 = jnp.bfloat16
LANES = 128
HALF = 64
N_DEV = 8
AXES = ("x", "y", "c")
VMEM_LIMIT = 48 * 1024 * 1024
VMEM_LIMIT_TALL = 56 * 1024 * 1024

D_MODEL = 1024
DEPTH = 4
ROPE_THETA = 10000.0
MLA_HEADS, MLA_NOPE, MLA_ROPE, MLA_V, MLA_Q_RANK, MLA_KV_RANK = 8, 64, 32, 64, 256, 128
FOX_HEADS, FOX_DIM = 8, 64
SWA_HEADS, SWA_KV_HEADS, SWA_DIM, WINDOW = 16, 2, 64, 128
RMS_EPS, LN_EPS = 1e-6, 1e-5
ALPHA = (2 * DEPTH) ** 0.25
ADAM_LR, ADAM_B1, ADAM_B2, ADAM_EPS, ADAM_WD, ADAM_STEP = 0.001, 0.9, 0.999, 1e-08, 0.01, 10
NEG = -1e30

WEIGHTS = ["even_w_in", "even_q_norm", "even_w_uq", "even_kv_norm", "even_w_ukv", "even_b_f", "even_w_out",
           "even_ln_g", "even_ln_b", "odd_w_in", "odd_sinks", "odd_w_out", "odd_ln_g", "odd_ln_b"]
SHARD_AXIS = {"even_w_in": 2, "even_w_uq": 2, "even_w_ukv": 2, "even_w_out": 1, "odd_w_in": 2, "odd_w_out": 1,
              "odd_ln_g": 1, "odd_ln_b": 1, "even_q_norm": None, "even_kv_norm": None, "even_b_f": None,
              "even_ln_g": None, "even_ln_b": None, "odd_sinks": None}
BIG = ["even_w_in", "even_w_uq", "even_w_ukv", "even_w_out", "odd_w_in", "odd_w_out"]
SMALL_SHARDED = ["odd_ln_g", "odd_ln_b"]
REPL = [n for n in WEIGHTS if SHARD_AXIS[n] is None]


def _pick(n, cands):
    for c in cands:
        if n % c == 0:
            return c
    return n


def _params(**kw):
    return pltpu.CompilerParams(vmem_limit_bytes=VMEM_LIMIT, **kw)


def _me():
    return lax.axis_index("x"), lax.axis_index("y"), lax.axis_index("c")


def _peer(k):
    x, y, c = _me()
    px = 1 - x if (k >> 2) & 1 else x
    py = 1 - y if (k >> 1) & 1 else y
    pc = 1 - c if k & 1 else c
    return px, py, pc


def _lin(p):
    return 4 * p[0] + 2 * p[1] + p[2]


def _comm_call(body, n, out_shape, args, name):
    any_spec = pl.BlockSpec(memory_space=pl.ANY)
    return pl.pallas_call(
        body, name=name, out_shape=out_shape, in_specs=[any_spec] * n, out_specs=[any_spec] * n,
        scratch_shapes=[pltpu.SemaphoreType.DMA((n, N_DEV - 1)), pltpu.SemaphoreType.DMA((n, N_DEV - 1)),
                        pltpu.SemaphoreType.DMA((n,))],
    )(*args)


def _all_gather(xs, name):
    n = len(xs)

    def body(*refs):
        x_refs, out_refs = refs[:n], refs[n:2 * n]
        send_sems, recv_sems, local_sems = refs[2 * n:]
        x, y, c = _me()
        me, sibling = (x, y, c), (x, y, 1 - c)
        chips = [(1 - x, y), (x, 1 - y), (1 - x, 1 - y)]

        def copy(a, k, block, to, src=None):
            slot = out_refs[a].at[_lin(block)]
            return pltpu.make_async_remote_copy(
                src_ref=slot if src is None else src, dst_ref=slot, send_sem=send_sems.at[a, k],
                recv_sem=recv_sems.at[a, k], device_id=to, device_id_type=pl.DeviceIdType.MESH)

        local = [pltpu.make_async_copy(x_refs[a], out_refs[a].at[_lin(me)], local_sems.at[a]) for a in range(n)]
        for cp in local:
            cp.start()
        sends = []
        for a in range(n):
            sends.append(copy(a, 0, me, sibling, src=x_refs[a]))
            sends += [copy(a, 1 + j, me, (*chip, c), src=x_refs[a]) for j, chip in enumerate(chips)]
        for cp in sends:
            cp.start()
        for j, chip in enumerate(chips):
            for a in range(n):
                copy(a, 1 + j, (*chip, c), me).wait_recv()
                passed = copy(a, 4 + j, (*chip, c), sibling)
                passed.start()
                sends.append(passed)
        for a in range(n):
            copy(a, 0, sibling, me).wait_recv()
            for j, chip in enumerate(chips):
                copy(a, 4 + j, (*chip, 1 - c), me).wait_recv()
        for cp in sends:
            cp.wait_send()
        for cp in local:
            cp.wait()

    out_shape = [jax.ShapeDtypeStruct((N_DEV,) + x.shape, x.dtype) for x in xs]
    return _comm_call(body, n, out_shape, xs, name)


def _exchange(parts, name):
    n = len(parts)

    def body(*refs):
        p_refs, out_refs = refs[:n], refs[n:2 * n]
        send_sems, recv_sems, local_sems = refs[2 * n:]
        me = _lin(_me())
        local = [pltpu.make_async_copy(p_refs[a].at[me], out_refs[a].at[me], local_sems.at[a]) for a in range(n)]
        for cp in local:
            cp.start()
        sends = []
        for k in range(1, N_DEV):
            peer = _peer(k)
            for a in range(n):
                cp = pltpu.make_async_remote_copy(
                    src_ref=p_refs[a].at[_lin(peer)], dst_ref=out_refs[a].at[me], send_sem=send_sems.at[a, k - 1],
                    recv_sem=recv_sems.at[a, k - 1], device_id=peer, device_id_type=pl.DeviceIdType.MESH)
                cp.start()
                sends.append(cp)
        for k in range(1, N_DEV):
            peer = _peer(k)
            for a in range(n):
                pltpu.make_async_remote_copy(
                    src_ref=p_refs[a].at[_lin(peer)], dst_ref=out_refs[a].at[_lin(peer)],
                    send_sem=send_sems.at[a, k - 1], recv_sem=recv_sems.at[a, k - 1], device_id=peer,
                    device_id_type=pl.DeviceIdType.MESH).wait_recv()
        for cp in sends:
            cp.wait_send()
        for cp in local:
            cp.wait()

    out_shape = [jax.ShapeDtypeStruct(p.shape, p.dtype) for p in parts]
    return _comm_call(body, n, out_shape, parts, name)


_HBM = pl.BlockSpec(memory_space=pltpu.HBM)
_SEM = pl.BlockSpec(memory_space=pltpu.SEMAPHORE)
_EFFECT = pltpu.SideEffectType.DATAFLOW_SIDE_EFFECTING


def _split_start(srcs, slotted, name):
    n = len(srcs)
    lands = [lax.empty(s.shape if slotted else (N_DEV,) + s.shape, s.dtype) for s in srcs]

    def body(*refs):
        src_refs, land_refs = refs[:n], refs[n:2 * n]
        send_sems, recv_sems, token = refs[2 * n], refs[2 * n + 1], refs[-1]
        me = _lin(_me())
        for k in range(1, N_DEV):
            peer = _peer(k)
            for a in range(n):
                pltpu.make_async_remote_copy(
                    src_ref=src_refs[a].at[_lin(peer)] if slotted else src_refs[a], dst_ref=land_refs[a].at[me],
                    send_sem=send_sems.at[a * (N_DEV - 1) + k - 1], recv_sem=recv_sems.at[a * (N_DEV - 1) + k - 1],
                    device_id=peer, device_id_type=pl.DeviceIdType.MESH).start()
        token[...] = jnp.zeros_like(token)

    both = list(srcs) + lands
    outs = pl.pallas_call(
        body, name=name,
        out_shape=(pltpu.SemaphoreType.DMA((n * (N_DEV - 1),)), pltpu.SemaphoreType.DMA((n * (N_DEV - 1),)),
                   *[pltpu.HBM(b.shape, b.dtype) for b in both], jax.ShapeDtypeStruct((8, LANES), F32)),
        in_specs=[_HBM] * (2 * n), out_specs=(_SEM, _SEM, *[_HBM] * (2 * n), pl.BlockSpec(memory_space=pltpu.VMEM)),
        input_output_aliases={a: 2 + a for a in range(2 * n)},
        compiler_params=pltpu.CompilerParams(has_side_effects=_EFFECT),
    )(*[pltpu.with_memory_space_constraint(b, pltpu.HBM) for b in both])
    return outs[0], outs[1], list(outs[2:2 + n]), list(outs[2 + n:2 + 2 * n]), outs[-1]


def _split_wait(send_sems, recv_sems, srcs, lands, after, slotted, name):
    n = len(srcs)

    def body(*refs):
        src_refs, land_refs = refs[:n], refs[n:2 * n]
        send_sems, recv_sems = refs[2 * n], refs[2 * n + 1]
        for k in range(1, N_DEV):
            peer = _peer(k)
            for a in range(n):
                cp = pltpu.make_async_remote_copy(
                    src_ref=src_refs[a].at[_lin(peer)] if slotted else src_refs[a],
                    dst_ref=land_refs[a].at[_lin(peer)], send_sem=send_sems.at[a * (N_DEV - 1) + k - 1],
                    recv_sem=recv_sems.at[a * (N_DEV - 1) + k - 1], device_id=peer,
                    device_id_type=pl.DeviceIdType.MESH)
                cp.wait_send()
                cp.wait_recv()

    both = list(srcs) + list(lands)
    outs = pl.pallas_call(
        body, name=name, out_shape=[pltpu.HBM(b.shape, b.dtype) for b in both],
        in_specs=[_HBM] * (2 * n) + [_SEM, _SEM, pl.BlockSpec(memory_space=pl.ANY)], out_specs=[_HBM] * (2 * n),
        input_output_aliases={a: a for a in range(2 * n)},
        compiler_params=pltpu.CompilerParams(has_side_effects=_EFFECT),
    )(*both, send_sems, recv_sems, after)
    return list(outs[n:])


def _own_slot(lands, own):
    me = _lin(_me())
    slot = lambda l: lax.broadcasted_iota(jnp.int32, (N_DEV,) + (1,) * (l.ndim - 1), 0)
    return [jnp.where(slot(l) == me, o.astype(l.dtype), l) for l, o in zip(lands, own)]


def _sum_adamw(recv, w, m, v):
    _, rows, lanes = recv.shape
    tr = _pick(rows, (256, 128, 64, 32, 16, 8))
    c1 = 1.0 - ADAM_B1 ** ADAM_STEP
    c2 = 1.0 - ADAM_B2 ** ADAM_STEP

    def body(r_ref, w_ref, m_ref, v_ref, g_out, d_out, m_out, v_out):
        g = r_ref[0].astype(F32)
        for s in range(1, N_DEV):
            g = g + r_ref[s].astype(F32)
        mn = ADAM_B1 * m_ref[...] + (1.0 - ADAM_B1) * g
        vn = ADAM_B2 * v_ref[...] + (1.0 - ADAM_B2) * (g * g)
        m_hat = mn / c1
        v_hat = vn / c2
        g_out[...] = g
        d_out[...] = -ADAM_LR * (m_hat / (jnp.sqrt(v_hat) + ADAM_EPS) + ADAM_WD * w_ref[...])
        m_out[...] = mn
        v_out[...] = vn

    blk = pl.BlockSpec((tr, lanes), lambda i: (i, 0))
    shp = jax.ShapeDtypeStruct((rows, lanes), F32)
    return pl.pallas_call(
        body, name=f"sum_adamw_{rows}x{lanes}", grid=(rows // tr,),
        in_specs=[pl.BlockSpec((N_DEV, tr, lanes), lambda i: (0, i, 0)), blk, blk, blk],
        out_specs=[blk, blk, blk, blk], out_shape=[shp, shp, shp, shp],
        compiler_params=_params(dimension_semantics=("parallel",)),
    )(recv, w, m, v)


def _rope_block(xb, cv, s1v, s2v, hs):
    return xb * cv + pltpu.roll(xb, LANES - hs, 1) * s1v + pltpu.roll(xb, hs, 1) * s2v


def _mm_nn(a, b, b_transposed=False, rope=None):
    m, k = a.shape
    n = b.shape[0] if b_transposed else b.shape[1]
    tm = _pick(m, (1024, 512, 256, 128))
    tn = _pick(n, (1024, 640, 512, 256, 128))
    tk = _pick(k, (1024, 640, 512, 256, 128))
    nk = k // tk

    def body(*refs):
        a_ref, b_ref = refs[:2]
        o_ref, acc_ref = refs[-2:]
        kk = pl.program_id(2)

        @pl.when(kk == 0)
        def _():
            acc_ref[...] = jnp.zeros_like(acc_ref)

        dims = (((1,), (1,)), ((), ())) if b_transposed else (((1,), (0,)), ((), ()))
        acc_ref[...] += lax.dot_general(a_ref[...].astype(BF16), b_ref[...].astype(BF16), dims,
                                        preferred_element_type=F32)

        @pl.when(kk == nk - 1)
        def _():
            if rope is None:
                o_ref[...] = acc_ref[...]
            else:
                cv, s1v, s2v = refs[2][...], refs[3][...], refs[4][...]
                for cb in range(tn // LANES):
                    sl = slice(cb * LANES, (cb + 1) * LANES)
                    o_ref[:, sl] = _rope_block(acc_ref[:, sl], cv, s1v, s2v, rope[3])

    b_spec = (pl.BlockSpec((tn, tk), lambda i, j, kk: (j, kk)) if b_transposed
              else pl.BlockSpec((tk, tn), lambda i, j, kk: (kk, j)))
    in_specs, args = [pl.BlockSpec((tm, tk), lambda i, j, kk: (i, kk)), b_spec], [a, b]
    if rope is not None:
        in_specs += [pl.BlockSpec((tm, LANES), lambda i, j, kk: (i, 0))] * 3
        args += list(rope[:3])
    tag = ("t" if b_transposed else "n") + ("" if rope is None else f"_rope{rope[3]}")
    return pl.pallas_call(
        body, name=f"mm_n{tag}_{m}x{k}x{n}", grid=(m // tm, n // tn, nk), in_specs=in_specs,
        out_specs=pl.BlockSpec((tm, tn), lambda i, j, kk: (i, j)),
        out_shape=jax.ShapeDtypeStruct((m, n), F32),
        scratch_shapes=[pltpu.VMEM((tm, tn), F32)],
        compiler_params=_params(dimension_semantics=("parallel", "parallel", "arbitrary")),
    )(*args)


def _mm_tn(a, g):
    s, k = a.shape
    _, n = g.shape
    tm = _pick(k, (1024, 512, 256, 128))
    tn = _pick(n, (1024, 640, 512, 256, 128))
    ts = _pick(s, (512, 256, 128))
    ns = s // ts

    def body(a_ref, g_ref, o_ref, acc_ref):
        ss = pl.program_id(2)

        @pl.when(ss == 0)
        def _():
            acc_ref[...] = jnp.zeros_like(acc_ref)

        acc_ref[...] += lax.dot_general(a_ref[...].astype(BF16), g_ref[...].astype(BF16),
                                        (((0,), (0,)), ((), ())), preferred_element_type=F32)

        @pl.when(ss == ns - 1)
        def _():
            o_ref[...] = acc_ref[...]

    return pl.pallas_call(
        body, name=f"mm_tn_{s}x{k}x{n}", grid=(k // tm, n // tn, ns),
        in_specs=[pl.BlockSpec((ts, tm), lambda i, j, ss: (ss, i)), pl.BlockSpec((ts, tn), lambda i, j, ss: (ss, j))],
        out_specs=pl.BlockSpec((tm, tn), lambda i, j, ss: (i, j)),
        out_shape=jax.ShapeDtypeStruct((k, n), F32),
        scratch_shapes=[pltpu.VMEM((tm, tn), F32)],
        compiler_params=_params(dimension_semantics=("parallel", "parallel", "arbitrary")),
    )(a, g)


@jax.custom_vjp
def mm(a, w):
    return _mm_nn(a, w.astype(BF16))


def _mm_fwd(a, w):
    wb = w.astype(BF16)
    return _mm_nn(a, wb), (a, wb)


def _mm_bwd(res, g):
    a, wb = res
    return _mm_nn(g, wb, b_transposed=True), _mm_tn(a, g)


mm.defvjp(_mm_fwd, _mm_bwd)


def _unrope(g, tabs, hs):
    return _rope_call(g, tabs[0], -tabs[1], -tabs[2], hs)


@functools.partial(jax.custom_vjp, nondiff_argnums=(3,))
def mm_rope(a, w, tabs, hs):
    return _mm_nn(a, w.astype(BF16), rope=(*tabs, hs))


def _mm_rope_fwd(a, w, tabs, hs):
    wb = w.astype(BF16)
    return _mm_nn(a, wb, rope=(*tabs, hs)), (a, wb, tabs)


def _mm_rope_bwd(hs, res, g):
    a, wb, tabs = res
    g = _unrope(g, tabs, hs)
    return _mm_nn(g, wb, b_transposed=True), _mm_tn(a, g), jax.tree.map(jnp.zeros_like, tabs)


mm_rope.defvjp(_mm_rope_fwd, _mm_rope_bwd)


def _proj_dx_call(gs, wb, cuts):
    s, (d, n), ng = gs[0].shape[0], wb.shape, len(gs)
    tm = _pick(s, (512, 256, 128))

    def body(*refs):
        w_ref, o_ref = refs[ng], refs[ng + 1]
        for gi in range(ng):
            part = lax.dot_general(refs[gi][...].astype(BF16), w_ref[:, cuts[gi]:cuts[gi + 1]],
                                   (((1,), (1,)), ((), ())), preferred_element_type=F32)
            if gi == 0:
                o_ref[...] = part
            else:
                o_ref[...] += part

    return pl.pallas_call(
        body, name=f"proj_dx_{ng}", grid=(s // tm,),
        in_specs=[pl.BlockSpec((tm, g.shape[1]), lambda i: (i, 0)) for g in gs] + [pl.BlockSpec((d, n), lambda i: (0, 0))],
        out_specs=pl.BlockSpec((tm, d), lambda i: (i, 0)), out_shape=jax.ShapeDtypeStruct((s, d), F32),
        compiler_params=_params(dimension_semantics=("parallel",)),
    )(*gs, wb)


def _proj_dw_call(xb, gs, cuts):
    (s, d), ng, n = xb.shape, len(gs), cuts[-1]
    ts = _pick(s, (512, 256, 128))
    ns = s // ts

    def body(*refs):
        x_ref, o_ref = refs[0], refs[ng + 1]
        ss = pl.program_id(0)
        xt = x_ref[...].T
        for gi in range(ng):
            cols = slice(cuts[gi], cuts[gi + 1])
            part = jnp.dot(xt, refs[1 + gi][...].astype(BF16), preferred_element_type=F32)

            @pl.when(ss == 0)
            def _():
                o_ref[:, cols] = part

            @pl.when(ss > 0)
            def _():
                o_ref[:, cols] += part

    return pl.pallas_call(
        body, name=f"proj_dw_{ng}", grid=(ns,),
        in_specs=[pl.BlockSpec((ts, d), lambda ss: (ss, 0))] + [pl.BlockSpec((ts, g.shape[1]), lambda ss: (ss, 0)) for g in gs],
        out_specs=pl.BlockSpec((d, n), lambda ss: (0, 0)), out_shape=jax.ShapeDtypeStruct((d, n), F32),
        compiler_params=pltpu.CompilerParams(vmem_limit_bytes=VMEM_LIMIT_TALL, dimension_semantics=("arbitrary",)),
    )(xb, *gs)


def _make_in_proj(widths, roped, hs):
    cuts = [sum(widths[:i]) for i in range(len(widths) + 1)]

    @jax.custom_vjp
    def in_proj(x, w, tabs):
        return fwd(x, w, tabs)[0]

    def fwd(x, w, tabs):
        xb, wb = x.astype(BF16), w.astype(BF16)
        outs = tuple(_mm_nn(xb, wb[:, a:b], rope=(*tabs, hs) if gi in roped else None)
                     for gi, (a, b) in enumerate(zip(cuts[:-1], cuts[1:])))
        return outs, (xb, wb, tabs)

    def bwd(res, gs):
        xb, wb, tabs = res
        gs = [_unrope(gg, tabs, hs) if gi in roped else gg for gi, gg in enumerate(gs)]
        return _proj_dx_call(gs, wb, cuts), _proj_dw_call(xb, gs, cuts), jax.tree.map(jnp.zeros_like, tabs)

    in_proj.defvjp(fwd, bwd)
    return in_proj


EVEN_GROUPS = (256, 128, 128, 512, 512, 512, 128, 1024)
ODD_GROUPS = (1024, 256, 256, 1024)
even_in_proj = _make_in_proj(EVEN_GROUPS, roped=(2,), hs=MLA_ROPE // 2)
odd_in_proj = _make_in_proj(ODD_GROUPS, roped=(0, 1), hs=SWA_DIM // 2)

SHARD_PAD = 384


def _source_columns(kind):
    if kind == "even":
        src = [list(range(0, 384)), [-1] * 64, list(range(384, 416)), [-1] * 32, list(range(416, 1952)),
               list(range(1952, 1960)), [-1] * 120, list(range(1960, 2984))]
        return sum(src, []), 373
    q0, k0, v0, g0 = 0, 1024, 1152, 1280
    dup = lambda base: [base + 64 * g + c for g in range(SWA_KV_HEADS) for _ in range(2) for c in range(64)]
    return list(range(q0, k0)) + dup(k0) + dup(v0) + list(range(g0, 2304)), 288


def _selection(kind, transposed):
    src, shard = _source_columns(kind)
    cat = [s + (SHARD_PAD - shard) * (s // shard) if s >= 0 else -1 for s in src]
    cat_arr = jnp.asarray(cat, jnp.int32)
    if transposed:
        cols = lax.broadcasted_iota(jnp.int32, (len(src), N_DEV * SHARD_PAD), 1)
        return (cols == cat_arr[:, None]).astype(BF16), [(c, r) for c, r in enumerate(cat) if r >= 0]
    rows = lax.broadcasted_iota(jnp.int32, (N_DEV * SHARD_PAD, len(src)), 0)
    return (rows == cat_arr[None, :]).astype(BF16), [(r, c) for c, r in enumerate(cat) if r >= 0]


def _mm_banded(a, b, nonzeros, a_slots=False, out_slots=False):
    k, n = b.shape
    m = a.shape[1] if a_slots else a.shape[0]
    tm = _pick(m, (1024, 512, 256, 128))
    tn = SHARD_PAD if out_slots else _pick(n, (1024, 640, 512, 256, 128))
    tk = SHARD_PAD if a_slots else _pick(k, (1024, 640, 512, 256, 128))
    lo, hi = [k // tk] * (n // tn), [-1] * (n // tn)
    for r, c in nonzeros:
        lo[c // tn], hi[c // tn] = min(lo[c // tn], r // tk), max(hi[c // tn], r // tk)
    first = [l if h >= 0 else 0 for l, h in zip(lo, hi)]
    count = [h - l + 1 if h >= 0 else 0 for l, h in zip(lo, hi)]
    steps = max(count)

    def body(first_ref, count_ref, a_ref, b_ref, o_ref, acc_ref):
        j, kk = pl.program_id(1), pl.program_id(2)

        @pl.when(kk == 0)
        def _():
            acc_ref[...] = jnp.zeros_like(acc_ref)

        @pl.when(kk < count_ref[j])
        def _():
            acc_ref[...] += jnp.dot(a_ref[...].astype(BF16), b_ref[...].astype(BF16), preferred_element_type=F32)

        @pl.when(kk == steps - 1)
        def _():
            o_ref[...] = acc_ref[...].astype(o_ref.dtype)

    kblk = lambda j, kk, f, c: jnp.minimum(f[j] + kk, f[j] + jnp.maximum(c[j], 1) - 1)
    a_spec = (pl.BlockSpec((None, tm, tk), lambda i, j, kk, f, c: (kblk(j, kk, f, c), i, 0)) if a_slots
              else pl.BlockSpec((tm, tk), lambda i, j, kk, f, c: (i, kblk(j, kk, f, c))))
    o_spec = (pl.BlockSpec((None, tm, tn), lambda i, j, kk, f, c: (j, i, 0)) if out_slots
              else pl.BlockSpec((tm, tn), lambda i, j, kk, f, c: (i, j)))
    o_shape = jax.ShapeDtypeStruct((n // tn, m, tn), BF16) if out_slots else jax.ShapeDtypeStruct((m, n), F32)
    return pl.pallas_call(
        body, name=f"mm_banded_{m}x{k}x{n}",
        grid_spec=pltpu.PrefetchScalarGridSpec(
            num_scalar_prefetch=2, grid=(m // tm, n // tn, steps),
            in_specs=[a_spec, pl.BlockSpec((tk, tn), lambda i, j, kk, f, c: (kblk(j, kk, f, c), j))],
            out_specs=o_spec, scratch_shapes=[pltpu.VMEM((tm, tn), F32)]),
        out_shape=o_shape,
        compiler_params=_params(dimension_semantics=("parallel", "parallel", "arbitrary")),
    )(jnp.asarray(first, jnp.int32), jnp.asarray(count, jnp.int32), a, b)


@functools.partial(jax.custom_vjp, nondiff_argnums=(1,))
def relayout(wslots, kind):
    return _mm_banded(wslots, *_selection(kind, False), a_slots=True)


def _relayout_bwd(kind, _, g):
    return (_mm_banded(g, *_selection(kind, True), out_slots=True),)


relayout.defvjp(lambda wslots, kind: (_mm_banded(wslots, *_selection(kind, False), a_slots=True), None),
                _relayout_bwd)


def _row_block(s):
    return _pick(s, (512, 256, 128, 64, 32, 16, 8))


def _rms_fwd_call(x, g):
    s, k = x.shape
    tr = _row_block(s)

    def body(x_ref, g_ref, o_ref):
        xv = x_ref[...]
        r = lax.rsqrt(jnp.mean(xv * xv, axis=-1, keepdims=True) + RMS_EPS)
        o_ref[...] = xv * r * g_ref[...]

    return pl.pallas_call(
        body, name=f"rms_fwd_{k}", grid=(s // tr,),
        in_specs=[pl.BlockSpec((tr, k), lambda i: (i, 0)), pl.BlockSpec((1, k), lambda i: (0, 0))],
        out_specs=pl.BlockSpec((tr, k), lambda i: (i, 0)), out_shape=jax.ShapeDtypeStruct((s, k), F32),
        compiler_params=_params(dimension_semantics=("parallel",)),
    )(x, g.reshape(1, k))


def _rms_bwd_call(x, g, dy):
    s, k = x.shape
    tr = _row_block(s)

    def body(x_ref, g_ref, dy_ref, dx_ref, dg_ref):
        @pl.when(pl.program_id(0) == 0)
        def _():
            dg_ref[...] = jnp.zeros_like(dg_ref)

        xv = x_ref[...]
        r = lax.rsqrt(jnp.mean(xv * xv, axis=-1, keepdims=True) + RMS_EPS)
        xh = xv * r
        dyv = dy_ref[...]
        dg_ref[...] += jnp.sum(dyv * xh, axis=0, keepdims=True)
        dxh = dyv * g_ref[...]
        dx_ref[...] = r * (dxh - xh * jnp.mean(dxh * xh, axis=-1, keepdims=True))

    dx, dg = pl.pallas_call(
        body, name=f"rms_bwd_{k}", grid=(s // tr,),
        in_specs=[pl.BlockSpec((tr, k), lambda i: (i, 0)), pl.BlockSpec((1, k), lambda i: (0, 0)),
                  pl.BlockSpec((tr, k), lambda i: (i, 0))],
        out_specs=[pl.BlockSpec((tr, k), lambda i: (i, 0)), pl.BlockSpec((1, k), lambda i: (0, 0))],
        out_shape=[jax.ShapeDtypeStruct((s, k), F32), jax.ShapeDtypeStruct((1, k), F32)],
        compiler_params=_params(dimension_semantics=("arbitrary",)),
    )(x, g.reshape(1, k), dy)
    return dx, dg.reshape(k)


@jax.custom_vjp
def rms_norm(x, g):
    return _rms_fwd_call(x, g)


rms_norm.defvjp(lambda x, g: (_rms_fwd_call(x, g), (x, g)), lambda res, dy: _rms_bwd_call(res[0], res[1], dy))


def _ln_fwd_call(x, y, g, b):
    s, k = x.shape
    tr = _row_block(s)

    def body(x_ref, y_ref, g_ref, b_ref, o_ref):
        u = ALPHA * x_ref[...] + y_ref[...]
        mu = jnp.mean(u, axis=-1, keepdims=True)
        d = u - mu
        var = jnp.mean(d * d, axis=-1, keepdims=True)
        o_ref[...] = d * lax.rsqrt(var + LN_EPS) * g_ref[...] + b_ref[...]

    row = pl.BlockSpec((tr, k), lambda i: (i, 0))
    vec = pl.BlockSpec((1, k), lambda i: (0, 0))
    return pl.pallas_call(
        body, name="ln_fwd", grid=(s // tr,), in_specs=[row, row, vec, vec], out_specs=row,
        out_shape=jax.ShapeDtypeStruct((s, k), F32), compiler_params=_params(dimension_semantics=("parallel",)),
    )(x, y, g.reshape(1, k), b.reshape(1, k))


def _ln_bwd_call(x, y, g, do):
    s, k = x.shape
    tr = _row_block(s)

    def body(x_ref, y_ref, g_ref, do_ref, dx_ref, dy_ref, dg_ref, db_ref):
        @pl.when(pl.program_id(0) == 0)
        def _():
            dg_ref[...] = jnp.zeros_like(dg_ref)
            db_ref[...] = jnp.zeros_like(db_ref)

        u = ALPHA * x_ref[...] + y_ref[...]
        mu = jnp.mean(u, axis=-1, keepdims=True)
        d = u - mu
        r = lax.rsqrt(jnp.mean(d * d, axis=-1, keepdims=True) + LN_EPS)
        xh = d * r
        dov = do_ref[...]
        dg_ref[...] += jnp.sum(dov * xh, axis=0, keepdims=True)
        db_ref[...] += jnp.sum(dov, axis=0, keepdims=True)
        dxh = dov * g_ref[...]
        du = r * (dxh - jnp.mean(dxh, axis=-1, keepdims=True) - xh * jnp.mean(dxh * xh, axis=-1, keepdims=True))
        dy_ref[...] = du
        dx_ref[...] = ALPHA * du

    row = pl.BlockSpec((tr, k), lambda i: (i, 0))
    vec = pl.BlockSpec((1, k), lambda i: (0, 0))
    dx, dy, dg, db = pl.pallas_call(
        body, name="ln_bwd", grid=(s // tr,), in_specs=[row, row, vec, row], out_specs=[row, row, vec, vec],
        out_shape=[jax.ShapeDtypeStruct((s, k), F32), jax.ShapeDtypeStruct((s, k), F32),
                   jax.ShapeDtypeStruct((1, k), F32), jax.ShapeDtypeStruct((1, k), F32)],
        compiler_params=_params(dimension_semantics=("arbitrary",)),
    )(x, y, g.reshape(1, k), do)
    return dx, dy, dg.reshape(k), db.reshape(k)


@jax.custom_vjp
def ln_res(x, y, g, b):
    return _ln_fwd_call(x, y, g, b)


ln_res.defvjp(lambda x, y, g, b: (_ln_fwd_call(x, y, g, b), (x, y, g)),
              lambda res, do: _ln_bwd_call(res[0], res[1], res[2], do))


def _rope_call(x, c, s1, s2, hs):
    s, w = x.shape
    tr = _row_block(s)
    nb = w // LANES

    def body(x_ref, c_ref, s1_ref, s2_ref, o_ref):
        cv, s1v, s2v = c_ref[...], s1_ref[...], s2_ref[...]
        for cb in range(nb):
            xb = x_ref[:, cb * LANES:(cb + 1) * LANES]
            o_ref[:, cb * LANES:(cb + 1) * LANES] = (
                xb * cv + pltpu.roll(xb, LANES - hs, 1) * s1v + pltpu.roll(xb, hs, 1) * s2v)

    row = pl.BlockSpec((tr, w), lambda i: (i, 0))
    tab = pl.BlockSpec((tr, LANES), lambda i: (i, 0))
    return pl.pallas_call(
        body, name=f"rope_{w}_{hs}", grid=(s // tr,), in_specs=[row, tab, tab, tab], out_specs=row,
        out_shape=jax.ShapeDtypeStruct((s, w), F32), compiler_params=_params(dimension_semantics=("parallel",)),
    )(x, c, s1, s2)


def _rope_tables(s, layout):
    pos = jnp.arange(s, dtype=F32)[:, None]
    lane = jnp.arange(LANES)
    if layout == "mla":
        dim, hs = MLA_ROPE, MLA_ROPE // 2
        r = lane - MLA_NOPE
        active = (r >= 0) & (r < MLA_ROPE)
    else:
        dim, hs = SWA_DIM, SWA_DIM // 2
        r = lane % SWA_DIM
        active = jnp.ones_like(lane, dtype=bool)
    f = jnp.where(active, r % hs, 0)
    inv = ROPE_THETA ** (-(2.0 * f.astype(F32)) / dim)
    ang = pos * inv[None, :]
    cos, sin = jnp.cos(ang), jnp.sin(ang)
    first = (active & (r < hs))[None, :]
    second = (active & (r >= hs))[None, :]
    c = jnp.where(active[None, :], cos, 1.0)
    s1 = jnp.where(first, -sin, 0.0)
    s2 = jnp.where(second, sin, 0.0)
    return (c, s1, s2), hs


def _gated(o_refs, g_ref):
    gv = g_ref[...]
    o = o_refs[0][...] if len(o_refs) == 1 else jnp.concatenate([r[...] for r in o_refs], axis=1)
    return (o * (gv * jax.nn.sigmoid(gv))).astype(BF16)


def _gated_mm_call(o_parts, gate, wb):
    s, k = gate.shape
    n = wb.shape[1]
    tm = _pick(s, (1024, 512, 256, 128))
    tn = _pick(n, (1024, 512, 256, 128))
    widths = [o.shape[1] for o in o_parts]
    no = len(widths)

    def body(*refs):
        refs[-1][...] = jnp.dot(_gated(refs[:no], refs[no]), refs[no + 1][...], preferred_element_type=F32)

    specs = [pl.BlockSpec((tm, wd), lambda i, j: (i, 0)) for wd in widths]
    return pl.pallas_call(
        body, name=f"gated_mm_{no}", grid=(s // tm, n // tn),
        in_specs=specs + [pl.BlockSpec((tm, k), lambda i, j: (i, 0)), pl.BlockSpec((k, tn), lambda i, j: (0, j))],
        out_specs=pl.BlockSpec((tm, tn), lambda i, j: (i, j)), out_shape=jax.ShapeDtypeStruct((s, n), F32),
        compiler_params=_params(dimension_semantics=("parallel", "parallel")),
    )(*o_parts, gate, wb)


def _gated_mm_tn_call(o_parts, gate, g):
    s, k = gate.shape
    n = g.shape[1]
    ts = _pick(s, (512, 256, 128))
    ns = s // ts
    widths = [o.shape[1] for o in o_parts]
    no = len(widths)

    def body(*refs):
        g_ref, o_ref, acc_ref = refs[no + 1], refs[no + 2], refs[no + 3]
        ss = pl.program_id(0)

        @pl.when(ss == 0)
        def _():
            acc_ref[...] = jnp.zeros_like(acc_ref)

        acc_ref[...] += lax.dot_general(_gated(refs[:no], refs[no]), g_ref[...].astype(BF16),
                                        (((0,), (0,)), ((), ())), preferred_element_type=F32)

        @pl.when(ss == ns - 1)
        def _():
            o_ref[...] = acc_ref[...]

    specs = [pl.BlockSpec((ts, wd), lambda ss: (ss, 0)) for wd in widths]
    return pl.pallas_call(
        body, name=f"gated_mm_tn_{no}", grid=(ns,),
        in_specs=specs + [pl.BlockSpec((ts, k), lambda ss: (ss, 0)), pl.BlockSpec((ts, n), lambda ss: (ss, 0))],
        out_specs=pl.BlockSpec((k, n), lambda ss: (0, 0)), out_shape=jax.ShapeDtypeStruct((k, n), F32),
        scratch_shapes=[pltpu.VMEM((k, n), F32)], compiler_params=_params(dimension_semantics=("arbitrary",)),
    )(*o_parts, gate, g)


def _gate_bwd_call(o_parts, gate, dz):
    s, w = gate.shape
    tr = _row_block(s)
    widths = [o.shape[1] for o in o_parts]
    n = len(widths)

    def body(*refs):
        o_refs, g_ref, dz_ref = refs[:n], refs[n], refs[n + 1]
        do_refs, dg_ref = refs[n + 2:2 * n + 2], refs[2 * n + 2]
        off = 0
        for o_ref, do_ref, wd in zip(o_refs, do_refs, widths):
            gv = g_ref[:, off:off + wd]
            sg = jax.nn.sigmoid(gv)
            dzv = dz_ref[:, off:off + wd]
            do_ref[...] = dzv * (gv * sg)
            dg_ref[:, off:off + wd] = dzv * o_ref[...] * (sg * (1.0 + gv * (1.0 - sg)))
            off += wd

    specs = [pl.BlockSpec((tr, wd), lambda i: (i, 0)) for wd in widths]
    row = pl.BlockSpec((tr, w), lambda i: (i, 0))
    outs = pl.pallas_call(
        body, name=f"gate_bwd_{n}", grid=(s // tr,), in_specs=specs + [row, row], out_specs=specs + [row],
        out_shape=[jax.ShapeDtypeStruct((s, wd), F32) for wd in widths] + [jax.ShapeDtypeStruct((s, w), F32)],
        compiler_params=_params(dimension_semantics=("parallel",)),
    )(*o_parts, gate, dz)
    return tuple(outs[:n]), outs[n]


@jax.custom_vjp
def gated_mm(o_parts, gate, w):
    return _gated_mm_call(o_parts, gate, w.astype(BF16))


def _gated_mm_fwd(o_parts, gate, w):
    wb = w.astype(BF16)
    return _gated_mm_call(o_parts, gate, wb), (o_parts, gate, wb)


def _gated_mm_bwd(res, g):
    o_parts, gate, wb = res
    do_parts, dgate = _gate_bwd_call(o_parts, gate, _mm_nn(g, wb, b_transposed=True))
    return do_parts, dgate, _gated_mm_tn_call(o_parts, gate, g)


gated_mm.defvjp(_gated_mm_fwd, _gated_mm_bwd)


def _loss_call(y, t):
    s, k = y.shape
    tr = _row_block(s)
    nsteps = s // tr

    def body(y_ref, t_ref, l_ref, dy_ref, acc_ref):
        i = pl.program_id(0)

        @pl.when(i == 0)
        def _():
            acc_ref[...] = jnp.zeros_like(acc_ref)

        d = y_ref[...] - t_ref[...]
        dy_ref[...] = d / k
        acc_ref[...] += jnp.sum(d * d, axis=0, keepdims=True)

        @pl.when(i == nsteps - 1)
        def _():
            tot = jnp.sum(acc_ref[...], axis=1, keepdims=True) * (0.5 / k)
            l_ref[...] = jnp.broadcast_to(tot, l_ref.shape)

    row = pl.BlockSpec((tr, k), lambda i: (i, 0))
    return pl.pallas_call(
        body, name="loss", grid=(nsteps,), in_specs=[row, row],
        out_specs=[pl.BlockSpec((1, LANES), lambda i: (0, 0)), row],
        out_shape=[jax.ShapeDtypeStruct((1, LANES), F32), jax.ShapeDtypeStruct((s, k), F32)],
        scratch_shapes=[pltpu.VMEM((1, k), F32)], compiler_params=_params(dimension_semantics=("arbitrary",)),
    )(y, t)


@jax.custom_vjp
def mse_loss(y, t):
    return _loss_call(y, t)[0][0, 0]


def _mse_fwd(y, t):
    l, dy = _loss_call(y, t)
    return l[0, 0], (dy, t)


mse_loss.defvjp(_mse_fwd, lambda res, g: (g * res[0], jnp.zeros_like(res[1])))


def _scan_call(x, b, mode):
    s, w = x.shape
    nt = s // 8

    def tile_scan(t):
        row = lax.broadcasted_iota(jnp.int32, (8, w), 0)
        for sh in (1, 2, 4):
            t = t + jnp.where(row >= sh, pltpu.roll(t, sh, 0), 0.0)
        return t

    def body(x_ref, b_ref, o_ref):
        def step(i, carry):
            rows = pl.ds(pl.multiple_of(i * 8, 8), 8)
            t = x_ref[rows, :]
            if mode == "fwd":
                t = jax.nn.log_sigmoid(t + b_ref[...])
            t = tile_scan(t) + carry
            o_ref[rows, :] = t
            return t[7:8, :]

        total = lax.fori_loop(0, nt, step, jnp.zeros((1, w), F32))
        if mode == "rev":
            def fix(i, c):
                rows = pl.ds(pl.multiple_of(i * 8, 8), 8)
                o_ref[rows, :] = total - o_ref[rows, :] + x_ref[rows, :]
                return c
            lax.fori_loop(0, nt, fix, 0)

    full = pl.BlockSpec((s, w), lambda: (0, 0))
    return pl.pallas_call(
        body, name=f"scan_{mode}", in_specs=[full, pl.BlockSpec((1, w), lambda: (0, 0))], out_specs=full,
        out_shape=jax.ShapeDtypeStruct((s, w), F32), compiler_params=_params(),
    )(x, b)


def _fox_dlogit_call(x, b, dlogf):
    s, w = x.shape
    tr = _row_block(s)

    def body(x_ref, b_ref, d_ref, dx_ref, db_ref):
        @pl.when(pl.program_id(0) == 0)
        def _():
            db_ref[...] = jnp.zeros_like(db_ref)

        dx = d_ref[...] * jax.nn.sigmoid(-(x_ref[...] + b_ref[...]))
        dx_ref[...] = dx
        db_ref[...] += jnp.sum(dx, axis=0, keepdims=True)

    row = pl.BlockSpec((tr, w), lambda i: (i, 0))
    vec = pl.BlockSpec((1, w), lambda i: (0, 0))
    return pl.pallas_call(
        body, name="fox_dlogit", grid=(s // tr,), in_specs=[row, vec, row], out_specs=[row, vec],
        out_shape=[jax.ShapeDtypeStruct((s, w), F32), jax.ShapeDtypeStruct((1, w), F32)],
        compiler_params=_params(dimension_semantics=("arbitrary",)),
    )(x, b, dlogf)


@jax.custom_vjp
def fox_cum(fl, b):
    return _scan_call(fl, b, "fwd")


def _fox_cum_bwd(res, dcum):
    fl, b = res
    dlogf = _scan_call(dcum, b, "rev")
    return _fox_dlogit_call(fl, b, dlogf)


fox_cum.defvjp(lambda fl, b: (_scan_call(fl, b, "fwd"), (fl, b)), _fox_cum_bwd)


def _lane_col(x, lane_idx):
    lane = lax.broadcasted_iota(jnp.int32, (1, x.shape[1]), 1)
    return jnp.sum(jnp.where(lane == lane_idx, x, 0.0), axis=1, keepdims=True)


def _row_of(x, row_idx):
    row = lax.broadcasted_iota(jnp.int32, (x.shape[0], 1), 0)
    return jnp.sum(jnp.where(row == row_idx, x, 0.0), axis=0, keepdims=True)


def _attn_cfg(mode, s):
    if mode == "swa":
        blk = 256 if s >= 2048 else 128
        return dict(blk=blk, n_outer=SWA_KV_HEADS, pps=4, wide=False, scale=SWA_DIM ** -0.5)
    blk = 512 if s >= 2048 else 128
    if mode == "mla":
        return dict(blk=blk, n_outer=4, pps=1, wide=True, scale=(MLA_NOPE + MLA_ROPE) ** -0.5)
    return dict(blk=blk, n_outer=4, pps=1, wide=False, scale=FOX_DIM ** -0.5)


ROW_CHUNK = 32


def _unrolled(n, body, carry):
    for c in range(n):
        carry = body(c, carry)
    return carry


def _valid_rows(mode, i, jb, blk, r0, rc):
    qpos = i * blk + r0 + lax.broadcasted_iota(jnp.int32, (rc, blk), 0)
    kpos = jb * blk + lax.broadcasted_iota(jnp.int32, (rc, blk), 1)
    ok = kpos <= qpos
    if mode == "swa":
        ok = ok & (qpos - kpos < WINDOW)
    return ok


def _attn_fwd_call(mode, q, k, v, extra):
    s = q.shape[0]
    cfg = _attn_cfg(mode, s)
    blk, n_outer, pps, wide, scale = cfg["blk"], cfg["n_outer"], cfg["pps"], cfg["wide"], cfg["scale"]
    rc = ROW_CHUNK
    nq = s // blk
    swa, fox = mode == "swa", mode == "fox"
    qw = (2 * LANES if wide else LANES) * pps
    kw = 2 * LANES if wide else LANES
    ow = LANES * pps
    reps = blk // LANES

    def body(*refs):
        if not swa:
            it_ref, jt_ref = refs[:2]
            refs = refs[2:]
        q_ref, k_ref, v_ref = refs[:3]
        n_in = 3
        if fox:
            cum_ref, cumt_ref = refs[3:5]
            n_in = 5
        if swa:
            sink_ref = refs[3]
            n_in = 4
        o_ref, lse_ref, m_ref, l_ref, acc_ref, a_ref, s_all, p_all, c_all = refs[n_in:]
        p_id = pl.program_id(0)
        if swa:
            i, j = pl.program_id(1), pl.program_id(2)
            jb, run, first, last = i - 1 + j, (i - 1 + j) >= 0, j == 0, j == 1
        else:
            i, j = it_ref[pl.program_id(1)], jt_ref[pl.program_id(1)]
            jb, first, last = j, j == 0, j == i
        lane = lax.broadcasted_iota(jnp.int32, (1, LANES), 1)
        msk = [lane < HALF, lane >= HALF]

        @pl.when(first)
        def _():
            for hh in range(2 * pps):
                if swa:
                    m_ref[hh] = jnp.broadcast_to(sink_ref[hh:hh + 1, :], (blk, LANES))
                    l_ref[hh] = jnp.ones((blk, LANES), F32)
                else:
                    m_ref[hh] = jnp.full((blk, LANES), NEG, F32)
                    l_ref[hh] = jnp.zeros((blk, LANES), F32)
            acc_ref[...] = jnp.zeros_like(acc_ref)

        def process(masked):
            for pp in range(pps):
                vb = v_ref[...]
                pvs = []
                for h in range(2):
                    hh = 2 * pp + h
                    s_ref, p_ref, c_ref = s_all.at[hh], p_all.at[hh], c_all.at[hh]
                    if wide:
                        qh = q_ref[:, h * LANES:(h + 1) * LANES] * scale
                        kh = k_ref[:, h * LANES:(h + 1) * LANES]
                    else:
                        qh = jnp.where(msk[h], q_ref[:, pp * LANES:(pp + 1) * LANES], 0.0) * scale
                        kh = k_ref[...]
                    s_ref[...] = lax.dot_general(qh.astype(BF16), kh.astype(BF16), (((1,), (1,)), ((), ())),
                                                 preferred_element_type=F32)
                    if fox:
                        head = 2 * p_id + h
                        c_ref[...] = jnp.broadcast_to(_lane_col(cum_ref[...], head), (blk, LANES))
                        ck = _row_of(cumt_ref[...], head)

                    def chunk(c, carry, hh=hh, h=h):
                        r0 = c * rc
                        rows = pl.ds(r0, rc)
                        u = s_ref[rows, :]
                        if fox:
                            u = u - ck
                        if masked:
                            u = jnp.where(_valid_rows(mode, i, jb, blk, r0, rc), u, NEG)
                        m_prev, l_prev = m_ref[hh, rows, :], l_ref[hh, rows, :]
                        m_cur = jnp.max(u, axis=1, keepdims=True)
                        if fox:
                            m_cur = m_cur + c_ref[rows, :]
                        m_next = jnp.maximum(m_prev, m_cur)
                        shift = m_next - c_ref[rows, :] if fox else m_next
                        p = jnp.exp(u - jnp.tile(shift, (1, reps)))
                        alpha = jnp.exp(m_prev - m_next)
                        l_ref[hh, rows, :] = alpha * l_prev + jnp.sum(p, axis=1, keepdims=True)
                        m_ref[hh, rows, :] = m_next
                        a_ref[hh, rows, :] = alpha
                        p_ref[rows, :] = p.astype(BF16)
                        return carry

                    _unrolled(blk // rc, chunk, 0)
                    vh = jnp.where(msk[h], vb, 0.0).astype(BF16)
                    pvs.append(jnp.dot(p_ref[...], vh, preferred_element_type=F32))
                acc_ref[pp] = acc_ref[pp] * jnp.where(msk[0], a_ref[2 * pp], a_ref[2 * pp + 1]) + pvs[0] + pvs[1]

        if swa:
            pl.when(run)(lambda: process(True))
        else:
            pl.when(j < i)(lambda: process(False))
            pl.when(j == i)(lambda: process(True))

        @pl.when(last)
        def _():
            for pp in range(pps):
                l0, l1 = l_ref[2 * pp], l_ref[2 * pp + 1]
                o_ref[:, pp * LANES:(pp + 1) * LANES] = acc_ref[pp] / jnp.where(msk[0], l0, l1)
                lse_ref[:, pp * LANES:(pp + 1) * LANES] = jnp.where(
                    msk[0], m_ref[2 * pp] + jnp.log(l0), m_ref[2 * pp + 1] + jnp.log(l1))

    if swa:
        kv_map = lambda g, i, j: (jnp.maximum(i - 1 + j, 0), g)
        q_map = lambda g, i, j: (i, g)
        grid, tables, sem = (n_outer, nq, 2), [], ("parallel", "parallel", "arbitrary")
    else:
        tri = [(i, j) for i in range(nq) for j in range(i + 1)]
        tables = [jnp.asarray([t[0] for t in tri], jnp.int32), jnp.asarray([t[1] for t in tri], jnp.int32)]
        kv_map = lambda p, t, it, jt: (jt[t], p)
        q_map = lambda p, t, it, jt: (it[t], p)
        grid, sem = (n_outer, len(tri)), ("parallel", "arbitrary")
    in_specs = [pl.BlockSpec((blk, qw), q_map), pl.BlockSpec((blk, kw), kv_map), pl.BlockSpec((blk, LANES), kv_map)]
    args = [q, k, v]
    if fox:
        cum, cumt = extra
        in_specs += [pl.BlockSpec((blk, LANES), lambda p, t, it, jt: (it[t], 0)),
                     pl.BlockSpec((8, blk), lambda p, t, it, jt: (0, jt[t]))]
        args += [cum, cumt]
    if swa:
        in_specs += [pl.BlockSpec((8, LANES), lambda g, i, j: (g, 0))]
        args += [extra]
    n_pairs = n_outer * pps
    return pl.pallas_call(
        body, name=f"attn_fwd_{mode}",
        grid_spec=pltpu.PrefetchScalarGridSpec(
            num_scalar_prefetch=len(tables), grid=grid, in_specs=in_specs,
            out_specs=[pl.BlockSpec((blk, ow), q_map), pl.BlockSpec((blk, ow), q_map)],
            scratch_shapes=[pltpu.VMEM((2 * pps, blk, LANES), F32), pltpu.VMEM((2 * pps, blk, LANES), F32),
                            pltpu.VMEM((pps, blk, LANES), F32), pltpu.VMEM((2 * pps, blk, LANES), F32),
                            pltpu.VMEM((2 * pps, blk, blk), F32), pltpu.VMEM((2 * pps, blk, blk), BF16),
                            pltpu.VMEM((2 * pps, blk, LANES), F32)]),
        out_shape=[jax.ShapeDtypeStruct((s, n_pairs * LANES), F32), jax.ShapeDtypeStruct((s, n_pairs * LANES), F32)],
        compiler_params=_params(dimension_semantics=sem),
    )(*tables, *args)


def _attn_bwd_call(mode, q, k, v, extra, lse, o, do):
    s = q.shape[0]
    cfg = _attn_cfg(mode, s)
    blk, n_outer, pps, wide, scale = cfg["blk"], cfg["n_outer"], cfg["pps"], cfg["wide"], cfg["scale"]
    rc = ROW_CHUNK
    nq = s // blk
    swa, fox = mode == "swa", mode == "fox"
    qw = (2 * LANES if wide else LANES) * pps
    kw = 2 * LANES if wide else LANES
    ow = LANES * pps
    reps = blk // LANES

    assert not swa

    def body(*refs):
        jt_ref, it_ref = refs[:2]
        refs = refs[2:]
        q_ref, k_ref, v_ref, lse_ref, o_ref, do_ref = refs[:6]
        n_in = 6
        if fox:
            cum_ref, cumt_ref = refs[6:8]
            n_in = 8
        dq_ref, dk_ref, dv_ref = refs[n_in:n_in + 3]
        n_out = n_in + 3
        if fox:
            dck_ref, dcq_ref = refs[n_out:n_out + 2]
            n_out += 2
        dk_acc, dv_acc, s_all, dp_all, p_all, ds_all, e_all, d_all = refs[n_out:n_out + 8]
        if fox:
            dck_acc, rs_all = refs[n_out + 8:n_out + 10]
        p_id, t = pl.program_id(0), pl.program_id(1)
        j, ii = jt_ref[t], it_ref[t]
        i, first_i, last_i = ii, ii == j, ii == nq - 1
        lane = lax.broadcasted_iota(jnp.int32, (1, LANES), 1)
        msk = [lane < HALF, lane >= HALF]

        @pl.when(t == 0)
        def _():
            dq_ref[...] = jnp.zeros_like(dq_ref)
            if fox:
                dcq_ref[...] = jnp.zeros_like(dcq_ref)

        @pl.when(first_i)
        def _():
            dk_acc[...] = jnp.zeros_like(dk_acc)
            dv_acc[...] = jnp.zeros_like(dv_acc)
            if fox:
                dck_acc[...] = jnp.zeros_like(dck_acc)

        def process(masked):
            rows = pl.ds(pl.multiple_of(i * blk, blk), blk)
            vb = v_ref[...].astype(BF16)
            dv_parts, dk_parts = [], []
            for pp in range(pps):
                psl = slice(pp * LANES, (pp + 1) * LANES)
                lse_blk, do_blk = lse_ref[:, psl], do_ref[:, psl]
                doo = do_blk * o_ref[:, psl]
                dq_pair = []
                for h in range(2):
                    hh = 2 * pp + h
                    s_ref, dp_ref, p_ref, ds_ref = s_all.at[hh], dp_all.at[hh], p_all.at[hh], ds_all.at[hh]
                    e_ref, d_ref = e_all.at[hh], d_all.at[hh]
                    if fox:
                        rs_ref = rs_all.at[hh]
                    if wide:
                        hsl = slice(h * LANES, (h + 1) * LANES)
                        qh = (q_ref[:, hsl] * scale).astype(BF16)
                        kh = k_ref[:, hsl].astype(BF16)
                    else:
                        qh = (jnp.where(msk[h], q_ref[:, psl], 0.0) * scale).astype(BF16)
                        kh = k_ref[...].astype(BF16)
                    s_ref[...] = lax.dot_general(qh, kh, (((1,), (1,)), ((), ())), preferred_element_type=F32)
                    do_h = jnp.where(msk[h], do_blk, 0.0).astype(BF16)
                    dp_ref[...] = lax.dot_general(do_h, vb, (((1,), (1,)), ((), ())), preferred_element_type=F32)
                    lse_h = _lane_col(lse_blk, HALF * h)
                    d_h = jnp.sum(jnp.where(msk[h], doo, 0.0), axis=1, keepdims=True)
                    e_ref[...] = jnp.broadcast_to(lse_h, (blk, LANES))
                    d_ref[...] = jnp.broadcast_to(d_h, (blk, LANES))
                    if fox:
                        head = 2 * p_id + h
                        e_ref[...] = e_ref[...] - jnp.broadcast_to(_lane_col(cum_ref[...], head), (blk, LANES))
                        ck = _row_of(cumt_ref[...], head)

                    def chunk(c, colsum):
                        r0 = c * rc
                        cr = pl.ds(r0, rc)
                        u = s_ref[cr, :]
                        if fox:
                            u = u - ck
                        p = jnp.exp(u - jnp.tile(e_ref[cr, :], (1, reps)))
                        if masked:
                            p = jnp.where(_valid_rows(mode, i, j, blk, r0, rc), p, 0.0)
                        ds = p * (dp_ref[cr, :] - jnp.tile(d_ref[cr, :], (1, reps)))
                        p_ref[cr, :] = p.astype(BF16)
                        ds_ref[cr, :] = ds.astype(BF16)
                        if fox:
                            colsum = colsum + jnp.sum(ds, axis=0, keepdims=True)
                            rs_ref[cr, :] = jnp.broadcast_to(jnp.sum(ds, axis=1, keepdims=True), (rc, LANES))
                        return colsum

                    colsum = _unrolled(blk // rc, chunk, jnp.zeros((1, blk), F32))
                    dv_parts.append(lax.dot_general(p_ref[...], do_h, (((0,), (0,)), ((), ())),
                                                    preferred_element_type=F32))
                    if fox:
                        dck_acc[h:h + 1, :] += -colsum
                        dcq_ref[rows, :] += jnp.where(msk[h], rs_ref[...], 0.0)
                    dq_h = jnp.dot(ds_ref[...], kh, preferred_element_type=F32) * scale
                    dk_h = lax.dot_general(ds_ref[...], qh, (((0,), (0,)), ((), ())), preferred_element_type=F32)
                    if wide:
                        dq_ref[rows, hsl] += dq_h
                        dk_acc[:, hsl] += dk_h
                    else:
                        dq_pair.append(jnp.where(msk[h], dq_h, 0.0))
                        dk_parts.append(dk_h)
                if not wide:
                    dq_ref[rows, psl] += dq_pair[0] + dq_pair[1]
            dv_acc[...] += functools.reduce(lambda a, b: a + b, dv_parts)
            if not wide:
                dk_acc[...] += functools.reduce(lambda a, b: a + b, dk_parts)

        pl.when(ii > j)(lambda: process(False))
        pl.when(ii == j)(lambda: process(True))

        @pl.when(last_i)
        def _():
            dk_ref[...] = dk_acc[...]
            dv_ref[...] = dv_acc[...]
            if fox:
                dck_ref[0] = dck_acc[...]

    tri = [(j, i) for j in range(nq) for i in range(j, nq)]
    tables = [jnp.asarray([t[0] for t in tri], jnp.int32), jnp.asarray([t[1] for t in tri], jnp.int32)]
    q_map = lambda p, t, jt, it: (it[t], p)
    kv_map = lambda p, t, jt, it: (jt[t], p)
    in_specs = [pl.BlockSpec((blk, qw), q_map), pl.BlockSpec((blk, kw), kv_map), pl.BlockSpec((blk, LANES), kv_map),
                pl.BlockSpec((blk, ow), q_map), pl.BlockSpec((blk, ow), q_map), pl.BlockSpec((blk, ow), q_map)]
    args = [q, k, v, lse, o, do]
    n_pairs = n_outer * pps
    out_specs = [pl.BlockSpec((s, qw), lambda p, t, jt, it: (0, p)), pl.BlockSpec((blk, kw), kv_map),
                 pl.BlockSpec((blk, LANES), kv_map)]
    out_shape = [jax.ShapeDtypeStruct((s, q.shape[1]), F32), jax.ShapeDtypeStruct((s, k.shape[1]), F32),
                 jax.ShapeDtypeStruct((s, v.shape[1]), F32)]
    nh = 2 * pps
    scratch = [pltpu.VMEM((blk, kw), F32), pltpu.VMEM((blk, LANES), F32), pltpu.VMEM((nh, blk, blk), F32),
               pltpu.VMEM((nh, blk, blk), F32), pltpu.VMEM((nh, blk, blk), BF16), pltpu.VMEM((nh, blk, blk), BF16),
               pltpu.VMEM((nh, blk, LANES), F32), pltpu.VMEM((nh, blk, LANES), F32)]
    if fox:
        cum, cumt = extra
        in_specs += [pl.BlockSpec((blk, LANES), lambda p, t, jt, it: (it[t], 0)),
                     pl.BlockSpec((8, blk), lambda p, t, jt, it: (0, jt[t]))]
        args += [cum, cumt]
        out_specs += [pl.BlockSpec((1, 8, blk), lambda p, t, jt, it: (p, 0, jt[t])),
                      pl.BlockSpec((s, LANES), lambda p, t, jt, it: (0, p))]
        out_shape += [jax.ShapeDtypeStruct((n_pairs, 8, s), F32), jax.ShapeDtypeStruct((s, n_pairs * LANES), F32)]
        scratch += [pltpu.VMEM((8, blk), F32), pltpu.VMEM((nh, blk, LANES), F32)]
    return pl.pallas_call(
        body, name=f"attn_bwd_{mode}",
        grid_spec=pltpu.PrefetchScalarGridSpec(num_scalar_prefetch=2, grid=(n_outer, len(tri)), in_specs=in_specs,
                                               out_specs=out_specs, scratch_shapes=scratch),
        out_shape=out_shape, compiler_params=_params(dimension_semantics=("parallel", "arbitrary")),
    )(*tables, *args)


def _swa_masks(i, blk):
    r = lax.broadcasted_iota(jnp.int32, (blk, blk), 0)
    c = lax.broadcasted_iota(jnp.int32, (blk, blk), 1)
    return (c > r) & (i > 0), c <= r


def _nt(a, b):
    return lax.dot_general(a, b, (((1,), (1,)), ((), ())), preferred_element_type=F32)


def _tn(a, b):
    return lax.dot_general(a, b, (((0,), (0,)), ((), ())), preferred_element_type=F32)


def _swa_bwd_call(q, k, v, sink, lse, o, do):
    s = q.shape[0]
    blk, pps, scale = WINDOW, 4, SWA_DIM ** -0.5
    nq = s // blk

    def body(q_ref, kp_ref, ko_ref, vp_ref, vo_ref, sink_ref, lse_ref, o_ref, do_ref,
             dq_ref, dk_ref, dv_ref, dsink_ref, ck_ref, cv_ref):
        i = pl.program_id(1)
        lane = lax.broadcasted_iota(jnp.int32, (1, LANES), 1)
        msk = [lane < HALF, lane >= HALF]

        @pl.when(i == 0)
        def _():
            ck_ref[...] = jnp.zeros_like(ck_ref)
            cv_ref[...] = jnp.zeros_like(cv_ref)
            dsink_ref[...] = jnp.zeros_like(dsink_ref)

        @pl.when(i < nq)
        def _():
            ok_prev, ok_own = _swa_masks(i, blk)
            kp, ko = kp_ref[...].astype(BF16), ko_ref[...].astype(BF16)
            vp, vo = vp_ref[...].astype(BF16), vo_ref[...].astype(BF16)
            dkp, dko, dvp, dvo = [], [], [], []
            for pp in range(pps):
                psl = slice(pp * LANES, (pp + 1) * LANES)
                qp, do_blk = q_ref[:, psl], do_ref[:, psl]
                doo = do_blk * o_ref[:, psl]
                dqs = []
                for h in range(2):
                    hh = 2 * pp + h
                    qh = (jnp.where(msk[h], qp, 0.0) * scale).astype(BF16)
                    lse_h = jnp.broadcast_to(_lane_col(lse_ref[:, psl], HALF * h), (blk, LANES))
                    d_h = jnp.broadcast_to(jnp.sum(jnp.where(msk[h], doo, 0.0), axis=1, keepdims=True), (blk, LANES))
                    p_p = jnp.where(ok_prev, jnp.exp(_nt(qh, kp) - lse_h), 0.0)
                    p_o = jnp.where(ok_own, jnp.exp(_nt(qh, ko) - lse_h), 0.0)
                    do_h = jnp.where(msk[h], do_blk, 0.0).astype(BF16)
                    ds_p = (p_p * (_nt(do_h, vp) - d_h)).astype(BF16)
                    ds_o = (p_o * (_nt(do_h, vo) - d_h)).astype(BF16)
                    dq_h = (jnp.dot(ds_p, kp, preferred_element_type=F32)
                            + jnp.dot(ds_o, ko, preferred_element_type=F32)) * scale
                    dqs.append(jnp.where(msk[h], dq_h, 0.0))
                    dkp.append(_tn(ds_p, qh))
                    dko.append(_tn(ds_o, qh))
                    dvp.append(_tn(p_p.astype(BF16), do_h))
                    dvo.append(_tn(p_o.astype(BF16), do_h))
                    sink_row = sink_ref[hh:hh + 1, :]
                    dsink_ref[hh:hh + 1, :] += -jnp.sum(jnp.exp(sink_row - lse_h) * d_h, axis=0, keepdims=True)
                dq_ref[:, psl] = dqs[0] + dqs[1]
            total = lambda parts: functools.reduce(lambda a, b: a + b, parts)
            dk_ref[...] = ck_ref[...] + total(dkp)
            dv_ref[...] = cv_ref[...] + total(dvp)
            ck_ref[...] = total(dko)
            cv_ref[...] = total(dvo)

        @pl.when(i == nq)
        def _():
            dk_ref[...] = ck_ref[...]
            dv_ref[...] = cv_ref[...]

    last = nq - 1
    prev = lambda g, i: (jnp.maximum(i - 1, 0), g)
    own = lambda g, i: (jnp.minimum(i, last), g)
    qspec = pl.BlockSpec((blk, pps * LANES), own)
    kspec = lambda m: pl.BlockSpec((blk, LANES), m)
    sspec = pl.BlockSpec((8, LANES), lambda g, i: (g, 0))
    return pl.pallas_call(
        body, name="swa_bwd", grid=(SWA_KV_HEADS, nq + 1),
        in_specs=[qspec, kspec(prev), kspec(own), kspec(prev), kspec(own), sspec, qspec, qspec, qspec],
        out_specs=[qspec, kspec(prev), kspec(prev), sspec],
        out_shape=[jax.ShapeDtypeStruct(q.shape, F32), jax.ShapeDtypeStruct(k.shape, F32),
                   jax.ShapeDtypeStruct(v.shape, F32), jax.ShapeDtypeStruct((SWA_HEADS, LANES), F32)],
        scratch_shapes=[pltpu.VMEM((blk, LANES), F32), pltpu.VMEM((blk, LANES), F32)],
        compiler_params=_params(dimension_semantics=("parallel", "arbitrary")),
    )(q, k, k, v, v, sink, lse, o, do)


def _make_attn(mode):
    swa = mode == "swa"

    @jax.custom_vjp
    def attn(q, k, v, extra):
        return fwd(q, k, v, extra)[0]

    def fwd(q, k, v, extra):
        o, lse = _attn_fwd_call(mode, q, k, v, extra)
        return o, (q, k, v, extra, o, lse)

    def bwd(res, do):
        q, k, v, extra, o, lse = res
        outs = (_swa_bwd_call(q, k, v, extra, lse, o, do) if swa
                else _attn_bwd_call(mode, q, k, v, extra, lse, o, do))
        dq, dk, dv = outs[:3]
        if mode == "fox":
            cum, cumt = extra
            dck = outs[3]
            dcumt = dck[:, :2, :].reshape(FOX_HEADS, -1)
            dcq = outs[4].reshape(-1, FOX_HEADS, HALF)[:, :, 0]
            dextra = (jnp.pad(dcq, ((0, 0), (0, LANES - FOX_HEADS))), dcumt)
        elif mode == "swa":
            dextra = jnp.where(jnp.arange(LANES)[None, :] == 0, outs[3], 0.0)
        else:
            dextra = None
        return dq, dk, dv, dextra

    attn.defvjp(fwd, bwd)
    return attn


attn_mla = _make_attn("mla")
attn_fox = _make_attn("fox")
attn_swa = _make_attn("swa")


def _ukv_layout(w):
    r = w.shape[0]
    w3 = w.reshape(r, MLA_HEADS, MLA_NOPE + MLA_V)
    wk = jnp.pad(w3[:, :, :MLA_NOPE], ((0, 0), (0, 0), (0, LANES - MLA_NOPE))).reshape(r, MLA_HEADS * LANES)
    wv = w3[:, :, MLA_NOPE:].reshape(r, MLA_HEADS * MLA_V)
    return wk, wv


def _even_layer(x, w_in_cat, q_norm, w_uq_p, kv_norm, w_ukv, b_f, w_out, ln_g, ln_b, tabs_mla):
    tabs, hs = tabs_mla
    cq, ckv, kpe, fq, fk, fv, fl, gate = even_in_proj(x, relayout(w_in_cat, "even"), tabs)
    q = mm_rope(rms_norm(cq, q_norm), w_uq_p, tabs, hs)
    ckvn = rms_norm(ckv, kv_norm)
    wk, wv = _ukv_layout(w_ukv)
    kk = mm(ckvn, wk) + jnp.tile(kpe, (1, MLA_HEADS))
    o_mla = attn_mla(q, kk, mm(ckvn, wv), None)
    cum = fox_cum(fl, jnp.pad(b_f, (0, LANES - FOX_HEADS)).reshape(1, LANES))
    o_fox = attn_fox(fq, fk, fv, (cum, cum[:, :8].T))
    y = gated_mm((o_mla, o_fox), gate, w_out)
    return ln_res(x, y, ln_g, ln_b)


def _odd_layer(x, w_in_cat, sinks, w_out, ln_g, ln_b, tabs_swa):
    q, kd, vd, gate = odd_in_proj(x, relayout(w_in_cat, "odd"), tabs_swa[0])
    o = attn_swa(q, kd, vd, jnp.broadcast_to(sinks[:, None], (SWA_HEADS, LANES)))
    y = gated_mm((o,), gate, w_out)
    return ln_res(x, y, ln_g, ln_b)


EVEN_SHARDED = ["even_w_in", "even_w_uq", "even_w_ukv", "even_w_out"]
ODD_SHARDED = ["odd_w_in", "odd_w_out", "odd_ln_g", "odd_ln_b"]
EVEN_REPL = ["even_q_norm", "even_kv_norm", "even_b_f", "even_ln_g", "even_ln_b"]
ODD_REPL = ["odd_sinks"]


def _layer_names(layer):
    return (EVEN_SHARDED, EVEN_REPL) if layer % 2 == 0 else (ODD_SHARDED, ODD_REPL)


def _layer_of(name, j):
    return 2 * j if name.startswith("even") else 2 * j + 1


def _layer_apply(layer, p, x, tabs):
    if layer % 2 == 0:
        return _even_layer(x, p["even_w_in"], p["even_q_norm"], p["even_w_uq"], p["even_kv_norm"], p["even_w_ukv"],
                           p["even_b_f"], p["even_w_out"], p["even_ln_g"], p["even_ln_b"], tabs["mla"])
    return _odd_layer(x, p["odd_w_in"], p["odd_sinks"], p["odd_w_out"], p["odd_ln_g"], p["odd_ln_b"], tabs["swa"])


def _pad_rows(flat, mult):
    n = flat.shape[-1]
    per = mult * LANES
    padded = -(-n // per) * per
    if padded != n:
        flat = jnp.pad(flat, [(0, 0)] * (flat.ndim - 1) + [(0, padded - n)])
    return flat.reshape(flat.shape[:-1] + (padded // LANES, LANES))


def _pad_last(a, width):
    if a.shape[-1] == width:
        return a
    return jnp.pad(a, [(0, 0)] * (a.ndim - 1) + [(0, width - a.shape[-1])])


def _join(slots, axis):
    shp = list(slots.shape[1:])
    shp[axis] *= N_DEV
    return jnp.moveaxis(slots, 0, axis).reshape(shp)


def _split(full, axis):
    shp = full.shape
    t = full.reshape(shp[:axis] + (N_DEV, shp[axis] // N_DEV) + shp[axis + 1:])
    return jnp.moveaxis(t, axis, 0)


PAD_TO = {"even_w_in": SHARD_PAD, "even_w_uq": LANES, "odd_w_in": SHARD_PAD}


def kernel(x, even_w_in, even_q_norm, even_w_uq, even_kv_norm, even_w_ukv, even_b_f, even_w_out, even_ln_g, even_ln_b, odd_w_in, odd_sinks, odd_w_out, odd_ln_g, odd_ln_b, loss_target, m_even_w_in, m_even_q_norm, m_even_w_uq, m_even_kv_norm, m_even_w_ukv, m_even_b_f, m_even_w_out, m_even_ln_g, m_even_ln_b, m_odd_w_in, m_odd_sinks, m_odd_w_out, m_odd_ln_g, m_odd_ln_b, v_even_w_in, v_even_q_norm, v_even_w_uq, v_even_kv_norm, v_even_w_ukv, v_even_b_f, v_even_w_out, v_even_ln_g, v_even_ln_b, v_odd_w_in, v_odd_sinks, v_odd_w_out, v_odd_ln_g, v_odd_ln_b):
    w = dict(even_w_in=even_w_in, even_q_norm=even_q_norm, even_w_uq=even_w_uq, even_kv_norm=even_kv_norm,
             even_w_ukv=even_w_ukv, even_b_f=even_b_f, even_w_out=even_w_out, even_ln_g=even_ln_g, even_ln_b=even_ln_b,
             odd_w_in=odd_w_in, odd_sinks=odd_sinks, odd_w_out=odd_w_out, odd_ln_g=odd_ln_g, odd_ln_b=odd_ln_b)
    mom = dict(even_w_in=m_even_w_in, even_q_norm=m_even_q_norm, even_w_uq=m_even_w_uq, even_kv_norm=m_even_kv_norm,
               even_w_ukv=m_even_w_ukv, even_b_f=m_even_b_f, even_w_out=m_even_w_out, even_ln_g=m_even_ln_g,
               even_ln_b=m_even_ln_b, odd_w_in=m_odd_w_in, odd_sinks=m_odd_sinks, odd_w_out=m_odd_w_out,
               odd_ln_g=m_odd_ln_g, odd_ln_b=m_odd_ln_b)
    vel = dict(even_w_in=v_even_w_in, even_q_norm=v_even_q_norm, even_w_uq=v_even_w_uq, even_kv_norm=v_even_kv_norm,
               even_w_ukv=v_even_w_ukv, even_b_f=v_even_b_f, even_w_out=v_even_w_out, even_ln_g=v_even_ln_g,
               even_ln_b=v_even_ln_b, odd_w_in=v_odd_w_in, odd_sinks=v_odd_sinks, odd_w_out=v_odd_w_out,
               odd_ln_g=v_odd_ln_g, odd_ln_b=v_odd_ln_b)
    sharded = BIG + SMALL_SHARDED
    padded = lambda d, n: _pad_last(d[n], PAD_TO.get(n, d[n].shape[-1]))

    tabs = {"mla": _rope_tables(x.shape[1], "mla"), "swa": _rope_tables(x.shape[1], "swa")}
    keys = lambda layers: [(n, layer // 2) for layer in layers for n in _layer_names(layer)[0]]
    first, rest = keys([0]), keys([1, 2, 3])
    me = _lin(_me())

    def shard(n, j):
        a = padded(w, n)[j]
        return a.astype(BF16) if n in BIG else a

    in_slots = ("even_w_in", "odd_w_in")
    to_full = lambda n, g: g if n in in_slots else _join(g, SHARD_AXIS[n] - 1).astype(F32)
    to_slots = lambda n, g: g if n in in_slots else _split(g, SHARD_AXIS[n] - 1).astype(BF16 if n in BIG else F32)

    def layer_params(layer, full_of):
        shn, rpn = _layer_names(layer)
        p = {n: full_of(n) for n in shn}
        p.update({n: w[n][layer // 2] for n in rpn})
        return p

    got0 = _all_gather([shard(n, j) for n, j in first], "all_gather_first")
    got0, mine_rest = lax.optimization_barrier((got0, [shard(n, j) for n, j in rest]))
    got0 = dict(zip(first, got0))
    send_sems, recv_sems, srcs, lands, token = _split_start(mine_rest, False, "all_gather_rest_start")
    tie = lambda a: a + token[0, 0].astype(a.dtype)
    p0 = layer_params(0, lambda n: tie(to_full(n, got0[(n, 0)])))
    x1, vjp0 = jax.vjp(lambda p, xx: _layer_apply(0, p, xx, tabs), p0, x[0])
    got = _split_wait(send_sems, recv_sems, srcs, lands, x1, False, "all_gather_rest_wait")
    got = dict(zip(rest, _own_slot(got, [m[None] for m in mine_rest])))

    xs, vjps = x1, [vjp0]
    for layer in (1, 2, 3):
        p = layer_params(layer, lambda n: to_full(n, got[(n, layer // 2)]))
        xs, vjp = jax.vjp(lambda p_, xx, layer=layer: _layer_apply(layer, p_, xx, tabs), p, xs)
        vjps.append(vjp)
    loss_local, vjp_loss = jax.vjp(lambda y: mse_loss(y, loss_target[0]), xs)
    (dy,) = vjp_loss(jnp.ones((), F32))
    loss = lax.psum(loss_local, AXES)
    grads = {}
    for layer in (3, 2, 1):
        grads[layer], dy = vjps[layer](dy)

    parts_rest = [to_slots(n, grads[_layer_of(n, j)][n]) for n, j in rest]
    send_sems, recv_sems, srcs, lands, token = _split_start(parts_rest, True, "grad_exchange_rest_start")
    grads[0], grad_x = vjps[0](dy + token[0, 0])
    recv_rest = _split_wait(send_sems, recv_sems, srcs, lands, grad_x, True, "grad_exchange_rest_wait")
    recv = dict(zip(rest, _own_slot(recv_rest, [lax.dynamic_slice_in_dim(p, me, 1, axis=0) for p in parts_rest])))
    repl_grad = lambda n: jnp.stack([grads[_layer_of(n, j)][n] for j in (0, 1)])
    repl_rows = _pad_rows(jnp.concatenate([repl_grad(n).reshape(-1) for n in REPL]), 8)
    parts_last = [to_slots(n, grads[0][n]) for n, j in first]
    parts_last.append(jnp.broadcast_to(repl_rows[None], (N_DEV,) + repl_rows.shape))
    recv_last = _exchange(parts_last, "grad_exchange_last")
    recv.update(zip(first, recv_last[:-1]))

    g_out, d_out, m_out, v_out = {}, {}, {}, {}
    for n in sharded:
        r = jnp.stack([recv[(n, 0)], recv[(n, 1)]], axis=1)
        cols = r.shape[-1]
        flat2 = lambda d: padded(d, n).reshape(-1, cols)
        outs = _sum_adamw(r.reshape(N_DEV, -1, cols), flat2(w), flat2(mom), flat2(vel))
        for dst, o in zip((g_out, d_out, m_out, v_out), outs):
            dst[n] = o.reshape(w[n].shape[:-1] + (cols,))[..., :w[n].shape[-1]]
    pack = lambda d: _pad_rows(jnp.concatenate([d[n].reshape(-1) for n in REPL]), 8)
    outs = _sum_adamw(recv_last[-1], pack(w), pack(mom), pack(vel))
    for dst, o in zip((g_out, d_out, m_out, v_out), outs):
        flat, off = o.reshape(-1), 0
        for n in REPL:
            size = math.prod(w[n].shape)
            dst[n] = flat[off:off + size].reshape(w[n].shape)
            off += size
    return (loss, grad_x[None], *[g_out[n] for n in WEIGHTS], *[d_out[n] for n in WEIGHTS],
            *[m_out[n] for n in WEIGHTS], *[v_out[n] for n in WEIGHTS])
```
